```python
import math
import jax, jax.numpy as jnp
from jax import lax
import numpy as np

D_MODEL = 1024
BATCH = 8
SEQ = 4096
DEPTH = 2

N_HEADS = 8
QK_NOPE_DIM = 64
QK_ROPE_DIM = 32
V_HEAD_DIM = 64
Q_LORA_RANK = 384
KV_LORA_RANK = 256
ROPE_THETA = 10000.0
Q_BLOCK = 128
CONV_CHANNELS = 512
CONV_WIDTH = 31
POOL_WINDOWS = (2, 4, 8, 16)
POOL_GROUPS = 4
POOL_CHANNELS = 512
POOL_GROUP_DIM = POOL_CHANNELS // POOL_GROUPS
N_BRANCHES = 3
D_FF = -(-8 * D_MODEL // (3 * 256)) * 256
EPS = 1e-6
IN_WIDTHS = (Q_LORA_RANK, KV_LORA_RANK, QK_ROPE_DIM, 2 * CONV_CHANNELS, POOL_CHANNELS, N_BRANCHES * D_MODEL)
D_IN = sum(IN_WIDTHS)

kernel_name = "hybrid_mla_conformer_pool_gated_block"


def rms_norm(x, g):
    x32 = x.astype(jnp.float32)
    y = x32 * lax.rsqrt(jnp.mean(x32 * x32, axis=-1, keepdims=True) + EPS)
    return (y * g.astype(jnp.float32)).astype(x.dtype)


def layer_norm(x, g, b):
    x32 = x.astype(jnp.float32)
    mu = jnp.mean(x32, axis=-1, keepdims=True)
    xc = x32 - mu
    y = xc * lax.rsqrt(jnp.mean(xc * xc, axis=-1, keepdims=True) + EPS)
    return (y * g.astype(jnp.float32) + b.astype(jnp.float32)).astype(x.dtype)


def rope_tables(positions):
    inv_freq = ROPE_THETA ** (-jnp.arange(0, QK_ROPE_DIM, 2, dtype=jnp.float32) / QK_ROPE_DIM)
    ang = positions.astype(jnp.float32)[..., None] * inv_freq
    return jnp.cos(ang), jnp.sin(ang)


def apply_rope(x, cos, sin):
    x32 = x.astype(jnp.float32)
    half = x32.shape[-1] // 2
    x1, x2 = x32[..., :half], x32[..., half:]
    out = jnp.concatenate([x1 * cos - x2 * sin, x2 * cos + x1 * sin], axis=-1)
    return out.astype(x.dtype)


def mla_branch(c_q, c_kv, k_rope_raw, cos, sin, q_norm, w_uq, kv_norm, w_uk, w_uv, w_o):
    B, S, _ = c_q.shape
    q = (rms_norm(c_q, q_norm) @ w_uq).reshape(B, S, N_HEADS, QK_NOPE_DIM + QK_ROPE_DIM)
    q_nope = q[..., :QK_NOPE_DIM]
    q_rope = apply_rope(q[..., QK_NOPE_DIM:], cos[:, :, None, :], sin[:, :, None, :])
    c_kv_n = rms_norm(c_kv, kv_norm)
    k_nope = (c_kv_n @ w_uk).reshape(B, S, N_HEADS, QK_NOPE_DIM)
    v = (c_kv_n @ w_uv).reshape(B, S, N_HEADS, V_HEAD_DIM)
    k_rope = apply_rope(k_rope_raw, cos, sin)
    nb = S // Q_BLOCK
    qn_blocks = q_nope.reshape(B, nb, Q_BLOCK, N_HEADS, QK_NOPE_DIM).transpose(1, 0, 2, 3, 4)
    qr_blocks = q_rope.reshape(B, nb, Q_BLOCK, N_HEADS, QK_ROPE_DIM).transpose(1, 0, 2, 3, 4)
    starts = jnp.arange(nb, dtype=jnp.int32) * Q_BLOCK
    key_idx = jnp.arange(S, dtype=jnp.int32)
    scale = 1.0 / math.sqrt(QK_NOPE_DIM + QK_ROPE_DIM)

    def attend(args):
        qn, qr, start = args
        s = jnp.einsum('bqhd,bkhd->bhqk', qn, k_nope) + jnp.einsum('bqhr,bkr->bhqk', qr, k_rope)
        s = s.astype(jnp.float32) * scale
        q_idx = start + jnp.arange(Q_BLOCK, dtype=jnp.int32)
        causal = key_idx[None, :] <= q_idx[:, None]
        s = jnp.where(causal[None, None], s, -jnp.inf)
        p = jax.nn.softmax(s, axis=-1).astype(v.dtype)
        return jnp.einsum('bhqk,bkhd->bqhd', p, v)

    o = lax.map(attend, (qn_blocks, qr_blocks, starts))
    o = o.transpose(1, 0, 2, 3, 4).reshape(B, S, N_HEADS * V_HEAD_DIM)
    return o @ w_o


def conv_branch(u, conv_w, conv_b, ln_g, ln_b, w_out):
    a, g = jnp.split(u, 2, axis=-1)
    h = a * jax.nn.sigmoid(g)
    h = lax.conv_general_dilated(h, conv_w[:, None, :], window_strides=(1,),
                                 padding=[(CONV_WIDTH - 1, 0)],
                                 dimension_numbers=('NWC', 'WIO', 'NWC'),
                                 feature_group_count=CONV_CHANNELS) + conv_b
    h = jax.nn.silu(layer_norm(h, ln_g, ln_b))
    return h @ w_out


def pool_branch(u, pool_w, pool_scale, w_out):
    B, S, _ = u.shape
    u32 = u.astype(jnp.float32).reshape(B, S, POOL_GROUPS, POOL_GROUP_DIM)
    cs = jnp.cumsum(u32, axis=1)
    t = jnp.arange(S, dtype=jnp.int32)
    outs = []
    for gi, w in enumerate(POOL_WINDOWS):
        c = cs[:, :, gi]
        lag = jnp.pad(c, ((0, 0), (w, 0), (0, 0)))[:, :S]
        count = jnp.minimum(t + 1, w).astype(jnp.float32)[None, :, None]
        outs.append((c - lag) / count)
    pooled = jnp.stack(outs, axis=2)
    d = (pooled - u32).astype(u.dtype)
    m = jnp.einsum('bsgc,gcd->bsgd', d, pool_w).reshape(B, S, POOL_CHANNELS) * pool_scale
    return m @ w_out


def mixer_sublayer(x, cos, sin, norm_pre, w_in, q_norm, w_uq, kv_norm, w_uk, w_uv, w_attn_o,
                   conv_w, conv_b, conv_ln_g, conv_ln_b, w_conv_o, pool_w, pool_scale, w_pool_o,
                   w_mix_o, norm_post):
    B, S, D = x.shape
    h = rms_norm(x, norm_pre)
    z = h @ w_in
    split_at = [int(v) for v in np.cumsum(IN_WIDTHS)[:-1]]
    c_q, c_kv, k_r, u_conv, u_pool, gate_logits = jnp.split(z, split_at, axis=-1)
    y_attn = mla_branch(c_q, c_kv, k_r, cos, sin, q_norm, w_uq, kv_norm, w_uk, w_uv, w_attn_o)
    y_conv = conv_branch(u_conv, conv_w, conv_b, conv_ln_g, conv_ln_b, w_conv_o)
    y_pool = pool_branch(u_pool, pool_w, pool_scale, w_pool_o)
    gates = jax.nn.sigmoid(gate_logits.astype(jnp.float32)).astype(x.dtype).reshape(B, S, N_BRANCHES, D)
    merged = gates[:, :, 0] * y_attn + gates[:, :, 1] * y_conv + gates[:, :, 2] * y_pool
    return rms_norm(merged @ w_mix_o, norm_post)


def ffn_sublayer(x, norm_pre, w_gate, w_up, w_down, norm_post):
    h = rms_norm(x, norm_pre)
    y = (jax.nn.silu(h @ w_gate) * (h @ w_up)) @ w_down
    return rms_norm(y, norm_post)


def _fwd_setup_inputs(seed: int = 0) -> dict:
    key = jax.random.key(seed)
    ks = jax.random.split(key, 32)
    L, D = DEPTH, D_MODEL

    def dense(k, shape, fan_in):
        return jax.random.normal(k, shape, jnp.float32) * fan_in ** -0.5

    def gain(k, shape):
        return 1.0 + 0.02 * jax.random.normal(k, shape, jnp.float32)

    x = jax.random.normal(ks[0], (BATCH, SEQ, D), jnp.float32)
    offsets = jax.random.randint(ks[1], (BATCH, 1), 0, 4096, dtype=jnp.int32)
    positions = offsets + jnp.arange(SEQ, dtype=jnp.int32)[None, :]
    return {
        "x": x,
        "positions": positions,
        "mix_norm_pre": gain(ks[2], (L, D)),
        "w_in": dense(ks[3], (L, D, D_IN), D),
        "q_norm": gain(ks[4], (L, Q_LORA_RANK)),
        "w_uq": dense(ks[5], (L, Q_LORA_RANK, N_HEADS * (QK_NOPE_DIM + QK_ROPE_DIM)), Q_LORA_RANK),
        "kv_norm": gain(ks[6], (L, KV_LORA_RANK)),
        "w_uk": dense(ks[7], (L, KV_LORA_RANK, N_HEADS * QK_NOPE_DIM), KV_LORA_RANK),
        "w_uv": dense(ks[8], (L, KV_LORA_RANK, N_HEADS * V_HEAD_DIM), KV_LORA_RANK),
        "w_attn_o": dense(ks[9], (L, N_HEADS * V_HEAD_DIM, D), N_HEADS * V_HEAD_DIM),
        "conv_w": dense(ks[10], (L, CONV_WIDTH, CONV_CHANNELS), CONV_WIDTH),
        "conv_b": 0.02 * jax.random.normal(ks[11], (L, CONV_CHANNELS), jnp.float32),
        "conv_ln_g": gain(ks[12], (L, CONV_CHANNELS)),
        "conv_ln_b": 0.02 * jax.random.normal(ks[13], (L, CONV_CHANNELS), jnp.float32),
        "w_conv_o": dense(ks[14], (L, CONV_CHANNELS, D), CONV_CHANNELS),
        "pool_w": dense(ks[15], (L, POOL_GROUPS, POOL_GROUP_DIM, POOL_GROUP_DIM), POOL_GROUP_DIM),
        "pool_scale": 1.0 + 0.1 * jax.random.normal(ks[16], (L, POOL_CHANNELS), jnp.float32),
        "w_pool_o": dense(ks[17], (L, POOL_CHANNELS, D), POOL_CHANNELS),
        "w_mix_o": dense(ks[18], (L, D, D), D),
        "mix_norm_post": gain(ks[19], (L, D)),
        "ffn_norm_pre": gain(ks[20], (L, D)),
        "w_gate": dense(ks[21], (L, D, D_FF), D),
        "w_up": dense(ks[22], (L, D, D_FF), D),
        "w_down": dense(ks[23], (L, D_FF, D), D_FF),
        "ffn_norm_post": gain(ks[24], (L, D)),
    }


def _fwd_reference(x, positions, mix_norm_pre, w_in, q_norm, w_uq, kv_norm, w_uk, w_uv, w_attn_o,
              conv_w, conv_b, conv_ln_g, conv_ln_b, w_conv_o, pool_w, pool_scale, w_pool_o,
              w_mix_o, mix_norm_post, ffn_norm_pre, w_gate, w_up, w_down, ffn_norm_post):
    cos, sin = rope_tables(positions)
    h = x
    for l in range(DEPTH):
        h = h + mixer_sublayer(h, cos, sin, mix_norm_pre[l], w_in[l], q_norm[l], w_uq[l], kv_norm[l],
                               w_uk[l], w_uv[l], w_attn_o[l], conv_w[l], conv_b[l], conv_ln_g[l],
                               conv_ln_b[l], w_conv_o[l], pool_w[l], pool_scale[l], w_pool_o[l],
                               w_mix_o[l], mix_norm_post[l])
        h = h + ffn_sublayer(h, ffn_norm_pre[l], w_gate[l], w_up[l], w_down[l], ffn_norm_post[l])
    return h


import jax as _jax
import jax.numpy as _jnp

TWIN_FORMAT = 'train_step'
FWD_PARAMS = ['x', 'positions', 'mix_norm_pre', 'w_in', 'q_norm', 'w_uq', 'kv_norm', 'w_uk', 'w_uv', 'w_attn_o', 'conv_w', 'conv_b', 'conv_ln_g', 'conv_ln_b', 'w_conv_o', 'pool_w', 'pool_scale', 'w_pool_o', 'w_mix_o', 'mix_norm_post', 'ffn_norm_pre', 'w_gate', 'w_up', 'w_down', 'ffn_norm_post']
TWIN_WEIGHTS = ['mix_norm_pre', 'w_in', 'q_norm', 'w_uq', 'kv_norm', 'w_uk', 'w_uv', 'w_attn_o', 'conv_w', 'conv_b', 'conv_ln_g', 'conv_ln_b', 'w_conv_o', 'pool_w', 'pool_scale', 'w_pool_o', 'w_mix_o', 'mix_norm_post', 'ffn_norm_pre', 'w_gate', 'w_up', 'w_down', 'ffn_norm_post']
TWIN_DIFF_INPUT = 'x'
TWIN_INPUTS = ['x', 'positions', 'mix_norm_pre', 'w_in', 'q_norm', 'w_uq', 'kv_norm', 'w_uk', 'w_uv', 'w_attn_o', 'conv_w', 'conv_b', 'conv_ln_g', 'conv_ln_b', 'w_conv_o', 'pool_w', 'pool_scale', 'w_pool_o', 'w_mix_o', 'mix_norm_post', 'ffn_norm_pre', 'w_gate', 'w_up', 'w_down', 'ffn_norm_post', 'loss_target', 'm_mix_norm_pre', 'm_w_in', 'm_q_norm', 'm_w_uq', 'm_kv_norm', 'm_w_uk', 'm_w_uv', 'm_w_attn_o', 'm_conv_w', 'm_conv_b', 'm_conv_ln_g', 'm_conv_ln_b', 'm_w_conv_o', 'm_pool_w', 'm_pool_scale', 'm_w_pool_o', 'm_w_mix_o', 'm_mix_norm_post', 'm_ffn_norm_pre', 'm_w_gate', 'm_w_up', 'm_w_down', 'm_ffn_norm_post', 'v_mix_norm_pre', 'v_w_in', 'v_q_norm', 'v_w_uq', 'v_kv_norm', 'v_w_uk', 'v_w_uv', 'v_w_attn_o', 'v_conv_w', 'v_conv_b', 'v_conv_ln_g', 'v_conv_ln_b', 'v_w_conv_o', 'v_pool_w', 'v_pool_scale', 'v_w_pool_o', 'v_w_mix_o', 'v_mix_norm_post', 'v_ffn_norm_pre', 'v_w_gate', 'v_w_up', 'v_w_down', 'v_ffn_norm_post']
TWIN_OUTPUTS = ['loss', 'grad_x', 'grad_mix_norm_pre', 'grad_w_in', 'grad_q_norm', 'grad_w_uq', 'grad_kv_norm', 'grad_w_uk', 'grad_w_uv', 'grad_w_attn_o', 'grad_conv_w', 'grad_conv_b', 'grad_conv_ln_g', 'grad_conv_ln_b', 'grad_w_conv_o', 'grad_pool_w', 'grad_pool_scale', 'grad_w_pool_o', 'grad_w_mix_o', 'grad_mix_norm_post', 'grad_ffn_norm_pre', 'grad_w_gate', 'grad_w_up', 'grad_w_down', 'grad_ffn_norm_post', 'delta_mix_norm_pre', 'delta_w_in', 'delta_q_norm', 'delta_w_uq', 'delta_kv_norm', 'delta_w_uk', 'delta_w_uv', 'delta_w_attn_o', 'delta_conv_w', 'delta_conv_b', 'delta_conv_ln_g', 'delta_conv_ln_b', 'delta_w_conv_o', 'delta_pool_w', 'delta_pool_scale', 'delta_w_pool_o', 'delta_w_mix_o', 'delta_mix_norm_post', 'delta_ffn_norm_pre', 'delta_w_gate', 'delta_w_up', 'delta_w_down', 'delta_ffn_norm_post', 'new_m_mix_norm_pre', 'new_m_w_in', 'new_m_q_norm', 'new_m_w_uq', 'new_m_kv_norm', 'new_m_w_uk', 'new_m_w_uv', 'new_m_w_attn_o', 'new_m_conv_w', 'new_m_conv_b', 'new_m_conv_ln_g', 'new_m_conv_ln_b', 'new_m_w_conv_o', 'new_m_pool_w', 'new_m_pool_scale', 'new_m_w_pool_o', 'new_m_w_mix_o', 'new_m_mix_norm_post', 'new_m_ffn_norm_pre', 'new_m_w_gate', 'new_m_w_up', 'new_m_w_down', 'new_m_ffn_norm_post', 'new_v_mix_norm_pre', 'new_v_w_in', 'new_v_q_norm', 'new_v_w_uq', 'new_v_kv_norm', 'new_v_w_uk', 'new_v_w_uv', 'new_v_w_attn_o', 'new_v_conv_w', 'new_v_conv_b', 'new_v_conv_ln_g', 'new_v_conv_ln_b', 'new_v_w_conv_o', 'new_v_pool_w', 'new_v_pool_scale', 'new_v_w_pool_o', 'new_v_w_mix_o', 'new_v_mix_norm_post', 'new_v_ffn_norm_pre', 'new_v_w_gate', 'new_v_w_up', 'new_v_w_down', 'new_v_ffn_norm_post']
TWIN_LEAF_KINDS = {'loss': 'loss', 'grad_x': 'grad_x', 'grad_mix_norm_pre': 'grad_w', 'grad_w_in': 'grad_w', 'grad_q_norm': 'grad_w', 'grad_w_uq': 'grad_w', 'grad_kv_norm': 'grad_w', 'grad_w_uk': 'grad_w', 'grad_w_uv': 'grad_w', 'grad_w_attn_o': 'grad_w', 'grad_conv_w': 'grad_w', 'grad_conv_b': 'grad_w', 'grad_conv_ln_g': 'grad_w', 'grad_conv_ln_b': 'grad_w', 'grad_w_conv_o': 'grad_w', 'grad_pool_w': 'grad_w', 'grad_pool_scale': 'grad_w', 'grad_w_pool_o': 'grad_w', 'grad_w_mix_o': 'grad_w', 'grad_mix_norm_post': 'grad_w', 'grad_ffn_norm_pre': 'grad_w', 'grad_w_gate': 'grad_w', 'grad_w_up': 'grad_w', 'grad_w_down': 'grad_w', 'grad_ffn_norm_post': 'grad_w', 'delta_mix_norm_pre': 'delta_w', 'delta_w_in': 'delta_w', 'delta_q_norm': 'delta_w', 'delta_w_uq': 'delta_w', 'delta_kv_norm': 'delta_w', 'delta_w_uk': 'delta_w', 'delta_w_uv': 'delta_w', 'delta_w_attn_o': 'delta_w', 'delta_conv_w': 'delta_w', 'delta_conv_b': 'delta_w', 'delta_conv_ln_g': 'delta_w', 'delta_conv_ln_b': 'delta_w', 'delta_w_conv_o': 'delta_w', 'delta_pool_w': 'delta_w', 'delta_pool_scale': 'delta_w', 'delta_w_pool_o': 'delta_w', 'delta_w_mix_o': 'delta_w', 'delta_mix_norm_post': 'delta_w', 'delta_ffn_norm_pre': 'delta_w', 'delta_w_gate': 'delta_w', 'delta_w_up': 'delta_w', 'delta_w_down': 'delta_w', 'delta_ffn_norm_post': 'delta_w', 'new_m_mix_norm_pre': 'new_m', 'new_m_w_in': 'new_m', 'new_m_q_norm': 'new_m', 'new_m_w_uq': 'new_m', 'new_m_kv_norm': 'new_m', 'new_m_w_uk': 'new_m', 'new_m_w_uv': 'new_m', 'new_m_w_attn_o': 'new_m', 'new_m_conv_w': 'new_m', 'new_m_conv_b': 'new_m', 'new_m_conv_ln_g': 'new_m', 'new_m_conv_ln_b': 'new_m', 'new_m_w_conv_o': 'new_m', 'new_m_pool_w': 'new_m', 'new_m_pool_scale': 'new_m', 'new_m_w_pool_o': 'new_m', 'new_m_w_mix_o': 'new_m', 'new_m_mix_norm_post': 'new_m', 'new_m_ffn_norm_pre': 'new_m', 'new_m_w_gate': 'new_m', 'new_m_w_up': 'new_m', 'new_m_w_down': 'new_m', 'new_m_ffn_norm_post': 'new_m', 'new_v_mix_norm_pre': 'new_v', 'new_v_w_in': 'new_v', 'new_v_q_norm': 'new_v', 'new_v_w_uq': 'new_v', 'new_v_kv_norm': 'new_v', 'new_v_w_uk': 'new_v', 'new_v_w_uv': 'new_v', 'new_v_w_attn_o': 'new_v', 'new_v_conv_w': 'new_v', 'new_v_conv_b': 'new_v', 'new_v_conv_ln_g': 'new_v', 'new_v_conv_ln_b': 'new_v', 'new_v_w_conv_o': 'new_v', 'new_v_pool_w': 'new_v', 'new_v_pool_scale': 'new_v', 'new_v_w_pool_o': 'new_v', 'new_v_w_mix_o': 'new_v', 'new_v_mix_norm_post': 'new_v', 'new_v_ffn_norm_pre': 'new_v', 'new_v_w_gate': 'new_v', 'new_v_w_up': 'new_v', 'new_v_w_down': 'new_v', 'new_v_ffn_norm_post': 'new_v'}


def _forward(args):
    return _fwd_reference(*[args[k] for k in FWD_PARAMS])


def _output_shape():
    out = _jax.eval_shape(lambda: _forward(_fwd_setup_inputs(0)))
    return out.shape, out.dtype

N_MICROBATCH = 1
ADAM_LR = 0.001
ADAM_B1 = 0.9
ADAM_B2 = 0.999
ADAM_EPS = 1e-08
ADAM_WD = 0.01
ADAM_STEP = 10
PER_EXAMPLE_BATCH_AXIS = {'x': 0, 'positions': 0, 'loss_target': 0}
SHARED_INPUTS = []
_WEIGHT_DTYPES = {'mix_norm_pre': _jnp.float32, 'w_in': _jnp.float32, 'q_norm': _jnp.float32, 'w_uq': _jnp.float32, 'kv_norm': _jnp.float32, 'w_uk': _jnp.float32, 'w_uv': _jnp.float32, 'w_attn_o': _jnp.float32, 'conv_w': _jnp.float32, 'conv_b': _jnp.float32, 'conv_ln_g': _jnp.float32, 'conv_ln_b': _jnp.float32, 'w_conv_o': _jnp.float32, 'pool_w': _jnp.float32, 'pool_scale': _jnp.float32, 'w_pool_o': _jnp.float32, 'w_mix_o': _jnp.float32, 'mix_norm_post': _jnp.float32, 'ffn_norm_pre': _jnp.float32, 'w_gate': _jnp.float32, 'w_up': _jnp.float32, 'w_down': _jnp.float32, 'ffn_norm_post': _jnp.float32}
MOMENT_SCALE = {'mix_norm_pre': 1.202925e+00, 'w_in': 4.880830e-01, 'q_norm': 1.832234e-01, 'w_uq': 1.323151e-01, 'kv_norm': 7.062357e-01, 'w_uk': 1.348637e-01, 'w_uv': 4.233443e-01, 'w_attn_o': 2.916403e-01, 'conv_w': 7.202153e-01, 'conv_b': 8.425937e+00, 'conv_ln_g': 3.210581e+00, 'conv_ln_b': 4.907629e+00, 'w_conv_o': 1.203656e+00, 'pool_w': 1.488207e+00, 'pool_scale': 1.408444e+00, 'w_pool_o': 1.108106e+00, 'w_mix_o': 1.683859e+00, 'mix_norm_post': 3.226061e+01, 'ffn_norm_pre': 1.086560e+00, 'w_gate': 3.678851e-01, 'w_up': 5.135599e-01, 'w_down': 8.784748e-01, 'ffn_norm_post': 3.195626e+01}


def _to_microbatches(a, axis):
    t = _jnp.moveaxis(a, axis, 0)
    t = t.reshape((N_MICROBATCH, t.shape[0] // N_MICROBATCH) + t.shape[1:])
    return _jnp.moveaxis(t, 1, axis + 1)


def setup_inputs(seed: int = 0) -> dict:
    inp = _fwd_setup_inputs(seed)
    key = _jax.random.fold_in(_jax.random.key(seed), 7919)
    shape, _ = _output_shape()
    out = dict(inp)
    out["loss_target"] = _jax.random.normal(_jax.random.fold_in(key, 0), shape, _jnp.float32)
    for i, name in enumerate(TWIN_WEIGHTS):
        w = inp[name].astype(_jnp.float32)
        if MOMENT_SCALE is None:
            s = _jnp.sqrt(_jnp.mean(_jnp.square(w)) + 1e-30)
        else:
            s = MOMENT_SCALE[name]
        km, kv = _jax.random.split(_jax.random.fold_in(key, i + 1))
        out[name] = w
        out["m_" + name] = s * _jax.random.normal(km, w.shape, _jnp.float32)
        out["v_" + name] = (s * s) * _jax.random.uniform(kv, w.shape, _jnp.float32, 0.5, 1.5)
    if N_MICROBATCH > 1:
        for name, axis in PER_EXAMPLE_BATCH_AXIS.items():
            out[name] = _to_microbatches(out[name], axis)
    return {'x': out['x'], 'positions': out['positions'], 'mix_norm_pre': out['mix_norm_pre'], 'w_in': out['w_in'], 'q_norm': out['q_norm'], 'w_uq': out['w_uq'], 'kv_norm': out['kv_norm'], 'w_uk': out['w_uk'], 'w_uv': out['w_uv'], 'w_attn_o': out['w_attn_o'], 'conv_w': out['conv_w'], 'conv_b': out['conv_b'], 'conv_ln_g': out['conv_ln_g'], 'conv_ln_b': out['conv_ln_b'], 'w_conv_o': out['w_conv_o'], 'pool_w': out['pool_w'], 'pool_scale': out['pool_scale'], 'w_pool_o': out['w_pool_o'], 'w_mix_o': out['w_mix_o'], 'mix_norm_post': out['mix_norm_post'], 'ffn_norm_pre': out['ffn_norm_pre'], 'w_gate': out['w_gate'], 'w_up': out['w_up'], 'w_down': out['w_down'], 'ffn_norm_post': out['ffn_norm_post'], 'loss_target': out['loss_target'], 'm_mix_norm_pre': out['m_mix_norm_pre'], 'm_w_in': out['m_w_in'], 'm_q_norm': out['m_q_norm'], 'm_w_uq': out['m_w_uq'], 'm_kv_norm': out['m_kv_norm'], 'm_w_uk': out['m_w_uk'], 'm_w_uv': out['m_w_uv'], 'm_w_attn_o': out['m_w_attn_o'], 'm_conv_w': out['m_conv_w'], 'm_conv_b': out['m_conv_b'], 'm_conv_ln_g': out['m_conv_ln_g'], 'm_conv_ln_b': out['m_conv_ln_b'], 'm_w_conv_o': out['m_w_conv_o'], 'm_pool_w': out['m_pool_w'], 'm_pool_scale': out['m_pool_scale'], 'm_w_pool_o': out['m_w_pool_o'], 'm_w_mix_o': out['m_w_mix_o'], 'm_mix_norm_post': out['m_mix_norm_post'], 'm_ffn_norm_pre': out['m_ffn_norm_pre'], 'm_w_gate': out['m_w_gate'], 'm_w_up': out['m_w_up'], 'm_w_down': out['m_w_down'], 'm_ffn_norm_post': out['m_ffn_norm_post'], 'v_mix_norm_pre': out['v_mix_norm_pre'], 'v_w_in': out['v_w_in'], 'v_q_norm': out['v_q_norm'], 'v_w_uq': out['v_w_uq'], 'v_kv_norm': out['v_kv_norm'], 'v_w_uk': out['v_w_uk'], 'v_w_uv': out['v_w_uv'], 'v_w_attn_o': out['v_w_attn_o'], 'v_conv_w': out['v_conv_w'], 'v_conv_b': out['v_conv_b'], 'v_conv_ln_g': out['v_conv_ln_g'], 'v_conv_ln_b': out['v_conv_ln_b'], 'v_w_conv_o': out['v_w_conv_o'], 'v_pool_w': out['v_pool_w'], 'v_pool_scale': out['v_pool_scale'], 'v_w_pool_o': out['v_w_pool_o'], 'v_w_mix_o': out['v_w_mix_o'], 'v_mix_norm_post': out['v_mix_norm_post'], 'v_ffn_norm_pre': out['v_ffn_norm_pre'], 'v_w_gate': out['v_w_gate'], 'v_w_up': out['v_w_up'], 'v_w_down': out['v_w_down'], 'v_ffn_norm_post': out['v_ffn_norm_post']}


def _loss(weights, diff, rest, loss_target):
    with _jax.named_scope("forward"):
        args = {**rest, TWIN_DIFF_INPUT: diff, **{k: w.astype(_WEIGHT_DTYPES[k]) for k, w in weights.items()}}
        y = _forward(args)
    with _jax.named_scope("loss_head"):
        err = _jnp.square(y.astype(_jnp.float32) - loss_target)
        return 0.5 * _jnp.sum(_jnp.mean(err, axis=-1)) if err.ndim else 0.5 * err


def _adamw(w, g, m, v):
    m = ADAM_B1 * m + (1.0 - ADAM_B1) * g
    v = ADAM_B2 * v + (1.0 - ADAM_B2) * _jnp.square(g)
    m_hat = m / (1.0 - ADAM_B1 ** ADAM_STEP)
    v_hat = v / (1.0 - ADAM_B2 ** ADAM_STEP)
    delta = -ADAM_LR * (m_hat / (_jnp.sqrt(v_hat) + ADAM_EPS) + ADAM_WD * w)
    return delta, m, v


def reference(x, positions, mix_norm_pre, w_in, q_norm, w_uq, kv_norm, w_uk, w_uv, w_attn_o, conv_w, conv_b, conv_ln_g, conv_ln_b, w_conv_o, pool_w, pool_scale, w_pool_o, w_mix_o, mix_norm_post, ffn_norm_pre, w_gate, w_up, w_down, ffn_norm_post, loss_target, m_mix_norm_pre, m_w_in, m_q_norm, m_w_uq, m_kv_norm, m_w_uk, m_w_uv, m_w_attn_o, m_conv_w, m_conv_b, m_conv_ln_g, m_conv_ln_b, m_w_conv_o, m_pool_w, m_pool_scale, m_w_pool_o, m_w_mix_o, m_mix_norm_post, m_ffn_norm_pre, m_w_gate, m_w_up, m_w_down, m_ffn_norm_post, v_mix_norm_pre, v_w_in, v_q_norm, v_w_uq, v_kv_norm, v_w_uk, v_w_uv, v_w_attn_o, v_conv_w, v_conv_b, v_conv_ln_g, v_conv_ln_b, v_w_conv_o, v_pool_w, v_pool_scale, v_w_pool_o, v_w_mix_o, v_mix_norm_post, v_ffn_norm_pre, v_w_gate, v_w_up, v_w_down, v_ffn_norm_post):
    given = dict(x=x, positions=positions, mix_norm_pre=mix_norm_pre, w_in=w_in, q_norm=q_norm, w_uq=w_uq, kv_norm=kv_norm, w_uk=w_uk, w_uv=w_uv, w_attn_o=w_attn_o, conv_w=conv_w, conv_b=conv_b, conv_ln_g=conv_ln_g, conv_ln_b=conv_ln_b, w_conv_o=w_conv_o, pool_w=pool_w, pool_scale=pool_scale, w_pool_o=w_pool_o, w_mix_o=w_mix_o, mix_norm_post=mix_norm_post, ffn_norm_pre=ffn_norm_pre, w_gate=w_gate, w_up=w_up, w_down=w_down, ffn_norm_post=ffn_norm_post, loss_target=loss_target, m_mix_norm_pre=m_mix_norm_pre, m_w_in=m_w_in, m_q_norm=m_q_norm, m_w_uq=m_w_uq, m_kv_norm=m_kv_norm, m_w_uk=m_w_uk, m_w_uv=m_w_uv, m_w_attn_o=m_w_attn_o, m_conv_w=m_conv_w, m_conv_b=m_conv_b, m_conv_ln_g=m_conv_ln_g, m_conv_ln_b=m_conv_ln_b, m_w_conv_o=m_w_conv_o, m_pool_w=m_pool_w, m_pool_scale=m_pool_scale, m_w_pool_o=m_w_pool_o, m_w_mix_o=m_w_mix_o, m_mix_norm_post=m_mix_norm_post, m_ffn_norm_pre=m_ffn_norm_pre, m_w_gate=m_w_gate, m_w_up=m_w_up, m_w_down=m_w_down, m_ffn_norm_post=m_ffn_norm_post, v_mix_norm_pre=v_mix_norm_pre, v_w_in=v_w_in, v_q_norm=v_q_norm, v_w_uq=v_w_uq, v_kv_norm=v_kv_norm, v_w_uk=v_w_uk, v_w_uv=v_w_uv, v_w_attn_o=v_w_attn_o, v_conv_w=v_conv_w, v_conv_b=v_conv_b, v_conv_ln_g=v_conv_ln_g, v_conv_ln_b=v_conv_ln_b, v_w_conv_o=v_w_conv_o, v_pool_w=v_pool_w, v_pool_scale=v_pool_scale, v_w_pool_o=v_w_pool_o, v_w_mix_o=v_w_mix_o, v_mix_norm_post=v_mix_norm_post, v_ffn_norm_pre=v_ffn_norm_pre, v_w_gate=v_w_gate, v_w_up=v_w_up, v_w_down=v_w_down, v_ffn_norm_post=v_ffn_norm_post)
    weights = {n: given[n] for n in TWIN_WEIGHTS}
    shared = {n: given[n] for n in SHARED_INPUTS}
    per_example = {n: given[n] for n in ['x', 'positions']}
    grad_fn = _jax.value_and_grad(_loss, argnums=(0, 1))

    def one_microbatch(ex, loss_target):
        ex = dict(ex)
        diff = ex.pop(TWIN_DIFF_INPUT)
        return grad_fn(weights, diff, {**shared, **ex}, loss_target)

    if N_MICROBATCH == 1:
        loss, (grad_w, grad_x) = one_microbatch(per_example, given["loss_target"])
    else:
        def body(carry, xs):
            loss_sum, grad_sum = carry
            l_k, (gw_k, gx_k) = one_microbatch(xs[0], xs[1])
            with _jax.named_scope("update"):
                return (loss_sum + l_k, _jax.tree.map(_jnp.add, grad_sum, gw_k)), gx_k

        init = (_jnp.zeros((), _jnp.float32), _jax.tree.map(_jnp.zeros_like, weights))
        (loss, grad_w), grad_x = _jax.lax.scan(body, init, (per_example, given["loss_target"]))
    with _jax.named_scope("update"):
        delta_w, new_m, new_v = {}, {}, {}
        for n in TWIN_WEIGHTS:
            delta_w[n], new_m[n], new_v[n] = _adamw(weights[n], grad_w[n], given["m_" + n], given["v_" + n])
    return (loss, grad_x, *[grad_w[n] for n in TWIN_WEIGHTS], *[delta_w[n] for n in TWIN_WEIGHTS],
            *[new_m[n] for n in TWIN_WEIGHTS], *[new_v[n] for n in TWIN_WEIGHTS])
```

```python
import functools
import math

import numpy as np
import jax
import jax.numpy as jnp
from jax import lax
from jax.experimental import pallas as pl
from jax.experimental.pallas import tpu as pltpu

F32, BF16 = jnp.float32, jnp.bfloat16

D_MODEL = 1024
N_HEADS = 8
NOPE, ROPE, VDIM = 64, 32, 64
HALF_ROPE = ROPE // 2
Q_RANK, KV_RANK = 384, 256
CONV_C, CONV_W = 512, 31
POOL_C, POOL_G, POOL_GD = 512, 4, 128
POOL_WINDOWS = (2, 4, 8, 16)
D_FF = 2816
N_LAYERS = 2
EPS = 1e-6
ROPE_THETA = 10000.0
ATT_SCALE = 1.0 / math.sqrt(NOPE + ROPE)
O_Q, O_KV, O_KR, O_CONV, O_POOL, O_GATE, D_IN = 0, 384, 640, 672, 1696, 2208, 5280

LANE = 128
HP = 128
ZG, ZCA, ZCG, ZP, ZA, ZW = 0, 3072, 3584, 4096, 4608, 5376
ZA_W = Q_RANK + KV_RANK + HP
KR_LANE = NOPE
HW = N_HEADS * HP

ADAM_LR, ADAM_B1, ADAM_B2, ADAM_EPS, ADAM_WD, ADAM_STEP = 0.001, 0.9, 0.999, 1e-08, 0.01, 10

ROW_TILE = 256
ATT_TILE = 512
CONV_CHUNK = 256
MM_TM, MM_TN, MM_TK = 512, 1024, 1024
VMEM_LIMIT = 56 * 1024 * 1024

MESH_AXES = ("x", "y", "c")
PACK_W = 1024
HALF_ROWS_ALIGN = 16


def _cparams(sem):
    return pltpu.CompilerParams(dimension_semantics=sem, vmem_limit_bytes=VMEM_LIMIT)


def _tile(n, target):
    if n <= target:
        return n
    best = None
    for t in range(LANE, target + 1, LANE):
        if n % t == 0:
            best = t
    assert best is not None, (n, target)
    return best


def mm(a, b, *, ta=False, tb=False, out_dtype=F32, name):
    m, k = (a.shape[1], a.shape[0]) if ta else a.shape
    n, k2 = b.shape if tb else (b.shape[1], b.shape[0])
    assert k == k2, (a.shape, b.shape, ta, tb)
    tm, tn, tk = _tile(m, MM_TM), _tile(n, MM_TN), _tile(k, MM_TK)
    nk = k // tk
    dims = (((0 if ta else 1,), (1 if tb else 0,)), ((), ()))

    def body(a_ref, b_ref, o_ref, acc_ref):
        kk = pl.program_id(2)
        part = lax.dot_general(a_ref[...].astype(BF16), b_ref[...].astype(BF16), dims, preferred_element_type=F32)

        @pl.when(kk == 0)
        def _():
            acc_ref[...] = part

        @pl.when(kk > 0)
        def _():
            acc_ref[...] += part

        @pl.when(kk == nk - 1)
        def _():
            o_ref[...] = acc_ref[...].astype(o_ref.dtype)

    a_spec = pl.BlockSpec((tk, tm), lambda i, j, kk: (kk, i)) if ta else pl.BlockSpec((tm, tk), lambda i, j, kk: (i, kk))
    b_spec = pl.BlockSpec((tn, tk), lambda i, j, kk: (j, kk)) if tb else pl.BlockSpec((tk, tn), lambda i, j, kk: (kk, j))
    return pl.pallas_call(
        body,
        grid=(m // tm, n // tn, nk),
        in_specs=[a_spec, b_spec],
        out_specs=pl.BlockSpec((tm, tn), lambda i, j, kk: (i, j)),
        out_shape=jax.ShapeDtypeStruct((m, n), out_dtype),
        scratch_shapes=[pltpu.VMEM((tm, tn), F32)],
        compiler_params=_cparams(("parallel", "parallel", "arbitrary")),
        name=name,
    )(a, b)


def rowwise(name, body, rows, row_ins, full_ins, row_outs, acc_outs=()):
    tile = min(ROW_TILE, rows)
    in_specs = [pl.BlockSpec((tile, w), lambda i, cb=cb: (i, cb)) for _, w, cb in row_ins]
    in_specs += [pl.BlockSpec(a.shape, lambda i, nd=a.ndim: (0,) * nd) for a in full_ins]
    out_specs = [pl.BlockSpec((tile, w), lambda i: (i, 0)) for w, _ in row_outs]
    out_specs += [pl.BlockSpec(s, lambda i, nd=len(s): (0,) * nd) for s, _ in acc_outs]
    out_shape = [jax.ShapeDtypeStruct((rows, w), dt) for w, dt in row_outs]
    out_shape += [jax.ShapeDtypeStruct(s, dt) for s, dt in acc_outs]
    outs = pl.pallas_call(
        functools.partial(body),
        grid=(rows // tile,),
        in_specs=in_specs,
        out_specs=out_specs,
        out_shape=out_shape,
        compiler_params=_cparams(("arbitrary",)),
        name=name,
    )(*[a for a, _, _ in row_ins], *full_ins)
    return outs


def _whole(a):
    return (a, a.shape[1], 0)


def _acc(ref, val):
    @pl.when(pl.program_id(0) == 0)
    def _():
        ref[...] = val

    @pl.when(pl.program_id(0) > 0)
    def _():
        ref[...] += val


def _rms(x, g):
    return x * lax.rsqrt(jnp.mean(x * x, axis=-1, keepdims=True) + EPS) * g


def _layer_norm(x, g, b):
    mu = jnp.mean(x, axis=-1, keepdims=True)
    xc = x - mu
    return xc * lax.rsqrt(jnp.mean(xc * xc, axis=-1, keepdims=True) + EPS) * g + b


def _silu(x):
    return x * jax.nn.sigmoid(x)


def _rope(x, cc, sa, sb):
    return x * cc + pltpu.roll(x, HALF_ROPE, 1) * sa + pltpu.roll(x, HP - HALF_ROPE, 1) * sb


def _rope_t(dy, cc, sa, sb):
    return dy * cc + pltpu.roll(dy * sa, HP - HALF_ROPE, 1) + pltpu.roll(dy * sb, HALF_ROPE, 1)


def rope_tables(pos_col, rows):
    lane = np.arange(HP)
    idx = np.where(lane < KR_LANE + HALF_ROPE, lane - KR_LANE, lane - KR_LANE - HALF_ROPE)
    in_rope = (lane >= KR_LANE) & (lane < KR_LANE + ROPE)
    inv_freq = (np.float32(ROPE_THETA) ** (-np.arange(0, ROPE, 2, dtype=np.float32) / np.float32(ROPE))).astype(np.float32)
    freq_row = np.where(in_rope, inv_freq[np.clip(idx, 0, HALF_ROPE - 1)], 0.0).astype(np.float32)[None, :]
    first = ((lane >= KR_LANE) & (lane < KR_LANE + HALF_ROPE)).astype(np.float32)[None, :]
    second = ((lane >= KR_LANE + HALF_ROPE) & (lane < KR_LANE + ROPE)).astype(np.float32)[None, :]

    def body(pos_ref, f_ref, a_ref, b_ref, cc_ref, sa_ref, sb_ref):
        ang = pos_ref[...].astype(F32) * f_ref[...]
        s = jnp.sin(ang)
        cc_ref[...] = jnp.cos(ang)
        sa_ref[...] = s * b_ref[...]
        sb_ref[...] = -s * a_ref[...]

    return rowwise("rope_tables", body, rows, [_whole(pos_col)], [jnp.asarray(freq_row), jnp.asarray(first), jnp.asarray(second)],
                   [(HP, F32)] * 3)


def _causal_mask(t):
    r = lax.broadcasted_iota(jnp.int32, (t, t), 0)
    c = lax.broadcasted_iota(jnp.int32, (t, t), 1)
    return r, c


def attention_fwd(q, k, v):
    s_len = q.shape[0]
    t = min(ATT_TILE, s_len)
    nb = s_len // t

    def body(q_ref, k_ref, v_ref, o_ref, lse_ref, m_sc, l_sc, acc_sc):
        qi, ki = pl.program_id(1), pl.program_id(2)

        @pl.when(ki == 0)
        def _():
            m_sc[...] = jnp.full_like(m_sc, -jnp.inf)
            l_sc[...] = jnp.zeros_like(l_sc)
            acc_sc[...] = jnp.zeros_like(acc_sc)

        def step(masked):
            s = lax.dot_general(q_ref[...], k_ref[...], (((1,), (1,)), ((), ())), preferred_element_type=F32) * ATT_SCALE
            if masked:
                r, c = _causal_mask(t)
                s = jnp.where(c <= r, s, -jnp.inf)
            m_new = jnp.maximum(m_sc[...], jnp.max(s, axis=-1, keepdims=True))
            alpha = jnp.exp(m_sc[...] - m_new)
            p = jnp.exp(s - m_new)
            l_sc[...] = alpha * l_sc[...] + jnp.sum(p, axis=-1, keepdims=True)
            acc_sc[...] = alpha * acc_sc[...] + jnp.dot(p.astype(BF16), v_ref[...], preferred_element_type=F32)
            m_sc[...] = m_new

        @pl.when(ki < qi)
        def _():
            step(False)

        @pl.when(ki == qi)
        def _():
            step(True)
            o_ref[...] = (acc_sc[...] / l_sc[...]).astype(o_ref.dtype)
            lse_ref[0] = m_sc[...] + jnp.log(l_sc[...])

    kv_spec = pl.BlockSpec((t, HP), lambda h, qi, ki: (jnp.minimum(ki, qi), h))
    return pl.pallas_call(
        body,
        grid=(N_HEADS, nb, nb),
        in_specs=[pl.BlockSpec((t, HP), lambda h, qi, ki: (qi, h)), kv_spec, kv_spec],
        out_specs=[pl.BlockSpec((t, HP), lambda h, qi, ki: (qi, h)), pl.BlockSpec((1, t, 1), lambda h, qi, ki: (h, qi, 0))],
        out_shape=[jax.ShapeDtypeStruct((s_len, HW), BF16), jax.ShapeDtypeStruct((N_HEADS, s_len, 1), F32)],
        scratch_shapes=[pltpu.VMEM((t, 1), F32), pltpu.VMEM((t, 1), F32), pltpu.VMEM((t, HP), F32)],
        compiler_params=_cparams(("parallel", "parallel", "arbitrary")),
        name="attention_fwd",
    )(q, k, v)


def attention_delta(do, o):
    s_len = do.shape[0]
    t = min(ROW_TILE, s_len)

    def body(do_ref, o_ref, d_ref):
        prod = do_ref[...].astype(F32) * o_ref[...].astype(F32)
        for h in range(N_HEADS):
            d_ref[h] = jnp.sum(prod[:, h * HP:(h + 1) * HP], axis=-1, keepdims=True)

    return pl.pallas_call(
        body,
        grid=(s_len // t,),
        in_specs=[pl.BlockSpec((t, HW), lambda i: (i, 0))] * 2,
        out_specs=pl.BlockSpec((N_HEADS, t, 1), lambda i: (0, i, 0)),
        out_shape=jax.ShapeDtypeStruct((N_HEADS, s_len, 1), F32),
        compiler_params=_cparams(("arbitrary",)),
        name="attention_delta",
    )(do, o)


def attention_bwd_dq(q, k, v, do, lse, delta):
    s_len = q.shape[0]
    t = min(ATT_TILE, s_len)
    nb = s_len // t

    def body(q_ref, k_ref, v_ref, do_ref, lse_ref, dl_ref, dq_ref, acc_sc):
        qi, ki = pl.program_id(1), pl.program_id(2)

        @pl.when(ki == 0)
        def _():
            acc_sc[...] = jnp.zeros_like(acc_sc)

        def step(masked):
            s = lax.dot_general(q_ref[...], k_ref[...], (((1,), (1,)), ((), ())), preferred_element_type=F32) * ATT_SCALE
            p = jnp.exp(s - lse_ref[0])
            if masked:
                r, c = _causal_mask(t)
                p = jnp.where(c <= r, p, 0.0)
            dp = lax.dot_general(do_ref[...], v_ref[...], (((1,), (1,)), ((), ())), preferred_element_type=F32)
            ds = p * (dp - dl_ref[0]) * ATT_SCALE
            acc_sc[...] += jnp.dot(ds.astype(BF16), k_ref[...], preferred_element_type=F32)

        @pl.when(ki < qi)
        def _():
            step(False)

        @pl.when(ki == qi)
        def _():
            step(True)
            dq_ref[...] = acc_sc[...].astype(dq_ref.dtype)

    q_spec = pl.BlockSpec((t, HP), lambda h, qi, ki: (qi, h))
    kv_spec = pl.BlockSpec((t, HP), lambda h, qi, ki: (jnp.minimum(ki, qi), h))
    col_spec = pl.BlockSpec((1, t, 1), lambda h, qi, ki: (h, qi, 0))
    return pl.pallas_call(
        body,
        grid=(N_HEADS, nb, nb),
        in_specs=[q_spec, kv_spec, kv_spec, q_spec, col_spec, col_spec],
        out_specs=q_spec,
        out_shape=jax.ShapeDtypeStruct((s_len, HW), F32),
        scratch_shapes=[pltpu.VMEM((t, HP), F32)],
        compiler_params=_cparams(("parallel", "parallel", "arbitrary")),
        name="attention_bwd_dq",
    )(q, k, v, do, lse, delta)


def attention_bwd_dkv(q, k, v, do, lse_row, delta_row):
    s_len = q.shape[0]
    t = min(ATT_TILE, s_len)
    nb = s_len // t

    def body(q_ref, k_ref, v_ref, do_ref, lse_ref, dl_ref, dk_ref, dv_ref, dk_sc, dv_sc):
        ki, qi = pl.program_id(1), pl.program_id(2)

        @pl.when(qi == 0)
        def _():
            dk_sc[...] = jnp.zeros_like(dk_sc)
            dv_sc[...] = jnp.zeros_like(dv_sc)

        def step(masked):
            st = lax.dot_general(k_ref[...], q_ref[...], (((1,), (1,)), ((), ())), preferred_element_type=F32) * ATT_SCALE
            pt = jnp.exp(st - lse_ref[0])
            if masked:
                r, c = _causal_mask(t)
                pt = jnp.where(r <= c, pt, 0.0)
            dv_sc[...] += jnp.dot(pt.astype(BF16), do_ref[...], preferred_element_type=F32)
            dpt = lax.dot_general(v_ref[...], do_ref[...], (((1,), (1,)), ((), ())), preferred_element_type=F32)
            dst = pt * (dpt - dl_ref[0]) * ATT_SCALE
            dk_sc[...] += jnp.dot(dst.astype(BF16), q_ref[...], preferred_element_type=F32)

        @pl.when(qi > ki)
        def _():
            step(False)

        @pl.when(qi == ki)
        def _():
            step(True)

        @pl.when(qi == nb - 1)
        def _():
            dk_ref[...] = dk_sc[...].astype(dk_ref.dtype)
            dv_ref[...] = dv_sc[...].astype(dv_ref.dtype)

    k_spec = pl.BlockSpec((t, HP), lambda h, ki, qi: (ki, h))
    q_spec = pl.BlockSpec((t, HP), lambda h, ki, qi: (jnp.maximum(qi, ki), h))
    row_spec = pl.BlockSpec((1, 1, t), lambda h, ki, qi: (h, 0, jnp.maximum(qi, ki)))
    return pl.pallas_call(
        body,
        grid=(N_HEADS, nb, nb),
        in_specs=[q_spec, k_spec, k_spec, q_spec, row_spec, row_spec],
        out_specs=[k_spec, k_spec],
        out_shape=[jax.ShapeDtypeStruct((s_len, HW), F32), jax.ShapeDtypeStruct((s_len, HW), BF16)],
        scratch_shapes=[pltpu.VMEM((t, HP), F32), pltpu.VMEM((t, HP), F32)],
        compiler_params=_cparams(("parallel", "parallel", "arbitrary")),
        name="attention_bwd_dkv",
    )(q, k, v, do, lse_row, delta_row)


CONV_PAD = 32


def conv_fwd(z, conv_w, conv_b):
    s_len = z.shape[0]
    ch = min(CONV_CHUNK, s_len)

    def body(a_ref, g_ref, w_ref, b_ref, c_ref, pad_ref):
        pad_ref[0:CONV_PAD, :] = jnp.zeros((CONV_PAD, LANE), F32)
        pad_ref[CONV_PAD:CONV_PAD + s_len, :] = a_ref[...] * jax.nn.sigmoid(g_ref[...])

        def chunk(i, carry):
            base = pl.multiple_of(i * ch, ch)
            acc = jnp.zeros((ch, LANE), F32) + b_ref[...]
            for kk in range(CONV_W):
                acc = acc + pad_ref[pl.ds(base + CONV_PAD - (CONV_W - 1) + kk, ch), :] * w_ref[kk:kk + 1, :]
            c_ref[pl.ds(base, ch), :] = acc
            return carry

        lax.fori_loop(0, s_len // ch, chunk, 0)

    nblk = CONV_C // LANE
    return pl.pallas_call(
        body,
        grid=(nblk,),
        in_specs=[pl.BlockSpec((s_len, LANE), lambda j: (0, ZCA // LANE + j)), pl.BlockSpec((s_len, LANE), lambda j: (0, ZCG // LANE + j)),
                  pl.BlockSpec((CONV_W, LANE), lambda j: (0, j)), pl.BlockSpec((1, LANE), lambda j: (0, j))],
        out_specs=pl.BlockSpec((s_len, LANE), lambda j: (0, j)),
        out_shape=jax.ShapeDtypeStruct((s_len, CONV_C), F32),
        scratch_shapes=[pltpu.VMEM((s_len + CONV_PAD, LANE), F32)],
        compiler_params=_cparams(("arbitrary",)),
        name="conv_fwd",
    )(z, z, conv_w, conv_b)


def conv_bwd(z, dc, conv_w):
    s_len = z.shape[0]
    ch = min(CONV_CHUNK, s_len)

    def body(a_ref, g_ref, dc_ref, w_ref, da_ref, dg_ref, dw_ref, db_ref, pad_ref, dpad_ref, wacc_ref):
        sg = jax.nn.sigmoid(g_ref[...])
        pad_ref[0:CONV_PAD, :] = jnp.zeros((CONV_PAD, LANE), F32)
        pad_ref[CONV_PAD:CONV_PAD + s_len, :] = a_ref[...] * sg
        dpad_ref[0:s_len, :] = dc_ref[...]
        dpad_ref[s_len:s_len + CONV_PAD, :] = jnp.zeros((CONV_PAD, LANE), F32)
        wacc_ref[...] = jnp.zeros_like(wacc_ref)
        db_ref[...] = jnp.sum(dc_ref[...], axis=0, keepdims=True)

        def chunk(i, carry):
            base = pl.multiple_of(i * ch, ch)
            dcc = dpad_ref[pl.ds(base, ch), :]
            dh = jnp.zeros((ch, LANE), F32)
            for kk in range(CONV_W):
                dh = dh + dpad_ref[pl.ds(base + (CONV_W - 1) - kk, ch), :] * w_ref[kk:kk + 1, :]
                prod = dcc * pad_ref[pl.ds(base + CONV_PAD - (CONV_W - 1) + kk, ch), :]
                wacc_ref[kk * 8:(kk + 1) * 8, :] += prod.reshape(ch // 8, 8, LANE).sum(axis=0)
            a = a_ref[pl.ds(base, ch), :]
            sgc = jax.nn.sigmoid(g_ref[pl.ds(base, ch), :])
            da_ref[pl.ds(base, ch), :] = (dh * sgc).astype(da_ref.dtype)
            dg_ref[pl.ds(base, ch), :] = (dh * a * sgc * (1.0 - sgc)).astype(dg_ref.dtype)
            return carry

        lax.fori_loop(0, s_len // ch, chunk, 0)
        for kk in range(CONV_W):
            dw_ref[kk:kk + 1, :] = jnp.sum(wacc_ref[kk * 8:(kk + 1) * 8, :], axis=0, keepdims=True)

    nblk = CONV_C // LANE
    blk = lambda off: pl.BlockSpec((s_len, LANE), lambda j: (0, off // LANE + j))
    return pl.pallas_call(
        body,
        grid=(nblk,),
        in_specs=[blk(ZCA), blk(ZCG), blk(0), pl.BlockSpec((CONV_W, LANE), lambda j: (0, j))],
        out_specs=[blk(0), blk(0), pl.BlockSpec((CONV_W, LANE), lambda j: (0, j)), pl.BlockSpec((1, LANE), lambda j: (0, j))],
        out_shape=[jax.ShapeDtypeStruct((s_len, CONV_C), BF16), jax.ShapeDtypeStruct((s_len, CONV_C), BF16),
                   jax.ShapeDtypeStruct((CONV_W, CONV_C), F32), jax.ShapeDtypeStruct((1, CONV_C), F32)],
        scratch_shapes=[pltpu.VMEM((s_len + CONV_PAD, LANE), F32), pltpu.VMEM((s_len + CONV_PAD, LANE), F32),
                        pltpu.VMEM((CONV_W * 8, LANE), F32)],
        compiler_params=_cparams(("arbitrary",)),
        name="conv_bwd",
    )(z, z, dc, conv_w)


POOL_PAD = 16


def _pool_count(base, ch, w):
    t = base + lax.broadcasted_iota(jnp.int32, (ch, 1), 0)
    return jnp.minimum(t + 1, w).astype(F32)


def pool_fwd(z, pool_w, pool_scale):
    s_len = z.shape[0]
    ch = min(CONV_CHUNK, s_len)

    def body(u_ref, pw_ref, sc_ref, m_ref, pad_ref):
        gi = pl.program_id(0)
        pad_ref[0:POOL_PAD, :] = jnp.zeros((POOL_PAD, LANE), F32)
        pad_ref[POOL_PAD:POOL_PAD + s_len, :] = u_ref[...]

        def run(w):
            def chunk(i, carry):
                base = pl.multiple_of(i * ch, ch)
                acc = jnp.zeros((ch, LANE), F32)
                for j in range(w):
                    acc = acc + pad_ref[pl.ds(base + POOL_PAD - j, ch), :]
                d = acc / _pool_count(base, ch, w) - u_ref[pl.ds(base, ch), :]
                md = jnp.dot(d.astype(BF16), pw_ref[0], preferred_element_type=F32)
                m_ref[pl.ds(base, ch), :] = (md * sc_ref[...]).astype(m_ref.dtype)
                return carry

            lax.fori_loop(0, s_len // ch, chunk, 0)

        for g, w in enumerate(POOL_WINDOWS):
            pl.when(gi == g)(functools.partial(run, w))

    return pl.pallas_call(
        body,
        grid=(POOL_G,),
        in_specs=[pl.BlockSpec((s_len, LANE), lambda g: (0, ZP // LANE + g)), pl.BlockSpec((1, POOL_GD, POOL_GD), lambda g: (g, 0, 0)),
                  pl.BlockSpec((1, LANE), lambda g: (0, g))],
        out_specs=pl.BlockSpec((s_len, LANE), lambda g: (0, g)),
        out_shape=jax.ShapeDtypeStruct((s_len, POOL_C), BF16),
        scratch_shapes=[pltpu.VMEM((s_len + POOL_PAD, LANE), F32)],
        compiler_params=_cparams(("arbitrary",)),
        name="pool_fwd",
    )(z, pool_w, pool_scale)


def pool_bwd(z, dm, pool_w, pool_scale):
    s_len = z.shape[0]
    ch = min(CONV_CHUNK, s_len)

    def body(u_ref, dm_ref, pw_ref, sc_ref, du_ref, dpw_ref, dsc_ref, pad_ref, epad_ref, dd_ref, sacc_ref):
        gi = pl.program_id(0)
        pad_ref[0:POOL_PAD, :] = jnp.zeros((POOL_PAD, LANE), F32)
        pad_ref[POOL_PAD:POOL_PAD + s_len, :] = u_ref[...]
        epad_ref[s_len:s_len + POOL_PAD, :] = jnp.zeros((POOL_PAD, LANE), F32)
        dpw_ref[...] = jnp.zeros_like(dpw_ref)
        sacc_ref[...] = jnp.zeros_like(sacc_ref)

        def run(w):
            def first(i, carry):
                base = pl.multiple_of(i * ch, ch)
                acc = jnp.zeros((ch, LANE), F32)
                for j in range(w):
                    acc = acc + pad_ref[pl.ds(base + POOL_PAD - j, ch), :]
                cnt = _pool_count(base, ch, w)
                d = (acc / cnt - u_ref[pl.ds(base, ch), :]).astype(BF16)
                md = jnp.dot(d, pw_ref[0], preferred_element_type=F32)
                dmc = dm_ref[pl.ds(base, ch), :]
                sacc_ref[...] += (dmc * md).reshape(ch // 8, 8, LANE).sum(axis=0)
                dmd = (dmc * sc_ref[...]).astype(BF16)
                dpw_ref[0] += lax.dot_general(d, dmd, (((0,), (0,)), ((), ())), preferred_element_type=F32)
                dd = lax.dot_general(dmd, pw_ref[0], (((1,), (1,)), ((), ())), preferred_element_type=F32)
                dd_ref[pl.ds(base, ch), :] = dd
                epad_ref[pl.ds(base, ch), :] = dd / cnt
                return carry

            lax.fori_loop(0, s_len // ch, first, 0)

            def second(i, carry):
                base = pl.multiple_of(i * ch, ch)
                acc = jnp.zeros((ch, LANE), F32)
                for j in range(w):
                    acc = acc + epad_ref[pl.ds(base + j, ch), :]
                du_ref[pl.ds(base, ch), :] = (acc - dd_ref[pl.ds(base, ch), :]).astype(du_ref.dtype)
                return carry

            lax.fori_loop(0, s_len // ch, second, 0)

        for g, w in enumerate(POOL_WINDOWS):
            pl.when(gi == g)(functools.partial(run, w))
        dsc_ref[...] = jnp.sum(sacc_ref[...], axis=0, keepdims=True)

    return pl.pallas_call(
        body,
        grid=(POOL_G,),
        in_specs=[pl.BlockSpec((s_len, LANE), lambda g: (0, ZP // LANE + g)), pl.BlockSpec((s_len, LANE), lambda g: (0, g)),
                  pl.BlockSpec((1, POOL_GD, POOL_GD), lambda g: (g, 0, 0)), pl.BlockSpec((1, LANE), lambda g: (0, g))],
        out_specs=[pl.BlockSpec((s_len, LANE), lambda g: (0, g)), pl.BlockSpec((1, POOL_GD, POOL_GD), lambda g: (g, 0, 0)),
                   pl.BlockSpec((1, LANE), lambda g: (0, g))],
        out_shape=[jax.ShapeDtypeStruct((s_len, POOL_C), BF16), jax.ShapeDtypeStruct((POOL_G, POOL_GD, POOL_GD), F32),
                   jax.ShapeDtypeStruct((1, POOL_C), F32)],
        scratch_shapes=[pltpu.VMEM((s_len + POOL_PAD, LANE), F32), pltpu.VMEM((s_len + POOL_PAD, LANE), F32),
                        pltpu.VMEM((s_len, LANE), F32), pltpu.VMEM((8, LANE), F32)],
        compiler_params=_cparams(("arbitrary",)),
        name="pool_bwd",
    )(z, dm, pool_w, pool_scale)


def _row(v):
    return v.reshape(1, -1)


def layer_fwd(x, tabs, w, tag):
    s_len = x.shape[0]
    cc, sa, sb = tabs
    sv = {"x": x}

    def rms_body(x_ref, g_ref, o_ref):
        o_ref[...] = _rms(x_ref[...], g_ref[...]).astype(o_ref.dtype)

    (h,) = rowwise("mix_norm_pre" + tag, rms_body, s_len, [_whole(x)], [_row(w["mix_norm_pre"])], [(D_MODEL, BF16)])
    z = mm(h, w["w_in"], name="in_proj" + tag)

    def prep_body(z_ref, cc_ref, sa_ref, sb_ref, qg_ref, kg_ref, qn_ref, ckv_ref, kr_ref):
        qn_ref[...] = _rms(z_ref[:, 0:Q_RANK], qg_ref[...]).astype(qn_ref.dtype)
        ckv_ref[...] = _rms(z_ref[:, Q_RANK:Q_RANK + KV_RANK], kg_ref[...]).astype(ckv_ref.dtype)
        kr_ref[...] = _rope(z_ref[:, Q_RANK + KV_RANK:ZA_W], cc_ref[...], sa_ref[...], sb_ref[...])

    qn, ckvn, kr = rowwise("attn_prep" + tag, prep_body, s_len, [(z, ZA_W, ZA // ZA_W), _whole(cc), _whole(sa), _whole(sb)],
                           [_row(w["q_norm"]), _row(w["kv_norm"])], [(Q_RANK, BF16), (KV_RANK, BF16), (HP, F32)])
    q_raw = mm(qn, w["w_uq"], name="q_proj" + tag)
    kv_raw = mm(ckvn, w["w_ukv"], name="kv_proj" + tag)

    def qkv_body(q_ref, kv_ref, kr_ref, cc_ref, sa_ref, sb_ref, qo_ref, ko_ref, vo_ref):
        c_, a_, b_, kro = cc_ref[...], sa_ref[...], sb_ref[...], kr_ref[...]
        for hh in range(N_HEADS):
            sl = slice(hh * HP, (hh + 1) * HP)
            qo_ref[:, sl] = _rope(q_ref[:, sl], c_, a_, b_).astype(qo_ref.dtype)
            ko_ref[:, sl] = (kv_ref[:, sl] + kro).astype(ko_ref.dtype)
        vo_ref[...] = kv_ref[:, HW:2 * HW].astype(vo_ref.dtype)

    q, k, v = rowwise("qkv_rope" + tag, qkv_body, s_len, [_whole(q_raw), _whole(kv_raw), _whole(kr), _whole(cc), _whole(sa), _whole(sb)], [],
                      [(HW, BF16)] * 3)
    o, lse = attention_fwd(q, k, v)
    y_attn = mm(o, w["w_attn_o"], name="attn_out" + tag)

    c = conv_fwd(z, w["conv_w"], _row(w["conv_b"]))

    def ln_body(c_ref, g_ref, b_ref, o_ref):
        o_ref[...] = _silu(_layer_norm(c_ref[...], g_ref[...], b_ref[...])).astype(o_ref.dtype)

    (cs,) = rowwise("conv_ln_silu" + tag, ln_body, s_len, [_whole(c)], [_row(w["conv_ln_g"]), _row(w["conv_ln_b"])], [(CONV_C, BF16)])
    y_conv = mm(cs, w["w_conv_o"], name="conv_out" + tag)

    m = pool_fwd(z, w["pool_w"], _row(w["pool_scale"]))
    y_pool = mm(m, w["w_pool_o"], name="pool_out" + tag)

    def merge_body(ya_ref, yc_ref, yp_ref, gl_ref, o_ref):
        gl = gl_ref[...]
        o_ref[...] = (jax.nn.sigmoid(gl[:, 0:D_MODEL]) * ya_ref[...] + jax.nn.sigmoid(gl[:, D_MODEL:2 * D_MODEL]) * yc_ref[...]
                      + jax.nn.sigmoid(gl[:, 2 * D_MODEL:3 * D_MODEL]) * yp_ref[...]).astype(o_ref.dtype)

    (merged,) = rowwise("gate_merge" + tag, merge_body, s_len, [_whole(y_attn), _whole(y_conv), _whole(y_pool), (z, 3 * D_MODEL, 0)], [],
                        [(D_MODEL, BF16)])
    mo = mm(merged, w["w_mix_o"], name="mix_out" + tag)

    def post_body(y_ref, x_ref, g_ref, o_ref):
        o_ref[...] = x_ref[...] + _rms(y_ref[...], g_ref[...])

    (h1,) = rowwise("mix_norm_post" + tag, post_body, s_len, [_whole(mo), _whole(x)], [_row(w["mix_norm_post"])], [(D_MODEL, F32)])

    (hn,) = rowwise("ffn_norm_pre" + tag, rms_body, s_len, [_whole(h1)], [_row(w["ffn_norm_pre"])], [(D_MODEL, BF16)])
    gu = mm(hn, w["w_gu"], name="ffn_in" + tag)

    def swiglu_body(gu_ref, o_ref):
        o_ref[...] = (_silu(gu_ref[:, 0:D_FF]) * gu_ref[:, D_FF:2 * D_FF]).astype(o_ref.dtype)

    (act,) = rowwise("swiglu" + tag, swiglu_body, s_len, [_whole(gu)], [], [(D_FF, BF16)])
    y = mm(act, w["w_down"], name="ffn_out" + tag)
    (h2,) = rowwise("ffn_norm_post" + tag, post_body, s_len, [_whole(y), _whole(h1)], [_row(w["ffn_norm_post"])], [(D_MODEL, F32)])

    sv.update(h=h, z=z, qn=qn, ckvn=ckvn, q=q, k=k, v=v, o=o, lse=lse, c=c, cs=cs, m=m, y_attn=y_attn, y_conv=y_conv, y_pool=y_pool,
              merged=merged, mo=mo, h1=h1, hn=hn, gu=gu, act=act, y=y)
    return h2, sv


def layer_bwd(dh2, sv, tabs, w, tag):
    s_len = dh2.shape[0]
    cc, sa, sb = tabs
    g = {}

    def post_bwd_body(y_ref, dh_ref, g_ref, dy_ref, dg_ref):
        _, vjp = jax.vjp(_rms, y_ref[...], g_ref[...])
        dy, dg = vjp(dh_ref[...])
        dy_ref[...] = dy.astype(dy_ref.dtype)
        _acc(dg_ref, dg)

    def pre_bwd_body(x_ref, dhn_ref, dres_ref, g_ref, dx_ref, dg_ref):
        _, vjp = jax.vjp(_rms, x_ref[...], g_ref[...])
        dx, dg = vjp(dhn_ref[...])
        dx_ref[...] = dres_ref[...] + dx
        _acc(dg_ref, dg)

    d_y, g["ffn_norm_post"] = rowwise("ffn_norm_post_bwd" + tag, post_bwd_body, s_len, [_whole(sv["y"]), _whole(dh2)],
                                      [_row(w["ffn_norm_post"])], [(D_MODEL, BF16)], [((1, D_MODEL), F32)])
    g["w_down"] = mm(sv["act"], d_y, ta=True, name="ffn_out_dw" + tag)
    d_act = mm(d_y, w["w_down"], tb=True, name="ffn_out_dx" + tag)

    def swiglu_bwd_body(gu_ref, da_ref, dgu_ref):
        f = lambda gt, up: _silu(gt) * up
        _, vjp = jax.vjp(f, gu_ref[:, 0:D_FF], gu_ref[:, D_FF:2 * D_FF])
        dgt, dup = vjp(da_ref[...])
        dgu_ref[:, 0:D_FF] = dgt.astype(dgu_ref.dtype)
        dgu_ref[:, D_FF:2 * D_FF] = dup.astype(dgu_ref.dtype)

    (d_gu,) = rowwise("swiglu_bwd" + tag, swiglu_bwd_body, s_len, [_whole(sv["gu"]), _whole(d_act)], [], [(2 * D_FF, BF16)])
    g["w_gu"] = mm(sv["hn"], d_gu, ta=True, name="ffn_in_dw" + tag)
    d_hn = mm(d_gu, w["w_gu"], tb=True, name="ffn_in_dx" + tag)
    dh1, g["ffn_norm_pre"] = rowwise(
        "ffn_norm_pre_bwd" + tag, pre_bwd_body, s_len, [_whole(sv["h1"]), _whole(d_hn), _whole(dh2)], [_row(w["ffn_norm_pre"])], [(D_MODEL, F32)], [((1, D_MODEL), F32)])

    d_mo, g["mix_norm_post"] = rowwise(
        "mix_norm_post_bwd" + tag, post_bwd_body, s_len, [_whole(sv["mo"]), _whole(dh1)], [_row(w["mix_norm_post"])], [(D_MODEL, BF16)], [((1, D_MODEL), F32)])
    g["w_mix_o"] = mm(sv["merged"], d_mo, ta=True, name="mix_out_dw" + tag)
    d_merged = mm(d_mo, w["w_mix_o"], tb=True, name="mix_out_dx" + tag)

    def merge_bwd_body(dm_ref, ya_ref, yc_ref, yp_ref, gl_ref, dya_ref, dyc_ref, dyp_ref, dgl_ref):
        dmg = dm_ref[...]
        for i, (y_ref, dy_ref) in enumerate(((ya_ref, dya_ref), (yc_ref, dyc_ref), (yp_ref, dyp_ref))):
            sg = jax.nn.sigmoid(gl_ref[:, i * D_MODEL:(i + 1) * D_MODEL])
            dy_ref[...] = (dmg * sg).astype(dy_ref.dtype)
            dgl_ref[:, i * D_MODEL:(i + 1) * D_MODEL] = (dmg * y_ref[...] * sg * (1.0 - sg)).astype(dgl_ref.dtype)

    d_ya, d_yc, d_yp, d_gl = rowwise(
        "gate_merge_bwd" + tag, merge_bwd_body, s_len,
        [_whole(d_merged), _whole(sv["y_attn"]), _whole(sv["y_conv"]), _whole(sv["y_pool"]), (sv["z"], 3 * D_MODEL, 0)], [],
        [(D_MODEL, BF16)] * 3 + [(3 * D_MODEL, BF16)])

    g["w_pool_o"] = mm(sv["m"], d_yp, ta=True, name="pool_out_dw" + tag)
    d_m = mm(d_yp, w["w_pool_o"], tb=True, name="pool_out_dx" + tag)
    d_up, g["pool_w"], g["pool_scale"] = pool_bwd(sv["z"], d_m, w["pool_w"], _row(w["pool_scale"]))

    g["w_conv_o"] = mm(sv["cs"], d_yc, ta=True, name="conv_out_dw" + tag)
    d_cs = mm(d_yc, w["w_conv_o"], tb=True, name="conv_out_dx" + tag)

    def ln_bwd_body(c_ref, dcs_ref, g_ref, b_ref, dc_ref, dg_ref, db_ref):
        f = lambda c_, g_, b_: _silu(_layer_norm(c_, g_, b_))
        _, vjp = jax.vjp(f, c_ref[...], g_ref[...], b_ref[...])
        dc, dg, db = vjp(dcs_ref[...])
        dc_ref[...] = dc
        _acc(dg_ref, dg)
        _acc(db_ref, db)

    d_c, g["conv_ln_g"], g["conv_ln_b"] = rowwise("conv_ln_silu_bwd" + tag, ln_bwd_body, s_len, [_whole(sv["c"]), _whole(d_cs)],
                                                  [_row(w["conv_ln_g"]), _row(w["conv_ln_b"])], [(CONV_C, F32)],
                                                  [((1, CONV_C), F32), ((1, CONV_C), F32)])
    d_ca, d_cg, g["conv_w"], g["conv_b"] = conv_bwd(sv["z"], d_c, w["conv_w"])

    g["w_attn_o"] = mm(sv["o"], d_ya, ta=True, name="attn_out_dw" + tag)
    d_o = mm(d_ya, w["w_attn_o"], tb=True, out_dtype=BF16, name="attn_out_dx" + tag)
    delta = attention_delta(d_o, sv["o"])
    dq = attention_bwd_dq(sv["q"], sv["k"], sv["v"], d_o, sv["lse"], delta)
    dk, dv = attention_bwd_dkv(sv["q"], sv["k"], sv["v"], d_o, sv["lse"].reshape(N_HEADS, 1, s_len), delta.reshape(N_HEADS, 1, s_len))

    def qkv_bwd_body(dq_ref, dk_ref, dv_ref, cc_ref, sa_ref, sb_ref, dqp_ref, dkv_ref, dkr_ref):
        c_, a_, b_ = cc_ref[...], sa_ref[...], sb_ref[...]
        dk_sum = jnp.zeros((dq_ref.shape[0], HP), F32)
        for hh in range(N_HEADS):
            sl = slice(hh * HP, (hh + 1) * HP)
            dqp_ref[:, sl] = _rope_t(dq_ref[:, sl], c_, a_, b_).astype(dqp_ref.dtype)
            dkh = dk_ref[:, sl]
            dkv_ref[:, sl] = dkh.astype(dkv_ref.dtype)
            dk_sum = dk_sum + dkh
        dkv_ref[:, HW:2 * HW] = dv_ref[...]
        dkr_ref[...] = _rope_t(dk_sum, c_, a_, b_)

    dq_pre, dkv_pre, d_kr = rowwise("qkv_rope_bwd" + tag, qkv_bwd_body, s_len,
                                    [_whole(dq), _whole(dk), _whole(dv), _whole(cc), _whole(sa), _whole(sb)], [],
                                    [(HW, BF16), (2 * HW, BF16), (HP, F32)])
    g["w_uq"] = mm(sv["qn"], dq_pre, ta=True, name="q_proj_dw" + tag)
    d_qn = mm(dq_pre, w["w_uq"], tb=True, name="q_proj_dx" + tag)
    g["w_ukv"] = mm(sv["ckvn"], dkv_pre, ta=True, name="kv_proj_dw" + tag)
    d_ckvn = mm(dkv_pre, w["w_ukv"], tb=True, name="kv_proj_dx" + tag)

    def prep_bwd_body(z_ref, dqn_ref, dckv_ref, dkr_ref, qg_ref, kg_ref, dz_ref, dqg_ref, dkg_ref):
        _, vq = jax.vjp(_rms, z_ref[:, 0:Q_RANK], qg_ref[...])
        dcq, dqg = vq(dqn_ref[...])
        _, vk = jax.vjp(_rms, z_ref[:, Q_RANK:Q_RANK + KV_RANK], kg_ref[...])
        dckv, dkg = vk(dckv_ref[...])
        dz_ref[:, 0:Q_RANK] = dcq.astype(dz_ref.dtype)
        dz_ref[:, Q_RANK:Q_RANK + KV_RANK] = dckv.astype(dz_ref.dtype)
        dz_ref[:, Q_RANK + KV_RANK:ZA_W] = dkr_ref[...].astype(dz_ref.dtype)
        _acc(dqg_ref, dqg)
        _acc(dkg_ref, dkg)

    d_za, g["q_norm"], g["kv_norm"] = rowwise("attn_prep_bwd" + tag, prep_bwd_body, s_len,
                                              [(sv["z"], ZA_W, ZA // ZA_W), _whole(d_qn), _whole(d_ckvn), _whole(d_kr)],
                                              [_row(w["q_norm"]), _row(w["kv_norm"])], [(ZA_W, BF16)],
                                              [((1, Q_RANK), F32), ((1, KV_RANK), F32)])

    dz = jnp.concatenate([d_gl, d_ca, d_cg, d_up, d_za], axis=1)
    g["w_in"] = mm(sv["h"], dz, ta=True, name="in_proj_dw" + tag)
    d_h = mm(dz, w["w_in"], tb=True, name="in_proj_dx" + tag)
    dx, g["mix_norm_pre"] = rowwise(
        "mix_norm_pre_bwd" + tag, pre_bwd_body, s_len, [_whole(sv["x"]), _whole(d_h), _whole(dh1)], [_row(w["mix_norm_pre"])], [(D_MODEL, F32)], [((1, D_MODEL), F32)])
    return dx, g


def loss_head(h, target):
    s_len = h.shape[0]

    def body(h_ref, t_ref, dy_ref, loss_ref):
        err = h_ref[...] - t_ref[...]
        dy_ref[...] = err * (1.0 / D_MODEL)
        part = 0.5 * jnp.sum(jnp.mean(err * err, axis=-1, keepdims=True), axis=0, keepdims=True)
        _acc(loss_ref, jnp.broadcast_to(part, (1, LANE)))

    return rowwise("loss_head", body, s_len, [_whole(h), _whole(target)], [], [(D_MODEL, F32)], [((1, LANE), F32)])


def local_step(x, pos_col, target, layers):
    s_len = x.shape[0]
    tabs = rope_tables(pos_col, s_len)
    h, saved = x, []
    for li, w in enumerate(layers):
        h, sv = layer_fwd(h, tabs, w, f"_l{li}")
        saved.append(sv)
    dh, loss = loss_head(h, target)
    grads = [None] * len(layers)
    for li in reversed(range(len(layers))):
        dh, grads[li] = layer_bwd(dh, saved[li], tabs, layers[li], f"_l{li}")
    return loss[0, 0], dh, grads


def _pad_heads_cols(wm, per_head):
    r = wm.shape[0]
    return jnp.pad(wm.reshape(r, N_HEADS, per_head), ((0, 0), (0, 0), (0, HP - per_head))).reshape(r, HW)


def _unpad_heads_cols(wm, per_head):
    r = wm.shape[0]
    return wm.reshape(r, N_HEADS, HP)[:, :, :per_head].reshape(r, N_HEADS * per_head)


def align_weights(p):
    w_in = p["w_in"]
    r = w_in.shape[0]
    zeros = lambda n: jnp.zeros((r, n), w_in.dtype)
    w_in_al = jnp.concatenate([
        w_in[:, O_GATE:D_IN], w_in[:, O_CONV:O_POOL], w_in[:, O_POOL:O_GATE], w_in[:, O_Q:O_KR],
        zeros(KR_LANE), w_in[:, O_KR:O_CONV], zeros(HP - KR_LANE - ROPE)], axis=1)
    out = dict(p)
    out["w_in"] = w_in_al
    out["w_uq"] = _pad_heads_cols(p["w_uq"], NOPE + ROPE)
    out["w_ukv"] = jnp.concatenate([_pad_heads_cols(p["w_uk"], NOPE), _pad_heads_cols(p["w_uv"], VDIM)], axis=1)
    wo = p["w_attn_o"]
    out["w_attn_o"] = jnp.pad(wo.reshape(N_HEADS, VDIM, D_MODEL), ((0, 0), (0, HP - VDIM), (0, 0))).reshape(HW, D_MODEL)
    out["w_gu"] = jnp.concatenate([p["w_gate"], p["w_up"]], axis=1)
    for name in ("w_uk", "w_uv", "w_gate", "w_up"):
        del out[name]
    return out


def unalign_grads(g):
    gi = g["w_in"]
    kr0 = ZA + Q_RANK + KV_RANK + KR_LANE
    out = dict(g)
    out["w_in"] = jnp.concatenate([gi[:, ZA:ZA + Q_RANK + KV_RANK], gi[:, kr0:kr0 + ROPE], gi[:, ZCA:ZP], gi[:, ZP:ZA], gi[:, ZG:ZCA]], axis=1)
    out["w_uq"] = _unpad_heads_cols(g["w_uq"], NOPE + ROPE)
    out["w_uk"] = _unpad_heads_cols(g["w_ukv"][:, :HW], NOPE)
    out["w_uv"] = _unpad_heads_cols(g["w_ukv"][:, HW:], VDIM)
    out["w_attn_o"] = g["w_attn_o"].reshape(N_HEADS, HP, D_MODEL)[:, :VDIM].reshape(N_HEADS * VDIM, D_MODEL)
    out["w_gate"] = g["w_gu"][:, :D_FF]
    out["w_up"] = g["w_gu"][:, D_FF:]
    del out["w_ukv"], out["w_gu"]
    return out


MESH = pl.DeviceIdType.MESH
ANY = pl.BlockSpec(memory_space=pl.ANY)


def _place():
    return lax.axis_index("x"), lax.axis_index("y"), lax.axis_index("c")


def _other_chips(x, y):
    return [(1 - x, y), (x, 1 - y), (1 - x, 1 - y)]


def gather_shards(local):
    _, rows, width = local.shape

    def body(w_ref, out_ref, send_sems, recv_sems, local_sem):
        x, y, c = _place()
        me = 2 * x + y
        chips = _other_chips(x, y)
        sibling = (x, y, 1 - c)

        def copy(k, slot, half, to, src=None):
            dst = out_ref.at[slot, half]
            return pltpu.make_async_remote_copy(src_ref=dst if src is None else src, dst_ref=dst, send_sem=send_sems.at[k],
                                                recv_sem=recv_sems.at[k], device_id=to, device_id_type=MESH)

        mine = pltpu.make_async_copy(w_ref, out_ref.at[me], local_sem)
        mine.start()
        first = [copy(j, me, c, (*chip, c), src=w_ref.at[c]) for j, chip in enumerate(chips)]
        for cp in first:
            cp.start()
        passed = []
        for j, chip in enumerate(chips):
            slot = 2 * chip[0] + chip[1]
            copy(j, slot, c, sibling).wait_recv()
            fwd = copy(3 + j, slot, c, sibling)
            fwd.start()
            passed.append(fwd)
        for j, chip in enumerate(chips):
            copy(3 + j, 2 * chip[0] + chip[1], 1 - c, sibling).wait_recv()
        for cp in first + passed:
            cp.wait_send()
        mine.wait()

    return pl.pallas_call(
        body,
        in_specs=[ANY],
        out_specs=ANY,
        out_shape=jax.ShapeDtypeStruct((4, 2, rows, width), local.dtype),
        scratch_shapes=[pltpu.SemaphoreType.DMA((6,)), pltpu.SemaphoreType.DMA((6,)), pltpu.SemaphoreType.DMA],
        name="gather_shards",
    )(local)


def sibling_swap(g):
    _, _, rows, width = g.shape

    def body(g_ref, mine_ref, theirs_ref, send_sems, recv_sems, local_sems):
        x, y, c = _place()
        sibling = (x, y, 1 - c)
        local, remote = [], []
        for j in range(4):
            lc = pltpu.make_async_copy(g_ref.at[j, c], mine_ref.at[j], local_sems.at[j])
            lc.start()
            local.append(lc)
            rc = pltpu.make_async_remote_copy(src_ref=g_ref.at[j, 1 - c], dst_ref=theirs_ref.at[j], send_sem=send_sems.at[j],
                                              recv_sem=recv_sems.at[j], device_id=sibling, device_id_type=MESH)
            rc.start()
            remote.append(rc)
        for rc in remote:
            rc.wait_recv()
        for rc in remote:
            rc.wait_send()
        for lc in local:
            lc.wait()

    shape = jax.ShapeDtypeStruct((4, rows, width), g.dtype)
    return pl.pallas_call(
        body,
        in_specs=[ANY],
        out_specs=[ANY, ANY],
        out_shape=[shape, shape],
        scratch_shapes=[pltpu.SemaphoreType.DMA((4,)), pltpu.SemaphoreType.DMA((4,)), pltpu.SemaphoreType.DMA((4,))],
        name="sibling_swap",
    )(g)


def chip_exchange(p):
    _, rows, width = p.shape

    def body(p_ref, out_ref, send_sems, recv_sems, local_sem):
        x, y, c = _place()
        me = 2 * x + y
        chips = _other_chips(x, y)
        mine = pltpu.make_async_copy(p_ref.at[me], out_ref.at[me], local_sem)
        mine.start()
        sends = []
        for j, chip in enumerate(chips):
            cp = pltpu.make_async_remote_copy(src_ref=p_ref.at[2 * chip[0] + chip[1]], dst_ref=out_ref.at[me], send_sem=send_sems.at[j],
                                              recv_sem=recv_sems.at[j], device_id=(*chip, c), device_id_type=MESH)
            cp.start()
            sends.append(cp)
        for j, chip in enumerate(chips):
            slot = 2 * chip[0] + chip[1]
            pltpu.make_async_remote_copy(src_ref=p_ref.at[slot], dst_ref=out_ref.at[slot], send_sem=send_sems.at[j], recv_sem=recv_sems.at[j],
                                         device_id=(*chip, c), device_id_type=MESH).wait_recv()
        for cp in sends:
            cp.wait_send()
        mine.wait()

    return pl.pallas_call(
        body,
        in_specs=[ANY],
        out_specs=ANY,
        out_shape=jax.ShapeDtypeStruct(p.shape, p.dtype),
        scratch_shapes=[pltpu.SemaphoreType.DMA((3,)), pltpu.SemaphoreType.DMA((3,)), pltpu.SemaphoreType.DMA],
        name="chip_exchange",
    )(p)


def sibling_gather(f):
    rows, width = f.shape

    def body(f_ref, out_ref, send_sem, recv_sem, local_sem):
        x, y, c = _place()
        mine = pltpu.make_async_copy(f_ref, out_ref.at[c], local_sem)
        mine.start()
        cp = pltpu.make_async_remote_copy(src_ref=f_ref, dst_ref=out_ref.at[c], send_sem=send_sem, recv_sem=recv_sem,
                                          device_id=(x, y, 1 - c), device_id_type=MESH)
        cp.start()
        pltpu.make_async_remote_copy(src_ref=f_ref, dst_ref=out_ref.at[1 - c], send_sem=send_sem, recv_sem=recv_sem,
                                     device_id=(x, y, 1 - c), device_id_type=MESH).wait_recv()
        cp.wait_send()
        mine.wait()

    return pl.pallas_call(
        body,
        in_specs=[ANY],
        out_specs=ANY,
        out_shape=jax.ShapeDtypeStruct((2, rows, width), f.dtype),
        scratch_shapes=[pltpu.SemaphoreType.DMA, pltpu.SemaphoreType.DMA, pltpu.SemaphoreType.DMA],
        name="sibling_gather",
    )(f)


def gather_all(v):
    rows, width = v.shape

    def body(v_ref, out_ref, send_sems, recv_sems, local_sem):
        x, y, c = _place()
        me, sibling = (x, y, c), (x, y, 1 - c)
        chips = _other_chips(x, y)

        def blk(px, py, pc):
            return out_ref.at[4 * px + 2 * py + pc]

        def copy(k, block, to, src=None):
            return pltpu.make_async_remote_copy(src_ref=blk(*block) if src is None else src, dst_ref=blk(*block), send_sem=send_sems.at[k],
                                                recv_sem=recv_sems.at[k], device_id=to, device_id_type=MESH)

        mine = pltpu.make_async_copy(v_ref, blk(*me), local_sem)
        mine.start()
        first = [copy(0, me, sibling, src=v_ref)]
        first += [copy(1 + j, me, (*chip, c), src=v_ref) for j, chip in enumerate(chips)]
        for cp in first:
            cp.start()
        passed = [copy(4 + j, (*chip, c), sibling) for j, chip in enumerate(chips)]
        for j, chip in enumerate(chips):
            copy(1 + j, (*chip, c), me).wait_recv()
            passed[j].start()
        copy(0, sibling, me).wait_recv()
        for j, chip in enumerate(chips):
            copy(4 + j, (*chip, 1 - c), me).wait_recv()
        for cp in first + passed:
            cp.wait_send()
        mine.wait()

    return pl.pallas_call(
        body,
        in_specs=[ANY],
        out_specs=ANY,
        out_shape=jax.ShapeDtypeStruct((8, rows, width), v.dtype),
        scratch_shapes=[pltpu.SemaphoreType.DMA((7,)), pltpu.SemaphoreType.DMA((7,)), pltpu.SemaphoreType.DMA],
        name="gather_all",
    )(v)


def sum_slots(a, out_dtype, name, extra=None):
    n, rows, width = a.shape
    tile = _tile(rows, 512) if rows % LANE == 0 else _row_tile(rows)
    if extra is not None:
        def body(a_ref, b_ref, o_ref):
            o_ref[...] = (a_ref[...].astype(F32) + b_ref[...].astype(F32)).astype(o_ref.dtype)

        spec = pl.BlockSpec((1, tile, width), lambda s, i: (s, i, 0))
        return pl.pallas_call(body, grid=(n, rows // tile), in_specs=[spec, spec], out_specs=spec,
                              out_shape=jax.ShapeDtypeStruct(a.shape, out_dtype), compiler_params=_cparams(("parallel", "parallel")),
                              name=name)(a, extra)

    def body(a_ref, o_ref):
        acc = a_ref[0].astype(F32)
        for s in range(1, n):
            acc = acc + a_ref[s].astype(F32)
        o_ref[...] = acc.astype(o_ref.dtype)

    return pl.pallas_call(body, grid=(rows // tile,), in_specs=[pl.BlockSpec((n, tile, width), lambda i: (0, i, 0))],
                          out_specs=pl.BlockSpec((tile, width), lambda i: (i, 0)), out_shape=jax.ShapeDtypeStruct((rows, width), out_dtype),
                          compiler_params=_cparams(("parallel",)), name=name)(a)


def _row_tile(rows):
    best = None
    for t in range(16, 513, 16):
        if rows % t == 0:
            best = t
    assert best is not None, rows
    return best


def adamw(w, g, m, v, name):
    rows, cols = w.shape
    tile = rows
    for t in (512, 256, 128, 64, 32, 16, 8):
        if rows % t == 0 and t * cols * 4 <= 2 * 1024 * 1024:
            tile = t
            break

    def body(w_ref, g_ref, m_ref, v_ref, d_ref, mo_ref, vo_ref):
        gg = g_ref[...]
        m_new = ADAM_B1 * m_ref[...] + (1.0 - ADAM_B1) * gg
        v_new = ADAM_B2 * v_ref[...] + (1.0 - ADAM_B2) * (gg * gg)
        m_hat = m_new / (1.0 - ADAM_B1 ** ADAM_STEP)
        v_hat = v_new / (1.0 - ADAM_B2 ** ADAM_STEP)
        d_ref[...] = -ADAM_LR * (m_hat / (jnp.sqrt(v_hat) + ADAM_EPS) + ADAM_WD * w_ref[...])
        mo_ref[...] = m_new
        vo_ref[...] = v_new

    spec = pl.BlockSpec((tile, cols), lambda i: (i, 0))
    shape = jax.ShapeDtypeStruct((rows, cols), F32)
    return pl.pallas_call(body, grid=(rows // tile,), in_specs=[spec] * 4, out_specs=[spec] * 3, out_shape=[shape] * 3,
                          compiler_params=_cparams(("parallel",)), name=name)(w, g, m, v)


WEIGHTS = ["mix_norm_pre", "w_in", "q_norm", "w_uq", "kv_norm", "w_uk", "w_uv", "w_attn_o", "conv_w", "conv_b", "conv_ln_g", "conv_ln_b",
           "w_conv_o", "pool_w", "pool_scale", "w_pool_o", "w_mix_o", "mix_norm_post", "ffn_norm_pre", "w_gate", "w_up", "w_down",
           "ffn_norm_post"]
SHARDED = {"w_in": 2, "w_uq": 2, "w_uk": 2, "w_uv": 2, "w_attn_o": 2, "conv_w": 2, "w_conv_o": 2, "w_pool_o": 2, "w_mix_o": 1,
           "w_gate": 2, "w_up": 2, "w_down": 1}
REPLICATED = [n for n in WEIGHTS if n not in SHARDED]
N_CHIPS = 4


def _pack(parts, rows_multiple):
    flat = jnp.concatenate([p.reshape(-1) for p in parts])
    rows = -(-flat.shape[0] // PACK_W)
    rows = -(-rows // rows_multiple) * rows_multiple
    return jnp.pad(flat, (0, rows * PACK_W - flat.shape[0])).reshape(rows, PACK_W)


def _unpack(flat, shapes):
    out, off = [], 0
    for s in shapes:
        n = int(np.prod(s))
        out.append(flat[off:off + n].reshape(s))
        off += n
    return out


def _shard_split(name, full):
    return jnp.split(full, N_CHIPS, axis=SHARDED[name])


def kernel(x, positions, mix_norm_pre, w_in, q_norm, w_uq, kv_norm, w_uk, w_uv, w_attn_o, conv_w, conv_b, conv_ln_g, conv_ln_b, w_conv_o, pool_w, pool_scale, w_pool_o, w_mix_o, mix_norm_post, ffn_norm_pre, w_gate, w_up, w_down, ffn_norm_post, loss_target, m_mix_norm_pre, m_w_in, m_q_norm, m_w_uq, m_kv_norm, m_w_uk, m_w_uv, m_w_attn_o, m_conv_w, m_conv_b, m_conv_ln_g, m_conv_ln_b, m_w_conv_o, m_pool_w, m_pool_scale, m_w_pool_o, m_w_mix_o, m_mix_norm_post, m_ffn_norm_pre, m_w_gate, m_w_up, m_w_down, m_ffn_norm_post, v_mix_norm_pre, v_w_in, v_q_norm, v_w_uq, v_kv_norm, v_w_uk, v_w_uv, v_w_attn_o, v_conv_w, v_conv_b, v_conv_ln_g, v_conv_ln_b, v_w_conv_o, v_pool_w, v_pool_scale, v_w_pool_o, v_w_mix_o, v_mix_norm_post, v_ffn_norm_pre, v_w_gate, v_w_up, v_w_down, v_ffn_norm_post):
    given = dict(mix_norm_pre=mix_norm_pre, w_in=w_in, q_norm=q_norm, w_uq=w_uq, kv_norm=kv_norm, w_uk=w_uk, w_uv=w_uv, w_attn_o=w_attn_o,
                 conv_w=conv_w, conv_b=conv_b, conv_ln_g=conv_ln_g, conv_ln_b=conv_ln_b, w_conv_o=w_conv_o, pool_w=pool_w,
                 pool_scale=pool_scale, w_pool_o=w_pool_o, w_mix_o=w_mix_o, mix_norm_post=mix_norm_post, ffn_norm_pre=ffn_norm_pre,
                 w_gate=w_gate, w_up=w_up, w_down=w_down, ffn_norm_post=ffn_norm_post)
    mom = dict(mix_norm_pre=m_mix_norm_pre, w_in=m_w_in, q_norm=m_q_norm, w_uq=m_w_uq, kv_norm=m_kv_norm, w_uk=m_w_uk, w_uv=m_w_uv,
               w_attn_o=m_w_attn_o, conv_w=m_conv_w, conv_b=m_conv_b, conv_ln_g=m_conv_ln_g, conv_ln_b=m_conv_ln_b, w_conv_o=m_w_conv_o,
               pool_w=m_pool_w, pool_scale=m_pool_scale, w_pool_o=m_w_pool_o, w_mix_o=m_w_mix_o, mix_norm_post=m_mix_norm_post,
               ffn_norm_pre=m_ffn_norm_pre, w_gate=m_w_gate, w_up=m_w_up, w_down=m_w_down, ffn_norm_post=m_ffn_norm_post)
    var = dict(mix_norm_pre=v_mix_norm_pre, w_in=v_w_in, q_norm=v_q_norm, w_uq=v_w_uq, kv_norm=v_kv_norm, w_uk=v_w_uk, w_uv=v_w_uv,
               w_attn_o=v_w_attn_o, conv_w=v_conv_w, conv_b=v_conv_b, conv_ln_g=v_conv_ln_g, conv_ln_b=v_conv_ln_b, w_conv_o=v_w_conv_o,
               pool_w=v_pool_w, pool_scale=v_pool_scale, w_pool_o=v_w_pool_o, w_mix_o=v_w_mix_o, mix_norm_post=v_mix_norm_post,
               ffn_norm_pre=v_ffn_norm_pre, w_gate=v_w_gate, w_up=v_w_up, w_down=v_w_down, ffn_norm_post=v_ffn_norm_post)
    s_len = x.shape[1]
    sharded_names = [n for n in WEIGHTS if n in SHARDED]

    def to_wire(name, a):
        if name == "conv_w":
            hi = a.astype(BF16)
            lo = (a - hi.astype(F32)).astype(BF16)
            return jnp.stack([hi, lo])
        return a.astype(BF16)

    local_parts = [to_wire(n, given[n]) for n in sharded_names]
    packed = _pack(local_parts, 2 * HALF_ROWS_ALIGN)
    half_rows = packed.shape[0] // 2
    gathered = gather_shards(packed.reshape(2, half_rows, PACK_W)).reshape(N_CHIPS, -1)
    per_chip = [_unpack(gathered[j], [p.shape for p in local_parts]) for j in range(N_CHIPS)]
    full = {}
    for i, n in enumerate(sharded_names):
        if n == "conv_w":
            parts = [pc[i][0].astype(F32) + pc[i][1].astype(F32) for pc in per_chip]
        else:
            parts = [pc[i] for pc in per_chip]
        full[n] = jnp.concatenate(parts, axis=SHARDED[n])
    layers = []
    for li in range(N_LAYERS):
        p = {n: full[n][li] for n in sharded_names}
        p.update({n: given[n][li] for n in REPLICATED})
        p["pool_w"] = p["pool_w"].astype(BF16)
        layers.append(align_weights(p))

    loss_local, grad_x, grads = local_step(x[0], positions.reshape(s_len, 1), loss_target[0], layers)
    loss = lax.psum(loss_local, MESH_AXES)
    grads = [unalign_grads(g) for g in grads]
    grad_full = {n: jnp.stack([g[n].reshape(given[n].shape[1:]) if n in REPLICATED else g[n] for g in grads]) for n in WEIGHTS}

    shard_shapes = [given[n].shape for n in sharded_names]
    per_dest = [[] for _ in range(N_CHIPS)]
    for n in sharded_names:
        for j, part in enumerate(_shard_split(n, grad_full[n])):
            per_dest[j].append(part.astype(BF16))
    gpack = jnp.stack([_pack(parts, 2 * HALF_ROWS_ALIGN) for parts in per_dest])
    grows = gpack.shape[1] // 2
    mine, theirs = sibling_swap(gpack.reshape(N_CHIPS, 2, grows, PACK_W))
    chip_part = sum_slots(mine, BF16, "sibling_sum", extra=theirs)
    from_chips = chip_exchange(chip_part)
    half_sum = sum_slots(from_chips, F32, "chip_sum")
    shard_sum = sibling_gather(half_sum).reshape(-1)
    g_shard = dict(zip(sharded_names, _unpack(shard_sum, shard_shapes)))

    rep_shapes = [given[n].shape for n in REPLICATED]
    rep_pack = _pack([grad_full[n] for n in REPLICATED], 8)
    rep_sum = sum_slots(gather_all(rep_pack), F32, "replicated_sum").reshape(-1)
    g_rep = dict(zip(REPLICATED, _unpack(rep_sum, rep_shapes)))

    g_out, d_out, m_out, v_out = {}, {}, {}, {}
    for n in sharded_names:
        shp = given[n].shape
        two_d = (shp[0] * shp[1], shp[2])
        d, mn, vn = adamw(given[n].reshape(two_d), g_shard[n].reshape(two_d), mom[n].reshape(two_d), var[n].reshape(two_d), "adamw_" + n)
        g_out[n], d_out[n], m_out[n], v_out[n] = g_shard[n], d.reshape(shp), mn.reshape(shp), vn.reshape(shp)
    rep = [_pack([src[n] for n in REPLICATED], 8) for src in (given, g_rep, mom, var)]
    rd, rm, rv = adamw(*rep, "adamw_replicated")
    for n, d, mn, vn in zip(REPLICATED, *[_unpack(a.reshape(-1), rep_shapes) for a in (rd, rm, rv)]):
        g_out[n], d_out[n], m_out[n], v_out[n] = g_rep[n], d, mn, vn

    return (loss, grad_x[None], *[g_out[n] for n in WEIGHTS], *[d_out[n] for n in WEIGHTS], *[m_out[n] for n in WEIGHTS],
            *[v_out[n] for n in WEIGHTS])
```

```python
import functools
import math

import numpy as np
import jax
import jax.numpy as jnp
from jax import lax
from jax.experimental import pallas as pl
from jax.experimental.pallas import tpu as pltpu

F32, BF16 = jnp.float32, jnp.bfloat16

D_MODEL = 1024
N_HEADS = 8
NOPE, ROPE, VDIM = 64, 32, 64
HALF_ROPE = ROPE // 2
Q_RANK, KV_RANK = 384, 256
CONV_C, CONV_W = 512, 31
POOL_C, POOL_G, POOL_GD = 512, 4, 128
POOL_WINDOWS = (2, 4, 8, 16)
D_FF = 2816
N_LAYERS = 2
EPS = 1e-6
ROPE_THETA = 10000.0
ATT_SCALE = 1.0 / math.sqrt(NOPE + ROPE)
O_Q, O_KV, O_KR, O_CONV, O_POOL, O_GATE, D_IN = 0, 384, 640, 672, 1696, 2208, 5280

LANE = 128
HP = 128
ZG, ZC, ZP, ZA, ZW = 0, 3072, 4096, 4608, 5376
ZA_W = Q_RANK + KV_RANK + HP
KR_LANE = NOPE
HW = N_HEADS * HP

ADAM_LR, ADAM_B1, ADAM_B2, ADAM_EPS, ADAM_WD, ADAM_STEP = 0.001, 0.9, 0.999, 1e-08, 0.01, 10

ROW_TILE = 256
ATT_TILE_FWD = 1024
ATT_TILE_BWD = 512
ATT_HEADS = 2
CONV_CHUNK = 256
MM_TM, MM_TN, MM_TK = 1024, 1408, 1024
VMEM_LIMIT = 56 * 1024 * 1024
SUM_TILE_BYTES = 1024 * 1024

MESH_AXES = ("x", "y", "c")
PACK_W = 1024
PACK_ROWS = 8


def _cparams(sem):
    return pltpu.CompilerParams(dimension_semantics=sem, vmem_limit_bytes=VMEM_LIMIT)


def _tile(n, target):
    if n <= target:
        return n
    best = None
    for t in range(LANE, target + 1, LANE):
        if n % t == 0:
            best = t
    assert best is not None, (n, target)
    return best


def mm(a, b, *, ta=False, tb=False, out_dtype=F32, name):
    m, k = (a.shape[1], a.shape[0]) if ta else a.shape
    n, k2 = b.shape if tb else (b.shape[1], b.shape[0])
    assert k == k2, (a.shape, b.shape, ta, tb)
    tm, tn, tk = _tile(m, MM_TM), _tile(n, MM_TN), _tile(k, MM_TK)
    nk = k // tk
    dims = (((0 if ta else 1,), (1 if tb else 0,)), ((), ()))

    def body(a_ref, b_ref, o_ref, *acc):
        part = lax.dot_general(a_ref[...].astype(BF16), b_ref[...].astype(BF16), dims, preferred_element_type=F32)
        if nk == 1:
            o_ref[...] = part.astype(o_ref.dtype)
            return
        (acc_ref,) = acc
        kk = pl.program_id(2)

        @pl.when(kk == 0)
        def _():
            acc_ref[...] = part

        @pl.when(kk > 0)
        def _():
            acc_ref[...] += part

        @pl.when(kk == nk - 1)
        def _():
            o_ref[...] = acc_ref[...].astype(o_ref.dtype)

    a_spec = pl.BlockSpec((tk, tm), lambda i, j, kk: (kk, i)) if ta else pl.BlockSpec((tm, tk), lambda i, j, kk: (i, kk))
    b_spec = pl.BlockSpec((tn, tk), lambda i, j, kk: (j, kk)) if tb else pl.BlockSpec((tk, tn), lambda i, j, kk: (kk, j))
    return pl.pallas_call(
        body,
        grid=(m // tm, n // tn, nk),
        in_specs=[a_spec, b_spec],
        out_specs=pl.BlockSpec((tm, tn), lambda i, j, kk: (i, j)),
        out_shape=jax.ShapeDtypeStruct((m, n), out_dtype),
        scratch_shapes=[] if nk == 1 else [pltpu.VMEM((tm, tn), F32)],
        compiler_params=_cparams(("parallel", "parallel", "arbitrary")),
        name=name,
    )(a, b)


def rowwise(name, body, rows, row_ins, full_ins, row_outs, acc_outs=(), into=None):
    tile = min(ROW_TILE, rows)
    into = into or {}
    in_specs = [pl.BlockSpec((tile, w), lambda i, cb=cb: (i, cb)) for _, w, cb in row_ins]
    in_specs += [pl.BlockSpec(a.shape, lambda i, nd=a.ndim: (0,) * nd) for a in full_ins]
    in_specs += [ANY for _ in into]
    n_in = len(row_ins) + len(full_ins)
    aliases = {n_in + k: oi for k, oi in enumerate(into)}
    out_specs, out_shape = [], []
    for ro in row_outs:
        w, dt, full_w, cb = ro if len(ro) == 4 else (*ro, ro[0], 0)
        out_specs.append(pl.BlockSpec((tile, w), lambda i, cb=cb: (i, cb)))
        out_shape.append(jax.ShapeDtypeStruct((rows, full_w), dt))
    out_specs += [pl.BlockSpec(s, lambda i, nd=len(s): (0,) * nd) for s, _ in acc_outs]
    out_shape += [jax.ShapeDtypeStruct(s, dt) for s, dt in acc_outs]
    n_refs = n_in

    def call_body(*refs):
        body(*refs[:n_refs], *refs[n_refs + len(into):])

    outs = pl.pallas_call(
        call_body,
        grid=(rows // tile,),
        in_specs=in_specs,
        out_specs=out_specs,
        out_shape=out_shape,
        input_output_aliases=aliases,
        compiler_params=_cparams(("arbitrary",)),
        name=name,
    )(*[a for a, _, _ in row_ins], *full_ins, *into.values())
    return outs


def _whole(a):
    return (a, a.shape[1], 0)


def _acc(ref, val):
    @pl.when(pl.program_id(0) == 0)
    def _():
        ref[...] = val

    @pl.when(pl.program_id(0) > 0)
    def _():
        ref[...] += val


def _rms(x, g):
    return x * lax.rsqrt(jnp.mean(x * x, axis=-1, keepdims=True) + EPS) * g


def _layer_norm(x, g, b):
    mu = jnp.mean(x, axis=-1, keepdims=True)
    xc = x - mu
    return xc * lax.rsqrt(jnp.mean(xc * xc, axis=-1, keepdims=True) + EPS) * g + b


def _silu(x):
    return x * jax.nn.sigmoid(x)


def _rope(x, cc, sa, sb):
    return x * cc + pltpu.roll(x, HALF_ROPE, 1) * sa + pltpu.roll(x, HP - HALF_ROPE, 1) * sb


def _rope_t(dy, cc, sa, sb):
    return dy * cc + pltpu.roll(dy * sa, HP - HALF_ROPE, 1) + pltpu.roll(dy * sb, HALF_ROPE, 1)


def rope_tables(pos_col, rows):
    lane = np.arange(HP)
    idx = np.where(lane < KR_LANE + HALF_ROPE, lane - KR_LANE, lane - KR_LANE - HALF_ROPE)
    in_rope = (lane >= KR_LANE) & (lane < KR_LANE + ROPE)
    inv_freq = (np.float32(ROPE_THETA) ** (-np.arange(0, ROPE, 2, dtype=np.float32) / np.float32(ROPE))).astype(np.float32)
    freq_row = np.where(in_rope, inv_freq[np.clip(idx, 0, HALF_ROPE - 1)], 0.0).astype(np.float32)[None, :]
    first = ((lane >= KR_LANE) & (lane < KR_LANE + HALF_ROPE)).astype(np.float32)[None, :]
    second = ((lane >= KR_LANE + HALF_ROPE) & (lane < KR_LANE + ROPE)).astype(np.float32)[None, :]

    def body(pos_ref, f_ref, a_ref, b_ref, cc_ref, sa_ref, sb_ref):
        ang = pos_ref[...].astype(F32) * f_ref[...]
        s = jnp.sin(ang)
        cc_ref[...] = jnp.cos(ang)
        sa_ref[...] = s * b_ref[...]
        sb_ref[...] = -s * a_ref[...]

    return rowwise("rope_tables", body, rows, [_whole(pos_col)], [jnp.asarray(freq_row), jnp.asarray(first), jnp.asarray(second)],
                   [(HP, F32)] * 3)


def _causal_mask(t):
    r = lax.broadcasted_iota(jnp.int32, (t, t), 0)
    c = lax.broadcasted_iota(jnp.int32, (t, t), 1)
    return r, c


NT_DIMS = (((1,), (1,)), ((), ()))


def attention_fwd(q, k, v):
    s_len = q.shape[0]
    t = min(ATT_TILE_FWD, s_len)
    nb = s_len // t
    hb = ATT_HEADS
    w = hb * HP

    def body(q_ref, k_ref, v_ref, o_ref, lse_ref, m_sc, acc_sc):
        qi = pl.program_id(1)
        m_sc[...] = jnp.full_like(m_sc, -jnp.inf)
        acc_sc[...] = jnp.zeros_like(acc_sc)

        def block(j, masked):
            ks = pl.ds(pl.multiple_of(j * t, t), t)
            for hh in range(hb):
                ls = slice(hh * HP, (hh + 1) * HP)
                s = lax.dot_general(q_ref[:, ls], k_ref[ks, ls], NT_DIMS, preferred_element_type=F32) * ATT_SCALE
                if masked:
                    r, c = _causal_mask(t)
                    s = jnp.where(c <= r, s, -jnp.inf)
                m_old = m_sc[hh]
                m_new = jnp.maximum(m_old, jnp.max(s, axis=-1, keepdims=True))
                p = jnp.exp(s - m_new)
                acc_sc[hh] = jnp.exp(m_old - m_new) * acc_sc[hh] + jnp.dot(p.astype(BF16), v_ref[ks, ls], preferred_element_type=F32)
                m_sc[hh] = m_new

        def loop_body(j, carry):
            block(j, False)
            return carry

        lax.fori_loop(0, qi, loop_body, 0)
        block(qi, True)
        lane = lax.broadcasted_iota(jnp.int32, (t, HP), 1)
        for hh in range(hb):
            acc = acc_sc[hh]
            l = jnp.sum(jnp.where(lane == VDIM, acc, 0.0), axis=-1, keepdims=True)
            o_ref[:, hh * HP:(hh + 1) * HP] = jnp.where(lane < VDIM, acc / l, 0.0).astype(o_ref.dtype)
            lse_ref[hh] = m_sc[hh] + jnp.log(l)

    resident = pl.BlockSpec((s_len, w), lambda h, qi: (0, h))
    return pl.pallas_call(
        body,
        grid=(N_HEADS // hb, nb),
        in_specs=[pl.BlockSpec((t, w), lambda h, qi: (qi, h)), resident, resident],
        out_specs=[pl.BlockSpec((t, w), lambda h, qi: (qi, h)), pl.BlockSpec((hb, t, 1), lambda h, qi: (h, qi, 0))],
        out_shape=[jax.ShapeDtypeStruct((s_len, HW), BF16), jax.ShapeDtypeStruct((N_HEADS, s_len, 1), F32)],
        scratch_shapes=[pltpu.VMEM((hb, t, 1), F32), pltpu.VMEM((hb, t, HP), F32)],
        compiler_params=_cparams(("parallel", "arbitrary")),
        name="attention_fwd",
    )(q, k, v)


def attention_delta(do, o):
    s_len = do.shape[0]
    t = min(ROW_TILE, s_len)

    def body(do_ref, o_ref, d_ref):
        prod = do_ref[...].astype(F32) * o_ref[...].astype(F32)
        for h in range(N_HEADS):
            d_ref[h] = jnp.sum(prod[:, h * HP:(h + 1) * HP], axis=-1, keepdims=True)

    return pl.pallas_call(
        body,
        grid=(s_len // t,),
        in_specs=[pl.BlockSpec((t, HW), lambda i: (i, 0))] * 2,
        out_specs=pl.BlockSpec((N_HEADS, t, 1), lambda i: (0, i, 0)),
        out_shape=jax.ShapeDtypeStruct((N_HEADS, s_len, 1), F32),
        compiler_params=_cparams(("arbitrary",)),
        name="attention_delta",
    )(do, o)


def attention_bwd_dq(q, k, v, do, lse, delta):
    s_len = q.shape[0]
    t = min(ATT_TILE_BWD, s_len)
    nb = s_len // t
    hb = ATT_HEADS
    w = hb * HP

    def body(q_ref, k_ref, v_ref, do_ref, lse_ref, dl_ref, dq_ref, acc_sc):
        qi = pl.program_id(1)
        acc_sc[...] = jnp.zeros_like(acc_sc)

        def block(j, masked):
            ks = pl.ds(pl.multiple_of(j * t, t), t)
            for hh in range(hb):
                ls = slice(hh * HP, (hh + 1) * HP)
                kb = k_ref[ks, ls]
                s = lax.dot_general(q_ref[:, ls], kb, NT_DIMS, preferred_element_type=F32) * ATT_SCALE
                p = jnp.exp(s - lse_ref[hh])
                if masked:
                    r, c = _causal_mask(t)
                    p = jnp.where(c <= r, p, 0.0)
                dp = lax.dot_general(do_ref[:, ls], v_ref[ks, ls], NT_DIMS, preferred_element_type=F32)
                ds = p * (dp - dl_ref[hh]) * ATT_SCALE
                acc_sc[hh] += jnp.dot(ds.astype(BF16), kb, preferred_element_type=F32)

        def loop_body(j, carry):
            block(j, False)
            return carry

        lax.fori_loop(0, qi, loop_body, 0)
        block(qi, True)
        for hh in range(hb):
            dq_ref[:, hh * HP:(hh + 1) * HP] = acc_sc[hh].astype(dq_ref.dtype)

    q_spec = pl.BlockSpec((t, w), lambda h, qi: (qi, h))
    resident = pl.BlockSpec((s_len, w), lambda h, qi: (0, h))
    col_spec = pl.BlockSpec((hb, t, 1), lambda h, qi: (h, qi, 0))
    return pl.pallas_call(
        body,
        grid=(N_HEADS // hb, nb),
        in_specs=[q_spec, resident, resident, q_spec, col_spec, col_spec],
        out_specs=q_spec,
        out_shape=jax.ShapeDtypeStruct((s_len, HW), F32),
        scratch_shapes=[pltpu.VMEM((hb, t, HP), F32)],
        compiler_params=_cparams(("parallel", "arbitrary")),
        name="attention_bwd_dq",
    )(q, k, v, do, lse, delta)


def attention_bwd_dkv(q, k, v, do, lse_row, delta_row):
    s_len = q.shape[0]
    t = min(ATT_TILE_BWD, s_len)
    nb = s_len // t
    hb = ATT_HEADS
    w = hb * HP

    def body(q_ref, k_ref, v_ref, do_ref, lse_ref, dl_ref, dk_ref, dv_ref, dk_sc, dv_sc):
        ki = pl.program_id(1)
        dk_sc[...] = jnp.zeros_like(dk_sc)
        dv_sc[...] = jnp.zeros_like(dv_sc)

        def block(j, masked):
            qs = pl.ds(pl.multiple_of(j * t, t), t)
            for hh in range(hb):
                ls = slice(hh * HP, (hh + 1) * HP)
                qb = q_ref[qs, ls]
                dob = do_ref[qs, ls]
                st = lax.dot_general(k_ref[:, ls], qb, NT_DIMS, preferred_element_type=F32) * ATT_SCALE
                pt = jnp.exp(st - lse_ref[hh, j])
                if masked:
                    r, c = _causal_mask(t)
                    pt = jnp.where(r <= c, pt, 0.0)
                dv_sc[hh] += jnp.dot(pt.astype(BF16), dob, preferred_element_type=F32)
                dpt = lax.dot_general(v_ref[:, ls], dob, NT_DIMS, preferred_element_type=F32)
                dst = pt * (dpt - dl_ref[hh, j]) * ATT_SCALE
                dk_sc[hh] += jnp.dot(dst.astype(BF16), qb, preferred_element_type=F32)

        block(ki, True)

        def loop_body(j, carry):
            block(j, False)
            return carry

        lax.fori_loop(ki + 1, nb, loop_body, 0)
        for hh in range(hb):
            ls = slice(hh * HP, (hh + 1) * HP)
            dk_ref[:, ls] = dk_sc[hh].astype(dk_ref.dtype)
            dv_ref[:, ls] = dv_sc[hh].astype(dv_ref.dtype)

    k_spec = pl.BlockSpec((t, w), lambda h, ki: (ki, h))
    resident = pl.BlockSpec((s_len, w), lambda h, ki: (0, h))
    row_spec = pl.BlockSpec((hb, nb, 1, t), lambda h, ki: (h, 0, 0, 0))
    return pl.pallas_call(
        body,
        grid=(N_HEADS // hb, nb),
        in_specs=[resident, k_spec, k_spec, resident, row_spec, row_spec],
        out_specs=[k_spec, k_spec],
        out_shape=[jax.ShapeDtypeStruct((s_len, HW), F32), jax.ShapeDtypeStruct((s_len, HW), BF16)],
        scratch_shapes=[pltpu.VMEM((hb, t, HP), F32), pltpu.VMEM((hb, t, HP), F32)],
        compiler_params=_cparams(("parallel", "arbitrary")),
        name="attention_bwd_dkv",
    )(q, k, v, do, lse_row, delta_row)


CONV_PAD = 32


def conv_fwd(z, conv_w, conv_b):
    s_len = z.shape[0]
    ch = min(CONV_CHUNK, s_len)

    def body(ag_ref, w_ref, b_ref, c_ref, pad_ref):
        pad_ref[0:CONV_PAD, :] = jnp.zeros((CONV_PAD, LANE), F32)
        pad_ref[CONV_PAD:CONV_PAD + s_len, :] = ag_ref[:, 0:LANE] * jax.nn.sigmoid(ag_ref[:, LANE:2 * LANE])

        def chunk(i, carry):
            base = pl.multiple_of(i * ch, ch)
            acc = jnp.zeros((ch, LANE), F32) + b_ref[...]
            for kk in range(CONV_W):
                acc = acc + pad_ref[pl.ds(base + CONV_PAD - (CONV_W - 1) + kk, ch), :] * w_ref[kk:kk + 1, :]
            c_ref[pl.ds(base, ch), :] = acc
            return carry

        lax.fori_loop(0, s_len // ch, chunk, 0)

    nblk = CONV_C // LANE
    return pl.pallas_call(
        body,
        grid=(nblk,),
        in_specs=[pl.BlockSpec((s_len, 2 * LANE), lambda j: (0, ZC // (2 * LANE) + j)),
                  pl.BlockSpec((CONV_W, LANE), lambda j: (0, j)), pl.BlockSpec((1, LANE), lambda j: (0, j))],
        out_specs=pl.BlockSpec((s_len, LANE), lambda j: (0, j)),
        out_shape=jax.ShapeDtypeStruct((s_len, CONV_C), F32),
        scratch_shapes=[pltpu.VMEM((s_len + CONV_PAD, LANE), F32)],
        compiler_params=_cparams(("arbitrary",)),
        name="conv_fwd",
    )(z, conv_w, conv_b)


def conv_bwd(z, dc, conv_w, dz):
    s_len = z.shape[0]
    ch = min(CONV_CHUNK, s_len)

    def body(ag_ref, dc_ref, w_ref, dz_in, dag_ref, dw_ref, db_ref, pad_ref, dpad_ref, wacc_ref):
        del dz_in
        pad_ref[0:CONV_PAD, :] = jnp.zeros((CONV_PAD, LANE), F32)
        pad_ref[CONV_PAD:CONV_PAD + s_len, :] = ag_ref[:, 0:LANE] * jax.nn.sigmoid(ag_ref[:, LANE:2 * LANE])
        dpad_ref[0:s_len, :] = dc_ref[...]
        dpad_ref[s_len:s_len + CONV_PAD, :] = jnp.zeros((CONV_PAD, LANE), F32)
        wacc_ref[...] = jnp.zeros_like(wacc_ref)
        db_ref[...] = jnp.sum(dc_ref[...], axis=0, keepdims=True)

        def chunk(i, carry):
            base = pl.multiple_of(i * ch, ch)
            dcc = dpad_ref[pl.ds(base, ch), :]
            dh = jnp.zeros((ch, LANE), F32)
            for kk in range(CONV_W):
                dh = dh + dpad_ref[pl.ds(base + (CONV_W - 1) - kk, ch), :] * w_ref[kk:kk + 1, :]
                prod = dcc * pad_ref[pl.ds(base + CONV_PAD - (CONV_W - 1) + kk, ch), :]
                wacc_ref[kk * 8:(kk + 1) * 8, :] += prod.reshape(ch // 8, 8, LANE).sum(axis=0)
            a = ag_ref[pl.ds(base, ch), 0:LANE]
            sgc = jax.nn.sigmoid(ag_ref[pl.ds(base, ch), LANE:2 * LANE])
            dag_ref[pl.ds(base, ch), 0:LANE] = (dh * sgc).astype(dag_ref.dtype)
            dag_ref[pl.ds(base, ch), LANE:2 * LANE] = (dh * a * sgc * (1.0 - sgc)).astype(dag_ref.dtype)
            return carry

        lax.fori_loop(0, s_len // ch, chunk, 0)
        for kk in range(CONV_W):
            dw_ref[kk:kk + 1, :] = jnp.sum(wacc_ref[kk * 8:(kk + 1) * 8, :], axis=0, keepdims=True)

    nblk = CONV_C // LANE
    pair = pl.BlockSpec((s_len, 2 * LANE), lambda j: (0, ZC // (2 * LANE) + j))
    return pl.pallas_call(
        body,
        grid=(nblk,),
        in_specs=[pair, pl.BlockSpec((s_len, LANE), lambda j: (0, j)), pl.BlockSpec((CONV_W, LANE), lambda j: (0, j)), ANY],
        out_specs=[pair, pl.BlockSpec((CONV_W, LANE), lambda j: (0, j)), pl.BlockSpec((1, LANE), lambda j: (0, j))],
        out_shape=[jax.ShapeDtypeStruct(dz.shape, dz.dtype), jax.ShapeDtypeStruct((CONV_W, CONV_C), F32), jax.ShapeDtypeStruct((1, CONV_C), F32)],
        scratch_shapes=[pltpu.VMEM((s_len + CONV_PAD, LANE), F32), pltpu.VMEM((s_len + CONV_PAD, LANE), F32),
                        pltpu.VMEM((CONV_W * 8, LANE), F32)],
        input_output_aliases={3: 0},
        compiler_params=_cparams(("arbitrary",)),
        name="conv_bwd",
    )(z, dc, conv_w, dz)


POOL_PAD = 16


def _pool_count(base, ch, w):
    t = base + lax.broadcasted_iota(jnp.int32, (ch, 1), 0)
    return jnp.minimum(t + 1, w).astype(F32)


def pool_fwd(z, pool_w, pool_scale):
    s_len = z.shape[0]
    ch = min(CONV_CHUNK, s_len)

    def body(u_ref, pw_ref, sc_ref, m_ref, pad_ref):
        gi = pl.program_id(0)
        pad_ref[0:POOL_PAD, :] = jnp.zeros((POOL_PAD, LANE), F32)
        pad_ref[POOL_PAD:POOL_PAD + s_len, :] = u_ref[...]

        def run(w):
            def chunk(i, carry):
                base = pl.multiple_of(i * ch, ch)
                acc = jnp.zeros((ch, LANE), F32)
                for j in range(w):
                    acc = acc + pad_ref[pl.ds(base + POOL_PAD - j, ch), :]
                d = acc / _pool_count(base, ch, w) - u_ref[pl.ds(base, ch), :]
                md = jnp.dot(d.astype(BF16), pw_ref[0], preferred_element_type=F32)
                m_ref[pl.ds(base, ch), :] = (md * sc_ref[...]).astype(m_ref.dtype)
                return carry

            lax.fori_loop(0, s_len // ch, chunk, 0)

        for g, w in enumerate(POOL_WINDOWS):
            pl.when(gi == g)(functools.partial(run, w))

    return pl.pallas_call(
        body,
        grid=(POOL_G,),
        in_specs=[pl.BlockSpec((s_len, LANE), lambda g: (0, ZP // LANE + g)), pl.BlockSpec((1, POOL_GD, POOL_GD), lambda g: (g, 0, 0)),
                  pl.BlockSpec((1, LANE), lambda g: (0, g))],
        out_specs=pl.BlockSpec((s_len, LANE), lambda g: (0, g)),
        out_shape=jax.ShapeDtypeStruct((s_len, POOL_C), BF16),
        scratch_shapes=[pltpu.VMEM((s_len + POOL_PAD, LANE), F32)],
        compiler_params=_cparams(("arbitrary",)),
        name="pool_fwd",
    )(z, pool_w, pool_scale)


def pool_bwd(z, dm, pool_w, pool_scale, dz):
    s_len = z.shape[0]
    ch = min(CONV_CHUNK, s_len)

    def body(u_ref, dm_ref, pw_ref, sc_ref, dz_in, du_ref, dpw_ref, dsc_ref, pad_ref, epad_ref, dd_ref, sacc_ref):
        del dz_in
        gi = pl.program_id(0)
        pad_ref[0:POOL_PAD, :] = jnp.zeros((POOL_PAD, LANE), F32)
        pad_ref[POOL_PAD:POOL_PAD + s_len, :] = u_ref[...]
        epad_ref[s_len:s_len + POOL_PAD, :] = jnp.zeros((POOL_PAD, LANE), F32)
        dpw_ref[...] = jnp.zeros_like(dpw_ref)
        sacc_ref[...] = jnp.zeros_like(sacc_ref)

        def run(w):
            def first(i, carry):
                base = pl.multiple_of(i * ch, ch)
                acc = jnp.zeros((ch, LANE), F32)
                for j in range(w):
                    acc = acc + pad_ref[pl.ds(base + POOL_PAD - j, ch), :]
                cnt = _pool_count(base, ch, w)
                d = (acc / cnt - u_ref[pl.ds(base, ch), :]).astype(BF16)
                md = jnp.dot(d, pw_ref[0], preferred_element_type=F32)
                dmc = dm_ref[pl.ds(base, ch), :]
                sacc_ref[...] += (dmc * md).reshape(ch // 8, 8, LANE).sum(axis=0)
                dmd = (dmc * sc_ref[...]).astype(BF16)
                dpw_ref[0] += lax.dot_general(d, dmd, (((0,), (0,)), ((), ())), preferred_element_type=F32)
                dd = lax.dot_general(dmd, pw_ref[0], (((1,), (1,)), ((), ())), preferred_element_type=F32)
                dd_ref[pl.ds(base, ch), :] = dd
                epad_ref[pl.ds(base, ch), :] = dd / cnt
                return carry

            lax.fori_loop(0, s_len // ch, first, 0)

            def second(i, carry):
                base = pl.multiple_of(i * ch, ch)
                acc = jnp.zeros((ch, LANE), F32)
                for j in range(w):
                    acc = acc + epad_ref[pl.ds(base + j, ch), :]
                du_ref[pl.ds(base, ch), :] = (acc - dd_ref[pl.ds(base, ch), :]).astype(du_ref.dtype)
                return carry

            lax.fori_loop(0, s_len // ch, second, 0)

        for g, w in enumerate(POOL_WINDOWS):
            pl.when(gi == g)(functools.partial(run, w))
        dsc_ref[...] = jnp.sum(sacc_ref[...], axis=0, keepdims=True)

    return pl.pallas_call(
        body,
        grid=(POOL_G,),
        in_specs=[pl.BlockSpec((s_len, LANE), lambda g: (0, ZP // LANE + g)), pl.BlockSpec((s_len, LANE), lambda g: (0, g)),
                  pl.BlockSpec((1, POOL_GD, POOL_GD), lambda g: (g, 0, 0)), pl.BlockSpec((1, LANE), lambda g: (0, g)), ANY],
        out_specs=[pl.BlockSpec((s_len, LANE), lambda g: (0, ZP // LANE + g)), pl.BlockSpec((1, POOL_GD, POOL_GD), lambda g: (g, 0, 0)),
                   pl.BlockSpec((1, LANE), lambda g: (0, g))],
        out_shape=[jax.ShapeDtypeStruct(dz.shape, dz.dtype), jax.ShapeDtypeStruct((POOL_G, POOL_GD, POOL_GD), F32),
                   jax.ShapeDtypeStruct((1, POOL_C), F32)],
        scratch_shapes=[pltpu.VMEM((s_len + POOL_PAD, LANE), F32), pltpu.VMEM((s_len + POOL_PAD, LANE), F32),
                        pltpu.VMEM((s_len, LANE), F32), pltpu.VMEM((8, LANE), F32)],
        input_output_aliases={4: 0},
        compiler_params=_cparams(("arbitrary",)),
        name="pool_bwd",
    )(z, dm, pool_w, pool_scale, dz)


def _row(v):
    return v.reshape(1, -1)


def layer_fwd(x, tabs, w, tag):
    s_len = x.shape[0]
    cc, sa, sb = tabs
    sv = {"x": x}

    def rms_body(x_ref, g_ref, o_ref):
        o_ref[...] = _rms(x_ref[...], g_ref[...]).astype(o_ref.dtype)

    (h,) = rowwise("mix_norm_pre" + tag, rms_body, s_len, [_whole(x)], [_row(w["mix_norm_pre"])], [(D_MODEL, BF16)])
    z = mm(h, w["w_in"], name="in_proj" + tag)

    def prep_body(z_ref, cc_ref, sa_ref, sb_ref, qg_ref, kg_ref, qn_ref, ckv_ref, kr_ref):
        qn_ref[...] = _rms(z_ref[:, 0:Q_RANK], qg_ref[...]).astype(qn_ref.dtype)
        ckv_ref[...] = _rms(z_ref[:, Q_RANK:Q_RANK + KV_RANK], kg_ref[...]).astype(ckv_ref.dtype)
        kr_ref[...] = _rope(z_ref[:, Q_RANK + KV_RANK:ZA_W], cc_ref[...], sa_ref[...], sb_ref[...])

    qn, ckvn, kr = rowwise("attn_prep" + tag, prep_body, s_len, [(z, ZA_W, ZA // ZA_W), _whole(cc), _whole(sa), _whole(sb)],
                           [_row(w["q_norm"]), _row(w["kv_norm"])], [(Q_RANK, BF16), (KV_RANK, BF16), (HP, F32)])
    q_raw = mm(qn, w["w_uq"], name="q_proj" + tag)
    kv_raw = mm(ckvn, w["w_ukv"], name="kv_proj" + tag)

    def qkv_body(q_ref, kv_ref, kr_ref, cc_ref, sa_ref, sb_ref, qo_ref, ko_ref, vo_ref):
        c_, a_, b_, kro = cc_ref[...], sa_ref[...], sb_ref[...], kr_ref[...]
        for hh in range(N_HEADS):
            sl = slice(hh * HP, (hh + 1) * HP)
            qo_ref[:, sl] = _rope(q_ref[:, sl], c_, a_, b_).astype(qo_ref.dtype)
            ko_ref[:, sl] = (kv_ref[:, sl] + kro).astype(ko_ref.dtype)
        lane = lax.broadcasted_iota(jnp.int32, (q_ref.shape[0], HW), 1)
        vo_ref[...] = jnp.where((lane & (HP - 1)) == VDIM, 1.0, kv_ref[:, HW:2 * HW]).astype(vo_ref.dtype)

    q, k, v = rowwise("qkv_rope" + tag, qkv_body, s_len, [_whole(q_raw), _whole(kv_raw), _whole(kr), _whole(cc), _whole(sa), _whole(sb)], [],
                      [(HW, BF16)] * 3)
    o, lse = attention_fwd(q, k, v)
    y_attn = mm(o, w["w_attn_o"], name="attn_out" + tag)

    c = conv_fwd(z, w["conv_w"], _row(w["conv_b"]))

    def ln_body(c_ref, g_ref, b_ref, o_ref):
        o_ref[...] = _silu(_layer_norm(c_ref[...], g_ref[...], b_ref[...])).astype(o_ref.dtype)

    (cs,) = rowwise("conv_ln_silu" + tag, ln_body, s_len, [_whole(c)], [_row(w["conv_ln_g"]), _row(w["conv_ln_b"])], [(CONV_C, BF16)])
    y_conv = mm(cs, w["w_conv_o"], name="conv_out" + tag)

    m = pool_fwd(z, w["pool_w"], _row(w["pool_scale"]))
    y_pool = mm(m, w["w_pool_o"], name="pool_out" + tag)

    def merge_body(ya_ref, yc_ref, yp_ref, gl_ref, o_ref):
        gl = gl_ref[...]
        o_ref[...] = (jax.nn.sigmoid(gl[:, 0:D_MODEL]) * ya_ref[...] + jax.nn.sigmoid(gl[:, D_MODEL:2 * D_MODEL]) * yc_ref[...]
                      + jax.nn.sigmoid(gl[:, 2 * D_MODEL:3 * D_MODEL]) * yp_ref[...]).astype(o_ref.dtype)

    (merged,) = rowwise("gate_merge" + tag, merge_body, s_len, [_whole(y_attn), _whole(y_conv), _whole(y_pool), (z, 3 * D_MODEL, 0)], [],
                        [(D_MODEL, BF16)])
    mo = mm(merged, w["w_mix_o"], name="mix_out" + tag)

    def post_body(y_ref, x_ref, g_ref, o_ref):
        o_ref[...] = x_ref[...] + _rms(y_ref[...], g_ref[...])

    (h1,) = rowwise("mix_norm_post" + tag, post_body, s_len, [_whole(mo), _whole(x)], [_row(w["mix_norm_post"])], [(D_MODEL, F32)])

    (hn,) = rowwise("ffn_norm_pre" + tag, rms_body, s_len, [_whole(h1)], [_row(w["ffn_norm_pre"])], [(D_MODEL, BF16)])
    gu = mm(hn, w["w_gu"], name="ffn_in" + tag)

    def swiglu_body(gu_ref, o_ref):
        o_ref[...] = (_silu(gu_ref[:, 0:D_FF]) * gu_ref[:, D_FF:2 * D_FF]).astype(o_ref.dtype)

    (act,) = rowwise("swiglu" + tag, swiglu_body, s_len, [_whole(gu)], [], [(D_FF, BF16)])
    y = mm(act, w["w_down"], name="ffn_out" + tag)
    (h2,) = rowwise("ffn_norm_post" + tag, post_body, s_len, [_whole(y), _whole(h1)], [_row(w["ffn_norm_post"])], [(D_MODEL, F32)])

    sv.update(h=h, z=z, qn=qn, ckvn=ckvn, q=q, k=k, v=v, o=o, lse=lse, c=c, cs=cs, m=m, y_attn=y_attn, y_conv=y_conv, y_pool=y_pool,
              merged=merged, mo=mo, h1=h1, hn=hn, gu=gu, act=act, y=y)
    return h2, sv


def layer_bwd(dh2, sv, tabs, w, tag):
    s_len = dh2.shape[0]
    cc, sa, sb = tabs
    g = {}

    def post_bwd_body(y_ref, dh_ref, g_ref, dy_ref, dg_ref):
        _, vjp = jax.vjp(_rms, y_ref[...], g_ref[...])
        dy, dg = vjp(dh_ref[...])
        dy_ref[...] = dy.astype(dy_ref.dtype)
        _acc(dg_ref, dg)

    def pre_bwd_body(x_ref, dhn_ref, dres_ref, g_ref, dx_ref, dg_ref):
        _, vjp = jax.vjp(_rms, x_ref[...], g_ref[...])
        dx, dg = vjp(dhn_ref[...])
        dx_ref[...] = dres_ref[...] + dx
        _acc(dg_ref, dg)

    d_y, g["ffn_norm_post"] = rowwise("ffn_norm_post_bwd" + tag, post_bwd_body, s_len, [_whole(sv["y"]), _whole(dh2)],
                                      [_row(w["ffn_norm_post"])], [(D_MODEL, BF16)], [((1, D_MODEL), F32)])
    g["w_down"] = mm(sv["act"], d_y, ta=True, name="ffn_out_dw" + tag)
    d_act = mm(d_y, w["w_down"], tb=True, name="ffn_out_dx" + tag)

    def swiglu_bwd_body(gu_ref, da_ref, dgu_ref):
        f = lambda gt, up: _silu(gt) * up
        _, vjp = jax.vjp(f, gu_ref[:, 0:D_FF], gu_ref[:, D_FF:2 * D_FF])
        dgt, dup = vjp(da_ref[...])
        dgu_ref[:, 0:D_FF] = dgt.astype(dgu_ref.dtype)
        dgu_ref[:, D_FF:2 * D_FF] = dup.astype(dgu_ref.dtype)

    (d_gu,) = rowwise("swiglu_bwd" + tag, swiglu_bwd_body, s_len, [_whole(sv["gu"]), _whole(d_act)], [], [(2 * D_FF, BF16)])
    g["w_gu"] = mm(sv["hn"], d_gu, ta=True, name="ffn_in_dw" + tag)
    d_hn = mm(d_gu, w["w_gu"], tb=True, name="ffn_in_dx" + tag)
    dh1, g["ffn_norm_pre"] = rowwise(
        "ffn_norm_pre_bwd" + tag, pre_bwd_body, s_len, [_whole(sv["h1"]), _whole(d_hn), _whole(dh2)], [_row(w["ffn_norm_pre"])], [(D_MODEL, F32)], [((1, D_MODEL), F32)])

    d_mo, g["mix_norm_post"] = rowwise(
        "mix_norm_post_bwd" + tag, post_bwd_body, s_len, [_whole(sv["mo"]), _whole(dh1)], [_row(w["mix_norm_post"])], [(D_MODEL, BF16)], [((1, D_MODEL), F32)])
    g["w_mix_o"] = mm(sv["merged"], d_mo, ta=True, name="mix_out_dw" + tag)
    d_merged = mm(d_mo, w["w_mix_o"], tb=True, name="mix_out_dx" + tag)

    def merge_bwd_body(dm_ref, ya_ref, yc_ref, yp_ref, gl_ref, dya_ref, dyc_ref, dyp_ref, dgl_ref):
        dmg = dm_ref[...]
        for i, (y_ref, dy_ref) in enumerate(((ya_ref, dya_ref), (yc_ref, dyc_ref), (yp_ref, dyp_ref))):
            sg = jax.nn.sigmoid(gl_ref[:, i * D_MODEL:(i + 1) * D_MODEL])
            dy_ref[...] = (dmg * sg).astype(dy_ref.dtype)
            dgl_ref[:, i * D_MODEL:(i + 1) * D_MODEL] = (dmg * y_ref[...] * sg * (1.0 - sg)).astype(dgl_ref.dtype)

    d_ya, d_yc, d_yp, dz = rowwise(
        "gate_merge_bwd" + tag, merge_bwd_body, s_len,
        [_whole(d_merged), _whole(sv["y_attn"]), _whole(sv["y_conv"]), _whole(sv["y_pool"]), (sv["z"], 3 * D_MODEL, 0)], [],
        [(D_MODEL, BF16)] * 3 + [(3 * D_MODEL, BF16, ZW, 0)])

    g["w_pool_o"] = mm(sv["m"], d_yp, ta=True, name="pool_out_dw" + tag)
    d_m = mm(d_yp, w["w_pool_o"], tb=True, name="pool_out_dx" + tag)
    dz, g["pool_w"], g["pool_scale"] = pool_bwd(sv["z"], d_m, w["pool_w"], _row(w["pool_scale"]), dz)

    g["w_conv_o"] = mm(sv["cs"], d_yc, ta=True, name="conv_out_dw" + tag)
    d_cs = mm(d_yc, w["w_conv_o"], tb=True, name="conv_out_dx" + tag)

    def ln_bwd_body(c_ref, dcs_ref, g_ref, b_ref, dc_ref, dg_ref, db_ref):
        f = lambda c_, g_, b_: _silu(_layer_norm(c_, g_, b_))
        _, vjp = jax.vjp(f, c_ref[...], g_ref[...], b_ref[...])
        dc, dg, db = vjp(dcs_ref[...])
        dc_ref[...] = dc
        _acc(dg_ref, dg)
        _acc(db_ref, db)

    d_c, g["conv_ln_g"], g["conv_ln_b"] = rowwise("conv_ln_silu_bwd" + tag, ln_bwd_body, s_len, [_whole(sv["c"]), _whole(d_cs)],
                                                  [_row(w["conv_ln_g"]), _row(w["conv_ln_b"])], [(CONV_C, F32)],
                                                  [((1, CONV_C), F32), ((1, CONV_C), F32)])
    dz, g["conv_w"], g["conv_b"] = conv_bwd(sv["z"], d_c, w["conv_w"], dz)

    g["w_attn_o"] = mm(sv["o"], d_ya, ta=True, name="attn_out_dw" + tag)
    d_o = mm(d_ya, w["w_attn_o"], tb=True, out_dtype=BF16, name="attn_out_dx" + tag)
    delta = attention_delta(d_o, sv["o"])
    dq = attention_bwd_dq(sv["q"], sv["k"], sv["v"], d_o, sv["lse"], delta)
    t_bwd = min(ATT_TILE_BWD, s_len)
    rows_of = lambda a: a.reshape(N_HEADS, s_len // t_bwd, 1, t_bwd)
    dk, dv = attention_bwd_dkv(sv["q"], sv["k"], sv["v"], d_o, rows_of(sv["lse"]), rows_of(delta))

    def qkv_bwd_body(dq_ref, dk_ref, dv_ref, cc_ref, sa_ref, sb_ref, dqp_ref, dkv_ref, dkr_ref):
        c_, a_, b_ = cc_ref[...], sa_ref[...], sb_ref[...]
        dk_sum = jnp.zeros((dq_ref.shape[0], HP), F32)
        for hh in range(N_HEADS):
            sl = slice(hh * HP, (hh + 1) * HP)
            dqp_ref[:, sl] = _rope_t(dq_ref[:, sl], c_, a_, b_).astype(dqp_ref.dtype)
            dkh = dk_ref[:, sl]
            dkv_ref[:, sl] = dkh.astype(dkv_ref.dtype)
            dk_sum = dk_sum + dkh
        dkv_ref[:, HW:2 * HW] = dv_ref[...]
        dkr_ref[...] = _rope_t(dk_sum, c_, a_, b_)

    dq_pre, dkv_pre, d_kr = rowwise("qkv_rope_bwd" + tag, qkv_bwd_body, s_len,
                                    [_whole(dq), _whole(dk), _whole(dv), _whole(cc), _whole(sa), _whole(sb)], [],
                                    [(HW, BF16), (2 * HW, BF16), (HP, F32)])
    g["w_uq"] = mm(sv["qn"], dq_pre, ta=True, name="q_proj_dw" + tag)
    d_qn = mm(dq_pre, w["w_uq"], tb=True, name="q_proj_dx" + tag)
    g["w_ukv"] = mm(sv["ckvn"], dkv_pre, ta=True, name="kv_proj_dw" + tag)
    d_ckvn = mm(dkv_pre, w["w_ukv"], tb=True, name="kv_proj_dx" + tag)

    def prep_bwd_body(z_ref, dqn_ref, dckv_ref, dkr_ref, qg_ref, kg_ref, dz_ref, dqg_ref, dkg_ref):
        _, vq = jax.vjp(_rms, z_ref[:, 0:Q_RANK], qg_ref[...])
        dcq, dqg = vq(dqn_ref[...])
        _, vk = jax.vjp(_rms, z_ref[:, Q_RANK:Q_RANK + KV_RANK], kg_ref[...])
        dckv, dkg = vk(dckv_ref[...])
        dz_ref[:, 0:Q_RANK] = dcq.astype(dz_ref.dtype)
        dz_ref[:, Q_RANK:Q_RANK + KV_RANK] = dckv.astype(dz_ref.dtype)
        dz_ref[:, Q_RANK + KV_RANK:ZA_W] = dkr_ref[...].astype(dz_ref.dtype)
        _acc(dqg_ref, dqg)
        _acc(dkg_ref, dkg)

    dz, g["q_norm"], g["kv_norm"] = rowwise("attn_prep_bwd" + tag, prep_bwd_body, s_len,
                                            [(sv["z"], ZA_W, ZA // ZA_W), _whole(d_qn), _whole(d_ckvn), _whole(d_kr)],
                                            [_row(w["q_norm"]), _row(w["kv_norm"])], [(ZA_W, BF16, ZW, ZA // ZA_W)],
                                            [((1, Q_RANK), F32), ((1, KV_RANK), F32)], into={0: dz})

    g["w_in"] = mm(sv["h"], dz, ta=True, name="in_proj_dw" + tag)
    d_h = mm(dz, w["w_in"], tb=True, name="in_proj_dx" + tag)
    dx, g["mix_norm_pre"] = rowwise(
        "mix_norm_pre_bwd" + tag, pre_bwd_body, s_len, [_whole(sv["x"]), _whole(d_h), _whole(dh1)], [_row(w["mix_norm_pre"])], [(D_MODEL, F32)], [((1, D_MODEL), F32)])
    return dx, g


def loss_head(h, target):
    s_len = h.shape[0]

    def body(h_ref, t_ref, dy_ref, loss_ref):
        err = h_ref[...] - t_ref[...]
        dy_ref[...] = err * (1.0 / D_MODEL)
        part = 0.5 * jnp.sum(jnp.mean(err * err, axis=-1, keepdims=True), axis=0, keepdims=True)
        _acc(loss_ref, jnp.broadcast_to(part, (1, LANE)))

    return rowwise("loss_head", body, s_len, [_whole(h), _whole(target)], [], [(D_MODEL, F32)], [((1, LANE), F32)])


def local_step(x, pos_col, target, layers):
    s_len = x.shape[0]
    tabs = rope_tables(pos_col, s_len)
    h, saved = x, []
    for li, w in enumerate(layers):
        h, sv = layer_fwd(h, tabs, w, f"_l{li}")
        saved.append(sv)
    dh, loss = loss_head(h, target)
    grads = [None] * len(layers)
    for li in reversed(range(len(layers))):
        dh, grads[li] = layer_bwd(dh, saved[li], tabs, layers[li], f"_l{li}")
    return loss[0, 0], dh, grads


def _pad_heads_cols(wm, per_head):
    r = wm.shape[0]
    return jnp.pad(wm.reshape(r, N_HEADS, per_head), ((0, 0), (0, 0), (0, HP - per_head))).reshape(r, HW)


def _unpad_heads_cols(wm, per_head):
    r = wm.shape[0]
    return wm.reshape(r, N_HEADS, HP)[:, :, :per_head].reshape(r, N_HEADS * per_head)


def align_weights(p):
    w_in = p["w_in"]
    r = w_in.shape[0]
    zeros = lambda n: jnp.zeros((r, n), w_in.dtype)
    conv = w_in[:, O_CONV:O_POOL].reshape(r, 2, CONV_C // LANE, LANE).transpose(0, 2, 1, 3).reshape(r, 2 * CONV_C)
    w_in_al = jnp.concatenate([
        w_in[:, O_GATE:D_IN], conv, w_in[:, O_POOL:O_GATE], w_in[:, O_Q:O_KR],
        zeros(KR_LANE), w_in[:, O_KR:O_CONV], zeros(HP - KR_LANE - ROPE)], axis=1)
    out = dict(p)
    out["w_in"] = w_in_al
    out["w_uq"] = _pad_heads_cols(p["w_uq"], NOPE + ROPE)
    out["w_ukv"] = jnp.concatenate([_pad_heads_cols(p["w_uk"], NOPE), _pad_heads_cols(p["w_uv"], VDIM)], axis=1)
    wo = p["w_attn_o"]
    out["w_attn_o"] = jnp.pad(wo.reshape(N_HEADS, VDIM, D_MODEL), ((0, 0), (0, HP - VDIM), (0, 0))).reshape(HW, D_MODEL)
    out["w_gu"] = jnp.concatenate([p["w_gate"], p["w_up"]], axis=1)
    for name in ("w_uk", "w_uv", "w_gate", "w_up"):
        del out[name]
    return out


def unalign_grads(g):
    gi = g["w_in"]
    kr0 = ZA + Q_RANK + KV_RANK + KR_LANE
    out = dict(g)
    r = gi.shape[0]
    conv = gi[:, ZC:ZP].reshape(r, CONV_C // LANE, 2, LANE).transpose(0, 2, 1, 3).reshape(r, 2 * CONV_C)
    out["w_in"] = jnp.concatenate([gi[:, ZA:ZA + Q_RANK + KV_RANK], gi[:, kr0:kr0 + ROPE], conv, gi[:, ZP:ZA], gi[:, ZG:ZC]], axis=1)
    out["w_uq"] = _unpad_heads_cols(g["w_uq"], NOPE + ROPE)
    out["w_uk"] = _unpad_heads_cols(g["w_ukv"][:, :HW], NOPE)
    out["w_uv"] = _unpad_heads_cols(g["w_ukv"][:, HW:], VDIM)
    out["w_attn_o"] = g["w_attn_o"].reshape(N_HEADS, HP, D_MODEL)[:, :VDIM].reshape(N_HEADS * VDIM, D_MODEL)
    out["w_gate"] = g["w_gu"][:, :D_FF]
    out["w_up"] = g["w_gu"][:, D_FF:]
    del out["w_ukv"], out["w_gu"]
    return out


MESH = pl.DeviceIdType.MESH
ANY = pl.BlockSpec(memory_space=pl.ANY)


def _place():
    return lax.axis_index("x"), lax.axis_index("y"), lax.axis_index("c")


def _other_chips(x, y):
    return [(1 - x, y), (x, 1 - y), (1 - x, 1 - y)]


def gather_shards(local):
    n = len(local)

    def body(*refs):
        w_refs, out_refs, (send_sems, recv_sems) = refs[:n], refs[n:2 * n], refs[2 * n:]
        x, y, c = _place()
        me = 2 * x + y
        chips = _other_chips(x, y)
        slots = [2 * cx + cy for cx, cy in chips]
        sibling = (x, y, 1 - c)

        def copy(i, k, slot, layer, to, src=None):
            dst = out_refs[i].at[slot, layer]
            return pltpu.make_async_remote_copy(src_ref=dst if src is None else src, dst_ref=dst, send_sem=send_sems.at[6 * i + k],
                                                recv_sem=recv_sems.at[6 * i + k], device_id=to, device_id_type=MESH)

        first = [copy(i, j, me, c, (*chip, c), src=w_refs[i].at[c]) for i in range(n) for j, chip in enumerate(chips)]
        for cp in first:
            cp.start()
        passed = []
        for i in range(n):
            for j in range(3):
                copy(i, j, slots[j], c, sibling).wait_recv()
                fwd = copy(i, 3 + j, slots[j], c, sibling)
                fwd.start()
                passed.append(fwd)
        for i in range(n):
            for j in range(3):
                copy(i, 3 + j, slots[j], 1 - c, sibling).wait_recv()
        for cp in first + passed:
            cp.wait_send()

    return pl.pallas_call(
        body,
        in_specs=[ANY] * n,
        out_specs=[ANY] * n,
        out_shape=[jax.ShapeDtypeStruct((N_CHIPS, *a.shape), a.dtype) for a in local],
        scratch_shapes=[pltpu.SemaphoreType.DMA((6 * n,)), pltpu.SemaphoreType.DMA((6 * n,))],
        name="gather_shards",
    )(*local)


def sibling_swap(gs):
    n = len(gs)

    def body(*refs):
        g_refs, out_refs, (send_sems, recv_sems) = refs[:n], refs[n:2 * n], refs[2 * n:]
        x, y, c = _place()
        copies = []
        for i in range(n):
            for j in range(N_CHIPS):
                cp = pltpu.make_async_remote_copy(src_ref=g_refs[i].at[j, 1 - c], dst_ref=out_refs[i].at[j], send_sem=send_sems.at[4 * i + j],
                                                  recv_sem=recv_sems.at[4 * i + j], device_id=(x, y, 1 - c), device_id_type=MESH)
                cp.start()
                copies.append(cp)
        for cp in copies:
            cp.wait_recv()
        for cp in copies:
            cp.wait_send()

    return pl.pallas_call(
        body,
        in_specs=[ANY] * n,
        out_specs=[ANY] * n,
        out_shape=[jax.ShapeDtypeStruct((N_CHIPS, *a.shape[2:]), a.dtype) for a in gs],
        scratch_shapes=[pltpu.SemaphoreType.DMA((4 * n,)), pltpu.SemaphoreType.DMA((4 * n,))],
        name="sibling_swap",
    )(*gs)


def chip_exchange(ps):
    n = len(ps)

    def body(*refs):
        p_refs, out_refs, (send_sems, recv_sems) = refs[:n], refs[n:2 * n], refs[2 * n:]
        x, y, c = _place()
        chips = _other_chips(x, y)
        copies = []
        for i in range(n):
            for j, chip in enumerate(chips):
                cp = pltpu.make_async_remote_copy(src_ref=p_refs[i].at[2 * chip[0] + chip[1]], dst_ref=out_refs[i].at[j],
                                                  send_sem=send_sems.at[3 * i + j], recv_sem=recv_sems.at[3 * i + j],
                                                  device_id=(*chip, c), device_id_type=MESH)
                cp.start()
                copies.append(cp)
        for cp in copies:
            cp.wait_recv()
        for cp in copies:
            cp.wait_send()

    return pl.pallas_call(
        body,
        in_specs=[ANY] * n,
        out_specs=[ANY] * n,
        out_shape=[jax.ShapeDtypeStruct((3, *a.shape[1:]), a.dtype) for a in ps],
        scratch_shapes=[pltpu.SemaphoreType.DMA((3 * n,)), pltpu.SemaphoreType.DMA((3 * n,))],
        name="chip_exchange",
    )(*ps)


def sibling_gather(fs):
    n = len(fs)

    def body(*refs):
        out_refs, (send_sems, recv_sems) = refs[n:2 * n], refs[2 * n:]
        x, y, c = _place()
        copies = []
        for i in range(n):
            cp = pltpu.make_async_remote_copy(src_ref=out_refs[i].at[c], dst_ref=out_refs[i].at[c], send_sem=send_sems.at[i],
                                              recv_sem=recv_sems.at[i], device_id=(x, y, 1 - c), device_id_type=MESH)
            cp.start()
            copies.append(cp)
        for i in range(n):
            pltpu.make_async_remote_copy(src_ref=out_refs[i].at[1 - c], dst_ref=out_refs[i].at[1 - c], send_sem=send_sems.at[i],
                                         recv_sem=recv_sems.at[i], device_id=(x, y, 1 - c), device_id_type=MESH).wait_recv()
        for cp in copies:
            cp.wait_send()

    return pl.pallas_call(
        body,
        in_specs=[ANY] * n,
        out_specs=[ANY] * n,
        out_shape=[jax.ShapeDtypeStruct(a.shape, a.dtype) for a in fs],
        scratch_shapes=[pltpu.SemaphoreType.DMA((n,)), pltpu.SemaphoreType.DMA((n,))],
        input_output_aliases={i: i for i in range(n)},
        name="sibling_gather",
    )(*fs)


def gather_all(vs):
    n = len(vs)

    def body(*refs):
        v_refs, out_refs, (send_sems, recv_sems, local_sems) = refs[:n], refs[n:2 * n], refs[2 * n:]
        x, y, c = _place()
        me, sibling = (x, y, c), (x, y, 1 - c)
        chips = _other_chips(x, y)

        def copy(i, k, block, to, src=None):
            px, py, pc = block
            dst = out_refs[i].at[4 * px + 2 * py + pc]
            return pltpu.make_async_remote_copy(src_ref=dst if src is None else src, dst_ref=dst, send_sem=send_sems.at[7 * i + k],
                                                recv_sem=recv_sems.at[7 * i + k], device_id=to, device_id_type=MESH)

        mine = [pltpu.make_async_copy(v_refs[i], out_refs[i].at[4 * x + 2 * y + c], local_sems.at[i]) for i in range(n)]
        for cp in mine:
            cp.start()
        first = []
        for i in range(n):
            first.append(copy(i, 0, me, sibling, src=v_refs[i]))
            first += [copy(i, 1 + j, me, (*chip, c), src=v_refs[i]) for j, chip in enumerate(chips)]
        for cp in first:
            cp.start()
        passed = []
        for i in range(n):
            for j, chip in enumerate(chips):
                copy(i, 1 + j, (*chip, c), me).wait_recv()
                fwd = copy(i, 4 + j, (*chip, c), sibling)
                fwd.start()
                passed.append(fwd)
        for i in range(n):
            copy(i, 0, sibling, me).wait_recv()
            for j, chip in enumerate(chips):
                copy(i, 4 + j, (*chip, 1 - c), me).wait_recv()
        for cp in first + passed:
            cp.wait_send()
        for cp in mine:
            cp.wait()

    return pl.pallas_call(
        body,
        in_specs=[ANY] * n,
        out_specs=[ANY] * n,
        out_shape=[jax.ShapeDtypeStruct((8, *a.shape), a.dtype) for a in vs],
        scratch_shapes=[pltpu.SemaphoreType.DMA((7 * n,)), pltpu.SemaphoreType.DMA((7 * n,)), pltpu.SemaphoreType.DMA((n,))],
        name="gather_all",
    )(*vs)


def _row_tile(rows, row_bytes):
    best = None
    for t in range(16, rows + 1, 16):
        if rows % t == 0 and t * row_bytes <= SUM_TILE_BYTES:
            best = t
    return best or rows


def sibling_sum(g, theirs, place, name):
    _, _, rows, cols = g.shape
    tile = _row_tile(rows, cols * 4)

    def body(place_ref, g_ref, t_ref, o_ref):
        o_ref[...] = (g_ref[0].astype(F32) + t_ref[...].astype(F32)).astype(o_ref.dtype)

    spec = pl.BlockSpec((1, tile, cols), lambda j, i, place_ref: (j, i, 0))
    return pl.pallas_call(
        body,
        grid_spec=pltpu.PrefetchScalarGridSpec(
            num_scalar_prefetch=1, grid=(N_CHIPS, rows // tile),
            in_specs=[pl.BlockSpec((1, 1, tile, cols), lambda j, i, place_ref: (j, place_ref[1], i, 0)), spec], out_specs=spec),
        out_shape=jax.ShapeDtypeStruct(theirs.shape, BF16),
        compiler_params=_cparams(("parallel", "parallel")),
        name=name,
    )(place, g, theirs)


def chip_sum(p, others, place, name):
    _, rows, cols = p.shape
    tile = _row_tile(rows, cols * 4)

    def body(place_ref, p_ref, o3_ref, o_ref):
        acc = p_ref[0].astype(F32)
        for k in range(3):
            acc = acc + o3_ref[k].astype(F32)
        o_ref[0] = acc

    return pl.pallas_call(
        body,
        grid_spec=pltpu.PrefetchScalarGridSpec(
            num_scalar_prefetch=1, grid=(rows // tile,),
            in_specs=[pl.BlockSpec((1, tile, cols), lambda i, place_ref: (place_ref[0], i, 0)),
                      pl.BlockSpec((3, tile, cols), lambda i, place_ref: (0, i, 0))],
            out_specs=pl.BlockSpec((1, tile, cols), lambda i, place_ref: (place_ref[1], i, 0))),
        out_shape=jax.ShapeDtypeStruct((2, rows, cols), F32),
        compiler_params=_cparams(("parallel",)),
        name=name,
    )(place, p, others)


def sum_devices(a, name):
    n, rows, cols = a.shape
    tile = _row_tile(rows, cols * 4 * n)

    def body(a_ref, o_ref):
        acc = a_ref[0]
        for s in range(1, n):
            acc = acc + a_ref[s]
        o_ref[...] = acc

    return pl.pallas_call(body, grid=(rows // tile,), in_specs=[pl.BlockSpec((n, tile, cols), lambda i: (0, i, 0))],
                          out_specs=pl.BlockSpec((tile, cols), lambda i: (i, 0)), out_shape=jax.ShapeDtypeStruct((rows, cols), F32),
                          compiler_params=_cparams(("parallel",)), name=name)(a)


def adamw(w, g, m, v, name):
    rows, cols = w.shape
    tile = rows
    for t in (512, 256, 128, 64, 32, 16, 8):
        if rows % t == 0 and t * cols * 4 <= 2 * 1024 * 1024:
            tile = t
            break

    def body(w_ref, g_ref, m_ref, v_ref, d_ref, mo_ref, vo_ref):
        gg = g_ref[...]
        m_new = ADAM_B1 * m_ref[...] + (1.0 - ADAM_B1) * gg
        v_new = ADAM_B2 * v_ref[...] + (1.0 - ADAM_B2) * (gg * gg)
        m_hat = m_new / (1.0 - ADAM_B1 ** ADAM_STEP)
        v_hat = v_new / (1.0 - ADAM_B2 ** ADAM_STEP)
        d_ref[...] = -ADAM_LR * (m_hat / (jnp.sqrt(v_hat) + ADAM_EPS) + ADAM_WD * w_ref[...])
        mo_ref[...] = m_new
        vo_ref[...] = v_new

    spec = pl.BlockSpec((tile, cols), lambda i: (i, 0))
    shape = jax.ShapeDtypeStruct((rows, cols), F32)
    return pl.pallas_call(body, grid=(rows // tile,), in_specs=[spec] * 4, out_specs=[spec] * 3, out_shape=[shape] * 3,
                          compiler_params=_cparams(("parallel",)), name=name)(w, g, m, v)


WEIGHTS = ["mix_norm_pre", "w_in", "q_norm", "w_uq", "kv_norm", "w_uk", "w_uv", "w_attn_o", "conv_w", "conv_b", "conv_ln_g", "conv_ln_b",
           "w_conv_o", "pool_w", "pool_scale", "w_pool_o", "w_mix_o", "mix_norm_post", "ffn_norm_pre", "w_gate", "w_up", "w_down",
           "ffn_norm_post"]
SHARDED = {"w_in": 2, "w_uq": 2, "w_uk": 2, "w_uv": 2, "w_attn_o": 2, "conv_w": 2, "w_conv_o": 2, "w_pool_o": 2, "w_mix_o": 1,
           "w_gate": 2, "w_up": 2, "w_down": 1}
REPLICATED = [n for n in WEIGHTS if n not in SHARDED]
ROW_PARAMS = [n for n in REPLICATED if n != "pool_w"]
N_CHIPS = 4
GROUPS = [(("w_in",), 1), (("w_uq",), 1), (("w_uk", "w_uv"), 1), (("w_attn_o", "w_conv_o", "w_pool_o"), 1), (("conv_w",), 1),
          (("w_mix_o",), 1), (("w_gate", "w_up"), 2), (("w_down",), 1)]


def _join(parts, axis):
    return parts[0] if len(parts) == 1 else jnp.concatenate(parts, axis=axis)


def _split_group(arr, names, axis, shapes):
    out, off = {}, 0
    ax = arr.ndim - 3 + axis
    for n in names:
        size = shapes[n][axis]
        out[n] = lax.slice_in_dim(arr, off, off + size, axis=ax)
        off += size
    return out


def _pack_rows(vectors):
    blocks = []
    for v in vectors:
        for li in range(v.shape[0]):
            blocks.append(jnp.pad(v[li][None, :], ((0, PACK_ROWS - 1), (0, PACK_W - v.shape[1]))))
    return jnp.concatenate(blocks, axis=0)


def _unpack_rows(packed, shapes):
    out, r = [], 0
    for layers, width in shapes:
        out.append(jnp.stack([packed[r + PACK_ROWS * li, :width] for li in range(layers)]))
        r += PACK_ROWS * layers
    return out


def _shard_split(name, full):
    return jnp.split(full, N_CHIPS, axis=SHARDED[name])


def kernel(x, positions, mix_norm_pre, w_in, q_norm, w_uq, kv_norm, w_uk, w_uv, w_attn_o, conv_w, conv_b, conv_ln_g, conv_ln_b, w_conv_o, pool_w, pool_scale, w_pool_o, w_mix_o, mix_norm_post, ffn_norm_pre, w_gate, w_up, w_down, ffn_norm_post, loss_target, m_mix_norm_pre, m_w_in, m_q_norm, m_w_uq, m_kv_norm, m_w_uk, m_w_uv, m_w_attn_o, m_conv_w, m_conv_b, m_conv_ln_g, m_conv_ln_b, m_w_conv_o, m_pool_w, m_pool_scale, m_w_pool_o, m_w_mix_o, m_mix_norm_post, m_ffn_norm_pre, m_w_gate, m_w_up, m_w_down, m_ffn_norm_post, v_mix_norm_pre, v_w_in, v_q_norm, v_w_uq, v_kv_norm, v_w_uk, v_w_uv, v_w_attn_o, v_conv_w, v_conv_b, v_conv_ln_g, v_conv_ln_b, v_w_conv_o, v_pool_w, v_pool_scale, v_w_pool_o, v_w_mix_o, v_mix_norm_post, v_ffn_norm_pre, v_w_gate, v_w_up, v_w_down, v_ffn_norm_post):
    given = dict(mix_norm_pre=mix_norm_pre, w_in=w_in, q_norm=q_norm, w_uq=w_uq, kv_norm=kv_norm, w_uk=w_uk, w_uv=w_uv, w_attn_o=w_attn_o,
                 conv_w=conv_w, conv_b=conv_b, conv_ln_g=conv_ln_g, conv_ln_b=conv_ln_b, w_conv_o=w_conv_o, pool_w=pool_w,
                 pool_scale=pool_scale, w_pool_o=w_pool_o, w_mix_o=w_mix_o, mix_norm_post=mix_norm_post, ffn_norm_pre=ffn_norm_pre,
                 w_gate=w_gate, w_up=w_up, w_down=w_down, ffn_norm_post=ffn_norm_post)
    mom = dict(mix_norm_pre=m_mix_norm_pre, w_in=m_w_in, q_norm=m_q_norm, w_uq=m_w_uq, kv_norm=m_kv_norm, w_uk=m_w_uk, w_uv=m_w_uv,
               w_attn_o=m_w_attn_o, conv_w=m_conv_w, conv_b=m_conv_b, conv_ln_g=m_conv_ln_g, conv_ln_b=m_conv_ln_b, w_conv_o=m_w_conv_o,
               pool_w=m_pool_w, pool_scale=m_pool_scale, w_pool_o=m_w_pool_o, w_mix_o=m_w_mix_o, mix_norm_post=m_mix_norm_post,
               ffn_norm_pre=m_ffn_norm_pre, w_gate=m_w_gate, w_up=m_w_up, w_down=m_w_down, ffn_norm_post=m_ffn_norm_post)
    var = dict(mix_norm_pre=v_mix_norm_pre, w_in=v_w_in, q_norm=v_q_norm, w_uq=v_w_uq, kv_norm=v_kv_norm, w_uk=v_w_uk, w_uv=v_w_uv,
               w_attn_o=v_w_attn_o, conv_w=v_conv_w, conv_b=v_conv_b, conv_ln_g=v_conv_ln_g, conv_ln_b=v_conv_ln_b, w_conv_o=v_w_conv_o,
               pool_w=v_pool_w, pool_scale=v_pool_scale, w_pool_o=v_w_pool_o, w_mix_o=v_w_mix_o, mix_norm_post=v_mix_norm_post,
               ffn_norm_pre=v_ffn_norm_pre, w_gate=v_w_gate, w_up=v_w_up, w_down=v_w_down, ffn_norm_post=v_ffn_norm_post)
    s_len = x.shape[1]
    sharded_names = [n for n in WEIGHTS if n in SHARDED]
    chip = 2 * lax.axis_index("x") + lax.axis_index("y")
    place = jnp.stack([chip, lax.axis_index("c")]).astype(jnp.int32)
    shard_shape = {n: given[n].shape for n in sharded_names}

    def to_wire(name):
        a = given[name]
        if name == "conv_w":
            hi = a.astype(BF16)
            lo = (a - hi.astype(F32)).astype(BF16)
            return jnp.concatenate([hi, lo], axis=1)
        return a.astype(BF16)

    wire_shape = {n: (N_LAYERS, 2 * CONV_W, shard_shape[n][2]) if n == "conv_w" else shard_shape[n] for n in sharded_names}
    local = [_join([to_wire(n) for n in names], axis) for names, axis in GROUPS]
    gathered = gather_shards(local)
    full = {}
    for (names, axis), loc, got in zip(GROUPS, local, gathered):
        per_chip = [_split_group(jnp.where(chip == j, loc, got[j]), names, axis, wire_shape) for j in range(N_CHIPS)]
        for n in names:
            parts = [pc[n] for pc in per_chip]
            if n == "conv_w":
                parts = [p[:, :CONV_W].astype(F32) + p[:, CONV_W:].astype(F32) for p in parts]
            full[n] = jnp.concatenate(parts, axis=SHARDED[n])
    layers = []
    for li in range(N_LAYERS):
        p = {n: full[n][li] for n in sharded_names}
        p.update({n: given[n][li] for n in REPLICATED})
        p["pool_w"] = p["pool_w"].astype(BF16)
        layers.append(align_weights(p))

    loss_local, grad_x, grads = local_step(x[0], positions.reshape(s_len, 1), loss_target[0], layers)
    loss = lax.psum(loss_local, MESH_AXES)
    grads = [unalign_grads(g) for g in grads]
    grad_full = {n: jnp.stack([g[n].reshape(given[n].shape[1:]) if n in REPLICATED else g[n] for g in grads]) for n in WEIGHTS}

    split = {n: _shard_split(n, grad_full[n]) for n in sharded_names}
    wires = [jnp.stack([_join([split[n][j].astype(BF16) for n in names], axis) for j in range(N_CHIPS)]) for names, axis in GROUPS]
    theirs = sibling_swap(wires)
    chip_parts = [sibling_sum(g, t, place, f"sibling_sum_{i}") for i, (g, t) in enumerate(zip(wires, theirs))]
    others = chip_exchange(chip_parts)
    layer_sums = [chip_sum(p, o, place, f"chip_sum_{i}") for i, (p, o) in enumerate(zip(chip_parts, others))]
    g_shard = {}
    for (names, axis), s in zip(GROUPS, sibling_gather(layer_sums)):
        g_shard.update(_split_group(s, names, axis, shard_shape))

    row_shapes = [given[n].shape for n in ROW_PARAMS]
    rows_all, pool_w_all = gather_all([_pack_rows([grad_full[n] for n in ROW_PARAMS]), grad_full["pool_w"].reshape(-1, POOL_GD)])
    g_rows = sum_devices(rows_all, "row_params_sum")
    g_pool_w = sum_devices(pool_w_all, "pool_w_sum")
    g_rep = dict(zip(ROW_PARAMS, _unpack_rows(g_rows, row_shapes)))
    g_rep["pool_w"] = g_pool_w.reshape(given["pool_w"].shape)

    g_out, d_out, m_out, v_out = {}, {}, {}, {}
    for n in sharded_names + ["pool_w"]:
        shp = given[n].shape
        two_d = (int(np.prod(shp[:-1])), shp[-1])
        g_n = g_shard[n] if n in SHARDED else g_rep[n]
        d, mn, vn = adamw(given[n].reshape(two_d), g_n.reshape(two_d), mom[n].reshape(two_d), var[n].reshape(two_d), "adamw_" + n)
        g_out[n], d_out[n], m_out[n], v_out[n] = g_n, d.reshape(shp), mn.reshape(shp), vn.reshape(shp)
    rd, rm, rv = adamw(_pack_rows([given[n] for n in ROW_PARAMS]), g_rows, _pack_rows([mom[n] for n in ROW_PARAMS]),
                       _pack_rows([var[n] for n in ROW_PARAMS]), "adamw_row_params")
    for n, d, mn, vn in zip(ROW_PARAMS, *[_unpack_rows(a, row_shapes) for a in (rd, rm, rv)]):
        g_out[n], d_out[n], m_out[n], v_out[n] = g_rep[n], d, mn, vn

    return (loss, grad_x[None], *[g_out[n] for n in WEIGHTS], *[d_out[n] for n in WEIGHTS], *[m_out[n] for n in WEIGHTS],
            *[v_out[n] for n in WEIGHTS])
```

```python
import functools
import math

import numpy as np
import jax
import jax.numpy as jnp
from jax import lax
from jax.experimental import pallas as pl
from jax.experimental.pallas import tpu as pltpu

F32, BF16 = jnp.float32, jnp.bfloat16

D_MODEL = 1024
N_HEADS = 8
NOPE, ROPE, VDIM = 64, 32, 64
HALF_ROPE = ROPE // 2
Q_RANK, KV_RANK = 384, 256
CONV_C, CONV_W = 512, 31
POOL_C, POOL_G, POOL_GD = 512, 4, 128
POOL_WINDOWS = (2, 4, 8, 16)
D_FF = 2816
N_LAYERS = 2
EPS = 1e-6
ROPE_THETA = 10000.0
ATT_SCALE = 1.0 / math.sqrt(NOPE + ROPE)
O_Q, O_KV, O_KR, O_CONV, O_POOL, O_GATE, D_IN = 0, 384, 640, 672, 1696, 2208, 5280

LANE = 128
HP = 128
ZG, ZC, ZP, ZA, ZW = 0, 3072, 4096, 4608, 5376
ZA_W = Q_RANK + KV_RANK + HP
KR_LANE = NOPE
HW = N_HEADS * HP

ADAM_LR, ADAM_B1, ADAM_B2, ADAM_EPS, ADAM_WD, ADAM_STEP = 0.001, 0.9, 0.999, 1e-08, 0.01, 10

ROW_TILE = 256
ATT_TILE_FWD = 1024
ATT_TILE_BWD = 512
ATT_HEADS = 2
CONV_CHUNK = 256
MM_TM, MM_TN, MM_TK = 1024, 1408, 1024
MM_TILE_MAX = 2048
MM_VMEM_BUDGET = 40 * 1024 * 1024
HBM_BYTES_PER_US = 3.0e6
GRID_STEP_US = 0.35
VMEM_LIMIT = 56 * 1024 * 1024
SUM_TILE_BYTES = 1024 * 1024

MESH_AXES = ("x", "y", "c")
PACK_W = 1024
PACK_ROWS = 8


def _cparams(sem):
    return pltpu.CompilerParams(dimension_semantics=sem, vmem_limit_bytes=VMEM_LIMIT)


def _tile(n, target):
    if n <= target:
        return n
    best = None
    for t in range(LANE, target + 1, LANE):
        if n % t == 0:
            best = t
    assert best is not None, (n, target)
    return best


def _mm_tiles(m, n, k, a_bytes, b_bytes, out_bytes):
    divs = lambda d: sorted({t for t in range(LANE, min(d, MM_TILE_MAX) + 1, LANE) if d % t == 0} | ({d} if d <= MM_TILE_MAX else set()))
    best = None
    for tm in divs(m):
        for tn in divs(n):
            blocks = tm * k * a_bytes + k * tn * b_bytes + tm * tn * out_bytes
            if 2 * blocks + tm * tn * 4 > MM_VMEM_BUDGET:
                continue
            steps = (m // tm) * (n // tn)
            moved = m * k * a_bytes + k * n * b_bytes * (m // tm) + m * n * out_bytes
            cost = (moved + blocks) / HBM_BYTES_PER_US + steps * GRID_STEP_US
            if best is None or cost < best[0]:
                best = (cost, tm, tn)
    if best is not None:
        return best[1], best[2], k
    return _tile(m, MM_TM), _tile(n, MM_TN), _tile(k, MM_TK)


def mm(a, b, *, ta=False, tb=False, out_dtype=F32, name):
    m, k = (a.shape[1], a.shape[0]) if ta else a.shape
    n, k2 = b.shape if tb else (b.shape[1], b.shape[0])
    assert k == k2, (a.shape, b.shape, ta, tb)
    tm, tn, tk = _mm_tiles(m, n, k, a.dtype.itemsize, b.dtype.itemsize, jnp.dtype(out_dtype).itemsize)
    nk = k // tk
    dims = (((0 if ta else 1,), (1 if tb else 0,)), ((), ()))

    def body(a_ref, b_ref, o_ref, *acc):
        part = lax.dot_general(a_ref[...].astype(BF16), b_ref[...].astype(BF16), dims, preferred_element_type=F32)
        if nk == 1:
            o_ref[...] = part.astype(o_ref.dtype)
            return
        (acc_ref,) = acc
        kk = pl.program_id(2)

        @pl.when(kk == 0)
        def _():
            acc_ref[...] = part

        @pl.when(kk > 0)
        def _():
            acc_ref[...] += part

        @pl.when(kk == nk - 1)
        def _():
            o_ref[...] = acc_ref[...].astype(o_ref.dtype)

    a_spec = pl.BlockSpec((tk, tm), lambda i, j, kk: (kk, i)) if ta else pl.BlockSpec((tm, tk), lambda i, j, kk: (i, kk))
    b_spec = pl.BlockSpec((tn, tk), lambda i, j, kk: (j, kk)) if tb else pl.BlockSpec((tk, tn), lambda i, j, kk: (kk, j))
    return pl.pallas_call(
        body,
        grid=(m // tm, n // tn, nk),
        in_specs=[a_spec, b_spec],
        out_specs=pl.BlockSpec((tm, tn), lambda i, j, kk: (i, j)),
        out_shape=jax.ShapeDtypeStruct((m, n), out_dtype),
        scratch_shapes=[] if nk == 1 else [pltpu.VMEM((tm, tn), F32)],
        compiler_params=_cparams(("parallel", "parallel", "arbitrary")),
        name=name,
    )(a, b)


def rowwise(name, body, rows, row_ins, full_ins, row_outs, acc_outs=(), into=None):
    tile = min(ROW_TILE, rows)
    into = into or {}
    in_specs = [pl.BlockSpec((tile, w), lambda i, cb=cb: (i, cb)) for _, w, cb in row_ins]
    in_specs += [pl.BlockSpec(a.shape, lambda i, nd=a.ndim: (0,) * nd) for a in full_ins]
    in_specs += [ANY for _ in into]
    n_in = len(row_ins) + len(full_ins)
    aliases = {n_in + k: oi for k, oi in enumerate(into)}
    out_specs, out_shape = [], []
    for ro in row_outs:
        w, dt, full_w, cb = ro if len(ro) == 4 else (*ro, ro[0], 0)
        out_specs.append(pl.BlockSpec((tile, w), lambda i, cb=cb: (i, cb)))
        out_shape.append(jax.ShapeDtypeStruct((rows, full_w), dt))
    out_specs += [pl.BlockSpec(s, lambda i, nd=len(s): (0,) * nd) for s, _ in acc_outs]
    out_shape += [jax.ShapeDtypeStruct(s, dt) for s, dt in acc_outs]
    n_refs = n_in

    def call_body(*refs):
        body(*refs[:n_refs], *refs[n_refs + len(into):])

    outs = pl.pallas_call(
        call_body,
        grid=(rows // tile,),
        in_specs=in_specs,
        out_specs=out_specs,
        out_shape=out_shape,
        input_output_aliases=aliases,
        compiler_params=_cparams(("arbitrary",)),
        name=name,
    )(*[a for a, _, _ in row_ins], *full_ins, *into.values())
    return outs


def _whole(a):
    return (a, a.shape[1], 0)


def _acc(ref, val):
    @pl.when(pl.program_id(0) == 0)
    def _():
        ref[...] = val

    @pl.when(pl.program_id(0) > 0)
    def _():
        ref[...] += val


def _rms(x, g):
    return x * lax.rsqrt(jnp.mean(x * x, axis=-1, keepdims=True) + EPS) * g


def _layer_norm(x, g, b):
    mu = jnp.mean(x, axis=-1, keepdims=True)
    xc = x - mu
    return xc * lax.rsqrt(jnp.mean(xc * xc, axis=-1, keepdims=True) + EPS) * g + b


def _silu(x):
    return x * jax.nn.sigmoid(x)


def _rope(x, cc, sa, sb):
    return x * cc + pltpu.roll(x, HALF_ROPE, 1) * sa + pltpu.roll(x, HP - HALF_ROPE, 1) * sb


def _rope_t(dy, cc, sa, sb):
    return dy * cc + pltpu.roll(dy * sa, HP - HALF_ROPE, 1) + pltpu.roll(dy * sb, HALF_ROPE, 1)


def rope_tables(pos_col, rows):
    lane = np.arange(HP)
    idx = np.where(lane < KR_LANE + HALF_ROPE, lane - KR_LANE, lane - KR_LANE - HALF_ROPE)
    in_rope = (lane >= KR_LANE) & (lane < KR_LANE + ROPE)
    inv_freq = (np.float32(ROPE_THETA) ** (-np.arange(0, ROPE, 2, dtype=np.float32) / np.float32(ROPE))).astype(np.float32)
    freq_row = np.where(in_rope, inv_freq[np.clip(idx, 0, HALF_ROPE - 1)], 0.0).astype(np.float32)[None, :]
    first = ((lane >= KR_LANE) & (lane < KR_LANE + HALF_ROPE)).astype(np.float32)[None, :]
    second = ((lane >= KR_LANE + HALF_ROPE) & (lane < KR_LANE + ROPE)).astype(np.float32)[None, :]

    def body(pos_ref, f_ref, a_ref, b_ref, cc_ref, sa_ref, sb_ref):
        ang = pos_ref[...].astype(F32) * f_ref[...]
        s = jnp.sin(ang)
        cc_ref[...] = jnp.cos(ang)
        sa_ref[...] = s * b_ref[...]
        sb_ref[...] = -s * a_ref[...]

    return rowwise("rope_tables", body, rows, [_whole(pos_col)], [jnp.asarray(freq_row), jnp.asarray(first), jnp.asarray(second)],
                   [(HP, F32)] * 3)


def _causal_mask(t):
    r = lax.broadcasted_iota(jnp.int32, (t, t), 0)
    c = lax.broadcasted_iota(jnp.int32, (t, t), 1)
    return r, c


NT_DIMS = (((1,), (1,)), ((), ()))


def attention_fwd(q, k, v):
    s_len = q.shape[0]
    t = min(ATT_TILE_FWD, s_len)
    nb = s_len // t
    hb = ATT_HEADS
    w = hb * HP

    def body(q_ref, k_ref, v_ref, o_ref, lse_ref, m_sc, acc_sc):
        qi = pl.program_id(1)
        m_sc[...] = jnp.full_like(m_sc, -jnp.inf)
        acc_sc[...] = jnp.zeros_like(acc_sc)

        def block(j, masked):
            ks = pl.ds(pl.multiple_of(j * t, t), t)
            for hh in range(hb):
                ls = slice(hh * HP, (hh + 1) * HP)
                s = lax.dot_general(q_ref[:, ls], k_ref[ks, ls], NT_DIMS, preferred_element_type=F32) * ATT_SCALE
                if masked:
                    r, c = _causal_mask(t)
                    s = jnp.where(c <= r, s, -jnp.inf)
                m_old = m_sc[hh]
                m_new = jnp.maximum(m_old, jnp.max(s, axis=-1, keepdims=True))
                p = jnp.exp(s - m_new)
                acc_sc[hh] = jnp.exp(m_old - m_new) * acc_sc[hh] + jnp.dot(p.astype(BF16), v_ref[ks, ls], preferred_element_type=F32)
                m_sc[hh] = m_new

        def loop_body(j, carry):
            block(j, False)
            return carry

        lax.fori_loop(0, qi, loop_body, 0)
        block(qi, True)
        lane = lax.broadcasted_iota(jnp.int32, (t, HP), 1)
        for hh in range(hb):
            acc = acc_sc[hh]
            l = jnp.sum(jnp.where(lane == VDIM, acc, 0.0), axis=-1, keepdims=True)
            o_ref[:, hh * HP:(hh + 1) * HP] = jnp.where(lane < VDIM, acc / l, 0.0).astype(o_ref.dtype)
            lse_ref[hh] = m_sc[hh] + jnp.log(l)

    resident = pl.BlockSpec((s_len, w), lambda h, qi: (0, h))
    return pl.pallas_call(
        body,
        grid=(N_HEADS // hb, nb),
        in_specs=[pl.BlockSpec((t, w), lambda h, qi: (qi, h)), resident, resident],
        out_specs=[pl.BlockSpec((t, w), lambda h, qi: (qi, h)), pl.BlockSpec((hb, t, 1), lambda h, qi: (h, qi, 0))],
        out_shape=[jax.ShapeDtypeStruct((s_len, HW), BF16), jax.ShapeDtypeStruct((N_HEADS, s_len, 1), F32)],
        scratch_shapes=[pltpu.VMEM((hb, t, 1), F32), pltpu.VMEM((hb, t, HP), F32)],
        compiler_params=_cparams(("parallel", "arbitrary")),
        name="attention_fwd",
    )(q, k, v)


def attention_delta(do, o):
    s_len = do.shape[0]
    t = min(ROW_TILE, s_len)

    def body(do_ref, o_ref, d_ref):
        prod = do_ref[...].astype(F32) * o_ref[...].astype(F32)
        for h in range(N_HEADS):
            d_ref[h] = jnp.sum(prod[:, h * HP:(h + 1) * HP], axis=-1, keepdims=True)

    return pl.pallas_call(
        body,
        grid=(s_len // t,),
        in_specs=[pl.BlockSpec((t, HW), lambda i: (i, 0))] * 2,
        out_specs=pl.BlockSpec((N_HEADS, t, 1), lambda i: (0, i, 0)),
        out_shape=jax.ShapeDtypeStruct((N_HEADS, s_len, 1), F32),
        compiler_params=_cparams(("arbitrary",)),
        name="attention_delta",
    )(do, o)


TN_DIMS = (((0,), (0,)), ((), ()))


def attention_bwd(q, k, v, do, lse_row, delta_row):
    s_len = q.shape[0]
    t = min(ATT_TILE_BWD, s_len)
    nb = s_len // t
    hb = ATT_HEADS
    w = hb * HP

    def body(q_ref, k_ref, v_ref, do_ref, lse_ref, dl_ref, dq_ref, dk_ref, dv_ref, dk_sc, dv_sc):
        ki = pl.program_id(1)

        @pl.when(ki == 0)
        def _():
            dq_ref[...] = jnp.zeros_like(dq_ref)

        dk_sc[...] = jnp.zeros_like(dk_sc)
        dv_sc[...] = jnp.zeros_like(dv_sc)

        def block(j, masked):
            qs = pl.ds(pl.multiple_of(j * t, t), t)
            for hh in range(hb):
                ls = slice(hh * HP, (hh + 1) * HP)
                qb = q_ref[qs, ls]
                dob = do_ref[qs, ls]
                kb = k_ref[:, ls]
                st = lax.dot_general(kb, qb, NT_DIMS, preferred_element_type=F32) * ATT_SCALE
                pt = jnp.exp(st - lse_ref[hh, j])
                if masked:
                    r, c = _causal_mask(t)
                    pt = jnp.where(r <= c, pt, 0.0)
                dv_sc[hh] += jnp.dot(pt.astype(BF16), dob, preferred_element_type=F32)
                dpt = lax.dot_general(v_ref[:, ls], dob, NT_DIMS, preferred_element_type=F32)
                dst = (pt * (dpt - dl_ref[hh, j]) * ATT_SCALE).astype(BF16)
                dk_sc[hh] += jnp.dot(dst, qb, preferred_element_type=F32)
                dq_ref[qs, ls] += lax.dot_general(dst, kb, TN_DIMS, preferred_element_type=F32)

        block(ki, True)

        def loop_body(j, carry):
            block(j, False)
            return carry

        lax.fori_loop(ki + 1, nb, loop_body, 0)
        for hh in range(hb):
            ls = slice(hh * HP, (hh + 1) * HP)
            dk_ref[:, ls] = dk_sc[hh].astype(dk_ref.dtype)
            dv_ref[:, ls] = dv_sc[hh].astype(dv_ref.dtype)

    k_spec = pl.BlockSpec((t, w), lambda h, ki: (ki, h))
    resident = pl.BlockSpec((s_len, w), lambda h, ki: (0, h))
    row_spec = pl.BlockSpec((hb, nb, 1, t), lambda h, ki: (h, 0, 0, 0))
    return pl.pallas_call(
        body,
        grid=(N_HEADS // hb, nb),
        in_specs=[resident, k_spec, k_spec, resident, row_spec, row_spec],
        out_specs=[resident, k_spec, k_spec],
        out_shape=[jax.ShapeDtypeStruct((s_len, HW), F32), jax.ShapeDtypeStruct((s_len, HW), F32), jax.ShapeDtypeStruct((s_len, HW), BF16)],
        scratch_shapes=[pltpu.VMEM((hb, t, HP), F32), pltpu.VMEM((hb, t, HP), F32)],
        compiler_params=_cparams(("parallel", "arbitrary")),
        name="attention_bwd",
    )(q, k, v, do, lse_row, delta_row)


CONV_PAD = 32


def conv_fwd(z, conv_w, conv_b):
    s_len = z.shape[0]
    ch = min(CONV_CHUNK, s_len)

    def body(ag_ref, w_ref, b_ref, c_ref, pad_ref):
        pad_ref[0:CONV_PAD, :] = jnp.zeros((CONV_PAD, LANE), F32)
        pad_ref[CONV_PAD:CONV_PAD + s_len, :] = ag_ref[:, 0:LANE] * jax.nn.sigmoid(ag_ref[:, LANE:2 * LANE])

        def chunk(i, carry):
            base = pl.multiple_of(i * ch, ch)
            acc = jnp.zeros((ch, LANE), F32) + b_ref[...]
            for kk in range(CONV_W):
                acc = acc + pad_ref[pl.ds(base + CONV_PAD - (CONV_W - 1) + kk, ch), :] * w_ref[kk:kk + 1, :]
            c_ref[pl.ds(base, ch), :] = acc
            return carry

        lax.fori_loop(0, s_len // ch, chunk, 0)

    nblk = CONV_C // LANE
    return pl.pallas_call(
        body,
        grid=(nblk,),
        in_specs=[pl.BlockSpec((s_len, 2 * LANE), lambda j: (0, ZC // (2 * LANE) + j)),
                  pl.BlockSpec((CONV_W, LANE), lambda j: (0, j)), pl.BlockSpec((1, LANE), lambda j: (0, j))],
        out_specs=pl.BlockSpec((s_len, LANE), lambda j: (0, j)),
        out_shape=jax.ShapeDtypeStruct((s_len, CONV_C), F32),
        scratch_shapes=[pltpu.VMEM((s_len + CONV_PAD, LANE), F32)],
        compiler_params=_cparams(("arbitrary",)),
        name="conv_fwd",
    )(z, conv_w, conv_b)


def conv_bwd(z, dc, conv_w, dz):
    s_len = z.shape[0]
    ch = min(CONV_CHUNK, s_len)

    def body(ag_ref, dc_ref, w_ref, dz_in, dag_ref, dw_ref, db_ref, pad_ref, dpad_ref, wacc_ref):
        del dz_in
        pad_ref[0:CONV_PAD, :] = jnp.zeros((CONV_PAD, LANE), F32)
        pad_ref[CONV_PAD:CONV_PAD + s_len, :] = ag_ref[:, 0:LANE] * jax.nn.sigmoid(ag_ref[:, LANE:2 * LANE])
        dpad_ref[0:s_len, :] = dc_ref[...]
        dpad_ref[s_len:s_len + CONV_PAD, :] = jnp.zeros((CONV_PAD, LANE), F32)
        wacc_ref[...] = jnp.zeros_like(wacc_ref)
        db_ref[...] = jnp.sum(dc_ref[...], axis=0, keepdims=True)

        def chunk(i, carry):
            base = pl.multiple_of(i * ch, ch)
            dcc = dpad_ref[pl.ds(base, ch), :]
            dh = jnp.zeros((ch, LANE), F32)
            for kk in range(CONV_W):
                dh = dh + dpad_ref[pl.ds(base + (CONV_W - 1) - kk, ch), :] * w_ref[kk:kk + 1, :]
                prod = dcc * pad_ref[pl.ds(base + CONV_PAD - (CONV_W - 1) + kk, ch), :]
                wacc_ref[kk * 8:(kk + 1) * 8, :] += prod.reshape(ch // 8, 8, LANE).sum(axis=0)
            a = ag_ref[pl.ds(base, ch), 0:LANE]
            sgc = jax.nn.sigmoid(ag_ref[pl.ds(base, ch), LANE:2 * LANE])
            dag_ref[pl.ds(base, ch), 0:LANE] = (dh * sgc).astype(dag_ref.dtype)
            dag_ref[pl.ds(base, ch), LANE:2 * LANE] = (dh * a * sgc * (1.0 - sgc)).astype(dag_ref.dtype)
            return carry

        lax.fori_loop(0, s_len // ch, chunk, 0)
        for kk in range(CONV_W):
            dw_ref[kk:kk + 1, :] = jnp.sum(wacc_ref[kk * 8:(kk + 1) * 8, :], axis=0, keepdims=True)

    nblk = CONV_C // LANE
    pair = pl.BlockSpec((s_len, 2 * LANE), lambda j: (0, ZC // (2 * LANE) + j))
    return pl.pallas_call(
        body,
        grid=(nblk,),
        in_specs=[pair, pl.BlockSpec((s_len, LANE), lambda j: (0, j)), pl.BlockSpec((CONV_W, LANE), lambda j: (0, j)), ANY],
        out_specs=[pair, pl.BlockSpec((CONV_W, LANE), lambda j: (0, j)), pl.BlockSpec((1, LANE), lambda j: (0, j))],
        out_shape=[jax.ShapeDtypeStruct(dz.shape, dz.dtype), jax.ShapeDtypeStruct((CONV_W, CONV_C), F32), jax.ShapeDtypeStruct((1, CONV_C), F32)],
        scratch_shapes=[pltpu.VMEM((s_len + CONV_PAD, LANE), F32), pltpu.VMEM((s_len + CONV_PAD, LANE), F32),
                        pltpu.VMEM((CONV_W * 8, LANE), F32)],
        input_output_aliases={3: 0},
        compiler_params=_cparams(("arbitrary",)),
        name="conv_bwd",
    )(z, dc, conv_w, dz)


POOL_PAD = 16


def _pool_count(base, ch, w):
    t = base + lax.broadcasted_iota(jnp.int32, (ch, 1), 0)
    return jnp.minimum(t + 1, w).astype(F32)


def pool_fwd(z, pool_w, pool_scale):
    s_len = z.shape[0]
    ch = min(CONV_CHUNK, s_len)

    def body(u_ref, pw_ref, sc_ref, m_ref, pad_ref):
        gi = pl.program_id(0)
        pad_ref[0:POOL_PAD, :] = jnp.zeros((POOL_PAD, LANE), F32)
        pad_ref[POOL_PAD:POOL_PAD + s_len, :] = u_ref[...]

        def run(w):
            def chunk(i, carry):
                base = pl.multiple_of(i * ch, ch)
                acc = jnp.zeros((ch, LANE), F32)
                for j in range(w):
                    acc = acc + pad_ref[pl.ds(base + POOL_PAD - j, ch), :]
                d = acc / _pool_count(base, ch, w) - u_ref[pl.ds(base, ch), :]
                md = jnp.dot(d.astype(BF16), pw_ref[0], preferred_element_type=F32)
                m_ref[pl.ds(base, ch), :] = (md * sc_ref[...]).astype(m_ref.dtype)
                return carry

            lax.fori_loop(0, s_len // ch, chunk, 0)

        for g, w in enumerate(POOL_WINDOWS):
            pl.when(gi == g)(functools.partial(run, w))

    return pl.pallas_call(
        body,
        grid=(POOL_G,),
        in_specs=[pl.BlockSpec((s_len, LANE), lambda g: (0, ZP // LANE + g)), pl.BlockSpec((1, POOL_GD, POOL_GD), lambda g: (g, 0, 0)),
                  pl.BlockSpec((1, LANE), lambda g: (0, g))],
        out_specs=pl.BlockSpec((s_len, LANE), lambda g: (0, g)),
        out_shape=jax.ShapeDtypeStruct((s_len, POOL_C), BF16),
        scratch_shapes=[pltpu.VMEM((s_len + POOL_PAD, LANE), F32)],
        compiler_params=_cparams(("arbitrary",)),
        name="pool_fwd",
    )(z, pool_w, pool_scale)


def pool_bwd(z, dm, pool_w, pool_scale, dz):
    s_len = z.shape[0]
    ch = min(CONV_CHUNK, s_len)

    def body(u_ref, dm_ref, pw_ref, sc_ref, dz_in, du_ref, dpw_ref, dsc_ref, pad_ref, epad_ref, dd_ref, sacc_ref):
        del dz_in
        gi = pl.program_id(0)
        pad_ref[0:POOL_PAD, :] = jnp.zeros((POOL_PAD, LANE), F32)
        pad_ref[POOL_PAD:POOL_PAD + s_len, :] = u_ref[...]
        epad_ref[s_len:s_len + POOL_PAD, :] = jnp.zeros((POOL_PAD, LANE), F32)
        dpw_ref[...] = jnp.zeros_like(dpw_ref)
        sacc_ref[...] = jnp.zeros_like(sacc_ref)

        def run(w):
            def first(i, carry):
                base = pl.multiple_of(i * ch, ch)
                acc = jnp.zeros((ch, LANE), F32)
                for j in range(w):
                    acc = acc + pad_ref[pl.ds(base + POOL_PAD - j, ch), :]
                cnt = _pool_count(base, ch, w)
                d = (acc / cnt - u_ref[pl.ds(base, ch), :]).astype(BF16)
                md = jnp.dot(d, pw_ref[0], preferred_element_type=F32)
                dmc = dm_ref[pl.ds(base, ch), :]
                sacc_ref[...] += (dmc * md).reshape(ch // 8, 8, LANE).sum(axis=0)
                dmd = (dmc * sc_ref[...]).astype(BF16)
                dpw_ref[0] += lax.dot_general(d, dmd, (((0,), (0,)), ((), ())), preferred_element_type=F32)
                dd = lax.dot_general(dmd, pw_ref[0], (((1,), (1,)), ((), ())), preferred_element_type=F32)
                dd_ref[pl.ds(base, ch), :] = dd
                epad_ref[pl.ds(base, ch), :] = dd / cnt
                return carry

            lax.fori_loop(0, s_len // ch, first, 0)

            def second(i, carry):
                base = pl.multiple_of(i * ch, ch)
                acc = jnp.zeros((ch, LANE), F32)
                for j in range(w):
                    acc = acc + epad_ref[pl.ds(base + j, ch), :]
                du_ref[pl.ds(base, ch), :] = (acc - dd_ref[pl.ds(base, ch), :]).astype(du_ref.dtype)
                return carry

            lax.fori_loop(0, s_len // ch, second, 0)

        for g, w in enumerate(POOL_WINDOWS):
            pl.when(gi == g)(functools.partial(run, w))
        dsc_ref[...] = jnp.sum(sacc_ref[...], axis=0, keepdims=True)

    return pl.pallas_call(
        body,
        grid=(POOL_G,),
        in_specs=[pl.BlockSpec((s_len, LANE), lambda g: (0, ZP // LANE + g)), pl.BlockSpec((s_len, LANE), lambda g: (0, g)),
                  pl.BlockSpec((1, POOL_GD, POOL_GD), lambda g: (g, 0, 0)), pl.BlockSpec((1, LANE), lambda g: (0, g)), ANY],
        out_specs=[pl.BlockSpec((s_len, LANE), lambda g: (0, ZP // LANE + g)), pl.BlockSpec((1, POOL_GD, POOL_GD), lambda g: (g, 0, 0)),
                   pl.BlockSpec((1, LANE), lambda g: (0, g))],
        out_shape=[jax.ShapeDtypeStruct(dz.shape, dz.dtype), jax.ShapeDtypeStruct((POOL_G, POOL_GD, POOL_GD), F32),
                   jax.ShapeDtypeStruct((1, POOL_C), F32)],
        scratch_shapes=[pltpu.VMEM((s_len + POOL_PAD, LANE), F32), pltpu.VMEM((s_len + POOL_PAD, LANE), F32),
                        pltpu.VMEM((s_len, LANE), F32), pltpu.VMEM((8, LANE), F32)],
        input_output_aliases={4: 0},
        compiler_params=_cparams(("arbitrary",)),
        name="pool_bwd",
    )(z, dm, pool_w, pool_scale, dz)


def _row(v):
    return v.reshape(1, -1)


def layer_fwd(x, tabs, w, tag):
    s_len = x.shape[0]
    cc, sa, sb = tabs
    sv = {"x": x}

    def rms_body(x_ref, g_ref, o_ref):
        o_ref[...] = _rms(x_ref[...], g_ref[...]).astype(o_ref.dtype)

    (h,) = rowwise("mix_norm_pre" + tag, rms_body, s_len, [_whole(x)], [_row(w["mix_norm_pre"])], [(D_MODEL, BF16)])
    z = mm(h, w["w_in"], name="in_proj" + tag)

    def prep_body(z_ref, cc_ref, sa_ref, sb_ref, qg_ref, kg_ref, qn_ref, ckv_ref, kr_ref):
        qn_ref[...] = _rms(z_ref[:, 0:Q_RANK], qg_ref[...]).astype(qn_ref.dtype)
        ckv_ref[...] = _rms(z_ref[:, Q_RANK:Q_RANK + KV_RANK], kg_ref[...]).astype(ckv_ref.dtype)
        kr_ref[...] = _rope(z_ref[:, Q_RANK + KV_RANK:ZA_W], cc_ref[...], sa_ref[...], sb_ref[...])

    qn, ckvn, kr = rowwise("attn_prep" + tag, prep_body, s_len, [(z, ZA_W, ZA // ZA_W), _whole(cc), _whole(sa), _whole(sb)],
                           [_row(w["q_norm"]), _row(w["kv_norm"])], [(Q_RANK, BF16), (KV_RANK, BF16), (HP, F32)])
    q_raw = mm(qn, w["w_uq"], name="q_proj" + tag)
    kv_raw = mm(ckvn, w["w_ukv"], name="kv_proj" + tag)

    def qkv_body(q_ref, kv_ref, kr_ref, cc_ref, sa_ref, sb_ref, qo_ref, ko_ref, vo_ref):
        c_, a_, b_, kro = cc_ref[...], sa_ref[...], sb_ref[...], kr_ref[...]
        for hh in range(N_HEADS):
            sl = slice(hh * HP, (hh + 1) * HP)
            qo_ref[:, sl] = _rope(q_ref[:, sl], c_, a_, b_).astype(qo_ref.dtype)
            ko_ref[:, sl] = (kv_ref[:, sl] + kro).astype(ko_ref.dtype)
        lane = lax.broadcasted_iota(jnp.int32, (q_ref.shape[0], HW), 1)
        vo_ref[...] = jnp.where((lane & (HP - 1)) == VDIM, 1.0, kv_ref[:, HW:2 * HW]).astype(vo_ref.dtype)

    q, k, v = rowwise("qkv_rope" + tag, qkv_body, s_len, [_whole(q_raw), _whole(kv_raw), _whole(kr), _whole(cc), _whole(sa), _whole(sb)], [],
                      [(HW, BF16)] * 3)
    o, lse = attention_fwd(q, k, v)
    y_attn = mm(o, w["w_attn_o"], out_dtype=BF16, name="attn_out" + tag)

    c = conv_fwd(z, w["conv_w"], _row(w["conv_b"]))

    def ln_body(c_ref, g_ref, b_ref, o_ref):
        o_ref[...] = _silu(_layer_norm(c_ref[...], g_ref[...], b_ref[...])).astype(o_ref.dtype)

    (cs,) = rowwise("conv_ln_silu" + tag, ln_body, s_len, [_whole(c)], [_row(w["conv_ln_g"]), _row(w["conv_ln_b"])], [(CONV_C, BF16)])
    y_conv = mm(cs, w["w_conv_o"], out_dtype=BF16, name="conv_out" + tag)

    m = pool_fwd(z, w["pool_w"], _row(w["pool_scale"]))
    y_pool = mm(m, w["w_pool_o"], out_dtype=BF16, name="pool_out" + tag)

    def merge_body(ya_ref, yc_ref, yp_ref, gl_ref, o_ref):
        gl = gl_ref[...]
        o_ref[...] = (jax.nn.sigmoid(gl[:, 0:D_MODEL]) * ya_ref[...].astype(F32) + jax.nn.sigmoid(gl[:, D_MODEL:2 * D_MODEL]) * yc_ref[...].astype(F32)
                      + jax.nn.sigmoid(gl[:, 2 * D_MODEL:3 * D_MODEL]) * yp_ref[...].astype(F32)).astype(o_ref.dtype)

    (merged,) = rowwise("gate_merge" + tag, merge_body, s_len, [_whole(y_attn), _whole(y_conv), _whole(y_pool), (z, 3 * D_MODEL, 0)], [],
                        [(D_MODEL, BF16)])
    mo = mm(merged, w["w_mix_o"], name="mix_out" + tag)

    def post_body(y_ref, x_ref, g_ref, o_ref):
        o_ref[...] = x_ref[...] + _rms(y_ref[...], g_ref[...])

    (h1,) = rowwise("mix_norm_post" + tag, post_body, s_len, [_whole(mo), _whole(x)], [_row(w["mix_norm_post"])], [(D_MODEL, F32)])

    (hn,) = rowwise("ffn_norm_pre" + tag, rms_body, s_len, [_whole(h1)], [_row(w["ffn_norm_pre"])], [(D_MODEL, BF16)])
    gu = mm(hn, w["w_gu"], out_dtype=BF16, name="ffn_in" + tag)

    def swiglu_body(gu_ref, o_ref):
        o_ref[...] = (_silu(gu_ref[:, 0:D_FF].astype(F32)) * gu_ref[:, D_FF:2 * D_FF].astype(F32)).astype(o_ref.dtype)

    (act,) = rowwise("swiglu" + tag, swiglu_body, s_len, [_whole(gu)], [], [(D_FF, BF16)])
    y = mm(act, w["w_down"], name="ffn_out" + tag)
    (h2,) = rowwise("ffn_norm_post" + tag, post_body, s_len, [_whole(y), _whole(h1)], [_row(w["ffn_norm_post"])], [(D_MODEL, F32)])

    sv.update(h=h, z=z, qn=qn, ckvn=ckvn, q=q, k=k, v=v, o=o, lse=lse, c=c, cs=cs, m=m, y_attn=y_attn, y_conv=y_conv, y_pool=y_pool,
              merged=merged, mo=mo, h1=h1, hn=hn, gu=gu, act=act, y=y)
    return h2, sv


def layer_bwd(dh2, sv, tabs, w, tag):
    s_len = dh2.shape[0]
    cc, sa, sb = tabs
    g = {}

    def post_bwd_body(y_ref, dh_ref, g_ref, dy_ref, dg_ref):
        _, vjp = jax.vjp(_rms, y_ref[...], g_ref[...])
        dy, dg = vjp(dh_ref[...])
        dy_ref[...] = dy.astype(dy_ref.dtype)
        _acc(dg_ref, dg)

    def pre_bwd_body(x_ref, dhn_ref, dres_ref, g_ref, dx_ref, dg_ref):
        _, vjp = jax.vjp(_rms, x_ref[...], g_ref[...])
        dx, dg = vjp(dhn_ref[...])
        dx_ref[...] = dres_ref[...] + dx
        _acc(dg_ref, dg)

    d_y, g["ffn_norm_post"] = rowwise("ffn_norm_post_bwd" + tag, post_bwd_body, s_len, [_whole(sv["y"]), _whole(dh2)],
                                      [_row(w["ffn_norm_post"])], [(D_MODEL, BF16)], [((1, D_MODEL), F32)])
    g["w_down"] = mm(sv["act"], d_y, ta=True, name="ffn_out_dw" + tag)
    d_act = mm(d_y, w["w_down"], tb=True, out_dtype=BF16, name="ffn_out_dx" + tag)

    def swiglu_bwd_body(gu_ref, da_ref, dgu_ref):
        f = lambda gt, up: _silu(gt) * up
        _, vjp = jax.vjp(f, gu_ref[:, 0:D_FF].astype(F32), gu_ref[:, D_FF:2 * D_FF].astype(F32))
        dgt, dup = vjp(da_ref[...].astype(F32))
        dgu_ref[:, 0:D_FF] = dgt.astype(dgu_ref.dtype)
        dgu_ref[:, D_FF:2 * D_FF] = dup.astype(dgu_ref.dtype)

    (d_gu,) = rowwise("swiglu_bwd" + tag, swiglu_bwd_body, s_len, [_whole(sv["gu"]), _whole(d_act)], [], [(2 * D_FF, BF16)])
    g["w_gu"] = mm(sv["hn"], d_gu, ta=True, name="ffn_in_dw" + tag)
    d_hn = mm(d_gu, w["w_gu"], tb=True, name="ffn_in_dx" + tag)
    dh1, g["ffn_norm_pre"] = rowwise(
        "ffn_norm_pre_bwd" + tag, pre_bwd_body, s_len, [_whole(sv["h1"]), _whole(d_hn), _whole(dh2)], [_row(w["ffn_norm_pre"])], [(D_MODEL, F32)], [((1, D_MODEL), F32)])

    d_mo, g["mix_norm_post"] = rowwise(
        "mix_norm_post_bwd" + tag, post_bwd_body, s_len, [_whole(sv["mo"]), _whole(dh1)], [_row(w["mix_norm_post"])], [(D_MODEL, BF16)], [((1, D_MODEL), F32)])
    g["w_mix_o"] = mm(sv["merged"], d_mo, ta=True, name="mix_out_dw" + tag)
    d_merged = mm(d_mo, w["w_mix_o"], tb=True, name="mix_out_dx" + tag)

    def merge_bwd_body(dm_ref, ya_ref, yc_ref, yp_ref, gl_ref, dya_ref, dyc_ref, dyp_ref, dgl_ref):
        dmg = dm_ref[...]
        for i, (y_ref, dy_ref) in enumerate(((ya_ref, dya_ref), (yc_ref, dyc_ref), (yp_ref, dyp_ref))):
            sg = jax.nn.sigmoid(gl_ref[:, i * D_MODEL:(i + 1) * D_MODEL])
            dy_ref[...] = (dmg * sg).astype(dy_ref.dtype)
            dgl_ref[:, i * D_MODEL:(i + 1) * D_MODEL] = (dmg * y_ref[...].astype(F32) * sg * (1.0 - sg)).astype(dgl_ref.dtype)

    d_ya, d_yc, d_yp, dz = rowwise(
        "gate_merge_bwd" + tag, merge_bwd_body, s_len,
        [_whole(d_merged), _whole(sv["y_attn"]), _whole(sv["y_conv"]), _whole(sv["y_pool"]), (sv["z"], 3 * D_MODEL, 0)], [],
        [(D_MODEL, BF16)] * 3 + [(3 * D_MODEL, BF16, ZW, 0)])

    g["w_pool_o"] = mm(sv["m"], d_yp, ta=True, name="pool_out_dw" + tag)
    d_m = mm(d_yp, w["w_pool_o"], tb=True, name="pool_out_dx" + tag)
    dz, g["pool_w"], g["pool_scale"] = pool_bwd(sv["z"], d_m, w["pool_w"], _row(w["pool_scale"]), dz)

    g["w_conv_o"] = mm(sv["cs"], d_yc, ta=True, name="conv_out_dw" + tag)
    d_cs = mm(d_yc, w["w_conv_o"], tb=True, name="conv_out_dx" + tag)

    def ln_bwd_body(c_ref, dcs_ref, g_ref, b_ref, dc_ref, dg_ref, db_ref):
        f = lambda c_, g_, b_: _silu(_layer_norm(c_, g_, b_))
        _, vjp = jax.vjp(f, c_ref[...], g_ref[...], b_ref[...])
        dc, dg, db = vjp(dcs_ref[...])
        dc_ref[...] = dc
        _acc(dg_ref, dg)
        _acc(db_ref, db)

    d_c, g["conv_ln_g"], g["conv_ln_b"] = rowwise("conv_ln_silu_bwd" + tag, ln_bwd_body, s_len, [_whole(sv["c"]), _whole(d_cs)],
                                                  [_row(w["conv_ln_g"]), _row(w["conv_ln_b"])], [(CONV_C, F32)],
                                                  [((1, CONV_C), F32), ((1, CONV_C), F32)])
    dz, g["conv_w"], g["conv_b"] = conv_bwd(sv["z"], d_c, w["conv_w"], dz)

    g["w_attn_o"] = mm(sv["o"], d_ya, ta=True, name="attn_out_dw" + tag)
    d_o = mm(d_ya, w["w_attn_o"], tb=True, out_dtype=BF16, name="attn_out_dx" + tag)
    delta = attention_delta(d_o, sv["o"])
    t_bwd = min(ATT_TILE_BWD, s_len)
    rows_of = lambda a: a.reshape(N_HEADS, s_len // t_bwd, 1, t_bwd)
    dq, dk, dv = attention_bwd(sv["q"], sv["k"], sv["v"], d_o, rows_of(sv["lse"]), rows_of(delta))

    def qkv_bwd_body(dq_ref, dk_ref, dv_ref, cc_ref, sa_ref, sb_ref, dqp_ref, dkv_ref, dkr_ref):
        c_, a_, b_ = cc_ref[...], sa_ref[...], sb_ref[...]
        dk_sum = jnp.zeros((dq_ref.shape[0], HP), F32)
        for hh in range(N_HEADS):
            sl = slice(hh * HP, (hh + 1) * HP)
            dqp_ref[:, sl] = _rope_t(dq_ref[:, sl], c_, a_, b_).astype(dqp_ref.dtype)
            dkh = dk_ref[:, sl]
            dkv_ref[:, sl] = dkh.astype(dkv_ref.dtype)
            dk_sum = dk_sum + dkh
        dkv_ref[:, HW:2 * HW] = dv_ref[...]
        dkr_ref[...] = _rope_t(dk_sum, c_, a_, b_)

    dq_pre, dkv_pre, d_kr = rowwise("qkv_rope_bwd" + tag, qkv_bwd_body, s_len,
                                    [_whole(dq), _whole(dk), _whole(dv), _whole(cc), _whole(sa), _whole(sb)], [],
                                    [(HW, BF16), (2 * HW, BF16), (HP, F32)])
    g["w_uq"] = mm(sv["qn"], dq_pre, ta=True, name="q_proj_dw" + tag)
    d_qn = mm(dq_pre, w["w_uq"], tb=True, name="q_proj_dx" + tag)
    g["w_ukv"] = mm(sv["ckvn"], dkv_pre, ta=True, name="kv_proj_dw" + tag)
    d_ckvn = mm(dkv_pre, w["w_ukv"], tb=True, name="kv_proj_dx" + tag)

    def prep_bwd_body(z_ref, dqn_ref, dckv_ref, dkr_ref, qg_ref, kg_ref, dz_ref, dqg_ref, dkg_ref):
        _, vq = jax.vjp(_rms, z_ref[:, 0:Q_RANK], qg_ref[...])
        dcq, dqg = vq(dqn_ref[...])
        _, vk = jax.vjp(_rms, z_ref[:, Q_RANK:Q_RANK + KV_RANK], kg_ref[...])
        dckv, dkg = vk(dckv_ref[...])
        dz_ref[:, 0:Q_RANK] = dcq.astype(dz_ref.dtype)
        dz_ref[:, Q_RANK:Q_RANK + KV_RANK] = dckv.astype(dz_ref.dtype)
        dz_ref[:, Q_RANK + KV_RANK:ZA_W] = dkr_ref[...].astype(dz_ref.dtype)
        _acc(dqg_ref, dqg)
        _acc(dkg_ref, dkg)

    dz, g["q_norm"], g["kv_norm"] = rowwise("attn_prep_bwd" + tag, prep_bwd_body, s_len,
                                            [(sv["z"], ZA_W, ZA // ZA_W), _whole(d_qn), _whole(d_ckvn), _whole(d_kr)],
                                            [_row(w["q_norm"]), _row(w["kv_norm"])], [(ZA_W, BF16, ZW, ZA // ZA_W)],
                                            [((1, Q_RANK), F32), ((1, KV_RANK), F32)], into={0: dz})

    g["w_in"] = mm(sv["h"], dz, ta=True, name="in_proj_dw" + tag)
    d_h = mm(dz, w["w_in"], tb=True, name="in_proj_dx" + tag)
    dx, g["mix_norm_pre"] = rowwise(
        "mix_norm_pre_bwd" + tag, pre_bwd_body, s_len, [_whole(sv["x"]), _whole(d_h), _whole(dh1)], [_row(w["mix_norm_pre"])], [(D_MODEL, F32)], [((1, D_MODEL), F32)])
    return dx, g


def loss_head(h, target):
    s_len = h.shape[0]

    def body(h_ref, t_ref, dy_ref, loss_ref):
        err = h_ref[...] - t_ref[...]
        dy_ref[...] = err * (1.0 / D_MODEL)
        part = 0.5 * jnp.sum(jnp.mean(err * err, axis=-1, keepdims=True), axis=0, keepdims=True)
        _acc(loss_ref, jnp.broadcast_to(part, (1, LANE)))

    return rowwise("loss_head", body, s_len, [_whole(h), _whole(target)], [], [(D_MODEL, F32)], [((1, LANE), F32)])


def local_step(x, pos_col, target, layers):
    s_len = x.shape[0]
    tabs = rope_tables(pos_col, s_len)
    h, saved = x, []
    for li, w in enumerate(layers):
        h, sv = layer_fwd(h, tabs, w, f"_l{li}")
        saved.append(sv)
    dh, loss = loss_head(h, target)
    grads = [None] * len(layers)
    for li in reversed(range(len(layers))):
        dh, grads[li] = layer_bwd(dh, saved[li], tabs, layers[li], f"_l{li}")
    return loss[0, 0], dh, grads


def _pad_heads_cols(wm, per_head):
    r = wm.shape[0]
    return jnp.pad(wm.reshape(r, N_HEADS, per_head), ((0, 0), (0, 0), (0, HP - per_head))).reshape(r, HW)


def _unpad_heads_cols(wm, per_head):
    r = wm.shape[0]
    return wm.reshape(r, N_HEADS, HP)[:, :, :per_head].reshape(r, N_HEADS * per_head)


def align_weights(p):
    w_in = p["w_in"]
    r = w_in.shape[0]
    zeros = lambda n: jnp.zeros((r, n), w_in.dtype)
    conv = w_in[:, O_CONV:O_POOL].reshape(r, 2, CONV_C // LANE, LANE).transpose(0, 2, 1, 3).reshape(r, 2 * CONV_C)
    w_in_al = jnp.concatenate([
        w_in[:, O_GATE:D_IN], conv, w_in[:, O_POOL:O_GATE], w_in[:, O_Q:O_KR],
        zeros(KR_LANE), w_in[:, O_KR:O_CONV], zeros(HP - KR_LANE - ROPE)], axis=1)
    out = dict(p)
    out["w_in"] = w_in_al
    out["w_uq"] = _pad_heads_cols(p["w_uq"], NOPE + ROPE)
    out["w_ukv"] = jnp.concatenate([_pad_heads_cols(p["w_uk"], NOPE), _pad_heads_cols(p["w_uv"], VDIM)], axis=1)
    wo = p["w_attn_o"]
    out["w_attn_o"] = jnp.pad(wo.reshape(N_HEADS, VDIM, D_MODEL), ((0, 0), (0, HP - VDIM), (0, 0))).reshape(HW, D_MODEL)
    out["w_gu"] = jnp.concatenate([p["w_gate"], p["w_up"]], axis=1)
    for name in ("w_uk", "w_uv", "w_gate", "w_up"):
        del out[name]
    return out


def unalign_grads(g):
    gi = g["w_in"]
    kr0 = ZA + Q_RANK + KV_RANK + KR_LANE
    out = dict(g)
    r = gi.shape[0]
    conv = gi[:, ZC:ZP].reshape(r, CONV_C // LANE, 2, LANE).transpose(0, 2, 1, 3).reshape(r, 2 * CONV_C)
    out["w_in"] = jnp.concatenate([gi[:, ZA:ZA + Q_RANK + KV_RANK], gi[:, kr0:kr0 + ROPE], conv, gi[:, ZP:ZA], gi[:, ZG:ZC]], axis=1)
    out["w_uq"] = _unpad_heads_cols(g["w_uq"], NOPE + ROPE)
    out["w_uk"] = _unpad_heads_cols(g["w_ukv"][:, :HW], NOPE)
    out["w_uv"] = _unpad_heads_cols(g["w_ukv"][:, HW:], VDIM)
    out["w_attn_o"] = g["w_attn_o"].reshape(N_HEADS, HP, D_MODEL)[:, :VDIM].reshape(N_HEADS * VDIM, D_MODEL)
    out["w_gate"] = g["w_gu"][:, :D_FF]
    out["w_up"] = g["w_gu"][:, D_FF:]
    del out["w_ukv"], out["w_gu"]
    return out


MESH = pl.DeviceIdType.MESH
ANY = pl.BlockSpec(memory_space=pl.ANY)


def _place():
    return lax.axis_index("x"), lax.axis_index("y"), lax.axis_index("c")


def _other_chips(x, y):
    return [(1 - x, y), (x, 1 - y), (1 - x, 1 - y)]


def gather_shards(local):
    n = len(local)

    def body(*refs):
        w_refs, out_refs, (send_sems, recv_sems) = refs[:n], refs[n:2 * n], refs[2 * n:]
        x, y, c = _place()
        me = 2 * x + y
        chips = _other_chips(x, y)
        slots = [2 * cx + cy for cx, cy in chips]
        sibling = (x, y, 1 - c)

        def copy(i, k, slot, layer, to, src=None):
            dst = out_refs[i].at[slot, layer]
            return pltpu.make_async_remote_copy(src_ref=dst if src is None else src, dst_ref=dst, send_sem=send_sems.at[6 * i + k],
                                                recv_sem=recv_sems.at[6 * i + k], device_id=to, device_id_type=MESH)

        first = [copy(i, j, me, c, (*chip, c), src=w_refs[i].at[c]) for i in range(n) for j, chip in enumerate(chips)]
        for cp in first:
            cp.start()
        passed = []
        for i in range(n):
            for j in range(3):
                copy(i, j, slots[j], c, sibling).wait_recv()
                fwd = copy(i, 3 + j, slots[j], c, sibling)
                fwd.start()
                passed.append(fwd)
        for i in range(n):
            for j in range(3):
                copy(i, 3 + j, slots[j], 1 - c, sibling).wait_recv()
        for cp in first + passed:
            cp.wait_send()

    return pl.pallas_call(
        body,
        in_specs=[ANY] * n,
        out_specs=[ANY] * n,
        out_shape=[jax.ShapeDtypeStruct((N_CHIPS, *a.shape), a.dtype) for a in local],
        scratch_shapes=[pltpu.SemaphoreType.DMA((6 * n,)), pltpu.SemaphoreType.DMA((6 * n,))],
        name="gather_shards",
    )(*local)


def sibling_swap(gs):
    n = len(gs)

    def body(*refs):
        g_refs, out_refs, (send_sems, recv_sems) = refs[:n], refs[n:2 * n], refs[2 * n:]
        x, y, c = _place()
        copies = []
        for i in range(n):
            for j in range(N_CHIPS):
                cp = pltpu.make_async_remote_copy(src_ref=g_refs[i].at[j, 1 - c], dst_ref=out_refs[i].at[j], send_sem=send_sems.at[4 * i + j],
                                                  recv_sem=recv_sems.at[4 * i + j], device_id=(x, y, 1 - c), device_id_type=MESH)
                cp.start()
                copies.append(cp)
        for cp in copies:
            cp.wait_recv()
        for cp in copies:
            cp.wait_send()

    return pl.pallas_call(
        body,
        in_specs=[ANY] * n,
        out_specs=[ANY] * n,
        out_shape=[jax.ShapeDtypeStruct((N_CHIPS, *a.shape[2:]), a.dtype) for a in gs],
        scratch_shapes=[pltpu.SemaphoreType.DMA((4 * n,)), pltpu.SemaphoreType.DMA((4 * n,))],
        name="sibling_swap",
    )(*gs)


def chip_exchange(ps):
    n = len(ps)

    def body(*refs):
        p_refs, out_refs, (send_sems, recv_sems) = refs[:n], refs[n:2 * n], refs[2 * n:]
        x, y, c = _place()
        chips = _other_chips(x, y)
        copies = []
        for i in range(n):
            for j, chip in enumerate(chips):
                cp = pltpu.make_async_remote_copy(src_ref=p_refs[i].at[2 * chip[0] + chip[1]], dst_ref=out_refs[i].at[j],
                                                  send_sem=send_sems.at[3 * i + j], recv_sem=recv_sems.at[3 * i + j],
                                                  device_id=(*chip, c), device_id_type=MESH)
                cp.start()
                copies.append(cp)
        for cp in copies:
            cp.wait_recv()
        for cp in copies:
            cp.wait_send()

    return pl.pallas_call(
        body,
        in_specs=[ANY] * n,
        out_specs=[ANY] * n,
        out_shape=[jax.ShapeDtypeStruct((3, *a.shape[1:]), a.dtype) for a in ps],
        scratch_shapes=[pltpu.SemaphoreType.DMA((3 * n,)), pltpu.SemaphoreType.DMA((3 * n,))],
        name="chip_exchange",
    )(*ps)


def sibling_gather(fs):
    n = len(fs)

    def body(*refs):
        out_refs, (send_sems, recv_sems) = refs[n:2 * n], refs[2 * n:]
        x, y, c = _place()
        copies = []
        for i in range(n):
            cp = pltpu.make_async_remote_copy(src_ref=out_refs[i].at[c], dst_ref=out_refs[i].at[c], send_sem=send_sems.at[i],
                                              recv_sem=recv_sems.at[i], device_id=(x, y, 1 - c), device_id_type=MESH)
            cp.start()
            copies.append(cp)
        for i in range(n):
            pltpu.make_async_remote_copy(src_ref=out_refs[i].at[1 - c], dst_ref=out_refs[i].at[1 - c], send_sem=send_sems.at[i],
                                         recv_sem=recv_sems.at[i], device_id=(x, y, 1 - c), device_id_type=MESH).wait_recv()
        for cp in copies:
            cp.wait_send()

    return pl.pallas_call(
        body,
        in_specs=[ANY] * n,
        out_specs=[ANY] * n,
        out_shape=[jax.ShapeDtypeStruct(a.shape, a.dtype) for a in fs],
        scratch_shapes=[pltpu.SemaphoreType.DMA((n,)), pltpu.SemaphoreType.DMA((n,))],
        input_output_aliases={i: i for i in range(n)},
        name="sibling_gather",
    )(*fs)


def gather_all(vs):
    n = len(vs)

    def body(*refs):
        v_refs, out_refs, (send_sems, recv_sems, local_sems) = refs[:n], refs[n:2 * n], refs[2 * n:]
        x, y, c = _place()
        me, sibling = (x, y, c), (x, y, 1 - c)
        chips = _other_chips(x, y)

        def copy(i, k, block, to, src=None):
            px, py, pc = block
            dst = out_refs[i].at[4 * px + 2 * py + pc]
            return pltpu.make_async_remote_copy(src_ref=dst if src is None else src, dst_ref=dst, send_sem=send_sems.at[7 * i + k],
                                                recv_sem=recv_sems.at[7 * i + k], device_id=to, device_id_type=MESH)

        mine = [pltpu.make_async_copy(v_refs[i], out_refs[i].at[4 * x + 2 * y + c], local_sems.at[i]) for i in range(n)]
        for cp in mine:
            cp.start()
        first = []
        for i in range(n):
            first.append(copy(i, 0, me, sibling, src=v_refs[i]))
            first += [copy(i, 1 + j, me, (*chip, c), src=v_refs[i]) for j, chip in enumerate(chips)]
        for cp in first:
            cp.start()
        passed = []
        for i in range(n):
            for j, chip in enumerate(chips):
                copy(i, 1 + j, (*chip, c), me).wait_recv()
                fwd = copy(i, 4 + j, (*chip, c), sibling)
                fwd.start()
                passed.append(fwd)
        for i in range(n):
            copy(i, 0, sibling, me).wait_recv()
            for j, chip in enumerate(chips):
                copy(i, 4 + j, (*chip, 1 - c), me).wait_recv()
        for cp in first + passed:
            cp.wait_send()
        for cp in mine:
            cp.wait()

    return pl.pallas_call(
        body,
        in_specs=[ANY] * n,
        out_specs=[ANY] * n,
        out_shape=[jax.ShapeDtypeStruct((8, *a.shape), a.dtype) for a in vs],
        scratch_shapes=[pltpu.SemaphoreType.DMA((7 * n,)), pltpu.SemaphoreType.DMA((7 * n,)), pltpu.SemaphoreType.DMA((n,))],
        name="gather_all",
    )(*vs)


def _row_tile(rows, row_bytes):
    best = None
    for t in range(16, rows + 1, 16):
        if rows % t == 0 and t * row_bytes <= SUM_TILE_BYTES:
            best = t
    return best or rows


def sibling_sum(g, theirs, place, name):
    _, _, rows, cols = g.shape
    tile = _row_tile(rows, cols * 4)

    def body(place_ref, g_ref, t_ref, o_ref):
        o_ref[...] = (g_ref[0].astype(F32) + t_ref[...].astype(F32)).astype(o_ref.dtype)

    spec = pl.BlockSpec((1, tile, cols), lambda j, i, place_ref: (j, i, 0))
    return pl.pallas_call(
        body,
        grid_spec=pltpu.PrefetchScalarGridSpec(
            num_scalar_prefetch=1, grid=(N_CHIPS, rows // tile),
            in_specs=[pl.BlockSpec((1, 1, tile, cols), lambda j, i, place_ref: (j, place_ref[1], i, 0)), spec], out_specs=spec),
        out_shape=jax.ShapeDtypeStruct(theirs.shape, BF16),
        compiler_params=_cparams(("parallel", "parallel")),
        name=name,
    )(place, g, theirs)


def chip_sum(p, others, place, name):
    _, rows, cols = p.shape
    tile = _row_tile(rows, cols * 4)

    def body(place_ref, p_ref, o3_ref, o_ref):
        acc = p_ref[0].astype(F32)
        for k in range(3):
            acc = acc + o3_ref[k].astype(F32)
        o_ref[0] = acc

    return pl.pallas_call(
        body,
        grid_spec=pltpu.PrefetchScalarGridSpec(
            num_scalar_prefetch=1, grid=(rows // tile,),
            in_specs=[pl.BlockSpec((1, tile, cols), lambda i, place_ref: (place_ref[0], i, 0)),
                      pl.BlockSpec((3, tile, cols), lambda i, place_ref: (0, i, 0))],
            out_specs=pl.BlockSpec((1, tile, cols), lambda i, place_ref: (place_ref[1], i, 0))),
        out_shape=jax.ShapeDtypeStruct((2, rows, cols), F32),
        compiler_params=_cparams(("parallel",)),
        name=name,
    )(place, p, others)


def sum_devices(a, name):
    n, rows, cols = a.shape
    tile = _row_tile(rows, cols * 4 * n)

    def body(a_ref, o_ref):
        acc = a_ref[0]
        for s in range(1, n):
            acc = acc + a_ref[s]
        o_ref[...] = acc

    return pl.pallas_call(body, grid=(rows // tile,), in_specs=[pl.BlockSpec((n, tile, cols), lambda i: (0, i, 0))],
                          out_specs=pl.BlockSpec((tile, cols), lambda i: (i, 0)), out_shape=jax.ShapeDtypeStruct((rows, cols), F32),
                          compiler_params=_cparams(("parallel",)), name=name)(a)


def adamw(w, g, m, v, name):
    rows, cols = w.shape
    tile = rows
    for t in (512, 256, 128, 64, 32, 16, 8):
        if rows % t == 0 and t * cols * 4 <= 2 * 1024 * 1024:
            tile = t
            break

    def body(w_ref, g_ref, m_ref, v_ref, d_ref, mo_ref, vo_ref):
        gg = g_ref[...]
        m_new = ADAM_B1 * m_ref[...] + (1.0 - ADAM_B1) * gg
        v_new = ADAM_B2 * v_ref[...] + (1.0 - ADAM_B2) * (gg * gg)
        m_hat = m_new / (1.0 - ADAM_B1 ** ADAM_STEP)
        v_hat = v_new / (1.0 - ADAM_B2 ** ADAM_STEP)
        d_ref[...] = -ADAM_LR * (m_hat / (jnp.sqrt(v_hat) + ADAM_EPS) + ADAM_WD * w_ref[...])
        mo_ref[...] = m_new
        vo_ref[...] = v_new

    spec = pl.BlockSpec((tile, cols), lambda i: (i, 0))
    shape = jax.ShapeDtypeStruct((rows, cols), F32)
    return pl.pallas_call(body, grid=(rows // tile,), in_specs=[spec] * 4, out_specs=[spec] * 3, out_shape=[shape] * 3,
                          compiler_params=_cparams(("parallel",)), name=name)(w, g, m, v)


WEIGHTS = ["mix_norm_pre", "w_in", "q_norm", "w_uq", "kv_norm", "w_uk", "w_uv", "w_attn_o", "conv_w", "conv_b", "conv_ln_g", "conv_ln_b",
           "w_conv_o", "pool_w", "pool_scale", "w_pool_o", "w_mix_o", "mix_norm_post", "ffn_norm_pre", "w_gate", "w_up", "w_down",
           "ffn_norm_post"]
SHARDED = {"w_in": 2, "w_uq": 2, "w_uk": 2, "w_uv": 2, "w_attn_o": 2, "conv_w": 2, "w_conv_o": 2, "w_pool_o": 2, "w_mix_o": 1,
           "w_gate": 2, "w_up": 2, "w_down": 1}
REPLICATED = [n for n in WEIGHTS if n not in SHARDED]
ROW_PARAMS = [n for n in REPLICATED if n != "pool_w"]
N_CHIPS = 4
GROUPS = [(("w_in",), 1), (("w_uq",), 1), (("w_uk", "w_uv"), 1), (("w_attn_o", "w_conv_o", "w_pool_o"), 1), (("conv_w",), 1),
          (("w_mix_o",), 1), (("w_gate", "w_up"), 2), (("w_down",), 1)]


def _join(parts, axis):
    return parts[0] if len(parts) == 1 else jnp.concatenate(parts, axis=axis)


def _split_group(arr, names, axis, shapes):
    out, off = {}, 0
    ax = arr.ndim - 3 + axis
    for n in names:
        size = shapes[n][axis]
        out[n] = lax.slice_in_dim(arr, off, off + size, axis=ax)
        off += size
    return out


def _pack_rows(vectors):
    blocks = []
    for v in vectors:
        for li in range(v.shape[0]):
            blocks.append(jnp.pad(v[li][None, :], ((0, PACK_ROWS - 1), (0, PACK_W - v.shape[1]))))
    return jnp.concatenate(blocks, axis=0)


def _unpack_rows(packed, shapes):
    out, r = [], 0
    for layers, width in shapes:
        out.append(jnp.stack([packed[r + PACK_ROWS * li, :width] for li in range(layers)]))
        r += PACK_ROWS * layers
    return out


def _shard_split(name, full):
    return jnp.split(full, N_CHIPS, axis=SHARDED[name])


def kernel(x, positions, mix_norm_pre, w_in, q_norm, w_uq, kv_norm, w_uk, w_uv, w_attn_o, conv_w, conv_b, conv_ln_g, conv_ln_b, w_conv_o, pool_w, pool_scale, w_pool_o, w_mix_o, mix_norm_post, ffn_norm_pre, w_gate, w_up, w_down, ffn_norm_post, loss_target, m_mix_norm_pre, m_w_in, m_q_norm, m_w_uq, m_kv_norm, m_w_uk, m_w_uv, m_w_attn_o, m_conv_w, m_conv_b, m_conv_ln_g, m_conv_ln_b, m_w_conv_o, m_pool_w, m_pool_scale, m_w_pool_o, m_w_mix_o, m_mix_norm_post, m_ffn_norm_pre, m_w_gate, m_w_up, m_w_down, m_ffn_norm_post, v_mix_norm_pre, v_w_in, v_q_norm, v_w_uq, v_kv_norm, v_w_uk, v_w_uv, v_w_attn_o, v_conv_w, v_conv_b, v_conv_ln_g, v_conv_ln_b, v_w_conv_o, v_pool_w, v_pool_scale, v_w_pool_o, v_w_mix_o, v_mix_norm_post, v_ffn_norm_pre, v_w_gate, v_w_up, v_w_down, v_ffn_norm_post):
    given = dict(mix_norm_pre=mix_norm_pre, w_in=w_in, q_norm=q_norm, w_uq=w_uq, kv_norm=kv_norm, w_uk=w_uk, w_uv=w_uv, w_attn_o=w_attn_o,
                 conv_w=conv_w, conv_b=conv_b, conv_ln_g=conv_ln_g, conv_ln_b=conv_ln_b, w_conv_o=w_conv_o, pool_w=pool_w,
                 pool_scale=pool_scale, w_pool_o=w_pool_o, w_mix_o=w_mix_o, mix_norm_post=mix_norm_post, ffn_norm_pre=ffn_norm_pre,
                 w_gate=w_gate, w_up=w_up, w_down=w_down, ffn_norm_post=ffn_norm_post)
    mom = dict(mix_norm_pre=m_mix_norm_pre, w_in=m_w_in, q_norm=m_q_norm, w_uq=m_w_uq, kv_norm=m_kv_norm, w_uk=m_w_uk, w_uv=m_w_uv,
               w_attn_o=m_w_attn_o, conv_w=m_conv_w, conv_b=m_conv_b, conv_ln_g=m_conv_ln_g, conv_ln_b=m_conv_ln_b, w_conv_o=m_w_conv_o,
               pool_w=m_pool_w, pool_scale=m_pool_scale, w_pool_o=m_w_pool_o, w_mix_o=m_w_mix_o, mix_norm_post=m_mix_norm_post,
               ffn_norm_pre=m_ffn_norm_pre, w_gate=m_w_gate, w_up=m_w_up, w_down=m_w_down, ffn_norm_post=m_ffn_norm_post)
    var = dict(mix_norm_pre=v_mix_norm_pre, w_in=v_w_in, q_norm=v_q_norm, w_uq=v_w_uq, kv_norm=v_kv_norm, w_uk=v_w_uk, w_uv=v_w_uv,
               w_attn_o=v_w_attn_o, conv_w=v_conv_w, conv_b=v_conv_b, conv_ln_g=v_conv_ln_g, conv_ln_b=v_conv_ln_b, w_conv_o=v_w_conv_o,
               pool_w=v_pool_w, pool_scale=v_pool_scale, w_pool_o=v_w_pool_o, w_mix_o=v_w_mix_o, mix_norm_post=v_mix_norm_post,
               ffn_norm_pre=v_ffn_norm_pre, w_gate=v_w_gate, w_up=v_w_up, w_down=v_w_down, ffn_norm_post=v_ffn_norm_post)
    s_len = x.shape[1]
    sharded_names = [n for n in WEIGHTS if n in SHARDED]
    chip = 2 * lax.axis_index("x") + lax.axis_index("y")
    place = jnp.stack([chip, lax.axis_index("c")]).astype(jnp.int32)
    shard_shape = {n: given[n].shape for n in sharded_names}

    def to_wire(name):
        a = given[name]
        if name == "conv_w":
            hi = a.astype(BF16)
            lo = (a - hi.astype(F32)).astype(BF16)
            return jnp.concatenate([hi, lo], axis=1)
        return a.astype(BF16)

    wire_shape = {n: (N_LAYERS, 2 * CONV_W, shard_shape[n][2]) if n == "conv_w" else shard_shape[n] for n in sharded_names}
    local = [_join([to_wire(n) for n in names], axis) for names, axis in GROUPS]
    gathered = gather_shards(local)
    full = {}
    for (names, axis), loc, got in zip(GROUPS, local, gathered):
        per_chip = [_split_group(jnp.where(chip == j, loc, got[j]), names, axis, wire_shape) for j in range(N_CHIPS)]
        for n in names:
            parts = [pc[n] for pc in per_chip]
            if n == "conv_w":
                parts = [p[:, :CONV_W].astype(F32) + p[:, CONV_W:].astype(F32) for p in parts]
            full[n] = jnp.concatenate(parts, axis=SHARDED[n])
    layers = []
    for li in range(N_LAYERS):
        p = {n: full[n][li] for n in sharded_names}
        p.update({n: given[n][li] for n in REPLICATED})
        p["pool_w"] = p["pool_w"].astype(BF16)
        layers.append(align_weights(p))

    loss_local, grad_x, grads = local_step(x[0], positions.reshape(s_len, 1), loss_target[0], layers)
    loss = lax.psum(loss_local, MESH_AXES)
    grads = [unalign_grads(g) for g in grads]
    grad_full = {n: jnp.stack([g[n].reshape(given[n].shape[1:]) if n in REPLICATED else g[n] for g in grads]) for n in WEIGHTS}

    split = {n: _shard_split(n, grad_full[n]) for n in sharded_names}
    wires = [jnp.stack([_join([split[n][j].astype(BF16) for n in names], axis) for j in range(N_CHIPS)]) for names, axis in GROUPS]
    theirs = sibling_swap(wires)
    chip_parts = [sibling_sum(g, t, place, f"sibling_sum_{i}") for i, (g, t) in enumerate(zip(wires, theirs))]
    others = chip_exchange(chip_parts)
    layer_sums = [chip_sum(p, o, place, f"chip_sum_{i}") for i, (p, o) in enumerate(zip(chip_parts, others))]
    g_shard = {}
    for (names, axis), s in zip(GROUPS, sibling_gather(layer_sums)):
        g_shard.update(_split_group(s, names, axis, shard_shape))

    row_shapes = [given[n].shape for n in ROW_PARAMS]
    rows_all, pool_w_all = gather_all([_pack_rows([grad_full[n] for n in ROW_PARAMS]), grad_full["pool_w"].reshape(-1, POOL_GD)])
    g_rows = sum_devices(rows_all, "row_params_sum")
    g_pool_w = sum_devices(pool_w_all, "pool_w_sum")
    g_rep = dict(zip(ROW_PARAMS, _unpack_rows(g_rows, row_shapes)))
    g_rep["pool_w"] = g_pool_w.reshape(given["pool_w"].shape)

    g_out, d_out, m_out, v_out = {}, {}, {}, {}
    for n in sharded_names + ["pool_w"]:
        shp = given[n].shape
        two_d = (int(np.prod(shp[:-1])), shp[-1])
        g_n = g_shard[n] if n in SHARDED else g_rep[n]
        d, mn, vn = adamw(given[n].reshape(two_d), g_n.reshape(two_d), mom[n].reshape(two_d), var[n].reshape(two_d), "adamw_" + n)
        g_out[n], d_out[n], m_out[n], v_out[n] = g_n, d.reshape(shp), mn.reshape(shp), vn.reshape(shp)
    rd, rm, rv = adamw(_pack_rows([given[n] for n in ROW_PARAMS]), g_rows, _pack_rows([mom[n] for n in ROW_PARAMS]),
                       _pack_rows([var[n] for n in ROW_PARAMS]), "adamw_row_params")
    for n, d, mn, vn in zip(ROW_PARAMS, *[_unpack_rows(a, row_shapes) for a in (rd, rm, rv)]):
        g_out[n], d_out[n], m_out[n], v_out[n] = g_rep[n], d, mn, vn

    return (loss, grad_x[None], *[g_out[n] for n in WEIGHTS], *[d_out[n] for n in WEIGHTS], *[m_out[n] for n in WEIGHTS],
            *[v_out[n] for n in WEIGHTS])
```

```python
import functools
import math

import numpy as np
import jax
import jax.numpy as jnp
from jax import lax
from jax.experimental import pallas as pl
from jax.experimental.pallas import tpu as pltpu

F32, BF16 = jnp.float32, jnp.bfloat16

D_MODEL = 1024
N_HEADS = 8
NOPE, ROPE, VDIM = 64, 32, 64
HALF_ROPE = ROPE // 2
Q_RANK, KV_RANK = 384, 256
CONV_C, CONV_W = 512, 31
POOL_C, POOL_G, POOL_GD = 512, 4, 128
POOL_WINDOWS = (2, 4, 8, 16)
D_FF = 2816
FF_HALF = D_FF // 2
N_LAYERS = 2
EPS = 1e-6
ROPE_THETA = 10000.0
ATT_SCALE = 1.0 / math.sqrt(NOPE + ROPE)
O_Q, O_KV, O_KR, O_CONV, O_POOL, O_GATE, D_IN = 0, 384, 640, 672, 1696, 2208, 5280

LANE = 128
HP = 128
ZG, ZC, ZP, ZA, ZW = 0, 3072, 4096, 4608, 5376
ZA_W = Q_RANK + KV_RANK + HP
KR_LANE = NOPE
HW = N_HEADS * HP

ADAM_LR, ADAM_B1, ADAM_B2, ADAM_EPS, ADAM_WD, ADAM_STEP = 0.001, 0.9, 0.999, 1e-08, 0.01, 10

ROW_TILE = 256
ATT_TILE_FWD = 1024
ATT_TILE_BWD = 512
ATT_HEADS = 2
CONV_CHUNK = 256
MM_TM, MM_TN, MM_TK = 1024, 1408, 1024
MM_TILE_MAX = 2048
FFN_TM = 512
MM_VMEM_BUDGET = 40 * 1024 * 1024
HBM_BYTES_PER_US = 3.0e6
GRID_STEP_US = 0.35
VMEM_LIMIT = 56 * 1024 * 1024
SUM_TILE_BYTES = 1024 * 1024

MESH_AXES = ("x", "y", "c")
PACK_W = 1024
PACK_ROWS = 8


def _cparams(sem):
    return pltpu.CompilerParams(dimension_semantics=sem, vmem_limit_bytes=VMEM_LIMIT)


def _tile(n, target):
    if n <= target:
        return n
    best = None
    for t in range(LANE, target + 1, LANE):
        if n % t == 0:
            best = t
    assert best is not None, (n, target)
    return best


def _mm_tiles(m, n, k, a_bytes, b_bytes, out_bytes):
    divs = lambda d: sorted({t for t in range(LANE, min(d, MM_TILE_MAX) + 1, LANE) if d % t == 0} | ({d} if d <= MM_TILE_MAX else set()))
    best = None
    for tm in divs(m):
        for tn in divs(n):
            blocks = tm * k * a_bytes + k * tn * b_bytes + tm * tn * out_bytes
            if 2 * blocks + tm * tn * 4 > MM_VMEM_BUDGET:
                continue
            steps = (m // tm) * (n // tn)
            moved = m * k * a_bytes + k * n * b_bytes * (m // tm) + m * n * out_bytes
            cost = (moved + blocks) / HBM_BYTES_PER_US + steps * GRID_STEP_US
            if best is None or cost < best[0]:
                best = (cost, tm, tn)
    if best is not None:
        return best[1], best[2], k
    return _tile(m, MM_TM), _tile(n, MM_TN), _tile(k, MM_TK)


def mm(a, b, *, ta=False, tb=False, out_dtype=F32, name):
    m, k = (a.shape[1], a.shape[0]) if ta else a.shape
    n, k2 = b.shape if tb else (b.shape[1], b.shape[0])
    assert k == k2, (a.shape, b.shape, ta, tb)
    tm, tn, tk = _mm_tiles(m, n, k, a.dtype.itemsize, b.dtype.itemsize, jnp.dtype(out_dtype).itemsize)
    nk = k // tk
    dims = (((0 if ta else 1,), (1 if tb else 0,)), ((), ()))

    def body(a_ref, b_ref, o_ref, *acc):
        part = lax.dot_general(a_ref[...].astype(BF16), b_ref[...].astype(BF16), dims, preferred_element_type=F32)
        if nk == 1:
            o_ref[...] = part.astype(o_ref.dtype)
            return
        (acc_ref,) = acc
        kk = pl.program_id(2)

        @pl.when(kk == 0)
        def _():
            acc_ref[...] = part

        @pl.when(kk > 0)
        def _():
            acc_ref[...] += part

        @pl.when(kk == nk - 1)
        def _():
            o_ref[...] = acc_ref[...].astype(o_ref.dtype)

    a_spec = pl.BlockSpec((tk, tm), lambda i, j, kk: (kk, i)) if ta else pl.BlockSpec((tm, tk), lambda i, j, kk: (i, kk))
    b_spec = pl.BlockSpec((tn, tk), lambda i, j, kk: (j, kk)) if tb else pl.BlockSpec((tk, tn), lambda i, j, kk: (kk, j))
    return pl.pallas_call(
        body,
        grid=(m // tm, n // tn, nk),
        in_specs=[a_spec, b_spec],
        out_specs=pl.BlockSpec((tm, tn), lambda i, j, kk: (i, j)),
        out_shape=jax.ShapeDtypeStruct((m, n), out_dtype),
        scratch_shapes=[] if nk == 1 else [pltpu.VMEM((tm, tn), F32)],
        compiler_params=_cparams(("parallel", "parallel", "arbitrary")),
        name=name,
    )(a, b)


def ffn_in(hn, w_gu, name):
    s_len, k = hn.shape
    tm = min(FFN_TM, s_len)

    def body(a_ref, b_ref, gu_ref, act_ref):
        r = jnp.dot(a_ref[...], b_ref[...], preferred_element_type=F32)
        gu_ref[...] = r.astype(gu_ref.dtype)
        act_ref[...] = (_silu(r[:, :FF_HALF]) * r[:, FF_HALF:]).astype(act_ref.dtype)

    return pl.pallas_call(
        body,
        grid=(s_len // tm, 2),
        in_specs=[pl.BlockSpec((tm, k), lambda i, j: (i, 0)), pl.BlockSpec((k, 2 * FF_HALF), lambda i, j: (0, j))],
        out_specs=[pl.BlockSpec((tm, 2 * FF_HALF), lambda i, j: (i, j)), pl.BlockSpec((tm, FF_HALF), lambda i, j: (i, j))],
        out_shape=[jax.ShapeDtypeStruct((s_len, 2 * D_FF), BF16), jax.ShapeDtypeStruct((s_len, D_FF), BF16)],
        compiler_params=_cparams(("parallel", "arbitrary")),
        name=name,
    )(hn, w_gu)


def ffn_out_dx(d_y, w_down, gu, name):
    s_len, k = d_y.shape
    tm = min(FFN_TM, s_len)

    def body(a_ref, b_ref, gu_ref, dgu_ref):
        da = lax.dot_general(a_ref[...], b_ref[...], NT_DIMS, preferred_element_type=F32)
        gt = gu_ref[:, :FF_HALF].astype(F32)
        up = gu_ref[:, FF_HALF:].astype(F32)
        sg = jax.nn.sigmoid(gt)
        dgu_ref[:, :FF_HALF] = (da * up * sg * (1.0 + gt * (1.0 - sg))).astype(dgu_ref.dtype)
        dgu_ref[:, FF_HALF:] = (da * gt * sg).astype(dgu_ref.dtype)

    pair = pl.BlockSpec((tm, 2 * FF_HALF), lambda i, j: (i, j))
    return pl.pallas_call(
        body,
        grid=(s_len // tm, 2),
        in_specs=[pl.BlockSpec((tm, k), lambda i, j: (i, 0)), pl.BlockSpec((FF_HALF, k), lambda i, j: (j, 0)), pair],
        out_specs=pair,
        out_shape=jax.ShapeDtypeStruct((s_len, 2 * D_FF), BF16),
        compiler_params=_cparams(("parallel", "arbitrary")),
        name=name,
    )(d_y, w_down, gu)


def rowwise(name, body, rows, row_ins, full_ins, row_outs, acc_outs=(), into=None):
    tile = min(ROW_TILE, rows)
    into = into or {}
    in_specs = [pl.BlockSpec((tile, w), lambda i, cb=cb: (i, cb)) for _, w, cb in row_ins]
    in_specs += [pl.BlockSpec(a.shape, lambda i, nd=a.ndim: (0,) * nd) for a in full_ins]
    in_specs += [ANY for _ in into]
    n_in = len(row_ins) + len(full_ins)
    aliases = {n_in + k: oi for k, oi in enumerate(into)}
    out_specs, out_shape = [], []
    for ro in row_outs:
        w, dt, full_w, cb = ro if len(ro) == 4 else (*ro, ro[0], 0)
        out_specs.append(pl.BlockSpec((tile, w), lambda i, cb=cb: (i, cb)))
        out_shape.append(jax.ShapeDtypeStruct((rows, full_w), dt))
    out_specs += [pl.BlockSpec(s, lambda i, nd=len(s): (0,) * nd) for s, _ in acc_outs]
    out_shape += [jax.ShapeDtypeStruct(s, dt) for s, dt in acc_outs]
    n_refs = n_in

    def call_body(*refs):
        body(*refs[:n_refs], *refs[n_refs + len(into):])

    outs = pl.pallas_call(
        call_body,
        grid=(rows // tile,),
        in_specs=in_specs,
        out_specs=out_specs,
        out_shape=out_shape,
        input_output_aliases=aliases,
        compiler_params=_cparams(("arbitrary",)),
        name=name,
    )(*[a for a, _, _ in row_ins], *full_ins, *into.values())
    return outs


def _whole(a):
    return (a, a.shape[1], 0)


def _acc(ref, val):
    @pl.when(pl.program_id(0) == 0)
    def _():
        ref[...] = val

    @pl.when(pl.program_id(0) > 0)
    def _():
        ref[...] += val


def _rms(x, g):
    return x * lax.rsqrt(jnp.mean(x * x, axis=-1, keepdims=True) + EPS) * g


def _layer_norm(x, g, b):
    mu = jnp.mean(x, axis=-1, keepdims=True)
    xc = x - mu
    return xc * lax.rsqrt(jnp.mean(xc * xc, axis=-1, keepdims=True) + EPS) * g + b


def _silu(x):
    return x * jax.nn.sigmoid(x)


def _rope(x, cc, sa, sb):
    return x * cc + pltpu.roll(x, HALF_ROPE, 1) * sa + pltpu.roll(x, HP - HALF_ROPE, 1) * sb


def _rope_t(dy, cc, sa, sb):
    return dy * cc + pltpu.roll(dy * sa, HP - HALF_ROPE, 1) + pltpu.roll(dy * sb, HALF_ROPE, 1)


def rope_tables(pos_col, rows):
    lane = np.arange(HP)
    idx = np.where(lane < KR_LANE + HALF_ROPE, lane - KR_LANE, lane - KR_LANE - HALF_ROPE)
    in_rope = (lane >= KR_LANE) & (lane < KR_LANE + ROPE)
    inv_freq = (np.float32(ROPE_THETA) ** (-np.arange(0, ROPE, 2, dtype=np.float32) / np.float32(ROPE))).astype(np.float32)
    freq_row = np.where(in_rope, inv_freq[np.clip(idx, 0, HALF_ROPE - 1)], 0.0).astype(np.float32)[None, :]
    first = ((lane >= KR_LANE) & (lane < KR_LANE + HALF_ROPE)).astype(np.float32)[None, :]
    second = ((lane >= KR_LANE + HALF_ROPE) & (lane < KR_LANE + ROPE)).astype(np.float32)[None, :]

    def body(pos_ref, f_ref, a_ref, b_ref, cc_ref, sa_ref, sb_ref):
        ang = pos_ref[...].astype(F32) * f_ref[...]
        s = jnp.sin(ang)
        cc_ref[...] = jnp.cos(ang)
        sa_ref[...] = s * b_ref[...]
        sb_ref[...] = -s * a_ref[...]

    return rowwise("rope_tables", body, rows, [_whole(pos_col)], [jnp.asarray(freq_row), jnp.asarray(first), jnp.asarray(second)],
                   [(HP, F32)] * 3)


def _causal_mask(t):
    r = lax.broadcasted_iota(jnp.int32, (t, t), 0)
    c = lax.broadcasted_iota(jnp.int32, (t, t), 1)
    return r, c


NT_DIMS = (((1,), (1,)), ((), ()))


def attention_fwd(q, k, v):
    s_len = q.shape[0]
    t = min(ATT_TILE_FWD, s_len)
    nb = s_len // t
    hb = ATT_HEADS
    w = hb * HP

    def body(q_ref, k_ref, v_ref, o_ref, lse_ref, m_sc, acc_sc):
        qi = pl.program_id(1)
        m_sc[...] = jnp.full_like(m_sc, -jnp.inf)
        acc_sc[...] = jnp.zeros_like(acc_sc)

        def block(j, masked):
            ks = pl.ds(pl.multiple_of(j * t, t), t)
            for hh in range(hb):
                ls = slice(hh * HP, (hh + 1) * HP)
                s = lax.dot_general(q_ref[:, ls], k_ref[ks, ls], NT_DIMS, preferred_element_type=F32) * ATT_SCALE
                if masked:
                    r, c = _causal_mask(t)
                    s = jnp.where(c <= r, s, -jnp.inf)
                m_old = m_sc[hh]
                m_new = jnp.maximum(m_old, jnp.max(s, axis=-1, keepdims=True))
                p = jnp.exp(s - m_new)
                acc_sc[hh] = jnp.exp(m_old - m_new) * acc_sc[hh] + jnp.dot(p.astype(BF16), v_ref[ks, ls], preferred_element_type=F32)
                m_sc[hh] = m_new

        def loop_body(j, carry):
            block(j, False)
            return carry

        lax.fori_loop(0, qi, loop_body, 0)
        block(qi, True)
        lane = lax.broadcasted_iota(jnp.int32, (t, HP), 1)
        for hh in range(hb):
            acc = acc_sc[hh]
            l = jnp.sum(jnp.where(lane == VDIM, acc, 0.0), axis=-1, keepdims=True)
            o_ref[:, hh * HP:(hh + 1) * HP] = jnp.where(lane < VDIM, acc / l, 0.0).astype(o_ref.dtype)
            lse_ref[hh] = m_sc[hh] + jnp.log(l)

    resident = pl.BlockSpec((s_len, w), lambda h, qi: (0, h))
    return pl.pallas_call(
        body,
        grid=(N_HEADS // hb, nb),
        in_specs=[pl.BlockSpec((t, w), lambda h, qi: (qi, h)), resident, resident],
        out_specs=[pl.BlockSpec((t, w), lambda h, qi: (qi, h)), pl.BlockSpec((hb, t, 1), lambda h, qi: (h, qi, 0))],
        out_shape=[jax.ShapeDtypeStruct((s_len, HW), BF16), jax.ShapeDtypeStruct((N_HEADS, s_len, 1), F32)],
        scratch_shapes=[pltpu.VMEM((hb, t, 1), F32), pltpu.VMEM((hb, t, HP), F32)],
        compiler_params=_cparams(("parallel", "arbitrary")),
        name="attention_fwd",
    )(q, k, v)


def attention_delta(do, o):
    s_len = do.shape[0]
    t = min(ROW_TILE, s_len)

    def body(do_ref, o_ref, d_ref):
        prod = do_ref[...].astype(F32) * o_ref[...].astype(F32)
        for h in range(N_HEADS):
            d_ref[h] = jnp.sum(prod[:, h * HP:(h + 1) * HP], axis=-1, keepdims=True)

    return pl.pallas_call(
        body,
        grid=(s_len // t,),
        in_specs=[pl.BlockSpec((t, HW), lambda i: (i, 0))] * 2,
        out_specs=pl.BlockSpec((N_HEADS, t, 1), lambda i: (0, i, 0)),
        out_shape=jax.ShapeDtypeStruct((N_HEADS, s_len, 1), F32),
        compiler_params=_cparams(("arbitrary",)),
        name="attention_delta",
    )(do, o)


TN_DIMS = (((0,), (0,)), ((), ()))


def attention_bwd(q, k, v, do, lse_row, delta_row):
    s_len = q.shape[0]
    t = min(ATT_TILE_BWD, s_len)
    nb = s_len // t
    hb = ATT_HEADS
    w = hb * HP

    def body(q_ref, k_ref, v_ref, do_ref, lse_ref, dl_ref, dq_ref, dk_ref, dv_ref, dk_sc, dv_sc):
        ki = pl.program_id(1)

        @pl.when(ki == 0)
        def _():
            dq_ref[...] = jnp.zeros_like(dq_ref)

        dk_sc[...] = jnp.zeros_like(dk_sc)
        dv_sc[...] = jnp.zeros_like(dv_sc)

        def block(j, masked):
            qs = pl.ds(pl.multiple_of(j * t, t), t)
            for hh in range(hb):
                ls = slice(hh * HP, (hh + 1) * HP)
                qb = q_ref[qs, ls]
                dob = do_ref[qs, ls]
                kb = k_ref[:, ls]
                st = lax.dot_general(kb, qb, NT_DIMS, preferred_element_type=F32) * ATT_SCALE
                pt = jnp.exp(st - lse_ref[hh, j])
                if masked:
                    r, c = _causal_mask(t)
                    pt = jnp.where(r <= c, pt, 0.0)
                dv_sc[hh] += jnp.dot(pt.astype(BF16), dob, preferred_element_type=F32)
                dpt = lax.dot_general(v_ref[:, ls], dob, NT_DIMS, preferred_element_type=F32)
                dst = (pt * (dpt - dl_ref[hh, j]) * ATT_SCALE).astype(BF16)
                dk_sc[hh] += jnp.dot(dst, qb, preferred_element_type=F32)
                dq_ref[qs, ls] += lax.dot_general(dst, kb, TN_DIMS, preferred_element_type=F32)

        block(ki, True)

        def loop_body(j, carry):
            block(j, False)
            return carry

        lax.fori_loop(ki + 1, nb, loop_body, 0)
        for hh in range(hb):
            ls = slice(hh * HP, (hh + 1) * HP)
            dk_ref[:, ls] = dk_sc[hh].astype(dk_ref.dtype)
            dv_ref[:, ls] = dv_sc[hh].astype(dv_ref.dtype)

    k_spec = pl.BlockSpec((t, w), lambda h, ki: (ki, h))
    resident = pl.BlockSpec((s_len, w), lambda h, ki: (0, h))
    row_spec = pl.BlockSpec((hb, nb, 1, t), lambda h, ki: (h, 0, 0, 0))
    return pl.pallas_call(
        body,
        grid=(N_HEADS // hb, nb),
        in_specs=[resident, k_spec, k_spec, resident, row_spec, row_spec],
        out_specs=[resident, k_spec, k_spec],
        out_shape=[jax.ShapeDtypeStruct((s_len, HW), F32), jax.ShapeDtypeStruct((s_len, HW), F32), jax.ShapeDtypeStruct((s_len, HW), BF16)],
        scratch_shapes=[pltpu.VMEM((hb, t, HP), F32), pltpu.VMEM((hb, t, HP), F32)],
        compiler_params=_cparams(("parallel", "arbitrary")),
        name="attention_bwd",
    )(q, k, v, do, lse_row, delta_row)


CONV_PAD = 32


def conv_fwd(z, conv_w, conv_b):
    s_len = z.shape[0]
    ch = min(CONV_CHUNK, s_len)

    def body(ag_ref, w_ref, b_ref, c_ref, pad_ref):
        pad_ref[0:CONV_PAD, :] = jnp.zeros((CONV_PAD, LANE), F32)
        pad_ref[CONV_PAD:CONV_PAD + s_len, :] = ag_ref[:, 0:LANE] * jax.nn.sigmoid(ag_ref[:, LANE:2 * LANE])

        def chunk(i, carry):
            base = pl.multiple_of(i * ch, ch)
            acc = jnp.zeros((ch, LANE), F32) + b_ref[...]
            for kk in range(CONV_W):
                acc = acc + pad_ref[pl.ds(base + CONV_PAD - (CONV_W - 1) + kk, ch), :] * w_ref[kk:kk + 1, :]
            c_ref[pl.ds(base, ch), :] = acc
            return carry

        lax.fori_loop(0, s_len // ch, chunk, 0)

    nblk = CONV_C // LANE
    return pl.pallas_call(
        body,
        grid=(nblk,),
        in_specs=[pl.BlockSpec((s_len, 2 * LANE), lambda j: (0, ZC // (2 * LANE) + j)),
                  pl.BlockSpec((CONV_W, LANE), lambda j: (0, j)), pl.BlockSpec((1, LANE), lambda j: (0, j))],
        out_specs=pl.BlockSpec((s_len, LANE), lambda j: (0, j)),
        out_shape=jax.ShapeDtypeStruct((s_len, CONV_C), F32),
        scratch_shapes=[pltpu.VMEM((s_len + CONV_PAD, LANE), F32)],
        compiler_params=_cparams(("arbitrary",)),
        name="conv_fwd",
    )(z, conv_w, conv_b)


def conv_bwd(z, dc, conv_w, dz):
    s_len = z.shape[0]
    ch = min(CONV_CHUNK, s_len)

    def body(ag_ref, dc_ref, w_ref, dz_in, dag_ref, dw_ref, db_ref, pad_ref, dpad_ref, wacc_ref):
        del dz_in
        pad_ref[0:CONV_PAD, :] = jnp.zeros((CONV_PAD, LANE), F32)
        pad_ref[CONV_PAD:CONV_PAD + s_len, :] = ag_ref[:, 0:LANE] * jax.nn.sigmoid(ag_ref[:, LANE:2 * LANE])
        dpad_ref[0:s_len, :] = dc_ref[...]
        dpad_ref[s_len:s_len + CONV_PAD, :] = jnp.zeros((CONV_PAD, LANE), F32)
        wacc_ref[...] = jnp.zeros_like(wacc_ref)
        db_ref[...] = jnp.sum(dc_ref[...], axis=0, keepdims=True)

        def chunk(i, carry):
            base = pl.multiple_of(i * ch, ch)
            dcc = dpad_ref[pl.ds(base, ch), :]
            dh = jnp.zeros((ch, LANE), F32)
            for kk in range(CONV_W):
                dh = dh + dpad_ref[pl.ds(base + (CONV_W - 1) - kk, ch), :] * w_ref[kk:kk + 1, :]
                prod = dcc * pad_ref[pl.ds(base + CONV_PAD - (CONV_W - 1) + kk, ch), :]
                wacc_ref[kk * 8:(kk + 1) * 8, :] += prod.reshape(ch // 8, 8, LANE).sum(axis=0)
            a = ag_ref[pl.ds(base, ch), 0:LANE]
            sgc = jax.nn.sigmoid(ag_ref[pl.ds(base, ch), LANE:2 * LANE])
            dag_ref[pl.ds(base, ch), 0:LANE] = (dh * sgc).astype(dag_ref.dtype)
            dag_ref[pl.ds(base, ch), LANE:2 * LANE] = (dh * a * sgc * (1.0 - sgc)).astype(dag_ref.dtype)
            return carry

        lax.fori_loop(0, s_len // ch, chunk, 0)
        for kk in range(CONV_W):
            dw_ref[kk:kk + 1, :] = jnp.sum(wacc_ref[kk * 8:(kk + 1) * 8, :], axis=0, keepdims=True)

    nblk = CONV_C // LANE
    pair = pl.BlockSpec((s_len, 2 * LANE), lambda j: (0, ZC // (2 * LANE) + j))
    return pl.pallas_call(
        body,
        grid=(nblk,),
        in_specs=[pair, pl.BlockSpec((s_len, LANE), lambda j: (0, j)), pl.BlockSpec((CONV_W, LANE), lambda j: (0, j)), ANY],
        out_specs=[pair, pl.BlockSpec((CONV_W, LANE), lambda j: (0, j)), pl.BlockSpec((1, LANE), lambda j: (0, j))],
        out_shape=[jax.ShapeDtypeStruct(dz.shape, dz.dtype), jax.ShapeDtypeStruct((CONV_W, CONV_C), F32), jax.ShapeDtypeStruct((1, CONV_C), F32)],
        scratch_shapes=[pltpu.VMEM((s_len + CONV_PAD, LANE), F32), pltpu.VMEM((s_len + CONV_PAD, LANE), F32),
                        pltpu.VMEM((CONV_W * 8, LANE), F32)],
        input_output_aliases={3: 0},
        compiler_params=_cparams(("arbitrary",)),
        name="conv_bwd",
    )(z, dc, conv_w, dz)


POOL_PAD = 16


def _pool_count(base, ch, w):
    t = base + lax.broadcasted_iota(jnp.int32, (ch, 1), 0)
    return jnp.minimum(t + 1, w).astype(F32)


def pool_fwd(z, pool_w, pool_scale):
    s_len = z.shape[0]
    ch = min(CONV_CHUNK, s_len)

    def body(u_ref, pw_ref, sc_ref, m_ref, pad_ref):
        gi = pl.program_id(0)
        pad_ref[0:POOL_PAD, :] = jnp.zeros((POOL_PAD, LANE), F32)
        pad_ref[POOL_PAD:POOL_PAD + s_len, :] = u_ref[...]

        def run(w):
            def chunk(i, carry):
                base = pl.multiple_of(i * ch, ch)
                acc = jnp.zeros((ch, LANE), F32)
                for j in range(w):
                    acc = acc + pad_ref[pl.ds(base + POOL_PAD - j, ch), :]
                d = acc / _pool_count(base, ch, w) - u_ref[pl.ds(base, ch), :]
                md = jnp.dot(d.astype(BF16), pw_ref[0], preferred_element_type=F32)
                m_ref[pl.ds(base, ch), :] = (md * sc_ref[...]).astype(m_ref.dtype)
                return carry

            lax.fori_loop(0, s_len // ch, chunk, 0)

        for g, w in enumerate(POOL_WINDOWS):
            pl.when(gi == g)(functools.partial(run, w))

    return pl.pallas_call(
        body,
        grid=(POOL_G,),
        in_specs=[pl.BlockSpec((s_len, LANE), lambda g: (0, ZP // LANE + g)), pl.BlockSpec((1, POOL_GD, POOL_GD), lambda g: (g, 0, 0)),
                  pl.BlockSpec((1, LANE), lambda g: (0, g))],
        out_specs=pl.BlockSpec((s_len, LANE), lambda g: (0, g)),
        out_shape=jax.ShapeDtypeStruct((s_len, POOL_C), BF16),
        scratch_shapes=[pltpu.VMEM((s_len + POOL_PAD, LANE), F32)],
        compiler_params=_cparams(("arbitrary",)),
        name="pool_fwd",
    )(z, pool_w, pool_scale)


def pool_bwd(z, dm, pool_w, pool_scale, dz):
    s_len = z.shape[0]
    ch = min(CONV_CHUNK, s_len)

    def body(u_ref, dm_ref, pw_ref, sc_ref, dz_in, du_ref, dpw_ref, dsc_ref, pad_ref, epad_ref, dd_ref, sacc_ref):
        del dz_in
        gi = pl.program_id(0)
        pad_ref[0:POOL_PAD, :] = jnp.zeros((POOL_PAD, LANE), F32)
        pad_ref[POOL_PAD:POOL_PAD + s_len, :] = u_ref[...]
        epad_ref[s_len:s_len + POOL_PAD, :] = jnp.zeros((POOL_PAD, LANE), F32)
        dpw_ref[...] = jnp.zeros_like(dpw_ref)
        sacc_ref[...] = jnp.zeros_like(sacc_ref)

        def run(w):
            def first(i, carry):
                base = pl.multiple_of(i * ch, ch)
                acc = jnp.zeros((ch, LANE), F32)
                for j in range(w):
                    acc = acc + pad_ref[pl.ds(base + POOL_PAD - j, ch), :]
                cnt = _pool_count(base, ch, w)
                d = (acc / cnt - u_ref[pl.ds(base, ch), :]).astype(BF16)
                md = jnp.dot(d, pw_ref[0], preferred_element_type=F32)
                dmc = dm_ref[pl.ds(base, ch), :]
                sacc_ref[...] += (dmc * md).reshape(ch // 8, 8, LANE).sum(axis=0)
                dmd = (dmc * sc_ref[...]).astype(BF16)
                dpw_ref[0] += lax.dot_general(d, dmd, (((0,), (0,)), ((), ())), preferred_element_type=F32)
                dd = lax.dot_general(dmd, pw_ref[0], (((1,), (1,)), ((), ())), preferred_element_type=F32)
                dd_ref[pl.ds(base, ch), :] = dd
                epad_ref[pl.ds(base, ch), :] = dd / cnt
                return carry

            lax.fori_loop(0, s_len // ch, first, 0)

            def second(i, carry):
                base = pl.multiple_of(i * ch, ch)
                acc = jnp.zeros((ch, LANE), F32)
                for j in range(w):
                    acc = acc + epad_ref[pl.ds(base + j, ch), :]
                du_ref[pl.ds(base, ch), :] = (acc - dd_ref[pl.ds(base, ch), :]).astype(du_ref.dtype)
                return carry

            lax.fori_loop(0, s_len // ch, second, 0)

        for g, w in enumerate(POOL_WINDOWS):
            pl.when(gi == g)(functools.partial(run, w))
        dsc_ref[...] = jnp.sum(sacc_ref[...], axis=0, keepdims=True)

    return pl.pallas_call(
        body,
        grid=(POOL_G,),
        in_specs=[pl.BlockSpec((s_len, LANE), lambda g: (0, ZP // LANE + g)), pl.BlockSpec((s_len, LANE), lambda g: (0, g)),
                  pl.BlockSpec((1, POOL_GD, POOL_GD), lambda g: (g, 0, 0)), pl.BlockSpec((1, LANE), lambda g: (0, g)), ANY],
        out_specs=[pl.BlockSpec((s_len, LANE), lambda g: (0, ZP // LANE + g)), pl.BlockSpec((1, POOL_GD, POOL_GD), lambda g: (g, 0, 0)),
                   pl.BlockSpec((1, LANE), lambda g: (0, g))],
        out_shape=[jax.ShapeDtypeStruct(dz.shape, dz.dtype), jax.ShapeDtypeStruct((POOL_G, POOL_GD, POOL_GD), F32),
                   jax.ShapeDtypeStruct((1, POOL_C), F32)],
        scratch_shapes=[pltpu.VMEM((s_len + POOL_PAD, LANE), F32), pltpu.VMEM((s_len + POOL_PAD, LANE), F32),
                        pltpu.VMEM((s_len, LANE), F32), pltpu.VMEM((8, LANE), F32)],
        input_output_aliases={4: 0},
        compiler_params=_cparams(("arbitrary",)),
        name="pool_bwd",
    )(z, dm, pool_w, pool_scale, dz)


def _row(v):
    return v.reshape(1, -1)


def layer_fwd(x, tabs, w, tag):
    s_len = x.shape[0]
    cc, sa, sb = tabs
    sv = {"x": x}

    def rms_body(x_ref, g_ref, o_ref):
        o_ref[...] = _rms(x_ref[...], g_ref[...]).astype(o_ref.dtype)

    (h,) = rowwise("mix_norm_pre" + tag, rms_body, s_len, [_whole(x)], [_row(w["mix_norm_pre"])], [(D_MODEL, BF16)])
    z = mm(h, w["w_in"], name="in_proj" + tag)

    def prep_body(z_ref, cc_ref, sa_ref, sb_ref, qg_ref, kg_ref, qn_ref, ckv_ref, kr_ref):
        qn_ref[...] = _rms(z_ref[:, 0:Q_RANK], qg_ref[...]).astype(qn_ref.dtype)
        ckv_ref[...] = _rms(z_ref[:, Q_RANK:Q_RANK + KV_RANK], kg_ref[...]).astype(ckv_ref.dtype)
        kr_ref[...] = _rope(z_ref[:, Q_RANK + KV_RANK:ZA_W], cc_ref[...], sa_ref[...], sb_ref[...])

    qn, ckvn, kr = rowwise("attn_prep" + tag, prep_body, s_len, [(z, ZA_W, ZA // ZA_W), _whole(cc), _whole(sa), _whole(sb)],
                           [_row(w["q_norm"]), _row(w["kv_norm"])], [(Q_RANK, BF16), (KV_RANK, BF16), (HP, F32)])
    q_raw = mm(qn, w["w_uq"], name="q_proj" + tag)
    kv_raw = mm(ckvn, w["w_ukv"], name="kv_proj" + tag)

    def qkv_body(q_ref, kv_ref, kr_ref, cc_ref, sa_ref, sb_ref, qo_ref, ko_ref, vo_ref):
        c_, a_, b_, kro = cc_ref[...], sa_ref[...], sb_ref[...], kr_ref[...]
        for hh in range(N_HEADS):
            sl = slice(hh * HP, (hh + 1) * HP)
            qo_ref[:, sl] = _rope(q_ref[:, sl], c_, a_, b_).astype(qo_ref.dtype)
            ko_ref[:, sl] = (kv_ref[:, sl] + kro).astype(ko_ref.dtype)
        lane = lax.broadcasted_iota(jnp.int32, (q_ref.shape[0], HW), 1)
        vo_ref[...] = jnp.where((lane & (HP - 1)) == VDIM, 1.0, kv_ref[:, HW:2 * HW]).astype(vo_ref.dtype)

    q, k, v = rowwise("qkv_rope" + tag, qkv_body, s_len, [_whole(q_raw), _whole(kv_raw), _whole(kr), _whole(cc), _whole(sa), _whole(sb)], [],
                      [(HW, BF16)] * 3)
    o, lse = attention_fwd(q, k, v)
    y_attn = mm(o, w["w_attn_o"], out_dtype=BF16, name="attn_out" + tag)

    c = conv_fwd(z, w["conv_w"], _row(w["conv_b"]))

    def ln_body(c_ref, g_ref, b_ref, o_ref):
        o_ref[...] = _silu(_layer_norm(c_ref[...], g_ref[...], b_ref[...])).astype(o_ref.dtype)

    (cs,) = rowwise("conv_ln_silu" + tag, ln_body, s_len, [_whole(c)], [_row(w["conv_ln_g"]), _row(w["conv_ln_b"])], [(CONV_C, BF16)])
    y_conv = mm(cs, w["w_conv_o"], out_dtype=BF16, name="conv_out" + tag)

    m = pool_fwd(z, w["pool_w"], _row(w["pool_scale"]))
    y_pool = mm(m, w["w_pool_o"], out_dtype=BF16, name="pool_out" + tag)

    def merge_body(ya_ref, yc_ref, yp_ref, gl_ref, o_ref):
        gl = gl_ref[...]
        o_ref[...] = (jax.nn.sigmoid(gl[:, 0:D_MODEL]) * ya_ref[...].astype(F32) + jax.nn.sigmoid(gl[:, D_MODEL:2 * D_MODEL]) * yc_ref[...].astype(F32)
                      + jax.nn.sigmoid(gl[:, 2 * D_MODEL:3 * D_MODEL]) * yp_ref[...].astype(F32)).astype(o_ref.dtype)

    (merged,) = rowwise("gate_merge" + tag, merge_body, s_len, [_whole(y_attn), _whole(y_conv), _whole(y_pool), (z, 3 * D_MODEL, 0)], [],
                        [(D_MODEL, BF16)])
    mo = mm(merged, w["w_mix_o"], name="mix_out" + tag)

    def post_body(y_ref, x_ref, g_ref, o_ref):
        o_ref[...] = x_ref[...] + _rms(y_ref[...], g_ref[...])

    (h1,) = rowwise("mix_norm_post" + tag, post_body, s_len, [_whole(mo), _whole(x)], [_row(w["mix_norm_post"])], [(D_MODEL, F32)])

    (hn,) = rowwise("ffn_norm_pre" + tag, rms_body, s_len, [_whole(h1)], [_row(w["ffn_norm_pre"])], [(D_MODEL, BF16)])
    gu, act = ffn_in(hn, w["w_gu"], "ffn_in" + tag)
    y = mm(act, w["w_down"], name="ffn_out" + tag)
    (h2,) = rowwise("ffn_norm_post" + tag, post_body, s_len, [_whole(y), _whole(h1)], [_row(w["ffn_norm_post"])], [(D_MODEL, F32)])

    sv.update(h=h, z=z, qn=qn, ckvn=ckvn, q=q, k=k, v=v, o=o, lse=lse, c=c, cs=cs, m=m, y_attn=y_attn, y_conv=y_conv, y_pool=y_pool,
              merged=merged, mo=mo, h1=h1, hn=hn, gu=gu, act=act, y=y)
    return h2, sv


def layer_bwd(dh2, sv, tabs, w, tag):
    s_len = dh2.shape[0]
    cc, sa, sb = tabs
    g = {}

    def post_bwd_body(y_ref, dh_ref, g_ref, dy_ref, dg_ref):
        _, vjp = jax.vjp(_rms, y_ref[...], g_ref[...])
        dy, dg = vjp(dh_ref[...])
        dy_ref[...] = dy.astype(dy_ref.dtype)
        _acc(dg_ref, dg)

    def pre_bwd_body(x_ref, dhn_ref, dres_ref, g_ref, dx_ref, dg_ref):
        _, vjp = jax.vjp(_rms, x_ref[...], g_ref[...])
        dx, dg = vjp(dhn_ref[...])
        dx_ref[...] = dres_ref[...] + dx
        _acc(dg_ref, dg)

    d_y, g["ffn_norm_post"] = rowwise("ffn_norm_post_bwd" + tag, post_bwd_body, s_len, [_whole(sv["y"]), _whole(dh2)],
                                      [_row(w["ffn_norm_post"])], [(D_MODEL, BF16)], [((1, D_MODEL), F32)])
    g["w_down"] = mm(sv["act"], d_y, ta=True, name="ffn_out_dw" + tag)
    d_gu = ffn_out_dx(d_y, w["w_down"], sv["gu"], "ffn_out_dx" + tag)
    g["w_gu"] = mm(sv["hn"], d_gu, ta=True, name="ffn_in_dw" + tag)
    d_hn = mm(d_gu, w["w_gu"], tb=True, name="ffn_in_dx" + tag)
    dh1, g["ffn_norm_pre"] = rowwise(
        "ffn_norm_pre_bwd" + tag, pre_bwd_body, s_len, [_whole(sv["h1"]), _whole(d_hn), _whole(dh2)], [_row(w["ffn_norm_pre"])], [(D_MODEL, F32)], [((1, D_MODEL), F32)])

    d_mo, g["mix_norm_post"] = rowwise(
        "mix_norm_post_bwd" + tag, post_bwd_body, s_len, [_whole(sv["mo"]), _whole(dh1)], [_row(w["mix_norm_post"])], [(D_MODEL, BF16)], [((1, D_MODEL), F32)])
    g["w_mix_o"] = mm(sv["merged"], d_mo, ta=True, name="mix_out_dw" + tag)
    d_merged = mm(d_mo, w["w_mix_o"], tb=True, name="mix_out_dx" + tag)

    def merge_bwd_body(dm_ref, ya_ref, yc_ref, yp_ref, gl_ref, dya_ref, dyc_ref, dyp_ref, dgl_ref):
        dmg = dm_ref[...]
        for i, (y_ref, dy_ref) in enumerate(((ya_ref, dya_ref), (yc_ref, dyc_ref), (yp_ref, dyp_ref))):
            sg = jax.nn.sigmoid(gl_ref[:, i * D_MODEL:(i + 1) * D_MODEL])
            dy_ref[...] = (dmg * sg).astype(dy_ref.dtype)
            dgl_ref[:, i * D_MODEL:(i + 1) * D_MODEL] = (dmg * y_ref[...].astype(F32) * sg * (1.0 - sg)).astype(dgl_ref.dtype)

    d_ya, d_yc, d_yp, dz = rowwise(
        "gate_merge_bwd" + tag, merge_bwd_body, s_len,
        [_whole(d_merged), _whole(sv["y_attn"]), _whole(sv["y_conv"]), _whole(sv["y_pool"]), (sv["z"], 3 * D_MODEL, 0)], [],
        [(D_MODEL, BF16)] * 3 + [(3 * D_MODEL, BF16, ZW, 0)])

    g["w_pool_o"] = mm(sv["m"], d_yp, ta=True, name="pool_out_dw" + tag)
    d_m = mm(d_yp, w["w_pool_o"], tb=True, name="pool_out_dx" + tag)
    dz, g["pool_w"], g["pool_scale"] = pool_bwd(sv["z"], d_m, w["pool_w"], _row(w["pool_scale"]), dz)

    g["w_conv_o"] = mm(sv["cs"], d_yc, ta=True, name="conv_out_dw" + tag)
    d_cs = mm(d_yc, w["w_conv_o"], tb=True, name="conv_out_dx" + tag)

    def ln_bwd_body(c_ref, dcs_ref, g_ref, b_ref, dc_ref, dg_ref, db_ref):
        f = lambda c_, g_, b_: _silu(_layer_norm(c_, g_, b_))
        _, vjp = jax.vjp(f, c_ref[...], g_ref[...], b_ref[...])
        dc, dg, db = vjp(dcs_ref[...])
        dc_ref[...] = dc
        _acc(dg_ref, dg)
        _acc(db_ref, db)

    d_c, g["conv_ln_g"], g["conv_ln_b"] = rowwise("conv_ln_silu_bwd" + tag, ln_bwd_body, s_len, [_whole(sv["c"]), _whole(d_cs)],
                                                  [_row(w["conv_ln_g"]), _row(w["conv_ln_b"])], [(CONV_C, F32)],
                                                  [((1, CONV_C), F32), ((1, CONV_C), F32)])
    dz, g["conv_w"], g["conv_b"] = conv_bwd(sv["z"], d_c, w["conv_w"], dz)

    g["w_attn_o"] = mm(sv["o"], d_ya, ta=True, name="attn_out_dw" + tag)
    d_o = mm(d_ya, w["w_attn_o"], tb=True, out_dtype=BF16, name="attn_out_dx" + tag)
    delta = attention_delta(d_o, sv["o"])
    t_bwd = min(ATT_TILE_BWD, s_len)
    rows_of = lambda a: a.reshape(N_HEADS, s_len // t_bwd, 1, t_bwd)
    dq, dk, dv = attention_bwd(sv["q"], sv["k"], sv["v"], d_o, rows_of(sv["lse"]), rows_of(delta))

    def qkv_bwd_body(dq_ref, dk_ref, dv_ref, cc_ref, sa_ref, sb_ref, dqp_ref, dkv_ref, dkr_ref):
        c_, a_, b_ = cc_ref[...], sa_ref[...], sb_ref[...]
        dk_sum = jnp.zeros((dq_ref.shape[0], HP), F32)
        for hh in range(N_HEADS):
            sl = slice(hh * HP, (hh + 1) * HP)
            dqp_ref[:, sl] = _rope_t(dq_ref[:, sl], c_, a_, b_).astype(dqp_ref.dtype)
            dkh = dk_ref[:, sl]
            dkv_ref[:, sl] = dkh.astype(dkv_ref.dtype)
            dk_sum = dk_sum + dkh
        dkv_ref[:, HW:2 * HW] = dv_ref[...]
        dkr_ref[...] = _rope_t(dk_sum, c_, a_, b_)

    dq_pre, dkv_pre, d_kr = rowwise("qkv_rope_bwd" + tag, qkv_bwd_body, s_len,
                                    [_whole(dq), _whole(dk), _whole(dv), _whole(cc), _whole(sa), _whole(sb)], [],
                                    [(HW, BF16), (2 * HW, BF16), (HP, F32)])
    g["w_uq"] = mm(sv["qn"], dq_pre, ta=True, name="q_proj_dw" + tag)
    d_qn = mm(dq_pre, w["w_uq"], tb=True, name="q_proj_dx" + tag)
    g["w_ukv"] = mm(sv["ckvn"], dkv_pre, ta=True, name="kv_proj_dw" + tag)
    d_ckvn = mm(dkv_pre, w["w_ukv"], tb=True, name="kv_proj_dx" + tag)

    def prep_bwd_body(z_ref, dqn_ref, dckv_ref, dkr_ref, qg_ref, kg_ref, dz_ref, dqg_ref, dkg_ref):
        _, vq = jax.vjp(_rms, z_ref[:, 0:Q_RANK], qg_ref[...])
        dcq, dqg = vq(dqn_ref[...])
        _, vk = jax.vjp(_rms, z_ref[:, Q_RANK:Q_RANK + KV_RANK], kg_ref[...])
        dckv, dkg = vk(dckv_ref[...])
        dz_ref[:, 0:Q_RANK] = dcq.astype(dz_ref.dtype)
        dz_ref[:, Q_RANK:Q_RANK + KV_RANK] = dckv.astype(dz_ref.dtype)
        dz_ref[:, Q_RANK + KV_RANK:ZA_W] = dkr_ref[...].astype(dz_ref.dtype)
        _acc(dqg_ref, dqg)
        _acc(dkg_ref, dkg)

    dz, g["q_norm"], g["kv_norm"] = rowwise("attn_prep_bwd" + tag, prep_bwd_body, s_len,
                                            [(sv["z"], ZA_W, ZA // ZA_W), _whole(d_qn), _whole(d_ckvn), _whole(d_kr)],
                                            [_row(w["q_norm"]), _row(w["kv_norm"])], [(ZA_W, BF16, ZW, ZA // ZA_W)],
                                            [((1, Q_RANK), F32), ((1, KV_RANK), F32)], into={0: dz})

    g["w_in"] = mm(sv["h"], dz, ta=True, name="in_proj_dw" + tag)
    d_h = mm(dz, w["w_in"], tb=True, name="in_proj_dx" + tag)
    dx, g["mix_norm_pre"] = rowwise(
        "mix_norm_pre_bwd" + tag, pre_bwd_body, s_len, [_whole(sv["x"]), _whole(d_h), _whole(dh1)], [_row(w["mix_norm_pre"])], [(D_MODEL, F32)], [((1, D_MODEL), F32)])
    return dx, g


def loss_head(h, target):
    s_len = h.shape[0]

    def body(h_ref, t_ref, dy_ref, loss_ref):
        err = h_ref[...] - t_ref[...]
        dy_ref[...] = err * (1.0 / D_MODEL)
        part = 0.5 * jnp.sum(jnp.mean(err * err, axis=-1, keepdims=True), axis=0, keepdims=True)
        _acc(loss_ref, jnp.broadcast_to(part, (1, LANE)))

    return rowwise("loss_head", body, s_len, [_whole(h), _whole(target)], [], [(D_MODEL, F32)], [((1, LANE), F32)])


def local_step(x, pos_col, target, layers):
    s_len = x.shape[0]
    tabs = rope_tables(pos_col, s_len)
    h, saved = x, []
    for li, w in enumerate(layers):
        h, sv = layer_fwd(h, tabs, w, f"_l{li}")
        saved.append(sv)
    dh, loss = loss_head(h, target)
    grads = [None] * len(layers)
    for li in reversed(range(len(layers))):
        dh, grads[li] = layer_bwd(dh, saved[li], tabs, layers[li], f"_l{li}")
    return loss[0, 0], dh, grads


def _pad_heads_cols(wm, per_head):
    r = wm.shape[0]
    return jnp.pad(wm.reshape(r, N_HEADS, per_head), ((0, 0), (0, 0), (0, HP - per_head))).reshape(r, HW)


def _unpad_heads_cols(wm, per_head):
    r = wm.shape[0]
    return wm.reshape(r, N_HEADS, HP)[:, :, :per_head].reshape(r, N_HEADS * per_head)


def align_weights(p):
    w_in = p["w_in"]
    r = w_in.shape[0]
    zeros = lambda n: jnp.zeros((r, n), w_in.dtype)
    conv = w_in[:, O_CONV:O_POOL].reshape(r, 2, CONV_C // LANE, LANE).transpose(0, 2, 1, 3).reshape(r, 2 * CONV_C)
    w_in_al = jnp.concatenate([
        w_in[:, O_GATE:D_IN], conv, w_in[:, O_POOL:O_GATE], w_in[:, O_Q:O_KR],
        zeros(KR_LANE), w_in[:, O_KR:O_CONV], zeros(HP - KR_LANE - ROPE)], axis=1)
    out = dict(p)
    out["w_in"] = w_in_al
    out["w_uq"] = _pad_heads_cols(p["w_uq"], NOPE + ROPE)
    out["w_ukv"] = jnp.concatenate([_pad_heads_cols(p["w_uk"], NOPE), _pad_heads_cols(p["w_uv"], VDIM)], axis=1)
    wo = p["w_attn_o"]
    out["w_attn_o"] = jnp.pad(wo.reshape(N_HEADS, VDIM, D_MODEL), ((0, 0), (0, HP - VDIM), (0, 0))).reshape(HW, D_MODEL)
    out["w_gu"] = jnp.concatenate([p["w_gate"][:, :FF_HALF], p["w_up"][:, :FF_HALF], p["w_gate"][:, FF_HALF:], p["w_up"][:, FF_HALF:]], axis=1)
    for name in ("w_uk", "w_uv", "w_gate", "w_up"):
        del out[name]
    return out


def unalign_grads(g):
    gi = g["w_in"]
    kr0 = ZA + Q_RANK + KV_RANK + KR_LANE
    out = dict(g)
    r = gi.shape[0]
    conv = gi[:, ZC:ZP].reshape(r, CONV_C // LANE, 2, LANE).transpose(0, 2, 1, 3).reshape(r, 2 * CONV_C)
    out["w_in"] = jnp.concatenate([gi[:, ZA:ZA + Q_RANK + KV_RANK], gi[:, kr0:kr0 + ROPE], conv, gi[:, ZP:ZA], gi[:, ZG:ZC]], axis=1)
    out["w_uq"] = _unpad_heads_cols(g["w_uq"], NOPE + ROPE)
    out["w_uk"] = _unpad_heads_cols(g["w_ukv"][:, :HW], NOPE)
    out["w_uv"] = _unpad_heads_cols(g["w_ukv"][:, HW:], VDIM)
    out["w_attn_o"] = g["w_attn_o"].reshape(N_HEADS, HP, D_MODEL)[:, :VDIM].reshape(N_HEADS * VDIM, D_MODEL)
    gu = g["w_gu"]
    out["w_gate"] = jnp.concatenate([gu[:, 0:FF_HALF], gu[:, 2 * FF_HALF:3 * FF_HALF]], axis=1)
    out["w_up"] = jnp.concatenate([gu[:, FF_HALF:2 * FF_HALF], gu[:, 3 * FF_HALF:]], axis=1)
    del out["w_ukv"], out["w_gu"]
    return out


MESH = pl.DeviceIdType.MESH
ANY = pl.BlockSpec(memory_space=pl.ANY)


def _place():
    return lax.axis_index("x"), lax.axis_index("y"), lax.axis_index("c")


def _other_chips(x, y):
    return [(1 - x, y), (x, 1 - y), (1 - x, 1 - y)]


def gather_shards(local):
    n = len(local)

    def body(*refs):
        w_refs, out_refs, (send_sems, recv_sems) = refs[:n], refs[n:2 * n], refs[2 * n:]
        x, y, c = _place()
        me = 2 * x + y
        chips = _other_chips(x, y)
        slots = [2 * cx + cy for cx, cy in chips]
        sibling = (x, y, 1 - c)

        def copy(i, k, slot, layer, to, src=None):
            dst = out_refs[i].at[slot, layer]
            return pltpu.make_async_remote_copy(src_ref=dst if src is None else src, dst_ref=dst, send_sem=send_sems.at[6 * i + k],
                                                recv_sem=recv_sems.at[6 * i + k], device_id=to, device_id_type=MESH)

        first = [copy(i, j, me, c, (*chip, c), src=w_refs[i].at[c]) for i in range(n) for j, chip in enumerate(chips)]
        for cp in first:
            cp.start()
        passed = []
        for i in range(n):
            for j in range(3):
                copy(i, j, slots[j], c, sibling).wait_recv()
                fwd = copy(i, 3 + j, slots[j], c, sibling)
                fwd.start()
                passed.append(fwd)
        for i in range(n):
            for j in range(3):
                copy(i, 3 + j, slots[j], 1 - c, sibling).wait_recv()
        for cp in first + passed:
            cp.wait_send()

    return pl.pallas_call(
        body,
        in_specs=[ANY] * n,
        out_specs=[ANY] * n,
        out_shape=[jax.ShapeDtypeStruct((N_CHIPS, *a.shape), a.dtype) for a in local],
        scratch_shapes=[pltpu.SemaphoreType.DMA((6 * n,)), pltpu.SemaphoreType.DMA((6 * n,))],
        name="gather_shards",
    )(*local)


def sibling_swap(gs):
    n = len(gs)

    def body(*refs):
        g_refs, out_refs, (send_sems, recv_sems) = refs[:n], refs[n:2 * n], refs[2 * n:]
        x, y, c = _place()
        copies = []
        for i in range(n):
            for j in range(N_CHIPS):
                cp = pltpu.make_async_remote_copy(src_ref=g_refs[i].at[j, 1 - c], dst_ref=out_refs[i].at[j], send_sem=send_sems.at[4 * i + j],
                                                  recv_sem=recv_sems.at[4 * i + j], device_id=(x, y, 1 - c), device_id_type=MESH)
                cp.start()
                copies.append(cp)
        for cp in copies:
            cp.wait_recv()
        for cp in copies:
            cp.wait_send()

    return pl.pallas_call(
        body,
        in_specs=[ANY] * n,
        out_specs=[ANY] * n,
        out_shape=[jax.ShapeDtypeStruct((N_CHIPS, *a.shape[2:]), a.dtype) for a in gs],
        scratch_shapes=[pltpu.SemaphoreType.DMA((4 * n,)), pltpu.SemaphoreType.DMA((4 * n,))],
        name="sibling_swap",
    )(*gs)


def chip_exchange(ps):
    n = len(ps)

    def body(*refs):
        p_refs, out_refs, (send_sems, recv_sems) = refs[:n], refs[n:2 * n], refs[2 * n:]
        x, y, c = _place()
        chips = _other_chips(x, y)
        copies = []
        for i in range(n):
            for j, chip in enumerate(chips):
                cp = pltpu.make_async_remote_copy(src_ref=p_refs[i].at[2 * chip[0] + chip[1]], dst_ref=out_refs[i].at[j],
                                                  send_sem=send_sems.at[3 * i + j], recv_sem=recv_sems.at[3 * i + j],
                                                  device_id=(*chip, c), device_id_type=MESH)
                cp.start()
                copies.append(cp)
        for cp in copies:
            cp.wait_recv()
        for cp in copies:
            cp.wait_send()

    return pl.pallas_call(
        body,
        in_specs=[ANY] * n,
        out_specs=[ANY] * n,
        out_shape=[jax.ShapeDtypeStruct((3, *a.shape[1:]), a.dtype) for a in ps],
        scratch_shapes=[pltpu.SemaphoreType.DMA((3 * n,)), pltpu.SemaphoreType.DMA((3 * n,))],
        name="chip_exchange",
    )(*ps)


def sibling_gather(fs):
    n = len(fs)

    def body(*refs):
        out_refs, (send_sems, recv_sems) = refs[n:2 * n], refs[2 * n:]
        x, y, c = _place()
        copies = []
        for i in range(n):
            cp = pltpu.make_async_remote_copy(src_ref=out_refs[i].at[c], dst_ref=out_refs[i].at[c], send_sem=send_sems.at[i],
                                              recv_sem=recv_sems.at[i], device_id=(x, y, 1 - c), device_id_type=MESH)
            cp.start()
            copies.append(cp)
        for i in range(n):
            pltpu.make_async_remote_copy(src_ref=out_refs[i].at[1 - c], dst_ref=out_refs[i].at[1 - c], send_sem=send_sems.at[i],
                                         recv_sem=recv_sems.at[i], device_id=(x, y, 1 - c), device_id_type=MESH).wait_recv()
        for cp in copies:
            cp.wait_send()

    return pl.pallas_call(
        body,
        in_specs=[ANY] * n,
        out_specs=[ANY] * n,
        out_shape=[jax.ShapeDtypeStruct(a.shape, a.dtype) for a in fs],
        scratch_shapes=[pltpu.SemaphoreType.DMA((n,)), pltpu.SemaphoreType.DMA((n,))],
        input_output_aliases={i: i for i in range(n)},
        name="sibling_gather",
    )(*fs)


def gather_all(vs):
    n = len(vs)

    def body(*refs):
        v_refs, out_refs, (send_sems, recv_sems, local_sems) = refs[:n], refs[n:2 * n], refs[2 * n:]
        x, y, c = _place()
        me, sibling = (x, y, c), (x, y, 1 - c)
        chips = _other_chips(x, y)

        def copy(i, k, block, to, src=None):
            px, py, pc = block
            dst = out_refs[i].at[4 * px + 2 * py + pc]
            return pltpu.make_async_remote_copy(src_ref=dst if src is None else src, dst_ref=dst, send_sem=send_sems.at[7 * i + k],
                                                recv_sem=recv_sems.at[7 * i + k], device_id=to, device_id_type=MESH)

        mine = [pltpu.make_async_copy(v_refs[i], out_refs[i].at[4 * x + 2 * y + c], local_sems.at[i]) for i in range(n)]
        for cp in mine:
            cp.start()
        first = []
        for i in range(n):
            first.append(copy(i, 0, me, sibling, src=v_refs[i]))
            first += [copy(i, 1 + j, me, (*chip, c), src=v_refs[i]) for j, chip in enumerate(chips)]
        for cp in first:
            cp.start()
        passed = []
        for i in range(n):
            for j, chip in enumerate(chips):
                copy(i, 1 + j, (*chip, c), me).wait_recv()
                fwd = copy(i, 4 + j, (*chip, c), sibling)
                fwd.start()
                passed.append(fwd)
        for i in range(n):
            copy(i, 0, sibling, me).wait_recv()
            for j, chip in enumerate(chips):
                copy(i, 4 + j, (*chip, 1 - c), me).wait_recv()
        for cp in first + passed:
            cp.wait_send()
        for cp in mine:
            cp.wait()

    return pl.pallas_call(
        body,
        in_specs=[ANY] * n,
        out_specs=[ANY] * n,
        out_shape=[jax.ShapeDtypeStruct((8, *a.shape), a.dtype) for a in vs],
        scratch_shapes=[pltpu.SemaphoreType.DMA((7 * n,)), pltpu.SemaphoreType.DMA((7 * n,)), pltpu.SemaphoreType.DMA((n,))],
        name="gather_all",
    )(*vs)


def _row_tile(rows, row_bytes):
    best = None
    for t in range(16, rows + 1, 16):
        if rows % t == 0 and t * row_bytes <= SUM_TILE_BYTES:
            best = t
    return best or rows


def sibling_sum(g, theirs, place, name):
    _, _, rows, cols = g.shape
    tile = _row_tile(rows, cols * 4)

    def body(place_ref, g_ref, t_ref, o_ref):
        o_ref[...] = (g_ref[0].astype(F32) + t_ref[...].astype(F32)).astype(o_ref.dtype)

    spec = pl.BlockSpec((1, tile, cols), lambda j, i, place_ref: (j, i, 0))
    return pl.pallas_call(
        body,
        grid_spec=pltpu.PrefetchScalarGridSpec(
            num_scalar_prefetch=1, grid=(N_CHIPS, rows // tile),
            in_specs=[pl.BlockSpec((1, 1, tile, cols), lambda j, i, place_ref: (j, place_ref[1], i, 0)), spec], out_specs=spec),
        out_shape=jax.ShapeDtypeStruct(theirs.shape, BF16),
        compiler_params=_cparams(("parallel", "parallel")),
        name=name,
    )(place, g, theirs)


def chip_sum(p, others, place, name):
    _, rows, cols = p.shape
    tile = _row_tile(rows, cols * 4)

    def body(place_ref, p_ref, o3_ref, o_ref):
        acc = p_ref[0].astype(F32)
        for k in range(3):
            acc = acc + o3_ref[k].astype(F32)
        o_ref[0] = acc

    return pl.pallas_call(
        body,
        grid_spec=pltpu.PrefetchScalarGridSpec(
            num_scalar_prefetch=1, grid=(rows // tile,),
            in_specs=[pl.BlockSpec((1, tile, cols), lambda i, place_ref: (place_ref[0], i, 0)),
                      pl.BlockSpec((3, tile, cols), lambda i, place_ref: (0, i, 0))],
            out_specs=pl.BlockSpec((1, tile, cols), lambda i, place_ref: (place_ref[1], i, 0))),
        out_shape=jax.ShapeDtypeStruct((2, rows, cols), F32),
        compiler_params=_cparams(("parallel",)),
        name=name,
    )(place, p, others)


def sum_devices(a, name):
    n, rows, cols = a.shape
    tile = _row_tile(rows, cols * 4 * n)

    def body(a_ref, o_ref):
        acc = a_ref[0]
        for s in range(1, n):
            acc = acc + a_ref[s]
        o_ref[...] = acc

    return pl.pallas_call(body, grid=(rows // tile,), in_specs=[pl.BlockSpec((n, tile, cols), lambda i: (0, i, 0))],
                          out_specs=pl.BlockSpec((tile, cols), lambda i: (i, 0)), out_shape=jax.ShapeDtypeStruct((rows, cols), F32),
                          compiler_params=_cparams(("parallel",)), name=name)(a)


def adamw(w, g, m, v, name):
    layers, rows, cols = w.shape
    tile = rows
    for t in (512, 256, 128, 64, 32, 16, 8):
        if rows % t == 0 and t * cols * 4 <= 2 * 1024 * 1024:
            tile = t
            break

    def body(w_ref, g_ref, m_ref, v_ref, d_ref, mo_ref, vo_ref):
        gg = g_ref[...]
        m_new = ADAM_B1 * m_ref[...] + (1.0 - ADAM_B1) * gg
        v_new = ADAM_B2 * v_ref[...] + (1.0 - ADAM_B2) * (gg * gg)
        m_hat = m_new / (1.0 - ADAM_B1 ** ADAM_STEP)
        v_hat = v_new / (1.0 - ADAM_B2 ** ADAM_STEP)
        d_ref[...] = -ADAM_LR * (m_hat / (jnp.sqrt(v_hat) + ADAM_EPS) + ADAM_WD * w_ref[...])
        mo_ref[...] = m_new
        vo_ref[...] = v_new

    spec = pl.BlockSpec((1, tile, cols), lambda l, i: (l, i, 0))
    shape = jax.ShapeDtypeStruct((layers, rows, cols), F32)
    return pl.pallas_call(body, grid=(layers, rows // tile), in_specs=[spec] * 4, out_specs=[spec] * 3, out_shape=[shape] * 3,
                          compiler_params=_cparams(("parallel", "parallel")), name=name)(w, g, m, v)


WEIGHTS = ["mix_norm_pre", "w_in", "q_norm", "w_uq", "kv_norm", "w_uk", "w_uv", "w_attn_o", "conv_w", "conv_b", "conv_ln_g", "conv_ln_b",
           "w_conv_o", "pool_w", "pool_scale", "w_pool_o", "w_mix_o", "mix_norm_post", "ffn_norm_pre", "w_gate", "w_up", "w_down",
           "ffn_norm_post"]
SHARDED = {"w_in": 2, "w_uq": 2, "w_uk": 2, "w_uv": 2, "w_attn_o": 2, "conv_w": 2, "w_conv_o": 2, "w_pool_o": 2, "w_mix_o": 1,
           "w_gate": 2, "w_up": 2, "w_down": 1}
REPLICATED = [n for n in WEIGHTS if n not in SHARDED]
ROW_PARAMS = [n for n in REPLICATED if n != "pool_w"]
N_CHIPS = 4
GROUPS = [(("w_in",), 1), (("w_uq",), 1), (("w_uk", "w_uv"), 1), (("w_attn_o", "w_conv_o", "w_pool_o"), 1), (("conv_w",), 1),
          (("w_mix_o",), 1), (("w_gate", "w_up"), 2), (("w_down",), 1)]


def _join(parts, axis):
    return parts[0] if len(parts) == 1 else jnp.concatenate(parts, axis=axis)


def _split_group(arr, names, axis, shapes):
    out, off = {}, 0
    ax = arr.ndim - 3 + axis
    for n in names:
        size = shapes[n][axis]
        out[n] = lax.slice_in_dim(arr, off, off + size, axis=ax)
        off += size
    return out


def _pack_rows(vectors):
    blocks = []
    for v in vectors:
        for li in range(v.shape[0]):
            blocks.append(jnp.pad(v[li][None, :], ((0, PACK_ROWS - 1), (0, PACK_W - v.shape[1]))))
    return jnp.concatenate(blocks, axis=0)


def _unpack_rows(packed, shapes):
    out, r = [], 0
    for layers, width in shapes:
        out.append(jnp.stack([packed[r + PACK_ROWS * li, :width] for li in range(layers)]))
        r += PACK_ROWS * layers
    return out


def _shard_split(name, full):
    return jnp.split(full, N_CHIPS, axis=SHARDED[name])


def kernel(x, positions, mix_norm_pre, w_in, q_norm, w_uq, kv_norm, w_uk, w_uv, w_attn_o, conv_w, conv_b, conv_ln_g, conv_ln_b, w_conv_o, pool_w, pool_scale, w_pool_o, w_mix_o, mix_norm_post, ffn_norm_pre, w_gate, w_up, w_down, ffn_norm_post, loss_target, m_mix_norm_pre, m_w_in, m_q_norm, m_w_uq, m_kv_norm, m_w_uk, m_w_uv, m_w_attn_o, m_conv_w, m_conv_b, m_conv_ln_g, m_conv_ln_b, m_w_conv_o, m_pool_w, m_pool_scale, m_w_pool_o, m_w_mix_o, m_mix_norm_post, m_ffn_norm_pre, m_w_gate, m_w_up, m_w_down, m_ffn_norm_post, v_mix_norm_pre, v_w_in, v_q_norm, v_w_uq, v_kv_norm, v_w_uk, v_w_uv, v_w_attn_o, v_conv_w, v_conv_b, v_conv_ln_g, v_conv_ln_b, v_w_conv_o, v_pool_w, v_pool_scale, v_w_pool_o, v_w_mix_o, v_mix_norm_post, v_ffn_norm_pre, v_w_gate, v_w_up, v_w_down, v_ffn_norm_post):
    given = dict(mix_norm_pre=mix_norm_pre, w_in=w_in, q_norm=q_norm, w_uq=w_uq, kv_norm=kv_norm, w_uk=w_uk, w_uv=w_uv, w_attn_o=w_attn_o,
                 conv_w=conv_w, conv_b=conv_b, conv_ln_g=conv_ln_g, conv_ln_b=conv_ln_b, w_conv_o=w_conv_o, pool_w=pool_w,
                 pool_scale=pool_scale, w_pool_o=w_pool_o, w_mix_o=w_mix_o, mix_norm_post=mix_norm_post, ffn_norm_pre=ffn_norm_pre,
                 w_gate=w_gate, w_up=w_up, w_down=w_down, ffn_norm_post=ffn_norm_post)
    mom = dict(mix_norm_pre=m_mix_norm_pre, w_in=m_w_in, q_norm=m_q_norm, w_uq=m_w_uq, kv_norm=m_kv_norm, w_uk=m_w_uk, w_uv=m_w_uv,
               w_attn_o=m_w_attn_o, conv_w=m_conv_w, conv_b=m_conv_b, conv_ln_g=m_conv_ln_g, conv_ln_b=m_conv_ln_b, w_conv_o=m_w_conv_o,
               pool_w=m_pool_w, pool_scale=m_pool_scale, w_pool_o=m_w_pool_o, w_mix_o=m_w_mix_o, mix_norm_post=m_mix_norm_post,
               ffn_norm_pre=m_ffn_norm_pre, w_gate=m_w_gate, w_up=m_w_up, w_down=m_w_down, ffn_norm_post=m_ffn_norm_post)
    var = dict(mix_norm_pre=v_mix_norm_pre, w_in=v_w_in, q_norm=v_q_norm, w_uq=v_w_uq, kv_norm=v_kv_norm, w_uk=v_w_uk, w_uv=v_w_uv,
               w_attn_o=v_w_attn_o, conv_w=v_conv_w, conv_b=v_conv_b, conv_ln_g=v_conv_ln_g, conv_ln_b=v_conv_ln_b, w_conv_o=v_w_conv_o,
               pool_w=v_pool_w, pool_scale=v_pool_scale, w_pool_o=v_w_pool_o, w_mix_o=v_w_mix_o, mix_norm_post=v_mix_norm_post,
               ffn_norm_pre=v_ffn_norm_pre, w_gate=v_w_gate, w_up=v_w_up, w_down=v_w_down, ffn_norm_post=v_ffn_norm_post)
    s_len = x.shape[1]
    sharded_names = [n for n in WEIGHTS if n in SHARDED]
    chip = 2 * lax.axis_index("x") + lax.axis_index("y")
    place = jnp.stack([chip, lax.axis_index("c")]).astype(jnp.int32)
    shard_shape = {n: given[n].shape for n in sharded_names}

    def to_wire(name):
        a = given[name]
        if name == "conv_w":
            hi = a.astype(BF16)
            lo = (a - hi.astype(F32)).astype(BF16)
            return jnp.concatenate([hi, lo], axis=1)
        return a.astype(BF16)

    wire_shape = {n: (N_LAYERS, 2 * CONV_W, shard_shape[n][2]) if n == "conv_w" else shard_shape[n] for n in sharded_names}
    local = [_join([to_wire(n) for n in names], axis) for names, axis in GROUPS]
    gathered = gather_shards(local)
    full = {}
    for (names, axis), loc, got in zip(GROUPS, local, gathered):
        got = lax.dynamic_update_slice(got, loc[None], (chip, 0, 0, 0))
        per_chip = [_split_group(got[j], names, axis, wire_shape) for j in range(N_CHIPS)]
        for n in names:
            parts = [pc[n] for pc in per_chip]
            if n == "conv_w":
                parts = [p[:, :CONV_W].astype(F32) + p[:, CONV_W:].astype(F32) for p in parts]
            full[n] = jnp.concatenate(parts, axis=SHARDED[n])
    layers = []
    for li in range(N_LAYERS):
        p = {n: full[n][li] for n in sharded_names}
        p.update({n: given[n][li] for n in REPLICATED})
        p["pool_w"] = p["pool_w"].astype(BF16)
        layers.append(align_weights(p))

    loss_local, grad_x, grads = local_step(x[0], positions.reshape(s_len, 1), loss_target[0], layers)
    loss = lax.psum(loss_local, MESH_AXES)
    grads = [unalign_grads(g) for g in grads]
    grad_full = {n: jnp.stack([g[n].reshape(given[n].shape[1:]) if n in REPLICATED else g[n] for g in grads]) for n in WEIGHTS}

    split = {n: _shard_split(n, grad_full[n]) for n in sharded_names}
    wires = [jnp.stack([_join([split[n][j].astype(BF16) for n in names], axis) for j in range(N_CHIPS)]) for names, axis in GROUPS]
    theirs = sibling_swap(wires)
    chip_parts = [sibling_sum(g, t, place, f"sibling_sum_{i}") for i, (g, t) in enumerate(zip(wires, theirs))]
    others = chip_exchange(chip_parts)
    layer_sums = [chip_sum(p, o, place, f"chip_sum_{i}") for i, (p, o) in enumerate(zip(chip_parts, others))]
    g_shard = {}
    for (names, axis), s in zip(GROUPS, sibling_gather(layer_sums)):
        g_shard.update(_split_group(s, names, axis, shard_shape))

    row_shapes = [given[n].shape for n in ROW_PARAMS]
    rows_all, pool_w_all = gather_all([_pack_rows([grad_full[n] for n in ROW_PARAMS]), grad_full["pool_w"].reshape(-1, POOL_GD)])
    g_rows = sum_devices(rows_all, "row_params_sum")
    g_pool_w = sum_devices(pool_w_all, "pool_w_sum")
    g_rep = dict(zip(ROW_PARAMS, _unpack_rows(g_rows, row_shapes)))
    g_rep["pool_w"] = g_pool_w.reshape(given["pool_w"].shape)

    g_out, d_out, m_out, v_out = {}, {}, {}, {}
    for n in sharded_names + ["pool_w"]:
        shp = given[n].shape
        three_d = (shp[0], int(np.prod(shp[1:-1])), shp[-1])
        g_n = g_shard[n] if n in SHARDED else g_rep[n]
        d, mn, vn = adamw(given[n].reshape(three_d), g_n.reshape(three_d), mom[n].reshape(three_d), var[n].reshape(three_d), "adamw_" + n)
        g_out[n], d_out[n], m_out[n], v_out[n] = g_n, d.reshape(shp), mn.reshape(shp), vn.reshape(shp)
    rd, rm, rv = adamw(_pack_rows([given[n] for n in ROW_PARAMS])[None], g_rows[None], _pack_rows([mom[n] for n in ROW_PARAMS])[None],
                       _pack_rows([var[n] for n in ROW_PARAMS])[None], "adamw_row_params")
    for n, d, mn, vn in zip(ROW_PARAMS, *[_unpack_rows(a[0], row_shapes) for a in (rd, rm, rv)]):
        g_out[n], d_out[n], m_out[n], v_out[n] = g_rep[n], d, mn, vn

    return (loss, grad_x[None], *[g_out[n] for n in WEIGHTS], *[d_out[n] for n in WEIGHTS], *[m_out[n] for n in WEIGHTS],
            *[v_out[n] for n in WEIGHTS])
```

```python
import functools
import math

import numpy as np
import jax
import jax.numpy as jnp
from jax import lax
from jax.experimental import pallas as pl
from jax.experimental.pallas import tpu as pltpu

F32, BF16 = jnp.float32, jnp.bfloat16

D_MODEL = 1024
N_HEADS = 8
NOPE, ROPE, VDIM = 64, 32, 64
HALF_ROPE = ROPE // 2
Q_RANK, KV_RANK = 384, 256
CONV_C, CONV_W = 512, 31
POOL_C, POOL_G, POOL_GD = 512, 4, 128
POOL_WINDOWS = (2, 4, 8, 16)
D_FF = 2816
FF_HALF = D_FF // 2
N_LAYERS = 2
EPS = 1e-6
ROPE_THETA = 10000.0
ATT_SCALE = 1.0 / math.sqrt(NOPE + ROPE)
O_Q, O_KV, O_KR, O_CONV, O_POOL, O_GATE, D_IN = 0, 384, 640, 672, 1696, 2208, 5280

LANE = 128
HP = 128
ZG, ZC, ZP, ZA, ZW = 0, 3072, 4096, 4608, 5376
ZA_W = Q_RANK + KV_RANK + HP
KR_LANE = NOPE
HW = N_HEADS * HP

ADAM_LR, ADAM_B1, ADAM_B2, ADAM_EPS, ADAM_WD, ADAM_STEP = 0.001, 0.9, 0.999, 1e-08, 0.01, 10

ROW_TILE = 256
ATT_TILE_FWD = 1024
ATT_TILE_BWD = 512
ATT_HEADS = 2
CONV_CHUNK = 256
MM_TM, MM_TN, MM_TK = 1024, 1408, 1024
MM_TILE_MAX = 2048
FFN_TM = 512
MM_VMEM_BUDGET = 40 * 1024 * 1024
HBM_BYTES_PER_US = 3.0e6
GRID_STEP_US = 0.35
VMEM_LIMIT = 56 * 1024 * 1024
SUM_TILE_BYTES = 1024 * 1024

HALF_ALIGN = 16
MESH_AXES = ("x", "y", "c")
PACK_W = 1024
PACK_ROWS = 8


def _cparams(sem):
    return pltpu.CompilerParams(dimension_semantics=sem, vmem_limit_bytes=VMEM_LIMIT)


def _tile(n, target):
    if n <= target:
        return n
    best = None
    for t in range(LANE, target + 1, LANE):
        if n % t == 0:
            best = t
    assert best is not None, (n, target)
    return best


def _mm_tiles(m, n, k, a_bytes, b_bytes, out_bytes):
    divs = lambda d: sorted({t for t in range(LANE, min(d, MM_TILE_MAX) + 1, LANE) if d % t == 0} | ({d} if d <= MM_TILE_MAX else set()))
    best = None
    for tm in divs(m):
        for tn in divs(n):
            blocks = tm * k * a_bytes + k * tn * b_bytes + tm * tn * out_bytes
            if 2 * blocks + tm * tn * 4 > MM_VMEM_BUDGET:
                continue
            steps = (m // tm) * (n // tn)
            for rows_outer in (True, False):
                moved = (m * k * a_bytes + k * n * b_bytes * (m // tm)) if rows_outer else (k * n * b_bytes + m * k * a_bytes * (n // tn))
                cost = (moved + m * n * out_bytes + blocks) / HBM_BYTES_PER_US + steps * GRID_STEP_US
                if best is None or cost < best[0]:
                    best = (cost, tm, tn, rows_outer)
    if best is not None:
        return best[1], best[2], k, best[3]
    return _tile(m, MM_TM), _tile(n, MM_TN), _tile(k, MM_TK), True


def mm(a, b, *, ta=False, tb=False, out_dtype=F32, name):
    m, k = (a.shape[1], a.shape[0]) if ta else a.shape
    n, k2 = b.shape if tb else (b.shape[1], b.shape[0])
    assert k == k2, (a.shape, b.shape, ta, tb)
    tm, tn, tk, rows_outer = _mm_tiles(m, n, k, a.dtype.itemsize, b.dtype.itemsize, jnp.dtype(out_dtype).itemsize)
    nk = k // tk
    dims = (((0 if ta else 1,), (1 if tb else 0,)), ((), ()))

    def body(a_ref, b_ref, o_ref, *acc):
        part = lax.dot_general(a_ref[...].astype(BF16), b_ref[...].astype(BF16), dims, preferred_element_type=F32)
        if nk == 1:
            o_ref[...] = part.astype(o_ref.dtype)
            return
        (acc_ref,) = acc
        kk = pl.program_id(2)

        @pl.when(kk == 0)
        def _():
            acc_ref[...] = part

        @pl.when(kk > 0)
        def _():
            acc_ref[...] += part

        @pl.when(kk == nk - 1)
        def _():
            o_ref[...] = acc_ref[...].astype(o_ref.dtype)

    ij = (lambda g0, g1: (g0, g1)) if rows_outer else (lambda g0, g1: (g1, g0))

    def a_map(g0, g1, kk):
        i, _ = ij(g0, g1)
        return (kk, i) if ta else (i, kk)

    def b_map(g0, g1, kk):
        _, j = ij(g0, g1)
        return (j, kk) if tb else (kk, j)

    a_spec = pl.BlockSpec((tk, tm) if ta else (tm, tk), a_map)
    b_spec = pl.BlockSpec((tn, tk) if tb else (tk, tn), b_map)
    return pl.pallas_call(
        body,
        grid=(m // tm, n // tn, nk) if rows_outer else (n // tn, m // tm, nk),
        in_specs=[a_spec, b_spec],
        out_specs=pl.BlockSpec((tm, tn), lambda g0, g1, kk: ij(g0, g1)),
        out_shape=jax.ShapeDtypeStruct((m, n), out_dtype),
        scratch_shapes=[] if nk == 1 else [pltpu.VMEM((tm, tn), F32)],
        compiler_params=_cparams(("parallel", "parallel", "arbitrary")),
        name=name,
    )(a, b)


def ffn_in(hn, w_gu, name):
    s_len, k = hn.shape
    tm = min(FFN_TM, s_len)

    def body(a_ref, b_ref, gu_ref, act_ref):
        r = jnp.dot(a_ref[...], b_ref[...], preferred_element_type=F32)
        gu_ref[...] = r.astype(gu_ref.dtype)
        act_ref[...] = (_silu(r[:, :FF_HALF]) * r[:, FF_HALF:]).astype(act_ref.dtype)

    return pl.pallas_call(
        body,
        grid=(2, s_len // tm),
        in_specs=[pl.BlockSpec((tm, k), lambda j, i: (i, 0)), pl.BlockSpec((k, 2 * FF_HALF), lambda j, i: (0, j))],
        out_specs=[pl.BlockSpec((tm, 2 * FF_HALF), lambda j, i: (i, j)), pl.BlockSpec((tm, FF_HALF), lambda j, i: (i, j))],
        out_shape=[jax.ShapeDtypeStruct((s_len, 2 * D_FF), BF16), jax.ShapeDtypeStruct((s_len, D_FF), BF16)],
        compiler_params=_cparams(("arbitrary", "parallel")),
        name=name,
    )(hn, w_gu)


def ffn_out_dx(d_y, w_down, gu, name):
    s_len, k = d_y.shape
    tm = min(FFN_TM, s_len)

    def body(a_ref, b_ref, gu_ref, dgu_ref):
        da = lax.dot_general(a_ref[...], b_ref[...], NT_DIMS, preferred_element_type=F32)
        gt = gu_ref[:, :FF_HALF].astype(F32)
        up = gu_ref[:, FF_HALF:].astype(F32)
        sg = jax.nn.sigmoid(gt)
        dgu_ref[:, :FF_HALF] = (da * up * sg * (1.0 + gt * (1.0 - sg))).astype(dgu_ref.dtype)
        dgu_ref[:, FF_HALF:] = (da * gt * sg).astype(dgu_ref.dtype)

    pair = pl.BlockSpec((tm, 2 * FF_HALF), lambda j, i: (i, j))
    return pl.pallas_call(
        body,
        grid=(2, s_len // tm),
        in_specs=[pl.BlockSpec((tm, k), lambda j, i: (i, 0)), pl.BlockSpec((FF_HALF, k), lambda j, i: (j, 0)), pair],
        out_specs=pair,
        out_shape=jax.ShapeDtypeStruct((s_len, 2 * D_FF), BF16),
        compiler_params=_cparams(("parallel", "arbitrary")),
        name=name,
    )(d_y, w_down, gu)


def rowwise(name, body, rows, row_ins, full_ins, row_outs, acc_outs=(), into=None):
    tile = min(ROW_TILE, rows)
    into = into or {}
    in_specs = [pl.BlockSpec((tile, w), lambda i, cb=cb: (i, cb)) for _, w, cb in row_ins]
    in_specs += [pl.BlockSpec(a.shape, lambda i, nd=a.ndim: (0,) * nd) for a in full_ins]
    in_specs += [ANY for _ in into]
    n_in = len(row_ins) + len(full_ins)
    aliases = {n_in + k: oi for k, oi in enumerate(into)}
    out_specs, out_shape = [], []
    for ro in row_outs:
        w, dt, full_w, cb = ro if len(ro) == 4 else (*ro, ro[0], 0)
        out_specs.append(pl.BlockSpec((tile, w), lambda i, cb=cb: (i, cb)))
        out_shape.append(jax.ShapeDtypeStruct((rows, full_w), dt))
    out_specs += [pl.BlockSpec(s, lambda i, nd=len(s): (0,) * nd) for s, _ in acc_outs]
    out_shape += [jax.ShapeDtypeStruct(s, dt) for s, dt in acc_outs]
    n_refs = n_in

    def call_body(*refs):
        body(*refs[:n_refs], *refs[n_refs + len(into):])

    outs = pl.pallas_call(
        call_body,
        grid=(rows // tile,),
        in_specs=in_specs,
        out_specs=out_specs,
        out_shape=out_shape,
        input_output_aliases=aliases,
        compiler_params=_cparams(("arbitrary",)),
        name=name,
    )(*[a for a, _, _ in row_ins], *full_ins, *into.values())
    return outs


def _whole(a):
    return (a, a.shape[1], 0)


def _acc(ref, val):
    @pl.when(pl.program_id(0) == 0)
    def _():
        ref[...] = val

    @pl.when(pl.program_id(0) > 0)
    def _():
        ref[...] += val


def _rms(x, g):
    return x * lax.rsqrt(jnp.mean(x * x, axis=-1, keepdims=True) + EPS) * g


def _layer_norm(x, g, b):
    mu = jnp.mean(x, axis=-1, keepdims=True)
    xc = x - mu
    return xc * lax.rsqrt(jnp.mean(xc * xc, axis=-1, keepdims=True) + EPS) * g + b


def _silu(x):
    return x * jax.nn.sigmoid(x)


def _rope(x, cc, sa, sb):
    return x * cc + pltpu.roll(x, HALF_ROPE, 1) * sa + pltpu.roll(x, HP - HALF_ROPE, 1) * sb


def _rope_t(dy, cc, sa, sb):
    return dy * cc + pltpu.roll(dy * sa, HP - HALF_ROPE, 1) + pltpu.roll(dy * sb, HALF_ROPE, 1)


def rope_tables(pos_col, rows):
    lane = np.arange(HP)
    idx = np.where(lane < KR_LANE + HALF_ROPE, lane - KR_LANE, lane - KR_LANE - HALF_ROPE)
    in_rope = (lane >= KR_LANE) & (lane < KR_LANE + ROPE)
    inv_freq = (np.float32(ROPE_THETA) ** (-np.arange(0, ROPE, 2, dtype=np.float32) / np.float32(ROPE))).astype(np.float32)
    freq_row = np.where(in_rope, inv_freq[np.clip(idx, 0, HALF_ROPE - 1)], 0.0).astype(np.float32)[None, :]
    first = ((lane >= KR_LANE) & (lane < KR_LANE + HALF_ROPE)).astype(np.float32)[None, :]
    second = ((lane >= KR_LANE + HALF_ROPE) & (lane < KR_LANE + ROPE)).astype(np.float32)[None, :]

    def body(pos_ref, f_ref, a_ref, b_ref, cc_ref, sa_ref, sb_ref):
        ang = pos_ref[...].astype(F32) * f_ref[...]
        s = jnp.sin(ang)
        cc_ref[...] = jnp.cos(ang)
        sa_ref[...] = s * b_ref[...]
        sb_ref[...] = -s * a_ref[...]

    return rowwise("rope_tables", body, rows, [_whole(pos_col)], [jnp.asarray(freq_row), jnp.asarray(first), jnp.asarray(second)],
                   [(HP, F32)] * 3)


def _causal_mask(t):
    r = lax.broadcasted_iota(jnp.int32, (t, t), 0)
    c = lax.broadcasted_iota(jnp.int32, (t, t), 1)
    return r, c


NT_DIMS = (((1,), (1,)), ((), ()))


def _carried(carry, refs, n_in, n_out, n_scratch):
    if carry is None:
        return refs, None, None
    ni, no = len(carry.ins), len(carry.outs)
    own_in, ex_in = refs[:n_in], refs[n_in:n_in + ni]
    own_out, ex_out = refs[n_in + ni:n_in + ni + n_out], refs[n_in + ni + n_out:n_in + ni + n_out + no]
    scratch = refs[n_in + ni + n_out + no:]
    sems = scratch[n_scratch:]
    return (*own_in, *own_out, *scratch[:n_scratch]), (lambda: carry.start(ex_in, ex_out, *sems)), (lambda: carry.finish(ex_in, ex_out, *sems))


def _carry_specs(carry):
    if carry is None:
        return [], [], [], [], []
    sems = [pltpu.SemaphoreType.DMA((carry.n_sems,)), pltpu.SemaphoreType.DMA((carry.n_sems,))]
    return [ANY] * len(carry.ins), [ANY] * len(carry.outs), list(carry.outs), sems, list(carry.ins)


def attention_fwd(q, k, v, name, carry=None):
    s_len = q.shape[0]
    t = min(ATT_TILE_FWD, s_len)
    nb = s_len // t
    hb = ATT_HEADS
    w = hb * HP
    nh = N_HEADS // hb

    def body(*refs):
        (q_ref, k_ref, v_ref, o_ref, lse_ref, m_sc, acc_sc), start, finish = _carried(carry, refs, 3, 2, 2)
        qi = pl.program_id(1)
        if start is not None:
            pl.when((pl.program_id(0) == 0) & (qi == 0))(start)
        m_sc[...] = jnp.full_like(m_sc, -jnp.inf)
        acc_sc[...] = jnp.zeros_like(acc_sc)

        def block(j, masked):
            ks = pl.ds(pl.multiple_of(j * t, t), t)
            for hh in range(hb):
                ls = slice(hh * HP, (hh + 1) * HP)
                s = lax.dot_general(q_ref[:, ls], k_ref[ks, ls], NT_DIMS, preferred_element_type=F32) * ATT_SCALE
                if masked:
                    r, c = _causal_mask(t)
                    s = jnp.where(c <= r, s, -jnp.inf)
                m_old = m_sc[hh]
                m_new = jnp.maximum(m_old, jnp.max(s, axis=-1, keepdims=True))
                p = jnp.exp(s - m_new)
                acc_sc[hh] = jnp.exp(m_old - m_new) * acc_sc[hh] + jnp.dot(p.astype(BF16), v_ref[ks, ls], preferred_element_type=F32)
                m_sc[hh] = m_new

        def loop_body(j, carry):
            block(j, False)
            return carry

        lax.fori_loop(0, qi, loop_body, 0)
        block(qi, True)
        lane = lax.broadcasted_iota(jnp.int32, (t, HP), 1)
        for hh in range(hb):
            acc = acc_sc[hh]
            l = jnp.sum(jnp.where(lane == VDIM, acc, 0.0), axis=-1, keepdims=True)
            o_ref[:, hh * HP:(hh + 1) * HP] = jnp.where(lane < VDIM, acc / l, 0.0).astype(o_ref.dtype)
            lse_ref[hh] = m_sc[hh] + jnp.log(l)
        if finish is not None:
            pl.when((pl.program_id(0) == nh - 1) & (qi == nb - 1))(finish)

    ex_in_specs, ex_out_specs, ex_out_shape, ex_scratch, ex_inputs = _carry_specs(carry)
    resident = pl.BlockSpec((s_len, w), lambda h, qi: (0, h))
    o, lse, *carried = pl.pallas_call(
        body,
        grid=(nh, nb),
        in_specs=[pl.BlockSpec((t, w), lambda h, qi: (qi, h)), resident, resident] + ex_in_specs,
        out_specs=[pl.BlockSpec((t, w), lambda h, qi: (qi, h)), pl.BlockSpec((hb, t, 1), lambda h, qi: (h, qi, 0))] + ex_out_specs,
        out_shape=[jax.ShapeDtypeStruct((s_len, HW), BF16), jax.ShapeDtypeStruct((N_HEADS, s_len, 1), F32)] + ex_out_shape,
        scratch_shapes=[pltpu.VMEM((hb, t, 1), F32), pltpu.VMEM((hb, t, HP), F32)] + ex_scratch,
        compiler_params=_cparams(("arbitrary", "arbitrary")),
        name=name,
    )(q, k, v, *ex_inputs)
    return o, lse, carried


def attention_delta(do, o):
    s_len = do.shape[0]
    t = min(ROW_TILE, s_len)

    def body(do_ref, o_ref, d_ref):
        prod = do_ref[...].astype(F32) * o_ref[...].astype(F32)
        for h in range(N_HEADS):
            d_ref[h] = jnp.sum(prod[:, h * HP:(h + 1) * HP], axis=-1, keepdims=True)

    return pl.pallas_call(
        body,
        grid=(s_len // t,),
        in_specs=[pl.BlockSpec((t, HW), lambda i: (i, 0))] * 2,
        out_specs=pl.BlockSpec((N_HEADS, t, 1), lambda i: (0, i, 0)),
        out_shape=jax.ShapeDtypeStruct((N_HEADS, s_len, 1), F32),
        compiler_params=_cparams(("arbitrary",)),
        name="attention_delta",
    )(do, o)


TN_DIMS = (((0,), (0,)), ((), ()))


def attention_bwd(q, k, v, do, lse_row, delta_row, name, carry=None):
    s_len = q.shape[0]
    t = min(ATT_TILE_BWD, s_len)
    nb = s_len // t
    hb = ATT_HEADS
    w = hb * HP
    nh = N_HEADS // hb

    def body(*refs):
        (q_ref, k_ref, v_ref, do_ref, lse_ref, dl_ref, dq_ref, dk_ref, dv_ref, dk_sc, dv_sc), start, finish = _carried(carry, refs, 6, 3, 2)
        ki = pl.program_id(1)
        if start is not None:
            pl.when((pl.program_id(0) == 0) & (ki == 0))(start)

        @pl.when(ki == 0)
        def _():
            dq_ref[...] = jnp.zeros_like(dq_ref)

        dk_sc[...] = jnp.zeros_like(dk_sc)
        dv_sc[...] = jnp.zeros_like(dv_sc)

        def block(j, masked):
            qs = pl.ds(pl.multiple_of(j * t, t), t)
            for hh in range(hb):
                ls = slice(hh * HP, (hh + 1) * HP)
                qb = q_ref[qs, ls]
                dob = do_ref[qs, ls]
                kb = k_ref[:, ls]
                st = lax.dot_general(kb, qb, NT_DIMS, preferred_element_type=F32) * ATT_SCALE
                pt = jnp.exp(st - lse_ref[hh, j])
                if masked:
                    r, c = _causal_mask(t)
                    pt = jnp.where(r <= c, pt, 0.0)
                dv_sc[hh] += jnp.dot(pt.astype(BF16), dob, preferred_element_type=F32)
                dpt = lax.dot_general(v_ref[:, ls], dob, NT_DIMS, preferred_element_type=F32)
                dst = (pt * (dpt - dl_ref[hh, j]) * ATT_SCALE).astype(BF16)
                dk_sc[hh] += jnp.dot(dst, qb, preferred_element_type=F32)
                dq_ref[qs, ls] += lax.dot_general(dst, kb, TN_DIMS, preferred_element_type=F32)

        block(ki, True)

        def loop_body(j, carry):
            block(j, False)
            return carry

        lax.fori_loop(ki + 1, nb, loop_body, 0)
        for hh in range(hb):
            ls = slice(hh * HP, (hh + 1) * HP)
            dk_ref[:, ls] = dk_sc[hh].astype(dk_ref.dtype)
            dv_ref[:, ls] = dv_sc[hh].astype(dv_ref.dtype)
        if finish is not None:
            pl.when((pl.program_id(0) == nh - 1) & (ki == nb - 1))(finish)

    ex_in_specs, ex_out_specs, ex_out_shape, ex_scratch, ex_inputs = _carry_specs(carry)
    k_spec = pl.BlockSpec((t, w), lambda h, ki: (ki, h))
    resident = pl.BlockSpec((s_len, w), lambda h, ki: (0, h))
    row_spec = pl.BlockSpec((hb, nb, 1, t), lambda h, ki: (h, 0, 0, 0))
    dq, dk, dv, *carried = pl.pallas_call(
        body,
        grid=(nh, nb),
        in_specs=[resident, k_spec, k_spec, resident, row_spec, row_spec] + ex_in_specs,
        out_specs=[resident, k_spec, k_spec] + ex_out_specs,
        out_shape=[jax.ShapeDtypeStruct((s_len, HW), F32), jax.ShapeDtypeStruct((s_len, HW), F32), jax.ShapeDtypeStruct((s_len, HW), BF16)]
        + ex_out_shape,
        scratch_shapes=[pltpu.VMEM((hb, t, HP), F32), pltpu.VMEM((hb, t, HP), F32)] + ex_scratch,
        compiler_params=_cparams(("arbitrary", "arbitrary")),
        name=name,
    )(q, k, v, do, lse_row, delta_row, *ex_inputs)
    return dq, dk, dv, carried


CONV_PAD = 32


def conv_fwd(z, conv_w, conv_b):
    s_len = z.shape[0]
    ch = min(CONV_CHUNK, s_len)

    def body(ag_ref, w_ref, b_ref, c_ref, pad_ref):
        pad_ref[0:CONV_PAD, :] = jnp.zeros((CONV_PAD, LANE), F32)
        pad_ref[CONV_PAD:CONV_PAD + s_len, :] = ag_ref[:, 0:LANE] * jax.nn.sigmoid(ag_ref[:, LANE:2 * LANE])

        def chunk(i, carry):
            base = pl.multiple_of(i * ch, ch)
            acc = jnp.zeros((ch, LANE), F32) + b_ref[...]
            for kk in range(CONV_W):
                acc = acc + pad_ref[pl.ds(base + CONV_PAD - (CONV_W - 1) + kk, ch), :] * w_ref[kk:kk + 1, :]
            c_ref[pl.ds(base, ch), :] = acc
            return carry

        lax.fori_loop(0, s_len // ch, chunk, 0)

    nblk = CONV_C // LANE
    return pl.pallas_call(
        body,
        grid=(nblk,),
        in_specs=[pl.BlockSpec((s_len, 2 * LANE), lambda j: (0, ZC // (2 * LANE) + j)),
                  pl.BlockSpec((CONV_W, LANE), lambda j: (0, j)), pl.BlockSpec((1, LANE), lambda j: (0, j))],
        out_specs=pl.BlockSpec((s_len, LANE), lambda j: (0, j)),
        out_shape=jax.ShapeDtypeStruct((s_len, CONV_C), F32),
        scratch_shapes=[pltpu.VMEM((s_len + CONV_PAD, LANE), F32)],
        compiler_params=_cparams(("arbitrary",)),
        name="conv_fwd",
    )(z, conv_w, conv_b)


def conv_bwd(z, dc, conv_w, dz):
    s_len = z.shape[0]
    ch = min(CONV_CHUNK, s_len)

    def body(ag_ref, dc_ref, w_ref, dz_in, dag_ref, dw_ref, db_ref, pad_ref, dpad_ref, wacc_ref):
        del dz_in
        pad_ref[0:CONV_PAD, :] = jnp.zeros((CONV_PAD, LANE), F32)
        pad_ref[CONV_PAD:CONV_PAD + s_len, :] = ag_ref[:, 0:LANE] * jax.nn.sigmoid(ag_ref[:, LANE:2 * LANE])
        dpad_ref[0:s_len, :] = dc_ref[...]
        dpad_ref[s_len:s_len + CONV_PAD, :] = jnp.zeros((CONV_PAD, LANE), F32)
        wacc_ref[...] = jnp.zeros_like(wacc_ref)
        db_ref[...] = jnp.sum(dc_ref[...], axis=0, keepdims=True)

        def chunk(i, carry):
            base = pl.multiple_of(i * ch, ch)
            dcc = dpad_ref[pl.ds(base, ch), :]
            dh = jnp.zeros((ch, LANE), F32)
            for kk in range(CONV_W):
                dh = dh + dpad_ref[pl.ds(base + (CONV_W - 1) - kk, ch), :] * w_ref[kk:kk + 1, :]
                prod = dcc * pad_ref[pl.ds(base + CONV_PAD - (CONV_W - 1) + kk, ch), :]
                wacc_ref[kk * 8:(kk + 1) * 8, :] += prod.reshape(ch // 8, 8, LANE).sum(axis=0)
            a = ag_ref[pl.ds(base, ch), 0:LANE]
            sgc = jax.nn.sigmoid(ag_ref[pl.ds(base, ch), LANE:2 * LANE])
            dag_ref[pl.ds(base, ch), 0:LANE] = (dh * sgc).astype(dag_ref.dtype)
            dag_ref[pl.ds(base, ch), LANE:2 * LANE] = (dh * a * sgc * (1.0 - sgc)).astype(dag_ref.dtype)
            return carry

        lax.fori_loop(0, s_len // ch, chunk, 0)
        for kk in range(CONV_W):
            dw_ref[kk:kk + 1, :] = jnp.sum(wacc_ref[kk * 8:(kk + 1) * 8, :], axis=0, keepdims=True)

    nblk = CONV_C // LANE
    pair = pl.BlockSpec((s_len, 2 * LANE), lambda j: (0, ZC // (2 * LANE) + j))
    return pl.pallas_call(
        body,
        grid=(nblk,),
        in_specs=[pair, pl.BlockSpec((s_len, LANE), lambda j: (0, j)), pl.BlockSpec((CONV_W, LANE), lambda j: (0, j)), ANY],
        out_specs=[pair, pl.BlockSpec((CONV_W, LANE), lambda j: (0, j)), pl.BlockSpec((1, LANE), lambda j: (0, j))],
        out_shape=[jax.ShapeDtypeStruct(dz.shape, dz.dtype), jax.ShapeDtypeStruct((CONV_W, CONV_C), F32), jax.ShapeDtypeStruct((1, CONV_C), F32)],
        scratch_shapes=[pltpu.VMEM((s_len + CONV_PAD, LANE), F32), pltpu.VMEM((s_len + CONV_PAD, LANE), F32),
                        pltpu.VMEM((CONV_W * 8, LANE), F32)],
        input_output_aliases={3: 0},
        compiler_params=_cparams(("arbitrary",)),
        name="conv_bwd",
    )(z, dc, conv_w, dz)


POOL_PAD = 16


def _pool_count(base, ch, w):
    t = base + lax.broadcasted_iota(jnp.int32, (ch, 1), 0)
    return jnp.minimum(t + 1, w).astype(F32)


def pool_fwd(z, pool_w, pool_scale):
    s_len = z.shape[0]
    ch = min(CONV_CHUNK, s_len)

    def body(u_ref, pw_ref, sc_ref, m_ref, pad_ref):
        gi = pl.program_id(0)
        pad_ref[0:POOL_PAD, :] = jnp.zeros((POOL_PAD, LANE), F32)
        pad_ref[POOL_PAD:POOL_PAD + s_len, :] = u_ref[...]

        def run(w):
            def chunk(i, carry):
                base = pl.multiple_of(i * ch, ch)
                acc = jnp.zeros((ch, LANE), F32)
                for j in range(w):
                    acc = acc + pad_ref[pl.ds(base + POOL_PAD - j, ch), :]
                d = acc / _pool_count(base, ch, w) - u_ref[pl.ds(base, ch), :]
                md = jnp.dot(d.astype(BF16), pw_ref[0], preferred_element_type=F32)
                m_ref[pl.ds(base, ch), :] = (md * sc_ref[...]).astype(m_ref.dtype)
                return carry

            lax.fori_loop(0, s_len // ch, chunk, 0)

        for g, w in enumerate(POOL_WINDOWS):
            pl.when(gi == g)(functools.partial(run, w))

    return pl.pallas_call(
        body,
        grid=(POOL_G,),
        in_specs=[pl.BlockSpec((s_len, LANE), lambda g: (0, ZP // LANE + g)), pl.BlockSpec((1, POOL_GD, POOL_GD), lambda g: (g, 0, 0)),
                  pl.BlockSpec((1, LANE), lambda g: (0, g))],
        out_specs=pl.BlockSpec((s_len, LANE), lambda g: (0, g)),
        out_shape=jax.ShapeDtypeStruct((s_len, POOL_C), BF16),
        scratch_shapes=[pltpu.VMEM((s_len + POOL_PAD, LANE), F32)],
        compiler_params=_cparams(("arbitrary",)),
        name="pool_fwd",
    )(z, pool_w, pool_scale)


def pool_bwd(z, dm, pool_w, pool_scale, dz):
    s_len = z.shape[0]
    ch = min(CONV_CHUNK, s_len)

    def body(u_ref, dm_ref, pw_ref, sc_ref, dz_in, du_ref, dpw_ref, dsc_ref, pad_ref, epad_ref, dd_ref, sacc_ref):
        del dz_in
        gi = pl.program_id(0)
        pad_ref[0:POOL_PAD, :] = jnp.zeros((POOL_PAD, LANE), F32)
        pad_ref[POOL_PAD:POOL_PAD + s_len, :] = u_ref[...]
        epad_ref[s_len:s_len + POOL_PAD, :] = jnp.zeros((POOL_PAD, LANE), F32)
        dpw_ref[...] = jnp.zeros_like(dpw_ref)
        sacc_ref[...] = jnp.zeros_like(sacc_ref)

        def run(w):
            def first(i, carry):
                base = pl.multiple_of(i * ch, ch)
                acc = jnp.zeros((ch, LANE), F32)
                for j in range(w):
                    acc = acc + pad_ref[pl.ds(base + POOL_PAD - j, ch), :]
                cnt = _pool_count(base, ch, w)
                d = (acc / cnt - u_ref[pl.ds(base, ch), :]).astype(BF16)
                md = jnp.dot(d, pw_ref[0], preferred_element_type=F32)
                dmc = dm_ref[pl.ds(base, ch), :]
                sacc_ref[...] += (dmc * md).reshape(ch // 8, 8, LANE).sum(axis=0)
                dmd = (dmc * sc_ref[...]).astype(BF16)
                dpw_ref[0] += lax.dot_general(d, dmd, (((0,), (0,)), ((), ())), preferred_element_type=F32)
                dd = lax.dot_general(dmd, pw_ref[0], (((1,), (1,)), ((), ())), preferred_element_type=F32)
                dd_ref[pl.ds(base, ch), :] = dd
                epad_ref[pl.ds(base, ch), :] = dd / cnt
                return carry

            lax.fori_loop(0, s_len // ch, first, 0)

            def second(i, carry):
                base = pl.multiple_of(i * ch, ch)
                acc = jnp.zeros((ch, LANE), F32)
                for j in range(w):
                    acc = acc + epad_ref[pl.ds(base + j, ch), :]
                du_ref[pl.ds(base, ch), :] = (acc - dd_ref[pl.ds(base, ch), :]).astype(du_ref.dtype)
                return carry

            lax.fori_loop(0, s_len // ch, second, 0)

        for g, w in enumerate(POOL_WINDOWS):
            pl.when(gi == g)(functools.partial(run, w))
        dsc_ref[...] = jnp.sum(sacc_ref[...], axis=0, keepdims=True)

    return pl.pallas_call(
        body,
        grid=(POOL_G,),
        in_specs=[pl.BlockSpec((s_len, LANE), lambda g: (0, ZP // LANE + g)), pl.BlockSpec((s_len, LANE), lambda g: (0, g)),
                  pl.BlockSpec((1, POOL_GD, POOL_GD), lambda g: (g, 0, 0)), pl.BlockSpec((1, LANE), lambda g: (0, g)), ANY],
        out_specs=[pl.BlockSpec((s_len, LANE), lambda g: (0, ZP // LANE + g)), pl.BlockSpec((1, POOL_GD, POOL_GD), lambda g: (g, 0, 0)),
                   pl.BlockSpec((1, LANE), lambda g: (0, g))],
        out_shape=[jax.ShapeDtypeStruct(dz.shape, dz.dtype), jax.ShapeDtypeStruct((POOL_G, POOL_GD, POOL_GD), F32),
                   jax.ShapeDtypeStruct((1, POOL_C), F32)],
        scratch_shapes=[pltpu.VMEM((s_len + POOL_PAD, LANE), F32), pltpu.VMEM((s_len + POOL_PAD, LANE), F32),
                        pltpu.VMEM((s_len, LANE), F32), pltpu.VMEM((8, LANE), F32)],
        input_output_aliases={4: 0},
        compiler_params=_cparams(("arbitrary",)),
        name="pool_bwd",
    )(z, dm, pool_w, pool_scale, dz)


def _row(v):
    return v.reshape(1, -1)


def _rms_body(x_ref, g_ref, o_ref):
    o_ref[...] = _rms(x_ref[...], g_ref[...]).astype(o_ref.dtype)


def _post_body(y_ref, x_ref, g_ref, o_ref):
    o_ref[...] = x_ref[...] + _rms(y_ref[...], g_ref[...])


def _post_bwd_body(y_ref, dh_ref, g_ref, dy_ref, dg_ref):
    _, vjp = jax.vjp(_rms, y_ref[...], g_ref[...])
    dy, dg = vjp(dh_ref[...])
    dy_ref[...] = dy.astype(dy_ref.dtype)
    _acc(dg_ref, dg)


def _pre_bwd_body(x_ref, dhn_ref, dres_ref, g_ref, dx_ref, dg_ref):
    _, vjp = jax.vjp(_rms, x_ref[...], g_ref[...])
    dx, dg = vjp(dhn_ref[...])
    dx_ref[...] = dres_ref[...] + dx
    _acc(dg_ref, dg)


def mixer_fwd(x, tabs, w, tag, carry=None):
    s_len = x.shape[0]
    cc, sa, sb = tabs
    sv = {"x": x}

    (h,) = rowwise("mix_norm_pre" + tag, _rms_body, s_len, [_whole(x)], [_row(w["mix_norm_pre"])], [(D_MODEL, BF16)])
    z = mm(h, w["w_in"], name="in_proj" + tag)

    def prep_body(z_ref, cc_ref, sa_ref, sb_ref, qg_ref, kg_ref, qn_ref, ckv_ref, kr_ref):
        qn_ref[...] = _rms(z_ref[:, 0:Q_RANK], qg_ref[...]).astype(qn_ref.dtype)
        ckv_ref[...] = _rms(z_ref[:, Q_RANK:Q_RANK + KV_RANK], kg_ref[...]).astype(ckv_ref.dtype)
        kr_ref[...] = _rope(z_ref[:, Q_RANK + KV_RANK:ZA_W], cc_ref[...], sa_ref[...], sb_ref[...])

    qn, ckvn, kr = rowwise("attn_prep" + tag, prep_body, s_len, [(z, ZA_W, ZA // ZA_W), _whole(cc), _whole(sa), _whole(sb)],
                           [_row(w["q_norm"]), _row(w["kv_norm"])], [(Q_RANK, BF16), (KV_RANK, BF16), (HP, F32)])
    q_raw = mm(qn, w["w_uq"], name="q_proj" + tag)
    kv_raw = mm(ckvn, w["w_ukv"], name="kv_proj" + tag)

    def qkv_body(q_ref, kv_ref, kr_ref, cc_ref, sa_ref, sb_ref, qo_ref, ko_ref, vo_ref):
        c_, a_, b_, kro = cc_ref[...], sa_ref[...], sb_ref[...], kr_ref[...]
        for hh in range(N_HEADS):
            sl = slice(hh * HP, (hh + 1) * HP)
            qo_ref[:, sl] = _rope(q_ref[:, sl], c_, a_, b_).astype(qo_ref.dtype)
            ko_ref[:, sl] = (kv_ref[:, sl] + kro).astype(ko_ref.dtype)
        lane = lax.broadcasted_iota(jnp.int32, (q_ref.shape[0], HW), 1)
        vo_ref[...] = jnp.where((lane & (HP - 1)) == VDIM, 1.0, kv_ref[:, HW:2 * HW]).astype(vo_ref.dtype)

    q, k, v = rowwise("qkv_rope" + tag, qkv_body, s_len, [_whole(q_raw), _whole(kv_raw), _whole(kr), _whole(cc), _whole(sa), _whole(sb)], [],
                      [(HW, BF16)] * 3)
    o, lse, carried = attention_fwd(q, k, v, "attention_fwd" + tag, carry)
    y_attn = mm(o, w["w_attn_o"], out_dtype=BF16, name="attn_out" + tag)

    c = conv_fwd(z, w["conv_w"], _row(w["conv_b"]))

    def ln_body(c_ref, g_ref, b_ref, o_ref):
        o_ref[...] = _silu(_layer_norm(c_ref[...], g_ref[...], b_ref[...])).astype(o_ref.dtype)

    (cs,) = rowwise("conv_ln_silu" + tag, ln_body, s_len, [_whole(c)], [_row(w["conv_ln_g"]), _row(w["conv_ln_b"])], [(CONV_C, BF16)])
    y_conv = mm(cs, w["w_conv_o"], out_dtype=BF16, name="conv_out" + tag)

    m = pool_fwd(z, w["pool_w"], _row(w["pool_scale"]))
    y_pool = mm(m, w["w_pool_o"], out_dtype=BF16, name="pool_out" + tag)

    def merge_body(ya_ref, yc_ref, yp_ref, gl_ref, o_ref):
        gl = gl_ref[...]
        o_ref[...] = (jax.nn.sigmoid(gl[:, 0:D_MODEL]) * ya_ref[...].astype(F32) + jax.nn.sigmoid(gl[:, D_MODEL:2 * D_MODEL]) * yc_ref[...].astype(F32)
                      + jax.nn.sigmoid(gl[:, 2 * D_MODEL:3 * D_MODEL]) * yp_ref[...].astype(F32)).astype(o_ref.dtype)

    (merged,) = rowwise("gate_merge" + tag, merge_body, s_len, [_whole(y_attn), _whole(y_conv), _whole(y_pool), (z, 3 * D_MODEL, 0)], [],
                        [(D_MODEL, BF16)])
    mo = mm(merged, w["w_mix_o"], name="mix_out" + tag)

    (h1,) = rowwise("mix_norm_post" + tag, _post_body, s_len, [_whole(mo), _whole(x)], [_row(w["mix_norm_post"])], [(D_MODEL, F32)])
    sv.update(h=h, z=z, qn=qn, ckvn=ckvn, q=q, k=k, v=v, o=o, lse=lse, c=c, cs=cs, m=m, y_attn=y_attn, y_conv=y_conv, y_pool=y_pool,
              merged=merged, mo=mo)
    return h1, sv, carried


def ffn_fwd(h1, w, tag):
    s_len = h1.shape[0]
    (hn,) = rowwise("ffn_norm_pre" + tag, _rms_body, s_len, [_whole(h1)], [_row(w["ffn_norm_pre"])], [(D_MODEL, BF16)])
    gu, act = ffn_in(hn, w["w_gu"], "ffn_in" + tag)
    y = mm(act, w["w_down"], name="ffn_out" + tag)
    (h2,) = rowwise("ffn_norm_post" + tag, _post_body, s_len, [_whole(y), _whole(h1)], [_row(w["ffn_norm_post"])], [(D_MODEL, F32)])
    return h2, dict(h1=h1, hn=hn, gu=gu, act=act, y=y)


def ffn_bwd(dh2, sv, w, tag):
    s_len = dh2.shape[0]
    g = {}
    d_y, g["ffn_norm_post"] = rowwise("ffn_norm_post_bwd" + tag, _post_bwd_body, s_len, [_whole(sv["y"]), _whole(dh2)],
                                      [_row(w["ffn_norm_post"])], [(D_MODEL, BF16)], [((1, D_MODEL), F32)])
    g["w_down"] = mm(sv["act"], d_y, ta=True, name="ffn_out_dw" + tag)
    d_gu = ffn_out_dx(d_y, w["w_down"], sv["gu"], "ffn_out_dx" + tag)
    g["w_gu"] = mm(sv["hn"], d_gu, ta=True, name="ffn_in_dw" + tag)
    d_hn = mm(d_gu, w["w_gu"], tb=True, name="ffn_in_dx" + tag)
    dh1, g["ffn_norm_pre"] = rowwise(
        "ffn_norm_pre_bwd" + tag, _pre_bwd_body, s_len, [_whole(sv["h1"]), _whole(d_hn), _whole(dh2)], [_row(w["ffn_norm_pre"])], [(D_MODEL, F32)], [((1, D_MODEL), F32)])
    return dh1, g


def mixer_bwd(dh1, sv, tabs, w, tag, carry=None):
    s_len = dh1.shape[0]
    cc, sa, sb = tabs
    g = {}

    d_mo, g["mix_norm_post"] = rowwise(
        "mix_norm_post_bwd" + tag, _post_bwd_body, s_len, [_whole(sv["mo"]), _whole(dh1)], [_row(w["mix_norm_post"])], [(D_MODEL, BF16)], [((1, D_MODEL), F32)])
    g["w_mix_o"] = mm(sv["merged"], d_mo, ta=True, name="mix_out_dw" + tag)
    d_merged = mm(d_mo, w["w_mix_o"], tb=True, name="mix_out_dx" + tag)

    def merge_bwd_body(dm_ref, ya_ref, yc_ref, yp_ref, gl_ref, dya_ref, dyc_ref, dyp_ref, dgl_ref):
        dmg = dm_ref[...]
        for i, (y_ref, dy_ref) in enumerate(((ya_ref, dya_ref), (yc_ref, dyc_ref), (yp_ref, dyp_ref))):
            sg = jax.nn.sigmoid(gl_ref[:, i * D_MODEL:(i + 1) * D_MODEL])
            dy_ref[...] = (dmg * sg).astype(dy_ref.dtype)
            dgl_ref[:, i * D_MODEL:(i + 1) * D_MODEL] = (dmg * y_ref[...].astype(F32) * sg * (1.0 - sg)).astype(dgl_ref.dtype)

    d_ya, d_yc, d_yp, dz = rowwise(
        "gate_merge_bwd" + tag, merge_bwd_body, s_len,
        [_whole(d_merged), _whole(sv["y_attn"]), _whole(sv["y_conv"]), _whole(sv["y_pool"]), (sv["z"], 3 * D_MODEL, 0)], [],
        [(D_MODEL, BF16)] * 3 + [(3 * D_MODEL, BF16, ZW, 0)])

    g["w_pool_o"] = mm(sv["m"], d_yp, ta=True, name="pool_out_dw" + tag)
    d_m = mm(d_yp, w["w_pool_o"], tb=True, name="pool_out_dx" + tag)
    dz, g["pool_w"], g["pool_scale"] = pool_bwd(sv["z"], d_m, w["pool_w"], _row(w["pool_scale"]), dz)

    g["w_conv_o"] = mm(sv["cs"], d_yc, ta=True, name="conv_out_dw" + tag)
    d_cs = mm(d_yc, w["w_conv_o"], tb=True, name="conv_out_dx" + tag)

    def ln_bwd_body(c_ref, dcs_ref, g_ref, b_ref, dc_ref, dg_ref, db_ref):
        f = lambda c_, g_, b_: _silu(_layer_norm(c_, g_, b_))
        _, vjp = jax.vjp(f, c_ref[...], g_ref[...], b_ref[...])
        dc, dg, db = vjp(dcs_ref[...])
        dc_ref[...] = dc
        _acc(dg_ref, dg)
        _acc(db_ref, db)

    d_c, g["conv_ln_g"], g["conv_ln_b"] = rowwise("conv_ln_silu_bwd" + tag, ln_bwd_body, s_len, [_whole(sv["c"]), _whole(d_cs)],
                                                  [_row(w["conv_ln_g"]), _row(w["conv_ln_b"])], [(CONV_C, F32)],
                                                  [((1, CONV_C), F32), ((1, CONV_C), F32)])
    dz, g["conv_w"], g["conv_b"] = conv_bwd(sv["z"], d_c, w["conv_w"], dz)

    g["w_attn_o"] = mm(sv["o"], d_ya, ta=True, name="attn_out_dw" + tag)
    d_o = mm(d_ya, w["w_attn_o"], tb=True, out_dtype=BF16, name="attn_out_dx" + tag)
    delta = attention_delta(d_o, sv["o"])
    t_bwd = min(ATT_TILE_BWD, s_len)
    rows_of = lambda a: a.reshape(N_HEADS, s_len // t_bwd, 1, t_bwd)
    dq, dk, dv, carried = attention_bwd(sv["q"], sv["k"], sv["v"], d_o, rows_of(sv["lse"]), rows_of(delta), "attention_bwd" + tag, carry)

    def qkv_bwd_body(dq_ref, dk_ref, dv_ref, cc_ref, sa_ref, sb_ref, dqp_ref, dkv_ref, dkr_ref):
        c_, a_, b_ = cc_ref[...], sa_ref[...], sb_ref[...]
        dk_sum = jnp.zeros((dq_ref.shape[0], HP), F32)
        for hh in range(N_HEADS):
            sl = slice(hh * HP, (hh + 1) * HP)
            dqp_ref[:, sl] = _rope_t(dq_ref[:, sl], c_, a_, b_).astype(dqp_ref.dtype)
            dkh = dk_ref[:, sl]
            dkv_ref[:, sl] = dkh.astype(dkv_ref.dtype)
            dk_sum = dk_sum + dkh
        dkv_ref[:, HW:2 * HW] = dv_ref[...]
        dkr_ref[...] = _rope_t(dk_sum, c_, a_, b_)

    dq_pre, dkv_pre, d_kr = rowwise("qkv_rope_bwd" + tag, qkv_bwd_body, s_len,
                                    [_whole(dq), _whole(dk), _whole(dv), _whole(cc), _whole(sa), _whole(sb)], [],
                                    [(HW, BF16), (2 * HW, BF16), (HP, F32)])
    g["w_uq"] = mm(sv["qn"], dq_pre, ta=True, name="q_proj_dw" + tag)
    d_qn = mm(dq_pre, w["w_uq"], tb=True, name="q_proj_dx" + tag)
    g["w_ukv"] = mm(sv["ckvn"], dkv_pre, ta=True, name="kv_proj_dw" + tag)
    d_ckvn = mm(dkv_pre, w["w_ukv"], tb=True, name="kv_proj_dx" + tag)

    def prep_bwd_body(z_ref, dqn_ref, dckv_ref, dkr_ref, qg_ref, kg_ref, dz_ref, dqg_ref, dkg_ref):
        _, vq = jax.vjp(_rms, z_ref[:, 0:Q_RANK], qg_ref[...])
        dcq, dqg = vq(dqn_ref[...])
        _, vk = jax.vjp(_rms, z_ref[:, Q_RANK:Q_RANK + KV_RANK], kg_ref[...])
        dckv, dkg = vk(dckv_ref[...])
        dz_ref[:, 0:Q_RANK] = dcq.astype(dz_ref.dtype)
        dz_ref[:, Q_RANK:Q_RANK + KV_RANK] = dckv.astype(dz_ref.dtype)
        dz_ref[:, Q_RANK + KV_RANK:ZA_W] = dkr_ref[...].astype(dz_ref.dtype)
        _acc(dqg_ref, dqg)
        _acc(dkg_ref, dkg)

    dz, g["q_norm"], g["kv_norm"] = rowwise("attn_prep_bwd" + tag, prep_bwd_body, s_len,
                                            [(sv["z"], ZA_W, ZA // ZA_W), _whole(d_qn), _whole(d_ckvn), _whole(d_kr)],
                                            [_row(w["q_norm"]), _row(w["kv_norm"])], [(ZA_W, BF16, ZW, ZA // ZA_W)],
                                            [((1, Q_RANK), F32), ((1, KV_RANK), F32)], into={0: dz})

    g["w_in"] = mm(sv["h"], dz, ta=True, name="in_proj_dw" + tag)
    d_h = mm(dz, w["w_in"], tb=True, name="in_proj_dx" + tag)
    dx, g["mix_norm_pre"] = rowwise(
        "mix_norm_pre_bwd" + tag, _pre_bwd_body, s_len, [_whole(sv["x"]), _whole(d_h), _whole(dh1)], [_row(w["mix_norm_pre"])], [(D_MODEL, F32)], [((1, D_MODEL), F32)])
    return dx, g, carried


def loss_head(h, target):
    s_len = h.shape[0]

    def body(h_ref, t_ref, dy_ref, loss_ref):
        err = h_ref[...] - t_ref[...]
        dy_ref[...] = err * (1.0 / D_MODEL)
        part = 0.5 * jnp.sum(jnp.mean(err * err, axis=-1, keepdims=True), axis=0, keepdims=True)
        _acc(loss_ref, jnp.broadcast_to(part, (1, LANE)))

    return rowwise("loss_head", body, s_len, [_whole(h), _whole(target)], [], [(D_MODEL, F32)], [((1, LANE), F32)])


def local_step(x, pos_col, target, layers):
    s_len = x.shape[0]
    tabs = rope_tables(pos_col, s_len)
    h, saved = x, []
    for li, w in enumerate(layers):
        h, sv_mix, _ = mixer_fwd(h, tabs, w, f"_l{li}")
        h, sv_ffn = ffn_fwd(h, w, f"_l{li}")
        saved.append((sv_mix, sv_ffn))
    dh, loss = loss_head(h, target)
    grads = [None] * len(layers)
    for li in reversed(range(len(layers))):
        dh, g_ffn = ffn_bwd(dh, saved[li][1], layers[li], f"_l{li}")
        dh, g_mix, _ = mixer_bwd(dh, saved[li][0], tabs, layers[li], f"_l{li}")
        grads[li] = {**g_mix, **g_ffn}
    return loss[0, 0], dh, grads


def _pad_heads_cols(wm, per_head):
    r = wm.shape[0]
    return jnp.pad(wm.reshape(r, N_HEADS, per_head), ((0, 0), (0, 0), (0, HP - per_head))).reshape(r, HW)


def _unpad_heads_cols(wm, per_head):
    r = wm.shape[0]
    return wm.reshape(r, N_HEADS, HP)[:, :, :per_head].reshape(r, N_HEADS * per_head)


def align_weights(p):
    out = dict(p)
    if "w_in" in p:
        w_in = p["w_in"]
        r = w_in.shape[0]
        zeros = lambda n: jnp.zeros((r, n), w_in.dtype)
        conv = w_in[:, O_CONV:O_POOL].reshape(r, 2, CONV_C // LANE, LANE).transpose(0, 2, 1, 3).reshape(r, 2 * CONV_C)
        out["w_in"] = jnp.concatenate([
            w_in[:, O_GATE:D_IN], conv, w_in[:, O_POOL:O_GATE], w_in[:, O_Q:O_KR],
            zeros(KR_LANE), w_in[:, O_KR:O_CONV], zeros(HP - KR_LANE - ROPE)], axis=1)
        out["w_uq"] = _pad_heads_cols(p["w_uq"], NOPE + ROPE)
        out["w_ukv"] = jnp.concatenate([_pad_heads_cols(p["w_uk"], NOPE), _pad_heads_cols(p["w_uv"], VDIM)], axis=1)
        wo = p["w_attn_o"]
        out["w_attn_o"] = jnp.pad(wo.reshape(N_HEADS, VDIM, D_MODEL), ((0, 0), (0, HP - VDIM), (0, 0))).reshape(HW, D_MODEL)
        del out["w_uk"], out["w_uv"]
    if "w_gate" in p:
        out["w_gu"] = jnp.concatenate([p["w_gate"][:, :FF_HALF], p["w_up"][:, :FF_HALF], p["w_gate"][:, FF_HALF:], p["w_up"][:, FF_HALF:]], axis=1)
        del out["w_gate"], out["w_up"]
    return out


def unalign_grads(g):
    out = dict(g)
    if "w_in" in g:
        gi = g["w_in"]
        kr0 = ZA + Q_RANK + KV_RANK + KR_LANE
        r = gi.shape[0]
        conv = gi[:, ZC:ZP].reshape(r, CONV_C // LANE, 2, LANE).transpose(0, 2, 1, 3).reshape(r, 2 * CONV_C)
        out["w_in"] = jnp.concatenate([gi[:, ZA:ZA + Q_RANK + KV_RANK], gi[:, kr0:kr0 + ROPE], conv, gi[:, ZP:ZA], gi[:, ZG:ZC]], axis=1)
        out["w_uq"] = _unpad_heads_cols(g["w_uq"], NOPE + ROPE)
        out["w_uk"] = _unpad_heads_cols(g["w_ukv"][:, :HW], NOPE)
        out["w_uv"] = _unpad_heads_cols(g["w_ukv"][:, HW:], VDIM)
        out["w_attn_o"] = g["w_attn_o"].reshape(N_HEADS, HP, D_MODEL)[:, :VDIM].reshape(N_HEADS * VDIM, D_MODEL)
        del out["w_ukv"]
    if "w_gu" in g:
        gu = g["w_gu"]
        out["w_gate"] = jnp.concatenate([gu[:, 0:FF_HALF], gu[:, 2 * FF_HALF:3 * FF_HALF]], axis=1)
        out["w_up"] = jnp.concatenate([gu[:, FF_HALF:2 * FF_HALF], gu[:, 3 * FF_HALF:]], axis=1)
        del out["w_gu"]
    return out


MESH = pl.DeviceIdType.MESH
ANY = pl.BlockSpec(memory_space=pl.ANY)


def _place():
    return lax.axis_index("x"), lax.axis_index("y"), lax.axis_index("c")


def _other_chips(x, y):
    return [(1 - x, y), (x, 1 - y), (1 - x, 1 - y)]


def _half_rows(rows, c):
    assert rows % (2 * HALF_ALIGN) == 0, rows
    return pl.ds(pl.multiple_of(c * (rows // 2), HALF_ALIGN), rows // 2)


class GatherShards:
    def __init__(self, local):
        self.ins = list(local)
        self.outs = [jax.ShapeDtypeStruct((N_CHIPS, *a.shape), a.dtype) for a in local]
        self.n_sems = 6 * len(local)

    def _first(self, in_refs, out_refs, send_sems, recv_sems):
        x, y, c = _place()
        me = 2 * x + y
        chips = _other_chips(x, y)

        def copy(i, k, slot, core, to, src=None):
            dst = out_refs[i].at[slot, _half_rows(out_refs[i].shape[1], core)]
            return pltpu.make_async_remote_copy(src_ref=dst if src is None else src, dst_ref=dst, send_sem=send_sems.at[6 * i + k],
                                                recv_sem=recv_sems.at[6 * i + k], device_id=to, device_id_type=MESH)

        first = [copy(i, j, me, c, (*chip, c), src=in_refs[i].at[_half_rows(in_refs[i].shape[0], c)])
                 for i in range(len(in_refs)) for j, chip in enumerate(chips)]
        return first, copy

    def start(self, in_refs, out_refs, send_sems, recv_sems):
        first, _ = self._first(in_refs, out_refs, send_sems, recv_sems)
        for cp in first:
            cp.start()

    def finish(self, in_refs, out_refs, send_sems, recv_sems):
        first, copy = self._first(in_refs, out_refs, send_sems, recv_sems)
        x, y, c = _place()
        slots = [2 * cx + cy for cx, cy in _other_chips(x, y)]
        sibling = (x, y, 1 - c)
        passed = []
        for i in range(len(in_refs)):
            for j in range(3):
                copy(i, j, slots[j], c, sibling).wait_recv()
                fwd = copy(i, 3 + j, slots[j], c, sibling)
                fwd.start()
                passed.append(fwd)
        for i in range(len(in_refs)):
            for j in range(3):
                copy(i, 3 + j, slots[j], 1 - c, sibling).wait_recv()
        for cp in first + passed:
            cp.wait_send()


class ChipExchange:
    def __init__(self, parts):
        self.ins = list(parts)
        self.outs = [jax.ShapeDtypeStruct((3, *a.shape[1:]), a.dtype) for a in parts]
        self.n_sems = 3 * len(parts)

    def _copies(self, in_refs, out_refs, send_sems, recv_sems):
        x, y, c = _place()
        return [pltpu.make_async_remote_copy(src_ref=in_refs[i].at[2 * chip[0] + chip[1]], dst_ref=out_refs[i].at[j],
                                             send_sem=send_sems.at[3 * i + j], recv_sem=recv_sems.at[3 * i + j],
                                             device_id=(*chip, c), device_id_type=MESH)
                for i in range(len(in_refs)) for j, chip in enumerate(_other_chips(x, y))]

    def start(self, in_refs, out_refs, send_sems, recv_sems):
        for cp in self._copies(in_refs, out_refs, send_sems, recv_sems):
            cp.start()

    def finish(self, in_refs, out_refs, send_sems, recv_sems):
        copies = self._copies(in_refs, out_refs, send_sems, recv_sems)
        for cp in copies:
            cp.wait_recv()
        for cp in copies:
            cp.wait_send()


def run_exchange(ex, name):
    n_in, n_out = len(ex.ins), len(ex.outs)

    def body(*refs):
        ins, outs, sems = refs[:n_in], refs[n_in:n_in + n_out], refs[n_in + n_out:]
        ex.start(ins, outs, *sems)
        ex.finish(ins, outs, *sems)

    return pl.pallas_call(
        body,
        in_specs=[ANY] * n_in,
        out_specs=[ANY] * n_out,
        out_shape=list(ex.outs),
        scratch_shapes=[pltpu.SemaphoreType.DMA((ex.n_sems,)), pltpu.SemaphoreType.DMA((ex.n_sems,))],
        name=name,
    )(*ex.ins)


def sibling_swap(gs, name):
    n = len(gs)

    def body(*refs):
        g_refs, out_refs, (send_sems, recv_sems) = refs[:n], refs[n:2 * n], refs[2 * n:]
        x, y, c = _place()
        copies = []
        for i in range(n):
            for j in range(N_CHIPS):
                cp = pltpu.make_async_remote_copy(src_ref=g_refs[i].at[j, _half_rows(g_refs[i].shape[1], 1 - c)], dst_ref=out_refs[i].at[j],
                                                  send_sem=send_sems.at[4 * i + j], recv_sem=recv_sems.at[4 * i + j],
                                                  device_id=(x, y, 1 - c), device_id_type=MESH)
                cp.start()
                copies.append(cp)
        for cp in copies:
            cp.wait_recv()
        for cp in copies:
            cp.wait_send()

    return pl.pallas_call(
        body,
        in_specs=[ANY] * n,
        out_specs=[ANY] * n,
        out_shape=[jax.ShapeDtypeStruct((N_CHIPS, a.shape[1] // 2, a.shape[2]), a.dtype) for a in gs],
        scratch_shapes=[pltpu.SemaphoreType.DMA((4 * n,)), pltpu.SemaphoreType.DMA((4 * n,))],
        name=name,
    )(*gs)


def sibling_gather(fs):
    n = len(fs)
    layers = fs[0].shape[0]

    def body(*refs):
        out_refs, (send_sems, recv_sems) = refs[n:2 * n], refs[2 * n:]
        x, y, c = _place()

        def copy(i, l, core):
            part = out_refs[i].at[l, _half_rows(out_refs[i].shape[1], core)]
            return pltpu.make_async_remote_copy(src_ref=part, dst_ref=part, send_sem=send_sems.at[layers * i + l],
                                                recv_sem=recv_sems.at[layers * i + l], device_id=(x, y, 1 - c), device_id_type=MESH)

        sends = [copy(i, l, c) for i in range(n) for l in range(layers)]
        for cp in sends:
            cp.start()
        for i in range(n):
            for l in range(layers):
                copy(i, l, 1 - c).wait_recv()
        for cp in sends:
            cp.wait_send()

    return pl.pallas_call(
        body,
        in_specs=[ANY] * n,
        out_specs=[ANY] * n,
        out_shape=[jax.ShapeDtypeStruct(a.shape, a.dtype) for a in fs],
        scratch_shapes=[pltpu.SemaphoreType.DMA((layers * n,)), pltpu.SemaphoreType.DMA((layers * n,))],
        input_output_aliases={i: i for i in range(n)},
        name="sibling_gather",
    )(*fs)


def gather_all(vs):
    n = len(vs)

    def body(*refs):
        v_refs, out_refs, (send_sems, recv_sems, local_sems) = refs[:n], refs[n:2 * n], refs[2 * n:]
        x, y, c = _place()
        me, sibling = (x, y, c), (x, y, 1 - c)
        chips = _other_chips(x, y)

        def copy(i, k, block, to, src=None):
            px, py, pc = block
            dst = out_refs[i].at[4 * px + 2 * py + pc]
            return pltpu.make_async_remote_copy(src_ref=dst if src is None else src, dst_ref=dst, send_sem=send_sems.at[7 * i + k],
                                                recv_sem=recv_sems.at[7 * i + k], device_id=to, device_id_type=MESH)

        mine = [pltpu.make_async_copy(v_refs[i], out_refs[i].at[4 * x + 2 * y + c], local_sems.at[i]) for i in range(n)]
        for cp in mine:
            cp.start()
        first = []
        for i in range(n):
            first.append(copy(i, 0, me, sibling, src=v_refs[i]))
            first += [copy(i, 1 + j, me, (*chip, c), src=v_refs[i]) for j, chip in enumerate(chips)]
        for cp in first:
            cp.start()
        passed = []
        for i in range(n):
            for j, chip in enumerate(chips):
                copy(i, 1 + j, (*chip, c), me).wait_recv()
                fwd = copy(i, 4 + j, (*chip, c), sibling)
                fwd.start()
                passed.append(fwd)
        for i in range(n):
            copy(i, 0, sibling, me).wait_recv()
            for j, chip in enumerate(chips):
                copy(i, 4 + j, (*chip, 1 - c), me).wait_recv()
        for cp in first + passed:
            cp.wait_send()
        for cp in mine:
            cp.wait()

    return pl.pallas_call(
        body,
        in_specs=[ANY] * n,
        out_specs=[ANY] * n,
        out_shape=[jax.ShapeDtypeStruct((8, *a.shape), a.dtype) for a in vs],
        scratch_shapes=[pltpu.SemaphoreType.DMA((7 * n,)), pltpu.SemaphoreType.DMA((7 * n,)), pltpu.SemaphoreType.DMA((n,))],
        name="gather_all",
    )(*vs)


def _row_tile(rows, row_bytes):
    best = None
    for t in range(16, rows + 1, 16):
        if rows % t == 0 and t * row_bytes <= SUM_TILE_BYTES:
            best = t
    return best or rows


def sibling_sum(g, theirs, place, name):
    _, half, cols = theirs.shape
    tile = _row_tile(half, cols * 4)
    nt = half // tile

    def body(place_ref, g_ref, t_ref, o_ref):
        o_ref[...] = (g_ref[...].astype(F32) + t_ref[...].astype(F32)).astype(o_ref.dtype)

    spec = pl.BlockSpec((1, tile, cols), lambda j, i, place_ref: (j, i, 0))
    return pl.pallas_call(
        body,
        grid_spec=pltpu.PrefetchScalarGridSpec(
            num_scalar_prefetch=1, grid=(N_CHIPS, nt),
            in_specs=[pl.BlockSpec((1, tile, cols), lambda j, i, place_ref: (j, place_ref[1] * nt + i, 0)), spec], out_specs=spec),
        out_shape=jax.ShapeDtypeStruct(theirs.shape, BF16),
        compiler_params=_cparams(("parallel", "parallel")),
        name=name,
    )(place, g, theirs)


def chip_sum(p, others, place, layer, into, name):
    _, half, cols = p.shape
    tile = _row_tile(half, cols * 4)
    nt = half // tile

    def body(place_ref, p_ref, o3_ref, *rest):
        o_ref = rest[-1]
        acc = p_ref[0].astype(F32)
        for k in range(3):
            acc = acc + o3_ref[k].astype(F32)
        o_ref[0] = acc

    return pl.pallas_call(
        body,
        grid_spec=pltpu.PrefetchScalarGridSpec(
            num_scalar_prefetch=1, grid=(nt,),
            in_specs=[pl.BlockSpec((1, tile, cols), lambda i, place_ref: (place_ref[0], i, 0)),
                      pl.BlockSpec((3, tile, cols), lambda i, place_ref: (0, i, 0))] + ([] if into is None else [ANY]),
            out_specs=pl.BlockSpec((1, tile, cols), lambda i, place_ref: (layer, place_ref[1] * nt + i, 0))),
        out_shape=jax.ShapeDtypeStruct((N_LAYERS, 2 * half, cols), F32),
        input_output_aliases={} if into is None else {3: 0},
        compiler_params=_cparams(("parallel",)),
        name=name,
    )(place, p, others, *([] if into is None else [into]))


def sum_devices(a, name):
    n, rows, cols = a.shape
    tile = _row_tile(rows, cols * 4 * n)

    def body(a_ref, o_ref):
        acc = a_ref[0]
        for s in range(1, n):
            acc = acc + a_ref[s]
        o_ref[...] = acc

    return pl.pallas_call(body, grid=(rows // tile,), in_specs=[pl.BlockSpec((n, tile, cols), lambda i: (0, i, 0))],
                          out_specs=pl.BlockSpec((tile, cols), lambda i: (i, 0)), out_shape=jax.ShapeDtypeStruct((rows, cols), F32),
                          compiler_params=_cparams(("parallel",)), name=name)(a)


def adamw(w, g, m, v, name):
    layers, rows, cols = w.shape
    tile = rows
    for t in range(8, rows, 8):
        if rows % t == 0 and t * cols * 4 <= 2 * SUM_TILE_BYTES:
            tile = t

    def body(w_ref, g_ref, m_ref, v_ref, d_ref, mo_ref, vo_ref):
        gg = g_ref[...]
        m_new = ADAM_B1 * m_ref[...] + (1.0 - ADAM_B1) * gg
        v_new = ADAM_B2 * v_ref[...] + (1.0 - ADAM_B2) * (gg * gg)
        m_hat = m_new / (1.0 - ADAM_B1 ** ADAM_STEP)
        v_hat = v_new / (1.0 - ADAM_B2 ** ADAM_STEP)
        d_ref[...] = -ADAM_LR * (m_hat / (jnp.sqrt(v_hat) + ADAM_EPS) + ADAM_WD * w_ref[...])
        mo_ref[...] = m_new
        vo_ref[...] = v_new

    spec = pl.BlockSpec((1, tile, cols), lambda l, i: (l, i, 0))
    shape = jax.ShapeDtypeStruct((layers, rows, cols), F32)
    return pl.pallas_call(body, grid=(layers, rows // tile), in_specs=[spec] * 4, out_specs=[spec] * 3, out_shape=[shape] * 3,
                          compiler_params=_cparams(("parallel", "parallel")), name=name)(w, g, m, v)


WEIGHTS = ["mix_norm_pre", "w_in", "q_norm", "w_uq", "kv_norm", "w_uk", "w_uv", "w_attn_o", "conv_w", "conv_b", "conv_ln_g", "conv_ln_b",
           "w_conv_o", "pool_w", "pool_scale", "w_pool_o", "w_mix_o", "mix_norm_post", "ffn_norm_pre", "w_gate", "w_up", "w_down",
           "ffn_norm_post"]
SHARDED = {"w_in": 2, "w_uq": 2, "w_uk": 2, "w_uv": 2, "w_attn_o": 2, "conv_w": 2, "w_conv_o": 2, "w_pool_o": 2, "w_mix_o": 1,
           "w_gate": 2, "w_up": 2, "w_down": 1}
REPLICATED = [n for n in WEIGHTS if n not in SHARDED]
ROW_PARAMS = [n for n in REPLICATED if n != "pool_w"]
ROWS_MINOR = ("w_in", "w_uq", "w_gate", "w_up", "conv_w")
N_CHIPS = 4
N_MIX_GROUPS = 6
CONV_WIRE_ROWS = 32
GROUPS = [(("w_in",), 1), (("w_uq",), 1), (("w_uk", "w_uv"), 1), (("w_attn_o", "w_conv_o", "w_pool_o"), 1), (("conv_w",), 1),
          (("w_mix_o",), 1), (("w_gate", "w_up"), 2), (("w_down",), 1)]


def _join(parts, axis):
    return parts[0] if len(parts) == 1 else jnp.concatenate(parts, axis=axis)


def _split_group(arr, names, axis, shapes):
    out, off = {}, 0
    ax = arr.ndim - 3 + axis
    for n in names:
        size = shapes[n][axis]
        out[n] = lax.slice_in_dim(arr, off, off + size, axis=ax)
        off += size
    return out


def _pack_rows(vectors):
    blocks = []
    for v in vectors:
        for li in range(v.shape[0]):
            blocks.append(jnp.pad(v[li][None, :], ((0, PACK_ROWS - 1), (0, PACK_W - v.shape[1]))))
    return jnp.concatenate(blocks, axis=0)


def _unpack_rows(packed, shapes):
    out, r = [], 0
    for layers, width in shapes:
        out.append(jnp.stack([packed[r + PACK_ROWS * li, :width] for li in range(layers)]))
        r += PACK_ROWS * layers
    return out


def kernel(x, positions, mix_norm_pre, w_in, q_norm, w_uq, kv_norm, w_uk, w_uv, w_attn_o, conv_w, conv_b, conv_ln_g, conv_ln_b, w_conv_o, pool_w, pool_scale, w_pool_o, w_mix_o, mix_norm_post, ffn_norm_pre, w_gate, w_up, w_down, ffn_norm_post, loss_target, m_mix_norm_pre, m_w_in, m_q_norm, m_w_uq, m_kv_norm, m_w_uk, m_w_uv, m_w_attn_o, m_conv_w, m_conv_b, m_conv_ln_g, m_conv_ln_b, m_w_conv_o, m_pool_w, m_pool_scale, m_w_pool_o, m_w_mix_o, m_mix_norm_post, m_ffn_norm_pre, m_w_gate, m_w_up, m_w_down, m_ffn_norm_post, v_mix_norm_pre, v_w_in, v_q_norm, v_w_uq, v_kv_norm, v_w_uk, v_w_uv, v_w_attn_o, v_conv_w, v_conv_b, v_conv_ln_g, v_conv_ln_b, v_w_conv_o, v_pool_w, v_pool_scale, v_w_pool_o, v_w_mix_o, v_mix_norm_post, v_ffn_norm_pre, v_w_gate, v_w_up, v_w_down, v_ffn_norm_post):
    given = dict(mix_norm_pre=mix_norm_pre, w_in=w_in, q_norm=q_norm, w_uq=w_uq, kv_norm=kv_norm, w_uk=w_uk, w_uv=w_uv, w_attn_o=w_attn_o,
                 conv_w=conv_w, conv_b=conv_b, conv_ln_g=conv_ln_g, conv_ln_b=conv_ln_b, w_conv_o=w_conv_o, pool_w=pool_w,
                 pool_scale=pool_scale, w_pool_o=w_pool_o, w_mix_o=w_mix_o, mix_norm_post=mix_norm_post, ffn_norm_pre=ffn_norm_pre,
                 w_gate=w_gate, w_up=w_up, w_down=w_down, ffn_norm_post=ffn_norm_post)
    mom = dict(mix_norm_pre=m_mix_norm_pre, w_in=m_w_in, q_norm=m_q_norm, w_uq=m_w_uq, kv_norm=m_kv_norm, w_uk=m_w_uk, w_uv=m_w_uv,
               w_attn_o=m_w_attn_o, conv_w=m_conv_w, conv_b=m_conv_b, conv_ln_g=m_conv_ln_g, conv_ln_b=m_conv_ln_b, w_conv_o=m_w_conv_o,
               pool_w=m_pool_w, pool_scale=m_pool_scale, w_pool_o=m_w_pool_o, w_mix_o=m_w_mix_o, mix_norm_post=m_mix_norm_post,
               ffn_norm_pre=m_ffn_norm_pre, w_gate=m_w_gate, w_up=m_w_up, w_down=m_w_down, ffn_norm_post=m_ffn_norm_post)
    var = dict(mix_norm_pre=v_mix_norm_pre, w_in=v_w_in, q_norm=v_q_norm, w_uq=v_w_uq, kv_norm=v_kv_norm, w_uk=v_w_uk, w_uv=v_w_uv,
               w_attn_o=v_w_attn_o, conv_w=v_conv_w, conv_b=v_conv_b, conv_ln_g=v_conv_ln_g, conv_ln_b=v_conv_ln_b, w_conv_o=v_w_conv_o,
               pool_w=v_pool_w, pool_scale=v_pool_scale, w_pool_o=v_w_pool_o, w_mix_o=v_w_mix_o, mix_norm_post=v_mix_norm_post,
               ffn_norm_pre=v_ffn_norm_pre, w_gate=v_w_gate, w_up=v_w_up, w_down=v_w_down, ffn_norm_post=v_ffn_norm_post)
    s_len = x.shape[1]
    sharded_names = [n for n in WEIGHTS if n in SHARDED]
    chip = 2 * lax.axis_index("x") + lax.axis_index("y")
    place = jnp.stack([chip, lax.axis_index("c")]).astype(jnp.int32)
    shard_shape = {n: given[n].shape for n in sharded_names}

    mix_groups, ffn_groups = GROUPS[:N_MIX_GROUPS], GROUPS[N_MIX_GROUPS:]
    weight_wire_shape = {n: (N_LAYERS, 2 * CONV_WIRE_ROWS, shard_shape[n][2]) if n == "conv_w" else shard_shape[n] for n in sharded_names}
    grad_wire_shape = {n: (N_LAYERS, CONV_WIRE_ROWS, shard_shape[n][2]) if n == "conv_w" else shard_shape[n] for n in sharded_names}
    pad_rows = lambda a: jnp.pad(a, ((0, CONV_WIRE_ROWS - CONV_W), (0, 0)))

    def weight_wires(groups, li):
        def wire(name):
            a = given[name][li]
            if name == "conv_w":
                hi = a.astype(BF16)
                return jnp.concatenate([pad_rows(hi), pad_rows((a - hi.astype(F32)).astype(BF16))], axis=0)
            return a.astype(BF16)

        return [_join([wire(n) for n in names], axis - 1) for names, axis in groups]

    def full_weights(groups, li, local, gathered):
        p = {n: given[n][li] for n in REPLICATED}
        p["pool_w"] = p["pool_w"].astype(BF16)
        for (names, axis), loc, got in zip(groups, local, gathered):
            got = lax.dynamic_update_slice(got, loc[None], (chip, 0, 0))
            per_chip = [_split_group(got[j], names, axis, weight_wire_shape) for j in range(N_CHIPS)]
            for n in names:
                parts = [pc[n] for pc in per_chip]
                if n == "conv_w":
                    parts = [q[:CONV_W].astype(F32) + q[CONV_WIRE_ROWS:CONV_WIRE_ROWS + CONV_W].astype(F32) for q in parts]
                p[n] = jnp.concatenate(parts, axis=SHARDED[n] - 1)
        return align_weights(p)

    def chip_partials(groups, g, tag):
        wires = []
        for names, axis in groups:
            split = {n: jnp.split(pad_rows(g[n]) if n == "conv_w" else g[n], N_CHIPS, axis=SHARDED[n] - 1) for n in names}
            wires.append(jnp.stack([_join([split[n][j].astype(BF16) for n in names], axis - 1) for j in range(N_CHIPS)]))
        theirs = sibling_swap(wires, "sibling_swap_" + tag)
        return [sibling_sum(w, t, place, f"sibling_sum_{tag}_{i}") for i, (w, t) in enumerate(zip(wires, theirs))]

    tabs = rope_tables(positions.reshape(s_len, 1), s_len)
    loc_m0 = weight_wires(mix_groups, 0)
    w_m0 = full_weights(mix_groups, 0, loc_m0, run_exchange(GatherShards(loc_m0), "gather_mix_l0"))
    loc_f0, loc_m1 = weight_wires(ffn_groups, 0), weight_wires(mix_groups, 1)
    h, sv_m0, got = mixer_fwd(x[0], tabs, w_m0, "_l0", GatherShards(loc_f0 + loc_m1))
    w_f0 = full_weights(ffn_groups, 0, loc_f0, got[:len(loc_f0)])
    w_m1 = full_weights(mix_groups, 1, loc_m1, got[len(loc_f0):])
    h, sv_f0 = ffn_fwd(h, w_f0, "_l0")
    loc_f1 = weight_wires(ffn_groups, 1)
    h, sv_m1, got = mixer_fwd(h, tabs, w_m1, "_l1", GatherShards(loc_f1))
    w_f1 = full_weights(ffn_groups, 1, loc_f1, got)
    h, sv_f1 = ffn_fwd(h, w_f1, "_l1")
    dh, loss_local = loss_head(h, loss_target[0])
    loss = lax.psum(loss_local[0, 0], MESH_AXES)

    dh, g_f1 = ffn_bwd(dh, sv_f1, w_f1, "_l1")
    g_f1 = unalign_grads(g_f1)
    p_f1 = chip_partials(ffn_groups, g_f1, "ffn_l1")
    dh, g_m1, o_f1 = mixer_bwd(dh, sv_m1, tabs, w_m1, "_l1", ChipExchange(p_f1))
    g_m1 = unalign_grads(g_m1)
    p_m1 = chip_partials(mix_groups, g_m1, "mix_l1")
    dh, g_f0 = ffn_bwd(dh, sv_f0, w_f0, "_l0")
    g_f0 = unalign_grads(g_f0)
    p_f0 = chip_partials(ffn_groups, g_f0, "ffn_l0")
    grad_x, g_m0, got = mixer_bwd(dh, sv_m0, tabs, w_m0, "_l0", ChipExchange(p_m1 + p_f0))
    o_m1, o_f0 = got[:len(p_m1)], got[len(p_m1):]
    g_m0 = unalign_grads(g_m0)
    p_m0 = chip_partials(mix_groups, g_m0, "mix_l0")
    o_m0 = run_exchange(ChipExchange(p_m0), "chip_exchange_mix_l0")
    grads = [{**g_m0, **g_f0}, {**g_m1, **g_f1}]
    grad_full = {n: jnp.stack([g[n].reshape(given[n].shape[1:]) for g in grads]) for n in REPLICATED}

    sums = {}
    for groups, base, per_layer in ((ffn_groups, N_MIX_GROUPS, ((1, p_f1, o_f1), (0, p_f0, o_f0))), (mix_groups, 0, ((1, p_m1, o_m1), (0, p_m0, o_m0)))):
        for li, parts, others in per_layer:
            for i, (p, o) in enumerate(zip(parts, others)):
                sums[base + i] = chip_sum(p, o, place, li, sums.get(base + i), f"chip_sum_{base + i}_l{li}")
    g_shard = {}
    for (names, axis), s in zip(GROUPS, sibling_gather([sums[i] for i in range(len(GROUPS))])):
        g_shard.update(_split_group(s, names, axis, grad_wire_shape))
    g_shard["conv_w"] = g_shard["conv_w"][:, :CONV_W]

    row_shapes = [given[n].shape for n in ROW_PARAMS]
    rows_all, pool_w_all = gather_all([_pack_rows([grad_full[n] for n in ROW_PARAMS]), grad_full["pool_w"].reshape(-1, POOL_GD)])
    g_rows = sum_devices(rows_all, "row_params_sum")
    g_pool_w = sum_devices(pool_w_all, "pool_w_sum")
    g_rep = dict(zip(ROW_PARAMS, _unpack_rows(g_rows, row_shapes)))
    g_rep["pool_w"] = g_pool_w.reshape(given["pool_w"].shape)

    g_out, d_out, m_out, v_out = {}, {}, {}, {}
    for n in sharded_names + ["pool_w"]:
        shp = given[n].shape
        three_d = (shp[0], int(np.prod(shp[1:-1])), shp[-1])
        g_n = g_shard[n] if n in SHARDED else g_rep[n]
        view = (lambda a: jnp.swapaxes(a.reshape(three_d), 1, 2)) if n in ROWS_MINOR else (lambda a: a.reshape(three_d))
        back = (lambda a: jnp.swapaxes(a, 1, 2).reshape(shp)) if n in ROWS_MINOR else (lambda a: a.reshape(shp))
        d, mn, vn = adamw(view(given[n]), view(g_n), view(mom[n]), view(var[n]), "adamw_" + n)
        g_out[n], d_out[n], m_out[n], v_out[n] = g_n, back(d), back(mn), back(vn)
    rd, rm, rv = adamw(_pack_rows([given[n] for n in ROW_PARAMS])[None], g_rows[None], _pack_rows([mom[n] for n in ROW_PARAMS])[None],
                       _pack_rows([var[n] for n in ROW_PARAMS])[None], "adamw_row_params")
    for n, d, mn, vn in zip(ROW_PARAMS, *[_unpack_rows(a[0], row_shapes) for a in (rd, rm, rv)]):
        g_out[n], d_out[n], m_out[n], v_out[n] = g_rep[n], d, mn, vn

    return (loss, grad_x[None], *[g_out[n] for n in WEIGHTS], *[d_out[n] for n in WEIGHTS], *[m_out[n] for n in WEIGHTS],
            *[v_out[n] for n in WEIGHTS])
```

```python
import functools
import math

import numpy as np
import jax
import jax.numpy as jnp
from jax import lax
from jax.experimental import pallas as pl
from jax.experimental.pallas import tpu as pltpu

F32, BF16 = jnp.float32, jnp.bfloat16

D_MODEL = 1024
N_HEADS = 8
NOPE, ROPE, VDIM = 64, 32, 64
HALF_ROPE = ROPE // 2
Q_RANK, KV_RANK = 384, 256
CONV_C, CONV_W = 512, 31
POOL_C, POOL_G, POOL_GD = 512, 4, 128
POOL_WINDOWS = (2, 4, 8, 16)
D_FF = 2816
FF_HALF = D_FF // 2
N_LAYERS = 2
EPS = 1e-6
ROPE_THETA = 10000.0
ATT_SCALE = 1.0 / math.sqrt(NOPE + ROPE)
O_Q, O_KV, O_KR, O_CONV, O_POOL, O_GATE, D_IN = 0, 384, 640, 672, 1696, 2208, 5280

LANE = 128
HP = 128
ZG, ZC, ZP, ZA, ZW = 0, 3072, 4096, 4608, 5376
ZA_W = Q_RANK + KV_RANK + HP
KR_LANE = NOPE
HW = N_HEADS * HP

ADAM_LR, ADAM_B1, ADAM_B2, ADAM_EPS, ADAM_WD, ADAM_STEP = 0.001, 0.9, 0.999, 1e-08, 0.01, 10

ROW_TILE = 512
WIDE_ROW_TILE = 256
ATT_TILE_FWD = 1024
ATT_TILE_BWD = 512
ATT_HEADS = 4
CONV_CHUNK = 256
MM_TM, MM_TN, MM_TK = 1024, 1408, 1024
MM_TILE_MAX = 2048
FFN_TM = 512
MM_VMEM_BUDGET = 40 * 1024 * 1024
HBM_BYTES_PER_US = 3.0e6
GRID_STEP_US = 0.35
VMEM_LIMIT = 56 * 1024 * 1024
SUM_TILE_BYTES = 1024 * 1024

HALF_ALIGN = 16
MESH_AXES = ("x", "y", "c")
PACK_W = 1024
PACK_ROWS = 8


def _cparams(sem):
    return pltpu.CompilerParams(dimension_semantics=sem, vmem_limit_bytes=VMEM_LIMIT)


def _tile(n, target):
    if n <= target:
        return n
    best = None
    for t in range(LANE, target + 1, LANE):
        if n % t == 0:
            best = t
    assert best is not None, (n, target)
    return best


def _mm_tiles(m, n, k, a_bytes, b_bytes, out_bytes):
    divs = lambda d: sorted({t for t in range(LANE, min(d, MM_TILE_MAX) + 1, LANE) if d % t == 0} | ({d} if d <= MM_TILE_MAX else set()))
    best = None
    for tm in divs(m):
        for tn in divs(n):
            blocks = tm * k * a_bytes + k * tn * b_bytes + tm * tn * out_bytes
            if 2 * blocks + tm * tn * 4 > MM_VMEM_BUDGET:
                continue
            steps = (m // tm) * (n // tn)
            for rows_outer in (True, False):
                moved = (m * k * a_bytes + k * n * b_bytes * (m // tm)) if rows_outer else (k * n * b_bytes + m * k * a_bytes * (n // tn))
                cost = (moved + m * n * out_bytes + blocks) / HBM_BYTES_PER_US + steps * GRID_STEP_US
                if best is None or cost < best[0]:
                    best = (cost, tm, tn, rows_outer)
    if best is not None:
        return best[1], best[2], k, best[3]
    return _tile(m, MM_TM), _tile(n, MM_TN), _tile(k, MM_TK), True


def mm(a, b, *, ta=False, tb=False, out_dtype=F32, name):
    m, k = (a.shape[1], a.shape[0]) if ta else a.shape
    n, k2 = b.shape if tb else (b.shape[1], b.shape[0])
    assert k == k2, (a.shape, b.shape, ta, tb)
    tm, tn, tk, rows_outer = _mm_tiles(m, n, k, a.dtype.itemsize, b.dtype.itemsize, jnp.dtype(out_dtype).itemsize)
    nk = k // tk
    dims = (((0 if ta else 1,), (1 if tb else 0,)), ((), ()))

    def body(a_ref, b_ref, o_ref, *acc):
        part = lax.dot_general(a_ref[...].astype(BF16), b_ref[...].astype(BF16), dims, preferred_element_type=F32)
        if nk == 1:
            o_ref[...] = part.astype(o_ref.dtype)
            return
        (acc_ref,) = acc
        kk = pl.program_id(2)

        @pl.when(kk == 0)
        def _():
            acc_ref[...] = part

        @pl.when(kk > 0)
        def _():
            acc_ref[...] += part

        @pl.when(kk == nk - 1)
        def _():
            o_ref[...] = acc_ref[...].astype(o_ref.dtype)

    ij = (lambda g0, g1: (g0, g1)) if rows_outer else (lambda g0, g1: (g1, g0))

    def a_map(g0, g1, kk):
        i, _ = ij(g0, g1)
        return (kk, i) if ta else (i, kk)

    def b_map(g0, g1, kk):
        _, j = ij(g0, g1)
        return (j, kk) if tb else (kk, j)

    a_spec = pl.BlockSpec((tk, tm) if ta else (tm, tk), a_map)
    b_spec = pl.BlockSpec((tn, tk) if tb else (tk, tn), b_map)
    return pl.pallas_call(
        body,
        grid=(m // tm, n // tn, nk) if rows_outer else (n // tn, m // tm, nk),
        in_specs=[a_spec, b_spec],
        out_specs=pl.BlockSpec((tm, tn), lambda g0, g1, kk: ij(g0, g1)),
        out_shape=jax.ShapeDtypeStruct((m, n), out_dtype),
        scratch_shapes=[] if nk == 1 else [pltpu.VMEM((tm, tn), F32)],
        compiler_params=_cparams(("parallel", "parallel", "arbitrary")),
        name=name,
    )(a, b)


def ffn_in(hn, w_gu, name):
    s_len, k = hn.shape
    tm = min(FFN_TM, s_len)

    def body(a_ref, b_ref, gu_ref, act_ref):
        r = jnp.dot(a_ref[...], b_ref[...], preferred_element_type=F32)
        gu_ref[...] = r.astype(gu_ref.dtype)
        act_ref[...] = (_silu(r[:, :FF_HALF]) * r[:, FF_HALF:]).astype(act_ref.dtype)

    return pl.pallas_call(
        body,
        grid=(2, s_len // tm),
        in_specs=[pl.BlockSpec((tm, k), lambda j, i: (i, 0)), pl.BlockSpec((k, 2 * FF_HALF), lambda j, i: (0, j))],
        out_specs=[pl.BlockSpec((tm, 2 * FF_HALF), lambda j, i: (i, j)), pl.BlockSpec((tm, FF_HALF), lambda j, i: (i, j))],
        out_shape=[jax.ShapeDtypeStruct((s_len, 2 * D_FF), BF16), jax.ShapeDtypeStruct((s_len, D_FF), BF16)],
        compiler_params=_cparams(("arbitrary", "parallel")),
        name=name,
    )(hn, w_gu)


def ffn_out_dx(d_y, w_down, gu, name):
    s_len, k = d_y.shape
    tm = min(FFN_TM, s_len)

    def body(a_ref, b_ref, gu_ref, dgu_ref):
        da = lax.dot_general(a_ref[...], b_ref[...], NT_DIMS, preferred_element_type=F32)
        gt = gu_ref[:, :FF_HALF].astype(F32)
        up = gu_ref[:, FF_HALF:].astype(F32)
        sg = jax.nn.sigmoid(gt)
        dgu_ref[:, :FF_HALF] = (da * up * sg * (1.0 + gt * (1.0 - sg))).astype(dgu_ref.dtype)
        dgu_ref[:, FF_HALF:] = (da * gt * sg).astype(dgu_ref.dtype)

    pair = pl.BlockSpec((tm, 2 * FF_HALF), lambda j, i: (i, j))
    return pl.pallas_call(
        body,
        grid=(2, s_len // tm),
        in_specs=[pl.BlockSpec((tm, k), lambda j, i: (i, 0)), pl.BlockSpec((FF_HALF, k), lambda j, i: (j, 0)), pair],
        out_specs=pair,
        out_shape=jax.ShapeDtypeStruct((s_len, 2 * D_FF), BF16),
        compiler_params=_cparams(("parallel", "arbitrary")),
        name=name,
    )(d_y, w_down, gu)


def rowwise(name, body, rows, row_ins, full_ins, row_outs, acc_outs=(), into=None, tile=None):
    tile = min(tile or ROW_TILE, rows)
    into = into or {}
    in_specs = [pl.BlockSpec((tile, w), lambda i, cb=cb: (i, cb)) for _, w, cb in row_ins]
    in_specs += [pl.BlockSpec(a.shape, lambda i, nd=a.ndim: (0,) * nd) for a in full_ins]
    in_specs += [ANY for _ in into]
    n_in = len(row_ins) + len(full_ins)
    aliases = {n_in + k: oi for k, oi in enumerate(into)}
    out_specs, out_shape = [], []
    for ro in row_outs:
        w, dt, full_w, cb = ro if len(ro) == 4 else (*ro, ro[0], 0)
        out_specs.append(pl.BlockSpec((tile, w), lambda i, cb=cb: (i, cb)))
        out_shape.append(jax.ShapeDtypeStruct((rows, full_w), dt))
    out_specs += [pl.BlockSpec(s, lambda i, nd=len(s): (0,) * nd) for s, _ in acc_outs]
    out_shape += [jax.ShapeDtypeStruct(s, dt) for s, dt in acc_outs]
    n_refs = n_in

    def call_body(*refs):
        body(*refs[:n_refs], *refs[n_refs + len(into):])

    outs = pl.pallas_call(
        call_body,
        grid=(rows // tile,),
        in_specs=in_specs,
        out_specs=out_specs,
        out_shape=out_shape,
        input_output_aliases=aliases,
        compiler_params=_cparams(("arbitrary",)),
        name=name,
    )(*[a for a, _, _ in row_ins], *full_ins, *into.values())
    return outs


def _whole(a):
    return (a, a.shape[1], 0)


def _acc(ref, val):
    @pl.when(pl.program_id(0) == 0)
    def _():
        ref[...] = val

    @pl.when(pl.program_id(0) > 0)
    def _():
        ref[...] += val


def _rms(x, g):
    return x * lax.rsqrt(jnp.mean(x * x, axis=-1, keepdims=True) + EPS) * g


def _layer_norm(x, g, b):
    mu = jnp.mean(x, axis=-1, keepdims=True)
    xc = x - mu
    return xc * lax.rsqrt(jnp.mean(xc * xc, axis=-1, keepdims=True) + EPS) * g + b


def _silu(x):
    return x * jax.nn.sigmoid(x)


def _rope(x, cc, sa, sb):
    return x * cc + pltpu.roll(x, HALF_ROPE, 1) * sa + pltpu.roll(x, HP - HALF_ROPE, 1) * sb


def _rope_t(dy, cc, sa, sb):
    return dy * cc + pltpu.roll(dy * sa, HP - HALF_ROPE, 1) + pltpu.roll(dy * sb, HALF_ROPE, 1)


def rope_tables(pos_col, rows):
    lane = np.arange(HP)
    idx = np.where(lane < KR_LANE + HALF_ROPE, lane - KR_LANE, lane - KR_LANE - HALF_ROPE)
    in_rope = (lane >= KR_LANE) & (lane < KR_LANE + ROPE)
    inv_freq = (np.float32(ROPE_THETA) ** (-np.arange(0, ROPE, 2, dtype=np.float32) / np.float32(ROPE))).astype(np.float32)
    freq_row = np.where(in_rope, inv_freq[np.clip(idx, 0, HALF_ROPE - 1)], 0.0).astype(np.float32)[None, :]
    first = ((lane >= KR_LANE) & (lane < KR_LANE + HALF_ROPE)).astype(np.float32)[None, :]
    second = ((lane >= KR_LANE + HALF_ROPE) & (lane < KR_LANE + ROPE)).astype(np.float32)[None, :]

    def body(pos_ref, f_ref, a_ref, b_ref, cc_ref, sa_ref, sb_ref):
        ang = pos_ref[...].astype(F32) * f_ref[...]
        s = jnp.sin(ang)
        cc_ref[...] = jnp.cos(ang)
        sa_ref[...] = s * b_ref[...]
        sb_ref[...] = -s * a_ref[...]

    return rowwise("rope_tables", body, rows, [_whole(pos_col)], [jnp.asarray(freq_row), jnp.asarray(first), jnp.asarray(second)],
                   [(HP, F32)] * 3)


def _causal_mask(t):
    r = lax.broadcasted_iota(jnp.int32, (t, t), 0)
    c = lax.broadcasted_iota(jnp.int32, (t, t), 1)
    return r, c


NT_DIMS = (((1,), (1,)), ((), ()))


def _carried(carry, refs, n_in, n_out, n_scratch):
    if carry is None:
        return refs, None, None
    ni, no = len(carry.ins), len(carry.outs)
    own_in, ex_in = refs[:n_in], refs[n_in:n_in + ni]
    own_out, ex_out = refs[n_in + ni:n_in + ni + n_out], refs[n_in + ni + n_out:n_in + ni + n_out + no]
    scratch = refs[n_in + ni + n_out + no:]
    sems = scratch[n_scratch:]
    return (*own_in, *own_out, *scratch[:n_scratch]), (lambda: carry.start(ex_in, ex_out, *sems)), (lambda: carry.finish(ex_in, ex_out, *sems))


def _carry_specs(carry):
    if carry is None:
        return [], [], [], [], []
    sems = [pltpu.SemaphoreType.DMA((carry.n_sems,)), pltpu.SemaphoreType.DMA((carry.n_sems,))]
    return [ANY] * len(carry.ins), [ANY] * len(carry.outs), list(carry.outs), sems, list(carry.ins)


def attention_fwd(q, k, v, name, carry=None):
    s_len = q.shape[0]
    t = min(ATT_TILE_FWD, s_len)
    nb = s_len // t
    hb = ATT_HEADS
    w = hb * HP
    nh = N_HEADS // hb

    def body(*refs):
        (q_ref, k_ref, v_ref, o_ref, lse_ref, m_sc, acc_sc), start, finish = _carried(carry, refs, 3, 2, 2)
        qi = pl.program_id(1)
        if start is not None:
            pl.when((pl.program_id(0) == 0) & (qi == 0))(start)
        m_sc[...] = jnp.full_like(m_sc, -jnp.inf)
        acc_sc[...] = jnp.zeros_like(acc_sc)

        def block(j, masked):
            ks = pl.ds(pl.multiple_of(j * t, t), t)
            for hh in range(hb):
                ls = slice(hh * HP, (hh + 1) * HP)
                s = lax.dot_general(q_ref[:, ls], k_ref[ks, ls], NT_DIMS, preferred_element_type=F32) * ATT_SCALE
                if masked:
                    r, c = _causal_mask(t)
                    s = jnp.where(c <= r, s, -jnp.inf)
                m_old = m_sc[hh]
                m_new = jnp.maximum(m_old, jnp.max(s, axis=-1, keepdims=True))
                p = jnp.exp(s - m_new)
                acc_sc[hh] = jnp.exp(m_old - m_new) * acc_sc[hh] + jnp.dot(p.astype(BF16), v_ref[ks, ls], preferred_element_type=F32)
                m_sc[hh] = m_new

        def loop_body(j, carry):
            block(j, False)
            return carry

        lax.fori_loop(0, qi, loop_body, 0)
        block(qi, True)
        lane = lax.broadcasted_iota(jnp.int32, (t, HP), 1)
        for hh in range(hb):
            acc = acc_sc[hh]
            l = jnp.sum(jnp.where(lane == VDIM, acc, 0.0), axis=-1, keepdims=True)
            o_ref[:, hh * HP:(hh + 1) * HP] = jnp.where(lane < VDIM, acc / l, 0.0).astype(o_ref.dtype)
            lse_ref[hh] = m_sc[hh] + jnp.log(l)
        if finish is not None:
            pl.when((pl.program_id(0) == nh - 1) & (qi == nb - 1))(finish)

    ex_in_specs, ex_out_specs, ex_out_shape, ex_scratch, ex_inputs = _carry_specs(carry)
    resident = pl.BlockSpec((s_len, w), lambda h, qi: (0, h))
    o, lse, *carried = pl.pallas_call(
        body,
        grid=(nh, nb),
        in_specs=[pl.BlockSpec((t, w), lambda h, qi: (qi, h)), resident, resident] + ex_in_specs,
        out_specs=[pl.BlockSpec((t, w), lambda h, qi: (qi, h)), pl.BlockSpec((hb, t, 1), lambda h, qi: (h, qi, 0))] + ex_out_specs,
        out_shape=[jax.ShapeDtypeStruct((s_len, HW), BF16), jax.ShapeDtypeStruct((N_HEADS, s_len, 1), F32)] + ex_out_shape,
        scratch_shapes=[pltpu.VMEM((hb, t, 1), F32), pltpu.VMEM((hb, t, HP), F32)] + ex_scratch,
        compiler_params=_cparams(("arbitrary", "arbitrary")),
        name=name,
    )(q, k, v, *ex_inputs)
    return o, lse, carried


def attention_delta(do, o):
    s_len = do.shape[0]
    t = min(ROW_TILE, s_len)

    def body(do_ref, o_ref, d_ref):
        prod = do_ref[...].astype(F32) * o_ref[...].astype(F32)
        for h in range(N_HEADS):
            d_ref[h] = jnp.sum(prod[:, h * HP:(h + 1) * HP], axis=-1, keepdims=True)

    return pl.pallas_call(
        body,
        grid=(s_len // t,),
        in_specs=[pl.BlockSpec((t, HW), lambda i: (i, 0))] * 2,
        out_specs=pl.BlockSpec((N_HEADS, t, 1), lambda i: (0, i, 0)),
        out_shape=jax.ShapeDtypeStruct((N_HEADS, s_len, 1), F32),
        compiler_params=_cparams(("arbitrary",)),
        name="attention_delta",
    )(do, o)


TN_DIMS = (((0,), (0,)), ((), ()))


def attention_bwd(q, k, v, do, lse_row, delta_row, name, carry=None):
    s_len = q.shape[0]
    t = min(ATT_TILE_BWD, s_len)
    nb = s_len // t
    hb = ATT_HEADS
    w = hb * HP
    nh = N_HEADS // hb

    def body(*refs):
        (q_ref, k_ref, v_ref, do_ref, lse_ref, dl_ref, dq_ref, dk_ref, dv_ref, dk_sc, dv_sc), start, finish = _carried(carry, refs, 6, 3, 2)
        ki = pl.program_id(1)
        if start is not None:
            pl.when((pl.program_id(0) == 0) & (ki == 0))(start)

        @pl.when(ki == 0)
        def _():
            dq_ref[...] = jnp.zeros_like(dq_ref)

        dk_sc[...] = jnp.zeros_like(dk_sc)
        dv_sc[...] = jnp.zeros_like(dv_sc)

        def block(j, masked):
            qs = pl.ds(pl.multiple_of(j * t, t), t)
            for hh in range(hb):
                ls = slice(hh * HP, (hh + 1) * HP)
                qb = q_ref[qs, ls]
                dob = do_ref[qs, ls]
                kb = k_ref[:, ls]
                st = lax.dot_general(kb, qb, NT_DIMS, preferred_element_type=F32) * ATT_SCALE
                pt = jnp.exp(st - lse_ref[hh, j])
                if masked:
                    r, c = _causal_mask(t)
                    pt = jnp.where(r <= c, pt, 0.0)
                dv_sc[hh] += jnp.dot(pt.astype(BF16), dob, preferred_element_type=F32)
                dpt = lax.dot_general(v_ref[:, ls], dob, NT_DIMS, preferred_element_type=F32)
                dst = (pt * (dpt - dl_ref[hh, j]) * ATT_SCALE).astype(BF16)
                dk_sc[hh] += jnp.dot(dst, qb, preferred_element_type=F32)
                dq_ref[qs, ls] += lax.dot_general(dst, kb, TN_DIMS, preferred_element_type=F32)

        block(ki, True)

        def loop_body(j, carry):
            block(j, False)
            return carry

        lax.fori_loop(ki + 1, nb, loop_body, 0)
        for hh in range(hb):
            ls = slice(hh * HP, (hh + 1) * HP)
            dk_ref[:, ls] = dk_sc[hh].astype(dk_ref.dtype)
            dv_ref[:, ls] = dv_sc[hh].astype(dv_ref.dtype)
        if finish is not None:
            pl.when((pl.program_id(0) == nh - 1) & (ki == nb - 1))(finish)

    ex_in_specs, ex_out_specs, ex_out_shape, ex_scratch, ex_inputs = _carry_specs(carry)
    k_spec = pl.BlockSpec((t, w), lambda h, ki: (ki, h))
    resident = pl.BlockSpec((s_len, w), lambda h, ki: (0, h))
    row_spec = pl.BlockSpec((hb, nb, 1, t), lambda h, ki: (h, 0, 0, 0))
    dq, dk, dv, *carried = pl.pallas_call(
        body,
        grid=(nh, nb),
        in_specs=[resident, k_spec, k_spec, resident, row_spec, row_spec] + ex_in_specs,
        out_specs=[resident, k_spec, k_spec] + ex_out_specs,
        out_shape=[jax.ShapeDtypeStruct((s_len, HW), F32), jax.ShapeDtypeStruct((s_len, HW), F32), jax.ShapeDtypeStruct((s_len, HW), BF16)]
        + ex_out_shape,
        scratch_shapes=[pltpu.VMEM((hb, t, HP), F32), pltpu.VMEM((hb, t, HP), F32)] + ex_scratch,
        compiler_params=_cparams(("arbitrary", "arbitrary")),
        name=name,
    )(q, k, v, do, lse_row, delta_row, *ex_inputs)
    return dq, dk, dv, carried


CONV_PAD = 32


def conv_fwd(z, conv_w, conv_b):
    s_len = z.shape[0]
    ch = min(CONV_CHUNK, s_len)

    def body(ag_ref, w_ref, b_ref, c_ref, pad_ref):
        pad_ref[0:CONV_PAD, :] = jnp.zeros((CONV_PAD, LANE), F32)
        pad_ref[CONV_PAD:CONV_PAD + s_len, :] = ag_ref[:, 0:LANE] * jax.nn.sigmoid(ag_ref[:, LANE:2 * LANE])

        def chunk(i, carry):
            base = pl.multiple_of(i * ch, ch)
            acc = jnp.zeros((ch, LANE), F32) + b_ref[...]
            for kk in range(CONV_W):
                acc = acc + pad_ref[pl.ds(base + CONV_PAD - (CONV_W - 1) + kk, ch), :] * w_ref[kk:kk + 1, :]
            c_ref[pl.ds(base, ch), :] = acc
            return carry

        lax.fori_loop(0, s_len // ch, chunk, 0)

    nblk = CONV_C // LANE
    return pl.pallas_call(
        body,
        grid=(nblk,),
        in_specs=[pl.BlockSpec((s_len, 2 * LANE), lambda j: (0, ZC // (2 * LANE) + j)),
                  pl.BlockSpec((CONV_W, LANE), lambda j: (0, j)), pl.BlockSpec((1, LANE), lambda j: (0, j))],
        out_specs=pl.BlockSpec((s_len, LANE), lambda j: (0, j)),
        out_shape=jax.ShapeDtypeStruct((s_len, CONV_C), F32),
        scratch_shapes=[pltpu.VMEM((s_len + CONV_PAD, LANE), F32)],
        compiler_params=_cparams(("arbitrary",)),
        name="conv_fwd",
    )(z, conv_w, conv_b)


def conv_bwd(z, dc, conv_w, dz):
    s_len = z.shape[0]
    ch = min(CONV_CHUNK, s_len)

    def body(ag_ref, dc_ref, w_ref, dz_in, dag_ref, dw_ref, db_ref, pad_ref, dpad_ref, wacc_ref):
        del dz_in
        pad_ref[0:CONV_PAD, :] = jnp.zeros((CONV_PAD, LANE), F32)
        pad_ref[CONV_PAD:CONV_PAD + s_len, :] = ag_ref[:, 0:LANE] * jax.nn.sigmoid(ag_ref[:, LANE:2 * LANE])
        dpad_ref[0:s_len, :] = dc_ref[...]
        dpad_ref[s_len:s_len + CONV_PAD, :] = jnp.zeros((CONV_PAD, LANE), F32)
        wacc_ref[...] = jnp.zeros_like(wacc_ref)
        db_ref[...] = jnp.sum(dc_ref[...], axis=0, keepdims=True)

        def chunk(i, carry):
            base = pl.multiple_of(i * ch, ch)
            dcc = dpad_ref[pl.ds(base, ch), :]
            dh = jnp.zeros((ch, LANE), F32)
            for kk in range(CONV_W):
                dh = dh + dpad_ref[pl.ds(base + (CONV_W - 1) - kk, ch), :] * w_ref[kk:kk + 1, :]
                prod = dcc * pad_ref[pl.ds(base + CONV_PAD - (CONV_W - 1) + kk, ch), :]
                wacc_ref[kk * 8:(kk + 1) * 8, :] += prod.reshape(ch // 8, 8, LANE).sum(axis=0)
            a = ag_ref[pl.ds(base, ch), 0:LANE]
            sgc = jax.nn.sigmoid(ag_ref[pl.ds(base, ch), LANE:2 * LANE])
            dag_ref[pl.ds(base, ch), 0:LANE] = (dh * sgc).astype(dag_ref.dtype)
            dag_ref[pl.ds(base, ch), LANE:2 * LANE] = (dh * a * sgc * (1.0 - sgc)).astype(dag_ref.dtype)
            return carry

        lax.fori_loop(0, s_len // ch, chunk, 0)
        for kk in range(CONV_W):
            dw_ref[kk:kk + 1, :] = jnp.sum(wacc_ref[kk * 8:(kk + 1) * 8, :], axis=0, keepdims=True)

    nblk = CONV_C // LANE
    pair = pl.BlockSpec((s_len, 2 * LANE), lambda j: (0, ZC // (2 * LANE) + j))
    return pl.pallas_call(
        body,
        grid=(nblk,),
        in_specs=[pair, pl.BlockSpec((s_len, LANE), lambda j: (0, j)), pl.BlockSpec((CONV_W, LANE), lambda j: (0, j)), ANY],
        out_specs=[pair, pl.BlockSpec((CONV_W, LANE), lambda j: (0, j)), pl.BlockSpec((1, LANE), lambda j: (0, j))],
        out_shape=[jax.ShapeDtypeStruct(dz.shape, dz.dtype), jax.ShapeDtypeStruct((CONV_W, CONV_C), F32), jax.ShapeDtypeStruct((1, CONV_C), F32)],
        scratch_shapes=[pltpu.VMEM((s_len + CONV_PAD, LANE), F32), pltpu.VMEM((s_len + CONV_PAD, LANE), F32),
                        pltpu.VMEM((CONV_W * 8, LANE), F32)],
        input_output_aliases={3: 0},
        compiler_params=_cparams(("arbitrary",)),
        name="conv_bwd",
    )(z, dc, conv_w, dz)


POOL_PAD = 16


def _pool_count(base, ch, w):
    t = base + lax.broadcasted_iota(jnp.int32, (ch, 1), 0)
    return jnp.minimum(t + 1, w).astype(F32)


def pool_fwd(z, pool_w, pool_scale):
    s_len = z.shape[0]
    ch = min(CONV_CHUNK, s_len)

    def body(u_ref, pw_ref, sc_ref, m_ref, pad_ref):
        gi = pl.program_id(0)
        pad_ref[0:POOL_PAD, :] = jnp.zeros((POOL_PAD, LANE), F32)
        pad_ref[POOL_PAD:POOL_PAD + s_len, :] = u_ref[...]

        def run(w):
            def chunk(i, carry):
                base = pl.multiple_of(i * ch, ch)
                acc = jnp.zeros((ch, LANE), F32)
                for j in range(w):
                    acc = acc + pad_ref[pl.ds(base + POOL_PAD - j, ch), :]
                d = acc / _pool_count(base, ch, w) - u_ref[pl.ds(base, ch), :]
                md = jnp.dot(d.astype(BF16), pw_ref[0], preferred_element_type=F32)
                m_ref[pl.ds(base, ch), :] = (md * sc_ref[...]).astype(m_ref.dtype)
                return carry

            lax.fori_loop(0, s_len // ch, chunk, 0)

        for g, w in enumerate(POOL_WINDOWS):
            pl.when(gi == g)(functools.partial(run, w))

    return pl.pallas_call(
        body,
        grid=(POOL_G,),
        in_specs=[pl.BlockSpec((s_len, LANE), lambda g: (0, ZP // LANE + g)), pl.BlockSpec((1, POOL_GD, POOL_GD), lambda g: (g, 0, 0)),
                  pl.BlockSpec((1, LANE), lambda g: (0, g))],
        out_specs=pl.BlockSpec((s_len, LANE), lambda g: (0, g)),
        out_shape=jax.ShapeDtypeStruct((s_len, POOL_C), BF16),
        scratch_shapes=[pltpu.VMEM((s_len + POOL_PAD, LANE), F32)],
        compiler_params=_cparams(("arbitrary",)),
        name="pool_fwd",
    )(z, pool_w, pool_scale)


def pool_bwd(z, dm, pool_w, pool_scale, dz):
    s_len = z.shape[0]
    ch = min(CONV_CHUNK, s_len)

    def body(u_ref, dm_ref, pw_ref, sc_ref, dz_in, du_ref, dpw_ref, dsc_ref, pad_ref, epad_ref, dd_ref, sacc_ref):
        del dz_in
        gi = pl.program_id(0)
        pad_ref[0:POOL_PAD, :] = jnp.zeros((POOL_PAD, LANE), F32)
        pad_ref[POOL_PAD:POOL_PAD + s_len, :] = u_ref[...]
        epad_ref[s_len:s_len + POOL_PAD, :] = jnp.zeros((POOL_PAD, LANE), F32)
        dpw_ref[...] = jnp.zeros_like(dpw_ref)
        sacc_ref[...] = jnp.zeros_like(sacc_ref)

        def run(w):
            def first(i, carry):
                base = pl.multiple_of(i * ch, ch)
                acc = jnp.zeros((ch, LANE), F32)
                for j in range(w):
                    acc = acc + pad_ref[pl.ds(base + POOL_PAD - j, ch), :]
                cnt = _pool_count(base, ch, w)
                d = (acc / cnt - u_ref[pl.ds(base, ch), :]).astype(BF16)
                md = jnp.dot(d, pw_ref[0], preferred_element_type=F32)
                dmc = dm_ref[pl.ds(base, ch), :]
                sacc_ref[...] += (dmc * md).reshape(ch // 8, 8, LANE).sum(axis=0)
                dmd = (dmc * sc_ref[...]).astype(BF16)
                dpw_ref[0] += lax.dot_general(d, dmd, (((0,), (0,)), ((), ())), preferred_element_type=F32)
                dd = lax.dot_general(dmd, pw_ref[0], (((1,), (1,)), ((), ())), preferred_element_type=F32)
                dd_ref[pl.ds(base, ch), :] = dd
                epad_ref[pl.ds(base, ch), :] = dd / cnt
                return carry

            lax.fori_loop(0, s_len // ch, first, 0)

            def second(i, carry):
                base = pl.multiple_of(i * ch, ch)
                acc = jnp.zeros((ch, LANE), F32)
                for j in range(w):
                    acc = acc + epad_ref[pl.ds(base + j, ch), :]
                du_ref[pl.ds(base, ch), :] = (acc - dd_ref[pl.ds(base, ch), :]).astype(du_ref.dtype)
                return carry

            lax.fori_loop(0, s_len // ch, second, 0)

        for g, w in enumerate(POOL_WINDOWS):
            pl.when(gi == g)(functools.partial(run, w))
        dsc_ref[...] = jnp.sum(sacc_ref[...], axis=0, keepdims=True)

    return pl.pallas_call(
        body,
        grid=(POOL_G,),
        in_specs=[pl.BlockSpec((s_len, LANE), lambda g: (0, ZP // LANE + g)), pl.BlockSpec((s_len, LANE), lambda g: (0, g)),
                  pl.BlockSpec((1, POOL_GD, POOL_GD), lambda g: (g, 0, 0)), pl.BlockSpec((1, LANE), lambda g: (0, g)), ANY],
        out_specs=[pl.BlockSpec((s_len, LANE), lambda g: (0, ZP // LANE + g)), pl.BlockSpec((1, POOL_GD, POOL_GD), lambda g: (g, 0, 0)),
                   pl.BlockSpec((1, LANE), lambda g: (0, g))],
        out_shape=[jax.ShapeDtypeStruct(dz.shape, dz.dtype), jax.ShapeDtypeStruct((POOL_G, POOL_GD, POOL_GD), F32),
                   jax.ShapeDtypeStruct((1, POOL_C), F32)],
        scratch_shapes=[pltpu.VMEM((s_len + POOL_PAD, LANE), F32), pltpu.VMEM((s_len + POOL_PAD, LANE), F32),
                        pltpu.VMEM((s_len, LANE), F32), pltpu.VMEM((8, LANE), F32)],
        input_output_aliases={4: 0},
        compiler_params=_cparams(("arbitrary",)),
        name="pool_bwd",
    )(z, dm, pool_w, pool_scale, dz)


def _row(v):
    return v.reshape(1, -1)


def _rms_body(x_ref, g_ref, o_ref):
    o_ref[...] = _rms(x_ref[...], g_ref[...]).astype(o_ref.dtype)


def _post_body(y_ref, x_ref, g_ref, o_ref):
    o_ref[...] = x_ref[...] + _rms(y_ref[...], g_ref[...])


def _post_bwd_body(y_ref, dh_ref, g_ref, dy_ref, dg_ref):
    _, vjp = jax.vjp(_rms, y_ref[...], g_ref[...])
    dy, dg = vjp(dh_ref[...])
    dy_ref[...] = dy.astype(dy_ref.dtype)
    _acc(dg_ref, dg)


def _pre_bwd_body(x_ref, dhn_ref, dres_ref, g_ref, dx_ref, dg_ref):
    _, vjp = jax.vjp(_rms, x_ref[...], g_ref[...])
    dx, dg = vjp(dhn_ref[...])
    dx_ref[...] = dres_ref[...] + dx
    _acc(dg_ref, dg)


def mixer_fwd(x, tabs, w, tag, carry=None):
    s_len = x.shape[0]
    cc, sa, sb = tabs
    sv = {"x": x}

    (h,) = rowwise("mix_norm_pre" + tag, _rms_body, s_len, [_whole(x)], [_row(w["mix_norm_pre"])], [(D_MODEL, BF16)])
    z = mm(h, w["w_in"], name="in_proj" + tag)

    def prep_body(z_ref, cc_ref, sa_ref, sb_ref, qg_ref, kg_ref, qn_ref, ckv_ref, kr_ref):
        qn_ref[...] = _rms(z_ref[:, 0:Q_RANK], qg_ref[...]).astype(qn_ref.dtype)
        ckv_ref[...] = _rms(z_ref[:, Q_RANK:Q_RANK + KV_RANK], kg_ref[...]).astype(ckv_ref.dtype)
        kr_ref[...] = _rope(z_ref[:, Q_RANK + KV_RANK:ZA_W], cc_ref[...], sa_ref[...], sb_ref[...])

    qn, ckvn, kr = rowwise("attn_prep" + tag, prep_body, s_len, [(z, ZA_W, ZA // ZA_W), _whole(cc), _whole(sa), _whole(sb)],
                           [_row(w["q_norm"]), _row(w["kv_norm"])], [(Q_RANK, BF16), (KV_RANK, BF16), (HP, F32)])
    q_raw = mm(qn, w["w_uq"], name="q_proj" + tag)
    kv_raw = mm(ckvn, w["w_ukv"], name="kv_proj" + tag)

    def qkv_body(q_ref, kv_ref, kr_ref, cc_ref, sa_ref, sb_ref, qo_ref, ko_ref, vo_ref):
        c_, a_, b_, kro = cc_ref[...], sa_ref[...], sb_ref[...], kr_ref[...]
        for hh in range(N_HEADS):
            sl = slice(hh * HP, (hh + 1) * HP)
            qo_ref[:, sl] = _rope(q_ref[:, sl], c_, a_, b_).astype(qo_ref.dtype)
            ko_ref[:, sl] = (kv_ref[:, sl] + kro).astype(ko_ref.dtype)
        lane = lax.broadcasted_iota(jnp.int32, (q_ref.shape[0], HW), 1)
        vo_ref[...] = jnp.where((lane & (HP - 1)) == VDIM, 1.0, kv_ref[:, HW:2 * HW]).astype(vo_ref.dtype)

    q, k, v = rowwise("qkv_rope" + tag, qkv_body, s_len, [_whole(q_raw), _whole(kv_raw), _whole(kr), _whole(cc), _whole(sa), _whole(sb)], [],
                      [(HW, BF16)] * 3)
    o, lse, carried = attention_fwd(q, k, v, "attention_fwd" + tag, carry)
    y_attn = mm(o, w["w_attn_o"], out_dtype=BF16, name="attn_out" + tag)

    c = conv_fwd(z, w["conv_w"], _row(w["conv_b"]))

    def ln_body(c_ref, g_ref, b_ref, o_ref):
        o_ref[...] = _silu(_layer_norm(c_ref[...], g_ref[...], b_ref[...])).astype(o_ref.dtype)

    (cs,) = rowwise("conv_ln_silu" + tag, ln_body, s_len, [_whole(c)], [_row(w["conv_ln_g"]), _row(w["conv_ln_b"])], [(CONV_C, BF16)])
    y_conv = mm(cs, w["w_conv_o"], out_dtype=BF16, name="conv_out" + tag)

    m = pool_fwd(z, w["pool_w"], _row(w["pool_scale"]))
    y_pool = mm(m, w["w_pool_o"], out_dtype=BF16, name="pool_out" + tag)

    def merge_body(ya_ref, yc_ref, yp_ref, gl_ref, o_ref):
        gl = gl_ref[...]
        o_ref[...] = (jax.nn.sigmoid(gl[:, 0:D_MODEL]) * ya_ref[...].astype(F32) + jax.nn.sigmoid(gl[:, D_MODEL:2 * D_MODEL]) * yc_ref[...].astype(F32)
                      + jax.nn.sigmoid(gl[:, 2 * D_MODEL:3 * D_MODEL]) * yp_ref[...].astype(F32)).astype(o_ref.dtype)

    (merged,) = rowwise("gate_merge" + tag, merge_body, s_len, [_whole(y_attn), _whole(y_conv), _whole(y_pool), (z, 3 * D_MODEL, 0)], [],
                        [(D_MODEL, BF16)])
    mo = mm(merged, w["w_mix_o"], name="mix_out" + tag)

    (h1,) = rowwise("mix_norm_post" + tag, _post_body, s_len, [_whole(mo), _whole(x)], [_row(w["mix_norm_post"])], [(D_MODEL, F32)])
    sv.update(h=h, z=z, qn=qn, ckvn=ckvn, q=q, k=k, v=v, o=o, lse=lse, c=c, cs=cs, m=m, y_attn=y_attn, y_conv=y_conv, y_pool=y_pool,
              merged=merged, mo=mo)
    return h1, sv, carried


def ffn_fwd(h1, w, tag):
    s_len = h1.shape[0]
    (hn,) = rowwise("ffn_norm_pre" + tag, _rms_body, s_len, [_whole(h1)], [_row(w["ffn_norm_pre"])], [(D_MODEL, BF16)])
    gu, act = ffn_in(hn, w["w_gu"], "ffn_in" + tag)
    y = mm(act, w["w_down"], name="ffn_out" + tag)
    (h2,) = rowwise("ffn_norm_post" + tag, _post_body, s_len, [_whole(y), _whole(h1)], [_row(w["ffn_norm_post"])], [(D_MODEL, F32)])
    return h2, dict(h1=h1, hn=hn, gu=gu, act=act, y=y)


def ffn_bwd(dh2, sv, w, tag):
    s_len = dh2.shape[0]
    g = {}
    d_y, g["ffn_norm_post"] = rowwise("ffn_norm_post_bwd" + tag, _post_bwd_body, s_len, [_whole(sv["y"]), _whole(dh2)],
                                      [_row(w["ffn_norm_post"])], [(D_MODEL, BF16)], [((1, D_MODEL), F32)])
    g["w_down"] = mm(sv["act"], d_y, ta=True, name="ffn_out_dw" + tag)
    d_gu = ffn_out_dx(d_y, w["w_down"], sv["gu"], "ffn_out_dx" + tag)
    g["w_gu"] = mm(sv["hn"], d_gu, ta=True, name="ffn_in_dw" + tag)
    d_hn = mm(d_gu, w["w_gu"], tb=True, name="ffn_in_dx" + tag)
    dh1, g["ffn_norm_pre"] = rowwise(
        "ffn_norm_pre_bwd" + tag, _pre_bwd_body, s_len, [_whole(sv["h1"]), _whole(d_hn), _whole(dh2)], [_row(w["ffn_norm_pre"])], [(D_MODEL, F32)], [((1, D_MODEL), F32)])
    return dh1, g


def mixer_bwd(dh1, sv, tabs, w, tag, carry=None):
    s_len = dh1.shape[0]
    cc, sa, sb = tabs
    g = {}

    d_mo, g["mix_norm_post"] = rowwise(
        "mix_norm_post_bwd" + tag, _post_bwd_body, s_len, [_whole(sv["mo"]), _whole(dh1)], [_row(w["mix_norm_post"])], [(D_MODEL, BF16)], [((1, D_MODEL), F32)])
    g["w_mix_o"] = mm(sv["merged"], d_mo, ta=True, name="mix_out_dw" + tag)
    d_merged = mm(d_mo, w["w_mix_o"], tb=True, name="mix_out_dx" + tag)

    def merge_bwd_body(dm_ref, ya_ref, yc_ref, yp_ref, gl_ref, dya_ref, dyc_ref, dyp_ref, dgl_ref):
        dmg = dm_ref[...]
        for i, (y_ref, dy_ref) in enumerate(((ya_ref, dya_ref), (yc_ref, dyc_ref), (yp_ref, dyp_ref))):
            sg = jax.nn.sigmoid(gl_ref[:, i * D_MODEL:(i + 1) * D_MODEL])
            dy_ref[...] = (dmg * sg).astype(dy_ref.dtype)
            dgl_ref[:, i * D_MODEL:(i + 1) * D_MODEL] = (dmg * y_ref[...].astype(F32) * sg * (1.0 - sg)).astype(dgl_ref.dtype)

    d_ya, d_yc, d_yp, dz = rowwise(
        "gate_merge_bwd" + tag, merge_bwd_body, s_len,
        [_whole(d_merged), _whole(sv["y_attn"]), _whole(sv["y_conv"]), _whole(sv["y_pool"]), (sv["z"], 3 * D_MODEL, 0)], [],
        [(D_MODEL, BF16)] * 3 + [(3 * D_MODEL, BF16, ZW, 0)], tile=WIDE_ROW_TILE)

    g["w_pool_o"] = mm(sv["m"], d_yp, ta=True, name="pool_out_dw" + tag)
    d_m = mm(d_yp, w["w_pool_o"], tb=True, name="pool_out_dx" + tag)
    dz, g["pool_w"], g["pool_scale"] = pool_bwd(sv["z"], d_m, w["pool_w"], _row(w["pool_scale"]), dz)

    g["w_conv_o"] = mm(sv["cs"], d_yc, ta=True, name="conv_out_dw" + tag)
    d_cs = mm(d_yc, w["w_conv_o"], tb=True, name="conv_out_dx" + tag)

    def ln_bwd_body(c_ref, dcs_ref, g_ref, b_ref, dc_ref, dg_ref, db_ref):
        f = lambda c_, g_, b_: _silu(_layer_norm(c_, g_, b_))
        _, vjp = jax.vjp(f, c_ref[...], g_ref[...], b_ref[...])
        dc, dg, db = vjp(dcs_ref[...])
        dc_ref[...] = dc
        _acc(dg_ref, dg)
        _acc(db_ref, db)

    d_c, g["conv_ln_g"], g["conv_ln_b"] = rowwise("conv_ln_silu_bwd" + tag, ln_bwd_body, s_len, [_whole(sv["c"]), _whole(d_cs)],
                                                  [_row(w["conv_ln_g"]), _row(w["conv_ln_b"])], [(CONV_C, F32)],
                                                  [((1, CONV_C), F32), ((1, CONV_C), F32)])
    dz, g["conv_w"], g["conv_b"] = conv_bwd(sv["z"], d_c, w["conv_w"], dz)

    g["w_attn_o"] = mm(sv["o"], d_ya, ta=True, name="attn_out_dw" + tag)
    d_o = mm(d_ya, w["w_attn_o"], tb=True, out_dtype=BF16, name="attn_out_dx" + tag)
    delta = attention_delta(d_o, sv["o"])
    t_bwd = min(ATT_TILE_BWD, s_len)
    rows_of = lambda a: a.reshape(N_HEADS, s_len // t_bwd, 1, t_bwd)
    dq, dk, dv, carried = attention_bwd(sv["q"], sv["k"], sv["v"], d_o, rows_of(sv["lse"]), rows_of(delta), "attention_bwd" + tag, carry)

    def qkv_bwd_body(dq_ref, dk_ref, dv_ref, cc_ref, sa_ref, sb_ref, dqp_ref, dkv_ref, dkr_ref):
        c_, a_, b_ = cc_ref[...], sa_ref[...], sb_ref[...]
        dk_sum = jnp.zeros((dq_ref.shape[0], HP), F32)
        for hh in range(N_HEADS):
            sl = slice(hh * HP, (hh + 1) * HP)
            dqp_ref[:, sl] = _rope_t(dq_ref[:, sl], c_, a_, b_).astype(dqp_ref.dtype)
            dkh = dk_ref[:, sl]
            dkv_ref[:, sl] = dkh.astype(dkv_ref.dtype)
            dk_sum = dk_sum + dkh
        dkv_ref[:, HW:2 * HW] = dv_ref[...]
        dkr_ref[...] = _rope_t(dk_sum, c_, a_, b_)

    dq_pre, dkv_pre, d_kr = rowwise("qkv_rope_bwd" + tag, qkv_bwd_body, s_len,
                                    [_whole(dq), _whole(dk), _whole(dv), _whole(cc), _whole(sa), _whole(sb)], [],
                                    [(HW, BF16), (2 * HW, BF16), (HP, F32)])
    g["w_uq"] = mm(sv["qn"], dq_pre, ta=True, name="q_proj_dw" + tag)
    d_qn = mm(dq_pre, w["w_uq"], tb=True, name="q_proj_dx" + tag)
    g["w_ukv"] = mm(sv["ckvn"], dkv_pre, ta=True, name="kv_proj_dw" + tag)
    d_ckvn = mm(dkv_pre, w["w_ukv"], tb=True, name="kv_proj_dx" + tag)

    def prep_bwd_body(z_ref, dqn_ref, dckv_ref, dkr_ref, qg_ref, kg_ref, dz_ref, dqg_ref, dkg_ref):
        _, vq = jax.vjp(_rms, z_ref[:, 0:Q_RANK], qg_ref[...])
        dcq, dqg = vq(dqn_ref[...])
        _, vk = jax.vjp(_rms, z_ref[:, Q_RANK:Q_RANK + KV_RANK], kg_ref[...])
        dckv, dkg = vk(dckv_ref[...])
        dz_ref[:, 0:Q_RANK] = dcq.astype(dz_ref.dtype)
        dz_ref[:, Q_RANK:Q_RANK + KV_RANK] = dckv.astype(dz_ref.dtype)
        dz_ref[:, Q_RANK + KV_RANK:ZA_W] = dkr_ref[...].astype(dz_ref.dtype)
        _acc(dqg_ref, dqg)
        _acc(dkg_ref, dkg)

    dz, g["q_norm"], g["kv_norm"] = rowwise("attn_prep_bwd" + tag, prep_bwd_body, s_len,
                                            [(sv["z"], ZA_W, ZA // ZA_W), _whole(d_qn), _whole(d_ckvn), _whole(d_kr)],
                                            [_row(w["q_norm"]), _row(w["kv_norm"])], [(ZA_W, BF16, ZW, ZA // ZA_W)],
                                            [((1, Q_RANK), F32), ((1, KV_RANK), F32)], into={0: dz})

    g["w_in"] = mm(sv["h"], dz, ta=True, name="in_proj_dw" + tag)
    d_h = mm(dz, w["w_in"], tb=True, name="in_proj_dx" + tag)
    dx, g["mix_norm_pre"] = rowwise(
        "mix_norm_pre_bwd" + tag, _pre_bwd_body, s_len, [_whole(sv["x"]), _whole(d_h), _whole(dh1)], [_row(w["mix_norm_pre"])], [(D_MODEL, F32)], [((1, D_MODEL), F32)])
    return dx, g, carried


def loss_head(h, target):
    s_len = h.shape[0]

    def body(h_ref, t_ref, dy_ref, loss_ref):
        err = h_ref[...] - t_ref[...]
        dy_ref[...] = err * (1.0 / D_MODEL)
        part = 0.5 * jnp.sum(jnp.mean(err * err, axis=-1, keepdims=True), axis=0, keepdims=True)
        _acc(loss_ref, jnp.broadcast_to(part, (1, LANE)))

    return rowwise("loss_head", body, s_len, [_whole(h), _whole(target)], [], [(D_MODEL, F32)], [((1, LANE), F32)])


def local_step(x, pos_col, target, layers):
    s_len = x.shape[0]
    tabs = rope_tables(pos_col, s_len)
    h, saved = x, []
    for li, w in enumerate(layers):
        h, sv_mix, _ = mixer_fwd(h, tabs, w, f"_l{li}")
        h, sv_ffn = ffn_fwd(h, w, f"_l{li}")
        saved.append((sv_mix, sv_ffn))
    dh, loss = loss_head(h, target)
    grads = [None] * len(layers)
    for li in reversed(range(len(layers))):
        dh, g_ffn = ffn_bwd(dh, saved[li][1], layers[li], f"_l{li}")
        dh, g_mix, _ = mixer_bwd(dh, saved[li][0], tabs, layers[li], f"_l{li}")
        grads[li] = {**g_mix, **g_ffn}
    return loss[0, 0], dh, grads


def _pad_heads_cols(wm, per_head):
    r = wm.shape[0]
    return jnp.pad(wm.reshape(r, N_HEADS, per_head), ((0, 0), (0, 0), (0, HP - per_head))).reshape(r, HW)


def _unpad_heads_cols(wm, per_head):
    r = wm.shape[0]
    return wm.reshape(r, N_HEADS, HP)[:, :, :per_head].reshape(r, N_HEADS * per_head)


def align_weights(p):
    out = dict(p)
    if "w_in" in p:
        w_in = p["w_in"]
        r = w_in.shape[0]
        zeros = lambda n: jnp.zeros((r, n), w_in.dtype)
        conv = w_in[:, O_CONV:O_POOL].reshape(r, 2, CONV_C // LANE, LANE).transpose(0, 2, 1, 3).reshape(r, 2 * CONV_C)
        out["w_in"] = jnp.concatenate([
            w_in[:, O_GATE:D_IN], conv, w_in[:, O_POOL:O_GATE], w_in[:, O_Q:O_KR],
            zeros(KR_LANE), w_in[:, O_KR:O_CONV], zeros(HP - KR_LANE - ROPE)], axis=1)
        out["w_uq"] = _pad_heads_cols(p["w_uq"], NOPE + ROPE)
        out["w_ukv"] = jnp.concatenate([_pad_heads_cols(p["w_uk"], NOPE), _pad_heads_cols(p["w_uv"], VDIM)], axis=1)
        wo = p["w_attn_o"]
        out["w_attn_o"] = jnp.pad(wo.reshape(N_HEADS, VDIM, D_MODEL), ((0, 0), (0, HP - VDIM), (0, 0))).reshape(HW, D_MODEL)
        del out["w_uk"], out["w_uv"]
    if "w_gate" in p:
        out["w_gu"] = jnp.concatenate([p["w_gate"][:, :FF_HALF], p["w_up"][:, :FF_HALF], p["w_gate"][:, FF_HALF:], p["w_up"][:, FF_HALF:]], axis=1)
        del out["w_gate"], out["w_up"]
    return out


def unalign_grads(g):
    out = dict(g)
    if "w_in" in g:
        gi = g["w_in"]
        kr0 = ZA + Q_RANK + KV_RANK + KR_LANE
        r = gi.shape[0]
        conv = gi[:, ZC:ZP].reshape(r, CONV_C // LANE, 2, LANE).transpose(0, 2, 1, 3).reshape(r, 2 * CONV_C)
        out["w_in"] = jnp.concatenate([gi[:, ZA:ZA + Q_RANK + KV_RANK], gi[:, kr0:kr0 + ROPE], conv, gi[:, ZP:ZA], gi[:, ZG:ZC]], axis=1)
        out["w_uq"] = _unpad_heads_cols(g["w_uq"], NOPE + ROPE)
        out["w_uk"] = _unpad_heads_cols(g["w_ukv"][:, :HW], NOPE)
        out["w_uv"] = _unpad_heads_cols(g["w_ukv"][:, HW:], VDIM)
        out["w_attn_o"] = g["w_attn_o"].reshape(N_HEADS, HP, D_MODEL)[:, :VDIM].reshape(N_HEADS * VDIM, D_MODEL)
        del out["w_ukv"]
    if "w_gu" in g:
        gu = g["w_gu"]
        out["w_gate"] = jnp.concatenate([gu[:, 0:FF_HALF], gu[:, 2 * FF_HALF:3 * FF_HALF]], axis=1)
        out["w_up"] = jnp.concatenate([gu[:, FF_HALF:2 * FF_HALF], gu[:, 3 * FF_HALF:]], axis=1)
        del out["w_gu"]
    return out


MESH = pl.DeviceIdType.MESH
ANY = pl.BlockSpec(memory_space=pl.ANY)


def _place():
    return lax.axis_index("x"), lax.axis_index("y"), lax.axis_index("c")


def _other_chips(x, y):
    return [(1 - x, y), (x, 1 - y), (1 - x, 1 - y)]


def _half_rows(rows, c):
    assert rows % (2 * HALF_ALIGN) == 0, rows
    return pl.ds(pl.multiple_of(c * (rows // 2), HALF_ALIGN), rows // 2)


class GatherShards:
    def __init__(self, local):
        self.ins = list(local)
        self.outs = [jax.ShapeDtypeStruct((N_CHIPS, *a.shape), a.dtype) for a in local]
        self.n_sems = 6 * len(local)
        self.base = 0

    def _first(self, in_refs, out_refs, send_sems, recv_sems):
        x, y, c = _place()
        me = 2 * x + y
        chips = _other_chips(x, y)

        def copy(i, k, slot, core, to, src=None):
            dst = out_refs[i].at[slot, _half_rows(out_refs[i].shape[1], core)]
            return pltpu.make_async_remote_copy(src_ref=dst if src is None else src, dst_ref=dst, send_sem=send_sems.at[self.base + 6 * i + k],
                                                recv_sem=recv_sems.at[self.base + 6 * i + k], device_id=to, device_id_type=MESH)

        first = [copy(i, j, me, c, (*chip, c), src=in_refs[i].at[_half_rows(in_refs[i].shape[0], c)])
                 for i in range(len(in_refs)) for j, chip in enumerate(chips)]
        return first, copy

    def start(self, in_refs, out_refs, send_sems, recv_sems):
        first, _ = self._first(in_refs, out_refs, send_sems, recv_sems)
        for cp in first:
            cp.start()

    def finish(self, in_refs, out_refs, send_sems, recv_sems):
        first, copy = self._first(in_refs, out_refs, send_sems, recv_sems)
        x, y, c = _place()
        slots = [2 * cx + cy for cx, cy in _other_chips(x, y)]
        sibling = (x, y, 1 - c)
        passed = []
        for i in range(len(in_refs)):
            for j in range(3):
                copy(i, j, slots[j], c, sibling).wait_recv()
                fwd = copy(i, 3 + j, slots[j], c, sibling)
                fwd.start()
                passed.append(fwd)
        for i in range(len(in_refs)):
            for j in range(3):
                copy(i, 3 + j, slots[j], 1 - c, sibling).wait_recv()
        for cp in first + passed:
            cp.wait_send()


class ChipExchange:
    def __init__(self, parts):
        self.ins = list(parts)
        self.outs = [jax.ShapeDtypeStruct((3, *a.shape[1:]), a.dtype) for a in parts]
        self.n_sems = 3 * len(parts)
        self.base = 0

    def _copies(self, in_refs, out_refs, send_sems, recv_sems):
        x, y, c = _place()
        return [pltpu.make_async_remote_copy(src_ref=in_refs[i].at[2 * chip[0] + chip[1]], dst_ref=out_refs[i].at[j],
                                             send_sem=send_sems.at[self.base + 3 * i + j], recv_sem=recv_sems.at[self.base + 3 * i + j],
                                             device_id=(*chip, c), device_id_type=MESH)
                for i in range(len(in_refs)) for j, chip in enumerate(_other_chips(x, y))]

    def start(self, in_refs, out_refs, send_sems, recv_sems):
        for cp in self._copies(in_refs, out_refs, send_sems, recv_sems):
            cp.start()

    def finish(self, in_refs, out_refs, send_sems, recv_sems):
        copies = self._copies(in_refs, out_refs, send_sems, recv_sems)
        for cp in copies:
            cp.wait_recv()
        for cp in copies:
            cp.wait_send()


def run_exchange(ex, name):
    n_in, n_out = len(ex.ins), len(ex.outs)

    def body(*refs):
        ins, outs, sems = refs[:n_in], refs[n_in:n_in + n_out], refs[n_in + n_out:]
        ex.start(ins, outs, *sems)
        ex.finish(ins, outs, *sems)

    return pl.pallas_call(
        body,
        in_specs=[ANY] * n_in,
        out_specs=[ANY] * n_out,
        out_shape=list(ex.outs),
        scratch_shapes=[pltpu.SemaphoreType.DMA((ex.n_sems,)), pltpu.SemaphoreType.DMA((ex.n_sems,))],
        name=name,
    )(*ex.ins)


def sibling_swap(gs, name):
    n = len(gs)

    def body(*refs):
        g_refs, out_refs, (send_sems, recv_sems) = refs[:n], refs[n:2 * n], refs[2 * n:]
        x, y, c = _place()
        copies = []
        for i in range(n):
            for j in range(N_CHIPS):
                cp = pltpu.make_async_remote_copy(src_ref=g_refs[i].at[j, _half_rows(g_refs[i].shape[1], 1 - c)], dst_ref=out_refs[i].at[j],
                                                  send_sem=send_sems.at[4 * i + j], recv_sem=recv_sems.at[4 * i + j],
                                                  device_id=(x, y, 1 - c), device_id_type=MESH)
                cp.start()
                copies.append(cp)
        for cp in copies:
            cp.wait_recv()
        for cp in copies:
            cp.wait_send()

    return pl.pallas_call(
        body,
        in_specs=[ANY] * n,
        out_specs=[ANY] * n,
        out_shape=[jax.ShapeDtypeStruct((N_CHIPS, a.shape[1] // 2, a.shape[2]), a.dtype) for a in gs],
        scratch_shapes=[pltpu.SemaphoreType.DMA((4 * n,)), pltpu.SemaphoreType.DMA((4 * n,))],
        name=name,
    )(*gs)


def sibling_gather(fs):
    n = len(fs)
    layers = fs[0].shape[0]

    def body(*refs):
        out_refs, (send_sems, recv_sems) = refs[n:2 * n], refs[2 * n:]
        x, y, c = _place()

        def copy(i, l, core):
            part = out_refs[i].at[l, _half_rows(out_refs[i].shape[1], core)]
            return pltpu.make_async_remote_copy(src_ref=part, dst_ref=part, send_sem=send_sems.at[layers * i + l],
                                                recv_sem=recv_sems.at[layers * i + l], device_id=(x, y, 1 - c), device_id_type=MESH)

        sends = [copy(i, l, c) for i in range(n) for l in range(layers)]
        for cp in sends:
            cp.start()
        for i in range(n):
            for l in range(layers):
                copy(i, l, 1 - c).wait_recv()
        for cp in sends:
            cp.wait_send()

    return pl.pallas_call(
        body,
        in_specs=[ANY] * n,
        out_specs=[ANY] * n,
        out_shape=[jax.ShapeDtypeStruct(a.shape, a.dtype) for a in fs],
        scratch_shapes=[pltpu.SemaphoreType.DMA((layers * n,)), pltpu.SemaphoreType.DMA((layers * n,))],
        input_output_aliases={i: i for i in range(n)},
        name="sibling_gather",
    )(*fs)


class GatherAll:
    def __init__(self, vs):
        self.ins = list(vs)
        self.outs = [jax.ShapeDtypeStruct((8, *a.shape), a.dtype) for a in vs]
        self.n_sems = 7 * len(vs)
        self.base = 0

    def _first(self, in_refs, out_refs, send_sems, recv_sems):
        x, y, c = _place()
        me, sibling = (x, y, c), (x, y, 1 - c)

        def copy(i, k, block, to, src=None):
            px, py, pc = block
            dst = out_refs[i].at[4 * px + 2 * py + pc]
            return pltpu.make_async_remote_copy(src_ref=dst if src is None else src, dst_ref=dst, send_sem=send_sems.at[self.base + 7 * i + k],
                                                recv_sem=recv_sems.at[self.base + 7 * i + k], device_id=to, device_id_type=MESH)

        first = []
        for i in range(len(in_refs)):
            first.append(copy(i, 0, me, sibling, src=in_refs[i]))
            first += [copy(i, 1 + j, me, (*chip, c), src=in_refs[i]) for j, chip in enumerate(_other_chips(x, y))]
        return first, copy

    def start(self, in_refs, out_refs, send_sems, recv_sems):
        first, _ = self._first(in_refs, out_refs, send_sems, recv_sems)
        for cp in first:
            cp.start()

    def finish(self, in_refs, out_refs, send_sems, recv_sems):
        first, copy = self._first(in_refs, out_refs, send_sems, recv_sems)
        x, y, c = _place()
        me, sibling = (x, y, c), (x, y, 1 - c)
        chips = _other_chips(x, y)
        passed = []
        for i in range(len(in_refs)):
            for j, chip in enumerate(chips):
                copy(i, 1 + j, (*chip, c), me).wait_recv()
                fwd = copy(i, 4 + j, (*chip, c), sibling)
                fwd.start()
                passed.append(fwd)
        for i in range(len(in_refs)):
            copy(i, 0, sibling, me).wait_recv()
            for j, chip in enumerate(chips):
                copy(i, 4 + j, (*chip, 1 - c), me).wait_recv()
        for cp in first + passed:
            cp.wait_send()


class Both:
    def __init__(self, a, b):
        self.a, self.b = a, b
        b.base = a.base + a.n_sems
        self.ins, self.outs, self.n_sems, self.base = a.ins + b.ins, a.outs + b.outs, a.n_sems + b.n_sems, a.base

    def _split(self, in_refs, out_refs):
        na, ma = len(self.a.ins), len(self.a.outs)
        return (in_refs[:na], out_refs[:ma]), (in_refs[na:], out_refs[ma:])

    def start(self, in_refs, out_refs, send_sems, recv_sems):
        for ex, (i, o) in zip((self.a, self.b), self._split(in_refs, out_refs)):
            ex.start(i, o, send_sems, recv_sems)

    def finish(self, in_refs, out_refs, send_sems, recv_sems):
        for ex, (i, o) in zip((self.a, self.b), self._split(in_refs, out_refs)):
            ex.finish(i, o, send_sems, recv_sems)


def _row_tile(rows, row_bytes):
    best = None
    for t in range(16, rows + 1, 16):
        if rows % t == 0 and t * row_bytes <= SUM_TILE_BYTES:
            best = t
    return best or rows


def sibling_sum(g, theirs, place, name):
    _, half, cols = theirs.shape
    tile = _row_tile(half, cols * 4)
    nt = half // tile

    def body(place_ref, g_ref, t_ref, o_ref):
        o_ref[...] = (g_ref[...].astype(F32) + t_ref[...].astype(F32)).astype(o_ref.dtype)

    spec = pl.BlockSpec((1, tile, cols), lambda j, i, place_ref: (j, i, 0))
    return pl.pallas_call(
        body,
        grid_spec=pltpu.PrefetchScalarGridSpec(
            num_scalar_prefetch=1, grid=(N_CHIPS, nt),
            in_specs=[pl.BlockSpec((1, tile, cols), lambda j, i, place_ref: (j, place_ref[1] * nt + i, 0)), spec], out_specs=spec),
        out_shape=jax.ShapeDtypeStruct(theirs.shape, BF16),
        compiler_params=_cparams(("parallel", "parallel")),
        name=name,
    )(place, g, theirs)


def chip_sum(p, others, place, layer, into, name):
    _, half, cols = p.shape
    tile = _row_tile(half, cols * 4)
    nt = half // tile

    def body(place_ref, p_ref, o3_ref, *rest):
        o_ref = rest[-1]
        acc = p_ref[0].astype(F32)
        for k in range(3):
            acc = acc + o3_ref[k].astype(F32)
        o_ref[0] = acc

    return pl.pallas_call(
        body,
        grid_spec=pltpu.PrefetchScalarGridSpec(
            num_scalar_prefetch=1, grid=(nt,),
            in_specs=[pl.BlockSpec((1, tile, cols), lambda i, place_ref: (place_ref[0], i, 0)),
                      pl.BlockSpec((3, tile, cols), lambda i, place_ref: (0, i, 0))] + ([] if into is None else [ANY]),
            out_specs=pl.BlockSpec((1, tile, cols), lambda i, place_ref: (layer, place_ref[1] * nt + i, 0))),
        out_shape=jax.ShapeDtypeStruct((N_LAYERS, 2 * half, cols), F32),
        input_output_aliases={} if into is None else {3: 0},
        compiler_params=_cparams(("parallel",)),
        name=name,
    )(place, p, others, *([] if into is None else [into]))


def sum_devices(a, name):
    n, rows, cols = a.shape
    tile = _row_tile(rows, cols * 4 * n)

    def body(a_ref, o_ref):
        acc = a_ref[0]
        for s in range(1, n):
            acc = acc + a_ref[s]
        o_ref[...] = acc

    return pl.pallas_call(body, grid=(rows // tile,), in_specs=[pl.BlockSpec((n, tile, cols), lambda i: (0, i, 0))],
                          out_specs=pl.BlockSpec((tile, cols), lambda i: (i, 0)), out_shape=jax.ShapeDtypeStruct((rows, cols), F32),
                          compiler_params=_cparams(("parallel",)), name=name)(a)


def adamw(w, g, m, v, name):
    layers, rows, cols = w.shape
    tile = rows
    for t in range(8, rows, 8):
        if rows % t == 0 and t * cols * 4 <= 2 * SUM_TILE_BYTES:
            tile = t

    def body(w_ref, g_ref, m_ref, v_ref, d_ref, mo_ref, vo_ref):
        gg = g_ref[...]
        m_new = ADAM_B1 * m_ref[...] + (1.0 - ADAM_B1) * gg
        v_new = ADAM_B2 * v_ref[...] + (1.0 - ADAM_B2) * (gg * gg)
        m_hat = m_new / (1.0 - ADAM_B1 ** ADAM_STEP)
        v_hat = v_new / (1.0 - ADAM_B2 ** ADAM_STEP)
        d_ref[...] = -ADAM_LR * (m_hat / (jnp.sqrt(v_hat) + ADAM_EPS) + ADAM_WD * w_ref[...])
        mo_ref[...] = m_new
        vo_ref[...] = v_new

    spec = pl.BlockSpec((1, tile, cols), lambda l, i: (l, i, 0))
    shape = jax.ShapeDtypeStruct((layers, rows, cols), F32)
    return pl.pallas_call(body, grid=(layers, rows // tile), in_specs=[spec] * 4, out_specs=[spec] * 3, out_shape=[shape] * 3,
                          compiler_params=_cparams(("parallel", "parallel")), name=name)(w, g, m, v)


WEIGHTS = ["mix_norm_pre", "w_in", "q_norm", "w_uq", "kv_norm", "w_uk", "w_uv", "w_attn_o", "conv_w", "conv_b", "conv_ln_g", "conv_ln_b",
           "w_conv_o", "pool_w", "pool_scale", "w_pool_o", "w_mix_o", "mix_norm_post", "ffn_norm_pre", "w_gate", "w_up", "w_down",
           "ffn_norm_post"]
SHARDED = {"w_in": 2, "w_uq": 2, "w_uk": 2, "w_uv": 2, "w_attn_o": 2, "conv_w": 2, "w_conv_o": 2, "w_pool_o": 2, "w_mix_o": 1,
           "w_gate": 2, "w_up": 2, "w_down": 1}
REPLICATED = [n for n in WEIGHTS if n not in SHARDED]
ROW_PARAMS = [n for n in REPLICATED if n != "pool_w"]
ROWS_MINOR = ("w_in", "w_uq", "w_gate", "w_up", "conv_w")
N_CHIPS = 4
N_MIX_GROUPS = 6
CONV_WIRE_ROWS = 32
GROUPS = [(("w_in",), 1), (("w_uq",), 1), (("w_uk", "w_uv"), 1), (("w_attn_o", "w_conv_o", "w_pool_o"), 1), (("conv_w",), 1),
          (("w_mix_o",), 1), (("w_gate", "w_up"), 2), (("w_down",), 1)]


def _join(parts, axis):
    return parts[0] if len(parts) == 1 else jnp.concatenate(parts, axis=axis)


def _split_group(arr, names, axis, shapes):
    out, off = {}, 0
    ax = arr.ndim - 3 + axis
    for n in names:
        size = shapes[n][axis]
        out[n] = lax.slice_in_dim(arr, off, off + size, axis=ax)
        off += size
    return out


def _pack_rows(vectors):
    blocks = []
    for v in vectors:
        for li in range(v.shape[0]):
            blocks.append(jnp.pad(v[li][None, :], ((0, PACK_ROWS - 1), (0, PACK_W - v.shape[1]))))
    return jnp.concatenate(blocks, axis=0)


def _unpack_rows(packed, shapes):
    out, r = [], 0
    for layers, width in shapes:
        out.append(jnp.stack([packed[r + PACK_ROWS * li, :width] for li in range(layers)]))
        r += PACK_ROWS * layers
    return out


def kernel(x, positions, mix_norm_pre, w_in, q_norm, w_uq, kv_norm, w_uk, w_uv, w_attn_o, conv_w, conv_b, conv_ln_g, conv_ln_b, w_conv_o, pool_w, pool_scale, w_pool_o, w_mix_o, mix_norm_post, ffn_norm_pre, w_gate, w_up, w_down, ffn_norm_post, loss_target, m_mix_norm_pre, m_w_in, m_q_norm, m_w_uq, m_kv_norm, m_w_uk, m_w_uv, m_w_attn_o, m_conv_w, m_conv_b, m_conv_ln_g, m_conv_ln_b, m_w_conv_o, m_pool_w, m_pool_scale, m_w_pool_o, m_w_mix_o, m_mix_norm_post, m_ffn_norm_pre, m_w_gate, m_w_up, m_w_down, m_ffn_norm_post, v_mix_norm_pre, v_w_in, v_q_norm, v_w_uq, v_kv_norm, v_w_uk, v_w_uv, v_w_attn_o, v_conv_w, v_conv_b, v_conv_ln_g, v_conv_ln_b, v_w_conv_o, v_pool_w, v_pool_scale, v_w_pool_o, v_w_mix_o, v_mix_norm_post, v_ffn_norm_pre, v_w_gate, v_w_up, v_w_down, v_ffn_norm_post):
    given = dict(mix_norm_pre=mix_norm_pre, w_in=w_in, q_norm=q_norm, w_uq=w_uq, kv_norm=kv_norm, w_uk=w_uk, w_uv=w_uv, w_attn_o=w_attn_o,
                 conv_w=conv_w, conv_b=conv_b, conv_ln_g=conv_ln_g, conv_ln_b=conv_ln_b, w_conv_o=w_conv_o, pool_w=pool_w,
                 pool_scale=pool_scale, w_pool_o=w_pool_o, w_mix_o=w_mix_o, mix_norm_post=mix_norm_post, ffn_norm_pre=ffn_norm_pre,
                 w_gate=w_gate, w_up=w_up, w_down=w_down, ffn_norm_post=ffn_norm_post)
    mom = dict(mix_norm_pre=m_mix_norm_pre, w_in=m_w_in, q_norm=m_q_norm, w_uq=m_w_uq, kv_norm=m_kv_norm, w_uk=m_w_uk, w_uv=m_w_uv,
               w_attn_o=m_w_attn_o, conv_w=m_conv_w, conv_b=m_conv_b, conv_ln_g=m_conv_ln_g, conv_ln_b=m_conv_ln_b, w_conv_o=m_w_conv_o,
               pool_w=m_pool_w, pool_scale=m_pool_scale, w_pool_o=m_w_pool_o, w_mix_o=m_w_mix_o, mix_norm_post=m_mix_norm_post,
               ffn_norm_pre=m_ffn_norm_pre, w_gate=m_w_gate, w_up=m_w_up, w_down=m_w_down, ffn_norm_post=m_ffn_norm_post)
    var = dict(mix_norm_pre=v_mix_norm_pre, w_in=v_w_in, q_norm=v_q_norm, w_uq=v_w_uq, kv_norm=v_kv_norm, w_uk=v_w_uk, w_uv=v_w_uv,
               w_attn_o=v_w_attn_o, conv_w=v_conv_w, conv_b=v_conv_b, conv_ln_g=v_conv_ln_g, conv_ln_b=v_conv_ln_b, w_conv_o=v_w_conv_o,
               pool_w=v_pool_w, pool_scale=v_pool_scale, w_pool_o=v_w_pool_o, w_mix_o=v_w_mix_o, mix_norm_post=v_mix_norm_post,
               ffn_norm_pre=v_ffn_norm_pre, w_gate=v_w_gate, w_up=v_w_up, w_down=v_w_down, ffn_norm_post=v_ffn_norm_post)
    s_len = x.shape[1]
    sharded_names = [n for n in WEIGHTS if n in SHARDED]
    chip = 2 * lax.axis_index("x") + lax.axis_index("y")
    place = jnp.stack([chip, lax.axis_index("c")]).astype(jnp.int32)
    shard_shape = {n: given[n].shape for n in sharded_names}

    mix_groups, ffn_groups = GROUPS[:N_MIX_GROUPS], GROUPS[N_MIX_GROUPS:]
    weight_wire_shape = {n: (N_LAYERS, 2 * CONV_WIRE_ROWS, shard_shape[n][2]) if n == "conv_w" else shard_shape[n] for n in sharded_names}
    grad_wire_shape = {n: (N_LAYERS, CONV_WIRE_ROWS, shard_shape[n][2]) if n == "conv_w" else shard_shape[n] for n in sharded_names}
    pad_rows = lambda a: jnp.pad(a, ((0, CONV_WIRE_ROWS - CONV_W), (0, 0)))

    def weight_wires(groups, li):
        def wire(name):
            a = given[name][li]
            if name == "conv_w":
                hi = a.astype(BF16)
                return jnp.concatenate([pad_rows(hi), pad_rows((a - hi.astype(F32)).astype(BF16))], axis=0)
            return a.astype(BF16)

        return [_join([wire(n) for n in names], axis - 1) for names, axis in groups]

    def full_weights(groups, li, local, gathered):
        p = {n: given[n][li] for n in REPLICATED}
        p["pool_w"] = p["pool_w"].astype(BF16)
        for (names, axis), loc, got in zip(groups, local, gathered):
            got = lax.dynamic_update_slice(got, loc[None], (chip, 0, 0))
            per_chip = [_split_group(got[j], names, axis, weight_wire_shape) for j in range(N_CHIPS)]
            for n in names:
                parts = [pc[n] for pc in per_chip]
                if n == "conv_w":
                    parts = [q[:CONV_W].astype(F32) + q[CONV_WIRE_ROWS:CONV_WIRE_ROWS + CONV_W].astype(F32) for q in parts]
                p[n] = jnp.concatenate(parts, axis=SHARDED[n] - 1)
        return align_weights(p)

    def chip_partials(groups, g, tag):
        wires = []
        for names, axis in groups:
            split = {n: jnp.split(pad_rows(g[n]) if n == "conv_w" else g[n], N_CHIPS, axis=SHARDED[n] - 1) for n in names}
            wires.append(jnp.stack([_join([split[n][j].astype(BF16) for n in names], axis - 1) for j in range(N_CHIPS)]))
        theirs = sibling_swap(wires, "sibling_swap_" + tag)
        return [sibling_sum(w, t, place, f"sibling_sum_{tag}_{i}") for i, (w, t) in enumerate(zip(wires, theirs))]

    tabs = rope_tables(positions.reshape(s_len, 1), s_len)
    loc_m0 = weight_wires(mix_groups, 0)
    w_m0 = full_weights(mix_groups, 0, loc_m0, run_exchange(GatherShards(loc_m0), "gather_mix_l0"))
    loc_f0, loc_m1 = weight_wires(ffn_groups, 0), weight_wires(mix_groups, 1)
    h, sv_m0, got = mixer_fwd(x[0], tabs, w_m0, "_l0", GatherShards(loc_f0 + loc_m1))
    w_f0 = full_weights(ffn_groups, 0, loc_f0, got[:len(loc_f0)])
    w_m1 = full_weights(mix_groups, 1, loc_m1, got[len(loc_f0):])
    h, sv_f0 = ffn_fwd(h, w_f0, "_l0")
    loc_f1 = weight_wires(ffn_groups, 1)
    h, sv_m1, got = mixer_fwd(h, tabs, w_m1, "_l1", GatherShards(loc_f1))
    w_f1 = full_weights(ffn_groups, 1, loc_f1, got)
    h, sv_f1 = ffn_fwd(h, w_f1, "_l1")
    dh, loss_local = loss_head(h, loss_target[0])
    loss = lax.psum(loss_local[0, 0], MESH_AXES)

    dh, g_f1 = ffn_bwd(dh, sv_f1, w_f1, "_l1")
    g_f1 = unalign_grads(g_f1)
    p_f1 = chip_partials(ffn_groups, g_f1, "ffn_l1")
    dh, g_m1, o_f1 = mixer_bwd(dh, sv_m1, tabs, w_m1, "_l1", ChipExchange(p_f1))
    g_m1 = unalign_grads(g_m1)
    p_m1 = chip_partials(mix_groups, g_m1, "mix_l1")
    dh, g_f0 = ffn_bwd(dh, sv_f0, w_f0, "_l0")
    g_f0 = unalign_grads(g_f0)
    p_f0 = chip_partials(ffn_groups, g_f0, "ffn_l0")
    grad_x, g_m0, got = mixer_bwd(dh, sv_m0, tabs, w_m0, "_l0", ChipExchange(p_m1 + p_f0))
    o_m1, o_f0 = got[:len(p_m1)], got[len(p_m1):]
    g_m0 = unalign_grads(g_m0)
    p_m0 = chip_partials(mix_groups, g_m0, "mix_l0")
    grads = [{**g_m0, **g_f0}, {**g_m1, **g_f1}]
    grad_full = {n: jnp.stack([g[n].reshape(given[n].shape[1:]) for g in grads]) for n in REPLICATED}
    replicated = [_pack_rows([grad_full[n] for n in ROW_PARAMS]), grad_full["pool_w"].reshape(-1, POOL_GD)]
    got = run_exchange(Both(ChipExchange(p_m0), GatherAll(replicated)), "chip_exchange_mix_l0")
    o_m0 = got[:len(p_m0)]
    device = 2 * chip + lax.axis_index("c")
    rows_all, pool_w_all = [lax.dynamic_update_slice(a, mine[None], (device, 0, 0)) for a, mine in zip(got[len(p_m0):], replicated)]

    sums = {}
    for groups, base, per_layer in ((ffn_groups, N_MIX_GROUPS, ((1, p_f1, o_f1), (0, p_f0, o_f0))), (mix_groups, 0, ((1, p_m1, o_m1), (0, p_m0, o_m0)))):
        for li, parts, others in per_layer:
            for i, (p, o) in enumerate(zip(parts, others)):
                sums[base + i] = chip_sum(p, o, place, li, sums.get(base + i), f"chip_sum_{base + i}_l{li}")
    g_shard = {}
    for (names, axis), s in zip(GROUPS, sibling_gather([sums[i] for i in range(len(GROUPS))])):
        g_shard.update(_split_group(s, names, axis, grad_wire_shape))
    g_shard["conv_w"] = g_shard["conv_w"][:, :CONV_W]

    row_shapes = [given[n].shape for n in ROW_PARAMS]
    g_rows = sum_devices(rows_all, "row_params_sum")
    g_pool_w = sum_devices(pool_w_all, "pool_w_sum")
    g_rep = dict(zip(ROW_PARAMS, _unpack_rows(g_rows, row_shapes)))
    g_rep["pool_w"] = g_pool_w.reshape(given["pool_w"].shape)

    g_out, d_out, m_out, v_out = {}, {}, {}, {}
    for n in sharded_names + ["pool_w"]:
        shp = given[n].shape
        three_d = (shp[0], int(np.prod(shp[1:-1])), shp[-1])
        g_n = g_shard[n] if n in SHARDED else g_rep[n]
        view = (lambda a: jnp.swapaxes(a.reshape(three_d), 1, 2)) if n in ROWS_MINOR else (lambda a: a.reshape(three_d))
        back = (lambda a: jnp.swapaxes(a, 1, 2).reshape(shp)) if n in ROWS_MINOR else (lambda a: a.reshape(shp))
        d, mn, vn = adamw(view(given[n]), view(g_n), view(mom[n]), view(var[n]), "adamw_" + n)
        g_out[n], d_out[n], m_out[n], v_out[n] = g_n, back(d), back(mn), back(vn)
    rd, rm, rv = adamw(_pack_rows([given[n] for n in ROW_PARAMS])[None], g_rows[None], _pack_rows([mom[n] for n in ROW_PARAMS])[None],
                       _pack_rows([var[n] for n in ROW_PARAMS])[None], "adamw_row_params")
    for n, d, mn, vn in zip(ROW_PARAMS, *[_unpack_rows(a[0], row_shapes) for a in (rd, rm, rv)]):
        g_out[n], d_out[n], m_out[n], v_out[n] = g_rep[n], d, mn, vn

    return (loss, grad_x[None], *[g_out[n] for n in WEIGHTS], *[d_out[n] for n in WEIGHTS], *[m_out[n] for n in WEIGHTS],
            *[v_out[n] for n in WEIGHTS])
```

```python
import functools
import math

import numpy as np
import jax
import jax.numpy as jnp
from jax import lax
from jax.experimental import pallas as pl
from jax.experimental.pallas import tpu as pltpu

F32, BF16 = jnp.float32, jnp.bfloat16

D_MODEL = 1024
N_HEADS = 8
NOPE, ROPE, VDIM = 64, 32, 64
HALF_ROPE = ROPE // 2
Q_RANK, KV_RANK = 384, 256
CONV_C, CONV_W = 512, 31
POOL_C, POOL_G, POOL_GD = 512, 4, 128
POOL_WINDOWS = (2, 4, 8, 16)
D_FF = 2816
FF_HALF = D_FF // 2
N_LAYERS = 2
EPS = 1e-6
ROPE_THETA = 10000.0
ATT_SCALE = 1.0 / math.sqrt(NOPE + ROPE)
O_Q, O_KV, O_KR, O_CONV, O_POOL, O_GATE, D_IN = 0, 384, 640, 672, 1696, 2208, 5280

LANE = 128
HP = 128
ZG, ZC, ZP, ZA, ZW = 0, 3072, 4096, 4608, 5376
ZA_W = Q_RANK + KV_RANK + HP
KR_LANE = NOPE
HW = N_HEADS * HP

ADAM_LR, ADAM_B1, ADAM_B2, ADAM_EPS, ADAM_WD, ADAM_STEP = 0.001, 0.9, 0.999, 1e-08, 0.01, 10

ROW_TILE = 512
WIDE_ROW_TILE = 256
ATT_TILE_FWD = 1024
ATT_TILE_BWD = 512
ATT_HEADS = 4
CONV_CHUNK = 256
MM_TM, MM_TN, MM_TK = 1024, 1408, 1024
MM_TILE_MAX = 2048
FFN_TM = 512
MM_VMEM_BUDGET = 40 * 1024 * 1024
HBM_BYTES_PER_US = 3.0e6
GRID_STEP_US = 0.35
VMEM_LIMIT = 56 * 1024 * 1024
SUM_TILE_BYTES = 1024 * 1024

HALF_ALIGN = 16
MESH_AXES = ("x", "y", "c")
PACK_W = 1024
PACK_ROWS = 8


def _cparams(sem):
    return pltpu.CompilerParams(dimension_semantics=sem, vmem_limit_bytes=VMEM_LIMIT)


def _tile(n, target):
    if n <= target:
        return n
    best = None
    for t in range(LANE, target + 1, LANE):
        if n % t == 0:
            best = t
    assert best is not None, (n, target)
    return best


def _mm_tiles(m, n, k, a_bytes, b_bytes, out_bytes):
    divs = lambda d: sorted({t for t in range(LANE, min(d, MM_TILE_MAX) + 1, LANE) if d % t == 0} | ({d} if d <= MM_TILE_MAX else set()))
    best = None
    for tm in divs(m):
        for tn in divs(n):
            blocks = tm * k * a_bytes + k * tn * b_bytes + tm * tn * out_bytes
            if 2 * blocks + tm * tn * 4 > MM_VMEM_BUDGET:
                continue
            steps = (m // tm) * (n // tn)
            for rows_outer in (True, False):
                moved = (m * k * a_bytes + k * n * b_bytes * (m // tm)) if rows_outer else (k * n * b_bytes + m * k * a_bytes * (n // tn))
                cost = (moved + m * n * out_bytes + blocks) / HBM_BYTES_PER_US + steps * GRID_STEP_US
                if best is None or cost < best[0]:
                    best = (cost, tm, tn, rows_outer)
    if best is not None:
        return best[1], best[2], k, best[3]
    return _tile(m, MM_TM), _tile(n, MM_TN), _tile(k, MM_TK), True


def mm(a, b, *, ta=False, tb=False, out_dtype=F32, name, carry=None):
    m, k = (a.shape[1], a.shape[0]) if ta else a.shape
    n, k2 = b.shape if tb else (b.shape[1], b.shape[0])
    assert k == k2, (a.shape, b.shape, ta, tb)
    tm, tn, tk, rows_outer = _mm_tiles(m, n, k, a.dtype.itemsize, b.dtype.itemsize, jnp.dtype(out_dtype).itemsize)
    nk = k // tk
    dims = (((0 if ta else 1,), (1 if tb else 0,)), ((), ()))
    grid = (m // tm, n // tn, nk) if rows_outer else (n // tn, m // tm, nk)

    def body(*refs):
        (a_ref, b_ref, o_ref, *acc), start, finish = _carried(carry, refs, 2, 1, 0 if nk == 1 else 1)
        ids = [pl.program_id(d) for d in range(3)]
        if start is not None:
            pl.when((ids[0] == 0) & (ids[1] == 0) & (ids[2] == 0))(start)
        part = lax.dot_general(a_ref[...].astype(BF16), b_ref[...].astype(BF16), dims, preferred_element_type=F32)
        if nk == 1:
            o_ref[...] = part.astype(o_ref.dtype)
        else:
            (acc_ref,) = acc
            kk = ids[2]

            @pl.when(kk == 0)
            def _():
                acc_ref[...] = part

            @pl.when(kk > 0)
            def _():
                acc_ref[...] += part

            @pl.when(kk == nk - 1)
            def _():
                o_ref[...] = acc_ref[...].astype(o_ref.dtype)
        if finish is not None:
            pl.when((ids[0] == grid[0] - 1) & (ids[1] == grid[1] - 1) & (ids[2] == nk - 1))(finish)

    ij = (lambda g0, g1: (g0, g1)) if rows_outer else (lambda g0, g1: (g1, g0))

    def a_map(g0, g1, kk):
        i, _ = ij(g0, g1)
        return (kk, i) if ta else (i, kk)

    def b_map(g0, g1, kk):
        _, j = ij(g0, g1)
        return (j, kk) if tb else (kk, j)

    ex_in_specs, ex_out_specs, ex_out_shape, ex_scratch, ex_inputs = _carry_specs(carry)
    a_spec = pl.BlockSpec((tk, tm) if ta else (tm, tk), a_map)
    b_spec = pl.BlockSpec((tn, tk) if tb else (tk, tn), b_map)
    out, *carried = pl.pallas_call(
        body,
        grid=grid,
        in_specs=[a_spec, b_spec] + ex_in_specs,
        out_specs=[pl.BlockSpec((tm, tn), lambda g0, g1, kk: ij(g0, g1))] + ex_out_specs,
        out_shape=[jax.ShapeDtypeStruct((m, n), out_dtype)] + ex_out_shape,
        scratch_shapes=([] if nk == 1 else [pltpu.VMEM((tm, tn), F32)]) + ex_scratch,
        compiler_params=_cparams(("arbitrary", "arbitrary", "arbitrary") if carry is not None else ("parallel", "parallel", "arbitrary")),
        name=name,
    )(a, b, *ex_inputs)
    return out if carry is None else (out, carried)


def ffn_in(hn, w_gu, name):
    s_len, k = hn.shape
    tm = min(FFN_TM, s_len)

    def body(a_ref, b_ref, gu_ref, act_ref):
        r = jnp.dot(a_ref[...], b_ref[...], preferred_element_type=F32)
        gu_ref[...] = r.astype(gu_ref.dtype)
        act_ref[...] = (_silu(r[:, :FF_HALF]) * r[:, FF_HALF:]).astype(act_ref.dtype)

    return pl.pallas_call(
        body,
        grid=(2, s_len // tm),
        in_specs=[pl.BlockSpec((tm, k), lambda j, i: (i, 0)), pl.BlockSpec((k, 2 * FF_HALF), lambda j, i: (0, j))],
        out_specs=[pl.BlockSpec((tm, 2 * FF_HALF), lambda j, i: (i, j)), pl.BlockSpec((tm, FF_HALF), lambda j, i: (i, j))],
        out_shape=[jax.ShapeDtypeStruct((s_len, 2 * D_FF), BF16), jax.ShapeDtypeStruct((s_len, D_FF), BF16)],
        compiler_params=_cparams(("arbitrary", "parallel")),
        name=name,
    )(hn, w_gu)


def ffn_out_dx(d_y, w_down, gu, name):
    s_len, k = d_y.shape
    tm = min(FFN_TM, s_len)

    def body(a_ref, b_ref, gu_ref, dgu_ref):
        da = lax.dot_general(a_ref[...], b_ref[...], NT_DIMS, preferred_element_type=F32)
        gt = gu_ref[:, :FF_HALF].astype(F32)
        up = gu_ref[:, FF_HALF:].astype(F32)
        sg = jax.nn.sigmoid(gt)
        dgu_ref[:, :FF_HALF] = (da * up * sg * (1.0 + gt * (1.0 - sg))).astype(dgu_ref.dtype)
        dgu_ref[:, FF_HALF:] = (da * gt * sg).astype(dgu_ref.dtype)

    pair = pl.BlockSpec((tm, 2 * FF_HALF), lambda j, i: (i, j))
    return pl.pallas_call(
        body,
        grid=(2, s_len // tm),
        in_specs=[pl.BlockSpec((tm, k), lambda j, i: (i, 0)), pl.BlockSpec((FF_HALF, k), lambda j, i: (j, 0)), pair],
        out_specs=pair,
        out_shape=jax.ShapeDtypeStruct((s_len, 2 * D_FF), BF16),
        compiler_params=_cparams(("parallel", "arbitrary")),
        name=name,
    )(d_y, w_down, gu)


def rowwise(name, body, rows, row_ins, full_ins, row_outs, acc_outs=(), into=None, tile=None):
    tile = min(tile or ROW_TILE, rows)
    into = into or {}
    in_specs = [pl.BlockSpec((tile, w), lambda i, cb=cb: (i, cb)) for _, w, cb in row_ins]
    in_specs += [pl.BlockSpec(a.shape, lambda i, nd=a.ndim: (0,) * nd) for a in full_ins]
    in_specs += [ANY for _ in into]
    n_in = len(row_ins) + len(full_ins)
    aliases = {n_in + k: oi for k, oi in enumerate(into)}
    out_specs, out_shape = [], []
    for ro in row_outs:
        w, dt, full_w, cb = ro if len(ro) == 4 else (*ro, ro[0], 0)
        out_specs.append(pl.BlockSpec((tile, w), lambda i, cb=cb: (i, cb)))
        out_shape.append(jax.ShapeDtypeStruct((rows, full_w), dt))
    out_specs += [pl.BlockSpec(s, lambda i, nd=len(s): (0,) * nd) for s, _ in acc_outs]
    out_shape += [jax.ShapeDtypeStruct(s, dt) for s, dt in acc_outs]
    n_refs = n_in

    def call_body(*refs):
        body(*refs[:n_refs], *refs[n_refs + len(into):])

    outs = pl.pallas_call(
        call_body,
        grid=(rows // tile,),
        in_specs=in_specs,
        out_specs=out_specs,
        out_shape=out_shape,
        input_output_aliases=aliases,
        compiler_params=_cparams(("arbitrary",)),
        name=name,
    )(*[a for a, _, _ in row_ins], *full_ins, *into.values())
    return outs


def _whole(a):
    return (a, a.shape[1], 0)


def _acc(ref, val):
    @pl.when(pl.program_id(0) == 0)
    def _():
        ref[...] = val

    @pl.when(pl.program_id(0) > 0)
    def _():
        ref[...] += val


def _rms(x, g):
    return x * lax.rsqrt(jnp.mean(x * x, axis=-1, keepdims=True) + EPS) * g


def _layer_norm(x, g, b):
    mu = jnp.mean(x, axis=-1, keepdims=True)
    xc = x - mu
    return xc * lax.rsqrt(jnp.mean(xc * xc, axis=-1, keepdims=True) + EPS) * g + b


def _silu(x):
    return x * jax.nn.sigmoid(x)


def _rope(x, cc, sa, sb):
    return x * cc + pltpu.roll(x, HALF_ROPE, 1) * sa + pltpu.roll(x, HP - HALF_ROPE, 1) * sb


def _rope_t(dy, cc, sa, sb):
    return dy * cc + pltpu.roll(dy * sa, HP - HALF_ROPE, 1) + pltpu.roll(dy * sb, HALF_ROPE, 1)


def rope_tables(pos_col, rows):
    lane = np.arange(HP)
    idx = np.where(lane < KR_LANE + HALF_ROPE, lane - KR_LANE, lane - KR_LANE - HALF_ROPE)
    in_rope = (lane >= KR_LANE) & (lane < KR_LANE + ROPE)
    inv_freq = (np.float32(ROPE_THETA) ** (-np.arange(0, ROPE, 2, dtype=np.float32) / np.float32(ROPE))).astype(np.float32)
    freq_row = np.where(in_rope, inv_freq[np.clip(idx, 0, HALF_ROPE - 1)], 0.0).astype(np.float32)[None, :]
    first = ((lane >= KR_LANE) & (lane < KR_LANE + HALF_ROPE)).astype(np.float32)[None, :]
    second = ((lane >= KR_LANE + HALF_ROPE) & (lane < KR_LANE + ROPE)).astype(np.float32)[None, :]

    def body(pos_ref, f_ref, a_ref, b_ref, cc_ref, sa_ref, sb_ref):
        ang = pos_ref[...].astype(F32) * f_ref[...]
        s = jnp.sin(ang)
        cc_ref[...] = jnp.cos(ang)
        sa_ref[...] = s * b_ref[...]
        sb_ref[...] = -s * a_ref[...]

    return rowwise("rope_tables", body, rows, [_whole(pos_col)], [jnp.asarray(freq_row), jnp.asarray(first), jnp.asarray(second)],
                   [(HP, F32)] * 3)


def _causal_mask(t):
    r = lax.broadcasted_iota(jnp.int32, (t, t), 0)
    c = lax.broadcasted_iota(jnp.int32, (t, t), 1)
    return r, c


NT_DIMS = (((1,), (1,)), ((), ()))


def _carried(carry, refs, n_in, n_out, n_scratch):
    if carry is None:
        return refs, None, None
    ni, no = len(carry.ins), len(carry.outs)
    own_in, ex_in = refs[:n_in], refs[n_in:n_in + ni]
    own_out, ex_out = refs[n_in + ni:n_in + ni + n_out], refs[n_in + ni + n_out:n_in + ni + n_out + no]
    scratch = refs[n_in + ni + n_out + no:]
    sems = scratch[n_scratch:]
    return (*own_in, *own_out, *scratch[:n_scratch]), (lambda: carry.start(ex_in, ex_out, *sems)), (lambda: carry.finish(ex_in, ex_out, *sems))


def _carry_specs(carry):
    if carry is None:
        return [], [], [], [], []
    sems = [pltpu.SemaphoreType.DMA((carry.n_sems,)), pltpu.SemaphoreType.DMA((carry.n_sems,))]
    return [ANY] * len(carry.ins), [ANY] * len(carry.outs), list(carry.outs), sems, list(carry.ins)


def attention_fwd(q, k, v, name, carry=None):
    s_len = q.shape[0]
    t = min(ATT_TILE_FWD, s_len)
    nb = s_len // t
    hb = ATT_HEADS
    w = hb * HP
    nh = N_HEADS // hb

    def body(*refs):
        (q_ref, k_ref, v_ref, o_ref, lse_ref, m_sc, acc_sc), start, finish = _carried(carry, refs, 3, 2, 2)
        qi = pl.program_id(1)
        if start is not None:
            pl.when((pl.program_id(0) == 0) & (qi == 0))(start)
        m_sc[...] = jnp.full_like(m_sc, -jnp.inf)
        acc_sc[...] = jnp.zeros_like(acc_sc)

        def block(j, masked):
            ks = pl.ds(pl.multiple_of(j * t, t), t)
            for hh in range(hb):
                ls = slice(hh * HP, (hh + 1) * HP)
                s = lax.dot_general(q_ref[:, ls], k_ref[ks, ls], NT_DIMS, preferred_element_type=F32) * ATT_SCALE
                if masked:
                    r, c = _causal_mask(t)
                    s = jnp.where(c <= r, s, -jnp.inf)
                m_old = m_sc[hh]
                m_new = jnp.maximum(m_old, jnp.max(s, axis=-1, keepdims=True))
                p = jnp.exp(s - m_new)
                acc_sc[hh] = jnp.exp(m_old - m_new) * acc_sc[hh] + jnp.dot(p.astype(BF16), v_ref[ks, ls], preferred_element_type=F32)
                m_sc[hh] = m_new

        def loop_body(j, carry):
            block(j, False)
            return carry

        lax.fori_loop(0, qi, loop_body, 0)
        block(qi, True)
        lane = lax.broadcasted_iota(jnp.int32, (t, HP), 1)
        for hh in range(hb):
            acc = acc_sc[hh]
            l = jnp.sum(jnp.where(lane == VDIM, acc, 0.0), axis=-1, keepdims=True)
            o_ref[:, hh * HP:(hh + 1) * HP] = jnp.where(lane < VDIM, acc / l, 0.0).astype(o_ref.dtype)
            lse_ref[hh] = m_sc[hh] + jnp.log(l)
        if finish is not None:
            pl.when((pl.program_id(0) == nh - 1) & (qi == nb - 1))(finish)

    ex_in_specs, ex_out_specs, ex_out_shape, ex_scratch, ex_inputs = _carry_specs(carry)
    resident = pl.BlockSpec((s_len, w), lambda h, qi: (0, h))
    o, lse, *carried = pl.pallas_call(
        body,
        grid=(nh, nb),
        in_specs=[pl.BlockSpec((t, w), lambda h, qi: (qi, h)), resident, resident] + ex_in_specs,
        out_specs=[pl.BlockSpec((t, w), lambda h, qi: (qi, h)), pl.BlockSpec((hb, t, 1), lambda h, qi: (h, qi, 0))] + ex_out_specs,
        out_shape=[jax.ShapeDtypeStruct((s_len, HW), BF16), jax.ShapeDtypeStruct((N_HEADS, s_len, 1), F32)] + ex_out_shape,
        scratch_shapes=[pltpu.VMEM((hb, t, 1), F32), pltpu.VMEM((hb, t, HP), F32)] + ex_scratch,
        compiler_params=_cparams(("arbitrary", "arbitrary")),
        name=name,
    )(q, k, v, *ex_inputs)
    return o, lse, carried


def attention_delta(do, o):
    s_len = do.shape[0]
    t = min(ROW_TILE, s_len)

    def body(do_ref, o_ref, d_ref):
        prod = do_ref[...].astype(F32) * o_ref[...].astype(F32)
        for h in range(N_HEADS):
            d_ref[h] = jnp.sum(prod[:, h * HP:(h + 1) * HP], axis=-1, keepdims=True)

    return pl.pallas_call(
        body,
        grid=(s_len // t,),
        in_specs=[pl.BlockSpec((t, HW), lambda i: (i, 0))] * 2,
        out_specs=pl.BlockSpec((N_HEADS, t, 1), lambda i: (0, i, 0)),
        out_shape=jax.ShapeDtypeStruct((N_HEADS, s_len, 1), F32),
        compiler_params=_cparams(("arbitrary",)),
        name="attention_delta",
    )(do, o)


TN_DIMS = (((0,), (0,)), ((), ()))


def attention_bwd(q, k, v, do, lse_row, delta_row, name, carry=None):
    s_len = q.shape[0]
    t = min(ATT_TILE_BWD, s_len)
    nb = s_len // t
    hb = ATT_HEADS
    w = hb * HP
    nh = N_HEADS // hb

    def body(*refs):
        (q_ref, k_ref, v_ref, do_ref, lse_ref, dl_ref, dq_ref, dk_ref, dv_ref, dk_sc, dv_sc), start, finish = _carried(carry, refs, 6, 3, 2)
        ki = pl.program_id(1)
        if start is not None:
            pl.when((pl.program_id(0) == 0) & (ki == 0))(start)

        @pl.when(ki == 0)
        def _():
            dq_ref[...] = jnp.zeros_like(dq_ref)

        dk_sc[...] = jnp.zeros_like(dk_sc)
        dv_sc[...] = jnp.zeros_like(dv_sc)

        def block(j, masked):
            qs = pl.ds(pl.multiple_of(j * t, t), t)
            for hh in range(hb):
                ls = slice(hh * HP, (hh + 1) * HP)
                qb = q_ref[qs, ls]
                dob = do_ref[qs, ls]
                kb = k_ref[:, ls]
                st = lax.dot_general(kb, qb, NT_DIMS, preferred_element_type=F32) * ATT_SCALE
                pt = jnp.exp(st - lse_ref[hh, j])
                if masked:
                    r, c = _causal_mask(t)
                    pt = jnp.where(r <= c, pt, 0.0)
                dv_sc[hh] += jnp.dot(pt.astype(BF16), dob, preferred_element_type=F32)
                dpt = lax.dot_general(v_ref[:, ls], dob, NT_DIMS, preferred_element_type=F32)
                dst = (pt * (dpt - dl_ref[hh, j]) * ATT_SCALE).astype(BF16)
                dk_sc[hh] += jnp.dot(dst, qb, preferred_element_type=F32)
                dq_ref[qs, ls] += lax.dot_general(dst, kb, TN_DIMS, preferred_element_type=F32)

        block(ki, True)

        def loop_body(j, carry):
            block(j, False)
            return carry

        lax.fori_loop(ki + 1, nb, loop_body, 0)
        for hh in range(hb):
            ls = slice(hh * HP, (hh + 1) * HP)
            dk_ref[:, ls] = dk_sc[hh].astype(dk_ref.dtype)
            dv_ref[:, ls] = dv_sc[hh].astype(dv_ref.dtype)
        if finish is not None:
            pl.when((pl.program_id(0) == nh - 1) & (ki == nb - 1))(finish)

    ex_in_specs, ex_out_specs, ex_out_shape, ex_scratch, ex_inputs = _carry_specs(carry)
    k_spec = pl.BlockSpec((t, w), lambda h, ki: (ki, h))
    resident = pl.BlockSpec((s_len, w), lambda h, ki: (0, h))
    row_spec = pl.BlockSpec((hb, nb, 1, t), lambda h, ki: (h, 0, 0, 0))
    dq, dk, dv, *carried = pl.pallas_call(
        body,
        grid=(nh, nb),
        in_specs=[resident, k_spec, k_spec, resident, row_spec, row_spec] + ex_in_specs,
        out_specs=[resident, k_spec, k_spec] + ex_out_specs,
        out_shape=[jax.ShapeDtypeStruct((s_len, HW), F32), jax.ShapeDtypeStruct((s_len, HW), F32), jax.ShapeDtypeStruct((s_len, HW), BF16)]
        + ex_out_shape,
        scratch_shapes=[pltpu.VMEM((hb, t, HP), F32), pltpu.VMEM((hb, t, HP), F32)] + ex_scratch,
        compiler_params=_cparams(("arbitrary", "arbitrary")),
        name=name,
    )(q, k, v, do, lse_row, delta_row, *ex_inputs)
    return dq, dk, dv, carried


CONV_PAD = 32


def conv_fwd(z, conv_w, conv_b):
    s_len = z.shape[0]
    ch = min(CONV_CHUNK, s_len)

    def body(ag_ref, w_ref, b_ref, c_ref, pad_ref):
        pad_ref[0:CONV_PAD, :] = jnp.zeros((CONV_PAD, LANE), F32)
        pad_ref[CONV_PAD:CONV_PAD + s_len, :] = ag_ref[:, 0:LANE] * jax.nn.sigmoid(ag_ref[:, LANE:2 * LANE])

        def chunk(i, carry):
            base = pl.multiple_of(i * ch, ch)
            acc = jnp.zeros((ch, LANE), F32) + b_ref[...]
            for kk in range(CONV_W):
                acc = acc + pad_ref[pl.ds(base + CONV_PAD - (CONV_W - 1) + kk, ch), :] * w_ref[kk:kk + 1, :]
            c_ref[pl.ds(base, ch), :] = acc
            return carry

        lax.fori_loop(0, s_len // ch, chunk, 0)

    nblk = CONV_C // LANE
    return pl.pallas_call(
        body,
        grid=(nblk,),
        in_specs=[pl.BlockSpec((s_len, 2 * LANE), lambda j: (0, ZC // (2 * LANE) + j)),
                  pl.BlockSpec((CONV_W, LANE), lambda j: (0, j)), pl.BlockSpec((1, LANE), lambda j: (0, j))],
        out_specs=pl.BlockSpec((s_len, LANE), lambda j: (0, j)),
        out_shape=jax.ShapeDtypeStruct((s_len, CONV_C), F32),
        scratch_shapes=[pltpu.VMEM((s_len + CONV_PAD, LANE), F32)],
        compiler_params=_cparams(("arbitrary",)),
        name="conv_fwd",
    )(z, conv_w, conv_b)


def conv_bwd(z, dc, conv_w, dz):
    s_len = z.shape[0]
    ch = min(CONV_CHUNK, s_len)

    def body(ag_ref, dc_ref, w_ref, dz_in, dag_ref, dw_ref, db_ref, pad_ref, dpad_ref, wacc_ref):
        del dz_in
        pad_ref[0:CONV_PAD, :] = jnp.zeros((CONV_PAD, LANE), F32)
        pad_ref[CONV_PAD:CONV_PAD + s_len, :] = ag_ref[:, 0:LANE] * jax.nn.sigmoid(ag_ref[:, LANE:2 * LANE])
        dpad_ref[0:s_len, :] = dc_ref[...]
        dpad_ref[s_len:s_len + CONV_PAD, :] = jnp.zeros((CONV_PAD, LANE), F32)
        wacc_ref[...] = jnp.zeros_like(wacc_ref)
        db_ref[...] = jnp.sum(dc_ref[...], axis=0, keepdims=True)

        def chunk(i, carry):
            base = pl.multiple_of(i * ch, ch)
            dcc = dpad_ref[pl.ds(base, ch), :]
            dh = jnp.zeros((ch, LANE), F32)
            for kk in range(CONV_W):
                dh = dh + dpad_ref[pl.ds(base + (CONV_W - 1) - kk, ch), :] * w_ref[kk:kk + 1, :]
                prod = dcc * pad_ref[pl.ds(base + CONV_PAD - (CONV_W - 1) + kk, ch), :]
                wacc_ref[kk * 8:(kk + 1) * 8, :] += prod.reshape(ch // 8, 8, LANE).sum(axis=0)
            a = ag_ref[pl.ds(base, ch), 0:LANE]
            sgc = jax.nn.sigmoid(ag_ref[pl.ds(base, ch), LANE:2 * LANE])
            dag_ref[pl.ds(base, ch), 0:LANE] = (dh * sgc).astype(dag_ref.dtype)
            dag_ref[pl.ds(base, ch), LANE:2 * LANE] = (dh * a * sgc * (1.0 - sgc)).astype(dag_ref.dtype)
            return carry

        lax.fori_loop(0, s_len // ch, chunk, 0)
        for kk in range(CONV_W):
            dw_ref[kk:kk + 1, :] = jnp.sum(wacc_ref[kk * 8:(kk + 1) * 8, :], axis=0, keepdims=True)

    nblk = CONV_C // LANE
    pair = pl.BlockSpec((s_len, 2 * LANE), lambda j: (0, ZC // (2 * LANE) + j))
    return pl.pallas_call(
        body,
        grid=(nblk,),
        in_specs=[pair, pl.BlockSpec((s_len, LANE), lambda j: (0, j)), pl.BlockSpec((CONV_W, LANE), lambda j: (0, j)), ANY],
        out_specs=[pair, pl.BlockSpec((CONV_W, LANE), lambda j: (0, j)), pl.BlockSpec((1, LANE), lambda j: (0, j))],
        out_shape=[jax.ShapeDtypeStruct(dz.shape, dz.dtype), jax.ShapeDtypeStruct((CONV_W, CONV_C), F32), jax.ShapeDtypeStruct((1, CONV_C), F32)],
        scratch_shapes=[pltpu.VMEM((s_len + CONV_PAD, LANE), F32), pltpu.VMEM((s_len + CONV_PAD, LANE), F32),
                        pltpu.VMEM((CONV_W * 8, LANE), F32)],
        input_output_aliases={3: 0},
        compiler_params=_cparams(("arbitrary",)),
        name="conv_bwd",
    )(z, dc, conv_w, dz)


POOL_PAD = 16


def _pool_count(base, ch, w):
    t = base + lax.broadcasted_iota(jnp.int32, (ch, 1), 0)
    return jnp.minimum(t + 1, w).astype(F32)


def pool_fwd(z, pool_w, pool_scale):
    s_len = z.shape[0]
    ch = min(CONV_CHUNK, s_len)

    def body(u_ref, pw_ref, sc_ref, m_ref, pad_ref):
        gi = pl.program_id(0)
        pad_ref[0:POOL_PAD, :] = jnp.zeros((POOL_PAD, LANE), F32)
        pad_ref[POOL_PAD:POOL_PAD + s_len, :] = u_ref[...]

        def run(w):
            def chunk(i, carry):
                base = pl.multiple_of(i * ch, ch)
                acc = jnp.zeros((ch, LANE), F32)
                for j in range(w):
                    acc = acc + pad_ref[pl.ds(base + POOL_PAD - j, ch), :]
                d = acc / _pool_count(base, ch, w) - u_ref[pl.ds(base, ch), :]
                md = jnp.dot(d.astype(BF16), pw_ref[0], preferred_element_type=F32)
                m_ref[pl.ds(base, ch), :] = (md * sc_ref[...]).astype(m_ref.dtype)
                return carry

            lax.fori_loop(0, s_len // ch, chunk, 0)

        for g, w in enumerate(POOL_WINDOWS):
            pl.when(gi == g)(functools.partial(run, w))

    return pl.pallas_call(
        body,
        grid=(POOL_G,),
        in_specs=[pl.BlockSpec((s_len, LANE), lambda g: (0, ZP // LANE + g)), pl.BlockSpec((1, POOL_GD, POOL_GD), lambda g: (g, 0, 0)),
                  pl.BlockSpec((1, LANE), lambda g: (0, g))],
        out_specs=pl.BlockSpec((s_len, LANE), lambda g: (0, g)),
        out_shape=jax.ShapeDtypeStruct((s_len, POOL_C), BF16),
        scratch_shapes=[pltpu.VMEM((s_len + POOL_PAD, LANE), F32)],
        compiler_params=_cparams(("arbitrary",)),
        name="pool_fwd",
    )(z, pool_w, pool_scale)


def pool_bwd(z, dm, pool_w, pool_scale, dz):
    s_len = z.shape[0]
    ch = min(CONV_CHUNK, s_len)

    def body(u_ref, dm_ref, pw_ref, sc_ref, dz_in, du_ref, dpw_ref, dsc_ref, pad_ref, epad_ref, dd_ref, sacc_ref):
        del dz_in
        gi = pl.program_id(0)
        pad_ref[0:POOL_PAD, :] = jnp.zeros((POOL_PAD, LANE), F32)
        pad_ref[POOL_PAD:POOL_PAD + s_len, :] = u_ref[...]
        epad_ref[s_len:s_len + POOL_PAD, :] = jnp.zeros((POOL_PAD, LANE), F32)
        dpw_ref[...] = jnp.zeros_like(dpw_ref)
        sacc_ref[...] = jnp.zeros_like(sacc_ref)

        def run(w):
            def first(i, carry):
                base = pl.multiple_of(i * ch, ch)
                acc = jnp.zeros((ch, LANE), F32)
                for j in range(w):
                    acc = acc + pad_ref[pl.ds(base + POOL_PAD - j, ch), :]
                cnt = _pool_count(base, ch, w)
                d = (acc / cnt - u_ref[pl.ds(base, ch), :]).astype(BF16)
                md = jnp.dot(d, pw_ref[0], preferred_element_type=F32)
                dmc = dm_ref[pl.ds(base, ch), :]
                sacc_ref[...] += (dmc * md).reshape(ch // 8, 8, LANE).sum(axis=0)
                dmd = (dmc * sc_ref[...]).astype(BF16)
                dpw_ref[0] += lax.dot_general(d, dmd, (((0,), (0,)), ((), ())), preferred_element_type=F32)
                dd = lax.dot_general(dmd, pw_ref[0], (((1,), (1,)), ((), ())), preferred_element_type=F32)
                dd_ref[pl.ds(base, ch), :] = dd
                epad_ref[pl.ds(base, ch), :] = dd / cnt
                return carry

            lax.fori_loop(0, s_len // ch, first, 0)

            def second(i, carry):
                base = pl.multiple_of(i * ch, ch)
                acc = jnp.zeros((ch, LANE), F32)
                for j in range(w):
                    acc = acc + epad_ref[pl.ds(base + j, ch), :]
                du_ref[pl.ds(base, ch), :] = (acc - dd_ref[pl.ds(base, ch), :]).astype(du_ref.dtype)
                return carry

            lax.fori_loop(0, s_len // ch, second, 0)

        for g, w in enumerate(POOL_WINDOWS):
            pl.when(gi == g)(functools.partial(run, w))
        dsc_ref[...] = jnp.sum(sacc_ref[...], axis=0, keepdims=True)

    return pl.pallas_call(
        body,
        grid=(POOL_G,),
        in_specs=[pl.BlockSpec((s_len, LANE), lambda g: (0, ZP // LANE + g)), pl.BlockSpec((s_len, LANE), lambda g: (0, g)),
                  pl.BlockSpec((1, POOL_GD, POOL_GD), lambda g: (g, 0, 0)), pl.BlockSpec((1, LANE), lambda g: (0, g)), ANY],
        out_specs=[pl.BlockSpec((s_len, LANE), lambda g: (0, ZP // LANE + g)), pl.BlockSpec((1, POOL_GD, POOL_GD), lambda g: (g, 0, 0)),
                   pl.BlockSpec((1, LANE), lambda g: (0, g))],
        out_shape=[jax.ShapeDtypeStruct(dz.shape, dz.dtype), jax.ShapeDtypeStruct((POOL_G, POOL_GD, POOL_GD), F32),
                   jax.ShapeDtypeStruct((1, POOL_C), F32)],
        scratch_shapes=[pltpu.VMEM((s_len + POOL_PAD, LANE), F32), pltpu.VMEM((s_len + POOL_PAD, LANE), F32),
                        pltpu.VMEM((s_len, LANE), F32), pltpu.VMEM((8, LANE), F32)],
        input_output_aliases={4: 0},
        compiler_params=_cparams(("arbitrary",)),
        name="pool_bwd",
    )(z, dm, pool_w, pool_scale, dz)


def _row(v):
    return v.reshape(1, -1)


def _rms_body(x_ref, g_ref, o_ref):
    o_ref[...] = _rms(x_ref[...], g_ref[...]).astype(o_ref.dtype)


def _post_body(y_ref, x_ref, g_ref, o_ref):
    o_ref[...] = x_ref[...] + _rms(y_ref[...], g_ref[...])


def _post_bwd_body(y_ref, dh_ref, g_ref, dy_ref, dg_ref):
    _, vjp = jax.vjp(_rms, y_ref[...], g_ref[...])
    dy, dg = vjp(dh_ref[...])
    dy_ref[...] = dy.astype(dy_ref.dtype)
    _acc(dg_ref, dg)


def _pre_bwd_body(x_ref, dhn_ref, dres_ref, g_ref, dx_ref, dg_ref):
    _, vjp = jax.vjp(_rms, x_ref[...], g_ref[...])
    dx, dg = vjp(dhn_ref[...])
    dx_ref[...] = dres_ref[...] + dx
    _acc(dg_ref, dg)


def mixer_fwd(x, tabs, w, tag, carry=None, head=None):
    s_len = x.shape[0]
    cc, sa, sb = tabs
    sv = {"x": x}

    (h,) = rowwise("mix_norm_pre" + tag, _rms_body, s_len, [_whole(x)], [_row(w["mix_norm_pre"])], [(D_MODEL, BF16)])
    if head is None:
        z = mm(h, w["w_in"], name="in_proj" + tag)
    else:
        z, arrived = mm(h, w["w_in"], name="in_proj" + tag, carry=head[0])
        w = {**w, **head[1](arrived)}

    def prep_body(z_ref, cc_ref, sa_ref, sb_ref, qg_ref, kg_ref, qn_ref, ckv_ref, kr_ref):
        qn_ref[...] = _rms(z_ref[:, 0:Q_RANK], qg_ref[...]).astype(qn_ref.dtype)
        ckv_ref[...] = _rms(z_ref[:, Q_RANK:Q_RANK + KV_RANK], kg_ref[...]).astype(ckv_ref.dtype)
        kr_ref[...] = _rope(z_ref[:, Q_RANK + KV_RANK:ZA_W], cc_ref[...], sa_ref[...], sb_ref[...])

    qn, ckvn, kr = rowwise("attn_prep" + tag, prep_body, s_len, [(z, ZA_W, ZA // ZA_W), _whole(cc), _whole(sa), _whole(sb)],
                           [_row(w["q_norm"]), _row(w["kv_norm"])], [(Q_RANK, BF16), (KV_RANK, BF16), (HP, F32)])
    q_raw = mm(qn, w["w_uq"], name="q_proj" + tag)
    kv_raw = mm(ckvn, w["w_ukv"], name="kv_proj" + tag)

    def qkv_body(q_ref, kv_ref, kr_ref, cc_ref, sa_ref, sb_ref, qo_ref, ko_ref, vo_ref):
        c_, a_, b_, kro = cc_ref[...], sa_ref[...], sb_ref[...], kr_ref[...]
        for hh in range(N_HEADS):
            sl = slice(hh * HP, (hh + 1) * HP)
            qo_ref[:, sl] = _rope(q_ref[:, sl], c_, a_, b_).astype(qo_ref.dtype)
            ko_ref[:, sl] = (kv_ref[:, sl] + kro).astype(ko_ref.dtype)
        lane = lax.broadcasted_iota(jnp.int32, (q_ref.shape[0], HW), 1)
        vo_ref[...] = jnp.where((lane & (HP - 1)) == VDIM, 1.0, kv_ref[:, HW:2 * HW]).astype(vo_ref.dtype)

    q, k, v = rowwise("qkv_rope" + tag, qkv_body, s_len, [_whole(q_raw), _whole(kv_raw), _whole(kr), _whole(cc), _whole(sa), _whole(sb)], [],
                      [(HW, BF16)] * 3)
    o, lse, carried = attention_fwd(q, k, v, "attention_fwd" + tag, carry)
    y_attn = mm(o, w["w_attn_o"], out_dtype=BF16, name="attn_out" + tag)

    c = conv_fwd(z, w["conv_w"], _row(w["conv_b"]))

    def ln_body(c_ref, g_ref, b_ref, o_ref):
        o_ref[...] = _silu(_layer_norm(c_ref[...], g_ref[...], b_ref[...])).astype(o_ref.dtype)

    (cs,) = rowwise("conv_ln_silu" + tag, ln_body, s_len, [_whole(c)], [_row(w["conv_ln_g"]), _row(w["conv_ln_b"])], [(CONV_C, BF16)])
    y_conv = mm(cs, w["w_conv_o"], out_dtype=BF16, name="conv_out" + tag)

    m = pool_fwd(z, w["pool_w"], _row(w["pool_scale"]))
    y_pool = mm(m, w["w_pool_o"], out_dtype=BF16, name="pool_out" + tag)

    def merge_body(ya_ref, yc_ref, yp_ref, gl_ref, o_ref):
        gl = gl_ref[...]
        o_ref[...] = (jax.nn.sigmoid(gl[:, 0:D_MODEL]) * ya_ref[...].astype(F32) + jax.nn.sigmoid(gl[:, D_MODEL:2 * D_MODEL]) * yc_ref[...].astype(F32)
                      + jax.nn.sigmoid(gl[:, 2 * D_MODEL:3 * D_MODEL]) * yp_ref[...].astype(F32)).astype(o_ref.dtype)

    (merged,) = rowwise("gate_merge" + tag, merge_body, s_len, [_whole(y_attn), _whole(y_conv), _whole(y_pool), (z, 3 * D_MODEL, 0)], [],
                        [(D_MODEL, BF16)])
    mo = mm(merged, w["w_mix_o"], name="mix_out" + tag)

    (h1,) = rowwise("mix_norm_post" + tag, _post_body, s_len, [_whole(mo), _whole(x)], [_row(w["mix_norm_post"])], [(D_MODEL, F32)])
    sv.update(h=h, z=z, qn=qn, ckvn=ckvn, q=q, k=k, v=v, o=o, lse=lse, c=c, cs=cs, m=m, y_attn=y_attn, y_conv=y_conv, y_pool=y_pool,
              merged=merged, mo=mo, w=w)
    return h1, sv, carried


def ffn_fwd(h1, w, tag):
    s_len = h1.shape[0]
    (hn,) = rowwise("ffn_norm_pre" + tag, _rms_body, s_len, [_whole(h1)], [_row(w["ffn_norm_pre"])], [(D_MODEL, BF16)])
    gu, act = ffn_in(hn, w["w_gu"], "ffn_in" + tag)
    y = mm(act, w["w_down"], name="ffn_out" + tag)
    (h2,) = rowwise("ffn_norm_post" + tag, _post_body, s_len, [_whole(y), _whole(h1)], [_row(w["ffn_norm_post"])], [(D_MODEL, F32)])
    return h2, dict(h1=h1, hn=hn, gu=gu, act=act, y=y)


def ffn_bwd(dh2, sv, w, tag):
    s_len = dh2.shape[0]
    g = {}
    d_y, g["ffn_norm_post"] = rowwise("ffn_norm_post_bwd" + tag, _post_bwd_body, s_len, [_whole(sv["y"]), _whole(dh2)],
                                      [_row(w["ffn_norm_post"])], [(D_MODEL, BF16)], [((1, D_MODEL), F32)])
    g["w_down"] = mm(sv["act"], d_y, ta=True, name="ffn_out_dw" + tag)
    d_gu = ffn_out_dx(d_y, w["w_down"], sv["gu"], "ffn_out_dx" + tag)
    g["w_gu"] = mm(sv["hn"], d_gu, ta=True, name="ffn_in_dw" + tag)
    d_hn = mm(d_gu, w["w_gu"], tb=True, name="ffn_in_dx" + tag)
    dh1, g["ffn_norm_pre"] = rowwise(
        "ffn_norm_pre_bwd" + tag, _pre_bwd_body, s_len, [_whole(sv["h1"]), _whole(d_hn), _whole(dh2)], [_row(w["ffn_norm_pre"])], [(D_MODEL, F32)], [((1, D_MODEL), F32)])
    return dh1, g


def mixer_bwd(dh1, sv, tabs, w, tag, carry=None, tail=None):
    s_len = dh1.shape[0]
    cc, sa, sb = tabs
    g = {}

    d_mo, g["mix_norm_post"] = rowwise(
        "mix_norm_post_bwd" + tag, _post_bwd_body, s_len, [_whole(sv["mo"]), _whole(dh1)], [_row(w["mix_norm_post"])], [(D_MODEL, BF16)], [((1, D_MODEL), F32)])
    g["w_mix_o"] = mm(sv["merged"], d_mo, ta=True, name="mix_out_dw" + tag)
    d_merged = mm(d_mo, w["w_mix_o"], tb=True, name="mix_out_dx" + tag)

    def merge_bwd_body(dm_ref, ya_ref, yc_ref, yp_ref, gl_ref, dya_ref, dyc_ref, dyp_ref, dgl_ref):
        dmg = dm_ref[...]
        for i, (y_ref, dy_ref) in enumerate(((ya_ref, dya_ref), (yc_ref, dyc_ref), (yp_ref, dyp_ref))):
            sg = jax.nn.sigmoid(gl_ref[:, i * D_MODEL:(i + 1) * D_MODEL])
            dy_ref[...] = (dmg * sg).astype(dy_ref.dtype)
            dgl_ref[:, i * D_MODEL:(i + 1) * D_MODEL] = (dmg * y_ref[...].astype(F32) * sg * (1.0 - sg)).astype(dgl_ref.dtype)

    d_ya, d_yc, d_yp, dz = rowwise(
        "gate_merge_bwd" + tag, merge_bwd_body, s_len,
        [_whole(d_merged), _whole(sv["y_attn"]), _whole(sv["y_conv"]), _whole(sv["y_pool"]), (sv["z"], 3 * D_MODEL, 0)], [],
        [(D_MODEL, BF16)] * 3 + [(3 * D_MODEL, BF16, ZW, 0)], tile=WIDE_ROW_TILE)

    g["w_pool_o"] = mm(sv["m"], d_yp, ta=True, name="pool_out_dw" + tag)
    d_m = mm(d_yp, w["w_pool_o"], tb=True, name="pool_out_dx" + tag)
    dz, g["pool_w"], g["pool_scale"] = pool_bwd(sv["z"], d_m, w["pool_w"], _row(w["pool_scale"]), dz)

    g["w_conv_o"] = mm(sv["cs"], d_yc, ta=True, name="conv_out_dw" + tag)
    d_cs = mm(d_yc, w["w_conv_o"], tb=True, name="conv_out_dx" + tag)

    def ln_bwd_body(c_ref, dcs_ref, g_ref, b_ref, dc_ref, dg_ref, db_ref):
        f = lambda c_, g_, b_: _silu(_layer_norm(c_, g_, b_))
        _, vjp = jax.vjp(f, c_ref[...], g_ref[...], b_ref[...])
        dc, dg, db = vjp(dcs_ref[...])
        dc_ref[...] = dc
        _acc(dg_ref, dg)
        _acc(db_ref, db)

    d_c, g["conv_ln_g"], g["conv_ln_b"] = rowwise("conv_ln_silu_bwd" + tag, ln_bwd_body, s_len, [_whole(sv["c"]), _whole(d_cs)],
                                                  [_row(w["conv_ln_g"]), _row(w["conv_ln_b"])], [(CONV_C, F32)],
                                                  [((1, CONV_C), F32), ((1, CONV_C), F32)])
    dz, g["conv_w"], g["conv_b"] = conv_bwd(sv["z"], d_c, w["conv_w"], dz)

    g["w_attn_o"] = mm(sv["o"], d_ya, ta=True, name="attn_out_dw" + tag)
    d_o = mm(d_ya, w["w_attn_o"], tb=True, out_dtype=BF16, name="attn_out_dx" + tag)
    delta = attention_delta(d_o, sv["o"])
    t_bwd = min(ATT_TILE_BWD, s_len)
    rows_of = lambda a: a.reshape(N_HEADS, s_len // t_bwd, 1, t_bwd)
    dq, dk, dv, carried = attention_bwd(sv["q"], sv["k"], sv["v"], d_o, rows_of(sv["lse"]), rows_of(delta), "attention_bwd" + tag, carry)

    def qkv_bwd_body(dq_ref, dk_ref, dv_ref, cc_ref, sa_ref, sb_ref, dqp_ref, dkv_ref, dkr_ref):
        c_, a_, b_ = cc_ref[...], sa_ref[...], sb_ref[...]
        dk_sum = jnp.zeros((dq_ref.shape[0], HP), F32)
        for hh in range(N_HEADS):
            sl = slice(hh * HP, (hh + 1) * HP)
            dqp_ref[:, sl] = _rope_t(dq_ref[:, sl], c_, a_, b_).astype(dqp_ref.dtype)
            dkh = dk_ref[:, sl]
            dkv_ref[:, sl] = dkh.astype(dkv_ref.dtype)
            dk_sum = dk_sum + dkh
        dkv_ref[:, HW:2 * HW] = dv_ref[...]
        dkr_ref[...] = _rope_t(dk_sum, c_, a_, b_)

    dq_pre, dkv_pre, d_kr = rowwise("qkv_rope_bwd" + tag, qkv_bwd_body, s_len,
                                    [_whole(dq), _whole(dk), _whole(dv), _whole(cc), _whole(sa), _whole(sb)], [],
                                    [(HW, BF16), (2 * HW, BF16), (HP, F32)])
    g["w_uq"] = mm(sv["qn"], dq_pre, ta=True, name="q_proj_dw" + tag)
    d_qn = mm(dq_pre, w["w_uq"], tb=True, name="q_proj_dx" + tag)
    g["w_ukv"] = mm(sv["ckvn"], dkv_pre, ta=True, name="kv_proj_dw" + tag)
    d_ckvn = mm(dkv_pre, w["w_ukv"], tb=True, name="kv_proj_dx" + tag)

    def prep_bwd_body(z_ref, dqn_ref, dckv_ref, dkr_ref, qg_ref, kg_ref, dz_ref, dqg_ref, dkg_ref):
        _, vq = jax.vjp(_rms, z_ref[:, 0:Q_RANK], qg_ref[...])
        dcq, dqg = vq(dqn_ref[...])
        _, vk = jax.vjp(_rms, z_ref[:, Q_RANK:Q_RANK + KV_RANK], kg_ref[...])
        dckv, dkg = vk(dckv_ref[...])
        dz_ref[:, 0:Q_RANK] = dcq.astype(dz_ref.dtype)
        dz_ref[:, Q_RANK:Q_RANK + KV_RANK] = dckv.astype(dz_ref.dtype)
        dz_ref[:, Q_RANK + KV_RANK:ZA_W] = dkr_ref[...].astype(dz_ref.dtype)
        _acc(dqg_ref, dqg)
        _acc(dkg_ref, dkg)

    dz, g["q_norm"], g["kv_norm"] = rowwise("attn_prep_bwd" + tag, prep_bwd_body, s_len,
                                            [(sv["z"], ZA_W, ZA // ZA_W), _whole(d_qn), _whole(d_ckvn), _whole(d_kr)],
                                            [_row(w["q_norm"]), _row(w["kv_norm"])], [(ZA_W, BF16, ZW, ZA // ZA_W)],
                                            [((1, Q_RANK), F32), ((1, KV_RANK), F32)], into={0: dz})

    g["w_in"] = mm(sv["h"], dz, ta=True, name="in_proj_dw" + tag)
    if tail is None:
        d_h = mm(dz, w["w_in"], tb=True, name="in_proj_dx" + tag)
    else:
        d_h, tailed = mm(dz, w["w_in"], tb=True, name="in_proj_dx" + tag, carry=tail(g))
        carried = [carried, tailed]
    dx, g["mix_norm_pre"] = rowwise(
        "mix_norm_pre_bwd" + tag, _pre_bwd_body, s_len, [_whole(sv["x"]), _whole(d_h), _whole(dh1)], [_row(w["mix_norm_pre"])], [(D_MODEL, F32)], [((1, D_MODEL), F32)])
    return dx, g, carried


def loss_head(h, target):
    s_len = h.shape[0]

    def body(h_ref, t_ref, dy_ref, loss_ref):
        err = h_ref[...] - t_ref[...]
        dy_ref[...] = err * (1.0 / D_MODEL)
        part = 0.5 * jnp.sum(jnp.mean(err * err, axis=-1, keepdims=True), axis=0, keepdims=True)
        _acc(loss_ref, jnp.broadcast_to(part, (1, LANE)))

    return rowwise("loss_head", body, s_len, [_whole(h), _whole(target)], [], [(D_MODEL, F32)], [((1, LANE), F32)])


def local_step(x, pos_col, target, layers):
    s_len = x.shape[0]
    tabs = rope_tables(pos_col, s_len)
    h, saved = x, []
    for li, w in enumerate(layers):
        h, sv_mix, _ = mixer_fwd(h, tabs, w, f"_l{li}")
        h, sv_ffn = ffn_fwd(h, w, f"_l{li}")
        saved.append((sv_mix, sv_ffn))
    dh, loss = loss_head(h, target)
    grads = [None] * len(layers)
    for li in reversed(range(len(layers))):
        dh, g_ffn = ffn_bwd(dh, saved[li][1], layers[li], f"_l{li}")
        dh, g_mix, _ = mixer_bwd(dh, saved[li][0], tabs, layers[li], f"_l{li}")
        grads[li] = {**g_mix, **g_ffn}
    return loss[0, 0], dh, grads


def _pad_heads_cols(wm, per_head):
    r = wm.shape[0]
    return jnp.pad(wm.reshape(r, N_HEADS, per_head), ((0, 0), (0, 0), (0, HP - per_head))).reshape(r, HW)


def _unpad_heads_cols(wm, per_head):
    r = wm.shape[0]
    return wm.reshape(r, N_HEADS, HP)[:, :, :per_head].reshape(r, N_HEADS * per_head)


def align_weights(p):
    out = dict(p)
    if "w_in" in p:
        w_in = p["w_in"]
        r = w_in.shape[0]
        zeros = lambda n: jnp.zeros((r, n), w_in.dtype)
        conv = w_in[:, O_CONV:O_POOL].reshape(r, 2, CONV_C // LANE, LANE).transpose(0, 2, 1, 3).reshape(r, 2 * CONV_C)
        out["w_in"] = jnp.concatenate([
            w_in[:, O_GATE:D_IN], conv, w_in[:, O_POOL:O_GATE], w_in[:, O_Q:O_KR],
            zeros(KR_LANE), w_in[:, O_KR:O_CONV], zeros(HP - KR_LANE - ROPE)], axis=1)
    if "w_uq" in p:
        out["w_uq"] = _pad_heads_cols(p["w_uq"], NOPE + ROPE)
        out["w_ukv"] = jnp.concatenate([_pad_heads_cols(p["w_uk"], NOPE), _pad_heads_cols(p["w_uv"], VDIM)], axis=1)
        wo = p["w_attn_o"]
        out["w_attn_o"] = jnp.pad(wo.reshape(N_HEADS, VDIM, D_MODEL), ((0, 0), (0, HP - VDIM), (0, 0))).reshape(HW, D_MODEL)
        del out["w_uk"], out["w_uv"]
    if "w_gate" in p:
        out["w_gu"] = jnp.concatenate([p["w_gate"][:, :FF_HALF], p["w_up"][:, :FF_HALF], p["w_gate"][:, FF_HALF:], p["w_up"][:, FF_HALF:]], axis=1)
        del out["w_gate"], out["w_up"]
    return out


def unalign_grads(g):
    out = dict(g)
    if "w_in" in g:
        gi = g["w_in"]
        kr0 = ZA + Q_RANK + KV_RANK + KR_LANE
        r = gi.shape[0]
        conv = gi[:, ZC:ZP].reshape(r, CONV_C // LANE, 2, LANE).transpose(0, 2, 1, 3).reshape(r, 2 * CONV_C)
        out["w_in"] = jnp.concatenate([gi[:, ZA:ZA + Q_RANK + KV_RANK], gi[:, kr0:kr0 + ROPE], conv, gi[:, ZP:ZA], gi[:, ZG:ZC]], axis=1)
        out["w_uq"] = _unpad_heads_cols(g["w_uq"], NOPE + ROPE)
        out["w_uk"] = _unpad_heads_cols(g["w_ukv"][:, :HW], NOPE)
        out["w_uv"] = _unpad_heads_cols(g["w_ukv"][:, HW:], VDIM)
        out["w_attn_o"] = g["w_attn_o"].reshape(N_HEADS, HP, D_MODEL)[:, :VDIM].reshape(N_HEADS * VDIM, D_MODEL)
        del out["w_ukv"]
    if "w_gu" in g:
        gu = g["w_gu"]
        out["w_gate"] = jnp.concatenate([gu[:, 0:FF_HALF], gu[:, 2 * FF_HALF:3 * FF_HALF]], axis=1)
        out["w_up"] = jnp.concatenate([gu[:, FF_HALF:2 * FF_HALF], gu[:, 3 * FF_HALF:]], axis=1)
        del out["w_gu"]
    return out


MESH = pl.DeviceIdType.MESH
ANY = pl.BlockSpec(memory_space=pl.ANY)


def _place():
    return lax.axis_index("x"), lax.axis_index("y"), lax.axis_index("c")


def _other_chips(x, y):
    return [(1 - x, y), (x, 1 - y), (1 - x, 1 - y)]


def _half_rows(rows, c):
    assert rows % (2 * HALF_ALIGN) == 0, rows
    return pl.ds(pl.multiple_of(c * (rows // 2), HALF_ALIGN), rows // 2)


class GatherShards:
    def __init__(self, local):
        self.ins = list(local)
        self.outs = [jax.ShapeDtypeStruct((N_CHIPS, *a.shape), a.dtype) for a in local]
        self.n_sems = 6 * len(local)
        self.base = 0

    def _first(self, in_refs, out_refs, send_sems, recv_sems):
        x, y, c = _place()
        me = 2 * x + y
        chips = _other_chips(x, y)

        def copy(i, k, slot, core, to, src=None):
            dst = out_refs[i].at[slot, _half_rows(out_refs[i].shape[1], core)]
            return pltpu.make_async_remote_copy(src_ref=dst if src is None else src, dst_ref=dst, send_sem=send_sems.at[self.base + 6 * i + k],
                                                recv_sem=recv_sems.at[self.base + 6 * i + k], device_id=to, device_id_type=MESH)

        first = [copy(i, j, me, c, (*chip, c), src=in_refs[i].at[_half_rows(in_refs[i].shape[0], c)])
                 for i in range(len(in_refs)) for j, chip in enumerate(chips)]
        return first, copy

    def start(self, in_refs, out_refs, send_sems, recv_sems):
        first, _ = self._first(in_refs, out_refs, send_sems, recv_sems)
        for cp in first:
            cp.start()

    def finish(self, in_refs, out_refs, send_sems, recv_sems):
        first, copy = self._first(in_refs, out_refs, send_sems, recv_sems)
        x, y, c = _place()
        slots = [2 * cx + cy for cx, cy in _other_chips(x, y)]
        sibling = (x, y, 1 - c)
        passed = []
        for i in range(len(in_refs)):
            for j in range(3):
                copy(i, j, slots[j], c, sibling).wait_recv()
                fwd = copy(i, 3 + j, slots[j], c, sibling)
                fwd.start()
                passed.append(fwd)
        for i in range(len(in_refs)):
            for j in range(3):
                copy(i, 3 + j, slots[j], 1 - c, sibling).wait_recv()
        for cp in first + passed:
            cp.wait_send()


class ChipExchange:
    def __init__(self, parts):
        self.ins = list(parts)
        self.outs = [jax.ShapeDtypeStruct((3, *a.shape[1:]), a.dtype) for a in parts]
        self.n_sems = 3 * len(parts)
        self.base = 0

    def _copies(self, in_refs, out_refs, send_sems, recv_sems):
        x, y, c = _place()
        return [pltpu.make_async_remote_copy(src_ref=in_refs[i].at[2 * chip[0] + chip[1]], dst_ref=out_refs[i].at[j],
                                             send_sem=send_sems.at[self.base + 3 * i + j], recv_sem=recv_sems.at[self.base + 3 * i + j],
                                             device_id=(*chip, c), device_id_type=MESH)
                for i in range(len(in_refs)) for j, chip in enumerate(_other_chips(x, y))]

    def start(self, in_refs, out_refs, send_sems, recv_sems):
        for cp in self._copies(in_refs, out_refs, send_sems, recv_sems):
            cp.start()

    def finish(self, in_refs, out_refs, send_sems, recv_sems):
        copies = self._copies(in_refs, out_refs, send_sems, recv_sems)
        for cp in copies:
            cp.wait_recv()
        for cp in copies:
            cp.wait_send()


def run_exchange(ex, name):
    n_in, n_out = len(ex.ins), len(ex.outs)

    def body(*refs):
        ins, outs, sems = refs[:n_in], refs[n_in:n_in + n_out], refs[n_in + n_out:]
        ex.start(ins, outs, *sems)
        ex.finish(ins, outs, *sems)

    return pl.pallas_call(
        body,
        in_specs=[ANY] * n_in,
        out_specs=[ANY] * n_out,
        out_shape=list(ex.outs),
        scratch_shapes=[pltpu.SemaphoreType.DMA((ex.n_sems,)), pltpu.SemaphoreType.DMA((ex.n_sems,))],
        name=name,
    )(*ex.ins)


def sibling_swap(gs, name):
    n = len(gs)

    def body(*refs):
        g_refs, out_refs, (send_sems, recv_sems) = refs[:n], refs[n:2 * n], refs[2 * n:]
        x, y, c = _place()
        copies = []
        for i in range(n):
            for j in range(N_CHIPS):
                cp = pltpu.make_async_remote_copy(src_ref=g_refs[i].at[j, _half_rows(g_refs[i].shape[1], 1 - c)], dst_ref=out_refs[i].at[j],
                                                  send_sem=send_sems.at[4 * i + j], recv_sem=recv_sems.at[4 * i + j],
                                                  device_id=(x, y, 1 - c), device_id_type=MESH)
                cp.start()
                copies.append(cp)
        for cp in copies:
            cp.wait_recv()
        for cp in copies:
            cp.wait_send()

    return pl.pallas_call(
        body,
        in_specs=[ANY] * n,
        out_specs=[ANY] * n,
        out_shape=[jax.ShapeDtypeStruct((N_CHIPS, a.shape[1] // 2, a.shape[2]), a.dtype) for a in gs],
        scratch_shapes=[pltpu.SemaphoreType.DMA((4 * n,)), pltpu.SemaphoreType.DMA((4 * n,))],
        name=name,
    )(*gs)


def sibling_gather(fs):
    n = len(fs)
    layers = fs[0].shape[0]

    def body(*refs):
        out_refs, (send_sems, recv_sems) = refs[n:2 * n], refs[2 * n:]
        x, y, c = _place()

        def copy(i, l, core):
            part = out_refs[i].at[l, _half_rows(out_refs[i].shape[1], core)]
            return pltpu.make_async_remote_copy(src_ref=part, dst_ref=part, send_sem=send_sems.at[layers * i + l],
                                                recv_sem=recv_sems.at[layers * i + l], device_id=(x, y, 1 - c), device_id_type=MESH)

        sends = [copy(i, l, c) for i in range(n) for l in range(layers)]
        for cp in sends:
            cp.start()
        for i in range(n):
            for l in range(layers):
                copy(i, l, 1 - c).wait_recv()
        for cp in sends:
            cp.wait_send()

    return pl.pallas_call(
        body,
        in_specs=[ANY] * n,
        out_specs=[ANY] * n,
        out_shape=[jax.ShapeDtypeStruct(a.shape, a.dtype) for a in fs],
        scratch_shapes=[pltpu.SemaphoreType.DMA((layers * n,)), pltpu.SemaphoreType.DMA((layers * n,))],
        input_output_aliases={i: i for i in range(n)},
        name="sibling_gather",
    )(*fs)


class GatherAll:
    def __init__(self, vs):
        self.ins = list(vs)
        self.outs = [jax.ShapeDtypeStruct((8, *a.shape), a.dtype) for a in vs]
        self.n_sems = 7 * len(vs)
        self.base = 0

    def _first(self, in_refs, out_refs, send_sems, recv_sems):
        x, y, c = _place()
        me, sibling = (x, y, c), (x, y, 1 - c)

        def copy(i, k, block, to, src=None):
            px, py, pc = block
            dst = out_refs[i].at[4 * px + 2 * py + pc]
            return pltpu.make_async_remote_copy(src_ref=dst if src is None else src, dst_ref=dst, send_sem=send_sems.at[self.base + 7 * i + k],
                                                recv_sem=recv_sems.at[self.base + 7 * i + k], device_id=to, device_id_type=MESH)

        first = []
        for i in range(len(in_refs)):
            first.append(copy(i, 0, me, sibling, src=in_refs[i]))
            first += [copy(i, 1 + j, me, (*chip, c), src=in_refs[i]) for j, chip in enumerate(_other_chips(x, y))]
        return first, copy

    def start(self, in_refs, out_refs, send_sems, recv_sems):
        first, _ = self._first(in_refs, out_refs, send_sems, recv_sems)
        for cp in first:
            cp.start()

    def finish(self, in_refs, out_refs, send_sems, recv_sems):
        first, copy = self._first(in_refs, out_refs, send_sems, recv_sems)
        x, y, c = _place()
        me, sibling = (x, y, c), (x, y, 1 - c)
        chips = _other_chips(x, y)
        passed = []
        for i in range(len(in_refs)):
            for j, chip in enumerate(chips):
                copy(i, 1 + j, (*chip, c), me).wait_recv()
                fwd = copy(i, 4 + j, (*chip, c), sibling)
                fwd.start()
                passed.append(fwd)
        for i in range(len(in_refs)):
            copy(i, 0, sibling, me).wait_recv()
            for j, chip in enumerate(chips):
                copy(i, 4 + j, (*chip, 1 - c), me).wait_recv()
        for cp in first + passed:
            cp.wait_send()


def _row_tile(rows, row_bytes):
    best = None
    for t in range(16, rows + 1, 16):
        if rows % t == 0 and t * row_bytes <= SUM_TILE_BYTES:
            best = t
    return best or rows


def sibling_sum(g, theirs, place, name):
    _, half, cols = theirs.shape
    tile = _row_tile(half, cols * 4)
    nt = half // tile

    def body(place_ref, g_ref, t_ref, o_ref):
        o_ref[...] = (g_ref[...].astype(F32) + t_ref[...].astype(F32)).astype(o_ref.dtype)

    spec = pl.BlockSpec((1, tile, cols), lambda j, i, place_ref: (j, i, 0))
    return pl.pallas_call(
        body,
        grid_spec=pltpu.PrefetchScalarGridSpec(
            num_scalar_prefetch=1, grid=(N_CHIPS, nt),
            in_specs=[pl.BlockSpec((1, tile, cols), lambda j, i, place_ref: (j, place_ref[1] * nt + i, 0)), spec], out_specs=spec),
        out_shape=jax.ShapeDtypeStruct(theirs.shape, BF16),
        compiler_params=_cparams(("parallel", "parallel")),
        name=name,
    )(place, g, theirs)


def chip_sum(p, others, place, layer, into, name):
    _, half, cols = p.shape
    tile = _row_tile(half, cols * 4)
    nt = half // tile

    def body(place_ref, p_ref, o3_ref, *rest):
        o_ref = rest[-1]
        acc = p_ref[0].astype(F32)
        for k in range(3):
            acc = acc + o3_ref[k].astype(F32)
        o_ref[0] = acc

    return pl.pallas_call(
        body,
        grid_spec=pltpu.PrefetchScalarGridSpec(
            num_scalar_prefetch=1, grid=(nt,),
            in_specs=[pl.BlockSpec((1, tile, cols), lambda i, place_ref: (place_ref[0], i, 0)),
                      pl.BlockSpec((3, tile, cols), lambda i, place_ref: (0, i, 0))] + ([] if into is None else [ANY]),
            out_specs=pl.BlockSpec((1, tile, cols), lambda i, place_ref: (layer, place_ref[1] * nt + i, 0))),
        out_shape=jax.ShapeDtypeStruct((N_LAYERS, 2 * half, cols), F32),
        input_output_aliases={} if into is None else {3: 0},
        compiler_params=_cparams(("parallel",)),
        name=name,
    )(place, p, others, *([] if into is None else [into]))


def sum_devices(a, name):
    n, rows, cols = a.shape
    tile = _row_tile(rows, cols * 4 * n)

    def body(a_ref, o_ref):
        acc = a_ref[0]
        for s in range(1, n):
            acc = acc + a_ref[s]
        o_ref[...] = acc

    return pl.pallas_call(body, grid=(rows // tile,), in_specs=[pl.BlockSpec((n, tile, cols), lambda i: (0, i, 0))],
                          out_specs=pl.BlockSpec((tile, cols), lambda i: (i, 0)), out_shape=jax.ShapeDtypeStruct((rows, cols), F32),
                          compiler_params=_cparams(("parallel",)), name=name)(a)


def adamw(w, g, m, v, name):
    layers, rows, cols = w.shape
    tile = rows
    for t in range(8, rows, 8):
        if rows % t == 0 and t * cols * 4 <= 2 * SUM_TILE_BYTES:
            tile = t

    def body(w_ref, g_ref, m_ref, v_ref, d_ref, mo_ref, vo_ref):
        gg = g_ref[...]
        m_new = ADAM_B1 * m_ref[...] + (1.0 - ADAM_B1) * gg
        v_new = ADAM_B2 * v_ref[...] + (1.0 - ADAM_B2) * (gg * gg)
        m_hat = m_new / (1.0 - ADAM_B1 ** ADAM_STEP)
        v_hat = v_new / (1.0 - ADAM_B2 ** ADAM_STEP)
        d_ref[...] = -ADAM_LR * (m_hat / (jnp.sqrt(v_hat) + ADAM_EPS) + ADAM_WD * w_ref[...])
        mo_ref[...] = m_new
        vo_ref[...] = v_new

    spec = pl.BlockSpec((1, tile, cols), lambda l, i: (l, i, 0))
    shape = jax.ShapeDtypeStruct((layers, rows, cols), F32)
    return pl.pallas_call(body, grid=(layers, rows // tile), in_specs=[spec] * 4, out_specs=[spec] * 3, out_shape=[shape] * 3,
                          compiler_params=_cparams(("parallel", "parallel")), name=name)(w, g, m, v)


WEIGHTS = ["mix_norm_pre", "w_in", "q_norm", "w_uq", "kv_norm", "w_uk", "w_uv", "w_attn_o", "conv_w", "conv_b", "conv_ln_g", "conv_ln_b",
           "w_conv_o", "pool_w", "pool_scale", "w_pool_o", "w_mix_o", "mix_norm_post", "ffn_norm_pre", "w_gate", "w_up", "w_down",
           "ffn_norm_post"]
SHARDED = {"w_in": 2, "w_uq": 2, "w_uk": 2, "w_uv": 2, "w_attn_o": 2, "conv_w": 2, "w_conv_o": 2, "w_pool_o": 2, "w_mix_o": 1,
           "w_gate": 2, "w_up": 2, "w_down": 1}
REPLICATED = [n for n in WEIGHTS if n not in SHARDED]
ROW_PARAMS = [n for n in REPLICATED if n != "pool_w"]
ROWS_MINOR = ("w_in", "w_uq", "w_gate", "w_up", "conv_w")
N_CHIPS = 4
N_MIX_GROUPS = 6
MIX_MATRICES = ("w_in", "w_uq", "w_ukv", "w_attn_o", "w_conv_o", "w_pool_o", "conv_w", "w_mix_o")
CONV_WIRE_ROWS = 32
GROUPS = [(("w_in",), 1), (("w_uq",), 1), (("w_uk", "w_uv"), 1), (("w_attn_o", "w_conv_o", "w_pool_o"), 1), (("conv_w",), 1),
          (("w_mix_o",), 1), (("w_gate", "w_up"), 2), (("w_down",), 1)]


def _join(parts, axis):
    return parts[0] if len(parts) == 1 else jnp.concatenate(parts, axis=axis)


def _split_group(arr, names, axis, shapes):
    out, off = {}, 0
    ax = arr.ndim - 3 + axis
    for n in names:
        size = shapes[n][axis]
        out[n] = lax.slice_in_dim(arr, off, off + size, axis=ax)
        off += size
    return out


def _pack_rows(vectors):
    blocks = []
    for v in vectors:
        for li in range(v.shape[0]):
            blocks.append(jnp.pad(v[li][None, :], ((0, PACK_ROWS - 1), (0, PACK_W - v.shape[1]))))
    return jnp.concatenate(blocks, axis=0)


def _unpack_rows(packed, shapes):
    out, r = [], 0
    for layers, width in shapes:
        out.append(jnp.stack([packed[r + PACK_ROWS * li, :width] for li in range(layers)]))
        r += PACK_ROWS * layers
    return out


def kernel(x, positions, mix_norm_pre, w_in, q_norm, w_uq, kv_norm, w_uk, w_uv, w_attn_o, conv_w, conv_b, conv_ln_g, conv_ln_b, w_conv_o, pool_w, pool_scale, w_pool_o, w_mix_o, mix_norm_post, ffn_norm_pre, w_gate, w_up, w_down, ffn_norm_post, loss_target, m_mix_norm_pre, m_w_in, m_q_norm, m_w_uq, m_kv_norm, m_w_uk, m_w_uv, m_w_attn_o, m_conv_w, m_conv_b, m_conv_ln_g, m_conv_ln_b, m_w_conv_o, m_pool_w, m_pool_scale, m_w_pool_o, m_w_mix_o, m_mix_norm_post, m_ffn_norm_pre, m_w_gate, m_w_up, m_w_down, m_ffn_norm_post, v_mix_norm_pre, v_w_in, v_q_norm, v_w_uq, v_kv_norm, v_w_uk, v_w_uv, v_w_attn_o, v_conv_w, v_conv_b, v_conv_ln_g, v_conv_ln_b, v_w_conv_o, v_pool_w, v_pool_scale, v_w_pool_o, v_w_mix_o, v_mix_norm_post, v_ffn_norm_pre, v_w_gate, v_w_up, v_w_down, v_ffn_norm_post):
    given = dict(mix_norm_pre=mix_norm_pre, w_in=w_in, q_norm=q_norm, w_uq=w_uq, kv_norm=kv_norm, w_uk=w_uk, w_uv=w_uv, w_attn_o=w_attn_o,
                 conv_w=conv_w, conv_b=conv_b, conv_ln_g=conv_ln_g, conv_ln_b=conv_ln_b, w_conv_o=w_conv_o, pool_w=pool_w,
                 pool_scale=pool_scale, w_pool_o=w_pool_o, w_mix_o=w_mix_o, mix_norm_post=mix_norm_post, ffn_norm_pre=ffn_norm_pre,
                 w_gate=w_gate, w_up=w_up, w_down=w_down, ffn_norm_post=ffn_norm_post)
    mom = dict(mix_norm_pre=m_mix_norm_pre, w_in=m_w_in, q_norm=m_q_norm, w_uq=m_w_uq, kv_norm=m_kv_norm, w_uk=m_w_uk, w_uv=m_w_uv,
               w_attn_o=m_w_attn_o, conv_w=m_conv_w, conv_b=m_conv_b, conv_ln_g=m_conv_ln_g, conv_ln_b=m_conv_ln_b, w_conv_o=m_w_conv_o,
               pool_w=m_pool_w, pool_scale=m_pool_scale, w_pool_o=m_w_pool_o, w_mix_o=m_w_mix_o, mix_norm_post=m_mix_norm_post,
               ffn_norm_pre=m_ffn_norm_pre, w_gate=m_w_gate, w_up=m_w_up, w_down=m_w_down, ffn_norm_post=m_ffn_norm_post)
    var = dict(mix_norm_pre=v_mix_norm_pre, w_in=v_w_in, q_norm=v_q_norm, w_uq=v_w_uq, kv_norm=v_kv_norm, w_uk=v_w_uk, w_uv=v_w_uv,
               w_attn_o=v_w_attn_o, conv_w=v_conv_w, conv_b=v_conv_b, conv_ln_g=v_conv_ln_g, conv_ln_b=v_conv_ln_b, w_conv_o=v_w_conv_o,
               pool_w=v_pool_w, pool_scale=v_pool_scale, w_pool_o=v_w_pool_o, w_mix_o=v_w_mix_o, mix_norm_post=v_mix_norm_post,
               ffn_norm_pre=v_ffn_norm_pre, w_gate=v_w_gate, w_up=v_w_up, w_down=v_w_down, ffn_norm_post=v_ffn_norm_post)
    s_len = x.shape[1]
    sharded_names = [n for n in WEIGHTS if n in SHARDED]
    chip = 2 * lax.axis_index("x") + lax.axis_index("y")
    place = jnp.stack([chip, lax.axis_index("c")]).astype(jnp.int32)
    shard_shape = {n: given[n].shape for n in sharded_names}

    mix_groups, ffn_groups = GROUPS[:N_MIX_GROUPS], GROUPS[N_MIX_GROUPS:]
    weight_wire_shape = {n: (N_LAYERS, 2 * CONV_WIRE_ROWS, shard_shape[n][2]) if n == "conv_w" else shard_shape[n] for n in sharded_names}
    grad_wire_shape = {n: (N_LAYERS, CONV_WIRE_ROWS, shard_shape[n][2]) if n == "conv_w" else shard_shape[n] for n in sharded_names}
    pad_rows = lambda a: jnp.pad(a, ((0, CONV_WIRE_ROWS - CONV_W), (0, 0)))

    def weight_wires(groups, li):
        def wire(name):
            a = given[name][li]
            if name == "conv_w":
                hi = a.astype(BF16)
                return jnp.concatenate([pad_rows(hi), pad_rows((a - hi.astype(F32)).astype(BF16))], axis=0)
            return a.astype(BF16)

        return [_join([wire(n) for n in names], axis - 1) for names, axis in groups]

    def full_weights(groups, li, local, gathered):
        p = {n: given[n][li] for n in REPLICATED}
        p["pool_w"] = p["pool_w"].astype(BF16)
        for (names, axis), loc, got in zip(groups, local, gathered):
            got = lax.dynamic_update_slice(got, loc[None], (chip, 0, 0))
            per_chip = [_split_group(got[j], names, axis, weight_wire_shape) for j in range(N_CHIPS)]
            for n in names:
                parts = [pc[n] for pc in per_chip]
                if n == "conv_w":
                    parts = [q[:CONV_W].astype(F32) + q[CONV_WIRE_ROWS:CONV_WIRE_ROWS + CONV_W].astype(F32) for q in parts]
                p[n] = jnp.concatenate(parts, axis=SHARDED[n] - 1)
        return align_weights(p)

    def chip_partials(groups, g, tag):
        wires = []
        for names, axis in groups:
            split = {n: jnp.split(pad_rows(g[n]) if n == "conv_w" else g[n], N_CHIPS, axis=SHARDED[n] - 1) for n in names}
            wires.append(jnp.stack([_join([split[n][j].astype(BF16) for n in names], axis - 1) for j in range(N_CHIPS)]))
        theirs = sibling_swap(wires, "sibling_swap_" + tag)
        return [sibling_sum(w, t, place, f"sibling_sum_{tag}_{i}") for i, (w, t) in enumerate(zip(wires, theirs))]

    tabs = rope_tables(positions.reshape(s_len, 1), s_len)
    in_groups, rest_groups = mix_groups[:1], mix_groups[1:]
    loc_in0, loc_rest0 = weight_wires(in_groups, 0), weight_wires(rest_groups, 0)
    w_in0 = full_weights(in_groups, 0, loc_in0, run_exchange(GatherShards(loc_in0), "gather_w_in_l0"))
    head = (GatherShards(loc_rest0), lambda arrived: full_weights(rest_groups, 0, loc_rest0, arrived))
    loc_f0, loc_m1 = weight_wires(ffn_groups, 0), weight_wires(mix_groups, 1)
    h, sv_m0, got = mixer_fwd(x[0], tabs, w_in0, "_l0", GatherShards(loc_f0 + loc_m1), head)
    w_m0 = sv_m0["w"]
    w_f0 = full_weights(ffn_groups, 0, loc_f0, got[:len(loc_f0)])
    w_m1 = full_weights(mix_groups, 1, loc_m1, got[len(loc_f0):])
    h, sv_f0 = ffn_fwd(h, w_f0, "_l0")
    loc_f1 = weight_wires(ffn_groups, 1)
    h, sv_m1, got = mixer_fwd(h, tabs, w_m1, "_l1", GatherShards(loc_f1))
    w_f1 = full_weights(ffn_groups, 1, loc_f1, got)
    h, sv_f1 = ffn_fwd(h, w_f1, "_l1")
    dh, loss_local = loss_head(h, loss_target[0])
    loss = lax.psum(loss_local[0, 0], MESH_AXES)

    dh, g_f1 = ffn_bwd(dh, sv_f1, w_f1, "_l1")
    g_f1 = unalign_grads(g_f1)
    p_f1 = chip_partials(ffn_groups, g_f1, "ffn_l1")
    dh, g_m1, o_f1 = mixer_bwd(dh, sv_m1, tabs, w_m1, "_l1", ChipExchange(p_f1))
    g_m1 = unalign_grads(g_m1)
    p_m1 = chip_partials(mix_groups, g_m1, "mix_l1")
    dh, g_f0 = ffn_bwd(dh, sv_f0, w_f0, "_l0")
    g_f0 = unalign_grads(g_f0)
    p_f0 = chip_partials(ffn_groups, g_f0, "ffn_l0")
    last = {}

    def tail(g):
        last["p"] = chip_partials(mix_groups, unalign_grads({n: g[n] for n in MIX_MATRICES}), "mix_l0")
        return ChipExchange(last["p"])

    grad_x, g_m0, (got, o_m0) = mixer_bwd(dh, sv_m0, tabs, w_m0, "_l0", ChipExchange(p_m1 + p_f0), tail)
    o_m1, o_f0 = got[:len(p_m1)], got[len(p_m1):]
    p_m0 = last["p"]
    grads = [{**g_m0, **g_f0}, {**g_m1, **g_f1}]
    grad_full = {n: jnp.stack([g[n].reshape(given[n].shape[1:]) for g in grads]) for n in REPLICATED}
    replicated = [_pack_rows([grad_full[n] for n in ROW_PARAMS]), grad_full["pool_w"].reshape(-1, POOL_GD)]
    device = 2 * chip + lax.axis_index("c")
    rows_all, pool_w_all = [lax.dynamic_update_slice(a, mine[None], (device, 0, 0))
                            for a, mine in zip(run_exchange(GatherAll(replicated), "gather_replicated"), replicated)]

    sums = {}
    for groups, base, per_layer in ((ffn_groups, N_MIX_GROUPS, ((1, p_f1, o_f1), (0, p_f0, o_f0))), (mix_groups, 0, ((1, p_m1, o_m1), (0, p_m0, o_m0)))):
        for li, parts, others in per_layer:
            for i, (p, o) in enumerate(zip(parts, others)):
                sums[base + i] = chip_sum(p, o, place, li, sums.get(base + i), f"chip_sum_{base + i}_l{li}")
    g_shard = {}
    for (names, axis), s in zip(GROUPS, sibling_gather([sums[i] for i in range(len(GROUPS))])):
        g_shard.update(_split_group(s, names, axis, grad_wire_shape))
    g_shard["conv_w"] = g_shard["conv_w"][:, :CONV_W]

    row_shapes = [given[n].shape for n in ROW_PARAMS]
    g_rows = sum_devices(rows_all, "row_params_sum")
    g_pool_w = sum_devices(pool_w_all, "pool_w_sum")
    g_rep = dict(zip(ROW_PARAMS, _unpack_rows(g_rows, row_shapes)))
    g_rep["pool_w"] = g_pool_w.reshape(given["pool_w"].shape)

    g_out, d_out, m_out, v_out = {}, {}, {}, {}
    for n in sharded_names + ["pool_w"]:
        shp = given[n].shape
        three_d = (shp[0], int(np.prod(shp[1:-1])), shp[-1])
        g_n = g_shard[n] if n in SHARDED else g_rep[n]
        view = (lambda a: jnp.swapaxes(a.reshape(three_d), 1, 2)) if n in ROWS_MINOR else (lambda a: a.reshape(three_d))
        back = (lambda a: jnp.swapaxes(a, 1, 2).reshape(shp)) if n in ROWS_MINOR else (lambda a: a.reshape(shp))
        d, mn, vn = adamw(view(given[n]), view(g_n), view(mom[n]), view(var[n]), "adamw_" + n)
        g_out[n], d_out[n], m_out[n], v_out[n] = g_n, back(d), back(mn), back(vn)
    rd, rm, rv = adamw(_pack_rows([given[n] for n in ROW_PARAMS])[None], g_rows[None], _pack_rows([mom[n] for n in ROW_PARAMS])[None],
                       _pack_rows([var[n] for n in ROW_PARAMS])[None], "adamw_row_params")
    for n, d, mn, vn in zip(ROW_PARAMS, *[_unpack_rows(a[0], row_shapes) for a in (rd, rm, rv)]):
        g_out[n], d_out[n], m_out[n], v_out[n] = g_rep[n], d, mn, vn

    return (loss, grad_x[None], *[g_out[n] for n in WEIGHTS], *[d_out[n] for n in WEIGHTS], *[m_out[n] for n in WEIGHTS],
            *[v_out[n] for n in WEIGHTS])
```

```python
import functools
import math

import numpy as np
import jax
import jax.numpy as jnp
from jax import lax
from jax.experimental import pallas as pl
from jax.experimental.pallas import tpu as pltpu

F32, BF16 = jnp.float32, jnp.bfloat16

D_MODEL = 1024
N_HEADS = 8
NOPE, ROPE, VDIM = 64, 32, 64
HALF_ROPE = ROPE // 2
Q_RANK, KV_RANK = 384, 256
CONV_C, CONV_W = 512, 31
POOL_C, POOL_G, POOL_GD = 512, 4, 128
POOL_WINDOWS = (2, 4, 8, 16)
D_FF = 2816
FF_HALF = D_FF // 2
N_LAYERS = 2
EPS = 1e-6
ROPE_THETA = 10000.0
ATT_SCALE = 1.0 / math.sqrt(NOPE + ROPE)
O_Q, O_KV, O_KR, O_CONV, O_POOL, O_GATE, D_IN = 0, 384, 640, 672, 1696, 2208, 5280

LANE = 128
HP = 128
ZG, ZC, ZP, ZA, ZW = 0, 3072, 4096, 4608, 5376
ZA_W = Q_RANK + KV_RANK + HP
KR_LANE = NOPE
HW = N_HEADS * HP

ADAM_LR, ADAM_B1, ADAM_B2, ADAM_EPS, ADAM_WD, ADAM_STEP = 0.001, 0.9, 0.999, 1e-08, 0.01, 10

ROW_TILE = 512
WIDE_ROW_TILE = 256
ATT_TILE_FWD = 1024
ATT_TILE_BWD = 512
ATT_HEADS = 4
CONV_CHUNK = 256
MM_TM, MM_TN, MM_TK = 1024, 1408, 1024
MM_TILE_MAX = 2048
FFN_TM = 512
MM_VMEM_BUDGET = 40 * 1024 * 1024
HBM_BYTES_PER_US = 3.0e6
GRID_STEP_US = 0.35
VMEM_LIMIT = 56 * 1024 * 1024
SUM_TILE_BYTES = 3 * 1024 * 1024
ADAM_TILE_BYTES = 2 * 1024 * 1024

HALF_ALIGN = 16
MESH_AXES = ("x", "y", "c")
PACK_W = 1024
PACK_ROWS = 8


def _cparams(sem):
    return pltpu.CompilerParams(dimension_semantics=sem, vmem_limit_bytes=VMEM_LIMIT)


def _tile(n, target):
    if n <= target:
        return n
    best = None
    for t in range(LANE, target + 1, LANE):
        if n % t == 0:
            best = t
    assert best is not None, (n, target)
    return best


def _mm_tiles(m, n, k, a_bytes, b_bytes, out_bytes):
    divs = lambda d: sorted({t for t in range(LANE, min(d, MM_TILE_MAX) + 1, LANE) if d % t == 0} | ({d} if d <= MM_TILE_MAX else set()))
    best = None
    for tm in divs(m):
        for tn in divs(n):
            blocks = tm * k * a_bytes + k * tn * b_bytes + tm * tn * out_bytes
            if 2 * blocks + tm * tn * 4 > MM_VMEM_BUDGET:
                continue
            steps = (m // tm) * (n // tn)
            for rows_outer in (True, False):
                moved = (m * k * a_bytes + k * n * b_bytes * (m // tm)) if rows_outer else (k * n * b_bytes + m * k * a_bytes * (n // tn))
                cost = (moved + m * n * out_bytes + blocks) / HBM_BYTES_PER_US + steps * GRID_STEP_US
                if best is None or cost < best[0]:
                    best = (cost, tm, tn, rows_outer)
    if best is not None:
        return best[1], best[2], k, best[3]
    return _tile(m, MM_TM), _tile(n, MM_TN), _tile(k, MM_TK), True


def mm(a, b, *, ta=False, tb=False, out_dtype=F32, name, carry=None):
    m, k = (a.shape[1], a.shape[0]) if ta else a.shape
    n, k2 = b.shape if tb else (b.shape[1], b.shape[0])
    assert k == k2, (a.shape, b.shape, ta, tb)
    tm, tn, tk, rows_outer = _mm_tiles(m, n, k, a.dtype.itemsize, b.dtype.itemsize, jnp.dtype(out_dtype).itemsize)
    nk = k // tk
    dims = (((0 if ta else 1,), (1 if tb else 0,)), ((), ()))
    grid = (m // tm, n // tn, nk) if rows_outer else (n // tn, m // tm, nk)

    def body(*refs):
        (a_ref, b_ref, o_ref, *acc), start, finish = _carried(carry, refs, 2, 1, 0 if nk == 1 else 1)
        ids = [pl.program_id(d) for d in range(3)]
        if start is not None:
            pl.when((ids[0] == 0) & (ids[1] == 0) & (ids[2] == 0))(start)
        part = lax.dot_general(a_ref[...].astype(BF16), b_ref[...].astype(BF16), dims, preferred_element_type=F32)
        if nk == 1:
            o_ref[...] = part.astype(o_ref.dtype)
        else:
            (acc_ref,) = acc
            kk = ids[2]

            @pl.when(kk == 0)
            def _():
                acc_ref[...] = part

            @pl.when(kk > 0)
            def _():
                acc_ref[...] += part

            @pl.when(kk == nk - 1)
            def _():
                o_ref[...] = acc_ref[...].astype(o_ref.dtype)
        if finish is not None:
            pl.when((ids[0] == grid[0] - 1) & (ids[1] == grid[1] - 1) & (ids[2] == nk - 1))(finish)

    ij = (lambda g0, g1: (g0, g1)) if rows_outer else (lambda g0, g1: (g1, g0))

    def a_map(g0, g1, kk):
        i, _ = ij(g0, g1)
        return (kk, i) if ta else (i, kk)

    def b_map(g0, g1, kk):
        _, j = ij(g0, g1)
        return (j, kk) if tb else (kk, j)

    ex_in_specs, ex_out_specs, ex_out_shape, ex_scratch, ex_inputs = _carry_specs(carry)
    a_spec = pl.BlockSpec((tk, tm) if ta else (tm, tk), a_map)
    b_spec = pl.BlockSpec((tn, tk) if tb else (tk, tn), b_map)
    out, *carried = pl.pallas_call(
        body,
        grid=grid,
        in_specs=[a_spec, b_spec] + ex_in_specs,
        out_specs=[pl.BlockSpec((tm, tn), lambda g0, g1, kk: ij(g0, g1))] + ex_out_specs,
        out_shape=[jax.ShapeDtypeStruct((m, n), out_dtype)] + ex_out_shape,
        scratch_shapes=([] if nk == 1 else [pltpu.VMEM((tm, tn), F32)]) + ex_scratch,
        compiler_params=_cparams(("arbitrary", "arbitrary", "arbitrary") if carry is not None else ("parallel", "parallel", "arbitrary")),
        name=name,
    )(a, b, *ex_inputs)
    return out if carry is None else (out, carried)


def ffn_in(hn, w_gu, name):
    s_len, k = hn.shape
    tm = min(FFN_TM, s_len)

    def body(a_ref, b_ref, gu_ref, act_ref):
        r = jnp.dot(a_ref[...], b_ref[...], preferred_element_type=F32)
        gu_ref[...] = r.astype(gu_ref.dtype)
        act_ref[...] = (_silu(r[:, :FF_HALF]) * r[:, FF_HALF:]).astype(act_ref.dtype)

    return pl.pallas_call(
        body,
        grid=(2, s_len // tm),
        in_specs=[pl.BlockSpec((tm, k), lambda j, i: (i, 0)), pl.BlockSpec((k, 2 * FF_HALF), lambda j, i: (0, j))],
        out_specs=[pl.BlockSpec((tm, 2 * FF_HALF), lambda j, i: (i, j)), pl.BlockSpec((tm, FF_HALF), lambda j, i: (i, j))],
        out_shape=[jax.ShapeDtypeStruct((s_len, 2 * D_FF), BF16), jax.ShapeDtypeStruct((s_len, D_FF), BF16)],
        compiler_params=_cparams(("arbitrary", "parallel")),
        name=name,
    )(hn, w_gu)


def ffn_out_dx(d_y, w_down, gu, name):
    s_len, k = d_y.shape
    tm = min(FFN_TM, s_len)

    def body(a_ref, b_ref, gu_ref, dgu_ref):
        da = lax.dot_general(a_ref[...], b_ref[...], NT_DIMS, preferred_element_type=F32)
        gt = gu_ref[:, :FF_HALF].astype(F32)
        up = gu_ref[:, FF_HALF:].astype(F32)
        sg = jax.nn.sigmoid(gt)
        dgu_ref[:, :FF_HALF] = (da * up * sg * (1.0 + gt * (1.0 - sg))).astype(dgu_ref.dtype)
        dgu_ref[:, FF_HALF:] = (da * gt * sg).astype(dgu_ref.dtype)

    pair = pl.BlockSpec((tm, 2 * FF_HALF), lambda j, i: (i, j))
    return pl.pallas_call(
        body,
        grid=(2, s_len // tm),
        in_specs=[pl.BlockSpec((tm, k), lambda j, i: (i, 0)), pl.BlockSpec((FF_HALF, k), lambda j, i: (j, 0)), pair],
        out_specs=pair,
        out_shape=jax.ShapeDtypeStruct((s_len, 2 * D_FF), BF16),
        compiler_params=_cparams(("parallel", "arbitrary")),
        name=name,
    )(d_y, w_down, gu)


def rowwise(name, body, rows, row_ins, full_ins, row_outs, acc_outs=(), into=None, tile=None):
    tile = min(tile or ROW_TILE, rows)
    into = into or {}
    in_specs = [pl.BlockSpec((tile, w), lambda i, cb=cb: (i, cb)) for _, w, cb in row_ins]
    in_specs += [pl.BlockSpec(a.shape, lambda i, nd=a.ndim: (0,) * nd) for a in full_ins]
    in_specs += [ANY for _ in into]
    n_in = len(row_ins) + len(full_ins)
    aliases = {n_in + k: oi for k, oi in enumerate(into)}
    out_specs, out_shape = [], []
    for ro in row_outs:
        w, dt, full_w, cb = ro if len(ro) == 4 else (*ro, ro[0], 0)
        out_specs.append(pl.BlockSpec((tile, w), lambda i, cb=cb: (i, cb)))
        out_shape.append(jax.ShapeDtypeStruct((rows, full_w), dt))
    out_specs += [pl.BlockSpec(s, lambda i, nd=len(s): (0,) * nd) for s, _ in acc_outs]
    out_shape += [jax.ShapeDtypeStruct(s, dt) for s, dt in acc_outs]
    n_refs = n_in

    def call_body(*refs):
        body(*refs[:n_refs], *refs[n_refs + len(into):])

    outs = pl.pallas_call(
        call_body,
        grid=(rows // tile,),
        in_specs=in_specs,
        out_specs=out_specs,
        out_shape=out_shape,
        input_output_aliases=aliases,
        compiler_params=_cparams(("arbitrary",)),
        name=name,
    )(*[a for a, _, _ in row_ins], *full_ins, *into.values())
    return outs


def _whole(a):
    return (a, a.shape[1], 0)


def _acc(ref, val):
    @pl.when(pl.program_id(0) == 0)
    def _():
        ref[...] = val

    @pl.when(pl.program_id(0) > 0)
    def _():
        ref[...] += val


def _rms(x, g):
    return x * lax.rsqrt(jnp.mean(x * x, axis=-1, keepdims=True) + EPS) * g


def _layer_norm(x, g, b):
    mu = jnp.mean(x, axis=-1, keepdims=True)
    xc = x - mu
    return xc * lax.rsqrt(jnp.mean(xc * xc, axis=-1, keepdims=True) + EPS) * g + b


def _silu(x):
    return x * jax.nn.sigmoid(x)


def _rope(x, cc, sa, sb):
    return x * cc + pltpu.roll(x, HALF_ROPE, 1) * sa + pltpu.roll(x, HP - HALF_ROPE, 1) * sb


def _rope_t(dy, cc, sa, sb):
    return dy * cc + pltpu.roll(dy * sa, HP - HALF_ROPE, 1) + pltpu.roll(dy * sb, HALF_ROPE, 1)


def rope_tables(pos_col, rows):
    lane = np.arange(HP)
    idx = np.where(lane < KR_LANE + HALF_ROPE, lane - KR_LANE, lane - KR_LANE - HALF_ROPE)
    in_rope = (lane >= KR_LANE) & (lane < KR_LANE + ROPE)
    inv_freq = (np.float32(ROPE_THETA) ** (-np.arange(0, ROPE, 2, dtype=np.float32) / np.float32(ROPE))).astype(np.float32)
    freq_row = np.where(in_rope, inv_freq[np.clip(idx, 0, HALF_ROPE - 1)], 0.0).astype(np.float32)[None, :]
    first = ((lane >= KR_LANE) & (lane < KR_LANE + HALF_ROPE)).astype(np.float32)[None, :]
    second = ((lane >= KR_LANE + HALF_ROPE) & (lane < KR_LANE + ROPE)).astype(np.float32)[None, :]

    def body(pos_ref, f_ref, a_ref, b_ref, cc_ref, sa_ref, sb_ref):
        ang = pos_ref[...].astype(F32) * f_ref[...]
        s = jnp.sin(ang)
        cc_ref[...] = jnp.cos(ang)
        sa_ref[...] = s * b_ref[...]
        sb_ref[...] = -s * a_ref[...]

    return rowwise("rope_tables", body, rows, [_whole(pos_col)], [jnp.asarray(freq_row), jnp.asarray(first), jnp.asarray(second)],
                   [(HP, F32)] * 3)


def _causal_mask(t):
    r = lax.broadcasted_iota(jnp.int32, (t, t), 0)
    c = lax.broadcasted_iota(jnp.int32, (t, t), 1)
    return r, c


NT_DIMS = (((1,), (1,)), ((), ()))


def _carried(carry, refs, n_in, n_out, n_scratch):
    if carry is None:
        return refs, None, None
    ni, no = len(carry.ins), len(carry.outs)
    own_in, ex_in = refs[:n_in], refs[n_in:n_in + ni]
    own_out, ex_out = refs[n_in + ni:n_in + ni + n_out], refs[n_in + ni + n_out:n_in + ni + n_out + no]
    scratch = refs[n_in + ni + n_out + no:]
    sems = scratch[n_scratch:]
    return (*own_in, *own_out, *scratch[:n_scratch]), (lambda: carry.start(ex_in, ex_out, *sems)), (lambda: carry.finish(ex_in, ex_out, *sems))


def _carry_specs(carry):
    if carry is None:
        return [], [], [], [], []
    sems = [pltpu.SemaphoreType.DMA((carry.n_sems,)), pltpu.SemaphoreType.DMA((carry.n_sems,))]
    return [ANY] * len(carry.ins), [ANY] * len(carry.outs), list(carry.outs), sems, list(carry.ins)


def attention_fwd(q, k, v, name, carry=None):
    s_len = q.shape[0]
    t = min(ATT_TILE_FWD, s_len)
    nb = s_len // t
    hb = ATT_HEADS
    w = hb * HP
    nh = N_HEADS // hb

    def body(*refs):
        (q_ref, k_ref, v_ref, o_ref, lse_ref, m_sc, acc_sc), start, finish = _carried(carry, refs, 3, 2, 2)
        qi = pl.program_id(1)
        if start is not None:
            pl.when((pl.program_id(0) == 0) & (qi == 0))(start)
        m_sc[...] = jnp.full_like(m_sc, -jnp.inf)
        acc_sc[...] = jnp.zeros_like(acc_sc)

        def block(j, masked):
            ks = pl.ds(pl.multiple_of(j * t, t), t)
            for hh in range(hb):
                ls = slice(hh * HP, (hh + 1) * HP)
                s = lax.dot_general(q_ref[:, ls], k_ref[ks, ls], NT_DIMS, preferred_element_type=F32) * ATT_SCALE
                if masked:
                    r, c = _causal_mask(t)
                    s = jnp.where(c <= r, s, -jnp.inf)
                m_old = m_sc[hh]
                m_new = jnp.maximum(m_old, jnp.max(s, axis=-1, keepdims=True))
                p = jnp.exp(s - m_new)
                acc_sc[hh] = jnp.exp(m_old - m_new) * acc_sc[hh] + jnp.dot(p.astype(BF16), v_ref[ks, ls], preferred_element_type=F32)
                m_sc[hh] = m_new

        def loop_body(j, carry):
            block(j, False)
            return carry

        lax.fori_loop(0, qi, loop_body, 0)
        block(qi, True)
        lane = lax.broadcasted_iota(jnp.int32, (t, HP), 1)
        for hh in range(hb):
            acc = acc_sc[hh]
            l = jnp.sum(jnp.where(lane == VDIM, acc, 0.0), axis=-1, keepdims=True)
            o_ref[:, hh * HP:(hh + 1) * HP] = jnp.where(lane < VDIM, acc / l, 0.0).astype(o_ref.dtype)
            lse_ref[hh] = m_sc[hh] + jnp.log(l)
        if finish is not None:
            pl.when((pl.program_id(0) == nh - 1) & (qi == nb - 1))(finish)

    ex_in_specs, ex_out_specs, ex_out_shape, ex_scratch, ex_inputs = _carry_specs(carry)
    resident = pl.BlockSpec((s_len, w), lambda h, qi: (0, h))
    o, lse, *carried = pl.pallas_call(
        body,
        grid=(nh, nb),
        in_specs=[pl.BlockSpec((t, w), lambda h, qi: (qi, h)), resident, resident] + ex_in_specs,
        out_specs=[pl.BlockSpec((t, w), lambda h, qi: (qi, h)), pl.BlockSpec((hb, t, 1), lambda h, qi: (h, qi, 0))] + ex_out_specs,
        out_shape=[jax.ShapeDtypeStruct((s_len, HW), BF16), jax.ShapeDtypeStruct((N_HEADS, s_len, 1), F32)] + ex_out_shape,
        scratch_shapes=[pltpu.VMEM((hb, t, 1), F32), pltpu.VMEM((hb, t, HP), F32)] + ex_scratch,
        compiler_params=_cparams(("arbitrary", "arbitrary")),
        name=name,
    )(q, k, v, *ex_inputs)
    return o, lse, carried


def attention_delta(do, o):
    s_len = do.shape[0]
    t = min(ROW_TILE, s_len)

    def body(do_ref, o_ref, d_ref):
        prod = do_ref[...].astype(F32) * o_ref[...].astype(F32)
        for h in range(N_HEADS):
            d_ref[h] = jnp.sum(prod[:, h * HP:(h + 1) * HP], axis=-1, keepdims=True)

    return pl.pallas_call(
        body,
        grid=(s_len // t,),
        in_specs=[pl.BlockSpec((t, HW), lambda i: (i, 0))] * 2,
        out_specs=pl.BlockSpec((N_HEADS, t, 1), lambda i: (0, i, 0)),
        out_shape=jax.ShapeDtypeStruct((N_HEADS, s_len, 1), F32),
        compiler_params=_cparams(("arbitrary",)),
        name="attention_delta",
    )(do, o)


TN_DIMS = (((0,), (0,)), ((), ()))


def attention_bwd(q, k, v, do, lse_row, delta_row, name, carry=None):
    s_len = q.shape[0]
    t = min(ATT_TILE_BWD, s_len)
    nb = s_len // t
    hb = ATT_HEADS
    w = hb * HP
    nh = N_HEADS // hb

    def body(*refs):
        (q_ref, k_ref, v_ref, do_ref, lse_ref, dl_ref, dq_ref, dk_ref, dv_ref, dk_sc, dv_sc), start, finish = _carried(carry, refs, 6, 3, 2)
        ki = pl.program_id(1)
        if start is not None:
            pl.when((pl.program_id(0) == 0) & (ki == 0))(start)

        @pl.when(ki == 0)
        def _():
            dq_ref[...] = jnp.zeros_like(dq_ref)

        dk_sc[...] = jnp.zeros_like(dk_sc)
        dv_sc[...] = jnp.zeros_like(dv_sc)

        def block(j, masked):
            qs = pl.ds(pl.multiple_of(j * t, t), t)
            for hh in range(hb):
                ls = slice(hh * HP, (hh + 1) * HP)
                qb = q_ref[qs, ls]
                dob = do_ref[qs, ls]
                kb = k_ref[:, ls]
                st = lax.dot_general(kb, qb, NT_DIMS, preferred_element_type=F32) * ATT_SCALE
                pt = jnp.exp(st - lse_ref[hh, j])
                if masked:
                    r, c = _causal_mask(t)
                    pt = jnp.where(r <= c, pt, 0.0)
                dv_sc[hh] += jnp.dot(pt.astype(BF16), dob, preferred_element_type=F32)
                dpt = lax.dot_general(v_ref[:, ls], dob, NT_DIMS, preferred_element_type=F32)
                dst = (pt * (dpt - dl_ref[hh, j]) * ATT_SCALE).astype(BF16)
                dk_sc[hh] += jnp.dot(dst, qb, preferred_element_type=F32)
                dq_ref[qs, ls] += lax.dot_general(dst, kb, TN_DIMS, preferred_element_type=F32)

        block(ki, True)

        def loop_body(j, carry):
            block(j, False)
            return carry

        lax.fori_loop(ki + 1, nb, loop_body, 0)
        for hh in range(hb):
            ls = slice(hh * HP, (hh + 1) * HP)
            dk_ref[:, ls] = dk_sc[hh].astype(dk_ref.dtype)
            dv_ref[:, ls] = dv_sc[hh].astype(dv_ref.dtype)
        if finish is not None:
            pl.when((pl.program_id(0) == nh - 1) & (ki == nb - 1))(finish)

    ex_in_specs, ex_out_specs, ex_out_shape, ex_scratch, ex_inputs = _carry_specs(carry)
    k_spec = pl.BlockSpec((t, w), lambda h, ki: (ki, h))
    resident = pl.BlockSpec((s_len, w), lambda h, ki: (0, h))
    row_spec = pl.BlockSpec((hb, nb, 1, t), lambda h, ki: (h, 0, 0, 0))
    dq, dk, dv, *carried = pl.pallas_call(
        body,
        grid=(nh, nb),
        in_specs=[resident, k_spec, k_spec, resident, row_spec, row_spec] + ex_in_specs,
        out_specs=[resident, k_spec, k_spec] + ex_out_specs,
        out_shape=[jax.ShapeDtypeStruct((s_len, HW), F32), jax.ShapeDtypeStruct((s_len, HW), F32), jax.ShapeDtypeStruct((s_len, HW), BF16)]
        + ex_out_shape,
        scratch_shapes=[pltpu.VMEM((hb, t, HP), F32), pltpu.VMEM((hb, t, HP), F32)] + ex_scratch,
        compiler_params=_cparams(("arbitrary", "arbitrary")),
        name=name,
    )(q, k, v, do, lse_row, delta_row, *ex_inputs)
    return dq, dk, dv, carried


CONV_PAD = 32


def conv_fwd(z, conv_w, conv_b):
    s_len = z.shape[0]
    ch = min(CONV_CHUNK, s_len)

    def body(ag_ref, w_ref, b_ref, c_ref, pad_ref):
        pad_ref[0:CONV_PAD, :] = jnp.zeros((CONV_PAD, LANE), F32)
        pad_ref[CONV_PAD:CONV_PAD + s_len, :] = ag_ref[:, 0:LANE] * jax.nn.sigmoid(ag_ref[:, LANE:2 * LANE])

        def chunk(i, carry):
            base = pl.multiple_of(i * ch, ch)
            acc = jnp.zeros((ch, LANE), F32) + b_ref[...]
            for kk in range(CONV_W):
                acc = acc + pad_ref[pl.ds(base + CONV_PAD - (CONV_W - 1) + kk, ch), :] * w_ref[kk:kk + 1, :]
            c_ref[pl.ds(base, ch), :] = acc
            return carry

        lax.fori_loop(0, s_len // ch, chunk, 0)

    nblk = CONV_C // LANE
    return pl.pallas_call(
        body,
        grid=(nblk,),
        in_specs=[pl.BlockSpec((s_len, 2 * LANE), lambda j: (0, ZC // (2 * LANE) + j)),
                  pl.BlockSpec((CONV_W, LANE), lambda j: (0, j)), pl.BlockSpec((1, LANE), lambda j: (0, j))],
        out_specs=pl.BlockSpec((s_len, LANE), lambda j: (0, j)),
        out_shape=jax.ShapeDtypeStruct((s_len, CONV_C), F32),
        scratch_shapes=[pltpu.VMEM((s_len + CONV_PAD, LANE), F32)],
        compiler_params=_cparams(("arbitrary",)),
        name="conv_fwd",
    )(z, conv_w, conv_b)


def conv_bwd(z, dc, conv_w, dz):
    s_len = z.shape[0]
    ch = min(CONV_CHUNK, s_len)

    def body(ag_ref, dc_ref, w_ref, dz_in, dag_ref, dw_ref, db_ref, pad_ref, dpad_ref, wacc_ref):
        del dz_in
        pad_ref[0:CONV_PAD, :] = jnp.zeros((CONV_PAD, LANE), F32)
        pad_ref[CONV_PAD:CONV_PAD + s_len, :] = ag_ref[:, 0:LANE] * jax.nn.sigmoid(ag_ref[:, LANE:2 * LANE])
        dpad_ref[0:s_len, :] = dc_ref[...]
        dpad_ref[s_len:s_len + CONV_PAD, :] = jnp.zeros((CONV_PAD, LANE), F32)
        wacc_ref[...] = jnp.zeros_like(wacc_ref)
        db_ref[...] = jnp.sum(dc_ref[...], axis=0, keepdims=True)

        def chunk(i, carry):
            base = pl.multiple_of(i * ch, ch)
            dcc = dpad_ref[pl.ds(base, ch), :]
            dh = jnp.zeros((ch, LANE), F32)
            for kk in range(CONV_W):
                dh = dh + dpad_ref[pl.ds(base + (CONV_W - 1) - kk, ch), :] * w_ref[kk:kk + 1, :]
                prod = dcc * pad_ref[pl.ds(base + CONV_PAD - (CONV_W - 1) + kk, ch), :]
                wacc_ref[kk * 8:(kk + 1) * 8, :] += prod.reshape(ch // 8, 8, LANE).sum(axis=0)
            a = ag_ref[pl.ds(base, ch), 0:LANE]
            sgc = jax.nn.sigmoid(ag_ref[pl.ds(base, ch), LANE:2 * LANE])
            dag_ref[pl.ds(base, ch), 0:LANE] = (dh * sgc).astype(dag_ref.dtype)
            dag_ref[pl.ds(base, ch), LANE:2 * LANE] = (dh * a * sgc * (1.0 - sgc)).astype(dag_ref.dtype)
            return carry

        lax.fori_loop(0, s_len // ch, chunk, 0)
        for kk in range(CONV_W):
            dw_ref[kk:kk + 1, :] = jnp.sum(wacc_ref[kk * 8:(kk + 1) * 8, :], axis=0, keepdims=True)

    nblk = CONV_C // LANE
    pair = pl.BlockSpec((s_len, 2 * LANE), lambda j: (0, ZC // (2 * LANE) + j))
    return pl.pallas_call(
        body,
        grid=(nblk,),
        in_specs=[pair, pl.BlockSpec((s_len, LANE), lambda j: (0, j)), pl.BlockSpec((CONV_W, LANE), lambda j: (0, j)), ANY],
        out_specs=[pair, pl.BlockSpec((CONV_W, LANE), lambda j: (0, j)), pl.BlockSpec((1, LANE), lambda j: (0, j))],
        out_shape=[jax.ShapeDtypeStruct(dz.shape, dz.dtype), jax.ShapeDtypeStruct((CONV_W, CONV_C), F32), jax.ShapeDtypeStruct((1, CONV_C), F32)],
        scratch_shapes=[pltpu.VMEM((s_len + CONV_PAD, LANE), F32), pltpu.VMEM((s_len + CONV_PAD, LANE), F32),
                        pltpu.VMEM((CONV_W * 8, LANE), F32)],
        input_output_aliases={3: 0},
        compiler_params=_cparams(("arbitrary",)),
        name="conv_bwd",
    )(z, dc, conv_w, dz)


POOL_PAD = 16


def _pool_count(base, ch, w):
    t = base + lax.broadcasted_iota(jnp.int32, (ch, 1), 0)
    return jnp.minimum(t + 1, w).astype(F32)


def pool_fwd(z, pool_w, pool_scale):
    s_len = z.shape[0]
    ch = min(CONV_CHUNK, s_len)

    def body(u_ref, pw_ref, sc_ref, m_ref, pad_ref):
        gi = pl.program_id(0)
        pad_ref[0:POOL_PAD, :] = jnp.zeros((POOL_PAD, LANE), F32)
        pad_ref[POOL_PAD:POOL_PAD + s_len, :] = u_ref[...]

        def run(w):
            def chunk(i, carry):
                base = pl.multiple_of(i * ch, ch)
                acc = jnp.zeros((ch, LANE), F32)
                for j in range(w):
                    acc = acc + pad_ref[pl.ds(base + POOL_PAD - j, ch), :]
                d = acc / _pool_count(base, ch, w) - u_ref[pl.ds(base, ch), :]
                md = jnp.dot(d.astype(BF16), pw_ref[0], preferred_element_type=F32)
                m_ref[pl.ds(base, ch), :] = (md * sc_ref[...]).astype(m_ref.dtype)
                return carry

            lax.fori_loop(0, s_len // ch, chunk, 0)

        for g, w in enumerate(POOL_WINDOWS):
            pl.when(gi == g)(functools.partial(run, w))

    return pl.pallas_call(
        body,
        grid=(POOL_G,),
        in_specs=[pl.BlockSpec((s_len, LANE), lambda g: (0, ZP // LANE + g)), pl.BlockSpec((1, POOL_GD, POOL_GD), lambda g: (g, 0, 0)),
                  pl.BlockSpec((1, LANE), lambda g: (0, g))],
        out_specs=pl.BlockSpec((s_len, LANE), lambda g: (0, g)),
        out_shape=jax.ShapeDtypeStruct((s_len, POOL_C), BF16),
        scratch_shapes=[pltpu.VMEM((s_len + POOL_PAD, LANE), F32)],
        compiler_params=_cparams(("arbitrary",)),
        name="pool_fwd",
    )(z, pool_w, pool_scale)


def pool_bwd(z, dm, pool_w, pool_scale, dz):
    s_len = z.shape[0]
    ch = min(CONV_CHUNK, s_len)

    def body(u_ref, dm_ref, pw_ref, sc_ref, dz_in, du_ref, dpw_ref, dsc_ref, pad_ref, epad_ref, dd_ref, sacc_ref):
        del dz_in
        gi = pl.program_id(0)
        pad_ref[0:POOL_PAD, :] = jnp.zeros((POOL_PAD, LANE), F32)
        pad_ref[POOL_PAD:POOL_PAD + s_len, :] = u_ref[...]
        epad_ref[s_len:s_len + POOL_PAD, :] = jnp.zeros((POOL_PAD, LANE), F32)
        dpw_ref[...] = jnp.zeros_like(dpw_ref)
        sacc_ref[...] = jnp.zeros_like(sacc_ref)

        def run(w):
            def first(i, carry):
                base = pl.multiple_of(i * ch, ch)
                acc = jnp.zeros((ch, LANE), F32)
                for j in range(w):
                    acc = acc + pad_ref[pl.ds(base + POOL_PAD - j, ch), :]
                cnt = _pool_count(base, ch, w)
                d = (acc / cnt - u_ref[pl.ds(base, ch), :]).astype(BF16)
                md = jnp.dot(d, pw_ref[0], preferred_element_type=F32)
                dmc = dm_ref[pl.ds(base, ch), :]
                sacc_ref[...] += (dmc * md).reshape(ch // 8, 8, LANE).sum(axis=0)
                dmd = (dmc * sc_ref[...]).astype(BF16)
                dpw_ref[0] += lax.dot_general(d, dmd, (((0,), (0,)), ((), ())), preferred_element_type=F32)
                dd = lax.dot_general(dmd, pw_ref[0], (((1,), (1,)), ((), ())), preferred_element_type=F32)
                dd_ref[pl.ds(base, ch), :] = dd
                epad_ref[pl.ds(base, ch), :] = dd / cnt
                return carry

            lax.fori_loop(0, s_len // ch, first, 0)

            def second(i, carry):
                base = pl.multiple_of(i * ch, ch)
                acc = jnp.zeros((ch, LANE), F32)
                for j in range(w):
                    acc = acc + epad_ref[pl.ds(base + j, ch), :]
                du_ref[pl.ds(base, ch), :] = (acc - dd_ref[pl.ds(base, ch), :]).astype(du_ref.dtype)
                return carry

            lax.fori_loop(0, s_len // ch, second, 0)

        for g, w in enumerate(POOL_WINDOWS):
            pl.when(gi == g)(functools.partial(run, w))
        dsc_ref[...] = jnp.sum(sacc_ref[...], axis=0, keepdims=True)

    return pl.pallas_call(
        body,
        grid=(POOL_G,),
        in_specs=[pl.BlockSpec((s_len, LANE), lambda g: (0, ZP // LANE + g)), pl.BlockSpec((s_len, LANE), lambda g: (0, g)),
                  pl.BlockSpec((1, POOL_GD, POOL_GD), lambda g: (g, 0, 0)), pl.BlockSpec((1, LANE), lambda g: (0, g)), ANY],
        out_specs=[pl.BlockSpec((s_len, LANE), lambda g: (0, ZP // LANE + g)), pl.BlockSpec((1, POOL_GD, POOL_GD), lambda g: (g, 0, 0)),
                   pl.BlockSpec((1, LANE), lambda g: (0, g))],
        out_shape=[jax.ShapeDtypeStruct(dz.shape, dz.dtype), jax.ShapeDtypeStruct((POOL_G, POOL_GD, POOL_GD), F32),
                   jax.ShapeDtypeStruct((1, POOL_C), F32)],
        scratch_shapes=[pltpu.VMEM((s_len + POOL_PAD, LANE), F32), pltpu.VMEM((s_len + POOL_PAD, LANE), F32),
                        pltpu.VMEM((s_len, LANE), F32), pltpu.VMEM((8, LANE), F32)],
        input_output_aliases={4: 0},
        compiler_params=_cparams(("arbitrary",)),
        name="pool_bwd",
    )(z, dm, pool_w, pool_scale, dz)


def _row(v):
    return v.reshape(1, -1)


def _rms_body(x_ref, g_ref, o_ref):
    o_ref[...] = _rms(x_ref[...], g_ref[...]).astype(o_ref.dtype)


def _post_body(y_ref, x_ref, g_ref, o_ref):
    o_ref[...] = x_ref[...] + _rms(y_ref[...], g_ref[...])


def _post_bwd_body(y_ref, dh_ref, g_ref, dy_ref, dg_ref):
    _, vjp = jax.vjp(_rms, y_ref[...], g_ref[...])
    dy, dg = vjp(dh_ref[...])
    dy_ref[...] = dy.astype(dy_ref.dtype)
    _acc(dg_ref, dg)


def _pre_bwd_body(x_ref, dhn_ref, dres_ref, g_ref, dx_ref, dg_ref):
    _, vjp = jax.vjp(_rms, x_ref[...], g_ref[...])
    dx, dg = vjp(dhn_ref[...])
    dx_ref[...] = dres_ref[...] + dx
    _acc(dg_ref, dg)


def _post_next_body(y_ref, x_ref, g_ref, gn_ref, o_ref, n_ref):
    o = x_ref[...] + _rms(y_ref[...], g_ref[...])
    o_ref[...] = o
    n_ref[...] = _rms(o, gn_ref[...]).astype(n_ref.dtype)


def _pre_post_bwd_body(x_ref, dhn_ref, dres_ref, y_ref, g_ref, gy_ref, dx_ref, dy_ref, dg_ref, dgy_ref):
    _, vjp = jax.vjp(_rms, x_ref[...], g_ref[...])
    dx, dg = vjp(dhn_ref[...])
    dx = dres_ref[...] + dx
    dx_ref[...] = dx
    _, vjp_y = jax.vjp(_rms, y_ref[...], gy_ref[...])
    dy, dgy = vjp_y(dx)
    dy_ref[...] = dy.astype(dy_ref.dtype)
    _acc(dg_ref, dg)
    _acc(dgy_ref, dgy)


def mixer_fwd(x, tabs, w, tag, carry=None, head=None, h=None):
    s_len = x.shape[0]
    cc, sa, sb = tabs
    sv = {"x": x}

    if h is None:
        (h,) = rowwise("mix_norm_pre" + tag, _rms_body, s_len, [_whole(x)], [_row(w["mix_norm_pre"])], [(D_MODEL, BF16)])
    if head is None:
        z = mm(h, w["w_in"], name="in_proj" + tag)
    else:
        z, arrived = mm(h, w["w_in"], name="in_proj" + tag, carry=head[0])
        w = {**w, **head[1](arrived)}

    def prep_body(z_ref, cc_ref, sa_ref, sb_ref, qg_ref, kg_ref, qn_ref, ckv_ref, kr_ref):
        qn_ref[...] = _rms(z_ref[:, 0:Q_RANK], qg_ref[...]).astype(qn_ref.dtype)
        ckv_ref[...] = _rms(z_ref[:, Q_RANK:Q_RANK + KV_RANK], kg_ref[...]).astype(ckv_ref.dtype)
        kr_ref[...] = _rope(z_ref[:, Q_RANK + KV_RANK:ZA_W], cc_ref[...], sa_ref[...], sb_ref[...])

    qn, ckvn, kr = rowwise("attn_prep" + tag, prep_body, s_len, [(z, ZA_W, ZA // ZA_W), _whole(cc), _whole(sa), _whole(sb)],
                           [_row(w["q_norm"]), _row(w["kv_norm"])], [(Q_RANK, BF16), (KV_RANK, BF16), (HP, F32)])
    q_raw = mm(qn, w["w_uq"], name="q_proj" + tag)
    kv_raw = mm(ckvn, w["w_ukv"], name="kv_proj" + tag)

    def qkv_body(q_ref, kv_ref, kr_ref, cc_ref, sa_ref, sb_ref, qo_ref, ko_ref, vo_ref):
        c_, a_, b_, kro = cc_ref[...], sa_ref[...], sb_ref[...], kr_ref[...]
        for hh in range(N_HEADS):
            sl = slice(hh * HP, (hh + 1) * HP)
            qo_ref[:, sl] = _rope(q_ref[:, sl], c_, a_, b_).astype(qo_ref.dtype)
            ko_ref[:, sl] = (kv_ref[:, sl] + kro).astype(ko_ref.dtype)
        lane = lax.broadcasted_iota(jnp.int32, (q_ref.shape[0], HW), 1)
        vo_ref[...] = jnp.where((lane & (HP - 1)) == VDIM, 1.0, kv_ref[:, HW:2 * HW]).astype(vo_ref.dtype)

    q, k, v = rowwise("qkv_rope" + tag, qkv_body, s_len, [_whole(q_raw), _whole(kv_raw), _whole(kr), _whole(cc), _whole(sa), _whole(sb)], [],
                      [(HW, BF16)] * 3)
    o, lse, carried = attention_fwd(q, k, v, "attention_fwd" + tag, carry)
    y_attn = mm(o, w["w_attn_o"], out_dtype=BF16, name="attn_out" + tag)

    c = conv_fwd(z, w["conv_w"], _row(w["conv_b"]))

    def ln_body(c_ref, g_ref, b_ref, o_ref):
        o_ref[...] = _silu(_layer_norm(c_ref[...], g_ref[...], b_ref[...])).astype(o_ref.dtype)

    (cs,) = rowwise("conv_ln_silu" + tag, ln_body, s_len, [_whole(c)], [_row(w["conv_ln_g"]), _row(w["conv_ln_b"])], [(CONV_C, BF16)])
    y_conv = mm(cs, w["w_conv_o"], out_dtype=BF16, name="conv_out" + tag)

    m = pool_fwd(z, w["pool_w"], _row(w["pool_scale"]))
    y_pool = mm(m, w["w_pool_o"], out_dtype=BF16, name="pool_out" + tag)

    def merge_body(ya_ref, yc_ref, yp_ref, gl_ref, o_ref):
        gl = gl_ref[...]
        o_ref[...] = (jax.nn.sigmoid(gl[:, 0:D_MODEL]) * ya_ref[...].astype(F32) + jax.nn.sigmoid(gl[:, D_MODEL:2 * D_MODEL]) * yc_ref[...].astype(F32)
                      + jax.nn.sigmoid(gl[:, 2 * D_MODEL:3 * D_MODEL]) * yp_ref[...].astype(F32)).astype(o_ref.dtype)

    (merged,) = rowwise("gate_merge" + tag, merge_body, s_len, [_whole(y_attn), _whole(y_conv), _whole(y_pool), (z, 3 * D_MODEL, 0)], [],
                        [(D_MODEL, BF16)])
    mo = mm(merged, w["w_mix_o"], name="mix_out" + tag)

    h1, hn = rowwise("mix_norm_post" + tag, _post_next_body, s_len, [_whole(mo), _whole(x)],
                     [_row(w["mix_norm_post"]), _row(w["ffn_norm_pre"])], [(D_MODEL, F32), (D_MODEL, BF16)])
    sv.update(h=h, z=z, qn=qn, ckvn=ckvn, q=q, k=k, v=v, o=o, lse=lse, c=c, cs=cs, m=m, y_attn=y_attn, y_conv=y_conv, y_pool=y_pool,
              merged=merged, mo=mo, w=w, hn=hn)
    return h1, sv, carried


def ffn_fwd(h1, hn, w, tag, next_gain=None):
    s_len = h1.shape[0]
    gu, act = ffn_in(hn, w["w_gu"], "ffn_in" + tag)
    y = mm(act, w["w_down"], name="ffn_out" + tag)
    if next_gain is None:
        (h2,) = rowwise("ffn_norm_post" + tag, _post_body, s_len, [_whole(y), _whole(h1)], [_row(w["ffn_norm_post"])], [(D_MODEL, F32)])
        h_next = None
    else:
        h2, h_next = rowwise("ffn_norm_post" + tag, _post_next_body, s_len, [_whole(y), _whole(h1)],
                             [_row(w["ffn_norm_post"]), _row(next_gain)], [(D_MODEL, F32), (D_MODEL, BF16)])
    return h2, dict(h1=h1, hn=hn, gu=gu, act=act, y=y), h_next


def ffn_bwd(dh2, sv, w, tag, mo, post=None):
    s_len = dh2.shape[0]
    g = {}
    if post is None:
        d_y, g["ffn_norm_post"] = rowwise("ffn_norm_post_bwd" + tag, _post_bwd_body, s_len, [_whole(sv["y"]), _whole(dh2)],
                                          [_row(w["ffn_norm_post"])], [(D_MODEL, BF16)], [((1, D_MODEL), F32)])
    else:
        d_y, g["ffn_norm_post"] = post
    g["w_down"] = mm(sv["act"], d_y, ta=True, name="ffn_out_dw" + tag)
    d_gu = ffn_out_dx(d_y, w["w_down"], sv["gu"], "ffn_out_dx" + tag)
    g["w_gu"] = mm(sv["hn"], d_gu, ta=True, name="ffn_in_dw" + tag)
    d_hn = mm(d_gu, w["w_gu"], tb=True, name="ffn_in_dx" + tag)
    dh1, d_mo, g["ffn_norm_pre"], d_mix_post = rowwise(
        "ffn_norm_pre_bwd" + tag, _pre_post_bwd_body, s_len, [_whole(sv["h1"]), _whole(d_hn), _whole(dh2), _whole(mo)],
        [_row(w["ffn_norm_pre"]), _row(w["mix_norm_post"])], [(D_MODEL, F32), (D_MODEL, BF16)], [((1, D_MODEL), F32), ((1, D_MODEL), F32)])
    return dh1, g, (d_mo, d_mix_post)


def mixer_bwd(dh1, sv, tabs, w, tag, post, carry=None, tail=None, prev=None):
    s_len = dh1.shape[0]
    cc, sa, sb = tabs
    g = {}

    d_mo, g["mix_norm_post"] = post
    g["w_mix_o"] = mm(sv["merged"], d_mo, ta=True, name="mix_out_dw" + tag)
    d_merged = mm(d_mo, w["w_mix_o"], tb=True, name="mix_out_dx" + tag)

    def merge_bwd_body(dm_ref, ya_ref, yc_ref, yp_ref, gl_ref, dya_ref, dyc_ref, dyp_ref, dgl_ref):
        dmg = dm_ref[...]
        for i, (y_ref, dy_ref) in enumerate(((ya_ref, dya_ref), (yc_ref, dyc_ref), (yp_ref, dyp_ref))):
            sg = jax.nn.sigmoid(gl_ref[:, i * D_MODEL:(i + 1) * D_MODEL])
            dy_ref[...] = (dmg * sg).astype(dy_ref.dtype)
            dgl_ref[:, i * D_MODEL:(i + 1) * D_MODEL] = (dmg * y_ref[...].astype(F32) * sg * (1.0 - sg)).astype(dgl_ref.dtype)

    d_ya, d_yc, d_yp, dz = rowwise(
        "gate_merge_bwd" + tag, merge_bwd_body, s_len,
        [_whole(d_merged), _whole(sv["y_attn"]), _whole(sv["y_conv"]), _whole(sv["y_pool"]), (sv["z"], 3 * D_MODEL, 0)], [],
        [(D_MODEL, BF16)] * 3 + [(3 * D_MODEL, BF16, ZW, 0)], tile=WIDE_ROW_TILE)

    g["w_pool_o"] = mm(sv["m"], d_yp, ta=True, name="pool_out_dw" + tag)
    d_m = mm(d_yp, w["w_pool_o"], tb=True, name="pool_out_dx" + tag)
    dz, g["pool_w"], g["pool_scale"] = pool_bwd(sv["z"], d_m, w["pool_w"], _row(w["pool_scale"]), dz)

    g["w_conv_o"] = mm(sv["cs"], d_yc, ta=True, name="conv_out_dw" + tag)
    d_cs = mm(d_yc, w["w_conv_o"], tb=True, name="conv_out_dx" + tag)

    def ln_bwd_body(c_ref, dcs_ref, g_ref, b_ref, dc_ref, dg_ref, db_ref):
        f = lambda c_, g_, b_: _silu(_layer_norm(c_, g_, b_))
        _, vjp = jax.vjp(f, c_ref[...], g_ref[...], b_ref[...])
        dc, dg, db = vjp(dcs_ref[...])
        dc_ref[...] = dc
        _acc(dg_ref, dg)
        _acc(db_ref, db)

    d_c, g["conv_ln_g"], g["conv_ln_b"] = rowwise("conv_ln_silu_bwd" + tag, ln_bwd_body, s_len, [_whole(sv["c"]), _whole(d_cs)],
                                                  [_row(w["conv_ln_g"]), _row(w["conv_ln_b"])], [(CONV_C, F32)],
                                                  [((1, CONV_C), F32), ((1, CONV_C), F32)])
    dz, g["conv_w"], g["conv_b"] = conv_bwd(sv["z"], d_c, w["conv_w"], dz)

    g["w_attn_o"] = mm(sv["o"], d_ya, ta=True, name="attn_out_dw" + tag)
    d_o = mm(d_ya, w["w_attn_o"], tb=True, out_dtype=BF16, name="attn_out_dx" + tag)
    delta = attention_delta(d_o, sv["o"])
    t_bwd = min(ATT_TILE_BWD, s_len)
    rows_of = lambda a: a.reshape(N_HEADS, s_len // t_bwd, 1, t_bwd)
    dq, dk, dv, carried = attention_bwd(sv["q"], sv["k"], sv["v"], d_o, rows_of(sv["lse"]), rows_of(delta), "attention_bwd" + tag, carry)

    def qkv_bwd_body(dq_ref, dk_ref, dv_ref, cc_ref, sa_ref, sb_ref, dqp_ref, dkv_ref, dkr_ref):
        c_, a_, b_ = cc_ref[...], sa_ref[...], sb_ref[...]
        dk_sum = jnp.zeros((dq_ref.shape[0], HP), F32)
        for hh in range(N_HEADS):
            sl = slice(hh * HP, (hh + 1) * HP)
            dqp_ref[:, sl] = _rope_t(dq_ref[:, sl], c_, a_, b_).astype(dqp_ref.dtype)
            dkh = dk_ref[:, sl]
            dkv_ref[:, sl] = dkh.astype(dkv_ref.dtype)
            dk_sum = dk_sum + dkh
        dkv_ref[:, HW:2 * HW] = dv_ref[...]
        dkr_ref[...] = _rope_t(dk_sum, c_, a_, b_)

    dq_pre, dkv_pre, d_kr = rowwise("qkv_rope_bwd" + tag, qkv_bwd_body, s_len,
                                    [_whole(dq), _whole(dk), _whole(dv), _whole(cc), _whole(sa), _whole(sb)], [],
                                    [(HW, BF16), (2 * HW, BF16), (HP, F32)])
    g["w_uq"] = mm(sv["qn"], dq_pre, ta=True, name="q_proj_dw" + tag)
    d_qn = mm(dq_pre, w["w_uq"], tb=True, name="q_proj_dx" + tag)
    g["w_ukv"] = mm(sv["ckvn"], dkv_pre, ta=True, name="kv_proj_dw" + tag)
    d_ckvn = mm(dkv_pre, w["w_ukv"], tb=True, name="kv_proj_dx" + tag)

    def prep_bwd_body(z_ref, dqn_ref, dckv_ref, dkr_ref, qg_ref, kg_ref, dz_ref, dqg_ref, dkg_ref):
        _, vq = jax.vjp(_rms, z_ref[:, 0:Q_RANK], qg_ref[...])
        dcq, dqg = vq(dqn_ref[...])
        _, vk = jax.vjp(_rms, z_ref[:, Q_RANK:Q_RANK + KV_RANK], kg_ref[...])
        dckv, dkg = vk(dckv_ref[...])
        dz_ref[:, 0:Q_RANK] = dcq.astype(dz_ref.dtype)
        dz_ref[:, Q_RANK:Q_RANK + KV_RANK] = dckv.astype(dz_ref.dtype)
        dz_ref[:, Q_RANK + KV_RANK:ZA_W] = dkr_ref[...].astype(dz_ref.dtype)
        _acc(dqg_ref, dqg)
        _acc(dkg_ref, dkg)

    dz, g["q_norm"], g["kv_norm"] = rowwise("attn_prep_bwd" + tag, prep_bwd_body, s_len,
                                            [(sv["z"], ZA_W, ZA // ZA_W), _whole(d_qn), _whole(d_ckvn), _whole(d_kr)],
                                            [_row(w["q_norm"]), _row(w["kv_norm"])], [(ZA_W, BF16, ZW, ZA // ZA_W)],
                                            [((1, Q_RANK), F32), ((1, KV_RANK), F32)], into={0: dz})

    g["w_in"] = mm(sv["h"], dz, ta=True, name="in_proj_dw" + tag)
    if tail is None:
        d_h = mm(dz, w["w_in"], tb=True, name="in_proj_dx" + tag)
    else:
        d_h, tailed = mm(dz, w["w_in"], tb=True, name="in_proj_dx" + tag, carry=tail(g))
        carried = [carried, tailed]
    if prev is None:
        dx, g["mix_norm_pre"] = rowwise(
            "mix_norm_pre_bwd" + tag, _pre_bwd_body, s_len, [_whole(sv["x"]), _whole(d_h), _whole(dh1)], [_row(w["mix_norm_pre"])], [(D_MODEL, F32)], [((1, D_MODEL), F32)])
        return dx, g, carried, None
    y_prev, gain_prev = prev
    dx, d_y_prev, g["mix_norm_pre"], d_gain_prev = rowwise(
        "mix_norm_pre_bwd" + tag, _pre_post_bwd_body, s_len, [_whole(sv["x"]), _whole(d_h), _whole(dh1), _whole(y_prev)],
        [_row(w["mix_norm_pre"]), _row(gain_prev)], [(D_MODEL, F32), (D_MODEL, BF16)], [((1, D_MODEL), F32), ((1, D_MODEL), F32)])
    return dx, g, carried, (d_y_prev, d_gain_prev)


def loss_head(h, target):
    s_len = h.shape[0]

    def body(h_ref, t_ref, dy_ref, loss_ref):
        err = h_ref[...] - t_ref[...]
        dy_ref[...] = err * (1.0 / D_MODEL)
        part = 0.5 * jnp.sum(jnp.mean(err * err, axis=-1, keepdims=True), axis=0, keepdims=True)
        _acc(loss_ref, jnp.broadcast_to(part, (1, LANE)))

    return rowwise("loss_head", body, s_len, [_whole(h), _whole(target)], [], [(D_MODEL, F32)], [((1, LANE), F32)])


def local_step(x, pos_col, target, layers):
    s_len = x.shape[0]
    tabs = rope_tables(pos_col, s_len)
    h, h_normed, saved = x, None, []
    for li, w in enumerate(layers):
        h, sv_mix, _ = mixer_fwd(h, tabs, w, f"_l{li}", h=h_normed)
        next_gain = layers[li + 1]["mix_norm_pre"] if li + 1 < len(layers) else None
        h, sv_ffn, h_normed = ffn_fwd(h, sv_mix["hn"], w, f"_l{li}", next_gain)
        saved.append((sv_mix, sv_ffn))
    dh, loss = loss_head(h, target)
    grads, post = [None] * len(layers), None
    for li in reversed(range(len(layers))):
        dh, g_ffn, mix_post = ffn_bwd(dh, saved[li][1], layers[li], f"_l{li}", saved[li][0]["mo"], post)
        prev = (saved[li - 1][1]["y"], layers[li - 1]["ffn_norm_post"]) if li > 0 else None
        dh, g_mix, _, post = mixer_bwd(dh, saved[li][0], tabs, layers[li], f"_l{li}", mix_post, prev=prev)
        grads[li] = {**g_mix, **g_ffn}
    return loss[0, 0], dh, grads


def _pad_heads_cols(wm, per_head):
    r = wm.shape[0]
    return jnp.pad(wm.reshape(r, N_HEADS, per_head), ((0, 0), (0, 0), (0, HP - per_head))).reshape(r, HW)


def _unpad_heads_cols(wm, per_head):
    r = wm.shape[0]
    return wm.reshape(r, N_HEADS, HP)[:, :, :per_head].reshape(r, N_HEADS * per_head)


def align_weights(p):
    out = dict(p)
    if "w_in" in p:
        w_in = p["w_in"]
        r = w_in.shape[0]
        zeros = lambda n: jnp.zeros((r, n), w_in.dtype)
        conv = w_in[:, O_CONV:O_POOL].reshape(r, 2, CONV_C // LANE, LANE).transpose(0, 2, 1, 3).reshape(r, 2 * CONV_C)
        out["w_in"] = jnp.concatenate([
            w_in[:, O_GATE:D_IN], conv, w_in[:, O_POOL:O_GATE], w_in[:, O_Q:O_KR],
            zeros(KR_LANE), w_in[:, O_KR:O_CONV], zeros(HP - KR_LANE - ROPE)], axis=1)
    if "w_uq" in p:
        out["w_uq"] = _pad_heads_cols(p["w_uq"], NOPE + ROPE)
        out["w_ukv"] = jnp.concatenate([_pad_heads_cols(p["w_uk"], NOPE), _pad_heads_cols(p["w_uv"], VDIM)], axis=1)
        wo = p["w_attn_o"]
        out["w_attn_o"] = jnp.pad(wo.reshape(N_HEADS, VDIM, D_MODEL), ((0, 0), (0, HP - VDIM), (0, 0))).reshape(HW, D_MODEL)
        del out["w_uk"], out["w_uv"]
    if "w_gate" in p:
        out["w_gu"] = jnp.concatenate([p["w_gate"][:, :FF_HALF], p["w_up"][:, :FF_HALF], p["w_gate"][:, FF_HALF:], p["w_up"][:, FF_HALF:]], axis=1)
        del out["w_gate"], out["w_up"]
    return out


def unalign_grads(g):
    out = dict(g)
    if "w_in" in g:
        gi = g["w_in"]
        kr0 = ZA + Q_RANK + KV_RANK + KR_LANE
        r = gi.shape[0]
        conv = gi[:, ZC:ZP].reshape(r, CONV_C // LANE, 2, LANE).transpose(0, 2, 1, 3).reshape(r, 2 * CONV_C)
        out["w_in"] = jnp.concatenate([gi[:, ZA:ZA + Q_RANK + KV_RANK], gi[:, kr0:kr0 + ROPE], conv, gi[:, ZP:ZA], gi[:, ZG:ZC]], axis=1)
        out["w_uq"] = _unpad_heads_cols(g["w_uq"], NOPE + ROPE)
        out["w_uk"] = _unpad_heads_cols(g["w_ukv"][:, :HW], NOPE)
        out["w_uv"] = _unpad_heads_cols(g["w_ukv"][:, HW:], VDIM)
        out["w_attn_o"] = g["w_attn_o"].reshape(N_HEADS, HP, D_MODEL)[:, :VDIM].reshape(N_HEADS * VDIM, D_MODEL)
        del out["w_ukv"]
    if "w_gu" in g:
        gu = g["w_gu"]
        out["w_gate"] = jnp.concatenate([gu[:, 0:FF_HALF], gu[:, 2 * FF_HALF:3 * FF_HALF]], axis=1)
        out["w_up"] = jnp.concatenate([gu[:, FF_HALF:2 * FF_HALF], gu[:, 3 * FF_HALF:]], axis=1)
        del out["w_gu"]
    return out


MESH = pl.DeviceIdType.MESH
ANY = pl.BlockSpec(memory_space=pl.ANY)


def _place():
    return lax.axis_index("x"), lax.axis_index("y"), lax.axis_index("c")


def _other_chips(x, y):
    return [(1 - x, y), (x, 1 - y), (1 - x, 1 - y)]


def _half_rows(rows, c):
    assert rows % (2 * HALF_ALIGN) == 0, rows
    return pl.ds(pl.multiple_of(c * (rows // 2), HALF_ALIGN), rows // 2)


class GatherShards:
    def __init__(self, local):
        self.ins = list(local)
        self.outs = [jax.ShapeDtypeStruct((N_CHIPS, *a.shape), a.dtype) for a in local]
        self.n_sems = 6 * len(local)
        self.base = 0

    def _first(self, in_refs, out_refs, send_sems, recv_sems):
        x, y, c = _place()
        me = 2 * x + y
        chips = _other_chips(x, y)

        def copy(i, k, slot, core, to, src=None):
            dst = out_refs[i].at[slot, _half_rows(out_refs[i].shape[1], core)]
            return pltpu.make_async_remote_copy(src_ref=dst if src is None else src, dst_ref=dst, send_sem=send_sems.at[self.base + 6 * i + k],
                                                recv_sem=recv_sems.at[self.base + 6 * i + k], device_id=to, device_id_type=MESH)

        first = [copy(i, j, me, c, (*chip, c), src=in_refs[i].at[_half_rows(in_refs[i].shape[0], c)])
                 for i in range(len(in_refs)) for j, chip in enumerate(chips)]
        return first, copy

    def start(self, in_refs, out_refs, send_sems, recv_sems):
        first, _ = self._first(in_refs, out_refs, send_sems, recv_sems)
        for cp in first:
            cp.start()

    def finish(self, in_refs, out_refs, send_sems, recv_sems):
        first, copy = self._first(in_refs, out_refs, send_sems, recv_sems)
        x, y, c = _place()
        slots = [2 * cx + cy for cx, cy in _other_chips(x, y)]
        sibling = (x, y, 1 - c)
        passed = []
        for i in range(len(in_refs)):
            for j in range(3):
                copy(i, j, slots[j], c, sibling).wait_recv()
                fwd = copy(i, 3 + j, slots[j], c, sibling)
                fwd.start()
                passed.append(fwd)
        for i in range(len(in_refs)):
            for j in range(3):
                copy(i, 3 + j, slots[j], 1 - c, sibling).wait_recv()
        for cp in first + passed:
            cp.wait_send()


class ChipExchange:
    def __init__(self, parts):
        self.ins = list(parts)
        self.outs = [jax.ShapeDtypeStruct((3, *a.shape[1:]), a.dtype) for a in parts]
        self.n_sems = 3 * len(parts)
        self.base = 0

    def _copies(self, in_refs, out_refs, send_sems, recv_sems):
        x, y, c = _place()
        return [pltpu.make_async_remote_copy(src_ref=in_refs[i].at[2 * chip[0] + chip[1]], dst_ref=out_refs[i].at[j],
                                             send_sem=send_sems.at[self.base + 3 * i + j], recv_sem=recv_sems.at[self.base + 3 * i + j],
                                             device_id=(*chip, c), device_id_type=MESH)
                for i in range(len(in_refs)) for j, chip in enumerate(_other_chips(x, y))]

    def start(self, in_refs, out_refs, send_sems, recv_sems):
        for cp in self._copies(in_refs, out_refs, send_sems, recv_sems):
            cp.start()

    def finish(self, in_refs, out_refs, send_sems, recv_sems):
        copies = self._copies(in_refs, out_refs, send_sems, recv_sems)
        for cp in copies:
            cp.wait_recv()
        for cp in copies:
            cp.wait_send()


def run_exchange(ex, name):
    n_in, n_out = len(ex.ins), len(ex.outs)

    def body(*refs):
        ins, outs, sems = refs[:n_in], refs[n_in:n_in + n_out], refs[n_in + n_out:]
        ex.start(ins, outs, *sems)
        ex.finish(ins, outs, *sems)

    return pl.pallas_call(
        body,
        in_specs=[ANY] * n_in,
        out_specs=[ANY] * n_out,
        out_shape=list(ex.outs),
        scratch_shapes=[pltpu.SemaphoreType.DMA((ex.n_sems,)), pltpu.SemaphoreType.DMA((ex.n_sems,))],
        name=name,
    )(*ex.ins)


def sibling_swap(gs, name):
    n = len(gs)

    def body(*refs):
        g_refs, out_refs, (send_sems, recv_sems) = refs[:n], refs[n:2 * n], refs[2 * n:]
        x, y, c = _place()
        copies = []
        for i in range(n):
            for j in range(N_CHIPS):
                cp = pltpu.make_async_remote_copy(src_ref=g_refs[i].at[j, _half_rows(g_refs[i].shape[1], 1 - c)], dst_ref=out_refs[i].at[j],
                                                  send_sem=send_sems.at[4 * i + j], recv_sem=recv_sems.at[4 * i + j],
                                                  device_id=(x, y, 1 - c), device_id_type=MESH)
                cp.start()
                copies.append(cp)
        for cp in copies:
            cp.wait_recv()
        for cp in copies:
            cp.wait_send()

    return pl.pallas_call(
        body,
        in_specs=[ANY] * n,
        out_specs=[ANY] * n,
        out_shape=[jax.ShapeDtypeStruct((N_CHIPS, a.shape[1] // 2, a.shape[2]), a.dtype) for a in gs],
        scratch_shapes=[pltpu.SemaphoreType.DMA((4 * n,)), pltpu.SemaphoreType.DMA((4 * n,))],
        name=name,
    )(*gs)


def sibling_gather(fs):
    n = len(fs)
    layers = fs[0].shape[0]

    def body(*refs):
        out_refs, (send_sems, recv_sems) = refs[n:2 * n], refs[2 * n:]
        x, y, c = _place()

        def copy(i, l, core):
            part = out_refs[i].at[l, _half_rows(out_refs[i].shape[1], core)]
            return pltpu.make_async_remote_copy(src_ref=part, dst_ref=part, send_sem=send_sems.at[layers * i + l],
                                                recv_sem=recv_sems.at[layers * i + l], device_id=(x, y, 1 - c), device_id_type=MESH)

        sends = [copy(i, l, c) for i in range(n) for l in range(layers)]
        for cp in sends:
            cp.start()
        for i in range(n):
            for l in range(layers):
                copy(i, l, 1 - c).wait_recv()
        for cp in sends:
            cp.wait_send()

    return pl.pallas_call(
        body,
        in_specs=[ANY] * n,
        out_specs=[ANY] * n,
        out_shape=[jax.ShapeDtypeStruct(a.shape, a.dtype) for a in fs],
        scratch_shapes=[pltpu.SemaphoreType.DMA((layers * n,)), pltpu.SemaphoreType.DMA((layers * n,))],
        input_output_aliases={i: i for i in range(n)},
        name="sibling_gather",
    )(*fs)


class GatherAll:
    def __init__(self, vs):
        self.ins = list(vs)
        self.outs = [jax.ShapeDtypeStruct((8, *a.shape), a.dtype) for a in vs]
        self.n_sems = 7 * len(vs)
        self.base = 0

    def _first(self, in_refs, out_refs, send_sems, recv_sems):
        x, y, c = _place()
        me, sibling = (x, y, c), (x, y, 1 - c)

        def copy(i, k, block, to, src=None):
            px, py, pc = block
            dst = out_refs[i].at[4 * px + 2 * py + pc]
            return pltpu.make_async_remote_copy(src_ref=dst if src is None else src, dst_ref=dst, send_sem=send_sems.at[self.base + 7 * i + k],
                                                recv_sem=recv_sems.at[self.base + 7 * i + k], device_id=to, device_id_type=MESH)

        first = []
        for i in range(len(in_refs)):
            first.append(copy(i, 0, me, sibling, src=in_refs[i]))
            first += [copy(i, 1 + j, me, (*chip, c), src=in_refs[i]) for j, chip in enumerate(_other_chips(x, y))]
        return first, copy

    def start(self, in_refs, out_refs, send_sems, recv_sems):
        first, _ = self._first(in_refs, out_refs, send_sems, recv_sems)
        for cp in first:
            cp.start()

    def finish(self, in_refs, out_refs, send_sems, recv_sems):
        first, copy = self._first(in_refs, out_refs, send_sems, recv_sems)
        x, y, c = _place()
        me, sibling = (x, y, c), (x, y, 1 - c)
        chips = _other_chips(x, y)
        passed = []
        for i in range(len(in_refs)):
            for j, chip in enumerate(chips):
                copy(i, 1 + j, (*chip, c), me).wait_recv()
                fwd = copy(i, 4 + j, (*chip, c), sibling)
                fwd.start()
                passed.append(fwd)
        for i in range(len(in_refs)):
            copy(i, 0, sibling, me).wait_recv()
            for j, chip in enumerate(chips):
                copy(i, 4 + j, (*chip, 1 - c), me).wait_recv()
        for cp in first + passed:
            cp.wait_send()


def _row_tile(rows, row_bytes):
    best = None
    for t in range(16, rows + 1, 16):
        if rows % t == 0 and t * row_bytes <= SUM_TILE_BYTES:
            best = t
    return best or rows


def sibling_sum(g, theirs, place, name):
    _, half, cols = theirs.shape
    tile = _row_tile(half, cols * 4)
    nt = half // tile

    def body(place_ref, g_ref, t_ref, o_ref):
        o_ref[...] = (g_ref[...].astype(F32) + t_ref[...].astype(F32)).astype(o_ref.dtype)

    spec = pl.BlockSpec((1, tile, cols), lambda j, i, place_ref: (j, i, 0))
    return pl.pallas_call(
        body,
        grid_spec=pltpu.PrefetchScalarGridSpec(
            num_scalar_prefetch=1, grid=(N_CHIPS, nt),
            in_specs=[pl.BlockSpec((1, tile, cols), lambda j, i, place_ref: (j, place_ref[1] * nt + i, 0)), spec], out_specs=spec),
        out_shape=jax.ShapeDtypeStruct(theirs.shape, BF16),
        compiler_params=_cparams(("parallel", "parallel")),
        name=name,
    )(place, g, theirs)


def chip_sum(p, others, place, layer, into, name):
    _, half, cols = p.shape
    tile = _row_tile(half, cols * 4)
    nt = half // tile

    def body(place_ref, p_ref, o3_ref, *rest):
        o_ref = rest[-1]
        acc = p_ref[0].astype(F32)
        for k in range(3):
            acc = acc + o3_ref[k].astype(F32)
        o_ref[0] = acc

    return pl.pallas_call(
        body,
        grid_spec=pltpu.PrefetchScalarGridSpec(
            num_scalar_prefetch=1, grid=(nt,),
            in_specs=[pl.BlockSpec((1, tile, cols), lambda i, place_ref: (place_ref[0], i, 0)),
                      pl.BlockSpec((3, tile, cols), lambda i, place_ref: (0, i, 0))] + ([] if into is None else [ANY]),
            out_specs=pl.BlockSpec((1, tile, cols), lambda i, place_ref: (layer, place_ref[1] * nt + i, 0))),
        out_shape=jax.ShapeDtypeStruct((N_LAYERS, 2 * half, cols), F32),
        input_output_aliases={} if into is None else {3: 0},
        compiler_params=_cparams(("parallel",)),
        name=name,
    )(place, p, others, *([] if into is None else [into]))


def sum_devices(a, name):
    n, rows, cols = a.shape
    tile = _row_tile(rows, cols * 4 * n)

    def body(a_ref, o_ref):
        acc = a_ref[0]
        for s in range(1, n):
            acc = acc + a_ref[s]
        o_ref[...] = acc

    return pl.pallas_call(body, grid=(rows // tile,), in_specs=[pl.BlockSpec((n, tile, cols), lambda i: (0, i, 0))],
                          out_specs=pl.BlockSpec((tile, cols), lambda i: (i, 0)), out_shape=jax.ShapeDtypeStruct((rows, cols), F32),
                          compiler_params=_cparams(("parallel",)), name=name)(a)


def adamw(w, g, m, v, name):
    layers, rows, cols = w.shape
    tile = rows
    for t in range(8, rows, 8):
        if rows % t == 0 and t * cols * 4 <= ADAM_TILE_BYTES:
            tile = t

    def body(w_ref, g_ref, m_ref, v_ref, d_ref, mo_ref, vo_ref):
        gg = g_ref[...]
        m_new = ADAM_B1 * m_ref[...] + (1.0 - ADAM_B1) * gg
        v_new = ADAM_B2 * v_ref[...] + (1.0 - ADAM_B2) * (gg * gg)
        m_hat = m_new / (1.0 - ADAM_B1 ** ADAM_STEP)
        v_hat = v_new / (1.0 - ADAM_B2 ** ADAM_STEP)
        d_ref[...] = -ADAM_LR * (m_hat / (jnp.sqrt(v_hat) + ADAM_EPS) + ADAM_WD * w_ref[...])
        mo_ref[...] = m_new
        vo_ref[...] = v_new

    spec = pl.BlockSpec((1, tile, cols), lambda l, i: (l, i, 0))
    shape = jax.ShapeDtypeStruct((layers, rows, cols), F32)
    return pl.pallas_call(body, grid=(layers, rows // tile), in_specs=[spec] * 4, out_specs=[spec] * 3, out_shape=[shape] * 3,
                          compiler_params=_cparams(("parallel", "parallel")), name=name)(w, g, m, v)


WEIGHTS = ["mix_norm_pre", "w_in", "q_norm", "w_uq", "kv_norm", "w_uk", "w_uv", "w_attn_o", "conv_w", "conv_b", "conv_ln_g", "conv_ln_b",
           "w_conv_o", "pool_w", "pool_scale", "w_pool_o", "w_mix_o", "mix_norm_post", "ffn_norm_pre", "w_gate", "w_up", "w_down",
           "ffn_norm_post"]
SHARDED = {"w_in": 2, "w_uq": 2, "w_uk": 2, "w_uv": 2, "w_attn_o": 2, "conv_w": 2, "w_conv_o": 2, "w_pool_o": 2, "w_mix_o": 1,
           "w_gate": 2, "w_up": 2, "w_down": 1}
REPLICATED = [n for n in WEIGHTS if n not in SHARDED]
ROW_PARAMS = [n for n in REPLICATED if n != "pool_w"]
ROWS_MINOR = ("w_in", "w_uq", "w_gate", "w_up", "conv_w")
N_CHIPS = 4
N_MIX_GROUPS = 6
MIX_MATRICES = ("w_in", "w_uq", "w_ukv", "w_attn_o", "w_conv_o", "w_pool_o", "conv_w", "w_mix_o")
CONV_WIRE_ROWS = 32
GROUPS = [(("w_in",), 1), (("w_uq",), 1), (("w_uk", "w_uv"), 1), (("w_attn_o", "w_conv_o", "w_pool_o"), 1), (("conv_w",), 1),
          (("w_mix_o",), 1), (("w_gate", "w_up"), 2), (("w_down",), 1)]


def _join(parts, axis):
    return parts[0] if len(parts) == 1 else jnp.concatenate(parts, axis=axis)


def _split_group(arr, names, axis, shapes):
    out, off = {}, 0
    ax = arr.ndim - 3 + axis
    for n in names:
        size = shapes[n][axis]
        out[n] = lax.slice_in_dim(arr, off, off + size, axis=ax)
        off += size
    return out


def _pack_rows(vectors):
    blocks = []
    for v in vectors:
        for li in range(v.shape[0]):
            blocks.append(jnp.pad(v[li][None, :], ((0, PACK_ROWS - 1), (0, PACK_W - v.shape[1]))))
    return jnp.concatenate(blocks, axis=0)


def _unpack_rows(packed, shapes):
    out, r = [], 0
    for layers, width in shapes:
        out.append(jnp.stack([packed[r + PACK_ROWS * li, :width] for li in range(layers)]))
        r += PACK_ROWS * layers
    return out


def kernel(x, positions, mix_norm_pre, w_in, q_norm, w_uq, kv_norm, w_uk, w_uv, w_attn_o, conv_w, conv_b, conv_ln_g, conv_ln_b, w_conv_o, pool_w, pool_scale, w_pool_o, w_mix_o, mix_norm_post, ffn_norm_pre, w_gate, w_up, w_down, ffn_norm_post, loss_target, m_mix_norm_pre, m_w_in, m_q_norm, m_w_uq, m_kv_norm, m_w_uk, m_w_uv, m_w_attn_o, m_conv_w, m_conv_b, m_conv_ln_g, m_conv_ln_b, m_w_conv_o, m_pool_w, m_pool_scale, m_w_pool_o, m_w_mix_o, m_mix_norm_post, m_ffn_norm_pre, m_w_gate, m_w_up, m_w_down, m_ffn_norm_post, v_mix_norm_pre, v_w_in, v_q_norm, v_w_uq, v_kv_norm, v_w_uk, v_w_uv, v_w_attn_o, v_conv_w, v_conv_b, v_conv_ln_g, v_conv_ln_b, v_w_conv_o, v_pool_w, v_pool_scale, v_w_pool_o, v_w_mix_o, v_mix_norm_post, v_ffn_norm_pre, v_w_gate, v_w_up, v_w_down, v_ffn_norm_post):
    given = dict(mix_norm_pre=mix_norm_pre, w_in=w_in, q_norm=q_norm, w_uq=w_uq, kv_norm=kv_norm, w_uk=w_uk, w_uv=w_uv, w_attn_o=w_attn_o,
                 conv_w=conv_w, conv_b=conv_b, conv_ln_g=conv_ln_g, conv_ln_b=conv_ln_b, w_conv_o=w_conv_o, pool_w=pool_w,
                 pool_scale=pool_scale, w_pool_o=w_pool_o, w_mix_o=w_mix_o, mix_norm_post=mix_norm_post, ffn_norm_pre=ffn_norm_pre,
                 w_gate=w_gate, w_up=w_up, w_down=w_down, ffn_norm_post=ffn_norm_post)
    mom = dict(mix_norm_pre=m_mix_norm_pre, w_in=m_w_in, q_norm=m_q_norm, w_uq=m_w_uq, kv_norm=m_kv_norm, w_uk=m_w_uk, w_uv=m_w_uv,
               w_attn_o=m_w_attn_o, conv_w=m_conv_w, conv_b=m_conv_b, conv_ln_g=m_conv_ln_g, conv_ln_b=m_conv_ln_b, w_conv_o=m_w_conv_o,
               pool_w=m_pool_w, pool_scale=m_pool_scale, w_pool_o=m_w_pool_o, w_mix_o=m_w_mix_o, mix_norm_post=m_mix_norm_post,
               ffn_norm_pre=m_ffn_norm_pre, w_gate=m_w_gate, w_up=m_w_up, w_down=m_w_down, ffn_norm_post=m_ffn_norm_post)
    var = dict(mix_norm_pre=v_mix_norm_pre, w_in=v_w_in, q_norm=v_q_norm, w_uq=v_w_uq, kv_norm=v_kv_norm, w_uk=v_w_uk, w_uv=v_w_uv,
               w_attn_o=v_w_attn_o, conv_w=v_conv_w, conv_b=v_conv_b, conv_ln_g=v_conv_ln_g, conv_ln_b=v_conv_ln_b, w_conv_o=v_w_conv_o,
               pool_w=v_pool_w, pool_scale=v_pool_scale, w_pool_o=v_w_pool_o, w_mix_o=v_w_mix_o, mix_norm_post=v_mix_norm_post,
               ffn_norm_pre=v_ffn_norm_pre, w_gate=v_w_gate, w_up=v_w_up, w_down=v_w_down, ffn_norm_post=v_ffn_norm_post)
    s_len = x.shape[1]
    sharded_names = [n for n in WEIGHTS if n in SHARDED]
    chip = 2 * lax.axis_index("x") + lax.axis_index("y")
    place = jnp.stack([chip, lax.axis_index("c")]).astype(jnp.int32)
    shard_shape = {n: given[n].shape for n in sharded_names}

    mix_groups, ffn_groups = GROUPS[:N_MIX_GROUPS], GROUPS[N_MIX_GROUPS:]
    weight_wire_shape = {n: (N_LAYERS, 2 * CONV_WIRE_ROWS, shard_shape[n][2]) if n == "conv_w" else shard_shape[n] for n in sharded_names}
    grad_wire_shape = {n: (N_LAYERS, CONV_WIRE_ROWS, shard_shape[n][2]) if n == "conv_w" else shard_shape[n] for n in sharded_names}
    pad_rows = lambda a: jnp.pad(a, ((0, CONV_WIRE_ROWS - CONV_W), (0, 0)))

    def weight_wires(groups, li):
        def wire(name):
            a = given[name][li]
            if name == "conv_w":
                hi = a.astype(BF16)
                return jnp.concatenate([pad_rows(hi), pad_rows((a - hi.astype(F32)).astype(BF16))], axis=0)
            return a.astype(BF16)

        return [_join([wire(n) for n in names], axis - 1) for names, axis in groups]

    def full_weights(groups, li, local, gathered):
        p = {n: given[n][li] for n in REPLICATED}
        p["pool_w"] = p["pool_w"].astype(BF16)
        for (names, axis), loc, got in zip(groups, local, gathered):
            got = lax.dynamic_update_slice(got, loc[None], (chip, 0, 0))
            per_chip = [_split_group(got[j], names, axis, weight_wire_shape) for j in range(N_CHIPS)]
            for n in names:
                parts = [pc[n] for pc in per_chip]
                if n == "conv_w":
                    parts = [q[:CONV_W].astype(F32) + q[CONV_WIRE_ROWS:CONV_WIRE_ROWS + CONV_W].astype(F32) for q in parts]
                p[n] = jnp.concatenate(parts, axis=SHARDED[n] - 1)
        return align_weights(p)

    def chip_partials(groups, g, tag):
        wires = []
        for names, axis in groups:
            split = {n: jnp.split(pad_rows(g[n]) if n == "conv_w" else g[n], N_CHIPS, axis=SHARDED[n] - 1) for n in names}
            wires.append(jnp.stack([_join([split[n][j].astype(BF16) for n in names], axis - 1) for j in range(N_CHIPS)]))
        theirs = sibling_swap(wires, "sibling_swap_" + tag)
        return [sibling_sum(w, t, place, f"sibling_sum_{tag}_{i}") for i, (w, t) in enumerate(zip(wires, theirs))]

    tabs = rope_tables(positions.reshape(s_len, 1), s_len)
    in_groups, rest_groups = mix_groups[:1], mix_groups[1:]
    loc_in0, loc_rest0 = weight_wires(in_groups, 0), weight_wires(rest_groups, 0)
    w_in0 = full_weights(in_groups, 0, loc_in0, run_exchange(GatherShards(loc_in0), "gather_w_in_l0"))
    head = (GatherShards(loc_rest0), lambda arrived: full_weights(rest_groups, 0, loc_rest0, arrived))
    loc_f0, loc_m1 = weight_wires(ffn_groups, 0), weight_wires(mix_groups, 1)
    h, sv_m0, got = mixer_fwd(x[0], tabs, w_in0, "_l0", GatherShards(loc_f0 + loc_m1), head)
    w_m0 = sv_m0["w"]
    w_f0 = full_weights(ffn_groups, 0, loc_f0, got[:len(loc_f0)])
    w_m1 = full_weights(mix_groups, 1, loc_m1, got[len(loc_f0):])
    h, sv_f0, h_normed = ffn_fwd(h, sv_m0["hn"], w_f0, "_l0", w_m1["mix_norm_pre"])
    loc_f1 = weight_wires(ffn_groups, 1)
    h, sv_m1, got = mixer_fwd(h, tabs, w_m1, "_l1", GatherShards(loc_f1), h=h_normed)
    w_f1 = full_weights(ffn_groups, 1, loc_f1, got)
    h, sv_f1, _ = ffn_fwd(h, sv_m1["hn"], w_f1, "_l1")
    dh, loss_local = loss_head(h, loss_target[0])
    loss = lax.psum(loss_local[0, 0], MESH_AXES)

    dh, g_f1, post_m1 = ffn_bwd(dh, sv_f1, w_f1, "_l1", sv_m1["mo"])
    g_f1 = unalign_grads(g_f1)
    p_f1 = chip_partials(ffn_groups, g_f1, "ffn_l1")
    dh, g_m1, o_f1, post_f0 = mixer_bwd(dh, sv_m1, tabs, w_m1, "_l1", post_m1, ChipExchange(p_f1), prev=(sv_f0["y"], w_f0["ffn_norm_post"]))
    g_m1 = unalign_grads(g_m1)
    p_m1 = chip_partials(mix_groups, g_m1, "mix_l1")
    dh, g_f0, post_m0 = ffn_bwd(dh, sv_f0, w_f0, "_l0", sv_m0["mo"], post_f0)
    g_f0 = unalign_grads(g_f0)
    p_f0 = chip_partials(ffn_groups, g_f0, "ffn_l0")
    last = {}

    def tail(g):
        last["p"] = chip_partials(mix_groups, unalign_grads({n: g[n] for n in MIX_MATRICES}), "mix_l0")
        return ChipExchange(last["p"])

    grad_x, g_m0, (got, o_m0), _ = mixer_bwd(dh, sv_m0, tabs, w_m0, "_l0", post_m0, ChipExchange(p_m1 + p_f0), tail)
    o_m1, o_f0 = got[:len(p_m1)], got[len(p_m1):]
    p_m0 = last["p"]
    grads = [{**g_m0, **g_f0}, {**g_m1, **g_f1}]
    grad_full = {n: jnp.stack([g[n].reshape(given[n].shape[1:]) for g in grads]) for n in REPLICATED}
    replicated = [_pack_rows([grad_full[n] for n in ROW_PARAMS]), grad_full["pool_w"].reshape(-1, POOL_GD)]
    device = 2 * chip + lax.axis_index("c")
    rows_all, pool_w_all = [lax.dynamic_update_slice(a, mine[None], (device, 0, 0))
                            for a, mine in zip(run_exchange(GatherAll(replicated), "gather_replicated"), replicated)]

    sums = {}
    for groups, base, per_layer in ((ffn_groups, N_MIX_GROUPS, ((1, p_f1, o_f1), (0, p_f0, o_f0))), (mix_groups, 0, ((1, p_m1, o_m1), (0, p_m0, o_m0)))):
        for li, parts, others in per_layer:
            for i, (p, o) in enumerate(zip(parts, others)):
                sums[base + i] = chip_sum(p, o, place, li, sums.get(base + i), f"chip_sum_{base + i}_l{li}")
    g_shard = {}
    for (names, axis), s in zip(GROUPS, sibling_gather([sums[i] for i in range(len(GROUPS))])):
        g_shard.update(_split_group(s, names, axis, grad_wire_shape))
    g_shard["conv_w"] = g_shard["conv_w"][:, :CONV_W]

    row_shapes = [given[n].shape for n in ROW_PARAMS]
    g_rows = sum_devices(rows_all, "row_params_sum")
    g_pool_w = sum_devices(pool_w_all, "pool_w_sum")
    g_rep = dict(zip(ROW_PARAMS, _unpack_rows(g_rows, row_shapes)))
    g_rep["pool_w"] = g_pool_w.reshape(given["pool_w"].shape)

    g_out, d_out, m_out, v_out = {}, {}, {}, {}
    for n in sharded_names + ["pool_w"]:
        shp = given[n].shape
        three_d = (shp[0], int(np.prod(shp[1:-1])), shp[-1])
        g_n = g_shard[n] if n in SHARDED else g_rep[n]
        view = (lambda a: jnp.swapaxes(a.reshape(three_d), 1, 2)) if n in ROWS_MINOR else (lambda a: a.reshape(three_d))
        back = (lambda a: jnp.swapaxes(a, 1, 2).reshape(shp)) if n in ROWS_MINOR else (lambda a: a.reshape(shp))
        d, mn, vn = adamw(view(given[n]), view(g_n), view(mom[n]), view(var[n]), "adamw_" + n)
        g_out[n], d_out[n], m_out[n], v_out[n] = g_n, back(d), back(mn), back(vn)
    rd, rm, rv = adamw(_pack_rows([given[n] for n in ROW_PARAMS])[None], g_rows[None], _pack_rows([mom[n] for n in ROW_PARAMS])[None],
                       _pack_rows([var[n] for n in ROW_PARAMS])[None], "adamw_row_params")
    for n, d, mn, vn in zip(ROW_PARAMS, *[_unpack_rows(a[0], row_shapes) for a in (rd, rm, rv)]):
        g_out[n], d_out[n], m_out[n], v_out[n] = g_rep[n], d, mn, vn

    return (loss, grad_x[None], *[g_out[n] for n in WEIGHTS], *[d_out[n] for n in WEIGHTS], *[m_out[n] for n in WEIGHTS],
            *[v_out[n] for n in WEIGHTS])
```

```python
import functools
import math

import numpy as np
import jax
import jax.numpy as jnp
from jax import lax
from jax.experimental import pallas as pl
from jax.experimental.pallas import tpu as pltpu

F32, BF16 = jnp.float32, jnp.bfloat16

D_MODEL = 1024
N_HEADS = 8
NOPE, ROPE, VDIM = 64, 32, 64
HALF_ROPE = ROPE // 2
Q_RANK, KV_RANK = 384, 256
CONV_C, CONV_W = 512, 31
POOL_C, POOL_G, POOL_GD = 512, 4, 128
POOL_WINDOWS = (2, 4, 8, 16)
D_FF = 2816
FF_HALF = D_FF // 2
N_LAYERS = 2
EPS = 1e-6
ROPE_THETA = 10000.0
ATT_SCALE = 1.0 / math.sqrt(NOPE + ROPE)
O_Q, O_KV, O_KR, O_CONV, O_POOL, O_GATE, D_IN = 0, 384, 640, 672, 1696, 2208, 5280

LANE = 128
HP = 128
ZG, ZC, ZP, ZA, ZW = 0, 3072, 4096, 4608, 5376
ZA_W = Q_RANK + KV_RANK + HP
KR_LANE = NOPE
HW = N_HEADS * HP

ADAM_LR, ADAM_B1, ADAM_B2, ADAM_EPS, ADAM_WD, ADAM_STEP = 0.001, 0.9, 0.999, 1e-08, 0.01, 10

ROW_TILE = 512
WIDE_ROW_TILE = 256
ATT_TILE_FWD = 1024
ATT_TILE_BWD = 512
ATT_HEADS = 4
CONV_CHUNK = 256
MM_TM, MM_TN, MM_TK = 1024, 1408, 1024
MM_TILE_MAX = 2048
FFN_TM = 512
MM_VMEM_BUDGET = 40 * 1024 * 1024
HBM_BYTES_PER_US = 3.0e6
GRID_STEP_US = 0.35
VMEM_LIMIT = 56 * 1024 * 1024
SUM_TILE_BYTES = 3 * 1024 * 1024
ADAM_TILE_BYTES = 2 * 1024 * 1024

HALF_ALIGN = 16
MESH_AXES = ("x", "y", "c")
PACK_W = 1024
PACK_ROWS = 8


def _cparams(sem):
    return pltpu.CompilerParams(dimension_semantics=sem, vmem_limit_bytes=VMEM_LIMIT)


def _tile(n, target):
    if n <= target:
        return n
    best = None
    for t in range(LANE, target + 1, LANE):
        if n % t == 0:
            best = t
    assert best is not None, (n, target)
    return best


def _mm_tiles(m, n, k, a_bytes, b_bytes, out_bytes):
    divs = lambda d: sorted({t for t in range(LANE, min(d, MM_TILE_MAX) + 1, LANE) if d % t == 0} | ({d} if d <= MM_TILE_MAX else set()))
    best = None
    for tm in divs(m):
        for tn in divs(n):
            blocks = tm * k * a_bytes + k * tn * b_bytes + tm * tn * out_bytes
            if 2 * blocks + tm * tn * 4 > MM_VMEM_BUDGET:
                continue
            steps = (m // tm) * (n // tn)
            for rows_outer in (True, False):
                moved = (m * k * a_bytes + k * n * b_bytes * (m // tm)) if rows_outer else (k * n * b_bytes + m * k * a_bytes * (n // tn))
                cost = (moved + m * n * out_bytes + blocks) / HBM_BYTES_PER_US + steps * GRID_STEP_US
                if best is None or cost < best[0]:
                    best = (cost, tm, tn, rows_outer)
    if best is not None:
        return best[1], best[2], k, best[3]
    return _tile(m, MM_TM), _tile(n, MM_TN), _tile(k, MM_TK), True


def mm(a, b, *, ta=False, tb=False, out_dtype=F32, name, carry=None):
    m, k = (a.shape[1], a.shape[0]) if ta else a.shape
    n, k2 = b.shape if tb else (b.shape[1], b.shape[0])
    assert k == k2, (a.shape, b.shape, ta, tb)
    tm, tn, tk, rows_outer = _mm_tiles(m, n, k, a.dtype.itemsize, b.dtype.itemsize, jnp.dtype(out_dtype).itemsize)
    nk = k // tk
    dims = (((0 if ta else 1,), (1 if tb else 0,)), ((), ()))
    grid = (m // tm, n // tn, nk) if rows_outer else (n // tn, m // tm, nk)

    def body(*refs):
        (a_ref, b_ref, o_ref, *acc), start, finish = _carried(carry, refs, 2, 1, 0 if nk == 1 else 1)
        ids = [pl.program_id(d) for d in range(3)]
        if start is not None:
            pl.when((ids[0] == 0) & (ids[1] == 0) & (ids[2] == 0))(start)
        part = lax.dot_general(a_ref[...].astype(BF16), b_ref[...].astype(BF16), dims, preferred_element_type=F32)
        if nk == 1:
            o_ref[...] = part.astype(o_ref.dtype)
        else:
            (acc_ref,) = acc
            kk = ids[2]

            @pl.when(kk == 0)
            def _():
                acc_ref[...] = part

            @pl.when(kk > 0)
            def _():
                acc_ref[...] += part

            @pl.when(kk == nk - 1)
            def _():
                o_ref[...] = acc_ref[...].astype(o_ref.dtype)
        if finish is not None:
            pl.when((ids[0] == grid[0] - 1) & (ids[1] == grid[1] - 1) & (ids[2] == nk - 1))(finish)

    ij = (lambda g0, g1: (g0, g1)) if rows_outer else (lambda g0, g1: (g1, g0))

    def a_map(g0, g1, kk):
        i, _ = ij(g0, g1)
        return (kk, i) if ta else (i, kk)

    def b_map(g0, g1, kk):
        _, j = ij(g0, g1)
        return (j, kk) if tb else (kk, j)

    ex_in_specs, ex_out_specs, ex_out_shape, ex_scratch, ex_inputs = _carry_specs(carry)
    a_spec = pl.BlockSpec((tk, tm) if ta else (tm, tk), a_map)
    b_spec = pl.BlockSpec((tn, tk) if tb else (tk, tn), b_map)
    out, *carried = pl.pallas_call(
        body,
        grid=grid,
        in_specs=[a_spec, b_spec] + ex_in_specs,
        out_specs=[pl.BlockSpec((tm, tn), lambda g0, g1, kk: ij(g0, g1))] + ex_out_specs,
        out_shape=[jax.ShapeDtypeStruct((m, n), out_dtype)] + ex_out_shape,
        scratch_shapes=([] if nk == 1 else [pltpu.VMEM((tm, tn), F32)]) + ex_scratch,
        compiler_params=_cparams(("arbitrary", "arbitrary", "arbitrary") if carry is not None else ("parallel", "parallel", "arbitrary")),
        name=name,
    )(a, b, *ex_inputs)
    return out if carry is None else (out, carried)


def ffn_in(hn, w_gu, name):
    s_len, k = hn.shape
    tm = min(FFN_TM, s_len)

    def body(a_ref, b_ref, gu_ref, act_ref):
        r = jnp.dot(a_ref[...], b_ref[...], preferred_element_type=F32)
        gu_ref[...] = r.astype(gu_ref.dtype)
        act_ref[...] = (_silu(r[:, :FF_HALF]) * r[:, FF_HALF:]).astype(act_ref.dtype)

    return pl.pallas_call(
        body,
        grid=(2, s_len // tm),
        in_specs=[pl.BlockSpec((tm, k), lambda j, i: (i, 0)), pl.BlockSpec((k, 2 * FF_HALF), lambda j, i: (0, j))],
        out_specs=[pl.BlockSpec((tm, 2 * FF_HALF), lambda j, i: (i, j)), pl.BlockSpec((tm, FF_HALF), lambda j, i: (i, j))],
        out_shape=[jax.ShapeDtypeStruct((s_len, 2 * D_FF), BF16), jax.ShapeDtypeStruct((s_len, D_FF), BF16)],
        compiler_params=_cparams(("arbitrary", "parallel")),
        name=name,
    )(hn, w_gu)


def ffn_out_dx(d_y, w_down, gu, name):
    s_len, k = d_y.shape
    tm = min(FFN_TM, s_len)

    def body(a_ref, b_ref, gu_ref, dgu_ref):
        da = lax.dot_general(a_ref[...], b_ref[...], NT_DIMS, preferred_element_type=F32)
        gt = gu_ref[:, :FF_HALF].astype(F32)
        up = gu_ref[:, FF_HALF:].astype(F32)
        sg = jax.nn.sigmoid(gt)
        dgu_ref[:, :FF_HALF] = (da * up * sg * (1.0 + gt * (1.0 - sg))).astype(dgu_ref.dtype)
        dgu_ref[:, FF_HALF:] = (da * gt * sg).astype(dgu_ref.dtype)

    pair = pl.BlockSpec((tm, 2 * FF_HALF), lambda j, i: (i, j))
    return pl.pallas_call(
        body,
        grid=(2, s_len // tm),
        in_specs=[pl.BlockSpec((tm, k), lambda j, i: (i, 0)), pl.BlockSpec((FF_HALF, k), lambda j, i: (j, 0)), pair],
        out_specs=pair,
        out_shape=jax.ShapeDtypeStruct((s_len, 2 * D_FF), BF16),
        compiler_params=_cparams(("parallel", "arbitrary")),
        name=name,
    )(d_y, w_down, gu)


def rowwise(name, body, rows, row_ins, full_ins, row_outs, acc_outs=(), into=None, tile=None):
    tile = min(tile or ROW_TILE, rows)
    into = into or {}
    in_specs = [pl.BlockSpec((tile, w), lambda i, cb=cb: (i, cb)) for _, w, cb in row_ins]
    in_specs += [pl.BlockSpec(a.shape, lambda i, nd=a.ndim: (0,) * nd) for a in full_ins]
    in_specs += [ANY for _ in into]
    n_in = len(row_ins) + len(full_ins)
    aliases = {n_in + k: oi for k, oi in enumerate(into)}
    out_specs, out_shape = [], []
    for ro in row_outs:
        w, dt, full_w, cb = ro if len(ro) == 4 else (*ro, ro[0], 0)
        out_specs.append(pl.BlockSpec((tile, w), lambda i, cb=cb: (i, cb)))
        out_shape.append(jax.ShapeDtypeStruct((rows, full_w), dt))
    out_specs += [pl.BlockSpec(s, lambda i, nd=len(s): (0,) * nd) for s, _ in acc_outs]
    out_shape += [jax.ShapeDtypeStruct(s, dt) for s, dt in acc_outs]
    n_refs = n_in

    def call_body(*refs):
        body(*refs[:n_refs], *refs[n_refs + len(into):])

    outs = pl.pallas_call(
        call_body,
        grid=(rows // tile,),
        in_specs=in_specs,
        out_specs=out_specs,
        out_shape=out_shape,
        input_output_aliases=aliases,
        compiler_params=_cparams(("arbitrary",)),
        name=name,
    )(*[a for a, _, _ in row_ins], *full_ins, *into.values())
    return outs


def _whole(a):
    return (a, a.shape[1], 0)


def _acc(ref, val):
    @pl.when(pl.program_id(0) == 0)
    def _():
        ref[...] = val

    @pl.when(pl.program_id(0) > 0)
    def _():
        ref[...] += val


def _rms(x, g):
    return x * lax.rsqrt(jnp.mean(x * x, axis=-1, keepdims=True) + EPS) * g


def _layer_norm(x, g, b):
    mu = jnp.mean(x, axis=-1, keepdims=True)
    xc = x - mu
    return xc * lax.rsqrt(jnp.mean(xc * xc, axis=-1, keepdims=True) + EPS) * g + b


def _silu(x):
    return x * jax.nn.sigmoid(x)


def _rope(x, cc, sa, sb):
    return x * cc + pltpu.roll(x, HALF_ROPE, 1) * sa + pltpu.roll(x, HP - HALF_ROPE, 1) * sb


def _rope_t(dy, cc, sa, sb):
    return dy * cc + pltpu.roll(dy * sa, HP - HALF_ROPE, 1) + pltpu.roll(dy * sb, HALF_ROPE, 1)


def rope_tables(pos_col, rows):
    lane = np.arange(HP)
    idx = np.where(lane < KR_LANE + HALF_ROPE, lane - KR_LANE, lane - KR_LANE - HALF_ROPE)
    in_rope = (lane >= KR_LANE) & (lane < KR_LANE + ROPE)
    inv_freq = (np.float32(ROPE_THETA) ** (-np.arange(0, ROPE, 2, dtype=np.float32) / np.float32(ROPE))).astype(np.float32)
    freq_row = np.where(in_rope, inv_freq[np.clip(idx, 0, HALF_ROPE - 1)], 0.0).astype(np.float32)[None, :]
    first = ((lane >= KR_LANE) & (lane < KR_LANE + HALF_ROPE)).astype(np.float32)[None, :]
    second = ((lane >= KR_LANE + HALF_ROPE) & (lane < KR_LANE + ROPE)).astype(np.float32)[None, :]

    def body(pos_ref, f_ref, a_ref, b_ref, cc_ref, sa_ref, sb_ref):
        ang = pos_ref[...].astype(F32) * f_ref[...]
        s = jnp.sin(ang)
        cc_ref[...] = jnp.cos(ang)
        sa_ref[...] = s * b_ref[...]
        sb_ref[...] = -s * a_ref[...]

    return rowwise("rope_tables", body, rows, [_whole(pos_col)], [jnp.asarray(freq_row), jnp.asarray(first), jnp.asarray(second)],
                   [(HP, F32)] * 3)


def _causal_mask(t):
    r = lax.broadcasted_iota(jnp.int32, (t, t), 0)
    c = lax.broadcasted_iota(jnp.int32, (t, t), 1)
    return r, c


NT_DIMS = (((1,), (1,)), ((), ()))


def _carried(carry, refs, n_in, n_out, n_scratch):
    if carry is None:
        return refs, None, None
    ni, no = len(carry.ins), len(carry.outs)
    own_in, ex_in = refs[:n_in], refs[n_in:n_in + ni]
    own_out, ex_out = refs[n_in + ni:n_in + ni + n_out], refs[n_in + ni + n_out:n_in + ni + n_out + no]
    scratch = refs[n_in + ni + n_out + no:]
    sems = scratch[n_scratch:]
    return (*own_in, *own_out, *scratch[:n_scratch]), (lambda: carry.start(ex_in, ex_out, *sems)), (lambda: carry.finish(ex_in, ex_out, *sems))


def _carry_specs(carry):
    if carry is None:
        return [], [], [], [], []
    sems = [pltpu.SemaphoreType.DMA((carry.n_sems,)), pltpu.SemaphoreType.DMA((carry.n_sems,))]
    return [ANY] * len(carry.ins), [ANY] * len(carry.outs), list(carry.outs), sems, list(carry.ins)


def attention_fwd(q, k, v, name, carry=None):
    s_len = q.shape[0]
    t = min(ATT_TILE_FWD, s_len)
    nb = s_len // t
    hb = ATT_HEADS
    w = hb * HP
    nh = N_HEADS // hb

    def body(*refs):
        (q_ref, k_ref, v_ref, o_ref, lse_ref, m_sc, acc_sc), start, finish = _carried(carry, refs, 3, 2, 2)
        qi = pl.program_id(1)
        if start is not None:
            pl.when((pl.program_id(0) == 0) & (qi == 0))(start)
        m_sc[...] = jnp.full_like(m_sc, -jnp.inf)
        acc_sc[...] = jnp.zeros_like(acc_sc)

        def block(j, masked):
            ks = pl.ds(pl.multiple_of(j * t, t), t)
            for hh in range(hb):
                ls = slice(hh * HP, (hh + 1) * HP)
                s = lax.dot_general(q_ref[:, ls], k_ref[ks, ls], NT_DIMS, preferred_element_type=F32) * ATT_SCALE
                if masked:
                    r, c = _causal_mask(t)
                    s = jnp.where(c <= r, s, -jnp.inf)
                m_old = m_sc[hh]
                m_new = jnp.maximum(m_old, jnp.max(s, axis=-1, keepdims=True))
                p = jnp.exp(s - m_new)
                acc_sc[hh] = jnp.exp(m_old - m_new) * acc_sc[hh] + jnp.dot(p.astype(BF16), v_ref[ks, ls], preferred_element_type=F32)
                m_sc[hh] = m_new

        def loop_body(j, carry):
            block(j, False)
            return carry

        lax.fori_loop(0, qi, loop_body, 0)
        block(qi, True)
        lane = lax.broadcasted_iota(jnp.int32, (t, HP), 1)
        for hh in range(hb):
            acc = acc_sc[hh]
            l = jnp.sum(jnp.where(lane == VDIM, acc, 0.0), axis=-1, keepdims=True)
            o_ref[:, hh * HP:(hh + 1) * HP] = jnp.where(lane < VDIM, acc / l, 0.0).astype(o_ref.dtype)
            lse_ref[hh] = m_sc[hh] + jnp.log(l)
        if finish is not None:
            pl.when((pl.program_id(0) == nh - 1) & (qi == nb - 1))(finish)

    ex_in_specs, ex_out_specs, ex_out_shape, ex_scratch, ex_inputs = _carry_specs(carry)
    resident = pl.BlockSpec((s_len, w), lambda h, qi: (0, h))
    o, lse, *carried = pl.pallas_call(
        body,
        grid=(nh, nb),
        in_specs=[pl.BlockSpec((t, w), lambda h, qi: (qi, h)), resident, resident] + ex_in_specs,
        out_specs=[pl.BlockSpec((t, w), lambda h, qi: (qi, h)), pl.BlockSpec((hb, t, 1), lambda h, qi: (h, qi, 0))] + ex_out_specs,
        out_shape=[jax.ShapeDtypeStruct((s_len, HW), BF16), jax.ShapeDtypeStruct((N_HEADS, s_len, 1), F32)] + ex_out_shape,
        scratch_shapes=[pltpu.VMEM((hb, t, 1), F32), pltpu.VMEM((hb, t, HP), F32)] + ex_scratch,
        compiler_params=_cparams(("arbitrary", "arbitrary")),
        name=name,
    )(q, k, v, *ex_inputs)
    return o, lse, carried


def attention_delta(do, o):
    s_len = do.shape[0]
    t = min(ROW_TILE, s_len)

    def body(do_ref, o_ref, d_ref):
        prod = do_ref[...].astype(F32) * o_ref[...].astype(F32)
        for h in range(N_HEADS):
            d_ref[h] = jnp.sum(prod[:, h * HP:(h + 1) * HP], axis=-1, keepdims=True)

    return pl.pallas_call(
        body,
        grid=(s_len // t,),
        in_specs=[pl.BlockSpec((t, HW), lambda i: (i, 0))] * 2,
        out_specs=pl.BlockSpec((N_HEADS, t, 1), lambda i: (0, i, 0)),
        out_shape=jax.ShapeDtypeStruct((N_HEADS, s_len, 1), F32),
        compiler_params=_cparams(("arbitrary",)),
        name="attention_delta",
    )(do, o)


TN_DIMS = (((0,), (0,)), ((), ()))


def attention_bwd(q, k, v, do, lse_row, delta_row, name, carry=None):
    s_len = q.shape[0]
    t = min(ATT_TILE_BWD, s_len)
    nb = s_len // t
    hb = ATT_HEADS
    w = hb * HP
    nh = N_HEADS // hb

    def body(*refs):
        (q_ref, k_ref, v_ref, do_ref, lse_ref, dl_ref, dq_ref, dk_ref, dv_ref, dk_sc, dv_sc), start, finish = _carried(carry, refs, 6, 3, 2)
        ki = pl.program_id(1)
        if start is not None:
            pl.when((pl.program_id(0) == 0) & (ki == 0))(start)

        @pl.when(ki == 0)
        def _():
            dq_ref[...] = jnp.zeros_like(dq_ref)

        dk_sc[...] = jnp.zeros_like(dk_sc)
        dv_sc[...] = jnp.zeros_like(dv_sc)

        def block(j, masked):
            qs = pl.ds(pl.multiple_of(j * t, t), t)
            for hh in range(hb):
                ls = slice(hh * HP, (hh + 1) * HP)
                qb = q_ref[qs, ls]
                dob = do_ref[qs, ls]
                kb = k_ref[:, ls]
                st = lax.dot_general(kb, qb, NT_DIMS, preferred_element_type=F32) * ATT_SCALE
                pt = jnp.exp(st - lse_ref[hh, j])
                if masked:
                    r, c = _causal_mask(t)
                    pt = jnp.where(r <= c, pt, 0.0)
                dv_sc[hh] += jnp.dot(pt.astype(BF16), dob, preferred_element_type=F32)
                dpt = lax.dot_general(v_ref[:, ls], dob, NT_DIMS, preferred_element_type=F32)
                dst = (pt * (dpt - dl_ref[hh, j]) * ATT_SCALE).astype(BF16)
                dk_sc[hh] += jnp.dot(dst, qb, preferred_element_type=F32)
                dq_ref[qs, ls] += lax.dot_general(dst, kb, TN_DIMS, preferred_element_type=F32)

        block(ki, True)

        def loop_body(j, carry):
            block(j, False)
            return carry

        lax.fori_loop(ki + 1, nb, loop_body, 0)
        for hh in range(hb):
            ls = slice(hh * HP, (hh + 1) * HP)
            dk_ref[:, ls] = dk_sc[hh].astype(dk_ref.dtype)
            dv_ref[:, ls] = dv_sc[hh].astype(dv_ref.dtype)
        if finish is not None:
            pl.when((pl.program_id(0) == nh - 1) & (ki == nb - 1))(finish)

    ex_in_specs, ex_out_specs, ex_out_shape, ex_scratch, ex_inputs = _carry_specs(carry)
    k_spec = pl.BlockSpec((t, w), lambda h, ki: (ki, h))
    resident = pl.BlockSpec((s_len, w), lambda h, ki: (0, h))
    row_spec = pl.BlockSpec((hb, nb, 1, t), lambda h, ki: (h, 0, 0, 0))
    dq, dk, dv, *carried = pl.pallas_call(
        body,
        grid=(nh, nb),
        in_specs=[resident, k_spec, k_spec, resident, row_spec, row_spec] + ex_in_specs,
        out_specs=[resident, k_spec, k_spec] + ex_out_specs,
        out_shape=[jax.ShapeDtypeStruct((s_len, HW), F32), jax.ShapeDtypeStruct((s_len, HW), BF16), jax.ShapeDtypeStruct((s_len, HW), BF16)]
        + ex_out_shape,
        scratch_shapes=[pltpu.VMEM((hb, t, HP), F32), pltpu.VMEM((hb, t, HP), F32)] + ex_scratch,
        compiler_params=_cparams(("arbitrary", "arbitrary")),
        name=name,
    )(q, k, v, do, lse_row, delta_row, *ex_inputs)
    return dq, dk, dv, carried


CONV_PAD = 32


def conv_fwd(z, conv_w, conv_b):
    s_len = z.shape[0]
    ch = min(CONV_CHUNK, s_len)

    def body(ag_ref, w_ref, b_ref, c_ref, pad_ref):
        pad_ref[0:CONV_PAD, :] = jnp.zeros((CONV_PAD, LANE), F32)
        pad_ref[CONV_PAD:CONV_PAD + s_len, :] = ag_ref[:, 0:LANE] * jax.nn.sigmoid(ag_ref[:, LANE:2 * LANE])

        def chunk(i, carry):
            base = pl.multiple_of(i * ch, ch)
            acc = jnp.zeros((ch, LANE), F32) + b_ref[...]
            for kk in range(CONV_W):
                acc = acc + pad_ref[pl.ds(base + CONV_PAD - (CONV_W - 1) + kk, ch), :] * w_ref[kk:kk + 1, :]
            c_ref[pl.ds(base, ch), :] = acc
            return carry

        lax.fori_loop(0, s_len // ch, chunk, 0)

    nblk = CONV_C // LANE
    return pl.pallas_call(
        body,
        grid=(nblk,),
        in_specs=[pl.BlockSpec((s_len, 2 * LANE), lambda j: (0, ZC // (2 * LANE) + j)),
                  pl.BlockSpec((CONV_W, LANE), lambda j: (0, j)), pl.BlockSpec((1, LANE), lambda j: (0, j))],
        out_specs=pl.BlockSpec((s_len, LANE), lambda j: (0, j)),
        out_shape=jax.ShapeDtypeStruct((s_len, CONV_C), F32),
        scratch_shapes=[pltpu.VMEM((s_len + CONV_PAD, LANE), F32)],
        compiler_params=_cparams(("arbitrary",)),
        name="conv_fwd",
    )(z, conv_w, conv_b)


def conv_bwd(z, dc, conv_w, dz):
    s_len = z.shape[0]
    ch = min(CONV_CHUNK, s_len)

    def body(ag_ref, dc_ref, w_ref, dz_in, dag_ref, dw_ref, db_ref, pad_ref, dpad_ref, wacc_ref):
        del dz_in
        pad_ref[0:CONV_PAD, :] = jnp.zeros((CONV_PAD, LANE), F32)
        pad_ref[CONV_PAD:CONV_PAD + s_len, :] = ag_ref[:, 0:LANE] * jax.nn.sigmoid(ag_ref[:, LANE:2 * LANE])
        dpad_ref[0:s_len, :] = dc_ref[...]
        dpad_ref[s_len:s_len + CONV_PAD, :] = jnp.zeros((CONV_PAD, LANE), F32)
        wacc_ref[...] = jnp.zeros_like(wacc_ref)
        db_ref[...] = jnp.sum(dc_ref[...], axis=0, keepdims=True)

        def chunk(i, carry):
            base = pl.multiple_of(i * ch, ch)
            dcc = dpad_ref[pl.ds(base, ch), :]
            dh = jnp.zeros((ch, LANE), F32)
            for kk in range(CONV_W):
                dh = dh + dpad_ref[pl.ds(base + (CONV_W - 1) - kk, ch), :] * w_ref[kk:kk + 1, :]
                prod = dcc * pad_ref[pl.ds(base + CONV_PAD - (CONV_W - 1) + kk, ch), :]
                wacc_ref[kk * 8:(kk + 1) * 8, :] += prod.reshape(ch // 8, 8, LANE).sum(axis=0)
            a = ag_ref[pl.ds(base, ch), 0:LANE]
            sgc = jax.nn.sigmoid(ag_ref[pl.ds(base, ch), LANE:2 * LANE])
            dag_ref[pl.ds(base, ch), 0:LANE] = (dh * sgc).astype(dag_ref.dtype)
            dag_ref[pl.ds(base, ch), LANE:2 * LANE] = (dh * a * sgc * (1.0 - sgc)).astype(dag_ref.dtype)
            return carry

        lax.fori_loop(0, s_len // ch, chunk, 0)
        for kk in range(CONV_W):
            dw_ref[kk:kk + 1, :] = jnp.sum(wacc_ref[kk * 8:(kk + 1) * 8, :], axis=0, keepdims=True)

    nblk = CONV_C // LANE
    pair = pl.BlockSpec((s_len, 2 * LANE), lambda j: (0, ZC // (2 * LANE) + j))
    return pl.pallas_call(
        body,
        grid=(nblk,),
        in_specs=[pair, pl.BlockSpec((s_len, LANE), lambda j: (0, j)), pl.BlockSpec((CONV_W, LANE), lambda j: (0, j)), ANY],
        out_specs=[pair, pl.BlockSpec((CONV_W, LANE), lambda j: (0, j)), pl.BlockSpec((1, LANE), lambda j: (0, j))],
        out_shape=[jax.ShapeDtypeStruct(dz.shape, dz.dtype), jax.ShapeDtypeStruct((CONV_W, CONV_C), F32), jax.ShapeDtypeStruct((1, CONV_C), F32)],
        scratch_shapes=[pltpu.VMEM((s_len + CONV_PAD, LANE), F32), pltpu.VMEM((s_len + CONV_PAD, LANE), F32),
                        pltpu.VMEM((CONV_W * 8, LANE), F32)],
        input_output_aliases={3: 0},
        compiler_params=_cparams(("arbitrary",)),
        name="conv_bwd",
    )(z, dc, conv_w, dz)


POOL_PAD = 16


def _pool_count(base, ch, w):
    t = base + lax.broadcasted_iota(jnp.int32, (ch, 1), 0)
    return jnp.minimum(t + 1, w).astype(F32)


def pool_fwd(z, pool_w, pool_scale):
    s_len = z.shape[0]
    ch = min(CONV_CHUNK, s_len)

    def body(u_ref, pw_ref, sc_ref, m_ref, pad_ref):
        gi = pl.program_id(0)
        pad_ref[0:POOL_PAD, :] = jnp.zeros((POOL_PAD, LANE), F32)
        pad_ref[POOL_PAD:POOL_PAD + s_len, :] = u_ref[...]

        def run(w):
            def chunk(i, carry):
                base = pl.multiple_of(i * ch, ch)
                acc = jnp.zeros((ch, LANE), F32)
                for j in range(w):
                    acc = acc + pad_ref[pl.ds(base + POOL_PAD - j, ch), :]
                d = acc / _pool_count(base, ch, w) - u_ref[pl.ds(base, ch), :]
                md = jnp.dot(d.astype(BF16), pw_ref[0], preferred_element_type=F32)
                m_ref[pl.ds(base, ch), :] = (md * sc_ref[...]).astype(m_ref.dtype)
                return carry

            lax.fori_loop(0, s_len // ch, chunk, 0)

        for g, w in enumerate(POOL_WINDOWS):
            pl.when(gi == g)(functools.partial(run, w))

    return pl.pallas_call(
        body,
        grid=(POOL_G,),
        in_specs=[pl.BlockSpec((s_len, LANE), lambda g: (0, ZP // LANE + g)), pl.BlockSpec((1, POOL_GD, POOL_GD), lambda g: (g, 0, 0)),
                  pl.BlockSpec((1, LANE), lambda g: (0, g))],
        out_specs=pl.BlockSpec((s_len, LANE), lambda g: (0, g)),
        out_shape=jax.ShapeDtypeStruct((s_len, POOL_C), BF16),
        scratch_shapes=[pltpu.VMEM((s_len + POOL_PAD, LANE), F32)],
        compiler_params=_cparams(("arbitrary",)),
        name="pool_fwd",
    )(z, pool_w, pool_scale)


def pool_bwd(z, dm, pool_w, pool_scale, dz):
    s_len = z.shape[0]
    ch = min(CONV_CHUNK, s_len)

    def body(u_ref, dm_ref, pw_ref, sc_ref, dz_in, du_ref, dpw_ref, dsc_ref, pad_ref, epad_ref, dd_ref, sacc_ref):
        del dz_in
        gi = pl.program_id(0)
        pad_ref[0:POOL_PAD, :] = jnp.zeros((POOL_PAD, LANE), F32)
        pad_ref[POOL_PAD:POOL_PAD + s_len, :] = u_ref[...]
        epad_ref[s_len:s_len + POOL_PAD, :] = jnp.zeros((POOL_PAD, LANE), F32)
        dpw_ref[...] = jnp.zeros_like(dpw_ref)
        sacc_ref[...] = jnp.zeros_like(sacc_ref)

        def run(w):
            def first(i, carry):
                base = pl.multiple_of(i * ch, ch)
                acc = jnp.zeros((ch, LANE), F32)
                for j in range(w):
                    acc = acc + pad_ref[pl.ds(base + POOL_PAD - j, ch), :]
                cnt = _pool_count(base, ch, w)
                d = (acc / cnt - u_ref[pl.ds(base, ch), :]).astype(BF16)
                md = jnp.dot(d, pw_ref[0], preferred_element_type=F32)
                dmc = dm_ref[pl.ds(base, ch), :]
                sacc_ref[...] += (dmc * md).reshape(ch // 8, 8, LANE).sum(axis=0)
                dmd = (dmc * sc_ref[...]).astype(BF16)
                dpw_ref[0] += lax.dot_general(d, dmd, (((0,), (0,)), ((), ())), preferred_element_type=F32)
                dd = lax.dot_general(dmd, pw_ref[0], (((1,), (1,)), ((), ())), preferred_element_type=F32)
                dd_ref[pl.ds(base, ch), :] = dd
                epad_ref[pl.ds(base, ch), :] = dd / cnt
                return carry

            lax.fori_loop(0, s_len // ch, first, 0)

            def second(i, carry):
                base = pl.multiple_of(i * ch, ch)
                acc = jnp.zeros((ch, LANE), F32)
                for j in range(w):
                    acc = acc + epad_ref[pl.ds(base + j, ch), :]
                du_ref[pl.ds(base, ch), :] = (acc - dd_ref[pl.ds(base, ch), :]).astype(du_ref.dtype)
                return carry

            lax.fori_loop(0, s_len // ch, second, 0)

        for g, w in enumerate(POOL_WINDOWS):
            pl.when(gi == g)(functools.partial(run, w))
        dsc_ref[...] = jnp.sum(sacc_ref[...], axis=0, keepdims=True)

    return pl.pallas_call(
        body,
        grid=(POOL_G,),
        in_specs=[pl.BlockSpec((s_len, LANE), lambda g: (0, ZP // LANE + g)), pl.BlockSpec((s_len, LANE), lambda g: (0, g)),
                  pl.BlockSpec((1, POOL_GD, POOL_GD), lambda g: (g, 0, 0)), pl.BlockSpec((1, LANE), lambda g: (0, g)), ANY],
        out_specs=[pl.BlockSpec((s_len, LANE), lambda g: (0, ZP // LANE + g)), pl.BlockSpec((1, POOL_GD, POOL_GD), lambda g: (g, 0, 0)),
                   pl.BlockSpec((1, LANE), lambda g: (0, g))],
        out_shape=[jax.ShapeDtypeStruct(dz.shape, dz.dtype), jax.ShapeDtypeStruct((POOL_G, POOL_GD, POOL_GD), F32),
                   jax.ShapeDtypeStruct((1, POOL_C), F32)],
        scratch_shapes=[pltpu.VMEM((s_len + POOL_PAD, LANE), F32), pltpu.VMEM((s_len + POOL_PAD, LANE), F32),
                        pltpu.VMEM((s_len, LANE), F32), pltpu.VMEM((8, LANE), F32)],
        input_output_aliases={4: 0},
        compiler_params=_cparams(("arbitrary",)),
        name="pool_bwd",
    )(z, dm, pool_w, pool_scale, dz)


def _row(v):
    return v.reshape(1, -1)


def _rms_body(x_ref, g_ref, o_ref):
    o_ref[...] = _rms(x_ref[...], g_ref[...]).astype(o_ref.dtype)


def _post_body(y_ref, x_ref, g_ref, o_ref):
    o_ref[...] = x_ref[...] + _rms(y_ref[...], g_ref[...])


def _post_bwd_body(y_ref, dh_ref, g_ref, dy_ref, dg_ref):
    _, vjp = jax.vjp(_rms, y_ref[...], g_ref[...])
    dy, dg = vjp(dh_ref[...])
    dy_ref[...] = dy.astype(dy_ref.dtype)
    _acc(dg_ref, dg)


def _pre_bwd_body(x_ref, dhn_ref, dres_ref, g_ref, dx_ref, dg_ref):
    _, vjp = jax.vjp(_rms, x_ref[...], g_ref[...])
    dx, dg = vjp(dhn_ref[...])
    dx_ref[...] = dres_ref[...] + dx
    _acc(dg_ref, dg)


def _post_next_body(y_ref, x_ref, g_ref, gn_ref, o_ref, n_ref):
    o = x_ref[...] + _rms(y_ref[...], g_ref[...])
    o_ref[...] = o
    n_ref[...] = _rms(o, gn_ref[...]).astype(n_ref.dtype)


def _pre_post_bwd_body(x_ref, dhn_ref, dres_ref, y_ref, g_ref, gy_ref, dx_ref, dy_ref, dg_ref, dgy_ref):
    _, vjp = jax.vjp(_rms, x_ref[...], g_ref[...])
    dx, dg = vjp(dhn_ref[...])
    dx = dres_ref[...] + dx
    dx_ref[...] = dx
    _, vjp_y = jax.vjp(_rms, y_ref[...], gy_ref[...])
    dy, dgy = vjp_y(dx)
    dy_ref[...] = dy.astype(dy_ref.dtype)
    _acc(dg_ref, dg)
    _acc(dgy_ref, dgy)


def mixer_fwd(x, tabs, w, tag, carry=None, head=None, h=None):
    s_len = x.shape[0]
    cc, sa, sb = tabs
    sv = {"x": x}

    if h is None:
        (h,) = rowwise("mix_norm_pre" + tag, _rms_body, s_len, [_whole(x)], [_row(w["mix_norm_pre"])], [(D_MODEL, BF16)])
    if head is None:
        z = mm(h, w["w_in"], name="in_proj" + tag)
    else:
        z, arrived = mm(h, w["w_in"], name="in_proj" + tag, carry=head[0])
        w = {**w, **head[1](arrived)}

    def prep_body(z_ref, cc_ref, sa_ref, sb_ref, qg_ref, kg_ref, qn_ref, ckv_ref, kr_ref):
        qn_ref[...] = _rms(z_ref[:, 0:Q_RANK], qg_ref[...]).astype(qn_ref.dtype)
        ckv_ref[...] = _rms(z_ref[:, Q_RANK:Q_RANK + KV_RANK], kg_ref[...]).astype(ckv_ref.dtype)
        kr_ref[...] = _rope(z_ref[:, Q_RANK + KV_RANK:ZA_W], cc_ref[...], sa_ref[...], sb_ref[...])

    qn, ckvn, kr = rowwise("attn_prep" + tag, prep_body, s_len, [(z, ZA_W, ZA // ZA_W), _whole(cc), _whole(sa), _whole(sb)],
                           [_row(w["q_norm"]), _row(w["kv_norm"])], [(Q_RANK, BF16), (KV_RANK, BF16), (HP, F32)])
    q_raw = mm(qn, w["w_uq"], out_dtype=BF16, name="q_proj" + tag)
    kv_raw = mm(ckvn, w["w_ukv"], out_dtype=BF16, name="kv_proj" + tag)

    def qkv_body(q_ref, kv_ref, kr_ref, cc_ref, sa_ref, sb_ref, qo_ref, ko_ref, vo_ref):
        c_, a_, b_, kro = cc_ref[...], sa_ref[...], sb_ref[...], kr_ref[...]
        for hh in range(N_HEADS):
            sl = slice(hh * HP, (hh + 1) * HP)
            qo_ref[:, sl] = _rope(q_ref[:, sl].astype(F32), c_, a_, b_).astype(qo_ref.dtype)
            ko_ref[:, sl] = (kv_ref[:, sl].astype(F32) + kro).astype(ko_ref.dtype)
        lane = lax.broadcasted_iota(jnp.int32, (q_ref.shape[0], HW), 1)
        vo_ref[...] = jnp.where((lane & (HP - 1)) == VDIM, 1.0, kv_ref[:, HW:2 * HW].astype(F32)).astype(vo_ref.dtype)

    q, k, v = rowwise("qkv_rope" + tag, qkv_body, s_len, [_whole(q_raw), _whole(kv_raw), _whole(kr), _whole(cc), _whole(sa), _whole(sb)], [],
                      [(HW, BF16)] * 3)
    o, lse, carried = attention_fwd(q, k, v, "attention_fwd" + tag, carry)
    y_attn = mm(o, w["w_attn_o"], out_dtype=BF16, name="attn_out" + tag)

    c = conv_fwd(z, w["conv_w"], _row(w["conv_b"]))

    def ln_body(c_ref, g_ref, b_ref, o_ref):
        o_ref[...] = _silu(_layer_norm(c_ref[...], g_ref[...], b_ref[...])).astype(o_ref.dtype)

    (cs,) = rowwise("conv_ln_silu" + tag, ln_body, s_len, [_whole(c)], [_row(w["conv_ln_g"]), _row(w["conv_ln_b"])], [(CONV_C, BF16)])
    y_conv = mm(cs, w["w_conv_o"], out_dtype=BF16, name="conv_out" + tag)

    m = pool_fwd(z, w["pool_w"], _row(w["pool_scale"]))
    y_pool = mm(m, w["w_pool_o"], out_dtype=BF16, name="pool_out" + tag)

    def merge_body(ya_ref, yc_ref, yp_ref, gl_ref, o_ref):
        gl = gl_ref[...]
        o_ref[...] = (jax.nn.sigmoid(gl[:, 0:D_MODEL]) * ya_ref[...].astype(F32) + jax.nn.sigmoid(gl[:, D_MODEL:2 * D_MODEL]) * yc_ref[...].astype(F32)
                      + jax.nn.sigmoid(gl[:, 2 * D_MODEL:3 * D_MODEL]) * yp_ref[...].astype(F32)).astype(o_ref.dtype)

    (merged,) = rowwise("gate_merge" + tag, merge_body, s_len, [_whole(y_attn), _whole(y_conv), _whole(y_pool), (z, 3 * D_MODEL, 0)], [],
                        [(D_MODEL, BF16)])
    mo = mm(merged, w["w_mix_o"], name="mix_out" + tag)

    h1, hn = rowwise("mix_norm_post" + tag, _post_next_body, s_len, [_whole(mo), _whole(x)],
                     [_row(w["mix_norm_post"]), _row(w["ffn_norm_pre"])], [(D_MODEL, F32), (D_MODEL, BF16)])
    sv.update(h=h, z=z, qn=qn, ckvn=ckvn, q=q, k=k, v=v, o=o, lse=lse, c=c, cs=cs, m=m, y_attn=y_attn, y_conv=y_conv, y_pool=y_pool,
              merged=merged, mo=mo, w=w, hn=hn)
    return h1, sv, carried


def ffn_fwd(h1, hn, w, tag, next_gain=None):
    s_len = h1.shape[0]
    gu, act = ffn_in(hn, w["w_gu"], "ffn_in" + tag)
    y = mm(act, w["w_down"], name="ffn_out" + tag)
    if next_gain is None:
        (h2,) = rowwise("ffn_norm_post" + tag, _post_body, s_len, [_whole(y), _whole(h1)], [_row(w["ffn_norm_post"])], [(D_MODEL, F32)])
        h_next = None
    else:
        h2, h_next = rowwise("ffn_norm_post" + tag, _post_next_body, s_len, [_whole(y), _whole(h1)],
                             [_row(w["ffn_norm_post"]), _row(next_gain)], [(D_MODEL, F32), (D_MODEL, BF16)])
    return h2, dict(h1=h1, hn=hn, gu=gu, act=act, y=y), h_next


def ffn_bwd(dh2, sv, w, tag, mo, post=None):
    s_len = dh2.shape[0]
    g = {}
    if post is None:
        d_y, g["ffn_norm_post"] = rowwise("ffn_norm_post_bwd" + tag, _post_bwd_body, s_len, [_whole(sv["y"]), _whole(dh2)],
                                          [_row(w["ffn_norm_post"])], [(D_MODEL, BF16)], [((1, D_MODEL), F32)])
    else:
        d_y, g["ffn_norm_post"] = post
    g["w_down"] = mm(sv["act"], d_y, ta=True, name="ffn_out_dw" + tag)
    d_gu = ffn_out_dx(d_y, w["w_down"], sv["gu"], "ffn_out_dx" + tag)
    g["w_gu"] = mm(sv["hn"], d_gu, ta=True, name="ffn_in_dw" + tag)
    d_hn = mm(d_gu, w["w_gu"], tb=True, name="ffn_in_dx" + tag)
    dh1, d_mo, g["ffn_norm_pre"], d_mix_post = rowwise(
        "ffn_norm_pre_bwd" + tag, _pre_post_bwd_body, s_len, [_whole(sv["h1"]), _whole(d_hn), _whole(dh2), _whole(mo)],
        [_row(w["ffn_norm_pre"]), _row(w["mix_norm_post"])], [(D_MODEL, F32), (D_MODEL, BF16)], [((1, D_MODEL), F32), ((1, D_MODEL), F32)])
    return dh1, g, (d_mo, d_mix_post)


def mixer_bwd(dh1, sv, tabs, w, tag, post, carry=None, tail=None, prev=None):
    s_len = dh1.shape[0]
    cc, sa, sb = tabs
    g = {}

    d_mo, g["mix_norm_post"] = post
    g["w_mix_o"] = mm(sv["merged"], d_mo, ta=True, name="mix_out_dw" + tag)
    d_merged = mm(d_mo, w["w_mix_o"], tb=True, name="mix_out_dx" + tag)

    def merge_bwd_body(dm_ref, ya_ref, yc_ref, yp_ref, gl_ref, dya_ref, dyc_ref, dyp_ref, dgl_ref):
        dmg = dm_ref[...]
        for i, (y_ref, dy_ref) in enumerate(((ya_ref, dya_ref), (yc_ref, dyc_ref), (yp_ref, dyp_ref))):
            sg = jax.nn.sigmoid(gl_ref[:, i * D_MODEL:(i + 1) * D_MODEL])
            dy_ref[...] = (dmg * sg).astype(dy_ref.dtype)
            dgl_ref[:, i * D_MODEL:(i + 1) * D_MODEL] = (dmg * y_ref[...].astype(F32) * sg * (1.0 - sg)).astype(dgl_ref.dtype)

    d_ya, d_yc, d_yp, dz = rowwise(
        "gate_merge_bwd" + tag, merge_bwd_body, s_len,
        [_whole(d_merged), _whole(sv["y_attn"]), _whole(sv["y_conv"]), _whole(sv["y_pool"]), (sv["z"], 3 * D_MODEL, 0)], [],
        [(D_MODEL, BF16)] * 3 + [(3 * D_MODEL, BF16, ZW, 0)], tile=WIDE_ROW_TILE)

    g["w_pool_o"] = mm(sv["m"], d_yp, ta=True, name="pool_out_dw" + tag)
    d_m = mm(d_yp, w["w_pool_o"], tb=True, name="pool_out_dx" + tag)
    dz, g["pool_w"], g["pool_scale"] = pool_bwd(sv["z"], d_m, w["pool_w"], _row(w["pool_scale"]), dz)

    g["w_conv_o"] = mm(sv["cs"], d_yc, ta=True, name="conv_out_dw" + tag)
    d_cs = mm(d_yc, w["w_conv_o"], tb=True, name="conv_out_dx" + tag)

    def ln_bwd_body(c_ref, dcs_ref, g_ref, b_ref, dc_ref, dg_ref, db_ref):
        f = lambda c_, g_, b_: _silu(_layer_norm(c_, g_, b_))
        _, vjp = jax.vjp(f, c_ref[...], g_ref[...], b_ref[...])
        dc, dg, db = vjp(dcs_ref[...])
        dc_ref[...] = dc
        _acc(dg_ref, dg)
        _acc(db_ref, db)

    d_c, g["conv_ln_g"], g["conv_ln_b"] = rowwise("conv_ln_silu_bwd" + tag, ln_bwd_body, s_len, [_whole(sv["c"]), _whole(d_cs)],
                                                  [_row(w["conv_ln_g"]), _row(w["conv_ln_b"])], [(CONV_C, F32)],
                                                  [((1, CONV_C), F32), ((1, CONV_C), F32)])
    dz, g["conv_w"], g["conv_b"] = conv_bwd(sv["z"], d_c, w["conv_w"], dz)

    g["w_attn_o"] = mm(sv["o"], d_ya, ta=True, name="attn_out_dw" + tag)
    d_o = mm(d_ya, w["w_attn_o"], tb=True, out_dtype=BF16, name="attn_out_dx" + tag)
    delta = attention_delta(d_o, sv["o"])
    t_bwd = min(ATT_TILE_BWD, s_len)
    rows_of = lambda a: a.reshape(N_HEADS, s_len // t_bwd, 1, t_bwd)
    dq, dk, dv, carried = attention_bwd(sv["q"], sv["k"], sv["v"], d_o, rows_of(sv["lse"]), rows_of(delta), "attention_bwd" + tag, carry)

    def qkv_bwd_body(dq_ref, dk_ref, dv_ref, cc_ref, sa_ref, sb_ref, dqp_ref, dkv_ref, dkr_ref):
        c_, a_, b_ = cc_ref[...], sa_ref[...], sb_ref[...]
        dk_sum = jnp.zeros((dq_ref.shape[0], HP), F32)
        for hh in range(N_HEADS):
            sl = slice(hh * HP, (hh + 1) * HP)
            dqp_ref[:, sl] = _rope_t(dq_ref[:, sl], c_, a_, b_).astype(dqp_ref.dtype)
            dkh = dk_ref[:, sl].astype(F32)
            dkv_ref[:, sl] = dk_ref[:, sl]
            dk_sum = dk_sum + dkh
        dkv_ref[:, HW:2 * HW] = dv_ref[...]
        dkr_ref[...] = _rope_t(dk_sum, c_, a_, b_)

    dq_pre, dkv_pre, d_kr = rowwise("qkv_rope_bwd" + tag, qkv_bwd_body, s_len,
                                    [_whole(dq), _whole(dk), _whole(dv), _whole(cc), _whole(sa), _whole(sb)], [],
                                    [(HW, BF16), (2 * HW, BF16), (HP, F32)])
    g["w_uq"] = mm(sv["qn"], dq_pre, ta=True, name="q_proj_dw" + tag)
    d_qn = mm(dq_pre, w["w_uq"], tb=True, name="q_proj_dx" + tag)
    g["w_ukv"] = mm(sv["ckvn"], dkv_pre, ta=True, name="kv_proj_dw" + tag)
    d_ckvn = mm(dkv_pre, w["w_ukv"], tb=True, name="kv_proj_dx" + tag)

    def prep_bwd_body(z_ref, dqn_ref, dckv_ref, dkr_ref, qg_ref, kg_ref, dz_ref, dqg_ref, dkg_ref):
        _, vq = jax.vjp(_rms, z_ref[:, 0:Q_RANK], qg_ref[...])
        dcq, dqg = vq(dqn_ref[...])
        _, vk = jax.vjp(_rms, z_ref[:, Q_RANK:Q_RANK + KV_RANK], kg_ref[...])
        dckv, dkg = vk(dckv_ref[...])
        dz_ref[:, 0:Q_RANK] = dcq.astype(dz_ref.dtype)
        dz_ref[:, Q_RANK:Q_RANK + KV_RANK] = dckv.astype(dz_ref.dtype)
        dz_ref[:, Q_RANK + KV_RANK:ZA_W] = dkr_ref[...].astype(dz_ref.dtype)
        _acc(dqg_ref, dqg)
        _acc(dkg_ref, dkg)

    dz, g["q_norm"], g["kv_norm"] = rowwise("attn_prep_bwd" + tag, prep_bwd_body, s_len,
                                            [(sv["z"], ZA_W, ZA // ZA_W), _whole(d_qn), _whole(d_ckvn), _whole(d_kr)],
                                            [_row(w["q_norm"]), _row(w["kv_norm"])], [(ZA_W, BF16, ZW, ZA // ZA_W)],
                                            [((1, Q_RANK), F32), ((1, KV_RANK), F32)], into={0: dz})

    if tail is None:
        g["w_in"] = mm(sv["h"], dz, ta=True, name="in_proj_dw" + tag)
        d_h = mm(dz, w["w_in"], tb=True, name="in_proj_dx" + tag)
    else:
        g["w_in"], tailed_rest = mm(sv["h"], dz, ta=True, name="in_proj_dw" + tag, carry=tail[0](g))
        d_h, tailed_in = mm(dz, w["w_in"], tb=True, name="in_proj_dx" + tag, carry=tail[1](g))
        carried = [carried, tailed_in + tailed_rest]
    if prev is None:
        dx, g["mix_norm_pre"] = rowwise(
            "mix_norm_pre_bwd" + tag, _pre_bwd_body, s_len, [_whole(sv["x"]), _whole(d_h), _whole(dh1)], [_row(w["mix_norm_pre"])], [(D_MODEL, F32)], [((1, D_MODEL), F32)])
        return dx, g, carried, None
    y_prev, gain_prev = prev
    dx, d_y_prev, g["mix_norm_pre"], d_gain_prev = rowwise(
        "mix_norm_pre_bwd" + tag, _pre_post_bwd_body, s_len, [_whole(sv["x"]), _whole(d_h), _whole(dh1), _whole(y_prev)],
        [_row(w["mix_norm_pre"]), _row(gain_prev)], [(D_MODEL, F32), (D_MODEL, BF16)], [((1, D_MODEL), F32), ((1, D_MODEL), F32)])
    return dx, g, carried, (d_y_prev, d_gain_prev)


def loss_head(h, target):
    s_len = h.shape[0]

    def body(h_ref, t_ref, dy_ref, loss_ref):
        err = h_ref[...] - t_ref[...]
        dy_ref[...] = err * (1.0 / D_MODEL)
        part = 0.5 * jnp.sum(jnp.mean(err * err, axis=-1, keepdims=True), axis=0, keepdims=True)
        _acc(loss_ref, jnp.broadcast_to(part, (1, LANE)))

    return rowwise("loss_head", body, s_len, [_whole(h), _whole(target)], [], [(D_MODEL, F32)], [((1, LANE), F32)])


def local_step(x, pos_col, target, layers):
    s_len = x.shape[0]
    tabs = rope_tables(pos_col, s_len)
    h, h_normed, saved = x, None, []
    for li, w in enumerate(layers):
        h, sv_mix, _ = mixer_fwd(h, tabs, w, f"_l{li}", h=h_normed)
        next_gain = layers[li + 1]["mix_norm_pre"] if li + 1 < len(layers) else None
        h, sv_ffn, h_normed = ffn_fwd(h, sv_mix["hn"], w, f"_l{li}", next_gain)
        saved.append((sv_mix, sv_ffn))
    dh, loss = loss_head(h, target)
    grads, post = [None] * len(layers), None
    for li in reversed(range(len(layers))):
        dh, g_ffn, mix_post = ffn_bwd(dh, saved[li][1], layers[li], f"_l{li}", saved[li][0]["mo"], post)
        prev = (saved[li - 1][1]["y"], layers[li - 1]["ffn_norm_post"]) if li > 0 else None
        dh, g_mix, _, post = mixer_bwd(dh, saved[li][0], tabs, layers[li], f"_l{li}", mix_post, prev=prev)
        grads[li] = {**g_mix, **g_ffn}
    return loss[0, 0], dh, grads


def _pad_heads_cols(wm, per_head):
    r = wm.shape[0]
    return jnp.pad(wm.reshape(r, N_HEADS, per_head), ((0, 0), (0, 0), (0, HP - per_head))).reshape(r, HW)


def _unpad_heads_cols(wm, per_head):
    r = wm.shape[0]
    return wm.reshape(r, N_HEADS, HP)[:, :, :per_head].reshape(r, N_HEADS * per_head)


def align_weights(p):
    out = dict(p)
    if "w_in" in p:
        w_in = p["w_in"]
        r = w_in.shape[0]
        zeros = lambda n: jnp.zeros((r, n), w_in.dtype)
        conv = w_in[:, O_CONV:O_POOL].reshape(r, 2, CONV_C // LANE, LANE).transpose(0, 2, 1, 3).reshape(r, 2 * CONV_C)
        out["w_in"] = jnp.concatenate([
            w_in[:, O_GATE:D_IN], conv, w_in[:, O_POOL:O_GATE], w_in[:, O_Q:O_KR],
            zeros(KR_LANE), w_in[:, O_KR:O_CONV], zeros(HP - KR_LANE - ROPE)], axis=1)
    if "w_uq" in p:
        out["w_uq"] = _pad_heads_cols(p["w_uq"], NOPE + ROPE)
        out["w_ukv"] = jnp.concatenate([_pad_heads_cols(p["w_uk"], NOPE), _pad_heads_cols(p["w_uv"], VDIM)], axis=1)
        wo = p["w_attn_o"]
        out["w_attn_o"] = jnp.pad(wo.reshape(N_HEADS, VDIM, D_MODEL), ((0, 0), (0, HP - VDIM), (0, 0))).reshape(HW, D_MODEL)
        del out["w_uk"], out["w_uv"]
    if "w_gate" in p:
        out["w_gu"] = jnp.concatenate([p["w_gate"][:, :FF_HALF], p["w_up"][:, :FF_HALF], p["w_gate"][:, FF_HALF:], p["w_up"][:, FF_HALF:]], axis=1)
        del out["w_gate"], out["w_up"]
    return out


def unalign_grads(g):
    out = dict(g)
    if "w_in" in g:
        gi = g["w_in"]
        kr0 = ZA + Q_RANK + KV_RANK + KR_LANE
        r = gi.shape[0]
        conv = gi[:, ZC:ZP].reshape(r, CONV_C // LANE, 2, LANE).transpose(0, 2, 1, 3).reshape(r, 2 * CONV_C)
        out["w_in"] = jnp.concatenate([gi[:, ZA:ZA + Q_RANK + KV_RANK], gi[:, kr0:kr0 + ROPE], conv, gi[:, ZP:ZA], gi[:, ZG:ZC]], axis=1)
    if "w_uq" in g:
        out["w_uq"] = _unpad_heads_cols(g["w_uq"], NOPE + ROPE)
        out["w_uk"] = _unpad_heads_cols(g["w_ukv"][:, :HW], NOPE)
        out["w_uv"] = _unpad_heads_cols(g["w_ukv"][:, HW:], VDIM)
        out["w_attn_o"] = g["w_attn_o"].reshape(N_HEADS, HP, D_MODEL)[:, :VDIM].reshape(N_HEADS * VDIM, D_MODEL)
        del out["w_ukv"]
    if "w_gu" in g:
        gu = g["w_gu"]
        out["w_gate"] = jnp.concatenate([gu[:, 0:FF_HALF], gu[:, 2 * FF_HALF:3 * FF_HALF]], axis=1)
        out["w_up"] = jnp.concatenate([gu[:, FF_HALF:2 * FF_HALF], gu[:, 3 * FF_HALF:]], axis=1)
        del out["w_gu"]
    return out


MESH = pl.DeviceIdType.MESH
ANY = pl.BlockSpec(memory_space=pl.ANY)


def _place():
    return lax.axis_index("x"), lax.axis_index("y"), lax.axis_index("c")


def _other_chips(x, y):
    return [(1 - x, y), (x, 1 - y), (1 - x, 1 - y)]


def _half_rows(rows, c):
    assert rows % (2 * HALF_ALIGN) == 0, rows
    return pl.ds(pl.multiple_of(c * (rows // 2), HALF_ALIGN), rows // 2)


class GatherShards:
    def __init__(self, local):
        self.ins = list(local)
        self.outs = [jax.ShapeDtypeStruct((N_CHIPS, *a.shape), a.dtype) for a in local]
        self.n_sems = 6 * len(local)
        self.base = 0

    def _first(self, in_refs, out_refs, send_sems, recv_sems):
        x, y, c = _place()
        me = 2 * x + y
        chips = _other_chips(x, y)

        def copy(i, k, slot, core, to, src=None):
            dst = out_refs[i].at[slot, _half_rows(out_refs[i].shape[1], core)]
            return pltpu.make_async_remote_copy(src_ref=dst if src is None else src, dst_ref=dst, send_sem=send_sems.at[self.base + 6 * i + k],
                                                recv_sem=recv_sems.at[self.base + 6 * i + k], device_id=to, device_id_type=MESH)

        first = [copy(i, j, me, c, (*chip, c), src=in_refs[i].at[_half_rows(in_refs[i].shape[0], c)])
                 for i in range(len(in_refs)) for j, chip in enumerate(chips)]
        return first, copy

    def start(self, in_refs, out_refs, send_sems, recv_sems):
        first, _ = self._first(in_refs, out_refs, send_sems, recv_sems)
        for cp in first:
            cp.start()

    def finish(self, in_refs, out_refs, send_sems, recv_sems):
        first, copy = self._first(in_refs, out_refs, send_sems, recv_sems)
        x, y, c = _place()
        slots = [2 * cx + cy for cx, cy in _other_chips(x, y)]
        sibling = (x, y, 1 - c)
        passed = []
        for i in range(len(in_refs)):
            for j in range(3):
                copy(i, j, slots[j], c, sibling).wait_recv()
                fwd = copy(i, 3 + j, slots[j], c, sibling)
                fwd.start()
                passed.append(fwd)
        for i in range(len(in_refs)):
            for j in range(3):
                copy(i, 3 + j, slots[j], 1 - c, sibling).wait_recv()
        for cp in first + passed:
            cp.wait_send()


class ChipExchange:
    def __init__(self, parts):
        self.ins = list(parts)
        self.outs = [jax.ShapeDtypeStruct((3, *a.shape[1:]), a.dtype) for a in parts]
        self.n_sems = 3 * len(parts)
        self.base = 0

    def _copies(self, in_refs, out_refs, send_sems, recv_sems):
        x, y, c = _place()
        return [pltpu.make_async_remote_copy(src_ref=in_refs[i].at[2 * chip[0] + chip[1]], dst_ref=out_refs[i].at[j],
                                             send_sem=send_sems.at[self.base + 3 * i + j], recv_sem=recv_sems.at[self.base + 3 * i + j],
                                             device_id=(*chip, c), device_id_type=MESH)
                for i in range(len(in_refs)) for j, chip in enumerate(_other_chips(x, y))]

    def start(self, in_refs, out_refs, send_sems, recv_sems):
        for cp in self._copies(in_refs, out_refs, send_sems, recv_sems):
            cp.start()

    def finish(self, in_refs, out_refs, send_sems, recv_sems):
        copies = self._copies(in_refs, out_refs, send_sems, recv_sems)
        for cp in copies:
            cp.wait_recv()
        for cp in copies:
            cp.wait_send()


def run_exchange(ex, name):
    n_in, n_out = len(ex.ins), len(ex.outs)

    def body(*refs):
        ins, outs, sems = refs[:n_in], refs[n_in:n_in + n_out], refs[n_in + n_out:]
        ex.start(ins, outs, *sems)
        ex.finish(ins, outs, *sems)

    return pl.pallas_call(
        body,
        in_specs=[ANY] * n_in,
        out_specs=[ANY] * n_out,
        out_shape=list(ex.outs),
        scratch_shapes=[pltpu.SemaphoreType.DMA((ex.n_sems,)), pltpu.SemaphoreType.DMA((ex.n_sems,))],
        name=name,
    )(*ex.ins)


def sibling_swap(gs, name):
    n = len(gs)

    def body(*refs):
        g_refs, out_refs, (send_sems, recv_sems) = refs[:n], refs[n:2 * n], refs[2 * n:]
        x, y, c = _place()
        copies = []
        for i in range(n):
            for j in range(N_CHIPS):
                cp = pltpu.make_async_remote_copy(src_ref=g_refs[i].at[j, _half_rows(g_refs[i].shape[1], 1 - c)], dst_ref=out_refs[i].at[j],
                                                  send_sem=send_sems.at[4 * i + j], recv_sem=recv_sems.at[4 * i + j],
                                                  device_id=(x, y, 1 - c), device_id_type=MESH)
                cp.start()
                copies.append(cp)
        for cp in copies:
            cp.wait_recv()
        for cp in copies:
            cp.wait_send()

    return pl.pallas_call(
        body,
        in_specs=[ANY] * n,
        out_specs=[ANY] * n,
        out_shape=[jax.ShapeDtypeStruct((N_CHIPS, a.shape[1] // 2, a.shape[2]), a.dtype) for a in gs],
        scratch_shapes=[pltpu.SemaphoreType.DMA((4 * n,)), pltpu.SemaphoreType.DMA((4 * n,))],
        name=name,
    )(*gs)


def sibling_gather(fs):
    n = len(fs)
    layers = fs[0].shape[0]

    def body(*refs):
        out_refs, (send_sems, recv_sems) = refs[n:2 * n], refs[2 * n:]
        x, y, c = _place()

        def copy(i, l, core):
            part = out_refs[i].at[l, _half_rows(out_refs[i].shape[1], core)]
            return pltpu.make_async_remote_copy(src_ref=part, dst_ref=part, send_sem=send_sems.at[layers * i + l],
                                                recv_sem=recv_sems.at[layers * i + l], device_id=(x, y, 1 - c), device_id_type=MESH)

        sends = [copy(i, l, c) for i in range(n) for l in range(layers)]
        for cp in sends:
            cp.start()
        for i in range(n):
            for l in range(layers):
                copy(i, l, 1 - c).wait_recv()
        for cp in sends:
            cp.wait_send()

    return pl.pallas_call(
        body,
        in_specs=[ANY] * n,
        out_specs=[ANY] * n,
        out_shape=[jax.ShapeDtypeStruct(a.shape, a.dtype) for a in fs],
        scratch_shapes=[pltpu.SemaphoreType.DMA((layers * n,)), pltpu.SemaphoreType.DMA((layers * n,))],
        input_output_aliases={i: i for i in range(n)},
        name="sibling_gather",
    )(*fs)


class GatherAll:
    def __init__(self, vs):
        self.ins = list(vs)
        self.outs = [jax.ShapeDtypeStruct((8, *a.shape), a.dtype) for a in vs]
        self.n_sems = 7 * len(vs)
        self.base = 0

    def _first(self, in_refs, out_refs, send_sems, recv_sems):
        x, y, c = _place()
        me, sibling = (x, y, c), (x, y, 1 - c)

        def copy(i, k, block, to, src=None):
            px, py, pc = block
            dst = out_refs[i].at[4 * px + 2 * py + pc]
            return pltpu.make_async_remote_copy(src_ref=dst if src is None else src, dst_ref=dst, send_sem=send_sems.at[self.base + 7 * i + k],
                                                recv_sem=recv_sems.at[self.base + 7 * i + k], device_id=to, device_id_type=MESH)

        first = []
        for i in range(len(in_refs)):
            first.append(copy(i, 0, me, sibling, src=in_refs[i]))
            first += [copy(i, 1 + j, me, (*chip, c), src=in_refs[i]) for j, chip in enumerate(_other_chips(x, y))]
        return first, copy

    def start(self, in_refs, out_refs, send_sems, recv_sems):
        first, _ = self._first(in_refs, out_refs, send_sems, recv_sems)
        for cp in first:
            cp.start()

    def finish(self, in_refs, out_refs, send_sems, recv_sems):
        first, copy = self._first(in_refs, out_refs, send_sems, recv_sems)
        x, y, c = _place()
        me, sibling = (x, y, c), (x, y, 1 - c)
        chips = _other_chips(x, y)
        passed = []
        for i in range(len(in_refs)):
            for j, chip in enumerate(chips):
                copy(i, 1 + j, (*chip, c), me).wait_recv()
                fwd = copy(i, 4 + j, (*chip, c), sibling)
                fwd.start()
                passed.append(fwd)
        for i in range(len(in_refs)):
            copy(i, 0, sibling, me).wait_recv()
            for j, chip in enumerate(chips):
                copy(i, 4 + j, (*chip, 1 - c), me).wait_recv()
        for cp in first + passed:
            cp.wait_send()


def _row_tile(rows, row_bytes):
    best = None
    for t in range(16, rows + 1, 16):
        if rows % t == 0 and t * row_bytes <= SUM_TILE_BYTES:
            best = t
    return best or rows


def sibling_sum(g, theirs, place, name):
    _, half, cols = theirs.shape
    tile = _row_tile(half, cols * 4)
    nt = half // tile

    def body(place_ref, g_ref, t_ref, o_ref):
        o_ref[...] = (g_ref[...].astype(F32) + t_ref[...].astype(F32)).astype(o_ref.dtype)

    spec = pl.BlockSpec((1, tile, cols), lambda j, i, place_ref: (j, i, 0))
    return pl.pallas_call(
        body,
        grid_spec=pltpu.PrefetchScalarGridSpec(
            num_scalar_prefetch=1, grid=(N_CHIPS, nt),
            in_specs=[pl.BlockSpec((1, tile, cols), lambda j, i, place_ref: (j, place_ref[1] * nt + i, 0)), spec], out_specs=spec),
        out_shape=jax.ShapeDtypeStruct(theirs.shape, BF16),
        compiler_params=_cparams(("parallel", "parallel")),
        name=name,
    )(place, g, theirs)


def chip_sum(p, others, place, layer, into, name):
    _, half, cols = p.shape
    tile = _row_tile(half, cols * 4)
    nt = half // tile

    def body(place_ref, p_ref, o3_ref, *rest):
        o_ref = rest[-1]
        acc = p_ref[0].astype(F32)
        for k in range(3):
            acc = acc + o3_ref[k].astype(F32)
        o_ref[0] = acc

    return pl.pallas_call(
        body,
        grid_spec=pltpu.PrefetchScalarGridSpec(
            num_scalar_prefetch=1, grid=(nt,),
            in_specs=[pl.BlockSpec((1, tile, cols), lambda i, place_ref: (place_ref[0], i, 0)),
                      pl.BlockSpec((3, tile, cols), lambda i, place_ref: (0, i, 0))] + ([] if into is None else [ANY]),
            out_specs=pl.BlockSpec((1, tile, cols), lambda i, place_ref: (layer, place_ref[1] * nt + i, 0))),
        out_shape=jax.ShapeDtypeStruct((N_LAYERS, 2 * half, cols), F32),
        input_output_aliases={} if into is None else {3: 0},
        compiler_params=_cparams(("parallel",)),
        name=name,
    )(place, p, others, *([] if into is None else [into]))


def sum_devices(a, name):
    n, rows, cols = a.shape
    tile = _row_tile(rows, cols * 4 * n)

    def body(a_ref, o_ref):
        acc = a_ref[0]
        for s in range(1, n):
            acc = acc + a_ref[s]
        o_ref[...] = acc

    return pl.pallas_call(body, grid=(rows // tile,), in_specs=[pl.BlockSpec((n, tile, cols), lambda i: (0, i, 0))],
                          out_specs=pl.BlockSpec((tile, cols), lambda i: (i, 0)), out_shape=jax.ShapeDtypeStruct((rows, cols), F32),
                          compiler_params=_cparams(("parallel",)), name=name)(a)


def adamw(w, g, m, v, name):
    layers, rows, cols = w.shape
    tile = rows
    for t in range(8, rows, 8):
        if rows % t == 0 and t * cols * 4 <= ADAM_TILE_BYTES:
            tile = t

    def body(w_ref, g_ref, m_ref, v_ref, d_ref, mo_ref, vo_ref):
        gg = g_ref[...]
        m_new = ADAM_B1 * m_ref[...] + (1.0 - ADAM_B1) * gg
        v_new = ADAM_B2 * v_ref[...] + (1.0 - ADAM_B2) * (gg * gg)
        m_hat = m_new / (1.0 - ADAM_B1 ** ADAM_STEP)
        v_hat = v_new / (1.0 - ADAM_B2 ** ADAM_STEP)
        d_ref[...] = -ADAM_LR * (m_hat / (jnp.sqrt(v_hat) + ADAM_EPS) + ADAM_WD * w_ref[...])
        mo_ref[...] = m_new
        vo_ref[...] = v_new

    spec = pl.BlockSpec((1, tile, cols), lambda l, i: (l, i, 0))
    shape = jax.ShapeDtypeStruct((layers, rows, cols), F32)
    return pl.pallas_call(body, grid=(layers, rows // tile), in_specs=[spec] * 4, out_specs=[spec] * 3, out_shape=[shape] * 3,
                          compiler_params=_cparams(("parallel", "parallel")), name=name)(w, g, m, v)


WEIGHTS = ["mix_norm_pre", "w_in", "q_norm", "w_uq", "kv_norm", "w_uk", "w_uv", "w_attn_o", "conv_w", "conv_b", "conv_ln_g", "conv_ln_b",
           "w_conv_o", "pool_w", "pool_scale", "w_pool_o", "w_mix_o", "mix_norm_post", "ffn_norm_pre", "w_gate", "w_up", "w_down",
           "ffn_norm_post"]
SHARDED = {"w_in": 2, "w_uq": 2, "w_uk": 2, "w_uv": 2, "w_attn_o": 2, "conv_w": 2, "w_conv_o": 2, "w_pool_o": 2, "w_mix_o": 1,
           "w_gate": 2, "w_up": 2, "w_down": 1}
REPLICATED = [n for n in WEIGHTS if n not in SHARDED]
ROW_PARAMS = [n for n in REPLICATED if n != "pool_w"]
ROWS_MINOR = ("w_in", "w_uq", "w_gate", "w_up", "conv_w")
N_CHIPS = 4
N_MIX_GROUPS = 6
MIX_MATRICES = ("w_in", "w_uq", "w_ukv", "w_attn_o", "w_conv_o", "w_pool_o", "conv_w", "w_mix_o")
CONV_WIRE_ROWS = 32
GROUPS = [(("w_in",), 1), (("w_uq",), 1), (("w_uk", "w_uv"), 1), (("w_attn_o", "w_conv_o", "w_pool_o"), 1), (("conv_w",), 1),
          (("w_mix_o",), 1), (("w_gate", "w_up"), 2), (("w_down",), 1)]


def _join(parts, axis):
    return parts[0] if len(parts) == 1 else jnp.concatenate(parts, axis=axis)


def _split_group(arr, names, axis, shapes):
    out, off = {}, 0
    ax = arr.ndim - 3 + axis
    for n in names:
        size = shapes[n][axis]
        out[n] = lax.slice_in_dim(arr, off, off + size, axis=ax)
        off += size
    return out


def _pack_rows(vectors):
    blocks = []
    for v in vectors:
        for li in range(v.shape[0]):
            blocks.append(jnp.pad(v[li][None, :], ((0, PACK_ROWS - 1), (0, PACK_W - v.shape[1]))))
    return jnp.concatenate(blocks, axis=0)


def _unpack_rows(packed, shapes):
    out, r = [], 0
    for layers, width in shapes:
        out.append(jnp.stack([packed[r + PACK_ROWS * li, :width] for li in range(layers)]))
        r += PACK_ROWS * layers
    return out


def kernel(x, positions, mix_norm_pre, w_in, q_norm, w_uq, kv_norm, w_uk, w_uv, w_attn_o, conv_w, conv_b, conv_ln_g, conv_ln_b, w_conv_o, pool_w, pool_scale, w_pool_o, w_mix_o, mix_norm_post, ffn_norm_pre, w_gate, w_up, w_down, ffn_norm_post, loss_target, m_mix_norm_pre, m_w_in, m_q_norm, m_w_uq, m_kv_norm, m_w_uk, m_w_uv, m_w_attn_o, m_conv_w, m_conv_b, m_conv_ln_g, m_conv_ln_b, m_w_conv_o, m_pool_w, m_pool_scale, m_w_pool_o, m_w_mix_o, m_mix_norm_post, m_ffn_norm_pre, m_w_gate, m_w_up, m_w_down, m_ffn_norm_post, v_mix_norm_pre, v_w_in, v_q_norm, v_w_uq, v_kv_norm, v_w_uk, v_w_uv, v_w_attn_o, v_conv_w, v_conv_b, v_conv_ln_g, v_conv_ln_b, v_w_conv_o, v_pool_w, v_pool_scale, v_w_pool_o, v_w_mix_o, v_mix_norm_post, v_ffn_norm_pre, v_w_gate, v_w_up, v_w_down, v_ffn_norm_post):
    given = dict(mix_norm_pre=mix_norm_pre, w_in=w_in, q_norm=q_norm, w_uq=w_uq, kv_norm=kv_norm, w_uk=w_uk, w_uv=w_uv, w_attn_o=w_attn_o,
                 conv_w=conv_w, conv_b=conv_b, conv_ln_g=conv_ln_g, conv_ln_b=conv_ln_b, w_conv_o=w_conv_o, pool_w=pool_w,
                 pool_scale=pool_scale, w_pool_o=w_pool_o, w_mix_o=w_mix_o, mix_norm_post=mix_norm_post, ffn_norm_pre=ffn_norm_pre,
                 w_gate=w_gate, w_up=w_up, w_down=w_down, ffn_norm_post=ffn_norm_post)
    mom = dict(mix_norm_pre=m_mix_norm_pre, w_in=m_w_in, q_norm=m_q_norm, w_uq=m_w_uq, kv_norm=m_kv_norm, w_uk=m_w_uk, w_uv=m_w_uv,
               w_attn_o=m_w_attn_o, conv_w=m_conv_w, conv_b=m_conv_b, conv_ln_g=m_conv_ln_g, conv_ln_b=m_conv_ln_b, w_conv_o=m_w_conv_o,
               pool_w=m_pool_w, pool_scale=m_pool_scale, w_pool_o=m_w_pool_o, w_mix_o=m_w_mix_o, mix_norm_post=m_mix_norm_post,
               ffn_norm_pre=m_ffn_norm_pre, w_gate=m_w_gate, w_up=m_w_up, w_down=m_w_down, ffn_norm_post=m_ffn_norm_post)
    var = dict(mix_norm_pre=v_mix_norm_pre, w_in=v_w_in, q_norm=v_q_norm, w_uq=v_w_uq, kv_norm=v_kv_norm, w_uk=v_w_uk, w_uv=v_w_uv,
               w_attn_o=v_w_attn_o, conv_w=v_conv_w, conv_b=v_conv_b, conv_ln_g=v_conv_ln_g, conv_ln_b=v_conv_ln_b, w_conv_o=v_w_conv_o,
               pool_w=v_pool_w, pool_scale=v_pool_scale, w_pool_o=v_w_pool_o, w_mix_o=v_w_mix_o, mix_norm_post=v_mix_norm_post,
               ffn_norm_pre=v_ffn_norm_pre, w_gate=v_w_gate, w_up=v_w_up, w_down=v_w_down, ffn_norm_post=v_ffn_norm_post)
    s_len = x.shape[1]
    sharded_names = [n for n in WEIGHTS if n in SHARDED]
    chip = 2 * lax.axis_index("x") + lax.axis_index("y")
    place = jnp.stack([chip, lax.axis_index("c")]).astype(jnp.int32)
    shard_shape = {n: given[n].shape for n in sharded_names}

    mix_groups, ffn_groups = GROUPS[:N_MIX_GROUPS], GROUPS[N_MIX_GROUPS:]
    weight_wire_shape = {n: (N_LAYERS, 2 * CONV_WIRE_ROWS, shard_shape[n][2]) if n == "conv_w" else shard_shape[n] for n in sharded_names}
    grad_wire_shape = {n: (N_LAYERS, CONV_WIRE_ROWS, shard_shape[n][2]) if n == "conv_w" else shard_shape[n] for n in sharded_names}
    pad_rows = lambda a: jnp.pad(a, ((0, CONV_WIRE_ROWS - CONV_W), (0, 0)))

    def weight_wires(groups, li):
        def wire(name):
            a = given[name][li]
            if name == "conv_w":
                hi = a.astype(BF16)
                return jnp.concatenate([pad_rows(hi), pad_rows((a - hi.astype(F32)).astype(BF16))], axis=0)
            return a.astype(BF16)

        return [_join([wire(n) for n in names], axis - 1) for names, axis in groups]

    def full_weights(groups, li, local, gathered):
        p = {n: given[n][li] for n in REPLICATED}
        p["pool_w"] = p["pool_w"].astype(BF16)
        for (names, axis), loc, got in zip(groups, local, gathered):
            got = lax.dynamic_update_slice(got, loc[None], (chip, 0, 0))
            per_chip = [_split_group(got[j], names, axis, weight_wire_shape) for j in range(N_CHIPS)]
            for n in names:
                parts = [pc[n] for pc in per_chip]
                if n == "conv_w":
                    parts = [q[:CONV_W].astype(F32) + q[CONV_WIRE_ROWS:CONV_WIRE_ROWS + CONV_W].astype(F32) for q in parts]
                p[n] = jnp.concatenate(parts, axis=SHARDED[n] - 1)
        return align_weights(p)

    def chip_partials(groups, g, tag):
        wires = []
        for names, axis in groups:
            split = {n: jnp.split(pad_rows(g[n]) if n == "conv_w" else g[n], N_CHIPS, axis=SHARDED[n] - 1) for n in names}
            wires.append(jnp.stack([_join([split[n][j].astype(BF16) for n in names], axis - 1) for j in range(N_CHIPS)]))
        theirs = sibling_swap(wires, "sibling_swap_" + tag)
        return [sibling_sum(w, t, place, f"sibling_sum_{tag}_{i}") for i, (w, t) in enumerate(zip(wires, theirs))]

    tabs = rope_tables(positions.reshape(s_len, 1), s_len)
    in_groups, rest_groups = mix_groups[:1], mix_groups[1:]
    loc_in0, loc_rest0 = weight_wires(in_groups, 0), weight_wires(rest_groups, 0)
    w_in0 = full_weights(in_groups, 0, loc_in0, run_exchange(GatherShards(loc_in0), "gather_w_in_l0"))
    head = (GatherShards(loc_rest0), lambda arrived: full_weights(rest_groups, 0, loc_rest0, arrived))
    loc_f0, loc_m1 = weight_wires(ffn_groups, 0), weight_wires(mix_groups, 1)
    h, sv_m0, got = mixer_fwd(x[0], tabs, w_in0, "_l0", GatherShards(loc_f0 + loc_m1), head)
    w_m0 = sv_m0["w"]
    w_f0 = full_weights(ffn_groups, 0, loc_f0, got[:len(loc_f0)])
    w_m1 = full_weights(mix_groups, 1, loc_m1, got[len(loc_f0):])
    h, sv_f0, h_normed = ffn_fwd(h, sv_m0["hn"], w_f0, "_l0", w_m1["mix_norm_pre"])
    loc_f1 = weight_wires(ffn_groups, 1)
    h, sv_m1, got = mixer_fwd(h, tabs, w_m1, "_l1", GatherShards(loc_f1), h=h_normed)
    w_f1 = full_weights(ffn_groups, 1, loc_f1, got)
    h, sv_f1, _ = ffn_fwd(h, sv_m1["hn"], w_f1, "_l1")
    dh, loss_local = loss_head(h, loss_target[0])
    loss = lax.psum(loss_local[0, 0], MESH_AXES)

    dh, g_f1, post_m1 = ffn_bwd(dh, sv_f1, w_f1, "_l1", sv_m1["mo"])
    g_f1 = unalign_grads(g_f1)
    p_f1 = chip_partials(ffn_groups, g_f1, "ffn_l1")
    dh, g_m1, o_f1, post_f0 = mixer_bwd(dh, sv_m1, tabs, w_m1, "_l1", post_m1, ChipExchange(p_f1), prev=(sv_f0["y"], w_f0["ffn_norm_post"]))
    g_m1 = unalign_grads(g_m1)
    p_m1 = chip_partials(mix_groups, g_m1, "mix_l1")
    dh, g_f0, post_m0 = ffn_bwd(dh, sv_f0, w_f0, "_l0", sv_m0["mo"], post_f0)
    g_f0 = unalign_grads(g_f0)
    p_f0 = chip_partials(ffn_groups, g_f0, "ffn_l0")
    last = {}

    def tail_rest(g):
        last["rest"] = chip_partials(rest_groups, unalign_grads({n: g[n] for n in MIX_MATRICES[1:]}), "rest_l0")
        return ChipExchange(last["rest"])

    def tail_in(g):
        last["in"] = chip_partials(in_groups, unalign_grads({"w_in": g["w_in"]}), "w_in_l0")
        return ChipExchange(last["in"])

    grad_x, g_m0, (got, o_m0), _ = mixer_bwd(dh, sv_m0, tabs, w_m0, "_l0", post_m0, ChipExchange(p_m1 + p_f0), (tail_rest, tail_in))
    o_m1, o_f0 = got[:len(p_m1)], got[len(p_m1):]
    p_m0 = last["in"] + last["rest"]
    grads = [{**g_m0, **g_f0}, {**g_m1, **g_f1}]
    grad_full = {n: jnp.stack([g[n].reshape(given[n].shape[1:]) for g in grads]) for n in REPLICATED}
    replicated = [_pack_rows([grad_full[n] for n in ROW_PARAMS]), grad_full["pool_w"].reshape(-1, POOL_GD)]
    device = 2 * chip + lax.axis_index("c")
    rows_all, pool_w_all = [lax.dynamic_update_slice(a, mine[None], (device, 0, 0))
                            for a, mine in zip(run_exchange(GatherAll(replicated), "gather_replicated"), replicated)]

    sums = {}
    for groups, base, per_layer in ((ffn_groups, N_MIX_GROUPS, ((1, p_f1, o_f1), (0, p_f0, o_f0))), (mix_groups, 0, ((1, p_m1, o_m1), (0, p_m0, o_m0)))):
        for li, parts, others in per_layer:
            for i, (p, o) in enumerate(zip(parts, others)):
                sums[base + i] = chip_sum(p, o, place, li, sums.get(base + i), f"chip_sum_{base + i}_l{li}")
    g_shard = {}
    for (names, axis), s in zip(GROUPS, sibling_gather([sums[i] for i in range(len(GROUPS))])):
        g_shard.update(_split_group(s, names, axis, grad_wire_shape))
    g_shard["conv_w"] = g_shard["conv_w"][:, :CONV_W]

    row_shapes = [given[n].shape for n in ROW_PARAMS]
    g_rows = sum_devices(rows_all, "row_params_sum")
    g_pool_w = sum_devices(pool_w_all, "pool_w_sum")
    g_rep = dict(zip(ROW_PARAMS, _unpack_rows(g_rows, row_shapes)))
    g_rep["pool_w"] = g_pool_w.reshape(given["pool_w"].shape)

    g_out, d_out, m_out, v_out = {}, {}, {}, {}
    for n in sharded_names + ["pool_w"]:
        shp = given[n].shape
        three_d = (shp[0], int(np.prod(shp[1:-1])), shp[-1])
        g_n = g_shard[n] if n in SHARDED else g_rep[n]
        view = (lambda a: jnp.swapaxes(a.reshape(three_d), 1, 2)) if n in ROWS_MINOR else (lambda a: a.reshape(three_d))
        back = (lambda a: jnp.swapaxes(a, 1, 2).reshape(shp)) if n in ROWS_MINOR else (lambda a: a.reshape(shp))
        d, mn, vn = adamw(view(given[n]), view(g_n), view(mom[n]), view(var[n]), "adamw_" + n)
        g_out[n], d_out[n], m_out[n], v_out[n] = g_n, back(d), back(mn), back(vn)
    rd, rm, rv = adamw(_pack_rows([given[n] for n in ROW_PARAMS])[None], g_rows[None], _pack_rows([mom[n] for n in ROW_PARAMS])[None],
                       _pack_rows([var[n] for n in ROW_PARAMS])[None], "adamw_row_params")
    for n, d, mn, vn in zip(ROW_PARAMS, *[_unpack_rows(a[0], row_shapes) for a in (rd, rm, rv)]):
        g_out[n], d_out[n], m_out[n], v_out[n] = g_rep[n], d, mn, vn

    return (loss, grad_x[None], *[g_out[n] for n in WEIGHTS], *[d_out[n] for n in WEIGHTS], *[m_out[n] for n in WEIGHTS],
            *[v_out[n] for n in WEIGHTS])
```

```python
import functools
import math

import numpy as np
import jax
import jax.numpy as jnp
from jax import lax
from jax.experimental import pallas as pl
from jax.experimental.pallas import tpu as pltpu

F32, BF16 = jnp.float32, jnp.bfloat16

D_MODEL = 1024
N_HEADS = 8
NOPE, ROPE, VDIM = 64, 32, 64
HALF_ROPE = ROPE // 2
Q_RANK, KV_RANK = 384, 256
CONV_C, CONV_W = 512, 31
POOL_C, POOL_G, POOL_GD = 512, 4, 128
POOL_WINDOWS = (2, 4, 8, 16)
D_FF = 2816
FF_HALF = D_FF // 2
N_LAYERS = 2
EPS = 1e-6
ROPE_THETA = 10000.0
ATT_SCALE = 1.0 / math.sqrt(NOPE + ROPE)
O_Q, O_KV, O_KR, O_CONV, O_POOL, O_GATE, D_IN = 0, 384, 640, 672, 1696, 2208, 5280

LANE = 128
HP = 128
ZG, ZC, ZP, ZA, ZW = 0, 3072, 4096, 4608, 5376
ZA_W = Q_RANK + KV_RANK + HP
KR_LANE = NOPE
HW = N_HEADS * HP

ADAM_LR, ADAM_B1, ADAM_B2, ADAM_EPS, ADAM_WD, ADAM_STEP = 0.001, 0.9, 0.999, 1e-08, 0.01, 10

ROW_TILE = 512
WIDE_ROW_TILE = 256
ATT_TILE_FWD = 1024
ATT_TILE_BWD = 512
ATT_HEADS = 4
CONV_CHUNK = 256
MM_TM, MM_TN, MM_TK = 1024, 1408, 1024
MM_TILE_MAX = 2048
FFN_TM = 512
MM_VMEM_BUDGET = 40 * 1024 * 1024
HBM_BYTES_PER_US = 3.0e6
GRID_STEP_US = 0.35
VMEM_LIMIT = 56 * 1024 * 1024
SUM_TILE_BYTES = 3 * 1024 * 1024
ADAM_TILE_BYTES = 2 * 1024 * 1024

HALF_ALIGN = 16
MESH_AXES = ("x", "y", "c")
PACK_W = 1024
PACK_ROWS = 8


def _cparams(sem):
    return pltpu.CompilerParams(dimension_semantics=sem, vmem_limit_bytes=VMEM_LIMIT)


def _tile(n, target):
    if n <= target:
        return n
    best = None
    for t in range(LANE, target + 1, LANE):
        if n % t == 0:
            best = t
    assert best is not None, (n, target)
    return best


def _mm_tiles(m, n, k, a_bytes, b_bytes, out_bytes):
    divs = lambda d: sorted({t for t in range(LANE, min(d, MM_TILE_MAX) + 1, LANE) if d % t == 0} | ({d} if d <= MM_TILE_MAX else set()))
    best = None
    for tm in divs(m):
        for tn in divs(n):
            blocks = tm * k * a_bytes + k * tn * b_bytes + tm * tn * out_bytes
            if 2 * blocks + tm * tn * 4 > MM_VMEM_BUDGET:
                continue
            steps = (m // tm) * (n // tn)
            for rows_outer in (True, False):
                moved = (m * k * a_bytes + k * n * b_bytes * (m // tm)) if rows_outer else (k * n * b_bytes + m * k * a_bytes * (n // tn))
                cost = (moved + m * n * out_bytes + blocks) / HBM_BYTES_PER_US + steps * GRID_STEP_US
                if best is None or cost < best[0]:
                    best = (cost, tm, tn, rows_outer)
    if best is not None:
        return best[1], best[2], k, best[3]
    return _tile(m, MM_TM), _tile(n, MM_TN), _tile(k, MM_TK), True


def mm(a, b, *, ta=False, tb=False, out_dtype=F32, name, carry=None):
    m, k = (a.shape[1], a.shape[0]) if ta else a.shape
    n, k2 = b.shape if tb else (b.shape[1], b.shape[0])
    assert k == k2, (a.shape, b.shape, ta, tb)
    tm, tn, tk, rows_outer = _mm_tiles(m, n, k, a.dtype.itemsize, b.dtype.itemsize, jnp.dtype(out_dtype).itemsize)
    nk = k // tk
    dims = (((0 if ta else 1,), (1 if tb else 0,)), ((), ()))
    grid = (m // tm, n // tn, nk) if rows_outer else (n // tn, m // tm, nk)

    def body(*refs):
        (a_ref, b_ref, o_ref, *acc), start, finish = _carried(carry, refs, 2, 1, 0 if nk == 1 else 1)
        ids = [pl.program_id(d) for d in range(3)]
        if start is not None:
            pl.when((ids[0] == 0) & (ids[1] == 0) & (ids[2] == 0))(start)
        part = lax.dot_general(a_ref[...].astype(BF16), b_ref[...].astype(BF16), dims, preferred_element_type=F32)
        if nk == 1:
            o_ref[...] = part.astype(o_ref.dtype)
        else:
            (acc_ref,) = acc
            kk = ids[2]

            @pl.when(kk == 0)
            def _():
                acc_ref[...] = part

            @pl.when(kk > 0)
            def _():
                acc_ref[...] += part

            @pl.when(kk == nk - 1)
            def _():
                o_ref[...] = acc_ref[...].astype(o_ref.dtype)
        if finish is not None:
            pl.when((ids[0] == grid[0] - 1) & (ids[1] == grid[1] - 1) & (ids[2] == nk - 1))(finish)

    ij = (lambda g0, g1: (g0, g1)) if rows_outer else (lambda g0, g1: (g1, g0))

    def a_map(g0, g1, kk):
        i, _ = ij(g0, g1)
        return (kk, i) if ta else (i, kk)

    def b_map(g0, g1, kk):
        _, j = ij(g0, g1)
        return (j, kk) if tb else (kk, j)

    ex_in_specs, ex_out_specs, ex_out_shape, ex_scratch, ex_inputs = _carry_specs(carry)
    a_spec = pl.BlockSpec((tk, tm) if ta else (tm, tk), a_map)
    b_spec = pl.BlockSpec((tn, tk) if tb else (tk, tn), b_map)
    out, *carried = pl.pallas_call(
        body,
        grid=grid,
        in_specs=[a_spec, b_spec] + ex_in_specs,
        out_specs=[pl.BlockSpec((tm, tn), lambda g0, g1, kk: ij(g0, g1))] + ex_out_specs,
        out_shape=[jax.ShapeDtypeStruct((m, n), out_dtype)] + ex_out_shape,
        scratch_shapes=([] if nk == 1 else [pltpu.VMEM((tm, tn), F32)]) + ex_scratch,
        compiler_params=_cparams(("arbitrary", "arbitrary", "arbitrary") if carry is not None else ("parallel", "parallel", "arbitrary")),
        name=name,
    )(a, b, *ex_inputs)
    return out if carry is None else (out, carried)


def ffn_in(hn, w_gu, name):
    s_len, k = hn.shape
    tm = min(FFN_TM, s_len)

    def body(a_ref, b_ref, gu_ref, act_ref):
        r = jnp.dot(a_ref[...], b_ref[...], preferred_element_type=F32)
        gu_ref[...] = r.astype(gu_ref.dtype)
        act_ref[...] = (_silu(r[:, :FF_HALF]) * r[:, FF_HALF:]).astype(act_ref.dtype)

    return pl.pallas_call(
        body,
        grid=(2, s_len // tm),
        in_specs=[pl.BlockSpec((tm, k), lambda j, i: (i, 0)), pl.BlockSpec((k, 2 * FF_HALF), lambda j, i: (0, j))],
        out_specs=[pl.BlockSpec((tm, 2 * FF_HALF), lambda j, i: (i, j)), pl.BlockSpec((tm, FF_HALF), lambda j, i: (i, j))],
        out_shape=[jax.ShapeDtypeStruct((s_len, 2 * D_FF), BF16), jax.ShapeDtypeStruct((s_len, D_FF), BF16)],
        compiler_params=_cparams(("arbitrary", "parallel")),
        name=name,
    )(hn, w_gu)


def ffn_out_dx(d_y, w_down, gu, name):
    s_len, k = d_y.shape
    tm = min(FFN_TM, s_len)

    def body(a_ref, b_ref, gu_ref, dgu_ref):
        da = lax.dot_general(a_ref[...], b_ref[...], NT_DIMS, preferred_element_type=F32)
        gt = gu_ref[:, :FF_HALF].astype(F32)
        up = gu_ref[:, FF_HALF:].astype(F32)
        sg = jax.nn.sigmoid(gt)
        dgu_ref[:, :FF_HALF] = (da * up * sg * (1.0 + gt * (1.0 - sg))).astype(dgu_ref.dtype)
        dgu_ref[:, FF_HALF:] = (da * gt * sg).astype(dgu_ref.dtype)

    pair = pl.BlockSpec((tm, 2 * FF_HALF), lambda j, i: (i, j))
    return pl.pallas_call(
        body,
        grid=(2, s_len // tm),
        in_specs=[pl.BlockSpec((tm, k), lambda j, i: (i, 0)), pl.BlockSpec((FF_HALF, k), lambda j, i: (j, 0)), pair],
        out_specs=pair,
        out_shape=jax.ShapeDtypeStruct((s_len, 2 * D_FF), BF16),
        compiler_params=_cparams(("parallel", "arbitrary")),
        name=name,
    )(d_y, w_down, gu)


def rowwise(name, body, rows, row_ins, full_ins, row_outs, acc_outs=(), into=None, tile=None):
    tile = min(tile or ROW_TILE, rows)
    into = into or {}
    in_specs = [pl.BlockSpec((tile, w), lambda i, cb=cb: (i, cb)) for _, w, cb in row_ins]
    in_specs += [pl.BlockSpec(a.shape, lambda i, nd=a.ndim: (0,) * nd) for a in full_ins]
    in_specs += [ANY for _ in into]
    n_in = len(row_ins) + len(full_ins)
    aliases = {n_in + k: oi for k, oi in enumerate(into)}
    out_specs, out_shape = [], []
    for ro in row_outs:
        w, dt, full_w, cb = ro if len(ro) == 4 else (*ro, ro[0], 0)
        out_specs.append(pl.BlockSpec((tile, w), lambda i, cb=cb: (i, cb)))
        out_shape.append(jax.ShapeDtypeStruct((rows, full_w), dt))
    out_specs += [pl.BlockSpec(s, lambda i, nd=len(s): (0,) * nd) for s, _ in acc_outs]
    out_shape += [jax.ShapeDtypeStruct(s, dt) for s, dt in acc_outs]
    n_refs = n_in

    def call_body(*refs):
        body(*refs[:n_refs], *refs[n_refs + len(into):])

    outs = pl.pallas_call(
        call_body,
        grid=(rows // tile,),
        in_specs=in_specs,
        out_specs=out_specs,
        out_shape=out_shape,
        input_output_aliases=aliases,
        compiler_params=_cparams(("arbitrary",)),
        name=name,
    )(*[a for a, _, _ in row_ins], *full_ins, *into.values())
    return outs


def _whole(a):
    return (a, a.shape[1], 0)


def _acc(ref, val):
    @pl.when(pl.program_id(0) == 0)
    def _():
        ref[...] = val

    @pl.when(pl.program_id(0) > 0)
    def _():
        ref[...] += val


def _rms(x, g):
    return x * lax.rsqrt(jnp.mean(x * x, axis=-1, keepdims=True) + EPS) * g


def _layer_norm(x, g, b):
    mu = jnp.mean(x, axis=-1, keepdims=True)
    xc = x - mu
    return xc * lax.rsqrt(jnp.mean(xc * xc, axis=-1, keepdims=True) + EPS) * g + b


def _silu(x):
    return x * jax.nn.sigmoid(x)


def _rope(x, cc, sa, sb):
    return x * cc + pltpu.roll(x, HALF_ROPE, 1) * sa + pltpu.roll(x, HP - HALF_ROPE, 1) * sb


def _rope_t(dy, cc, sa, sb):
    return dy * cc + pltpu.roll(dy * sa, HP - HALF_ROPE, 1) + pltpu.roll(dy * sb, HALF_ROPE, 1)


def rope_tables(pos_col, rows):
    lane = np.arange(HP)
    idx = np.where(lane < KR_LANE + HALF_ROPE, lane - KR_LANE, lane - KR_LANE - HALF_ROPE)
    in_rope = (lane >= KR_LANE) & (lane < KR_LANE + ROPE)
    inv_freq = (np.float32(ROPE_THETA) ** (-np.arange(0, ROPE, 2, dtype=np.float32) / np.float32(ROPE))).astype(np.float32)
    freq_row = np.where(in_rope, inv_freq[np.clip(idx, 0, HALF_ROPE - 1)], 0.0).astype(np.float32)[None, :]
    first = ((lane >= KR_LANE) & (lane < KR_LANE + HALF_ROPE)).astype(np.float32)[None, :]
    second = ((lane >= KR_LANE + HALF_ROPE) & (lane < KR_LANE + ROPE)).astype(np.float32)[None, :]

    def body(pos_ref, f_ref, a_ref, b_ref, cc_ref, sa_ref, sb_ref):
        ang = pos_ref[...].astype(F32) * f_ref[...]
        s = jnp.sin(ang)
        cc_ref[...] = jnp.cos(ang)
        sa_ref[...] = s * b_ref[...]
        sb_ref[...] = -s * a_ref[...]

    return rowwise("rope_tables", body, rows, [_whole(pos_col)], [jnp.asarray(freq_row), jnp.asarray(first), jnp.asarray(second)],
                   [(HP, F32)] * 3)


def _causal_mask(t):
    r = lax.broadcasted_iota(jnp.int32, (t, t), 0)
    c = lax.broadcasted_iota(jnp.int32, (t, t), 1)
    return r, c


NT_DIMS = (((1,), (1,)), ((), ()))


def _carried(carry, refs, n_in, n_out, n_scratch):
    if carry is None:
        return refs, None, None
    ni, no = len(carry.ins), len(carry.outs)
    own_in, ex_in = refs[:n_in], refs[n_in:n_in + ni]
    own_out, ex_out = refs[n_in + ni:n_in + ni + n_out], refs[n_in + ni + n_out:n_in + ni + n_out + no]
    scratch = refs[n_in + ni + n_out + no:]
    sems = scratch[n_scratch:]
    return (*own_in, *own_out, *scratch[:n_scratch]), (lambda: carry.start(ex_in, ex_out, *sems)), (lambda: carry.finish(ex_in, ex_out, *sems))


def _carry_specs(carry):
    if carry is None:
        return [], [], [], [], []
    sems = [pltpu.SemaphoreType.DMA((carry.n_sems,)), pltpu.SemaphoreType.DMA((carry.n_sems,))]
    return [ANY] * len(carry.ins), [ANY] * len(carry.outs), list(carry.outs), sems, list(carry.ins)


def attention_fwd(q, k, v, name, carry=None):
    s_len = q.shape[0]
    t = min(ATT_TILE_FWD, s_len)
    nb = s_len // t
    hb = ATT_HEADS
    w = hb * HP
    nh = N_HEADS // hb

    def body(*refs):
        (q_ref, k_ref, v_ref, o_ref, lse_ref, m_sc, acc_sc), start, finish = _carried(carry, refs, 3, 2, 2)
        qi = pl.program_id(1)
        if start is not None:
            pl.when((pl.program_id(0) == 0) & (qi == 0))(start)
        m_sc[...] = jnp.full_like(m_sc, -jnp.inf)
        acc_sc[...] = jnp.zeros_like(acc_sc)

        def block(j, masked):
            ks = pl.ds(pl.multiple_of(j * t, t), t)
            for hh in range(hb):
                ls = slice(hh * HP, (hh + 1) * HP)
                s = lax.dot_general(q_ref[:, ls], k_ref[ks, ls], NT_DIMS, preferred_element_type=F32) * ATT_SCALE
                if masked:
                    r, c = _causal_mask(t)
                    s = jnp.where(c <= r, s, -jnp.inf)
                m_old = m_sc[hh]
                m_new = jnp.maximum(m_old, jnp.max(s, axis=-1, keepdims=True))
                p = jnp.exp(s - m_new)
                acc_sc[hh] = jnp.exp(m_old - m_new) * acc_sc[hh] + jnp.dot(p.astype(BF16), v_ref[ks, ls], preferred_element_type=F32)
                m_sc[hh] = m_new

        def loop_body(j, carry):
            block(j, False)
            return carry

        lax.fori_loop(0, qi, loop_body, 0)
        block(qi, True)
        lane = lax.broadcasted_iota(jnp.int32, (t, HP), 1)
        for hh in range(hb):
            acc = acc_sc[hh]
            l = jnp.sum(jnp.where(lane == VDIM, acc, 0.0), axis=-1, keepdims=True)
            o_ref[:, hh * HP:(hh + 1) * HP] = jnp.where(lane < VDIM, acc / l, 0.0).astype(o_ref.dtype)
            lse_ref[hh] = m_sc[hh] + jnp.log(l)
        if finish is not None:
            pl.when((pl.program_id(0) == nh - 1) & (qi == nb - 1))(finish)

    ex_in_specs, ex_out_specs, ex_out_shape, ex_scratch, ex_inputs = _carry_specs(carry)
    resident = pl.BlockSpec((s_len, w), lambda h, qi: (0, h))
    o, lse, *carried = pl.pallas_call(
        body,
        grid=(nh, nb),
        in_specs=[pl.BlockSpec((t, w), lambda h, qi: (qi, h)), resident, resident] + ex_in_specs,
        out_specs=[pl.BlockSpec((t, w), lambda h, qi: (qi, h)), pl.BlockSpec((hb, t, 1), lambda h, qi: (h, qi, 0))] + ex_out_specs,
        out_shape=[jax.ShapeDtypeStruct((s_len, HW), BF16), jax.ShapeDtypeStruct((N_HEADS, s_len, 1), F32)] + ex_out_shape,
        scratch_shapes=[pltpu.VMEM((hb, t, 1), F32), pltpu.VMEM((hb, t, HP), F32)] + ex_scratch,
        compiler_params=_cparams(("arbitrary", "arbitrary")),
        name=name,
    )(q, k, v, *ex_inputs)
    return o, lse, carried


def attention_delta(do, o):
    s_len = do.shape[0]
    t = min(ROW_TILE, s_len)

    def body(do_ref, o_ref, d_ref):
        prod = do_ref[...].astype(F32) * o_ref[...].astype(F32)
        for h in range(N_HEADS):
            d_ref[h] = jnp.sum(prod[:, h * HP:(h + 1) * HP], axis=-1, keepdims=True)

    return pl.pallas_call(
        body,
        grid=(s_len // t,),
        in_specs=[pl.BlockSpec((t, HW), lambda i: (i, 0))] * 2,
        out_specs=pl.BlockSpec((N_HEADS, t, 1), lambda i: (0, i, 0)),
        out_shape=jax.ShapeDtypeStruct((N_HEADS, s_len, 1), F32),
        compiler_params=_cparams(("arbitrary",)),
        name="attention_delta",
    )(do, o)


TN_DIMS = (((0,), (0,)), ((), ()))


def attention_bwd(q, k, v, do, lse_row, delta_row, name, carry=None):
    s_len = q.shape[0]
    t = min(ATT_TILE_BWD, s_len)
    nb = s_len // t
    hb = ATT_HEADS
    w = hb * HP
    nh = N_HEADS // hb

    def body(*refs):
        (q_ref, k_ref, v_ref, do_ref, lse_ref, dl_ref, dq_ref, dk_ref, dv_ref, dk_sc, dv_sc), start, finish = _carried(carry, refs, 6, 3, 2)
        ki = pl.program_id(1)
        if start is not None:
            pl.when((pl.program_id(0) == 0) & (ki == 0))(start)

        @pl.when(ki == 0)
        def _():
            dq_ref[...] = jnp.zeros_like(dq_ref)

        dk_sc[...] = jnp.zeros_like(dk_sc)
        dv_sc[...] = jnp.zeros_like(dv_sc)

        def block(j, masked):
            qs = pl.ds(pl.multiple_of(j * t, t), t)
            for hh in range(hb):
                ls = slice(hh * HP, (hh + 1) * HP)
                qb = q_ref[qs, ls]
                dob = do_ref[qs, ls]
                kb = k_ref[:, ls]
                st = lax.dot_general(kb, qb, NT_DIMS, preferred_element_type=F32) * ATT_SCALE
                pt = jnp.exp(st - lse_ref[hh, j])
                if masked:
                    r, c = _causal_mask(t)
                    pt = jnp.where(r <= c, pt, 0.0)
                dv_sc[hh] += jnp.dot(pt.astype(BF16), dob, preferred_element_type=F32)
                dpt = lax.dot_general(v_ref[:, ls], dob, NT_DIMS, preferred_element_type=F32)
                dst = (pt * (dpt - dl_ref[hh, j]) * ATT_SCALE).astype(BF16)
                dk_sc[hh] += jnp.dot(dst, qb, preferred_element_type=F32)
                dq_ref[qs, ls] += lax.dot_general(dst, kb, TN_DIMS, preferred_element_type=F32)

        block(ki, True)

        def loop_body(j, carry):
            block(j, False)
            return carry

        lax.fori_loop(ki + 1, nb, loop_body, 0)
        for hh in range(hb):
            ls = slice(hh * HP, (hh + 1) * HP)
            dk_ref[:, ls] = dk_sc[hh].astype(dk_ref.dtype)
            dv_ref[:, ls] = dv_sc[hh].astype(dv_ref.dtype)
        if finish is not None:
            pl.when((pl.program_id(0) == nh - 1) & (ki == nb - 1))(finish)

    ex_in_specs, ex_out_specs, ex_out_shape, ex_scratch, ex_inputs = _carry_specs(carry)
    k_spec = pl.BlockSpec((t, w), lambda h, ki: (ki, h))
    resident = pl.BlockSpec((s_len, w), lambda h, ki: (0, h))
    row_spec = pl.BlockSpec((hb, nb, 1, t), lambda h, ki: (h, 0, 0, 0))
    dq, dk, dv, *carried = pl.pallas_call(
        body,
        grid=(nh, nb),
        in_specs=[resident, k_spec, k_spec, resident, row_spec, row_spec] + ex_in_specs,
        out_specs=[resident, k_spec, k_spec] + ex_out_specs,
        out_shape=[jax.ShapeDtypeStruct((s_len, HW), F32), jax.ShapeDtypeStruct((s_len, HW), BF16), jax.ShapeDtypeStruct((s_len, HW), BF16)]
        + ex_out_shape,
        scratch_shapes=[pltpu.VMEM((hb, t, HP), F32), pltpu.VMEM((hb, t, HP), F32)] + ex_scratch,
        compiler_params=_cparams(("arbitrary", "arbitrary")),
        name=name,
    )(q, k, v, do, lse_row, delta_row, *ex_inputs)
    return dq, dk, dv, carried


CONV_PAD = 32


def conv_fwd(z, conv_w, conv_b):
    s_len = z.shape[0]
    ch = min(CONV_CHUNK, s_len)

    def body(ag_ref, w_ref, b_ref, c_ref, pad_ref):
        pad_ref[0:CONV_PAD, :] = jnp.zeros((CONV_PAD, LANE), F32)
        pad_ref[CONV_PAD:CONV_PAD + s_len, :] = ag_ref[:, 0:LANE].astype(F32) * jax.nn.sigmoid(ag_ref[:, LANE:2 * LANE].astype(F32))

        def chunk(i, carry):
            base = pl.multiple_of(i * ch, ch)
            acc = jnp.zeros((ch, LANE), F32) + b_ref[...]
            for kk in range(CONV_W):
                acc = acc + pad_ref[pl.ds(base + CONV_PAD - (CONV_W - 1) + kk, ch), :] * w_ref[kk:kk + 1, :]
            c_ref[pl.ds(base, ch), :] = acc
            return carry

        lax.fori_loop(0, s_len // ch, chunk, 0)

    nblk = CONV_C // LANE
    return pl.pallas_call(
        body,
        grid=(nblk,),
        in_specs=[pl.BlockSpec((s_len, 2 * LANE), lambda j: (0, ZC // (2 * LANE) + j)),
                  pl.BlockSpec((CONV_W, LANE), lambda j: (0, j)), pl.BlockSpec((1, LANE), lambda j: (0, j))],
        out_specs=pl.BlockSpec((s_len, LANE), lambda j: (0, j)),
        out_shape=jax.ShapeDtypeStruct((s_len, CONV_C), F32),
        scratch_shapes=[pltpu.VMEM((s_len + CONV_PAD, LANE), F32)],
        compiler_params=_cparams(("arbitrary",)),
        name="conv_fwd",
    )(z, conv_w, conv_b)


def conv_bwd(z, dc, conv_w, dz):
    s_len = z.shape[0]
    ch = min(CONV_CHUNK, s_len)

    def body(ag_ref, dc_ref, w_ref, dz_in, dag_ref, dw_ref, db_ref, pad_ref, dpad_ref, wacc_ref):
        del dz_in
        pad_ref[0:CONV_PAD, :] = jnp.zeros((CONV_PAD, LANE), F32)
        pad_ref[CONV_PAD:CONV_PAD + s_len, :] = ag_ref[:, 0:LANE].astype(F32) * jax.nn.sigmoid(ag_ref[:, LANE:2 * LANE].astype(F32))
        dpad_ref[0:s_len, :] = dc_ref[...]
        dpad_ref[s_len:s_len + CONV_PAD, :] = jnp.zeros((CONV_PAD, LANE), F32)
        wacc_ref[...] = jnp.zeros_like(wacc_ref)
        db_ref[...] = jnp.sum(dc_ref[...], axis=0, keepdims=True)

        def chunk(i, carry):
            base = pl.multiple_of(i * ch, ch)
            dcc = dpad_ref[pl.ds(base, ch), :]
            dh = jnp.zeros((ch, LANE), F32)
            for kk in range(CONV_W):
                dh = dh + dpad_ref[pl.ds(base + (CONV_W - 1) - kk, ch), :] * w_ref[kk:kk + 1, :]
                prod = dcc * pad_ref[pl.ds(base + CONV_PAD - (CONV_W - 1) + kk, ch), :]
                wacc_ref[kk * 8:(kk + 1) * 8, :] += prod.reshape(ch // 8, 8, LANE).sum(axis=0)
            a = ag_ref[pl.ds(base, ch), 0:LANE].astype(F32)
            sgc = jax.nn.sigmoid(ag_ref[pl.ds(base, ch), LANE:2 * LANE].astype(F32))
            dag_ref[pl.ds(base, ch), 0:LANE] = (dh * sgc).astype(dag_ref.dtype)
            dag_ref[pl.ds(base, ch), LANE:2 * LANE] = (dh * a * sgc * (1.0 - sgc)).astype(dag_ref.dtype)
            return carry

        lax.fori_loop(0, s_len // ch, chunk, 0)
        for kk in range(CONV_W):
            dw_ref[kk:kk + 1, :] = jnp.sum(wacc_ref[kk * 8:(kk + 1) * 8, :], axis=0, keepdims=True)

    nblk = CONV_C // LANE
    pair = pl.BlockSpec((s_len, 2 * LANE), lambda j: (0, ZC // (2 * LANE) + j))
    return pl.pallas_call(
        body,
        grid=(nblk,),
        in_specs=[pair, pl.BlockSpec((s_len, LANE), lambda j: (0, j)), pl.BlockSpec((CONV_W, LANE), lambda j: (0, j)), ANY],
        out_specs=[pair, pl.BlockSpec((CONV_W, LANE), lambda j: (0, j)), pl.BlockSpec((1, LANE), lambda j: (0, j))],
        out_shape=[jax.ShapeDtypeStruct(dz.shape, dz.dtype), jax.ShapeDtypeStruct((CONV_W, CONV_C), F32), jax.ShapeDtypeStruct((1, CONV_C), F32)],
        scratch_shapes=[pltpu.VMEM((s_len + CONV_PAD, LANE), F32), pltpu.VMEM((s_len + CONV_PAD, LANE), F32),
                        pltpu.VMEM((CONV_W * 8, LANE), F32)],
        input_output_aliases={3: 0},
        compiler_params=_cparams(("arbitrary",)),
        name="conv_bwd",
    )(z, dc, conv_w, dz)


POOL_PAD = 16


def _pool_count(base, ch, w):
    t = base + lax.broadcasted_iota(jnp.int32, (ch, 1), 0)
    return jnp.minimum(t + 1, w).astype(F32)


def pool_fwd(z, pool_w, pool_scale):
    s_len = z.shape[0]
    ch = min(CONV_CHUNK, s_len)

    def body(u_ref, pw_ref, sc_ref, m_ref, pad_ref):
        gi = pl.program_id(0)
        pad_ref[0:POOL_PAD, :] = jnp.zeros((POOL_PAD, LANE), F32)
        pad_ref[POOL_PAD:POOL_PAD + s_len, :] = u_ref[...].astype(F32)

        def run(w):
            def chunk(i, carry):
                base = pl.multiple_of(i * ch, ch)
                acc = jnp.zeros((ch, LANE), F32)
                for j in range(w):
                    acc = acc + pad_ref[pl.ds(base + POOL_PAD - j, ch), :]
                d = acc / _pool_count(base, ch, w) - pad_ref[pl.ds(base + POOL_PAD, ch), :]
                md = jnp.dot(d.astype(BF16), pw_ref[0], preferred_element_type=F32)
                m_ref[pl.ds(base, ch), :] = (md * sc_ref[...]).astype(m_ref.dtype)
                return carry

            lax.fori_loop(0, s_len // ch, chunk, 0)

        for g, w in enumerate(POOL_WINDOWS):
            pl.when(gi == g)(functools.partial(run, w))

    return pl.pallas_call(
        body,
        grid=(POOL_G,),
        in_specs=[pl.BlockSpec((s_len, LANE), lambda g: (0, ZP // LANE + g)), pl.BlockSpec((1, POOL_GD, POOL_GD), lambda g: (g, 0, 0)),
                  pl.BlockSpec((1, LANE), lambda g: (0, g))],
        out_specs=pl.BlockSpec((s_len, LANE), lambda g: (0, g)),
        out_shape=jax.ShapeDtypeStruct((s_len, POOL_C), BF16),
        scratch_shapes=[pltpu.VMEM((s_len + POOL_PAD, LANE), F32)],
        compiler_params=_cparams(("arbitrary",)),
        name="pool_fwd",
    )(z, pool_w, pool_scale)


def pool_bwd(z, dm, pool_w, pool_scale, dz):
    s_len = z.shape[0]
    ch = min(CONV_CHUNK, s_len)

    def body(u_ref, dm_ref, pw_ref, sc_ref, dz_in, du_ref, dpw_ref, dsc_ref, pad_ref, epad_ref, dd_ref, sacc_ref):
        del dz_in
        gi = pl.program_id(0)
        pad_ref[0:POOL_PAD, :] = jnp.zeros((POOL_PAD, LANE), F32)
        pad_ref[POOL_PAD:POOL_PAD + s_len, :] = u_ref[...].astype(F32)
        epad_ref[s_len:s_len + POOL_PAD, :] = jnp.zeros((POOL_PAD, LANE), F32)
        dpw_ref[...] = jnp.zeros_like(dpw_ref)
        sacc_ref[...] = jnp.zeros_like(sacc_ref)

        def run(w):
            def first(i, carry):
                base = pl.multiple_of(i * ch, ch)
                acc = jnp.zeros((ch, LANE), F32)
                for j in range(w):
                    acc = acc + pad_ref[pl.ds(base + POOL_PAD - j, ch), :]
                cnt = _pool_count(base, ch, w)
                d = (acc / cnt - pad_ref[pl.ds(base + POOL_PAD, ch), :]).astype(BF16)
                md = jnp.dot(d, pw_ref[0], preferred_element_type=F32)
                dmc = dm_ref[pl.ds(base, ch), :]
                sacc_ref[...] += (dmc * md).reshape(ch // 8, 8, LANE).sum(axis=0)
                dmd = (dmc * sc_ref[...]).astype(BF16)
                dpw_ref[0] += lax.dot_general(d, dmd, (((0,), (0,)), ((), ())), preferred_element_type=F32)
                dd = lax.dot_general(dmd, pw_ref[0], (((1,), (1,)), ((), ())), preferred_element_type=F32)
                dd_ref[pl.ds(base, ch), :] = dd
                epad_ref[pl.ds(base, ch), :] = dd / cnt
                return carry

            lax.fori_loop(0, s_len // ch, first, 0)

            def second(i, carry):
                base = pl.multiple_of(i * ch, ch)
                acc = jnp.zeros((ch, LANE), F32)
                for j in range(w):
                    acc = acc + epad_ref[pl.ds(base + j, ch), :]
                du_ref[pl.ds(base, ch), :] = (acc - dd_ref[pl.ds(base, ch), :]).astype(du_ref.dtype)
                return carry

            lax.fori_loop(0, s_len // ch, second, 0)

        for g, w in enumerate(POOL_WINDOWS):
            pl.when(gi == g)(functools.partial(run, w))
        dsc_ref[...] = jnp.sum(sacc_ref[...], axis=0, keepdims=True)

    return pl.pallas_call(
        body,
        grid=(POOL_G,),
        in_specs=[pl.BlockSpec((s_len, LANE), lambda g: (0, ZP // LANE + g)), pl.BlockSpec((s_len, LANE), lambda g: (0, g)),
                  pl.BlockSpec((1, POOL_GD, POOL_GD), lambda g: (g, 0, 0)), pl.BlockSpec((1, LANE), lambda g: (0, g)), ANY],
        out_specs=[pl.BlockSpec((s_len, LANE), lambda g: (0, ZP // LANE + g)), pl.BlockSpec((1, POOL_GD, POOL_GD), lambda g: (g, 0, 0)),
                   pl.BlockSpec((1, LANE), lambda g: (0, g))],
        out_shape=[jax.ShapeDtypeStruct(dz.shape, dz.dtype), jax.ShapeDtypeStruct((POOL_G, POOL_GD, POOL_GD), F32),
                   jax.ShapeDtypeStruct((1, POOL_C), F32)],
        scratch_shapes=[pltpu.VMEM((s_len + POOL_PAD, LANE), F32), pltpu.VMEM((s_len + POOL_PAD, LANE), F32),
                        pltpu.VMEM((s_len, LANE), F32), pltpu.VMEM((8, LANE), F32)],
        input_output_aliases={4: 0},
        compiler_params=_cparams(("arbitrary",)),
        name="pool_bwd",
    )(z, dm, pool_w, pool_scale, dz)


def _row(v):
    return v.reshape(1, -1)


def _rms_body(x_ref, g_ref, o_ref):
    o_ref[...] = _rms(x_ref[...], g_ref[...]).astype(o_ref.dtype)


def _post_body(y_ref, x_ref, g_ref, o_ref):
    o_ref[...] = x_ref[...] + _rms(y_ref[...], g_ref[...])


def _post_bwd_body(y_ref, dh_ref, g_ref, dy_ref, dg_ref):
    _, vjp = jax.vjp(_rms, y_ref[...], g_ref[...])
    dy, dg = vjp(dh_ref[...])
    dy_ref[...] = dy.astype(dy_ref.dtype)
    _acc(dg_ref, dg)


def _pre_bwd_body(x_ref, dhn_ref, dres_ref, g_ref, dx_ref, dg_ref):
    _, vjp = jax.vjp(_rms, x_ref[...], g_ref[...])
    dx, dg = vjp(dhn_ref[...])
    dx_ref[...] = dres_ref[...] + dx
    _acc(dg_ref, dg)


def _post_next_body(y_ref, x_ref, g_ref, gn_ref, o_ref, n_ref):
    o = x_ref[...] + _rms(y_ref[...], g_ref[...])
    o_ref[...] = o
    n_ref[...] = _rms(o, gn_ref[...]).astype(n_ref.dtype)


def _pre_post_bwd_body(x_ref, dhn_ref, dres_ref, y_ref, g_ref, gy_ref, dx_ref, dy_ref, dg_ref, dgy_ref):
    _, vjp = jax.vjp(_rms, x_ref[...], g_ref[...])
    dx, dg = vjp(dhn_ref[...])
    dx = dres_ref[...] + dx
    dx_ref[...] = dx
    _, vjp_y = jax.vjp(_rms, y_ref[...], gy_ref[...])
    dy, dgy = vjp_y(dx)
    dy_ref[...] = dy.astype(dy_ref.dtype)
    _acc(dg_ref, dg)
    _acc(dgy_ref, dgy)


def mixer_fwd(x, tabs, w, tag, carry=None, head=None, h=None):
    s_len = x.shape[0]
    cc, sa, sb = tabs
    sv = {"x": x}

    if h is None:
        (h,) = rowwise("mix_norm_pre" + tag, _rms_body, s_len, [_whole(x)], [_row(w["mix_norm_pre"])], [(D_MODEL, BF16)])
    if head is None:
        z = mm(h, w["w_in"], out_dtype=BF16, name="in_proj" + tag)
    else:
        z, arrived = mm(h, w["w_in"], out_dtype=BF16, name="in_proj" + tag, carry=head[0])
        w = {**w, **head[1](arrived)}

    def prep_body(z_ref, cc_ref, sa_ref, sb_ref, qg_ref, kg_ref, qn_ref, ckv_ref, kr_ref):
        qn_ref[...] = _rms(z_ref[:, 0:Q_RANK].astype(F32), qg_ref[...]).astype(qn_ref.dtype)
        ckv_ref[...] = _rms(z_ref[:, Q_RANK:Q_RANK + KV_RANK].astype(F32), kg_ref[...]).astype(ckv_ref.dtype)
        kr_ref[...] = _rope(z_ref[:, Q_RANK + KV_RANK:ZA_W].astype(F32), cc_ref[...], sa_ref[...], sb_ref[...])

    qn, ckvn, kr = rowwise("attn_prep" + tag, prep_body, s_len, [(z, ZA_W, ZA // ZA_W), _whole(cc), _whole(sa), _whole(sb)],
                           [_row(w["q_norm"]), _row(w["kv_norm"])], [(Q_RANK, BF16), (KV_RANK, BF16), (HP, F32)])
    q_raw = mm(qn, w["w_uq"], out_dtype=BF16, name="q_proj" + tag)
    kv_raw = mm(ckvn, w["w_ukv"], out_dtype=BF16, name="kv_proj" + tag)

    def qkv_body(q_ref, kv_ref, kr_ref, cc_ref, sa_ref, sb_ref, qo_ref, ko_ref, vo_ref):
        c_, a_, b_, kro = cc_ref[...], sa_ref[...], sb_ref[...], kr_ref[...]
        for hh in range(N_HEADS):
            sl = slice(hh * HP, (hh + 1) * HP)
            qo_ref[:, sl] = _rope(q_ref[:, sl].astype(F32), c_, a_, b_).astype(qo_ref.dtype)
            ko_ref[:, sl] = (kv_ref[:, sl].astype(F32) + kro).astype(ko_ref.dtype)
        lane = lax.broadcasted_iota(jnp.int32, (q_ref.shape[0], HW), 1)
        vo_ref[...] = jnp.where((lane & (HP - 1)) == VDIM, 1.0, kv_ref[:, HW:2 * HW].astype(F32)).astype(vo_ref.dtype)

    q, k, v = rowwise("qkv_rope" + tag, qkv_body, s_len, [_whole(q_raw), _whole(kv_raw), _whole(kr), _whole(cc), _whole(sa), _whole(sb)], [],
                      [(HW, BF16)] * 3)
    o, lse, carried = attention_fwd(q, k, v, "attention_fwd" + tag, carry)
    y_attn = mm(o, w["w_attn_o"], out_dtype=BF16, name="attn_out" + tag)

    c = conv_fwd(z, w["conv_w"], _row(w["conv_b"]))

    def ln_body(c_ref, g_ref, b_ref, o_ref):
        o_ref[...] = _silu(_layer_norm(c_ref[...], g_ref[...], b_ref[...])).astype(o_ref.dtype)

    (cs,) = rowwise("conv_ln_silu" + tag, ln_body, s_len, [_whole(c)], [_row(w["conv_ln_g"]), _row(w["conv_ln_b"])], [(CONV_C, BF16)])
    y_conv = mm(cs, w["w_conv_o"], out_dtype=BF16, name="conv_out" + tag)

    m = pool_fwd(z, w["pool_w"], _row(w["pool_scale"]))
    y_pool = mm(m, w["w_pool_o"], out_dtype=BF16, name="pool_out" + tag)

    def merge_body(ya_ref, yc_ref, yp_ref, gl_ref, o_ref):
        gl = gl_ref[...].astype(F32)
        o_ref[...] = (jax.nn.sigmoid(gl[:, 0:D_MODEL]) * ya_ref[...].astype(F32) + jax.nn.sigmoid(gl[:, D_MODEL:2 * D_MODEL]) * yc_ref[...].astype(F32)
                      + jax.nn.sigmoid(gl[:, 2 * D_MODEL:3 * D_MODEL]) * yp_ref[...].astype(F32)).astype(o_ref.dtype)

    (merged,) = rowwise("gate_merge" + tag, merge_body, s_len, [_whole(y_attn), _whole(y_conv), _whole(y_pool), (z, 3 * D_MODEL, 0)], [],
                        [(D_MODEL, BF16)])
    mo = mm(merged, w["w_mix_o"], name="mix_out" + tag)

    h1, hn = rowwise("mix_norm_post" + tag, _post_next_body, s_len, [_whole(mo), _whole(x)],
                     [_row(w["mix_norm_post"]), _row(w["ffn_norm_pre"])], [(D_MODEL, F32), (D_MODEL, BF16)])
    sv.update(h=h, z=z, qn=qn, ckvn=ckvn, q=q, k=k, v=v, o=o, lse=lse, c=c, cs=cs, m=m, y_attn=y_attn, y_conv=y_conv, y_pool=y_pool,
              merged=merged, mo=mo, w=w, hn=hn)
    return h1, sv, carried


def ffn_fwd(h1, hn, w, tag, next_gain=None):
    s_len = h1.shape[0]
    gu, act = ffn_in(hn, w["w_gu"], "ffn_in" + tag)
    y = mm(act, w["w_down"], name="ffn_out" + tag)
    if next_gain is None:
        (h2,) = rowwise("ffn_norm_post" + tag, _post_body, s_len, [_whole(y), _whole(h1)], [_row(w["ffn_norm_post"])], [(D_MODEL, F32)])
        h_next = None
    else:
        h2, h_next = rowwise("ffn_norm_post" + tag, _post_next_body, s_len, [_whole(y), _whole(h1)],
                             [_row(w["ffn_norm_post"]), _row(next_gain)], [(D_MODEL, F32), (D_MODEL, BF16)])
    return h2, dict(h1=h1, hn=hn, gu=gu, act=act, y=y), h_next


def ffn_bwd(dh2, sv, w, tag, mo, post=None):
    s_len = dh2.shape[0]
    g = {}
    if post is None:
        d_y, g["ffn_norm_post"] = rowwise("ffn_norm_post_bwd" + tag, _post_bwd_body, s_len, [_whole(sv["y"]), _whole(dh2)],
                                          [_row(w["ffn_norm_post"])], [(D_MODEL, BF16)], [((1, D_MODEL), F32)])
    else:
        d_y, g["ffn_norm_post"] = post
    g["w_down"] = mm(sv["act"], d_y, ta=True, name="ffn_out_dw" + tag)
    d_gu = ffn_out_dx(d_y, w["w_down"], sv["gu"], "ffn_out_dx" + tag)
    g["w_gu"] = mm(sv["hn"], d_gu, ta=True, name="ffn_in_dw" + tag)
    d_hn = mm(d_gu, w["w_gu"], tb=True, name="ffn_in_dx" + tag)
    dh1, d_mo, g["ffn_norm_pre"], d_mix_post = rowwise(
        "ffn_norm_pre_bwd" + tag, _pre_post_bwd_body, s_len, [_whole(sv["h1"]), _whole(d_hn), _whole(dh2), _whole(mo)],
        [_row(w["ffn_norm_pre"]), _row(w["mix_norm_post"])], [(D_MODEL, F32), (D_MODEL, BF16)], [((1, D_MODEL), F32), ((1, D_MODEL), F32)])
    return dh1, g, (d_mo, d_mix_post)


def mixer_bwd(dh1, sv, tabs, w, tag, post, carry=None, tail=None, prev=None):
    s_len = dh1.shape[0]
    cc, sa, sb = tabs
    g = {}

    d_mo, g["mix_norm_post"] = post
    g["w_mix_o"] = mm(sv["merged"], d_mo, ta=True, name="mix_out_dw" + tag)
    d_merged = mm(d_mo, w["w_mix_o"], tb=True, name="mix_out_dx" + tag)

    def merge_bwd_body(dm_ref, ya_ref, yc_ref, yp_ref, gl_ref, dya_ref, dyc_ref, dyp_ref, dgl_ref):
        dmg = dm_ref[...]
        for i, (y_ref, dy_ref) in enumerate(((ya_ref, dya_ref), (yc_ref, dyc_ref), (yp_ref, dyp_ref))):
            sg = jax.nn.sigmoid(gl_ref[:, i * D_MODEL:(i + 1) * D_MODEL].astype(F32))
            dy_ref[...] = (dmg * sg).astype(dy_ref.dtype)
            dgl_ref[:, i * D_MODEL:(i + 1) * D_MODEL] = (dmg * y_ref[...].astype(F32) * sg * (1.0 - sg)).astype(dgl_ref.dtype)

    d_ya, d_yc, d_yp, dz = rowwise(
        "gate_merge_bwd" + tag, merge_bwd_body, s_len,
        [_whole(d_merged), _whole(sv["y_attn"]), _whole(sv["y_conv"]), _whole(sv["y_pool"]), (sv["z"], 3 * D_MODEL, 0)], [],
        [(D_MODEL, BF16)] * 3 + [(3 * D_MODEL, BF16, ZW, 0)], tile=WIDE_ROW_TILE)

    g["w_pool_o"] = mm(sv["m"], d_yp, ta=True, name="pool_out_dw" + tag)
    d_m = mm(d_yp, w["w_pool_o"], tb=True, name="pool_out_dx" + tag)
    dz, g["pool_w"], g["pool_scale"] = pool_bwd(sv["z"], d_m, w["pool_w"], _row(w["pool_scale"]), dz)

    g["w_conv_o"] = mm(sv["cs"], d_yc, ta=True, name="conv_out_dw" + tag)
    d_cs = mm(d_yc, w["w_conv_o"], tb=True, name="conv_out_dx" + tag)

    def ln_bwd_body(c_ref, dcs_ref, g_ref, b_ref, dc_ref, dg_ref, db_ref):
        f = lambda c_, g_, b_: _silu(_layer_norm(c_, g_, b_))
        _, vjp = jax.vjp(f, c_ref[...], g_ref[...], b_ref[...])
        dc, dg, db = vjp(dcs_ref[...])
        dc_ref[...] = dc
        _acc(dg_ref, dg)
        _acc(db_ref, db)

    d_c, g["conv_ln_g"], g["conv_ln_b"] = rowwise("conv_ln_silu_bwd" + tag, ln_bwd_body, s_len, [_whole(sv["c"]), _whole(d_cs)],
                                                  [_row(w["conv_ln_g"]), _row(w["conv_ln_b"])], [(CONV_C, F32)],
                                                  [((1, CONV_C), F32), ((1, CONV_C), F32)])
    dz, g["conv_w"], g["conv_b"] = conv_bwd(sv["z"], d_c, w["conv_w"], dz)

    g["w_attn_o"] = mm(sv["o"], d_ya, ta=True, name="attn_out_dw" + tag)
    d_o = mm(d_ya, w["w_attn_o"], tb=True, out_dtype=BF16, name="attn_out_dx" + tag)
    delta = attention_delta(d_o, sv["o"])
    t_bwd = min(ATT_TILE_BWD, s_len)
    rows_of = lambda a: a.reshape(N_HEADS, s_len // t_bwd, 1, t_bwd)
    dq, dk, dv, carried = attention_bwd(sv["q"], sv["k"], sv["v"], d_o, rows_of(sv["lse"]), rows_of(delta), "attention_bwd" + tag, carry)

    def qkv_bwd_body(dq_ref, dk_ref, dv_ref, cc_ref, sa_ref, sb_ref, dqp_ref, dkv_ref, dkr_ref):
        c_, a_, b_ = cc_ref[...], sa_ref[...], sb_ref[...]
        dk_sum = jnp.zeros((dq_ref.shape[0], HP), F32)
        for hh in range(N_HEADS):
            sl = slice(hh * HP, (hh + 1) * HP)
            dqp_ref[:, sl] = _rope_t(dq_ref[:, sl], c_, a_, b_).astype(dqp_ref.dtype)
            dkh = dk_ref[:, sl].astype(F32)
            dkv_ref[:, sl] = dk_ref[:, sl]
            dk_sum = dk_sum + dkh
        dkv_ref[:, HW:2 * HW] = dv_ref[...]
        dkr_ref[...] = _rope_t(dk_sum, c_, a_, b_)

    dq_pre, dkv_pre, d_kr = rowwise("qkv_rope_bwd" + tag, qkv_bwd_body, s_len,
                                    [_whole(dq), _whole(dk), _whole(dv), _whole(cc), _whole(sa), _whole(sb)], [],
                                    [(HW, BF16), (2 * HW, BF16), (HP, F32)])
    g["w_uq"] = mm(sv["qn"], dq_pre, ta=True, name="q_proj_dw" + tag)
    d_qn = mm(dq_pre, w["w_uq"], tb=True, name="q_proj_dx" + tag)
    g["w_ukv"] = mm(sv["ckvn"], dkv_pre, ta=True, name="kv_proj_dw" + tag)
    d_ckvn = mm(dkv_pre, w["w_ukv"], tb=True, name="kv_proj_dx" + tag)

    def prep_bwd_body(z_ref, dqn_ref, dckv_ref, dkr_ref, qg_ref, kg_ref, dz_ref, dqg_ref, dkg_ref):
        _, vq = jax.vjp(_rms, z_ref[:, 0:Q_RANK].astype(F32), qg_ref[...])
        dcq, dqg = vq(dqn_ref[...])
        _, vk = jax.vjp(_rms, z_ref[:, Q_RANK:Q_RANK + KV_RANK].astype(F32), kg_ref[...])
        dckv, dkg = vk(dckv_ref[...])
        dz_ref[:, 0:Q_RANK] = dcq.astype(dz_ref.dtype)
        dz_ref[:, Q_RANK:Q_RANK + KV_RANK] = dckv.astype(dz_ref.dtype)
        dz_ref[:, Q_RANK + KV_RANK:ZA_W] = dkr_ref[...].astype(dz_ref.dtype)
        _acc(dqg_ref, dqg)
        _acc(dkg_ref, dkg)

    dz, g["q_norm"], g["kv_norm"] = rowwise("attn_prep_bwd" + tag, prep_bwd_body, s_len,
                                            [(sv["z"], ZA_W, ZA // ZA_W), _whole(d_qn), _whole(d_ckvn), _whole(d_kr)],
                                            [_row(w["q_norm"]), _row(w["kv_norm"])], [(ZA_W, BF16, ZW, ZA // ZA_W)],
                                            [((1, Q_RANK), F32), ((1, KV_RANK), F32)], into={0: dz})

    if tail is None:
        g["w_in"] = mm(sv["h"], dz, ta=True, name="in_proj_dw" + tag)
        d_h = mm(dz, w["w_in"], tb=True, name="in_proj_dx" + tag)
    else:
        g["w_in"], tailed_rest = mm(sv["h"], dz, ta=True, name="in_proj_dw" + tag, carry=tail[0](g))
        d_h, tailed_in = mm(dz, w["w_in"], tb=True, name="in_proj_dx" + tag, carry=tail[1](g))
        carried = [carried, tailed_in + tailed_rest]
    if prev is None:
        dx, g["mix_norm_pre"] = rowwise(
            "mix_norm_pre_bwd" + tag, _pre_bwd_body, s_len, [_whole(sv["x"]), _whole(d_h), _whole(dh1)], [_row(w["mix_norm_pre"])], [(D_MODEL, F32)], [((1, D_MODEL), F32)])
        return dx, g, carried, None
    y_prev, gain_prev = prev
    dx, d_y_prev, g["mix_norm_pre"], d_gain_prev = rowwise(
        "mix_norm_pre_bwd" + tag, _pre_post_bwd_body, s_len, [_whole(sv["x"]), _whole(d_h), _whole(dh1), _whole(y_prev)],
        [_row(w["mix_norm_pre"]), _row(gain_prev)], [(D_MODEL, F32), (D_MODEL, BF16)], [((1, D_MODEL), F32), ((1, D_MODEL), F32)])
    return dx, g, carried, (d_y_prev, d_gain_prev)


def loss_head(h, target):
    s_len = h.shape[0]

    def body(h_ref, t_ref, dy_ref, loss_ref):
        err = h_ref[...] - t_ref[...]
        dy_ref[...] = err * (1.0 / D_MODEL)
        part = 0.5 * jnp.sum(jnp.mean(err * err, axis=-1, keepdims=True), axis=0, keepdims=True)
        _acc(loss_ref, jnp.broadcast_to(part, (1, LANE)))

    return rowwise("loss_head", body, s_len, [_whole(h), _whole(target)], [], [(D_MODEL, F32)], [((1, LANE), F32)])


def local_step(x, pos_col, target, layers):
    s_len = x.shape[0]
    tabs = rope_tables(pos_col, s_len)
    h, h_normed, saved = x, None, []
    for li, w in enumerate(layers):
        h, sv_mix, _ = mixer_fwd(h, tabs, w, f"_l{li}", h=h_normed)
        next_gain = layers[li + 1]["mix_norm_pre"] if li + 1 < len(layers) else None
        h, sv_ffn, h_normed = ffn_fwd(h, sv_mix["hn"], w, f"_l{li}", next_gain)
        saved.append((sv_mix, sv_ffn))
    dh, loss = loss_head(h, target)
    grads, post = [None] * len(layers), None
    for li in reversed(range(len(layers))):
        dh, g_ffn, mix_post = ffn_bwd(dh, saved[li][1], layers[li], f"_l{li}", saved[li][0]["mo"], post)
        prev = (saved[li - 1][1]["y"], layers[li - 1]["ffn_norm_post"]) if li > 0 else None
        dh, g_mix, _, post = mixer_bwd(dh, saved[li][0], tabs, layers[li], f"_l{li}", mix_post, prev=prev)
        grads[li] = {**g_mix, **g_ffn}
    return loss[0, 0], dh, grads


def _pad_heads_cols(wm, per_head):
    r = wm.shape[0]
    return jnp.pad(wm.reshape(r, N_HEADS, per_head), ((0, 0), (0, 0), (0, HP - per_head))).reshape(r, HW)


def _unpad_heads_cols(wm, per_head):
    r = wm.shape[0]
    return wm.reshape(r, N_HEADS, HP)[:, :, :per_head].reshape(r, N_HEADS * per_head)


def align_weights(p):
    out = dict(p)
    if "w_in" in p:
        w_in = p["w_in"]
        r = w_in.shape[0]
        zeros = lambda n: jnp.zeros((r, n), w_in.dtype)
        conv = w_in[:, O_CONV:O_POOL].reshape(r, 2, CONV_C // LANE, LANE).transpose(0, 2, 1, 3).reshape(r, 2 * CONV_C)
        out["w_in"] = jnp.concatenate([
            w_in[:, O_GATE:D_IN], conv, w_in[:, O_POOL:O_GATE], w_in[:, O_Q:O_KR],
            zeros(KR_LANE), w_in[:, O_KR:O_CONV], zeros(HP - KR_LANE - ROPE)], axis=1)
    if "w_uq" in p:
        out["w_uq"] = _pad_heads_cols(p["w_uq"], NOPE + ROPE)
        out["w_ukv"] = jnp.concatenate([_pad_heads_cols(p["w_uk"], NOPE), _pad_heads_cols(p["w_uv"], VDIM)], axis=1)
        wo = p["w_attn_o"]
        out["w_attn_o"] = jnp.pad(wo.reshape(N_HEADS, VDIM, D_MODEL), ((0, 0), (0, HP - VDIM), (0, 0))).reshape(HW, D_MODEL)
        del out["w_uk"], out["w_uv"]
    if "w_gate" in p:
        out["w_gu"] = jnp.concatenate([p["w_gate"][:, :FF_HALF], p["w_up"][:, :FF_HALF], p["w_gate"][:, FF_HALF:], p["w_up"][:, FF_HALF:]], axis=1)
        del out["w_gate"], out["w_up"]
    return out


def unalign_grads(g):
    out = dict(g)
    if "w_in" in g:
        gi = g["w_in"]
        kr0 = ZA + Q_RANK + KV_RANK + KR_LANE
        r = gi.shape[0]
        conv = gi[:, ZC:ZP].reshape(r, CONV_C // LANE, 2, LANE).transpose(0, 2, 1, 3).reshape(r, 2 * CONV_C)
        out["w_in"] = jnp.concatenate([gi[:, ZA:ZA + Q_RANK + KV_RANK], gi[:, kr0:kr0 + ROPE], conv, gi[:, ZP:ZA], gi[:, ZG:ZC]], axis=1)
    if "w_uq" in g:
        out["w_uq"] = _unpad_heads_cols(g["w_uq"], NOPE + ROPE)
        out["w_uk"] = _unpad_heads_cols(g["w_ukv"][:, :HW], NOPE)
        out["w_uv"] = _unpad_heads_cols(g["w_ukv"][:, HW:], VDIM)
        out["w_attn_o"] = g["w_attn_o"].reshape(N_HEADS, HP, D_MODEL)[:, :VDIM].reshape(N_HEADS * VDIM, D_MODEL)
        del out["w_ukv"]
    if "w_gu" in g:
        gu = g["w_gu"]
        out["w_gate"] = jnp.concatenate([gu[:, 0:FF_HALF], gu[:, 2 * FF_HALF:3 * FF_HALF]], axis=1)
        out["w_up"] = jnp.concatenate([gu[:, FF_HALF:2 * FF_HALF], gu[:, 3 * FF_HALF:]], axis=1)
        del out["w_gu"]
    return out


MESH = pl.DeviceIdType.MESH
ANY = pl.BlockSpec(memory_space=pl.ANY)


def _place():
    return lax.axis_index("x"), lax.axis_index("y"), lax.axis_index("c")


def _other_chips(x, y):
    return [(1 - x, y), (x, 1 - y), (1 - x, 1 - y)]


def _half_rows(rows, c):
    assert rows % (2 * HALF_ALIGN) == 0, rows
    return pl.ds(pl.multiple_of(c * (rows // 2), HALF_ALIGN), rows // 2)


class GatherShards:
    def __init__(self, local):
        self.ins = list(local)
        self.outs = [jax.ShapeDtypeStruct((N_CHIPS, *a.shape), a.dtype) for a in local]
        self.n_sems = 6 * len(local)
        self.base = 0

    def _first(self, in_refs, out_refs, send_sems, recv_sems):
        x, y, c = _place()
        me = 2 * x + y
        chips = _other_chips(x, y)

        def copy(i, k, slot, core, to, src=None):
            dst = out_refs[i].at[slot, _half_rows(out_refs[i].shape[1], core)]
            return pltpu.make_async_remote_copy(src_ref=dst if src is None else src, dst_ref=dst, send_sem=send_sems.at[self.base + 6 * i + k],
                                                recv_sem=recv_sems.at[self.base + 6 * i + k], device_id=to, device_id_type=MESH)

        first = [copy(i, j, me, c, (*chip, c), src=in_refs[i].at[_half_rows(in_refs[i].shape[0], c)])
                 for i in range(len(in_refs)) for j, chip in enumerate(chips)]
        return first, copy

    def start(self, in_refs, out_refs, send_sems, recv_sems):
        first, _ = self._first(in_refs, out_refs, send_sems, recv_sems)
        for cp in first:
            cp.start()

    def finish(self, in_refs, out_refs, send_sems, recv_sems):
        first, copy = self._first(in_refs, out_refs, send_sems, recv_sems)
        x, y, c = _place()
        slots = [2 * cx + cy for cx, cy in _other_chips(x, y)]
        sibling = (x, y, 1 - c)
        passed = []
        for i in range(len(in_refs)):
            for j in range(3):
                copy(i, j, slots[j], c, sibling).wait_recv()
                fwd = copy(i, 3 + j, slots[j], c, sibling)
                fwd.start()
                passed.append(fwd)
        for i in range(len(in_refs)):
            for j in range(3):
                copy(i, 3 + j, slots[j], 1 - c, sibling).wait_recv()
        for cp in first + passed:
            cp.wait_send()


class ChipExchange:
    def __init__(self, parts):
        self.ins = list(parts)
        self.outs = [jax.ShapeDtypeStruct((3, *a.shape[1:]), a.dtype) for a in parts]
        self.n_sems = 3 * len(parts)
        self.base = 0

    def _copies(self, in_refs, out_refs, send_sems, recv_sems):
        x, y, c = _place()
        return [pltpu.make_async_remote_copy(src_ref=in_refs[i].at[2 * chip[0] + chip[1]], dst_ref=out_refs[i].at[j],
                                             send_sem=send_sems.at[self.base + 3 * i + j], recv_sem=recv_sems.at[self.base + 3 * i + j],
                                             device_id=(*chip, c), device_id_type=MESH)
                for i in range(len(in_refs)) for j, chip in enumerate(_other_chips(x, y))]

    def start(self, in_refs, out_refs, send_sems, recv_sems):
        for cp in self._copies(in_refs, out_refs, send_sems, recv_sems):
            cp.start()

    def finish(self, in_refs, out_refs, send_sems, recv_sems):
        copies = self._copies(in_refs, out_refs, send_sems, recv_sems)
        for cp in copies:
            cp.wait_recv()
        for cp in copies:
            cp.wait_send()


def run_exchange(ex, name):
    n_in, n_out = len(ex.ins), len(ex.outs)

    def body(*refs):
        ins, outs, sems = refs[:n_in], refs[n_in:n_in + n_out], refs[n_in + n_out:]
        ex.start(ins, outs, *sems)
        ex.finish(ins, outs, *sems)

    return pl.pallas_call(
        body,
        in_specs=[ANY] * n_in,
        out_specs=[ANY] * n_out,
        out_shape=list(ex.outs),
        scratch_shapes=[pltpu.SemaphoreType.DMA((ex.n_sems,)), pltpu.SemaphoreType.DMA((ex.n_sems,))],
        name=name,
    )(*ex.ins)


def sibling_swap(gs, name):
    n = len(gs)

    def body(*refs):
        g_refs, out_refs, (send_sems, recv_sems) = refs[:n], refs[n:2 * n], refs[2 * n:]
        x, y, c = _place()
        copies = []
        for i in range(n):
            for j in range(N_CHIPS):
                cp = pltpu.make_async_remote_copy(src_ref=g_refs[i].at[j, _half_rows(g_refs[i].shape[1], 1 - c)], dst_ref=out_refs[i].at[j],
                                                  send_sem=send_sems.at[4 * i + j], recv_sem=recv_sems.at[4 * i + j],
                                                  device_id=(x, y, 1 - c), device_id_type=MESH)
                cp.start()
                copies.append(cp)
        for cp in copies:
            cp.wait_recv()
        for cp in copies:
            cp.wait_send()

    return pl.pallas_call(
        body,
        in_specs=[ANY] * n,
        out_specs=[ANY] * n,
        out_shape=[jax.ShapeDtypeStruct((N_CHIPS, a.shape[1] // 2, a.shape[2]), a.dtype) for a in gs],
        scratch_shapes=[pltpu.SemaphoreType.DMA((4 * n,)), pltpu.SemaphoreType.DMA((4 * n,))],
        name=name,
    )(*gs)


def sibling_gather(fs):
    n = len(fs)
    layers = fs[0].shape[0]

    def body(*refs):
        out_refs, (send_sems, recv_sems) = refs[n:2 * n], refs[2 * n:]
        x, y, c = _place()

        def copy(i, l, core):
            part = out_refs[i].at[l, _half_rows(out_refs[i].shape[1], core)]
            return pltpu.make_async_remote_copy(src_ref=part, dst_ref=part, send_sem=send_sems.at[layers * i + l],
                                                recv_sem=recv_sems.at[layers * i + l], device_id=(x, y, 1 - c), device_id_type=MESH)

        sends = [copy(i, l, c) for i in range(n) for l in range(layers)]
        for cp in sends:
            cp.start()
        for i in range(n):
            for l in range(layers):
                copy(i, l, 1 - c).wait_recv()
        for cp in sends:
            cp.wait_send()

    return pl.pallas_call(
        body,
        in_specs=[ANY] * n,
        out_specs=[ANY] * n,
        out_shape=[jax.ShapeDtypeStruct(a.shape, a.dtype) for a in fs],
        scratch_shapes=[pltpu.SemaphoreType.DMA((layers * n,)), pltpu.SemaphoreType.DMA((layers * n,))],
        input_output_aliases={i: i for i in range(n)},
        name="sibling_gather",
    )(*fs)


class GatherAll:
    def __init__(self, vs):
        self.ins = list(vs)
        self.outs = [jax.ShapeDtypeStruct((8, *a.shape), a.dtype) for a in vs]
        self.n_sems = 7 * len(vs)
        self.base = 0

    def _first(self, in_refs, out_refs, send_sems, recv_sems):
        x, y, c = _place()
        me, sibling = (x, y, c), (x, y, 1 - c)

        def copy(i, k, block, to, src=None):
            px, py, pc = block
            dst = out_refs[i].at[4 * px + 2 * py + pc]
            return pltpu.make_async_remote_copy(src_ref=dst if src is None else src, dst_ref=dst, send_sem=send_sems.at[self.base + 7 * i + k],
                                                recv_sem=recv_sems.at[self.base + 7 * i + k], device_id=to, device_id_type=MESH)

        first = []
        for i in range(len(in_refs)):
            first.append(copy(i, 0, me, sibling, src=in_refs[i]))
            first += [copy(i, 1 + j, me, (*chip, c), src=in_refs[i]) for j, chip in enumerate(_other_chips(x, y))]
        return first, copy

    def start(self, in_refs, out_refs, send_sems, recv_sems):
        first, _ = self._first(in_refs, out_refs, send_sems, recv_sems)
        for cp in first:
            cp.start()

    def finish(self, in_refs, out_refs, send_sems, recv_sems):
        first, copy = self._first(in_refs, out_refs, send_sems, recv_sems)
        x, y, c = _place()
        me, sibling = (x, y, c), (x, y, 1 - c)
        chips = _other_chips(x, y)
        passed = []
        for i in range(len(in_refs)):
            for j, chip in enumerate(chips):
                copy(i, 1 + j, (*chip, c), me).wait_recv()
                fwd = copy(i, 4 + j, (*chip, c), sibling)
                fwd.start()
                passed.append(fwd)
        for i in range(len(in_refs)):
            copy(i, 0, sibling, me).wait_recv()
            for j, chip in enumerate(chips):
                copy(i, 4 + j, (*chip, 1 - c), me).wait_recv()
        for cp in first + passed:
            cp.wait_send()


def _row_tile(rows, row_bytes):
    best = None
    for t in range(16, rows + 1, 16):
        if rows % t == 0 and t * row_bytes <= SUM_TILE_BYTES:
            best = t
    return best or rows


def sibling_sum(g, theirs, place, name):
    _, half, cols = theirs.shape
    tile = _row_tile(half, cols * 4)
    nt = half // tile

    def body(place_ref, g_ref, t_ref, o_ref):
        o_ref[...] = (g_ref[...].astype(F32) + t_ref[...].astype(F32)).astype(o_ref.dtype)

    spec = pl.BlockSpec((1, tile, cols), lambda j, i, place_ref: (j, i, 0))
    return pl.pallas_call(
        body,
        grid_spec=pltpu.PrefetchScalarGridSpec(
            num_scalar_prefetch=1, grid=(N_CHIPS, nt),
            in_specs=[pl.BlockSpec((1, tile, cols), lambda j, i, place_ref: (j, place_ref[1] * nt + i, 0)), spec], out_specs=spec),
        out_shape=jax.ShapeDtypeStruct(theirs.shape, BF16),
        compiler_params=_cparams(("parallel", "parallel")),
        name=name,
    )(place, g, theirs)


def chip_sum(p, others, place, layer, into, name):
    _, half, cols = p.shape
    tile = _row_tile(half, cols * 4)
    nt = half // tile

    def body(place_ref, p_ref, o3_ref, *rest):
        o_ref = rest[-1]
        acc = p_ref[0].astype(F32)
        for k in range(3):
            acc = acc + o3_ref[k].astype(F32)
        o_ref[0] = acc

    return pl.pallas_call(
        body,
        grid_spec=pltpu.PrefetchScalarGridSpec(
            num_scalar_prefetch=1, grid=(nt,),
            in_specs=[pl.BlockSpec((1, tile, cols), lambda i, place_ref: (place_ref[0], i, 0)),
                      pl.BlockSpec((3, tile, cols), lambda i, place_ref: (0, i, 0))] + ([] if into is None else [ANY]),
            out_specs=pl.BlockSpec((1, tile, cols), lambda i, place_ref: (layer, place_ref[1] * nt + i, 0))),
        out_shape=jax.ShapeDtypeStruct((N_LAYERS, 2 * half, cols), F32),
        input_output_aliases={} if into is None else {3: 0},
        compiler_params=_cparams(("parallel",)),
        name=name,
    )(place, p, others, *([] if into is None else [into]))


def sum_devices(a, name):
    n, rows, cols = a.shape
    tile = _row_tile(rows, cols * 4 * n)

    def body(a_ref, o_ref):
        acc = a_ref[0]
        for s in range(1, n):
            acc = acc + a_ref[s]
        o_ref[...] = acc

    return pl.pallas_call(body, grid=(rows // tile,), in_specs=[pl.BlockSpec((n, tile, cols), lambda i: (0, i, 0))],
                          out_specs=pl.BlockSpec((tile, cols), lambda i: (i, 0)), out_shape=jax.ShapeDtypeStruct((rows, cols), F32),
                          compiler_params=_cparams(("parallel",)), name=name)(a)


def adamw(w, g, m, v, name):
    layers, rows, cols = w.shape
    tile = rows
    for t in range(8, rows, 8):
        if rows % t == 0 and t * cols * 4 <= ADAM_TILE_BYTES:
            tile = t

    def body(w_ref, g_ref, m_ref, v_ref, d_ref, mo_ref, vo_ref):
        gg = g_ref[...]
        m_new = ADAM_B1 * m_ref[...] + (1.0 - ADAM_B1) * gg
        v_new = ADAM_B2 * v_ref[...] + (1.0 - ADAM_B2) * (gg * gg)
        m_hat = m_new / (1.0 - ADAM_B1 ** ADAM_STEP)
        v_hat = v_new / (1.0 - ADAM_B2 ** ADAM_STEP)
        d_ref[...] = -ADAM_LR * (m_hat / (jnp.sqrt(v_hat) + ADAM_EPS) + ADAM_WD * w_ref[...])
        mo_ref[...] = m_new
        vo_ref[...] = v_new

    spec = pl.BlockSpec((1, tile, cols), lambda l, i: (l, i, 0))
    shape = jax.ShapeDtypeStruct((layers, rows, cols), F32)
    return pl.pallas_call(body, grid=(layers, rows // tile), in_specs=[spec] * 4, out_specs=[spec] * 3, out_shape=[shape] * 3,
                          compiler_params=_cparams(("parallel", "parallel")), name=name)(w, g, m, v)


WEIGHTS = ["mix_norm_pre", "w_in", "q_norm", "w_uq", "kv_norm", "w_uk", "w_uv", "w_attn_o", "conv_w", "conv_b", "conv_ln_g", "conv_ln_b",
           "w_conv_o", "pool_w", "pool_scale", "w_pool_o", "w_mix_o", "mix_norm_post", "ffn_norm_pre", "w_gate", "w_up", "w_down",
           "ffn_norm_post"]
SHARDED = {"w_in": 2, "w_uq": 2, "w_uk": 2, "w_uv": 2, "w_attn_o": 2, "conv_w": 2, "w_conv_o": 2, "w_pool_o": 2, "w_mix_o": 1,
           "w_gate": 2, "w_up": 2, "w_down": 1}
REPLICATED = [n for n in WEIGHTS if n not in SHARDED]
ROW_PARAMS = [n for n in REPLICATED if n != "pool_w"]
ROWS_MINOR = ("w_in", "w_uq", "w_gate", "w_up", "conv_w")
N_CHIPS = 4
N_MIX_GROUPS = 6
MIX_MATRICES = ("w_in", "w_uq", "w_ukv", "w_attn_o", "w_conv_o", "w_pool_o", "conv_w", "w_mix_o")
CONV_WIRE_ROWS = 32
GROUPS = [(("w_in",), 1), (("w_uq",), 1), (("w_uk", "w_uv"), 1), (("w_attn_o", "w_conv_o", "w_pool_o"), 1), (("conv_w",), 1),
          (("w_mix_o",), 1), (("w_gate", "w_up"), 2), (("w_down",), 1)]


def _join(parts, axis):
    return parts[0] if len(parts) == 1 else jnp.concatenate(parts, axis=axis)


def _split_group(arr, names, axis, shapes):
    out, off = {}, 0
    ax = arr.ndim - 3 + axis
    for n in names:
        size = shapes[n][axis]
        out[n] = lax.slice_in_dim(arr, off, off + size, axis=ax)
        off += size
    return out


def _pack_rows(vectors):
    blocks = []
    for v in vectors:
        for li in range(v.shape[0]):
            blocks.append(jnp.pad(v[li][None, :], ((0, PACK_ROWS - 1), (0, PACK_W - v.shape[1]))))
    return jnp.concatenate(blocks, axis=0)


def _unpack_rows(packed, shapes):
    out, r = [], 0
    for layers, width in shapes:
        out.append(jnp.stack([packed[r + PACK_ROWS * li, :width] for li in range(layers)]))
        r += PACK_ROWS * layers
    return out


def kernel(x, positions, mix_norm_pre, w_in, q_norm, w_uq, kv_norm, w_uk, w_uv, w_attn_o, conv_w, conv_b, conv_ln_g, conv_ln_b, w_conv_o, pool_w, pool_scale, w_pool_o, w_mix_o, mix_norm_post, ffn_norm_pre, w_gate, w_up, w_down, ffn_norm_post, loss_target, m_mix_norm_pre, m_w_in, m_q_norm, m_w_uq, m_kv_norm, m_w_uk, m_w_uv, m_w_attn_o, m_conv_w, m_conv_b, m_conv_ln_g, m_conv_ln_b, m_w_conv_o, m_pool_w, m_pool_scale, m_w_pool_o, m_w_mix_o, m_mix_norm_post, m_ffn_norm_pre, m_w_gate, m_w_up, m_w_down, m_ffn_norm_post, v_mix_norm_pre, v_w_in, v_q_norm, v_w_uq, v_kv_norm, v_w_uk, v_w_uv, v_w_attn_o, v_conv_w, v_conv_b, v_conv_ln_g, v_conv_ln_b, v_w_conv_o, v_pool_w, v_pool_scale, v_w_pool_o, v_w_mix_o, v_mix_norm_post, v_ffn_norm_pre, v_w_gate, v_w_up, v_w_down, v_ffn_norm_post):
    given = dict(mix_norm_pre=mix_norm_pre, w_in=w_in, q_norm=q_norm, w_uq=w_uq, kv_norm=kv_norm, w_uk=w_uk, w_uv=w_uv, w_attn_o=w_attn_o,
                 conv_w=conv_w, conv_b=conv_b, conv_ln_g=conv_ln_g, conv_ln_b=conv_ln_b, w_conv_o=w_conv_o, pool_w=pool_w,
                 pool_scale=pool_scale, w_pool_o=w_pool_o, w_mix_o=w_mix_o, mix_norm_post=mix_norm_post, ffn_norm_pre=ffn_norm_pre,
                 w_gate=w_gate, w_up=w_up, w_down=w_down, ffn_norm_post=ffn_norm_post)
    mom = dict(mix_norm_pre=m_mix_norm_pre, w_in=m_w_in, q_norm=m_q_norm, w_uq=m_w_uq, kv_norm=m_kv_norm, w_uk=m_w_uk, w_uv=m_w_uv,
               w_attn_o=m_w_attn_o, conv_w=m_conv_w, conv_b=m_conv_b, conv_ln_g=m_conv_ln_g, conv_ln_b=m_conv_ln_b, w_conv_o=m_w_conv_o,
               pool_w=m_pool_w, pool_scale=m_pool_scale, w_pool_o=m_w_pool_o, w_mix_o=m_w_mix_o, mix_norm_post=m_mix_norm_post,
               ffn_norm_pre=m_ffn_norm_pre, w_gate=m_w_gate, w_up=m_w_up, w_down=m_w_down, ffn_norm_post=m_ffn_norm_post)
    var = dict(mix_norm_pre=v_mix_norm_pre, w_in=v_w_in, q_norm=v_q_norm, w_uq=v_w_uq, kv_norm=v_kv_norm, w_uk=v_w_uk, w_uv=v_w_uv,
               w_attn_o=v_w_attn_o, conv_w=v_conv_w, conv_b=v_conv_b, conv_ln_g=v_conv_ln_g, conv_ln_b=v_conv_ln_b, w_conv_o=v_w_conv_o,
               pool_w=v_pool_w, pool_scale=v_pool_scale, w_pool_o=v_w_pool_o, w_mix_o=v_w_mix_o, mix_norm_post=v_mix_norm_post,
               ffn_norm_pre=v_ffn_norm_pre, w_gate=v_w_gate, w_up=v_w_up, w_down=v_w_down, ffn_norm_post=v_ffn_norm_post)
    s_len = x.shape[1]
    sharded_names = [n for n in WEIGHTS if n in SHARDED]
    chip = 2 * lax.axis_index("x") + lax.axis_index("y")
    place = jnp.stack([chip, lax.axis_index("c")]).astype(jnp.int32)
    shard_shape = {n: given[n].shape for n in sharded_names}

    mix_groups, ffn_groups = GROUPS[:N_MIX_GROUPS], GROUPS[N_MIX_GROUPS:]
    weight_wire_shape = {n: (N_LAYERS, 2 * CONV_WIRE_ROWS, shard_shape[n][2]) if n == "conv_w" else shard_shape[n] for n in sharded_names}
    grad_wire_shape = {n: (N_LAYERS, CONV_WIRE_ROWS, shard_shape[n][2]) if n == "conv_w" else shard_shape[n] for n in sharded_names}
    pad_rows = lambda a: jnp.pad(a, ((0, CONV_WIRE_ROWS - CONV_W), (0, 0)))

    def weight_wires(groups, li):
        def wire(name):
            a = given[name][li]
            if name == "conv_w":
                hi = a.astype(BF16)
                return jnp.concatenate([pad_rows(hi), pad_rows((a - hi.astype(F32)).astype(BF16))], axis=0)
            return a.astype(BF16)

        return [_join([wire(n) for n in names], axis - 1) for names, axis in groups]

    def full_weights(groups, li, local, gathered):
        p = {n: given[n][li] for n in REPLICATED}
        p["pool_w"] = p["pool_w"].astype(BF16)
        for (names, axis), loc, got in zip(groups, local, gathered):
            got = lax.dynamic_update_slice(got, loc[None], (chip, 0, 0))
            per_chip = [_split_group(got[j], names, axis, weight_wire_shape) for j in range(N_CHIPS)]
            for n in names:
                parts = [pc[n] for pc in per_chip]
                if n == "conv_w":
                    parts = [q[:CONV_W].astype(F32) + q[CONV_WIRE_ROWS:CONV_WIRE_ROWS + CONV_W].astype(F32) for q in parts]
                p[n] = jnp.concatenate(parts, axis=SHARDED[n] - 1)
        return align_weights(p)

    def chip_partials(groups, g, tag):
        wires = []
        for names, axis in groups:
            split = {n: jnp.split(pad_rows(g[n]) if n == "conv_w" else g[n], N_CHIPS, axis=SHARDED[n] - 1) for n in names}
            wires.append(jnp.stack([_join([split[n][j].astype(BF16) for n in names], axis - 1) for j in range(N_CHIPS)]))
        theirs = sibling_swap(wires, "sibling_swap_" + tag)
        return [sibling_sum(w, t, place, f"sibling_sum_{tag}_{i}") for i, (w, t) in enumerate(zip(wires, theirs))]

    tabs = rope_tables(positions.reshape(s_len, 1), s_len)
    in_groups, rest_groups = mix_groups[:1], mix_groups[1:]
    heads = []
    for li in range(N_LAYERS):
        loc_rest = weight_wires(rest_groups, li)
        heads.append((GatherShards(loc_rest), functools.partial(full_weights, rest_groups, li, loc_rest)))
    loc_in0 = weight_wires(in_groups, 0)
    w_in0 = full_weights(in_groups, 0, loc_in0, run_exchange(GatherShards(loc_in0), "gather_w_in_l0"))
    loc_f0, loc_in1 = weight_wires(ffn_groups, 0), weight_wires(in_groups, 1)
    h, sv_m0, got = mixer_fwd(x[0], tabs, w_in0, "_l0", GatherShards(loc_f0 + loc_in1), heads[0])
    w_m0 = sv_m0["w"]
    w_f0 = full_weights(ffn_groups, 0, loc_f0, got[:len(loc_f0)])
    w_in1 = full_weights(in_groups, 1, loc_in1, got[len(loc_f0):])
    h, sv_f0, h_normed = ffn_fwd(h, sv_m0["hn"], w_f0, "_l0", w_in1["mix_norm_pre"])
    loc_f1 = weight_wires(ffn_groups, 1)
    h, sv_m1, got = mixer_fwd(h, tabs, w_in1, "_l1", GatherShards(loc_f1), heads[1], h=h_normed)
    w_m1 = sv_m1["w"]
    w_f1 = full_weights(ffn_groups, 1, loc_f1, got)
    h, sv_f1, _ = ffn_fwd(h, sv_m1["hn"], w_f1, "_l1")
    dh, loss_local = loss_head(h, loss_target[0])
    loss = lax.psum(loss_local[0, 0], MESH_AXES)

    dh, g_f1, post_m1 = ffn_bwd(dh, sv_f1, w_f1, "_l1", sv_m1["mo"])
    g_f1 = unalign_grads(g_f1)
    p_f1 = chip_partials(ffn_groups, g_f1, "ffn_l1")
    dh, g_m1, o_f1, post_f0 = mixer_bwd(dh, sv_m1, tabs, w_m1, "_l1", post_m1, ChipExchange(p_f1), prev=(sv_f0["y"], w_f0["ffn_norm_post"]))
    g_m1 = unalign_grads(g_m1)
    p_m1 = chip_partials(mix_groups, g_m1, "mix_l1")
    dh, g_f0, post_m0 = ffn_bwd(dh, sv_f0, w_f0, "_l0", sv_m0["mo"], post_f0)
    g_f0 = unalign_grads(g_f0)
    p_f0 = chip_partials(ffn_groups, g_f0, "ffn_l0")
    last = {}

    def tail_rest(g):
        last["rest"] = chip_partials(rest_groups, unalign_grads({n: g[n] for n in MIX_MATRICES[1:]}), "rest_l0")
        return ChipExchange(last["rest"])

    def tail_in(g):
        last["in"] = chip_partials(in_groups, unalign_grads({"w_in": g["w_in"]}), "w_in_l0")
        return ChipExchange(last["in"])

    grad_x, g_m0, (got, o_m0), _ = mixer_bwd(dh, sv_m0, tabs, w_m0, "_l0", post_m0, ChipExchange(p_m1 + p_f0), (tail_rest, tail_in))
    o_m1, o_f0 = got[:len(p_m1)], got[len(p_m1):]
    p_m0 = last["in"] + last["rest"]
    grads = [{**g_m0, **g_f0}, {**g_m1, **g_f1}]
    grad_full = {n: jnp.stack([g[n].reshape(given[n].shape[1:]) for g in grads]) for n in REPLICATED}
    replicated = [_pack_rows([grad_full[n] for n in ROW_PARAMS]), grad_full["pool_w"].reshape(-1, POOL_GD)]
    device = 2 * chip + lax.axis_index("c")
    rows_all, pool_w_all = [lax.dynamic_update_slice(a, mine[None], (device, 0, 0))
                            for a, mine in zip(run_exchange(GatherAll(replicated), "gather_replicated"), replicated)]

    sums = {}
    for groups, base, per_layer in ((ffn_groups, N_MIX_GROUPS, ((1, p_f1, o_f1), (0, p_f0, o_f0))), (mix_groups, 0, ((1, p_m1, o_m1), (0, p_m0, o_m0)))):
        for li, parts, others in per_layer:
            for i, (p, o) in enumerate(zip(parts, others)):
                sums[base + i] = chip_sum(p, o, place, li, sums.get(base + i), f"chip_sum_{base + i}_l{li}")
    g_shard = {}
    for (names, axis), s in zip(GROUPS, sibling_gather([sums[i] for i in range(len(GROUPS))])):
        g_shard.update(_split_group(s, names, axis, grad_wire_shape))
    g_shard["conv_w"] = g_shard["conv_w"][:, :CONV_W]

    row_shapes = [given[n].shape for n in ROW_PARAMS]
    g_rows = sum_devices(rows_all, "row_params_sum")
    g_pool_w = sum_devices(pool_w_all, "pool_w_sum")
    g_rep = dict(zip(ROW_PARAMS, _unpack_rows(g_rows, row_shapes)))
    g_rep["pool_w"] = g_pool_w.reshape(given["pool_w"].shape)

    g_out, d_out, m_out, v_out = {}, {}, {}, {}
    for n in sharded_names + ["pool_w"]:
        shp = given[n].shape
        three_d = (shp[0], int(np.prod(shp[1:-1])), shp[-1])
        g_n = g_shard[n] if n in SHARDED else g_rep[n]
        view = (lambda a: jnp.swapaxes(a.reshape(three_d), 1, 2)) if n in ROWS_MINOR else (lambda a: a.reshape(three_d))
        back = (lambda a: jnp.swapaxes(a, 1, 2).reshape(shp)) if n in ROWS_MINOR else (lambda a: a.reshape(shp))
        d, mn, vn = adamw(view(given[n]), view(g_n), view(mom[n]), view(var[n]), "adamw_" + n)
        g_out[n], d_out[n], m_out[n], v_out[n] = g_n, back(d), back(mn), back(vn)
    rd, rm, rv = adamw(_pack_rows([given[n] for n in ROW_PARAMS])[None], g_rows[None], _pack_rows([mom[n] for n in ROW_PARAMS])[None],
                       _pack_rows([var[n] for n in ROW_PARAMS])[None], "adamw_row_params")
    for n, d, mn, vn in zip(ROW_PARAMS, *[_unpack_rows(a[0], row_shapes) for a in (rd, rm, rv)]):
        g_out[n], d_out[n], m_out[n], v_out[n] = g_rep[n], d, mn, vn

    return (loss, grad_x[None], *[g_out[n] for n in WEIGHTS], *[d_out[n] for n in WEIGHTS], *[m_out[n] for n in WEIGHTS],
            *[v_out[n] for n in WEIGHTS])
```

```python
import functools
import math

import numpy as np
import jax
import jax.numpy as jnp
from jax import lax
from jax.experimental import pallas as pl
from jax.experimental.pallas import tpu as pltpu

F32, BF16 = jnp.float32, jnp.bfloat16

D_MODEL = 1024
N_HEADS = 8
NOPE, ROPE, VDIM = 64, 32, 64
HALF_ROPE = ROPE // 2
Q_RANK, KV_RANK = 384, 256
CONV_C, CONV_W = 512, 31
POOL_C, POOL_G, POOL_GD = 512, 4, 128
POOL_WINDOWS = (2, 4, 8, 16)
D_FF = 2816
FF_HALF = D_FF // 2
N_LAYERS = 2
EPS = 1e-6
ROPE_THETA = 10000.0
ATT_SCALE = 1.0 / math.sqrt(NOPE + ROPE)
O_Q, O_KV, O_KR, O_CONV, O_POOL, O_GATE, D_IN = 0, 384, 640, 672, 1696, 2208, 5280

LANE = 128
HP = 128
ZG, ZC, ZP, ZA, ZW = 0, 3072, 4096, 4608, 5376
ZA_W = Q_RANK + KV_RANK + HP
KR_LANE = NOPE
HW = N_HEADS * HP

ADAM_LR, ADAM_B1, ADAM_B2, ADAM_EPS, ADAM_WD, ADAM_STEP = 0.001, 0.9, 0.999, 1e-08, 0.01, 10

ROW_TILE = 512
WIDE_ROW_TILE = 256
ATT_TILE_FWD = 1024
ATT_TILE_BWD = 512
ATT_HEADS = 4
CONV_CHUNK = 256
MM_TM, MM_TN, MM_TK = 1024, 1408, 1024
MM_TILE_MAX = 2048
FFN_TM = 512
MM_VMEM_BUDGET = 40 * 1024 * 1024
HBM_BYTES_PER_US = 3.0e6
GRID_STEP_US = 0.35
VMEM_LIMIT = 56 * 1024 * 1024
SUM_TILE_BYTES = 3 * 1024 * 1024
ADAM_TILE_BYTES = 2 * 1024 * 1024

HALF_ALIGN = 16
MESH_AXES = ("x", "y", "c")
PACK_W = 1024
PACK_ROWS = 8


def _cparams(sem):
    return pltpu.CompilerParams(dimension_semantics=sem, vmem_limit_bytes=VMEM_LIMIT)


def _tile(n, target):
    if n <= target:
        return n
    best = None
    for t in range(LANE, target + 1, LANE):
        if n % t == 0:
            best = t
    assert best is not None, (n, target)
    return best


def _mm_tiles(m, n, k, a_bytes, b_bytes, out_bytes):
    divs = lambda d: sorted({t for t in range(LANE, min(d, MM_TILE_MAX) + 1, LANE) if d % t == 0} | ({d} if d <= MM_TILE_MAX else set()))
    best = None
    for tm in divs(m):
        for tn in divs(n):
            blocks = tm * k * a_bytes + k * tn * b_bytes + tm * tn * out_bytes
            if 2 * blocks + tm * tn * 4 > MM_VMEM_BUDGET:
                continue
            steps = (m // tm) * (n // tn)
            for rows_outer in (True, False):
                moved = (m * k * a_bytes + k * n * b_bytes * (m // tm)) if rows_outer else (k * n * b_bytes + m * k * a_bytes * (n // tn))
                cost = (moved + m * n * out_bytes + blocks) / HBM_BYTES_PER_US + steps * GRID_STEP_US
                if best is None or cost < best[0]:
                    best = (cost, tm, tn, rows_outer)
    if best is not None:
        return best[1], best[2], k, best[3]
    return _tile(m, MM_TM), _tile(n, MM_TN), _tile(k, MM_TK), True


def mm(a, b, *, ta=False, tb=False, out_dtype=F32, name, carry=None):
    m, k = (a.shape[1], a.shape[0]) if ta else a.shape
    n, k2 = b.shape if tb else (b.shape[1], b.shape[0])
    assert k == k2, (a.shape, b.shape, ta, tb)
    tm, tn, tk, rows_outer = _mm_tiles(m, n, k, a.dtype.itemsize, b.dtype.itemsize, jnp.dtype(out_dtype).itemsize)
    nk = k // tk
    dims = (((0 if ta else 1,), (1 if tb else 0,)), ((), ()))
    grid = (m // tm, n // tn, nk) if rows_outer else (n // tn, m // tm, nk)

    def body(*refs):
        (a_ref, b_ref, o_ref, *acc), start, finish = _carried(carry, refs, 2, 1, 0 if nk == 1 else 1)
        ids = [pl.program_id(d) for d in range(3)]
        if start is not None:
            pl.when((ids[0] == 0) & (ids[1] == 0) & (ids[2] == 0))(start)
        part = lax.dot_general(a_ref[...].astype(BF16), b_ref[...].astype(BF16), dims, preferred_element_type=F32)
        if nk == 1:
            o_ref[...] = part.astype(o_ref.dtype)
        else:
            (acc_ref,) = acc
            kk = ids[2]

            @pl.when(kk == 0)
            def _():
                acc_ref[...] = part

            @pl.when(kk > 0)
            def _():
                acc_ref[...] += part

            @pl.when(kk == nk - 1)
            def _():
                o_ref[...] = acc_ref[...].astype(o_ref.dtype)
        if finish is not None:
            pl.when((ids[0] == grid[0] - 1) & (ids[1] == grid[1] - 1) & (ids[2] == nk - 1))(finish)

    ij = (lambda g0, g1: (g0, g1)) if rows_outer else (lambda g0, g1: (g1, g0))

    def a_map(g0, g1, kk):
        i, _ = ij(g0, g1)
        return (kk, i) if ta else (i, kk)

    def b_map(g0, g1, kk):
        _, j = ij(g0, g1)
        return (j, kk) if tb else (kk, j)

    ex_in_specs, ex_out_specs, ex_out_shape, ex_scratch, ex_inputs = _carry_specs(carry)
    a_spec = pl.BlockSpec((tk, tm) if ta else (tm, tk), a_map)
    b_spec = pl.BlockSpec((tn, tk) if tb else (tk, tn), b_map)
    out, *carried = pl.pallas_call(
        body,
        grid=grid,
        in_specs=[a_spec, b_spec] + ex_in_specs,
        out_specs=[pl.BlockSpec((tm, tn), lambda g0, g1, kk: ij(g0, g1))] + ex_out_specs,
        out_shape=[jax.ShapeDtypeStruct((m, n), out_dtype)] + ex_out_shape,
        scratch_shapes=([] if nk == 1 else [pltpu.VMEM((tm, tn), F32)]) + ex_scratch,
        compiler_params=_cparams(("arbitrary", "arbitrary", "arbitrary") if carry is not None else ("parallel", "parallel", "arbitrary")),
        name=name,
    )(a, b, *ex_inputs)
    return out if carry is None else (out, carried)


def ffn_in(hn, w_gu, name):
    s_len, k = hn.shape
    tm = min(FFN_TM, s_len)

    def body(a_ref, b_ref, gu_ref, act_ref):
        r = jnp.dot(a_ref[...], b_ref[...], preferred_element_type=F32)
        gu_ref[...] = r.astype(gu_ref.dtype)
        act_ref[...] = (_silu(r[:, :FF_HALF]) * r[:, FF_HALF:]).astype(act_ref.dtype)

    return pl.pallas_call(
        body,
        grid=(2, s_len // tm),
        in_specs=[pl.BlockSpec((tm, k), lambda j, i: (i, 0)), pl.BlockSpec((k, 2 * FF_HALF), lambda j, i: (0, j))],
        out_specs=[pl.BlockSpec((tm, 2 * FF_HALF), lambda j, i: (i, j)), pl.BlockSpec((tm, FF_HALF), lambda j, i: (i, j))],
        out_shape=[jax.ShapeDtypeStruct((s_len, 2 * D_FF), BF16), jax.ShapeDtypeStruct((s_len, D_FF), BF16)],
        compiler_params=_cparams(("arbitrary", "parallel")),
        name=name,
    )(hn, w_gu)


def ffn_out_dx(d_y, w_down, gu, name):
    s_len, k = d_y.shape
    tm = min(FFN_TM, s_len)

    def body(a_ref, b_ref, gu_ref, dgu_ref):
        da = lax.dot_general(a_ref[...], b_ref[...], NT_DIMS, preferred_element_type=F32)
        gt = gu_ref[:, :FF_HALF].astype(F32)
        up = gu_ref[:, FF_HALF:].astype(F32)
        sg = jax.nn.sigmoid(gt)
        dgu_ref[:, :FF_HALF] = (da * up * sg * (1.0 + gt * (1.0 - sg))).astype(dgu_ref.dtype)
        dgu_ref[:, FF_HALF:] = (da * gt * sg).astype(dgu_ref.dtype)

    pair = pl.BlockSpec((tm, 2 * FF_HALF), lambda j, i: (i, j))
    return pl.pallas_call(
        body,
        grid=(2, s_len // tm),
        in_specs=[pl.BlockSpec((tm, k), lambda j, i: (i, 0)), pl.BlockSpec((FF_HALF, k), lambda j, i: (j, 0)), pair],
        out_specs=pair,
        out_shape=jax.ShapeDtypeStruct((s_len, 2 * D_FF), BF16),
        compiler_params=_cparams(("parallel", "arbitrary")),
        name=name,
    )(d_y, w_down, gu)


def rowwise(name, body, rows, row_ins, full_ins, row_outs, acc_outs=(), into=None, tile=None):
    tile = min(tile or ROW_TILE, rows)
    into = into or {}
    in_specs = [pl.BlockSpec((tile, w), lambda i, cb=cb: (i, cb)) for _, w, cb in row_ins]
    in_specs += [pl.BlockSpec(a.shape, lambda i, nd=a.ndim: (0,) * nd) for a in full_ins]
    in_specs += [ANY for _ in into]
    n_in = len(row_ins) + len(full_ins)
    aliases = {n_in + k: oi for k, oi in enumerate(into)}
    out_specs, out_shape = [], []
    for ro in row_outs:
        w, dt, full_w, cb = ro if len(ro) == 4 else (*ro, ro[0], 0)
        out_specs.append(pl.BlockSpec((tile, w), lambda i, cb=cb: (i, cb)))
        out_shape.append(jax.ShapeDtypeStruct((rows, full_w), dt))
    out_specs += [pl.BlockSpec(s, lambda i, nd=len(s): (0,) * nd) for s, _ in acc_outs]
    out_shape += [jax.ShapeDtypeStruct(s, dt) for s, dt in acc_outs]
    n_refs = n_in

    def call_body(*refs):
        body(*refs[:n_refs], *refs[n_refs + len(into):])

    outs = pl.pallas_call(
        call_body,
        grid=(rows // tile,),
        in_specs=in_specs,
        out_specs=out_specs,
        out_shape=out_shape,
        input_output_aliases=aliases,
        compiler_params=_cparams(("arbitrary",)),
        name=name,
    )(*[a for a, _, _ in row_ins], *full_ins, *into.values())
    return outs


def _whole(a):
    return (a, a.shape[1], 0)


def _acc(ref, val):
    @pl.when(pl.program_id(0) == 0)
    def _():
        ref[...] = val

    @pl.when(pl.program_id(0) > 0)
    def _():
        ref[...] += val


def _rms(x, g):
    return x * lax.rsqrt(jnp.mean(x * x, axis=-1, keepdims=True) + EPS) * g


def _layer_norm(x, g, b):
    mu = jnp.mean(x, axis=-1, keepdims=True)
    xc = x - mu
    return xc * lax.rsqrt(jnp.mean(xc * xc, axis=-1, keepdims=True) + EPS) * g + b


def _silu(x):
    return x * jax.nn.sigmoid(x)


def _rope(x, cc, sa, sb):
    return x * cc + pltpu.roll(x, HALF_ROPE, 1) * sa + pltpu.roll(x, HP - HALF_ROPE, 1) * sb


def _rope_t(dy, cc, sa, sb):
    return dy * cc + pltpu.roll(dy * sa, HP - HALF_ROPE, 1) + pltpu.roll(dy * sb, HALF_ROPE, 1)


def rope_tables(pos_col, rows):
    lane = np.arange(HP)
    idx = np.where(lane < KR_LANE + HALF_ROPE, lane - KR_LANE, lane - KR_LANE - HALF_ROPE)
    in_rope = (lane >= KR_LANE) & (lane < KR_LANE + ROPE)
    inv_freq = (np.float32(ROPE_THETA) ** (-np.arange(0, ROPE, 2, dtype=np.float32) / np.float32(ROPE))).astype(np.float32)
    freq_row = np.where(in_rope, inv_freq[np.clip(idx, 0, HALF_ROPE - 1)], 0.0).astype(np.float32)[None, :]
    first = ((lane >= KR_LANE) & (lane < KR_LANE + HALF_ROPE)).astype(np.float32)[None, :]
    second = ((lane >= KR_LANE + HALF_ROPE) & (lane < KR_LANE + ROPE)).astype(np.float32)[None, :]

    def body(pos_ref, f_ref, a_ref, b_ref, cc_ref, sa_ref, sb_ref):
        ang = pos_ref[...].astype(F32) * f_ref[...]
        s = jnp.sin(ang)
        cc_ref[...] = jnp.cos(ang)
        sa_ref[...] = s * b_ref[...]
        sb_ref[...] = -s * a_ref[...]

    return rowwise("rope_tables", body, rows, [_whole(pos_col)], [jnp.asarray(freq_row), jnp.asarray(first), jnp.asarray(second)],
                   [(HP, F32)] * 3)


def _causal_mask(t):
    r = lax.broadcasted_iota(jnp.int32, (t, t), 0)
    c = lax.broadcasted_iota(jnp.int32, (t, t), 1)
    return r, c


NT_DIMS = (((1,), (1,)), ((), ()))


def _carried(carry, refs, n_in, n_out, n_scratch):
    if carry is None:
        return refs, None, None
    ni, no = len(carry.ins), len(carry.outs)
    own_in, ex_in = refs[:n_in], refs[n_in:n_in + ni]
    own_out, ex_out = refs[n_in + ni:n_in + ni + n_out], refs[n_in + ni + n_out:n_in + ni + n_out + no]
    scratch = refs[n_in + ni + n_out + no:]
    sems = scratch[n_scratch:]
    return (*own_in, *own_out, *scratch[:n_scratch]), (lambda: carry.start(ex_in, ex_out, *sems)), (lambda: carry.finish(ex_in, ex_out, *sems))


def _carry_specs(carry):
    if carry is None:
        return [], [], [], [], []
    sems = [pltpu.SemaphoreType.DMA((carry.n_sems,)), pltpu.SemaphoreType.DMA((carry.n_sems,))]
    return [ANY] * len(carry.ins), [ANY] * len(carry.outs), list(carry.outs), sems, list(carry.ins)


def attention_fwd(q, k, v, name, carry=None):
    s_len = q.shape[0]
    t = min(ATT_TILE_FWD, s_len)
    nb = s_len // t
    hb = ATT_HEADS
    w = hb * HP
    nh = N_HEADS // hb

    def body(*refs):
        (q_ref, k_ref, v_ref, o_ref, lse_ref, m_sc, acc_sc), start, finish = _carried(carry, refs, 3, 2, 2)
        qi = pl.program_id(1)
        if start is not None:
            pl.when((pl.program_id(0) == 0) & (qi == 0))(start)
        m_sc[...] = jnp.full_like(m_sc, -jnp.inf)
        acc_sc[...] = jnp.zeros_like(acc_sc)

        def block(j, masked):
            ks = pl.ds(pl.multiple_of(j * t, t), t)
            for hh in range(hb):
                ls = slice(hh * HP, (hh + 1) * HP)
                s = lax.dot_general(q_ref[:, ls], k_ref[ks, ls], NT_DIMS, preferred_element_type=F32) * ATT_SCALE
                if masked:
                    r, c = _causal_mask(t)
                    s = jnp.where(c <= r, s, -jnp.inf)
                m_old = m_sc[hh]
                m_new = jnp.maximum(m_old, jnp.max(s, axis=-1, keepdims=True))
                p = jnp.exp(s - m_new)
                acc_sc[hh] = jnp.exp(m_old - m_new) * acc_sc[hh] + jnp.dot(p.astype(BF16), v_ref[ks, ls], preferred_element_type=F32)
                m_sc[hh] = m_new

        def loop_body(j, carry):
            block(j, False)
            return carry

        lax.fori_loop(0, qi, loop_body, 0)
        block(qi, True)
        lane = lax.broadcasted_iota(jnp.int32, (t, HP), 1)
        for hh in range(hb):
            acc = acc_sc[hh]
            l = jnp.sum(jnp.where(lane == VDIM, acc, 0.0), axis=-1, keepdims=True)
            o_ref[:, hh * HP:(hh + 1) * HP] = jnp.where(lane < VDIM, acc / l, 0.0).astype(o_ref.dtype)
            lse_ref[hh] = m_sc[hh] + jnp.log(l)
        if finish is not None:
            pl.when((pl.program_id(0) == nh - 1) & (qi == nb - 1))(finish)

    ex_in_specs, ex_out_specs, ex_out_shape, ex_scratch, ex_inputs = _carry_specs(carry)
    resident = pl.BlockSpec((s_len, w), lambda h, qi: (0, h))
    o, lse, *carried = pl.pallas_call(
        body,
        grid=(nh, nb),
        in_specs=[pl.BlockSpec((t, w), lambda h, qi: (qi, h)), resident, resident] + ex_in_specs,
        out_specs=[pl.BlockSpec((t, w), lambda h, qi: (qi, h)), pl.BlockSpec((hb, t, 1), lambda h, qi: (h, qi, 0))] + ex_out_specs,
        out_shape=[jax.ShapeDtypeStruct((s_len, HW), BF16), jax.ShapeDtypeStruct((N_HEADS, s_len, 1), F32)] + ex_out_shape,
        scratch_shapes=[pltpu.VMEM((hb, t, 1), F32), pltpu.VMEM((hb, t, HP), F32)] + ex_scratch,
        compiler_params=_cparams(("arbitrary", "arbitrary")),
        name=name,
    )(q, k, v, *ex_inputs)
    return o, lse, carried


def attention_delta(do, o):
    s_len = do.shape[0]
    t = min(ROW_TILE, s_len)

    def body(do_ref, o_ref, d_ref):
        prod = do_ref[...].astype(F32) * o_ref[...].astype(F32)
        for h in range(N_HEADS):
            d_ref[h] = jnp.sum(prod[:, h * HP:(h + 1) * HP], axis=-1, keepdims=True)

    return pl.pallas_call(
        body,
        grid=(s_len // t,),
        in_specs=[pl.BlockSpec((t, HW), lambda i: (i, 0))] * 2,
        out_specs=pl.BlockSpec((N_HEADS, t, 1), lambda i: (0, i, 0)),
        out_shape=jax.ShapeDtypeStruct((N_HEADS, s_len, 1), F32),
        compiler_params=_cparams(("arbitrary",)),
        name="attention_delta",
    )(do, o)


TN_DIMS = (((0,), (0,)), ((), ()))


def attention_bwd(q, k, v, do, lse_row, delta_row, name, carry=None):
    s_len = q.shape[0]
    t = min(ATT_TILE_BWD, s_len)
    nb = s_len // t
    hb = ATT_HEADS
    w = hb * HP
    nh = N_HEADS // hb

    def body(*refs):
        (q_ref, k_ref, v_ref, do_ref, lse_ref, dl_ref, dq_ref, dk_ref, dv_ref, dk_sc, dv_sc), start, finish = _carried(carry, refs, 6, 3, 2)
        ki = pl.program_id(1)
        if start is not None:
            pl.when((pl.program_id(0) == 0) & (ki == 0))(start)

        @pl.when(ki == 0)
        def _():
            dq_ref[...] = jnp.zeros_like(dq_ref)

        dk_sc[...] = jnp.zeros_like(dk_sc)
        dv_sc[...] = jnp.zeros_like(dv_sc)

        def block(j, masked):
            qs = pl.ds(pl.multiple_of(j * t, t), t)
            for hh in range(hb):
                ls = slice(hh * HP, (hh + 1) * HP)
                qb = q_ref[qs, ls]
                dob = do_ref[qs, ls]
                kb = k_ref[:, ls]
                st = lax.dot_general(kb, qb, NT_DIMS, preferred_element_type=F32) * ATT_SCALE
                pt = jnp.exp(st - lse_ref[hh, j])
                if masked:
                    r, c = _causal_mask(t)
                    pt = jnp.where(r <= c, pt, 0.0)
                dv_sc[hh] += jnp.dot(pt.astype(BF16), dob, preferred_element_type=F32)
                dpt = lax.dot_general(v_ref[:, ls], dob, NT_DIMS, preferred_element_type=F32)
                dst = (pt * (dpt - dl_ref[hh, j]) * ATT_SCALE).astype(BF16)
                dk_sc[hh] += jnp.dot(dst, qb, preferred_element_type=F32)
                dq_ref[qs, ls] += lax.dot_general(dst, kb, TN_DIMS, preferred_element_type=F32)

        block(ki, True)

        def loop_body(j, carry):
            block(j, False)
            return carry

        lax.fori_loop(ki + 1, nb, loop_body, 0)
        for hh in range(hb):
            ls = slice(hh * HP, (hh + 1) * HP)
            dk_ref[:, ls] = dk_sc[hh].astype(dk_ref.dtype)
            dv_ref[:, ls] = dv_sc[hh].astype(dv_ref.dtype)
        if finish is not None:
            pl.when((pl.program_id(0) == nh - 1) & (ki == nb - 1))(finish)

    ex_in_specs, ex_out_specs, ex_out_shape, ex_scratch, ex_inputs = _carry_specs(carry)
    k_spec = pl.BlockSpec((t, w), lambda h, ki: (ki, h))
    resident = pl.BlockSpec((s_len, w), lambda h, ki: (0, h))
    row_spec = pl.BlockSpec((hb, nb, 1, t), lambda h, ki: (h, 0, 0, 0))
    dq, dk, dv, *carried = pl.pallas_call(
        body,
        grid=(nh, nb),
        in_specs=[resident, k_spec, k_spec, resident, row_spec, row_spec] + ex_in_specs,
        out_specs=[resident, k_spec, k_spec] + ex_out_specs,
        out_shape=[jax.ShapeDtypeStruct((s_len, HW), F32), jax.ShapeDtypeStruct((s_len, HW), BF16), jax.ShapeDtypeStruct((s_len, HW), BF16)]
        + ex_out_shape,
        scratch_shapes=[pltpu.VMEM((hb, t, HP), F32), pltpu.VMEM((hb, t, HP), F32)] + ex_scratch,
        compiler_params=_cparams(("arbitrary", "arbitrary")),
        name=name,
    )(q, k, v, do, lse_row, delta_row, *ex_inputs)
    return dq, dk, dv, carried


CONV_PAD = 32


def conv_fwd(z, conv_w, conv_b):
    s_len = z.shape[0]
    ch = min(CONV_CHUNK, s_len)

    def body(ag_ref, w_ref, b_ref, c_ref, pad_ref):
        pad_ref[0:CONV_PAD, :] = jnp.zeros((CONV_PAD, LANE), F32)
        pad_ref[CONV_PAD:CONV_PAD + s_len, :] = ag_ref[:, 0:LANE].astype(F32) * jax.nn.sigmoid(ag_ref[:, LANE:2 * LANE].astype(F32))

        def chunk(i, carry):
            base = pl.multiple_of(i * ch, ch)
            acc = jnp.zeros((ch, LANE), F32) + b_ref[...]
            for kk in range(CONV_W):
                acc = acc + pad_ref[pl.ds(base + CONV_PAD - (CONV_W - 1) + kk, ch), :] * w_ref[kk:kk + 1, :]
            c_ref[pl.ds(base, ch), :] = acc
            return carry

        lax.fori_loop(0, s_len // ch, chunk, 0)

    nblk = CONV_C // LANE
    return pl.pallas_call(
        body,
        grid=(nblk,),
        in_specs=[pl.BlockSpec((s_len, 2 * LANE), lambda j: (0, ZC // (2 * LANE) + j)),
                  pl.BlockSpec((CONV_W, LANE), lambda j: (0, j)), pl.BlockSpec((1, LANE), lambda j: (0, j))],
        out_specs=pl.BlockSpec((s_len, LANE), lambda j: (0, j)),
        out_shape=jax.ShapeDtypeStruct((s_len, CONV_C), F32),
        scratch_shapes=[pltpu.VMEM((s_len + CONV_PAD, LANE), F32)],
        compiler_params=_cparams(("arbitrary",)),
        name="conv_fwd",
    )(z, conv_w, conv_b)


def conv_bwd(z, dc, conv_w, dz):
    s_len = z.shape[0]
    ch = min(CONV_CHUNK, s_len)

    def body(ag_ref, dc_ref, w_ref, dz_in, dag_ref, dw_ref, db_ref, pad_ref, dpad_ref, wacc_ref):
        del dz_in
        pad_ref[0:CONV_PAD, :] = jnp.zeros((CONV_PAD, LANE), F32)
        pad_ref[CONV_PAD:CONV_PAD + s_len, :] = ag_ref[:, 0:LANE].astype(F32) * jax.nn.sigmoid(ag_ref[:, LANE:2 * LANE].astype(F32))
        dpad_ref[0:s_len, :] = dc_ref[...]
        dpad_ref[s_len:s_len + CONV_PAD, :] = jnp.zeros((CONV_PAD, LANE), F32)
        wacc_ref[...] = jnp.zeros_like(wacc_ref)
        db_ref[...] = jnp.sum(dc_ref[...], axis=0, keepdims=True)

        def chunk(i, carry):
            base = pl.multiple_of(i * ch, ch)
            dcc = dpad_ref[pl.ds(base, ch), :]
            dh = jnp.zeros((ch, LANE), F32)
            for kk in range(CONV_W):
                dh = dh + dpad_ref[pl.ds(base + (CONV_W - 1) - kk, ch), :] * w_ref[kk:kk + 1, :]
                prod = dcc * pad_ref[pl.ds(base + CONV_PAD - (CONV_W - 1) + kk, ch), :]
                wacc_ref[kk * 8:(kk + 1) * 8, :] += prod.reshape(ch // 8, 8, LANE).sum(axis=0)
            a = ag_ref[pl.ds(base, ch), 0:LANE].astype(F32)
            sgc = jax.nn.sigmoid(ag_ref[pl.ds(base, ch), LANE:2 * LANE].astype(F32))
            dag_ref[pl.ds(base, ch), 0:LANE] = (dh * sgc).astype(dag_ref.dtype)
            dag_ref[pl.ds(base, ch), LANE:2 * LANE] = (dh * a * sgc * (1.0 - sgc)).astype(dag_ref.dtype)
            return carry

        lax.fori_loop(0, s_len // ch, chunk, 0)
        for kk in range(CONV_W):
            dw_ref[kk:kk + 1, :] = jnp.sum(wacc_ref[kk * 8:(kk + 1) * 8, :], axis=0, keepdims=True)

    nblk = CONV_C // LANE
    pair = pl.BlockSpec((s_len, 2 * LANE), lambda j: (0, ZC // (2 * LANE) + j))
    return pl.pallas_call(
        body,
        grid=(nblk,),
        in_specs=[pair, pl.BlockSpec((s_len, LANE), lambda j: (0, j)), pl.BlockSpec((CONV_W, LANE), lambda j: (0, j)), ANY],
        out_specs=[pair, pl.BlockSpec((CONV_W, LANE), lambda j: (0, j)), pl.BlockSpec((1, LANE), lambda j: (0, j))],
        out_shape=[jax.ShapeDtypeStruct(dz.shape, dz.dtype), jax.ShapeDtypeStruct((CONV_W, CONV_C), F32), jax.ShapeDtypeStruct((1, CONV_C), F32)],
        scratch_shapes=[pltpu.VMEM((s_len + CONV_PAD, LANE), F32), pltpu.VMEM((s_len + CONV_PAD, LANE), F32),
                        pltpu.VMEM((CONV_W * 8, LANE), F32)],
        input_output_aliases={3: 0},
        compiler_params=_cparams(("arbitrary",)),
        name="conv_bwd",
    )(z, dc, conv_w, dz)


POOL_PAD = 16


def _pool_count(base, ch, w):
    t = base + lax.broadcasted_iota(jnp.int32, (ch, 1), 0)
    return jnp.minimum(t + 1, w).astype(F32)


def pool_fwd(z, pool_w, pool_scale):
    s_len = z.shape[0]
    ch = min(CONV_CHUNK, s_len)

    def body(u_ref, pw_ref, sc_ref, m_ref, pad_ref):
        gi = pl.program_id(0)
        pad_ref[0:POOL_PAD, :] = jnp.zeros((POOL_PAD, LANE), F32)
        pad_ref[POOL_PAD:POOL_PAD + s_len, :] = u_ref[...].astype(F32)

        def run(w):
            def chunk(i, carry):
                base = pl.multiple_of(i * ch, ch)
                acc = jnp.zeros((ch, LANE), F32)
                for j in range(w):
                    acc = acc + pad_ref[pl.ds(base + POOL_PAD - j, ch), :]
                d = acc / _pool_count(base, ch, w) - pad_ref[pl.ds(base + POOL_PAD, ch), :]
                md = jnp.dot(d.astype(BF16), pw_ref[0], preferred_element_type=F32)
                m_ref[pl.ds(base, ch), :] = (md * sc_ref[...]).astype(m_ref.dtype)
                return carry

            lax.fori_loop(0, s_len // ch, chunk, 0)

        for g, w in enumerate(POOL_WINDOWS):
            pl.when(gi == g)(functools.partial(run, w))

    return pl.pallas_call(
        body,
        grid=(POOL_G,),
        in_specs=[pl.BlockSpec((s_len, LANE), lambda g: (0, ZP // LANE + g)), pl.BlockSpec((1, POOL_GD, POOL_GD), lambda g: (g, 0, 0)),
                  pl.BlockSpec((1, LANE), lambda g: (0, g))],
        out_specs=pl.BlockSpec((s_len, LANE), lambda g: (0, g)),
        out_shape=jax.ShapeDtypeStruct((s_len, POOL_C), BF16),
        scratch_shapes=[pltpu.VMEM((s_len + POOL_PAD, LANE), F32)],
        compiler_params=_cparams(("arbitrary",)),
        name="pool_fwd",
    )(z, pool_w, pool_scale)


def pool_bwd(z, dm, pool_w, pool_scale, dz):
    s_len = z.shape[0]
    ch = min(CONV_CHUNK, s_len)

    def body(u_ref, dm_ref, pw_ref, sc_ref, dz_in, du_ref, dpw_ref, dsc_ref, pad_ref, epad_ref, dd_ref, sacc_ref):
        del dz_in
        gi = pl.program_id(0)
        pad_ref[0:POOL_PAD, :] = jnp.zeros((POOL_PAD, LANE), F32)
        pad_ref[POOL_PAD:POOL_PAD + s_len, :] = u_ref[...].astype(F32)
        epad_ref[s_len:s_len + POOL_PAD, :] = jnp.zeros((POOL_PAD, LANE), F32)
        dpw_ref[...] = jnp.zeros_like(dpw_ref)
        sacc_ref[...] = jnp.zeros_like(sacc_ref)

        def run(w):
            def first(i, carry):
                base = pl.multiple_of(i * ch, ch)
                acc = jnp.zeros((ch, LANE), F32)
                for j in range(w):
                    acc = acc + pad_ref[pl.ds(base + POOL_PAD - j, ch), :]
                cnt = _pool_count(base, ch, w)
                d = (acc / cnt - pad_ref[pl.ds(base + POOL_PAD, ch), :]).astype(BF16)
                md = jnp.dot(d, pw_ref[0], preferred_element_type=F32)
                dmc = dm_ref[pl.ds(base, ch), :]
                sacc_ref[...] += (dmc * md).reshape(ch // 8, 8, LANE).sum(axis=0)
                dmd = (dmc * sc_ref[...]).astype(BF16)
                dpw_ref[0] += lax.dot_general(d, dmd, (((0,), (0,)), ((), ())), preferred_element_type=F32)
                dd = lax.dot_general(dmd, pw_ref[0], (((1,), (1,)), ((), ())), preferred_element_type=F32)
                dd_ref[pl.ds(base, ch), :] = dd
                epad_ref[pl.ds(base, ch), :] = dd / cnt
                return carry

            lax.fori_loop(0, s_len // ch, first, 0)

            def second(i, carry):
                base = pl.multiple_of(i * ch, ch)
                acc = jnp.zeros((ch, LANE), F32)
                for j in range(w):
                    acc = acc + epad_ref[pl.ds(base + j, ch), :]
                du_ref[pl.ds(base, ch), :] = (acc - dd_ref[pl.ds(base, ch), :]).astype(du_ref.dtype)
                return carry

            lax.fori_loop(0, s_len // ch, second, 0)

        for g, w in enumerate(POOL_WINDOWS):
            pl.when(gi == g)(functools.partial(run, w))
        dsc_ref[...] = jnp.sum(sacc_ref[...], axis=0, keepdims=True)

    return pl.pallas_call(
        body,
        grid=(POOL_G,),
        in_specs=[pl.BlockSpec((s_len, LANE), lambda g: (0, ZP // LANE + g)), pl.BlockSpec((s_len, LANE), lambda g: (0, g)),
                  pl.BlockSpec((1, POOL_GD, POOL_GD), lambda g: (g, 0, 0)), pl.BlockSpec((1, LANE), lambda g: (0, g)), ANY],
        out_specs=[pl.BlockSpec((s_len, LANE), lambda g: (0, ZP // LANE + g)), pl.BlockSpec((1, POOL_GD, POOL_GD), lambda g: (g, 0, 0)),
                   pl.BlockSpec((1, LANE), lambda g: (0, g))],
        out_shape=[jax.ShapeDtypeStruct(dz.shape, dz.dtype), jax.ShapeDtypeStruct((POOL_G, POOL_GD, POOL_GD), F32),
                   jax.ShapeDtypeStruct((1, POOL_C), F32)],
        scratch_shapes=[pltpu.VMEM((s_len + POOL_PAD, LANE), F32), pltpu.VMEM((s_len + POOL_PAD, LANE), F32),
                        pltpu.VMEM((s_len, LANE), F32), pltpu.VMEM((8, LANE), F32)],
        input_output_aliases={4: 0},
        compiler_params=_cparams(("arbitrary",)),
        name="pool_bwd",
    )(z, dm, pool_w, pool_scale, dz)


def _row(v):
    return v.reshape(1, -1)


def _rms_body(x_ref, g_ref, o_ref):
    o_ref[...] = _rms(x_ref[...], g_ref[...]).astype(o_ref.dtype)


def _post_body(y_ref, x_ref, g_ref, o_ref):
    o_ref[...] = x_ref[...] + _rms(y_ref[...], g_ref[...])


def _post_bwd_body(y_ref, dh_ref, g_ref, dy_ref, dg_ref):
    _, vjp = jax.vjp(_rms, y_ref[...], g_ref[...])
    dy, dg = vjp(dh_ref[...])
    dy_ref[...] = dy.astype(dy_ref.dtype)
    _acc(dg_ref, dg)


def _pre_bwd_body(x_ref, dhn_ref, dres_ref, g_ref, dx_ref, dg_ref):
    _, vjp = jax.vjp(_rms, x_ref[...], g_ref[...])
    dx, dg = vjp(dhn_ref[...])
    dx_ref[...] = dres_ref[...] + dx
    _acc(dg_ref, dg)


def _post_next_body(y_ref, x_ref, g_ref, gn_ref, o_ref, n_ref):
    o = x_ref[...] + _rms(y_ref[...], g_ref[...])
    o_ref[...] = o
    n_ref[...] = _rms(o, gn_ref[...]).astype(n_ref.dtype)


def _pre_post_bwd_body(x_ref, dhn_ref, dres_ref, y_ref, g_ref, gy_ref, dx_ref, dy_ref, dg_ref, dgy_ref):
    _, vjp = jax.vjp(_rms, x_ref[...], g_ref[...])
    dx, dg = vjp(dhn_ref[...])
    dx = dres_ref[...] + dx
    dx_ref[...] = dx
    _, vjp_y = jax.vjp(_rms, y_ref[...], gy_ref[...])
    dy, dgy = vjp_y(dx)
    dy_ref[...] = dy.astype(dy_ref.dtype)
    _acc(dg_ref, dg)
    _acc(dgy_ref, dgy)


def mixer_fwd(x, tabs, w, tag, carry=None, head=None, h=None):
    s_len = x.shape[0]
    cc, sa, sb = tabs
    sv = {"x": x}

    if h is None:
        (h,) = rowwise("mix_norm_pre" + tag, _rms_body, s_len, [_whole(x)], [_row(w["mix_norm_pre"])], [(D_MODEL, BF16)])
    if head is None:
        z = mm(h, w["w_in"], out_dtype=BF16, name="in_proj" + tag)
    else:
        z, arrived = mm(h, w["w_in"], out_dtype=BF16, name="in_proj" + tag, carry=head[0])
        w = {**w, **head[1](arrived)}

    def prep_body(z_ref, cc_ref, sa_ref, sb_ref, qg_ref, kg_ref, qn_ref, ckv_ref, kr_ref):
        qn_ref[...] = _rms(z_ref[:, 0:Q_RANK].astype(F32), qg_ref[...]).astype(qn_ref.dtype)
        ckv_ref[...] = _rms(z_ref[:, Q_RANK:Q_RANK + KV_RANK].astype(F32), kg_ref[...]).astype(ckv_ref.dtype)
        kr_ref[...] = _rope(z_ref[:, Q_RANK + KV_RANK:ZA_W].astype(F32), cc_ref[...], sa_ref[...], sb_ref[...])

    qn, ckvn, kr = rowwise("attn_prep" + tag, prep_body, s_len, [(z, ZA_W, ZA // ZA_W), _whole(cc), _whole(sa), _whole(sb)],
                           [_row(w["q_norm"]), _row(w["kv_norm"])], [(Q_RANK, BF16), (KV_RANK, BF16), (HP, F32)])
    q_raw = mm(qn, w["w_uq"], out_dtype=BF16, name="q_proj" + tag)
    kv_raw = mm(ckvn, w["w_ukv"], out_dtype=BF16, name="kv_proj" + tag)

    def qkv_body(q_ref, kv_ref, kr_ref, cc_ref, sa_ref, sb_ref, qo_ref, ko_ref, vo_ref):
        c_, a_, b_, kro = cc_ref[...], sa_ref[...], sb_ref[...], kr_ref[...]
        for hh in range(N_HEADS):
            sl = slice(hh * HP, (hh + 1) * HP)
            qo_ref[:, sl] = _rope(q_ref[:, sl].astype(F32), c_, a_, b_).astype(qo_ref.dtype)
            ko_ref[:, sl] = (kv_ref[:, sl].astype(F32) + kro).astype(ko_ref.dtype)
        lane = lax.broadcasted_iota(jnp.int32, (q_ref.shape[0], HW), 1)
        vo_ref[...] = jnp.where((lane & (HP - 1)) == VDIM, 1.0, kv_ref[:, HW:2 * HW].astype(F32)).astype(vo_ref.dtype)

    q, k, v = rowwise("qkv_rope" + tag, qkv_body, s_len, [_whole(q_raw), _whole(kv_raw), _whole(kr), _whole(cc), _whole(sa), _whole(sb)], [],
                      [(HW, BF16)] * 3)
    o, lse, carried = attention_fwd(q, k, v, "attention_fwd" + tag, carry)
    y_attn = mm(o, w["w_attn_o"], out_dtype=BF16, name="attn_out" + tag)

    c = conv_fwd(z, w["conv_w"], _row(w["conv_b"]))

    def ln_body(c_ref, g_ref, b_ref, o_ref):
        o_ref[...] = _silu(_layer_norm(c_ref[...], g_ref[...], b_ref[...])).astype(o_ref.dtype)

    (cs,) = rowwise("conv_ln_silu" + tag, ln_body, s_len, [_whole(c)], [_row(w["conv_ln_g"]), _row(w["conv_ln_b"])], [(CONV_C, BF16)])
    y_conv = mm(cs, w["w_conv_o"], out_dtype=BF16, name="conv_out" + tag)

    m = pool_fwd(z, w["pool_w"], _row(w["pool_scale"]))
    y_pool = mm(m, w["w_pool_o"], out_dtype=BF16, name="pool_out" + tag)

    def merge_body(ya_ref, yc_ref, yp_ref, gl_ref, o_ref):
        gl = gl_ref[...].astype(F32)
        o_ref[...] = (jax.nn.sigmoid(gl[:, 0:D_MODEL]) * ya_ref[...].astype(F32) + jax.nn.sigmoid(gl[:, D_MODEL:2 * D_MODEL]) * yc_ref[...].astype(F32)
                      + jax.nn.sigmoid(gl[:, 2 * D_MODEL:3 * D_MODEL]) * yp_ref[...].astype(F32)).astype(o_ref.dtype)

    (merged,) = rowwise("gate_merge" + tag, merge_body, s_len, [_whole(y_attn), _whole(y_conv), _whole(y_pool), (z, 3 * D_MODEL, 0)], [],
                        [(D_MODEL, BF16)])
    mo = mm(merged, w["w_mix_o"], name="mix_out" + tag)

    h1, hn = rowwise("mix_norm_post" + tag, _post_next_body, s_len, [_whole(mo), _whole(x)],
                     [_row(w["mix_norm_post"]), _row(w["ffn_norm_pre"])], [(D_MODEL, F32), (D_MODEL, BF16)])
    sv.update(h=h, z=z, qn=qn, ckvn=ckvn, q=q, k=k, v=v, o=o, lse=lse, c=c, cs=cs, m=m, y_attn=y_attn, y_conv=y_conv, y_pool=y_pool,
              merged=merged, mo=mo, w=w, hn=hn)
    return h1, sv, carried


def ffn_fwd(h1, hn, w, tag, next_gain=None):
    s_len = h1.shape[0]
    gu, act = ffn_in(hn, w["w_gu"], "ffn_in" + tag)
    y = mm(act, w["w_down"], name="ffn_out" + tag)
    if next_gain is None:
        (h2,) = rowwise("ffn_norm_post" + tag, _post_body, s_len, [_whole(y), _whole(h1)], [_row(w["ffn_norm_post"])], [(D_MODEL, F32)])
        h_next = None
    else:
        h2, h_next = rowwise("ffn_norm_post" + tag, _post_next_body, s_len, [_whole(y), _whole(h1)],
                             [_row(w["ffn_norm_post"]), _row(next_gain)], [(D_MODEL, F32), (D_MODEL, BF16)])
    return h2, dict(h1=h1, hn=hn, gu=gu, act=act, y=y), h_next


def ffn_bwd(dh2, sv, w, tag, mo, post=None, first=None):
    s_len = dh2.shape[0]
    g = {}
    if post is None:
        d_y, g["ffn_norm_post"] = rowwise("ffn_norm_post_bwd" + tag, _post_bwd_body, s_len, [_whole(sv["y"]), _whole(dh2)],
                                          [_row(w["ffn_norm_post"])], [(D_MODEL, BF16)], [((1, D_MODEL), F32)])
    else:
        d_y, g["ffn_norm_post"] = post
    swapped = None
    if first is None:
        g["w_down"] = mm(sv["act"], d_y, ta=True, name="ffn_out_dw" + tag)
    else:
        g["w_down"], swapped = mm(sv["act"], d_y, ta=True, name="ffn_out_dw" + tag, carry=first)
    d_gu = ffn_out_dx(d_y, w["w_down"], sv["gu"], "ffn_out_dx" + tag)
    g["w_gu"] = mm(sv["hn"], d_gu, ta=True, name="ffn_in_dw" + tag)
    d_hn = mm(d_gu, w["w_gu"], tb=True, name="ffn_in_dx" + tag)
    dh1, d_mo, g["ffn_norm_pre"], d_mix_post = rowwise(
        "ffn_norm_pre_bwd" + tag, _pre_post_bwd_body, s_len, [_whole(sv["h1"]), _whole(d_hn), _whole(dh2), _whole(mo)],
        [_row(w["ffn_norm_pre"]), _row(w["mix_norm_post"])], [(D_MODEL, F32), (D_MODEL, BF16)], [((1, D_MODEL), F32), ((1, D_MODEL), F32)])
    return dh1, g, (d_mo, d_mix_post), swapped


def mixer_bwd(dh1, sv, tabs, w, tag, post, carry=None, tail=None, prev=None, first=None):
    s_len = dh1.shape[0]
    cc, sa, sb = tabs
    g = {}

    d_mo, g["mix_norm_post"] = post
    if first is None:
        g["w_mix_o"] = mm(sv["merged"], d_mo, ta=True, name="mix_out_dw" + tag)
    else:
        g["w_mix_o"], swapped = mm(sv["merged"], d_mo, ta=True, name="mix_out_dw" + tag, carry=first)
        carry = carry(swapped)
    d_merged = mm(d_mo, w["w_mix_o"], tb=True, name="mix_out_dx" + tag)

    def merge_bwd_body(dm_ref, ya_ref, yc_ref, yp_ref, gl_ref, dya_ref, dyc_ref, dyp_ref, dgl_ref):
        dmg = dm_ref[...]
        for i, (y_ref, dy_ref) in enumerate(((ya_ref, dya_ref), (yc_ref, dyc_ref), (yp_ref, dyp_ref))):
            sg = jax.nn.sigmoid(gl_ref[:, i * D_MODEL:(i + 1) * D_MODEL].astype(F32))
            dy_ref[...] = (dmg * sg).astype(dy_ref.dtype)
            dgl_ref[:, i * D_MODEL:(i + 1) * D_MODEL] = (dmg * y_ref[...].astype(F32) * sg * (1.0 - sg)).astype(dgl_ref.dtype)

    d_ya, d_yc, d_yp, dz = rowwise(
        "gate_merge_bwd" + tag, merge_bwd_body, s_len,
        [_whole(d_merged), _whole(sv["y_attn"]), _whole(sv["y_conv"]), _whole(sv["y_pool"]), (sv["z"], 3 * D_MODEL, 0)], [],
        [(D_MODEL, BF16)] * 3 + [(3 * D_MODEL, BF16, ZW, 0)], tile=WIDE_ROW_TILE)

    g["w_pool_o"] = mm(sv["m"], d_yp, ta=True, name="pool_out_dw" + tag)
    d_m = mm(d_yp, w["w_pool_o"], tb=True, name="pool_out_dx" + tag)
    dz, g["pool_w"], g["pool_scale"] = pool_bwd(sv["z"], d_m, w["pool_w"], _row(w["pool_scale"]), dz)

    g["w_conv_o"] = mm(sv["cs"], d_yc, ta=True, name="conv_out_dw" + tag)
    d_cs = mm(d_yc, w["w_conv_o"], tb=True, name="conv_out_dx" + tag)

    def ln_bwd_body(c_ref, dcs_ref, g_ref, b_ref, dc_ref, dg_ref, db_ref):
        f = lambda c_, g_, b_: _silu(_layer_norm(c_, g_, b_))
        _, vjp = jax.vjp(f, c_ref[...], g_ref[...], b_ref[...])
        dc, dg, db = vjp(dcs_ref[...])
        dc_ref[...] = dc
        _acc(dg_ref, dg)
        _acc(db_ref, db)

    d_c, g["conv_ln_g"], g["conv_ln_b"] = rowwise("conv_ln_silu_bwd" + tag, ln_bwd_body, s_len, [_whole(sv["c"]), _whole(d_cs)],
                                                  [_row(w["conv_ln_g"]), _row(w["conv_ln_b"])], [(CONV_C, F32)],
                                                  [((1, CONV_C), F32), ((1, CONV_C), F32)])
    dz, g["conv_w"], g["conv_b"] = conv_bwd(sv["z"], d_c, w["conv_w"], dz)

    g["w_attn_o"] = mm(sv["o"], d_ya, ta=True, name="attn_out_dw" + tag)
    d_o = mm(d_ya, w["w_attn_o"], tb=True, out_dtype=BF16, name="attn_out_dx" + tag)
    delta = attention_delta(d_o, sv["o"])
    t_bwd = min(ATT_TILE_BWD, s_len)
    rows_of = lambda a: a.reshape(N_HEADS, s_len // t_bwd, 1, t_bwd)
    dq, dk, dv, carried = attention_bwd(sv["q"], sv["k"], sv["v"], d_o, rows_of(sv["lse"]), rows_of(delta), "attention_bwd" + tag, carry)

    def qkv_bwd_body(dq_ref, dk_ref, dv_ref, cc_ref, sa_ref, sb_ref, dqp_ref, dkv_ref, dkr_ref):
        c_, a_, b_ = cc_ref[...], sa_ref[...], sb_ref[...]
        dk_sum = jnp.zeros((dq_ref.shape[0], HP), F32)
        for hh in range(N_HEADS):
            sl = slice(hh * HP, (hh + 1) * HP)
            dqp_ref[:, sl] = _rope_t(dq_ref[:, sl], c_, a_, b_).astype(dqp_ref.dtype)
            dkh = dk_ref[:, sl].astype(F32)
            dkv_ref[:, sl] = dk_ref[:, sl]
            dk_sum = dk_sum + dkh
        dkv_ref[:, HW:2 * HW] = dv_ref[...]
        dkr_ref[...] = _rope_t(dk_sum, c_, a_, b_)

    dq_pre, dkv_pre, d_kr = rowwise("qkv_rope_bwd" + tag, qkv_bwd_body, s_len,
                                    [_whole(dq), _whole(dk), _whole(dv), _whole(cc), _whole(sa), _whole(sb)], [],
                                    [(HW, BF16), (2 * HW, BF16), (HP, F32)])
    g["w_uq"] = mm(sv["qn"], dq_pre, ta=True, name="q_proj_dw" + tag)
    d_qn = mm(dq_pre, w["w_uq"], tb=True, name="q_proj_dx" + tag)
    g["w_ukv"] = mm(sv["ckvn"], dkv_pre, ta=True, name="kv_proj_dw" + tag)
    d_ckvn = mm(dkv_pre, w["w_ukv"], tb=True, name="kv_proj_dx" + tag)

    def prep_bwd_body(z_ref, dqn_ref, dckv_ref, dkr_ref, qg_ref, kg_ref, dz_ref, dqg_ref, dkg_ref):
        _, vq = jax.vjp(_rms, z_ref[:, 0:Q_RANK].astype(F32), qg_ref[...])
        dcq, dqg = vq(dqn_ref[...])
        _, vk = jax.vjp(_rms, z_ref[:, Q_RANK:Q_RANK + KV_RANK].astype(F32), kg_ref[...])
        dckv, dkg = vk(dckv_ref[...])
        dz_ref[:, 0:Q_RANK] = dcq.astype(dz_ref.dtype)
        dz_ref[:, Q_RANK:Q_RANK + KV_RANK] = dckv.astype(dz_ref.dtype)
        dz_ref[:, Q_RANK + KV_RANK:ZA_W] = dkr_ref[...].astype(dz_ref.dtype)
        _acc(dqg_ref, dqg)
        _acc(dkg_ref, dkg)

    dz, g["q_norm"], g["kv_norm"] = rowwise("attn_prep_bwd" + tag, prep_bwd_body, s_len,
                                            [(sv["z"], ZA_W, ZA // ZA_W), _whole(d_qn), _whole(d_ckvn), _whole(d_kr)],
                                            [_row(w["q_norm"]), _row(w["kv_norm"])], [(ZA_W, BF16, ZW, ZA // ZA_W)],
                                            [((1, Q_RANK), F32), ((1, KV_RANK), F32)], into={0: dz})

    if tail is None:
        g["w_in"] = mm(sv["h"], dz, ta=True, name="in_proj_dw" + tag)
        d_h = mm(dz, w["w_in"], tb=True, name="in_proj_dx" + tag)
    else:
        g["w_in"], tailed_rest = mm(sv["h"], dz, ta=True, name="in_proj_dw" + tag, carry=tail[0](g))
        d_h, tailed_in = mm(dz, w["w_in"], tb=True, name="in_proj_dx" + tag, carry=tail[1](g))
        carried = [carried, tailed_in + tailed_rest]
    if prev is None:
        dx, g["mix_norm_pre"] = rowwise(
            "mix_norm_pre_bwd" + tag, _pre_bwd_body, s_len, [_whole(sv["x"]), _whole(d_h), _whole(dh1)], [_row(w["mix_norm_pre"])], [(D_MODEL, F32)], [((1, D_MODEL), F32)])
        return dx, g, carried, None
    y_prev, gain_prev = prev
    dx, d_y_prev, g["mix_norm_pre"], d_gain_prev = rowwise(
        "mix_norm_pre_bwd" + tag, _pre_post_bwd_body, s_len, [_whole(sv["x"]), _whole(d_h), _whole(dh1), _whole(y_prev)],
        [_row(w["mix_norm_pre"]), _row(gain_prev)], [(D_MODEL, F32), (D_MODEL, BF16)], [((1, D_MODEL), F32), ((1, D_MODEL), F32)])
    return dx, g, carried, (d_y_prev, d_gain_prev)


def loss_head(h, target):
    s_len = h.shape[0]

    def body(h_ref, t_ref, dy_ref, loss_ref):
        err = h_ref[...] - t_ref[...]
        dy_ref[...] = err * (1.0 / D_MODEL)
        part = 0.5 * jnp.sum(jnp.mean(err * err, axis=-1, keepdims=True), axis=0, keepdims=True)
        _acc(loss_ref, jnp.broadcast_to(part, (1, LANE)))

    return rowwise("loss_head", body, s_len, [_whole(h), _whole(target)], [], [(D_MODEL, F32)], [((1, LANE), F32)])


def local_step(x, pos_col, target, layers):
    s_len = x.shape[0]
    tabs = rope_tables(pos_col, s_len)
    h, h_normed, saved = x, None, []
    for li, w in enumerate(layers):
        h, sv_mix, _ = mixer_fwd(h, tabs, w, f"_l{li}", h=h_normed)
        next_gain = layers[li + 1]["mix_norm_pre"] if li + 1 < len(layers) else None
        h, sv_ffn, h_normed = ffn_fwd(h, sv_mix["hn"], w, f"_l{li}", next_gain)
        saved.append((sv_mix, sv_ffn))
    dh, loss = loss_head(h, target)
    grads, post = [None] * len(layers), None
    for li in reversed(range(len(layers))):
        dh, g_ffn, mix_post, _ = ffn_bwd(dh, saved[li][1], layers[li], f"_l{li}", saved[li][0]["mo"], post)
        prev = (saved[li - 1][1]["y"], layers[li - 1]["ffn_norm_post"]) if li > 0 else None
        dh, g_mix, _, post = mixer_bwd(dh, saved[li][0], tabs, layers[li], f"_l{li}", mix_post, prev=prev)
        grads[li] = {**g_mix, **g_ffn}
    return loss[0, 0], dh, grads


def _pad_heads_cols(wm, per_head):
    r = wm.shape[0]
    return jnp.pad(wm.reshape(r, N_HEADS, per_head), ((0, 0), (0, 0), (0, HP - per_head))).reshape(r, HW)


def _unpad_heads_cols(wm, per_head):
    r = wm.shape[0]
    return wm.reshape(r, N_HEADS, HP)[:, :, :per_head].reshape(r, N_HEADS * per_head)


def align_weights(p):
    out = dict(p)
    if "w_in" in p:
        w_in = p["w_in"]
        r = w_in.shape[0]
        zeros = lambda n: jnp.zeros((r, n), w_in.dtype)
        conv = w_in[:, O_CONV:O_POOL].reshape(r, 2, CONV_C // LANE, LANE).transpose(0, 2, 1, 3).reshape(r, 2 * CONV_C)
        out["w_in"] = jnp.concatenate([
            w_in[:, O_GATE:D_IN], conv, w_in[:, O_POOL:O_GATE], w_in[:, O_Q:O_KR],
            zeros(KR_LANE), w_in[:, O_KR:O_CONV], zeros(HP - KR_LANE - ROPE)], axis=1)
    if "w_uq" in p:
        out["w_uq"] = _pad_heads_cols(p["w_uq"], NOPE + ROPE)
        out["w_ukv"] = jnp.concatenate([_pad_heads_cols(p["w_uk"], NOPE), _pad_heads_cols(p["w_uv"], VDIM)], axis=1)
        wo = p["w_attn_o"]
        out["w_attn_o"] = jnp.pad(wo.reshape(N_HEADS, VDIM, D_MODEL), ((0, 0), (0, HP - VDIM), (0, 0))).reshape(HW, D_MODEL)
        del out["w_uk"], out["w_uv"]
    if "w_gate" in p:
        out["w_gu"] = jnp.concatenate([p["w_gate"][:, :FF_HALF], p["w_up"][:, :FF_HALF], p["w_gate"][:, FF_HALF:], p["w_up"][:, FF_HALF:]], axis=1)
        del out["w_gate"], out["w_up"]
    return out


def unalign_grads(g):
    out = dict(g)
    if "w_in" in g:
        gi = g["w_in"]
        kr0 = ZA + Q_RANK + KV_RANK + KR_LANE
        r = gi.shape[0]
        conv = gi[:, ZC:ZP].reshape(r, CONV_C // LANE, 2, LANE).transpose(0, 2, 1, 3).reshape(r, 2 * CONV_C)
        out["w_in"] = jnp.concatenate([gi[:, ZA:ZA + Q_RANK + KV_RANK], gi[:, kr0:kr0 + ROPE], conv, gi[:, ZP:ZA], gi[:, ZG:ZC]], axis=1)
    if "w_uq" in g:
        out["w_uq"] = _unpad_heads_cols(g["w_uq"], NOPE + ROPE)
        out["w_uk"] = _unpad_heads_cols(g["w_ukv"][:, :HW], NOPE)
        out["w_uv"] = _unpad_heads_cols(g["w_ukv"][:, HW:], VDIM)
        out["w_attn_o"] = g["w_attn_o"].reshape(N_HEADS, HP, D_MODEL)[:, :VDIM].reshape(N_HEADS * VDIM, D_MODEL)
        del out["w_ukv"]
    if "w_gu" in g:
        gu = g["w_gu"]
        out["w_gate"] = jnp.concatenate([gu[:, 0:FF_HALF], gu[:, 2 * FF_HALF:3 * FF_HALF]], axis=1)
        out["w_up"] = jnp.concatenate([gu[:, FF_HALF:2 * FF_HALF], gu[:, 3 * FF_HALF:]], axis=1)
        del out["w_gu"]
    return out


MESH = pl.DeviceIdType.MESH
ANY = pl.BlockSpec(memory_space=pl.ANY)


def _place():
    return lax.axis_index("x"), lax.axis_index("y"), lax.axis_index("c")


def _other_chips(x, y):
    return [(1 - x, y), (x, 1 - y), (1 - x, 1 - y)]


def _half_rows(rows, c):
    assert rows % (2 * HALF_ALIGN) == 0, rows
    return pl.ds(pl.multiple_of(c * (rows // 2), HALF_ALIGN), rows // 2)


class GatherShards:
    def __init__(self, local):
        self.ins = list(local)
        self.outs = [jax.ShapeDtypeStruct((N_CHIPS, *a.shape), a.dtype) for a in local]
        self.n_sems = 6 * len(local)
        self.base = 0

    def _first(self, in_refs, out_refs, send_sems, recv_sems):
        x, y, c = _place()
        me = 2 * x + y
        chips = _other_chips(x, y)

        def copy(i, k, slot, core, to, src=None):
            dst = out_refs[i].at[slot, _half_rows(out_refs[i].shape[1], core)]
            return pltpu.make_async_remote_copy(src_ref=dst if src is None else src, dst_ref=dst, send_sem=send_sems.at[self.base + 6 * i + k],
                                                recv_sem=recv_sems.at[self.base + 6 * i + k], device_id=to, device_id_type=MESH)

        first = [copy(i, j, me, c, (*chip, c), src=in_refs[i].at[_half_rows(in_refs[i].shape[0], c)])
                 for i in range(len(in_refs)) for j, chip in enumerate(chips)]
        return first, copy

    def start(self, in_refs, out_refs, send_sems, recv_sems):
        first, _ = self._first(in_refs, out_refs, send_sems, recv_sems)
        for cp in first:
            cp.start()

    def finish(self, in_refs, out_refs, send_sems, recv_sems):
        first, copy = self._first(in_refs, out_refs, send_sems, recv_sems)
        x, y, c = _place()
        slots = [2 * cx + cy for cx, cy in _other_chips(x, y)]
        sibling = (x, y, 1 - c)
        passed = []
        for i in range(len(in_refs)):
            for j in range(3):
                copy(i, j, slots[j], c, sibling).wait_recv()
                fwd = copy(i, 3 + j, slots[j], c, sibling)
                fwd.start()
                passed.append(fwd)
        for i in range(len(in_refs)):
            for j in range(3):
                copy(i, 3 + j, slots[j], 1 - c, sibling).wait_recv()
        for cp in first + passed:
            cp.wait_send()


class ChipExchange:
    def __init__(self, parts):
        self.ins = list(parts)
        self.outs = [jax.ShapeDtypeStruct((3, *a.shape[1:]), a.dtype) for a in parts]
        self.n_sems = 3 * len(parts)
        self.base = 0

    def _copies(self, in_refs, out_refs, send_sems, recv_sems):
        x, y, c = _place()
        return [pltpu.make_async_remote_copy(src_ref=in_refs[i].at[2 * chip[0] + chip[1]], dst_ref=out_refs[i].at[j],
                                             send_sem=send_sems.at[self.base + 3 * i + j], recv_sem=recv_sems.at[self.base + 3 * i + j],
                                             device_id=(*chip, c), device_id_type=MESH)
                for i in range(len(in_refs)) for j, chip in enumerate(_other_chips(x, y))]

    def start(self, in_refs, out_refs, send_sems, recv_sems):
        for cp in self._copies(in_refs, out_refs, send_sems, recv_sems):
            cp.start()

    def finish(self, in_refs, out_refs, send_sems, recv_sems):
        copies = self._copies(in_refs, out_refs, send_sems, recv_sems)
        for cp in copies:
            cp.wait_recv()
        for cp in copies:
            cp.wait_send()


def run_exchange(ex, name):
    n_in, n_out = len(ex.ins), len(ex.outs)

    def body(*refs):
        ins, outs, sems = refs[:n_in], refs[n_in:n_in + n_out], refs[n_in + n_out:]
        ex.start(ins, outs, *sems)
        ex.finish(ins, outs, *sems)

    return pl.pallas_call(
        body,
        in_specs=[ANY] * n_in,
        out_specs=[ANY] * n_out,
        out_shape=list(ex.outs),
        scratch_shapes=[pltpu.SemaphoreType.DMA((ex.n_sems,)), pltpu.SemaphoreType.DMA((ex.n_sems,))],
        name=name,
    )(*ex.ins)


class SiblingSwap:
    def __init__(self, gs):
        self.ins = list(gs)
        self.outs = [jax.ShapeDtypeStruct((N_CHIPS, a.shape[1] // 2, a.shape[2]), a.dtype) for a in gs]
        self.n_sems = 4 * len(gs)

    def _copies(self, in_refs, out_refs, send_sems, recv_sems):
        x, y, c = _place()
        return [pltpu.make_async_remote_copy(src_ref=in_refs[i].at[j, _half_rows(in_refs[i].shape[1], 1 - c)], dst_ref=out_refs[i].at[j],
                                             send_sem=send_sems.at[4 * i + j], recv_sem=recv_sems.at[4 * i + j],
                                             device_id=(x, y, 1 - c), device_id_type=MESH)
                for i in range(len(in_refs)) for j in range(N_CHIPS)]

    def start(self, in_refs, out_refs, send_sems, recv_sems):
        for cp in self._copies(in_refs, out_refs, send_sems, recv_sems):
            cp.start()

    def finish(self, in_refs, out_refs, send_sems, recv_sems):
        copies = self._copies(in_refs, out_refs, send_sems, recv_sems)
        for cp in copies:
            cp.wait_recv()
        for cp in copies:
            cp.wait_send()


def sibling_gather(fs):
    n = len(fs)
    layers = fs[0].shape[0]

    def body(*refs):
        out_refs, (send_sems, recv_sems) = refs[n:2 * n], refs[2 * n:]
        x, y, c = _place()

        def copy(i, l, core):
            part = out_refs[i].at[l, _half_rows(out_refs[i].shape[1], core)]
            return pltpu.make_async_remote_copy(src_ref=part, dst_ref=part, send_sem=send_sems.at[layers * i + l],
                                                recv_sem=recv_sems.at[layers * i + l], device_id=(x, y, 1 - c), device_id_type=MESH)

        sends = [copy(i, l, c) for i in range(n) for l in range(layers)]
        for cp in sends:
            cp.start()
        for i in range(n):
            for l in range(layers):
                copy(i, l, 1 - c).wait_recv()
        for cp in sends:
            cp.wait_send()

    return pl.pallas_call(
        body,
        in_specs=[ANY] * n,
        out_specs=[ANY] * n,
        out_shape=[jax.ShapeDtypeStruct(a.shape, a.dtype) for a in fs],
        scratch_shapes=[pltpu.SemaphoreType.DMA((layers * n,)), pltpu.SemaphoreType.DMA((layers * n,))],
        input_output_aliases={i: i for i in range(n)},
        name="sibling_gather",
    )(*fs)


class GatherAll:
    def __init__(self, vs):
        self.ins = list(vs)
        self.outs = [jax.ShapeDtypeStruct((8, *a.shape), a.dtype) for a in vs]
        self.n_sems = 7 * len(vs)
        self.base = 0

    def _first(self, in_refs, out_refs, send_sems, recv_sems):
        x, y, c = _place()
        me, sibling = (x, y, c), (x, y, 1 - c)

        def copy(i, k, block, to, src=None):
            px, py, pc = block
            dst = out_refs[i].at[4 * px + 2 * py + pc]
            return pltpu.make_async_remote_copy(src_ref=dst if src is None else src, dst_ref=dst, send_sem=send_sems.at[self.base + 7 * i + k],
                                                recv_sem=recv_sems.at[self.base + 7 * i + k], device_id=to, device_id_type=MESH)

        first = []
        for i in range(len(in_refs)):
            first.append(copy(i, 0, me, sibling, src=in_refs[i]))
            first += [copy(i, 1 + j, me, (*chip, c), src=in_refs[i]) for j, chip in enumerate(_other_chips(x, y))]
        return first, copy

    def start(self, in_refs, out_refs, send_sems, recv_sems):
        first, _ = self._first(in_refs, out_refs, send_sems, recv_sems)
        for cp in first:
            cp.start()

    def finish(self, in_refs, out_refs, send_sems, recv_sems):
        first, copy = self._first(in_refs, out_refs, send_sems, recv_sems)
        x, y, c = _place()
        me, sibling = (x, y, c), (x, y, 1 - c)
        chips = _other_chips(x, y)
        passed = []
        for i in range(len(in_refs)):
            for j, chip in enumerate(chips):
                copy(i, 1 + j, (*chip, c), me).wait_recv()
                fwd = copy(i, 4 + j, (*chip, c), sibling)
                fwd.start()
                passed.append(fwd)
        for i in range(len(in_refs)):
            copy(i, 0, sibling, me).wait_recv()
            for j, chip in enumerate(chips):
                copy(i, 4 + j, (*chip, 1 - c), me).wait_recv()
        for cp in first + passed:
            cp.wait_send()


def _row_tile(rows, row_bytes):
    best = None
    for t in range(16, rows + 1, 16):
        if rows % t == 0 and t * row_bytes <= SUM_TILE_BYTES:
            best = t
    return best or rows


def sibling_sum(g, theirs, place, name):
    _, half, cols = theirs.shape
    tile = _row_tile(half, cols * 4)
    nt = half // tile

    def body(place_ref, g_ref, t_ref, o_ref):
        o_ref[...] = (g_ref[...].astype(F32) + t_ref[...].astype(F32)).astype(o_ref.dtype)

    spec = pl.BlockSpec((1, tile, cols), lambda j, i, place_ref: (j, i, 0))
    return pl.pallas_call(
        body,
        grid_spec=pltpu.PrefetchScalarGridSpec(
            num_scalar_prefetch=1, grid=(N_CHIPS, nt),
            in_specs=[pl.BlockSpec((1, tile, cols), lambda j, i, place_ref: (j, place_ref[1] * nt + i, 0)), spec], out_specs=spec),
        out_shape=jax.ShapeDtypeStruct(theirs.shape, BF16),
        compiler_params=_cparams(("parallel", "parallel")),
        name=name,
    )(place, g, theirs)


def chip_sum(p, others, place, layer, into, name):
    _, half, cols = p.shape
    tile = _row_tile(half, cols * 4)
    nt = half // tile

    def body(place_ref, p_ref, o3_ref, *rest):
        o_ref = rest[-1]
        acc = p_ref[0].astype(F32)
        for k in range(3):
            acc = acc + o3_ref[k].astype(F32)
        o_ref[0] = acc

    return pl.pallas_call(
        body,
        grid_spec=pltpu.PrefetchScalarGridSpec(
            num_scalar_prefetch=1, grid=(nt,),
            in_specs=[pl.BlockSpec((1, tile, cols), lambda i, place_ref: (place_ref[0], i, 0)),
                      pl.BlockSpec((3, tile, cols), lambda i, place_ref: (0, i, 0))] + ([] if into is None else [ANY]),
            out_specs=pl.BlockSpec((1, tile, cols), lambda i, place_ref: (layer, place_ref[1] * nt + i, 0))),
        out_shape=jax.ShapeDtypeStruct((N_LAYERS, 2 * half, cols), F32),
        input_output_aliases={} if into is None else {3: 0},
        compiler_params=_cparams(("parallel",)),
        name=name,
    )(place, p, others, *([] if into is None else [into]))


def sum_devices(a, name):
    n, rows, cols = a.shape
    tile = _row_tile(rows, cols * 4 * n)

    def body(a_ref, o_ref):
        acc = a_ref[0]
        for s in range(1, n):
            acc = acc + a_ref[s]
        o_ref[...] = acc

    return pl.pallas_call(body, grid=(rows // tile,), in_specs=[pl.BlockSpec((n, tile, cols), lambda i: (0, i, 0))],
                          out_specs=pl.BlockSpec((tile, cols), lambda i: (i, 0)), out_shape=jax.ShapeDtypeStruct((rows, cols), F32),
                          compiler_params=_cparams(("parallel",)), name=name)(a)


def adamw(w, g, m, v, name):
    layers, rows, cols = w.shape
    tile = rows
    for t in range(8, rows, 8):
        if rows % t == 0 and t * cols * 4 <= ADAM_TILE_BYTES:
            tile = t

    def body(w_ref, g_ref, m_ref, v_ref, d_ref, mo_ref, vo_ref):
        gg = g_ref[...]
        m_new = ADAM_B1 * m_ref[...] + (1.0 - ADAM_B1) * gg
        v_new = ADAM_B2 * v_ref[...] + (1.0 - ADAM_B2) * (gg * gg)
        m_hat = m_new / (1.0 - ADAM_B1 ** ADAM_STEP)
        v_hat = v_new / (1.0 - ADAM_B2 ** ADAM_STEP)
        d_ref[...] = -ADAM_LR * (m_hat / (jnp.sqrt(v_hat) + ADAM_EPS) + ADAM_WD * w_ref[...])
        mo_ref[...] = m_new
        vo_ref[...] = v_new

    spec = pl.BlockSpec((1, tile, cols), lambda l, i: (l, i, 0))
    shape = jax.ShapeDtypeStruct((layers, rows, cols), F32)
    return pl.pallas_call(body, grid=(layers, rows // tile), in_specs=[spec] * 4, out_specs=[spec] * 3, out_shape=[shape] * 3,
                          compiler_params=_cparams(("parallel", "parallel")), name=name)(w, g, m, v)


WEIGHTS = ["mix_norm_pre", "w_in", "q_norm", "w_uq", "kv_norm", "w_uk", "w_uv", "w_attn_o", "conv_w", "conv_b", "conv_ln_g", "conv_ln_b",
           "w_conv_o", "pool_w", "pool_scale", "w_pool_o", "w_mix_o", "mix_norm_post", "ffn_norm_pre", "w_gate", "w_up", "w_down",
           "ffn_norm_post"]
SHARDED = {"w_in": 2, "w_uq": 2, "w_uk": 2, "w_uv": 2, "w_attn_o": 2, "conv_w": 2, "w_conv_o": 2, "w_pool_o": 2, "w_mix_o": 1,
           "w_gate": 2, "w_up": 2, "w_down": 1}
REPLICATED = [n for n in WEIGHTS if n not in SHARDED]
ROW_PARAMS = [n for n in REPLICATED if n != "pool_w"]
ROWS_MINOR = ("w_in", "w_uq", "w_gate", "w_up", "conv_w")
N_CHIPS = 4
N_MIX_GROUPS = 6
MIX_MATRICES = ("w_in", "w_uq", "w_ukv", "w_attn_o", "w_conv_o", "w_pool_o", "conv_w", "w_mix_o")
CONV_WIRE_ROWS = 32
GROUPS = [(("w_in",), 1), (("w_uq",), 1), (("w_uk", "w_uv"), 1), (("w_attn_o", "w_conv_o", "w_pool_o"), 1), (("conv_w",), 1),
          (("w_mix_o",), 1), (("w_gate", "w_up"), 2), (("w_down",), 1)]


def _join(parts, axis):
    return parts[0] if len(parts) == 1 else jnp.concatenate(parts, axis=axis)


def _split_group(arr, names, axis, shapes):
    out, off = {}, 0
    ax = arr.ndim - 3 + axis
    for n in names:
        size = shapes[n][axis]
        out[n] = lax.slice_in_dim(arr, off, off + size, axis=ax)
        off += size
    return out


def _pack_rows(vectors):
    blocks = []
    for v in vectors:
        for li in range(v.shape[0]):
            blocks.append(jnp.pad(v[li][None, :], ((0, PACK_ROWS - 1), (0, PACK_W - v.shape[1]))))
    return jnp.concatenate(blocks, axis=0)


def _unpack_rows(packed, shapes):
    out, r = [], 0
    for layers, width in shapes:
        out.append(jnp.stack([packed[r + PACK_ROWS * li, :width] for li in range(layers)]))
        r += PACK_ROWS * layers
    return out


def kernel(x, positions, mix_norm_pre, w_in, q_norm, w_uq, kv_norm, w_uk, w_uv, w_attn_o, conv_w, conv_b, conv_ln_g, conv_ln_b, w_conv_o, pool_w, pool_scale, w_pool_o, w_mix_o, mix_norm_post, ffn_norm_pre, w_gate, w_up, w_down, ffn_norm_post, loss_target, m_mix_norm_pre, m_w_in, m_q_norm, m_w_uq, m_kv_norm, m_w_uk, m_w_uv, m_w_attn_o, m_conv_w, m_conv_b, m_conv_ln_g, m_conv_ln_b, m_w_conv_o, m_pool_w, m_pool_scale, m_w_pool_o, m_w_mix_o, m_mix_norm_post, m_ffn_norm_pre, m_w_gate, m_w_up, m_w_down, m_ffn_norm_post, v_mix_norm_pre, v_w_in, v_q_norm, v_w_uq, v_kv_norm, v_w_uk, v_w_uv, v_w_attn_o, v_conv_w, v_conv_b, v_conv_ln_g, v_conv_ln_b, v_w_conv_o, v_pool_w, v_pool_scale, v_w_pool_o, v_w_mix_o, v_mix_norm_post, v_ffn_norm_pre, v_w_gate, v_w_up, v_w_down, v_ffn_norm_post):
    given = dict(mix_norm_pre=mix_norm_pre, w_in=w_in, q_norm=q_norm, w_uq=w_uq, kv_norm=kv_norm, w_uk=w_uk, w_uv=w_uv, w_attn_o=w_attn_o,
                 conv_w=conv_w, conv_b=conv_b, conv_ln_g=conv_ln_g, conv_ln_b=conv_ln_b, w_conv_o=w_conv_o, pool_w=pool_w,
                 pool_scale=pool_scale, w_pool_o=w_pool_o, w_mix_o=w_mix_o, mix_norm_post=mix_norm_post, ffn_norm_pre=ffn_norm_pre,
                 w_gate=w_gate, w_up=w_up, w_down=w_down, ffn_norm_post=ffn_norm_post)
    mom = dict(mix_norm_pre=m_mix_norm_pre, w_in=m_w_in, q_norm=m_q_norm, w_uq=m_w_uq, kv_norm=m_kv_norm, w_uk=m_w_uk, w_uv=m_w_uv,
               w_attn_o=m_w_attn_o, conv_w=m_conv_w, conv_b=m_conv_b, conv_ln_g=m_conv_ln_g, conv_ln_b=m_conv_ln_b, w_conv_o=m_w_conv_o,
               pool_w=m_pool_w, pool_scale=m_pool_scale, w_pool_o=m_w_pool_o, w_mix_o=m_w_mix_o, mix_norm_post=m_mix_norm_post,
               ffn_norm_pre=m_ffn_norm_pre, w_gate=m_w_gate, w_up=m_w_up, w_down=m_w_down, ffn_norm_post=m_ffn_norm_post)
    var = dict(mix_norm_pre=v_mix_norm_pre, w_in=v_w_in, q_norm=v_q_norm, w_uq=v_w_uq, kv_norm=v_kv_norm, w_uk=v_w_uk, w_uv=v_w_uv,
               w_attn_o=v_w_attn_o, conv_w=v_conv_w, conv_b=v_conv_b, conv_ln_g=v_conv_ln_g, conv_ln_b=v_conv_ln_b, w_conv_o=v_w_conv_o,
               pool_w=v_pool_w, pool_scale=v_pool_scale, w_pool_o=v_w_pool_o, w_mix_o=v_w_mix_o, mix_norm_post=v_mix_norm_post,
               ffn_norm_pre=v_ffn_norm_pre, w_gate=v_w_gate, w_up=v_w_up, w_down=v_w_down, ffn_norm_post=v_ffn_norm_post)
    s_len = x.shape[1]
    sharded_names = [n for n in WEIGHTS if n in SHARDED]
    chip = 2 * lax.axis_index("x") + lax.axis_index("y")
    place = jnp.stack([chip, lax.axis_index("c")]).astype(jnp.int32)
    shard_shape = {n: given[n].shape for n in sharded_names}

    mix_groups, ffn_groups = GROUPS[:N_MIX_GROUPS], GROUPS[N_MIX_GROUPS:]
    weight_wire_shape = {n: (N_LAYERS, 2 * CONV_WIRE_ROWS, shard_shape[n][2]) if n == "conv_w" else shard_shape[n] for n in sharded_names}
    grad_wire_shape = {n: (N_LAYERS, CONV_WIRE_ROWS, shard_shape[n][2]) if n == "conv_w" else shard_shape[n] for n in sharded_names}
    pad_rows = lambda a: jnp.pad(a, ((0, CONV_WIRE_ROWS - CONV_W), (0, 0)))

    def weight_wires(groups, li):
        def wire(name):
            a = given[name][li]
            if name == "conv_w":
                hi = a.astype(BF16)
                return jnp.concatenate([pad_rows(hi), pad_rows((a - hi.astype(F32)).astype(BF16))], axis=0)
            return a.astype(BF16)

        return [_join([wire(n) for n in names], axis - 1) for names, axis in groups]

    def full_weights(groups, li, local, gathered):
        p = {n: given[n][li] for n in REPLICATED}
        p["pool_w"] = p["pool_w"].astype(BF16)
        for (names, axis), loc, got in zip(groups, local, gathered):
            got = lax.dynamic_update_slice(got, loc[None], (chip, 0, 0))
            per_chip = [_split_group(got[j], names, axis, weight_wire_shape) for j in range(N_CHIPS)]
            for n in names:
                parts = [pc[n] for pc in per_chip]
                if n == "conv_w":
                    parts = [q[:CONV_W].astype(F32) + q[CONV_WIRE_ROWS:CONV_WIRE_ROWS + CONV_W].astype(F32) for q in parts]
                p[n] = jnp.concatenate(parts, axis=SHARDED[n] - 1)
        return align_weights(p)

    def grad_wires(groups, g):
        wires = []
        for names, axis in groups:
            split = {n: jnp.split(pad_rows(g[n]) if n == "conv_w" else g[n], N_CHIPS, axis=SHARDED[n] - 1) for n in names}
            wires.append(jnp.stack([_join([split[n][j].astype(BF16) for n in names], axis - 1) for j in range(N_CHIPS)]))
        return wires

    def partial_sums(wires, theirs, tag):
        return [sibling_sum(w, t, place, f"sibling_sum_{tag}_{i}") for i, (w, t) in enumerate(zip(wires, theirs))]

    def chip_partials(groups, g, tag):
        wires = grad_wires(groups, g)
        return partial_sums(wires, run_exchange(SiblingSwap(wires), "sibling_swap_" + tag), tag)

    tabs = rope_tables(positions.reshape(s_len, 1), s_len)
    in_groups, rest_groups = mix_groups[:1], mix_groups[1:]
    heads = []
    for li in range(N_LAYERS):
        loc_rest = weight_wires(rest_groups, li)
        heads.append((GatherShards(loc_rest), functools.partial(full_weights, rest_groups, li, loc_rest)))
    loc_in0 = weight_wires(in_groups, 0)
    w_in0 = full_weights(in_groups, 0, loc_in0, run_exchange(GatherShards(loc_in0), "gather_w_in_l0"))
    loc_f0, loc_in1 = weight_wires(ffn_groups, 0), weight_wires(in_groups, 1)
    h, sv_m0, got = mixer_fwd(x[0], tabs, w_in0, "_l0", GatherShards(loc_f0 + loc_in1), heads[0])
    w_m0 = sv_m0["w"]
    w_f0 = full_weights(ffn_groups, 0, loc_f0, got[:len(loc_f0)])
    w_in1 = full_weights(in_groups, 1, loc_in1, got[len(loc_f0):])
    h, sv_f0, h_normed = ffn_fwd(h, sv_m0["hn"], w_f0, "_l0", w_in1["mix_norm_pre"])
    loc_f1 = weight_wires(ffn_groups, 1)
    h, sv_m1, got = mixer_fwd(h, tabs, w_in1, "_l1", GatherShards(loc_f1), heads[1], h=h_normed)
    w_m1 = sv_m1["w"]
    w_f1 = full_weights(ffn_groups, 1, loc_f1, got)
    h, sv_f1, _ = ffn_fwd(h, sv_m1["hn"], w_f1, "_l1")
    dh, loss_local = loss_head(h, loss_target[0])
    loss = lax.psum(loss_local[0, 0], MESH_AXES)

    last = {}

    def then_exchange(key, wires, tag, more=()):
        def make(theirs):
            last[key] = partial_sums(wires, theirs, tag)
            return ChipExchange(list(more) + last[key])
        return make

    dh, g_f1, post_m1, _ = ffn_bwd(dh, sv_f1, w_f1, "_l1", sv_m1["mo"])
    g_f1 = unalign_grads(g_f1)
    wires_f1 = grad_wires(ffn_groups, g_f1)
    dh, g_m1, o_f1, post_f0 = mixer_bwd(dh, sv_m1, tabs, w_m1, "_l1", post_m1, then_exchange("f1", wires_f1, "ffn_l1"),
                                        prev=(sv_f0["y"], w_f0["ffn_norm_post"]), first=SiblingSwap(wires_f1))
    p_f1 = last["f1"]
    g_m1 = unalign_grads(g_m1)
    wires_m1 = grad_wires(mix_groups, g_m1)
    dh, g_f0, post_m0, theirs_m1 = ffn_bwd(dh, sv_f0, w_f0, "_l0", sv_m0["mo"], post_f0, first=SiblingSwap(wires_m1))
    p_m1 = partial_sums(wires_m1, theirs_m1, "mix_l1")
    g_f0 = unalign_grads(g_f0)
    wires_f0 = grad_wires(ffn_groups, g_f0)

    def tail_rest(g):
        last["rest"] = chip_partials(rest_groups, unalign_grads({n: g[n] for n in MIX_MATRICES[1:]}), "rest_l0")
        return ChipExchange(last["rest"])

    def tail_in(g):
        last["in"] = chip_partials(in_groups, unalign_grads({"w_in": g["w_in"]}), "w_in_l0")
        return ChipExchange(last["in"])

    grad_x, g_m0, (got, o_m0), _ = mixer_bwd(dh, sv_m0, tabs, w_m0, "_l0", post_m0, then_exchange("f0", wires_f0, "ffn_l0", p_m1),
                                             (tail_rest, tail_in), first=SiblingSwap(wires_f0))
    p_f0 = last["f0"]
    o_m1, o_f0 = got[:len(p_m1)], got[len(p_m1):]
    p_m0 = last["in"] + last["rest"]
    grads = [{**g_m0, **g_f0}, {**g_m1, **g_f1}]
    grad_full = {n: jnp.stack([g[n].reshape(given[n].shape[1:]) for g in grads]) for n in REPLICATED}
    replicated = [_pack_rows([grad_full[n] for n in ROW_PARAMS]), grad_full["pool_w"].reshape(-1, POOL_GD)]
    device = 2 * chip + lax.axis_index("c")
    rows_all, pool_w_all = [lax.dynamic_update_slice(a, mine[None], (device, 0, 0))
                            for a, mine in zip(run_exchange(GatherAll(replicated), "gather_replicated"), replicated)]

    sums = {}
    for groups, base, per_layer in ((ffn_groups, N_MIX_GROUPS, ((1, p_f1, o_f1), (0, p_f0, o_f0))), (mix_groups, 0, ((1, p_m1, o_m1), (0, p_m0, o_m0)))):
        for li, parts, others in per_layer:
            for i, (p, o) in enumerate(zip(parts, others)):
                sums[base + i] = chip_sum(p, o, place, li, sums.get(base + i), f"chip_sum_{base + i}_l{li}")
    g_shard = {}
    for (names, axis), s in zip(GROUPS, sibling_gather([sums[i] for i in range(len(GROUPS))])):
        g_shard.update(_split_group(s, names, axis, grad_wire_shape))
    g_shard["conv_w"] = g_shard["conv_w"][:, :CONV_W]

    row_shapes = [given[n].shape for n in ROW_PARAMS]
    g_rows = sum_devices(rows_all, "row_params_sum")
    g_pool_w = sum_devices(pool_w_all, "pool_w_sum")
    g_rep = dict(zip(ROW_PARAMS, _unpack_rows(g_rows, row_shapes)))
    g_rep["pool_w"] = g_pool_w.reshape(given["pool_w"].shape)

    g_out, d_out, m_out, v_out = {}, {}, {}, {}
    for n in sharded_names + ["pool_w"]:
        shp = given[n].shape
        three_d = (shp[0], int(np.prod(shp[1:-1])), shp[-1])
        g_n = g_shard[n] if n in SHARDED else g_rep[n]
        view = (lambda a: jnp.swapaxes(a.reshape(three_d), 1, 2)) if n in ROWS_MINOR else (lambda a: a.reshape(three_d))
        back = (lambda a: jnp.swapaxes(a, 1, 2).reshape(shp)) if n in ROWS_MINOR else (lambda a: a.reshape(shp))
        d, mn, vn = adamw(view(given[n]), view(g_n), view(mom[n]), view(var[n]), "adamw_" + n)
        g_out[n], d_out[n], m_out[n], v_out[n] = g_n, back(d), back(mn), back(vn)
    rd, rm, rv = adamw(_pack_rows([given[n] for n in ROW_PARAMS])[None], g_rows[None], _pack_rows([mom[n] for n in ROW_PARAMS])[None],
                       _pack_rows([var[n] for n in ROW_PARAMS])[None], "adamw_row_params")
    for n, d, mn, vn in zip(ROW_PARAMS, *[_unpack_rows(a[0], row_shapes) for a in (rd, rm, rv)]):
        g_out[n], d_out[n], m_out[n], v_out[n] = g_rep[n], d, mn, vn

    return (loss, grad_x[None], *[g_out[n] for n in WEIGHTS], *[d_out[n] for n in WEIGHTS], *[m_out[n] for n in WEIGHTS],
            *[v_out[n] for n in WEIGHTS])
```

```python
import functools
import math

import numpy as np
import jax
import jax.numpy as jnp
from jax import lax
from jax.experimental import pallas as pl
from jax.experimental.pallas import tpu as pltpu

F32, BF16 = jnp.float32, jnp.bfloat16

D_MODEL = 1024
N_HEADS = 8
NOPE, ROPE, VDIM = 64, 32, 64
HALF_ROPE = ROPE // 2
Q_RANK, KV_RANK = 384, 256
CONV_C, CONV_W = 512, 31
POOL_C, POOL_G, POOL_GD = 512, 4, 128
POOL_WINDOWS = (2, 4, 8, 16)
D_FF = 2816
FF_HALF = D_FF // 2
N_LAYERS = 2
EPS = 1e-6
ROPE_THETA = 10000.0
ATT_SCALE = 1.0 / math.sqrt(NOPE + ROPE)
O_Q, O_KV, O_KR, O_CONV, O_POOL, O_GATE, D_IN = 0, 384, 640, 672, 1696, 2208, 5280

LANE = 128
HP = 128
ZG, ZC, ZP, ZA, ZW = 0, 3072, 4096, 4608, 5376
ZA_W = Q_RANK + KV_RANK + HP
KR_LANE = NOPE
HW = N_HEADS * HP

ADAM_LR, ADAM_B1, ADAM_B2, ADAM_EPS, ADAM_WD, ADAM_STEP = 0.001, 0.9, 0.999, 1e-08, 0.01, 10

ROW_TILE = 512
WIDE_ROW_TILE = 256
ATT_TILE_FWD = 1024
ATT_TILE_BWD = 512
ATT_HEADS = 4
CONV_CHUNK = 256
MM_TM, MM_TN, MM_TK = 1024, 1408, 1024
MM_TILE_MAX = 2048
FFN_TM = 512
MM_VMEM_BUDGET = 40 * 1024 * 1024
HBM_BYTES_PER_US = 3.0e6
GRID_STEP_US = 0.35
VMEM_LIMIT = 56 * 1024 * 1024
SUM_TILE_BYTES = 3 * 1024 * 1024
ADAM_TILE_BYTES = 2 * 1024 * 1024

HALF_ALIGN = 16
MESH_AXES = ("x", "y", "c")
PACK_W = 1024
PACK_ROWS = 8


def _cparams(sem):
    return pltpu.CompilerParams(dimension_semantics=sem, vmem_limit_bytes=VMEM_LIMIT)


def _tile(n, target):
    if n <= target:
        return n
    best = None
    for t in range(LANE, target + 1, LANE):
        if n % t == 0:
            best = t
    assert best is not None, (n, target)
    return best


def _mm_tiles(m, n, k, a_bytes, b_bytes, out_bytes):
    divs = lambda d: sorted({t for t in range(LANE, min(d, MM_TILE_MAX) + 1, LANE) if d % t == 0} | ({d} if d <= MM_TILE_MAX else set()))
    best = None
    for tm in divs(m):
        for tn in divs(n):
            blocks = tm * k * a_bytes + k * tn * b_bytes + tm * tn * out_bytes
            if 2 * blocks + tm * tn * 4 > MM_VMEM_BUDGET:
                continue
            steps = (m // tm) * (n // tn)
            for rows_outer in (True, False):
                moved = (m * k * a_bytes + k * n * b_bytes * (m // tm)) if rows_outer else (k * n * b_bytes + m * k * a_bytes * (n // tn))
                cost = (moved + m * n * out_bytes + blocks) / HBM_BYTES_PER_US + steps * GRID_STEP_US
                if best is None or cost < best[0]:
                    best = (cost, tm, tn, rows_outer)
    if best is not None:
        return best[1], best[2], k, best[3]
    return _tile(m, MM_TM), _tile(n, MM_TN), _tile(k, MM_TK), True


def mm(a, b, *, ta=False, tb=False, out_dtype=F32, name, carry=None):
    m, k = (a.shape[1], a.shape[0]) if ta else a.shape
    n, k2 = b.shape if tb else (b.shape[1], b.shape[0])
    assert k == k2, (a.shape, b.shape, ta, tb)
    tm, tn, tk, rows_outer = _mm_tiles(m, n, k, a.dtype.itemsize, b.dtype.itemsize, jnp.dtype(out_dtype).itemsize)
    nk = k // tk
    dims = (((0 if ta else 1,), (1 if tb else 0,)), ((), ()))
    grid = (m // tm, n // tn, nk) if rows_outer else (n // tn, m // tm, nk)

    def body(*refs):
        (a_ref, b_ref, o_ref, *acc), start, finish = _carried(carry, refs, 2, 1, 0 if nk == 1 else 1)
        ids = [pl.program_id(d) for d in range(3)]
        if start is not None:
            pl.when((ids[0] == 0) & (ids[1] == 0) & (ids[2] == 0))(start)
        part = lax.dot_general(a_ref[...].astype(BF16), b_ref[...].astype(BF16), dims, preferred_element_type=F32)
        if nk == 1:
            o_ref[...] = part.astype(o_ref.dtype)
        else:
            (acc_ref,) = acc
            kk = ids[2]

            @pl.when(kk == 0)
            def _():
                acc_ref[...] = part

            @pl.when(kk > 0)
            def _():
                acc_ref[...] += part

            @pl.when(kk == nk - 1)
            def _():
                o_ref[...] = acc_ref[...].astype(o_ref.dtype)
        if finish is not None:
            pl.when((ids[0] == grid[0] - 1) & (ids[1] == grid[1] - 1) & (ids[2] == nk - 1))(finish)

    ij = (lambda g0, g1: (g0, g1)) if rows_outer else (lambda g0, g1: (g1, g0))

    def a_map(g0, g1, kk):
        i, _ = ij(g0, g1)
        return (kk, i) if ta else (i, kk)

    def b_map(g0, g1, kk):
        _, j = ij(g0, g1)
        return (j, kk) if tb else (kk, j)

    ex_in_specs, ex_out_specs, ex_out_shape, ex_scratch, ex_inputs = _carry_specs(carry)
    a_spec = pl.BlockSpec((tk, tm) if ta else (tm, tk), a_map)
    b_spec = pl.BlockSpec((tn, tk) if tb else (tk, tn), b_map)
    out, *carried = pl.pallas_call(
        body,
        grid=grid,
        in_specs=[a_spec, b_spec] + ex_in_specs,
        out_specs=[pl.BlockSpec((tm, tn), lambda g0, g1, kk: ij(g0, g1))] + ex_out_specs,
        out_shape=[jax.ShapeDtypeStruct((m, n), out_dtype)] + ex_out_shape,
        scratch_shapes=([] if nk == 1 else [pltpu.VMEM((tm, tn), F32)]) + ex_scratch,
        compiler_params=_cparams(("arbitrary", "arbitrary", "arbitrary") if carry is not None else ("parallel", "parallel", "arbitrary")),
        name=name,
    )(a, b, *ex_inputs)
    return out if carry is None else (out, carried)


def ffn_in(hn, w_gu, name):
    s_len, k = hn.shape
    tm = min(FFN_TM, s_len)

    def body(a_ref, b_ref, gu_ref, act_ref):
        r = jnp.dot(a_ref[...], b_ref[...], preferred_element_type=F32)
        gu_ref[...] = r.astype(gu_ref.dtype)
        act_ref[...] = (_silu(r[:, :FF_HALF]) * r[:, FF_HALF:]).astype(act_ref.dtype)

    return pl.pallas_call(
        body,
        grid=(2, s_len // tm),
        in_specs=[pl.BlockSpec((tm, k), lambda j, i: (i, 0)), pl.BlockSpec((k, 2 * FF_HALF), lambda j, i: (0, j))],
        out_specs=[pl.BlockSpec((tm, 2 * FF_HALF), lambda j, i: (i, j)), pl.BlockSpec((tm, FF_HALF), lambda j, i: (i, j))],
        out_shape=[jax.ShapeDtypeStruct((s_len, 2 * D_FF), BF16), jax.ShapeDtypeStruct((s_len, D_FF), BF16)],
        compiler_params=_cparams(("arbitrary", "parallel")),
        name=name,
    )(hn, w_gu)


def ffn_out_dx(d_y, w_down, gu, name):
    s_len, k = d_y.shape
    tm = min(FFN_TM, s_len)

    def body(a_ref, b_ref, gu_ref, dgu_ref):
        da = lax.dot_general(a_ref[...], b_ref[...], NT_DIMS, preferred_element_type=F32)
        gt = gu_ref[:, :FF_HALF].astype(F32)
        up = gu_ref[:, FF_HALF:].astype(F32)
        sg = jax.nn.sigmoid(gt)
        dgu_ref[:, :FF_HALF] = (da * up * sg * (1.0 + gt * (1.0 - sg))).astype(dgu_ref.dtype)
        dgu_ref[:, FF_HALF:] = (da * gt * sg).astype(dgu_ref.dtype)

    pair = pl.BlockSpec((tm, 2 * FF_HALF), lambda j, i: (i, j))
    return pl.pallas_call(
        body,
        grid=(2, s_len // tm),
        in_specs=[pl.BlockSpec((tm, k), lambda j, i: (i, 0)), pl.BlockSpec((FF_HALF, k), lambda j, i: (j, 0)), pair],
        out_specs=pair,
        out_shape=jax.ShapeDtypeStruct((s_len, 2 * D_FF), BF16),
        compiler_params=_cparams(("parallel", "arbitrary")),
        name=name,
    )(d_y, w_down, gu)


def rowwise(name, body, rows, row_ins, full_ins, row_outs, acc_outs=(), into=None, tile=None):
    tile = min(tile or ROW_TILE, rows)
    into = into or {}
    in_specs = [pl.BlockSpec((tile, w), lambda i, cb=cb: (i, cb)) for _, w, cb in row_ins]
    in_specs += [pl.BlockSpec(a.shape, lambda i, nd=a.ndim: (0,) * nd) for a in full_ins]
    in_specs += [ANY for _ in into]
    n_in = len(row_ins) + len(full_ins)
    aliases = {n_in + k: oi for k, oi in enumerate(into)}
    out_specs, out_shape = [], []
    for ro in row_outs:
        w, dt, full_w, cb = ro if len(ro) == 4 else (*ro, ro[0], 0)
        out_specs.append(pl.BlockSpec((tile, w), lambda i, cb=cb: (i, cb)))
        out_shape.append(jax.ShapeDtypeStruct((rows, full_w), dt))
    out_specs += [pl.BlockSpec(s, lambda i, nd=len(s): (0,) * nd) for s, _ in acc_outs]
    out_shape += [jax.ShapeDtypeStruct(s, dt) for s, dt in acc_outs]
    n_refs = n_in

    def call_body(*refs):
        body(*refs[:n_refs], *refs[n_refs + len(into):])

    outs = pl.pallas_call(
        call_body,
        grid=(rows // tile,),
        in_specs=in_specs,
        out_specs=out_specs,
        out_shape=out_shape,
        input_output_aliases=aliases,
        compiler_params=_cparams(("arbitrary",)),
        name=name,
    )(*[a for a, _, _ in row_ins], *full_ins, *into.values())
    return outs


def _whole(a):
    return (a, a.shape[1], 0)


def _acc(ref, val):
    @pl.when(pl.program_id(0) == 0)
    def _():
        ref[...] = val

    @pl.when(pl.program_id(0) > 0)
    def _():
        ref[...] += val


def _rms(x, g):
    return x * lax.rsqrt(jnp.mean(x * x, axis=-1, keepdims=True) + EPS) * g


def _layer_norm(x, g, b):
    mu = jnp.mean(x, axis=-1, keepdims=True)
    xc = x - mu
    return xc * lax.rsqrt(jnp.mean(xc * xc, axis=-1, keepdims=True) + EPS) * g + b


def _silu(x):
    return x * jax.nn.sigmoid(x)


def _rope(x, cc, sa, sb):
    return x * cc + pltpu.roll(x, HALF_ROPE, 1) * sa + pltpu.roll(x, HP - HALF_ROPE, 1) * sb


def _rope_t(dy, cc, sa, sb):
    return dy * cc + pltpu.roll(dy * sa, HP - HALF_ROPE, 1) + pltpu.roll(dy * sb, HALF_ROPE, 1)


def rope_tables(pos_col, rows):
    lane = np.arange(HP)
    idx = np.where(lane < KR_LANE + HALF_ROPE, lane - KR_LANE, lane - KR_LANE - HALF_ROPE)
    in_rope = (lane >= KR_LANE) & (lane < KR_LANE + ROPE)
    inv_freq = (np.float32(ROPE_THETA) ** (-np.arange(0, ROPE, 2, dtype=np.float32) / np.float32(ROPE))).astype(np.float32)
    freq_row = np.where(in_rope, inv_freq[np.clip(idx, 0, HALF_ROPE - 1)], 0.0).astype(np.float32)[None, :]
    first = ((lane >= KR_LANE) & (lane < KR_LANE + HALF_ROPE)).astype(np.float32)[None, :]
    second = ((lane >= KR_LANE + HALF_ROPE) & (lane < KR_LANE + ROPE)).astype(np.float32)[None, :]

    def body(pos_ref, f_ref, a_ref, b_ref, cc_ref, sa_ref, sb_ref):
        ang = pos_ref[...].astype(F32) * f_ref[...]
        s = jnp.sin(ang)
        cc_ref[...] = jnp.cos(ang)
        sa_ref[...] = s * b_ref[...]
        sb_ref[...] = -s * a_ref[...]

    return rowwise("rope_tables", body, rows, [_whole(pos_col)], [jnp.asarray(freq_row), jnp.asarray(first), jnp.asarray(second)],
                   [(HP, F32)] * 3)


def _causal_mask(t):
    r = lax.broadcasted_iota(jnp.int32, (t, t), 0)
    c = lax.broadcasted_iota(jnp.int32, (t, t), 1)
    return r, c


NT_DIMS = (((1,), (1,)), ((), ()))


def _carried(carry, refs, n_in, n_out, n_scratch):
    if carry is None:
        return refs, None, None
    ni, no = len(carry.ins), len(carry.outs)
    own_in, ex_in = refs[:n_in], refs[n_in:n_in + ni]
    own_out, ex_out = refs[n_in + ni:n_in + ni + n_out], refs[n_in + ni + n_out:n_in + ni + n_out + no]
    scratch = refs[n_in + ni + n_out + no:]
    sems = scratch[n_scratch:]
    return (*own_in, *own_out, *scratch[:n_scratch]), (lambda: carry.start(ex_in, ex_out, *sems)), (lambda: carry.finish(ex_in, ex_out, *sems))


def _carry_specs(carry):
    if carry is None:
        return [], [], [], [], []
    sems = [pltpu.SemaphoreType.DMA((carry.n_sems,)), pltpu.SemaphoreType.DMA((carry.n_sems,))]
    return [ANY] * len(carry.ins), [ANY] * len(carry.outs), list(carry.outs), sems, list(carry.ins)


def attention_fwd(q, k, v, name, carry=None):
    s_len = q.shape[0]
    t = min(ATT_TILE_FWD, s_len)
    nb = s_len // t
    hb = ATT_HEADS
    w = hb * HP
    nh = N_HEADS // hb

    def body(*refs):
        (q_ref, k_ref, v_ref, o_ref, lse_ref, m_sc, acc_sc), start, finish = _carried(carry, refs, 3, 2, 2)
        qi = pl.program_id(1)
        if start is not None:
            pl.when((pl.program_id(0) == 0) & (qi == 0))(start)
        m_sc[...] = jnp.full_like(m_sc, -jnp.inf)
        acc_sc[...] = jnp.zeros_like(acc_sc)

        def block(j, masked):
            ks = pl.ds(pl.multiple_of(j * t, t), t)
            for hh in range(hb):
                ls = slice(hh * HP, (hh + 1) * HP)
                s = lax.dot_general(q_ref[:, ls], k_ref[ks, ls], NT_DIMS, preferred_element_type=F32) * ATT_SCALE
                if masked:
                    r, c = _causal_mask(t)
                    s = jnp.where(c <= r, s, -jnp.inf)
                m_old = m_sc[hh]
                m_new = jnp.maximum(m_old, jnp.max(s, axis=-1, keepdims=True))
                p = jnp.exp(s - m_new)
                acc_sc[hh] = jnp.exp(m_old - m_new) * acc_sc[hh] + jnp.dot(p.astype(BF16), v_ref[ks, ls], preferred_element_type=F32)
                m_sc[hh] = m_new

        def loop_body(j, carry):
            block(j, False)
            return carry

        lax.fori_loop(0, qi, loop_body, 0)
        block(qi, True)
        lane = lax.broadcasted_iota(jnp.int32, (t, HP), 1)
        for hh in range(hb):
            acc = acc_sc[hh]
            l = jnp.sum(jnp.where(lane == VDIM, acc, 0.0), axis=-1, keepdims=True)
            o_ref[:, hh * HP:(hh + 1) * HP] = jnp.where(lane < VDIM, acc / l, 0.0).astype(o_ref.dtype)
            lse_ref[hh] = m_sc[hh] + jnp.log(l)
        if finish is not None:
            pl.when((pl.program_id(0) == nh - 1) & (qi == nb - 1))(finish)

    ex_in_specs, ex_out_specs, ex_out_shape, ex_scratch, ex_inputs = _carry_specs(carry)
    resident = pl.BlockSpec((s_len, w), lambda h, qi: (0, h))
    o, lse, *carried = pl.pallas_call(
        body,
        grid=(nh, nb),
        in_specs=[pl.BlockSpec((t, w), lambda h, qi: (qi, h)), resident, resident] + ex_in_specs,
        out_specs=[pl.BlockSpec((t, w), lambda h, qi: (qi, h)), pl.BlockSpec((hb, t, 1), lambda h, qi: (h, qi, 0))] + ex_out_specs,
        out_shape=[jax.ShapeDtypeStruct((s_len, HW), BF16), jax.ShapeDtypeStruct((N_HEADS, s_len, 1), F32)] + ex_out_shape,
        scratch_shapes=[pltpu.VMEM((hb, t, 1), F32), pltpu.VMEM((hb, t, HP), F32)] + ex_scratch,
        compiler_params=_cparams(("arbitrary", "arbitrary")),
        name=name,
    )(q, k, v, *ex_inputs)
    return o, lse, carried


def attention_delta(do, o):
    s_len = do.shape[0]
    t = min(ROW_TILE, s_len)

    def body(do_ref, o_ref, d_ref):
        prod = do_ref[...].astype(F32) * o_ref[...].astype(F32)
        for h in range(N_HEADS):
            d_ref[h] = jnp.sum(prod[:, h * HP:(h + 1) * HP], axis=-1, keepdims=True)

    return pl.pallas_call(
        body,
        grid=(s_len // t,),
        in_specs=[pl.BlockSpec((t, HW), lambda i: (i, 0))] * 2,
        out_specs=pl.BlockSpec((N_HEADS, t, 1), lambda i: (0, i, 0)),
        out_shape=jax.ShapeDtypeStruct((N_HEADS, s_len, 1), F32),
        compiler_params=_cparams(("arbitrary",)),
        name="attention_delta",
    )(do, o)


TN_DIMS = (((0,), (0,)), ((), ()))


def attention_bwd(q, k, v, do, lse_row, delta_row, name, carry=None):
    s_len = q.shape[0]
    t = min(ATT_TILE_BWD, s_len)
    nb = s_len // t
    hb = ATT_HEADS
    w = hb * HP
    nh = N_HEADS // hb

    def body(*refs):
        (q_ref, k_ref, v_ref, do_ref, lse_ref, dl_ref, dq_ref, dk_ref, dv_ref, dk_sc, dv_sc), start, finish = _carried(carry, refs, 6, 3, 2)
        ki = pl.program_id(1)
        if start is not None:
            pl.when((pl.program_id(0) == 0) & (ki == 0))(start)

        @pl.when(ki == 0)
        def _():
            dq_ref[...] = jnp.zeros_like(dq_ref)

        dk_sc[...] = jnp.zeros_like(dk_sc)
        dv_sc[...] = jnp.zeros_like(dv_sc)

        def block(j, masked):
            qs = pl.ds(pl.multiple_of(j * t, t), t)
            for hh in range(hb):
                ls = slice(hh * HP, (hh + 1) * HP)
                qb = q_ref[qs, ls]
                dob = do_ref[qs, ls]
                kb = k_ref[:, ls]
                st = lax.dot_general(kb, qb, NT_DIMS, preferred_element_type=F32) * ATT_SCALE
                pt = jnp.exp(st - lse_ref[hh, j])
                if masked:
                    r, c = _causal_mask(t)
                    pt = jnp.where(r <= c, pt, 0.0)
                dv_sc[hh] += jnp.dot(pt.astype(BF16), dob, preferred_element_type=F32)
                dpt = lax.dot_general(v_ref[:, ls], dob, NT_DIMS, preferred_element_type=F32)
                dst = (pt * (dpt - dl_ref[hh, j]) * ATT_SCALE).astype(BF16)
                dk_sc[hh] += jnp.dot(dst, qb, preferred_element_type=F32)
                dq_ref[qs, ls] += lax.dot_general(dst, kb, TN_DIMS, preferred_element_type=F32)

        block(ki, True)

        def loop_body(j, carry):
            block(j, False)
            return carry

        lax.fori_loop(ki + 1, nb, loop_body, 0)
        for hh in range(hb):
            ls = slice(hh * HP, (hh + 1) * HP)
            dk_ref[:, ls] = dk_sc[hh].astype(dk_ref.dtype)
            dv_ref[:, ls] = dv_sc[hh].astype(dv_ref.dtype)
        if finish is not None:
            pl.when((pl.program_id(0) == nh - 1) & (ki == nb - 1))(finish)

    ex_in_specs, ex_out_specs, ex_out_shape, ex_scratch, ex_inputs = _carry_specs(carry)
    k_spec = pl.BlockSpec((t, w), lambda h, ki: (ki, h))
    resident = pl.BlockSpec((s_len, w), lambda h, ki: (0, h))
    row_spec = pl.BlockSpec((hb, nb, 1, t), lambda h, ki: (h, 0, 0, 0))
    dq, dk, dv, *carried = pl.pallas_call(
        body,
        grid=(nh, nb),
        in_specs=[resident, k_spec, k_spec, resident, row_spec, row_spec] + ex_in_specs,
        out_specs=[resident, k_spec, k_spec] + ex_out_specs,
        out_shape=[jax.ShapeDtypeStruct((s_len, HW), F32), jax.ShapeDtypeStruct((s_len, HW), BF16), jax.ShapeDtypeStruct((s_len, HW), BF16)]
        + ex_out_shape,
        scratch_shapes=[pltpu.VMEM((hb, t, HP), F32), pltpu.VMEM((hb, t, HP), F32)] + ex_scratch,
        compiler_params=_cparams(("arbitrary", "arbitrary")),
        name=name,
    )(q, k, v, do, lse_row, delta_row, *ex_inputs)
    return dq, dk, dv, carried


CONV_PAD = 32


def conv_fwd(z, conv_w, conv_b):
    s_len = z.shape[0]
    ch = min(CONV_CHUNK, s_len)

    def body(ag_ref, w_ref, b_ref, c_ref, pad_ref):
        pad_ref[0:CONV_PAD, :] = jnp.zeros((CONV_PAD, LANE), F32)
        pad_ref[CONV_PAD:CONV_PAD + s_len, :] = ag_ref[:, 0:LANE].astype(F32) * jax.nn.sigmoid(ag_ref[:, LANE:2 * LANE].astype(F32))

        def chunk(i, carry):
            base = pl.multiple_of(i * ch, ch)
            acc = jnp.zeros((ch, LANE), F32) + b_ref[...]
            for kk in range(CONV_W):
                acc = acc + pad_ref[pl.ds(base + CONV_PAD - (CONV_W - 1) + kk, ch), :] * w_ref[kk:kk + 1, :]
            c_ref[pl.ds(base, ch), :] = acc
            return carry

        lax.fori_loop(0, s_len // ch, chunk, 0)

    nblk = CONV_C // LANE
    return pl.pallas_call(
        body,
        grid=(nblk,),
        in_specs=[pl.BlockSpec((s_len, 2 * LANE), lambda j: (0, ZC // (2 * LANE) + j)),
                  pl.BlockSpec((CONV_W, LANE), lambda j: (0, j)), pl.BlockSpec((1, LANE), lambda j: (0, j))],
        out_specs=pl.BlockSpec((s_len, LANE), lambda j: (0, j)),
        out_shape=jax.ShapeDtypeStruct((s_len, CONV_C), F32),
        scratch_shapes=[pltpu.VMEM((s_len + CONV_PAD, LANE), F32)],
        compiler_params=_cparams(("arbitrary",)),
        name="conv_fwd",
    )(z, conv_w, conv_b)


def conv_bwd(z, dc, conv_w, dz):
    s_len = z.shape[0]
    ch = min(CONV_CHUNK, s_len)

    def body(ag_ref, dc_ref, w_ref, dz_in, dag_ref, dw_ref, db_ref, pad_ref, dpad_ref, wacc_ref):
        del dz_in
        pad_ref[0:CONV_PAD, :] = jnp.zeros((CONV_PAD, LANE), F32)
        pad_ref[CONV_PAD:CONV_PAD + s_len, :] = ag_ref[:, 0:LANE].astype(F32) * jax.nn.sigmoid(ag_ref[:, LANE:2 * LANE].astype(F32))
        dpad_ref[0:s_len, :] = dc_ref[...]
        dpad_ref[s_len:s_len + CONV_PAD, :] = jnp.zeros((CONV_PAD, LANE), F32)
        wacc_ref[...] = jnp.zeros_like(wacc_ref)
        db_ref[...] = jnp.sum(dc_ref[...], axis=0, keepdims=True)

        def chunk(i, carry):
            base = pl.multiple_of(i * ch, ch)
            dcc = dpad_ref[pl.ds(base, ch), :]
            dh = jnp.zeros((ch, LANE), F32)
            for kk in range(CONV_W):
                dh = dh + dpad_ref[pl.ds(base + (CONV_W - 1) - kk, ch), :] * w_ref[kk:kk + 1, :]
                prod = dcc * pad_ref[pl.ds(base + CONV_PAD - (CONV_W - 1) + kk, ch), :]
                wacc_ref[kk * 8:(kk + 1) * 8, :] += prod.reshape(ch // 8, 8, LANE).sum(axis=0)
            a = ag_ref[pl.ds(base, ch), 0:LANE].astype(F32)
            sgc = jax.nn.sigmoid(ag_ref[pl.ds(base, ch), LANE:2 * LANE].astype(F32))
            dag_ref[pl.ds(base, ch), 0:LANE] = (dh * sgc).astype(dag_ref.dtype)
            dag_ref[pl.ds(base, ch), LANE:2 * LANE] = (dh * a * sgc * (1.0 - sgc)).astype(dag_ref.dtype)
            return carry

        lax.fori_loop(0, s_len // ch, chunk, 0)
        for kk in range(CONV_W):
            dw_ref[kk:kk + 1, :] = jnp.sum(wacc_ref[kk * 8:(kk + 1) * 8, :], axis=0, keepdims=True)

    nblk = CONV_C // LANE
    pair = pl.BlockSpec((s_len, 2 * LANE), lambda j: (0, ZC // (2 * LANE) + j))
    return pl.pallas_call(
        body,
        grid=(nblk,),
        in_specs=[pair, pl.BlockSpec((s_len, LANE), lambda j: (0, j)), pl.BlockSpec((CONV_W, LANE), lambda j: (0, j)), ANY],
        out_specs=[pair, pl.BlockSpec((CONV_W, LANE), lambda j: (0, j)), pl.BlockSpec((1, LANE), lambda j: (0, j))],
        out_shape=[jax.ShapeDtypeStruct(dz.shape, dz.dtype), jax.ShapeDtypeStruct((CONV_W, CONV_C), F32), jax.ShapeDtypeStruct((1, CONV_C), F32)],
        scratch_shapes=[pltpu.VMEM((s_len + CONV_PAD, LANE), F32), pltpu.VMEM((s_len + CONV_PAD, LANE), F32),
                        pltpu.VMEM((CONV_W * 8, LANE), F32)],
        input_output_aliases={3: 0},
        compiler_params=_cparams(("arbitrary",)),
        name="conv_bwd",
    )(z, dc, conv_w, dz)


POOL_PAD = 16


def _pool_count(base, ch, w):
    t = base + lax.broadcasted_iota(jnp.int32, (ch, 1), 0)
    return jnp.minimum(t + 1, w).astype(F32)


def pool_fwd(z, pool_w, pool_scale):
    s_len = z.shape[0]
    ch = min(CONV_CHUNK, s_len)

    def body(u_ref, pw_ref, sc_ref, m_ref, pad_ref):
        gi = pl.program_id(0)
        pad_ref[0:POOL_PAD, :] = jnp.zeros((POOL_PAD, LANE), F32)
        pad_ref[POOL_PAD:POOL_PAD + s_len, :] = u_ref[...].astype(F32)

        def run(w):
            def chunk(i, carry):
                base = pl.multiple_of(i * ch, ch)
                acc = jnp.zeros((ch, LANE), F32)
                for j in range(w):
                    acc = acc + pad_ref[pl.ds(base + POOL_PAD - j, ch), :]
                d = acc / _pool_count(base, ch, w) - pad_ref[pl.ds(base + POOL_PAD, ch), :]
                md = jnp.dot(d.astype(BF16), pw_ref[0], preferred_element_type=F32)
                m_ref[pl.ds(base, ch), :] = (md * sc_ref[...]).astype(m_ref.dtype)
                return carry

            lax.fori_loop(0, s_len // ch, chunk, 0)

        for g, w in enumerate(POOL_WINDOWS):
            pl.when(gi == g)(functools.partial(run, w))

    return pl.pallas_call(
        body,
        grid=(POOL_G,),
        in_specs=[pl.BlockSpec((s_len, LANE), lambda g: (0, ZP // LANE + g)), pl.BlockSpec((1, POOL_GD, POOL_GD), lambda g: (g, 0, 0)),
                  pl.BlockSpec((1, LANE), lambda g: (0, g))],
        out_specs=pl.BlockSpec((s_len, LANE), lambda g: (0, g)),
        out_shape=jax.ShapeDtypeStruct((s_len, POOL_C), BF16),
        scratch_shapes=[pltpu.VMEM((s_len + POOL_PAD, LANE), F32)],
        compiler_params=_cparams(("arbitrary",)),
        name="pool_fwd",
    )(z, pool_w, pool_scale)


def pool_bwd(z, dm, pool_w, pool_scale, dz):
    s_len = z.shape[0]
    ch = min(CONV_CHUNK, s_len)

    def body(u_ref, dm_ref, pw_ref, sc_ref, dz_in, du_ref, dpw_ref, dsc_ref, pad_ref, epad_ref, dd_ref, sacc_ref):
        del dz_in
        gi = pl.program_id(0)
        pad_ref[0:POOL_PAD, :] = jnp.zeros((POOL_PAD, LANE), F32)
        pad_ref[POOL_PAD:POOL_PAD + s_len, :] = u_ref[...].astype(F32)
        epad_ref[s_len:s_len + POOL_PAD, :] = jnp.zeros((POOL_PAD, LANE), F32)
        dpw_ref[...] = jnp.zeros_like(dpw_ref)
        sacc_ref[...] = jnp.zeros_like(sacc_ref)

        def run(w):
            def first(i, carry):
                base = pl.multiple_of(i * ch, ch)
                acc = jnp.zeros((ch, LANE), F32)
                for j in range(w):
                    acc = acc + pad_ref[pl.ds(base + POOL_PAD - j, ch), :]
                cnt = _pool_count(base, ch, w)
                d = (acc / cnt - pad_ref[pl.ds(base + POOL_PAD, ch), :]).astype(BF16)
                md = jnp.dot(d, pw_ref[0], preferred_element_type=F32)
                dmc = dm_ref[pl.ds(base, ch), :].astype(F32)
                sacc_ref[...] += (dmc * md).reshape(ch // 8, 8, LANE).sum(axis=0)
                dmd = (dmc * sc_ref[...]).astype(BF16)
                dpw_ref[0] += lax.dot_general(d, dmd, (((0,), (0,)), ((), ())), preferred_element_type=F32)
                dd = lax.dot_general(dmd, pw_ref[0], (((1,), (1,)), ((), ())), preferred_element_type=F32)
                dd_ref[pl.ds(base, ch), :] = dd
                epad_ref[pl.ds(base, ch), :] = dd / cnt
                return carry

            lax.fori_loop(0, s_len // ch, first, 0)

            def second(i, carry):
                base = pl.multiple_of(i * ch, ch)
                acc = jnp.zeros((ch, LANE), F32)
                for j in range(w):
                    acc = acc + epad_ref[pl.ds(base + j, ch), :]
                du_ref[pl.ds(base, ch), :] = (acc - dd_ref[pl.ds(base, ch), :]).astype(du_ref.dtype)
                return carry

            lax.fori_loop(0, s_len // ch, second, 0)

        for g, w in enumerate(POOL_WINDOWS):
            pl.when(gi == g)(functools.partial(run, w))
        dsc_ref[...] = jnp.sum(sacc_ref[...], axis=0, keepdims=True)

    return pl.pallas_call(
        body,
        grid=(POOL_G,),
        in_specs=[pl.BlockSpec((s_len, LANE), lambda g: (0, ZP // LANE + g)), pl.BlockSpec((s_len, LANE), lambda g: (0, g)),
                  pl.BlockSpec((1, POOL_GD, POOL_GD), lambda g: (g, 0, 0)), pl.BlockSpec((1, LANE), lambda g: (0, g)), ANY],
        out_specs=[pl.BlockSpec((s_len, LANE), lambda g: (0, ZP // LANE + g)), pl.BlockSpec((1, POOL_GD, POOL_GD), lambda g: (g, 0, 0)),
                   pl.BlockSpec((1, LANE), lambda g: (0, g))],
        out_shape=[jax.ShapeDtypeStruct(dz.shape, dz.dtype), jax.ShapeDtypeStruct((POOL_G, POOL_GD, POOL_GD), F32),
                   jax.ShapeDtypeStruct((1, POOL_C), F32)],
        scratch_shapes=[pltpu.VMEM((s_len + POOL_PAD, LANE), F32), pltpu.VMEM((s_len + POOL_PAD, LANE), F32),
                        pltpu.VMEM((s_len, LANE), F32), pltpu.VMEM((8, LANE), F32)],
        input_output_aliases={4: 0},
        compiler_params=_cparams(("arbitrary",)),
        name="pool_bwd",
    )(z, dm, pool_w, pool_scale, dz)


def _row(v):
    return v.reshape(1, -1)


def _rms_body(x_ref, g_ref, o_ref):
    o_ref[...] = _rms(x_ref[...], g_ref[...]).astype(o_ref.dtype)


def _post_body(y_ref, x_ref, g_ref, o_ref):
    o_ref[...] = x_ref[...] + _rms(y_ref[...], g_ref[...])


def _post_bwd_body(y_ref, dh_ref, g_ref, dy_ref, dg_ref):
    _, vjp = jax.vjp(_rms, y_ref[...], g_ref[...])
    dy, dg = vjp(dh_ref[...])
    dy_ref[...] = dy.astype(dy_ref.dtype)
    _acc(dg_ref, dg)


def _pre_bwd_body(x_ref, dhn_ref, dres_ref, g_ref, dx_ref, dg_ref):
    _, vjp = jax.vjp(_rms, x_ref[...], g_ref[...])
    dx, dg = vjp(dhn_ref[...])
    dx_ref[...] = dres_ref[...] + dx
    _acc(dg_ref, dg)


def _post_next_body(y_ref, x_ref, g_ref, gn_ref, o_ref, n_ref):
    o = x_ref[...] + _rms(y_ref[...], g_ref[...])
    o_ref[...] = o
    n_ref[...] = _rms(o, gn_ref[...]).astype(n_ref.dtype)


def _pre_post_bwd_body(x_ref, dhn_ref, dres_ref, y_ref, g_ref, gy_ref, dx_ref, dy_ref, dg_ref, dgy_ref):
    _, vjp = jax.vjp(_rms, x_ref[...], g_ref[...])
    dx, dg = vjp(dhn_ref[...])
    dx = dres_ref[...] + dx
    dx_ref[...] = dx
    _, vjp_y = jax.vjp(_rms, y_ref[...], gy_ref[...])
    dy, dgy = vjp_y(dx)
    dy_ref[...] = dy.astype(dy_ref.dtype)
    _acc(dg_ref, dg)
    _acc(dgy_ref, dgy)


def mixer_fwd(x, tabs, w, tag, carry=None, head=None, h=None):
    s_len = x.shape[0]
    cc, sa, sb = tabs
    sv = {"x": x}

    if h is None:
        (h,) = rowwise("mix_norm_pre" + tag, _rms_body, s_len, [_whole(x)], [_row(w["mix_norm_pre"])], [(D_MODEL, BF16)])
    if head is None:
        z = mm(h, w["w_in"], out_dtype=BF16, name="in_proj" + tag)
    else:
        z, arrived = mm(h, w["w_in"], out_dtype=BF16, name="in_proj" + tag, carry=head[0])
        w = {**w, **head[1](arrived)}

    def prep_body(z_ref, cc_ref, sa_ref, sb_ref, qg_ref, kg_ref, qn_ref, ckv_ref, kr_ref):
        qn_ref[...] = _rms(z_ref[:, 0:Q_RANK].astype(F32), qg_ref[...]).astype(qn_ref.dtype)
        ckv_ref[...] = _rms(z_ref[:, Q_RANK:Q_RANK + KV_RANK].astype(F32), kg_ref[...]).astype(ckv_ref.dtype)
        kr_ref[...] = _rope(z_ref[:, Q_RANK + KV_RANK:ZA_W].astype(F32), cc_ref[...], sa_ref[...], sb_ref[...])

    qn, ckvn, kr = rowwise("attn_prep" + tag, prep_body, s_len, [(z, ZA_W, ZA // ZA_W), _whole(cc), _whole(sa), _whole(sb)],
                           [_row(w["q_norm"]), _row(w["kv_norm"])], [(Q_RANK, BF16), (KV_RANK, BF16), (HP, F32)])
    q_raw = mm(qn, w["w_uq"], out_dtype=BF16, name="q_proj" + tag)
    kv_raw = mm(ckvn, w["w_ukv"], out_dtype=BF16, name="kv_proj" + tag)

    def qkv_body(q_ref, kv_ref, kr_ref, cc_ref, sa_ref, sb_ref, qo_ref, ko_ref, vo_ref):
        c_, a_, b_, kro = cc_ref[...], sa_ref[...], sb_ref[...], kr_ref[...]
        for hh in range(N_HEADS):
            sl = slice(hh * HP, (hh + 1) * HP)
            qo_ref[:, sl] = _rope(q_ref[:, sl].astype(F32), c_, a_, b_).astype(qo_ref.dtype)
            ko_ref[:, sl] = (kv_ref[:, sl].astype(F32) + kro).astype(ko_ref.dtype)
        lane = lax.broadcasted_iota(jnp.int32, (q_ref.shape[0], HW), 1)
        vo_ref[...] = jnp.where((lane & (HP - 1)) == VDIM, 1.0, kv_ref[:, HW:2 * HW].astype(F32)).astype(vo_ref.dtype)

    q, k, v = rowwise("qkv_rope" + tag, qkv_body, s_len, [_whole(q_raw), _whole(kv_raw), _whole(kr), _whole(cc), _whole(sa), _whole(sb)], [],
                      [(HW, BF16)] * 3)
    o, lse, carried = attention_fwd(q, k, v, "attention_fwd" + tag, carry)
    y_attn = mm(o, w["w_attn_o"], out_dtype=BF16, name="attn_out" + tag)

    c = conv_fwd(z, w["conv_w"], _row(w["conv_b"]))

    def ln_body(c_ref, g_ref, b_ref, o_ref):
        o_ref[...] = _silu(_layer_norm(c_ref[...], g_ref[...], b_ref[...])).astype(o_ref.dtype)

    (cs,) = rowwise("conv_ln_silu" + tag, ln_body, s_len, [_whole(c)], [_row(w["conv_ln_g"]), _row(w["conv_ln_b"])], [(CONV_C, BF16)])
    y_conv = mm(cs, w["w_conv_o"], out_dtype=BF16, name="conv_out" + tag)

    m = pool_fwd(z, w["pool_w"], _row(w["pool_scale"]))
    y_pool = mm(m, w["w_pool_o"], out_dtype=BF16, name="pool_out" + tag)

    def merge_body(ya_ref, yc_ref, yp_ref, gl_ref, o_ref):
        gl = gl_ref[...].astype(F32)
        o_ref[...] = (jax.nn.sigmoid(gl[:, 0:D_MODEL]) * ya_ref[...].astype(F32) + jax.nn.sigmoid(gl[:, D_MODEL:2 * D_MODEL]) * yc_ref[...].astype(F32)
                      + jax.nn.sigmoid(gl[:, 2 * D_MODEL:3 * D_MODEL]) * yp_ref[...].astype(F32)).astype(o_ref.dtype)

    (merged,) = rowwise("gate_merge" + tag, merge_body, s_len, [_whole(y_attn), _whole(y_conv), _whole(y_pool), (z, 3 * D_MODEL, 0)], [],
                        [(D_MODEL, BF16)])
    mo = mm(merged, w["w_mix_o"], name="mix_out" + tag)

    h1, hn = rowwise("mix_norm_post" + tag, _post_next_body, s_len, [_whole(mo), _whole(x)],
                     [_row(w["mix_norm_post"]), _row(w["ffn_norm_pre"])], [(D_MODEL, F32), (D_MODEL, BF16)])
    sv.update(h=h, z=z, qn=qn, ckvn=ckvn, q=q, k=k, v=v, o=o, lse=lse, c=c, cs=cs, m=m, y_attn=y_attn, y_conv=y_conv, y_pool=y_pool,
              merged=merged, mo=mo, w=w, hn=hn)
    return h1, sv, carried


def ffn_fwd(h1, hn, w, tag, next_gain=None):
    s_len = h1.shape[0]
    gu, act = ffn_in(hn, w["w_gu"], "ffn_in" + tag)
    y = mm(act, w["w_down"], name="ffn_out" + tag)
    if next_gain is None:
        (h2,) = rowwise("ffn_norm_post" + tag, _post_body, s_len, [_whole(y), _whole(h1)], [_row(w["ffn_norm_post"])], [(D_MODEL, F32)])
        h_next = None
    else:
        h2, h_next = rowwise("ffn_norm_post" + tag, _post_next_body, s_len, [_whole(y), _whole(h1)],
                             [_row(w["ffn_norm_post"]), _row(next_gain)], [(D_MODEL, F32), (D_MODEL, BF16)])
    return h2, dict(h1=h1, hn=hn, gu=gu, act=act, y=y), h_next


def ffn_bwd(dh2, sv, w, tag, mo, post=None, first=None):
    s_len = dh2.shape[0]
    g = {}
    if post is None:
        d_y, g["ffn_norm_post"] = rowwise("ffn_norm_post_bwd" + tag, _post_bwd_body, s_len, [_whole(sv["y"]), _whole(dh2)],
                                          [_row(w["ffn_norm_post"])], [(D_MODEL, BF16)], [((1, D_MODEL), F32)])
    else:
        d_y, g["ffn_norm_post"] = post
    swapped = None
    if first is None:
        g["w_down"] = mm(sv["act"], d_y, ta=True, name="ffn_out_dw" + tag)
    else:
        g["w_down"], swapped = mm(sv["act"], d_y, ta=True, name="ffn_out_dw" + tag, carry=first)
    d_gu = ffn_out_dx(d_y, w["w_down"], sv["gu"], "ffn_out_dx" + tag)
    g["w_gu"] = mm(sv["hn"], d_gu, ta=True, name="ffn_in_dw" + tag)
    d_hn = mm(d_gu, w["w_gu"], tb=True, name="ffn_in_dx" + tag)
    dh1, d_mo, g["ffn_norm_pre"], d_mix_post = rowwise(
        "ffn_norm_pre_bwd" + tag, _pre_post_bwd_body, s_len, [_whole(sv["h1"]), _whole(d_hn), _whole(dh2), _whole(mo)],
        [_row(w["ffn_norm_pre"]), _row(w["mix_norm_post"])], [(D_MODEL, F32), (D_MODEL, BF16)], [((1, D_MODEL), F32), ((1, D_MODEL), F32)])
    return dh1, g, (d_mo, d_mix_post), swapped


def mixer_bwd(dh1, sv, tabs, w, tag, post, carry=None, tail=None, prev=None, first=None):
    s_len = dh1.shape[0]
    cc, sa, sb = tabs
    g = {}

    d_mo, g["mix_norm_post"] = post
    if first is None:
        g["w_mix_o"] = mm(sv["merged"], d_mo, ta=True, name="mix_out_dw" + tag)
    else:
        g["w_mix_o"], swapped = mm(sv["merged"], d_mo, ta=True, name="mix_out_dw" + tag, carry=first)
        carry = carry(swapped)
    d_merged = mm(d_mo, w["w_mix_o"], tb=True, out_dtype=BF16, name="mix_out_dx" + tag)

    def merge_bwd_body(dm_ref, ya_ref, yc_ref, yp_ref, gl_ref, dya_ref, dyc_ref, dyp_ref, dgl_ref):
        dmg = dm_ref[...].astype(F32)
        for i, (y_ref, dy_ref) in enumerate(((ya_ref, dya_ref), (yc_ref, dyc_ref), (yp_ref, dyp_ref))):
            sg = jax.nn.sigmoid(gl_ref[:, i * D_MODEL:(i + 1) * D_MODEL].astype(F32))
            dy_ref[...] = (dmg * sg).astype(dy_ref.dtype)
            dgl_ref[:, i * D_MODEL:(i + 1) * D_MODEL] = (dmg * y_ref[...].astype(F32) * sg * (1.0 - sg)).astype(dgl_ref.dtype)

    d_ya, d_yc, d_yp, dz = rowwise(
        "gate_merge_bwd" + tag, merge_bwd_body, s_len,
        [_whole(d_merged), _whole(sv["y_attn"]), _whole(sv["y_conv"]), _whole(sv["y_pool"]), (sv["z"], 3 * D_MODEL, 0)], [],
        [(D_MODEL, BF16)] * 3 + [(3 * D_MODEL, BF16, ZW, 0)], tile=WIDE_ROW_TILE)

    g["w_pool_o"] = mm(sv["m"], d_yp, ta=True, name="pool_out_dw" + tag)
    d_m = mm(d_yp, w["w_pool_o"], tb=True, out_dtype=BF16, name="pool_out_dx" + tag)
    dz, g["pool_w"], g["pool_scale"] = pool_bwd(sv["z"], d_m, w["pool_w"], _row(w["pool_scale"]), dz)

    g["w_conv_o"] = mm(sv["cs"], d_yc, ta=True, name="conv_out_dw" + tag)
    d_cs = mm(d_yc, w["w_conv_o"], tb=True, out_dtype=BF16, name="conv_out_dx" + tag)

    def ln_bwd_body(c_ref, dcs_ref, g_ref, b_ref, dc_ref, dg_ref, db_ref):
        f = lambda c_, g_, b_: _silu(_layer_norm(c_, g_, b_))
        _, vjp = jax.vjp(f, c_ref[...], g_ref[...], b_ref[...])
        dc, dg, db = vjp(dcs_ref[...].astype(F32))
        dc_ref[...] = dc
        _acc(dg_ref, dg)
        _acc(db_ref, db)

    d_c, g["conv_ln_g"], g["conv_ln_b"] = rowwise("conv_ln_silu_bwd" + tag, ln_bwd_body, s_len, [_whole(sv["c"]), _whole(d_cs)],
                                                  [_row(w["conv_ln_g"]), _row(w["conv_ln_b"])], [(CONV_C, F32)],
                                                  [((1, CONV_C), F32), ((1, CONV_C), F32)])
    dz, g["conv_w"], g["conv_b"] = conv_bwd(sv["z"], d_c, w["conv_w"], dz)

    g["w_attn_o"] = mm(sv["o"], d_ya, ta=True, name="attn_out_dw" + tag)
    d_o = mm(d_ya, w["w_attn_o"], tb=True, out_dtype=BF16, name="attn_out_dx" + tag)
    delta = attention_delta(d_o, sv["o"])
    t_bwd = min(ATT_TILE_BWD, s_len)
    rows_of = lambda a: a.reshape(N_HEADS, s_len // t_bwd, 1, t_bwd)
    dq, dk, dv, carried = attention_bwd(sv["q"], sv["k"], sv["v"], d_o, rows_of(sv["lse"]), rows_of(delta), "attention_bwd" + tag, carry)

    def qkv_bwd_body(dq_ref, dk_ref, dv_ref, cc_ref, sa_ref, sb_ref, dqp_ref, dkv_ref, dkr_ref):
        c_, a_, b_ = cc_ref[...], sa_ref[...], sb_ref[...]
        dk_sum = jnp.zeros((dq_ref.shape[0], HP), F32)
        for hh in range(N_HEADS):
            sl = slice(hh * HP, (hh + 1) * HP)
            dqp_ref[:, sl] = _rope_t(dq_ref[:, sl], c_, a_, b_).astype(dqp_ref.dtype)
            dkh = dk_ref[:, sl].astype(F32)
            dkv_ref[:, sl] = dk_ref[:, sl]
            dk_sum = dk_sum + dkh
        dkv_ref[:, HW:2 * HW] = dv_ref[...]
        dkr_ref[...] = _rope_t(dk_sum, c_, a_, b_)

    dq_pre, dkv_pre, d_kr = rowwise("qkv_rope_bwd" + tag, qkv_bwd_body, s_len,
                                    [_whole(dq), _whole(dk), _whole(dv), _whole(cc), _whole(sa), _whole(sb)], [],
                                    [(HW, BF16), (2 * HW, BF16), (HP, F32)])
    g["w_uq"] = mm(sv["qn"], dq_pre, ta=True, name="q_proj_dw" + tag)
    d_qn = mm(dq_pre, w["w_uq"], tb=True, name="q_proj_dx" + tag)
    g["w_ukv"] = mm(sv["ckvn"], dkv_pre, ta=True, name="kv_proj_dw" + tag)
    d_ckvn = mm(dkv_pre, w["w_ukv"], tb=True, name="kv_proj_dx" + tag)

    def prep_bwd_body(z_ref, dqn_ref, dckv_ref, dkr_ref, qg_ref, kg_ref, dz_ref, dqg_ref, dkg_ref):
        _, vq = jax.vjp(_rms, z_ref[:, 0:Q_RANK].astype(F32), qg_ref[...])
        dcq, dqg = vq(dqn_ref[...])
        _, vk = jax.vjp(_rms, z_ref[:, Q_RANK:Q_RANK + KV_RANK].astype(F32), kg_ref[...])
        dckv, dkg = vk(dckv_ref[...])
        dz_ref[:, 0:Q_RANK] = dcq.astype(dz_ref.dtype)
        dz_ref[:, Q_RANK:Q_RANK + KV_RANK] = dckv.astype(dz_ref.dtype)
        dz_ref[:, Q_RANK + KV_RANK:ZA_W] = dkr_ref[...].astype(dz_ref.dtype)
        _acc(dqg_ref, dqg)
        _acc(dkg_ref, dkg)

    dz, g["q_norm"], g["kv_norm"] = rowwise("attn_prep_bwd" + tag, prep_bwd_body, s_len,
                                            [(sv["z"], ZA_W, ZA // ZA_W), _whole(d_qn), _whole(d_ckvn), _whole(d_kr)],
                                            [_row(w["q_norm"]), _row(w["kv_norm"])], [(ZA_W, BF16, ZW, ZA // ZA_W)],
                                            [((1, Q_RANK), F32), ((1, KV_RANK), F32)], into={0: dz})

    if tail is None:
        g["w_in"] = mm(sv["h"], dz, ta=True, name="in_proj_dw" + tag)
        d_h = mm(dz, w["w_in"], tb=True, name="in_proj_dx" + tag)
    else:
        g["w_in"], tailed_rest = mm(sv["h"], dz, ta=True, name="in_proj_dw" + tag, carry=tail[0](g))
        d_h, tailed_in = mm(dz, w["w_in"], tb=True, name="in_proj_dx" + tag, carry=tail[1](g))
        carried = [carried, tailed_in + tailed_rest]
    if prev is None:
        dx, g["mix_norm_pre"] = rowwise(
            "mix_norm_pre_bwd" + tag, _pre_bwd_body, s_len, [_whole(sv["x"]), _whole(d_h), _whole(dh1)], [_row(w["mix_norm_pre"])], [(D_MODEL, F32)], [((1, D_MODEL), F32)])
        return dx, g, carried, None
    y_prev, gain_prev = prev
    dx, d_y_prev, g["mix_norm_pre"], d_gain_prev = rowwise(
        "mix_norm_pre_bwd" + tag, _pre_post_bwd_body, s_len, [_whole(sv["x"]), _whole(d_h), _whole(dh1), _whole(y_prev)],
        [_row(w["mix_norm_pre"]), _row(gain_prev)], [(D_MODEL, F32), (D_MODEL, BF16)], [((1, D_MODEL), F32), ((1, D_MODEL), F32)])
    return dx, g, carried, (d_y_prev, d_gain_prev)


def loss_head(h, target):
    s_len = h.shape[0]

    def body(h_ref, t_ref, dy_ref, loss_ref):
        err = h_ref[...] - t_ref[...]
        dy_ref[...] = err * (1.0 / D_MODEL)
        part = 0.5 * jnp.sum(jnp.mean(err * err, axis=-1, keepdims=True), axis=0, keepdims=True)
        _acc(loss_ref, jnp.broadcast_to(part, (1, LANE)))

    return rowwise("loss_head", body, s_len, [_whole(h), _whole(target)], [], [(D_MODEL, F32)], [((1, LANE), F32)])


def local_step(x, pos_col, target, layers):
    s_len = x.shape[0]
    tabs = rope_tables(pos_col, s_len)
    h, h_normed, saved = x, None, []
    for li, w in enumerate(layers):
        h, sv_mix, _ = mixer_fwd(h, tabs, w, f"_l{li}", h=h_normed)
        next_gain = layers[li + 1]["mix_norm_pre"] if li + 1 < len(layers) else None
        h, sv_ffn, h_normed = ffn_fwd(h, sv_mix["hn"], w, f"_l{li}", next_gain)
        saved.append((sv_mix, sv_ffn))
    dh, loss = loss_head(h, target)
    grads, post = [None] * len(layers), None
    for li in reversed(range(len(layers))):
        dh, g_ffn, mix_post, _ = ffn_bwd(dh, saved[li][1], layers[li], f"_l{li}", saved[li][0]["mo"], post)
        prev = (saved[li - 1][1]["y"], layers[li - 1]["ffn_norm_post"]) if li > 0 else None
        dh, g_mix, _, post = mixer_bwd(dh, saved[li][0], tabs, layers[li], f"_l{li}", mix_post, prev=prev)
        grads[li] = {**g_mix, **g_ffn}
    return loss[0, 0], dh, grads


def _pad_heads_cols(wm, per_head):
    r = wm.shape[0]
    return jnp.pad(wm.reshape(r, N_HEADS, per_head), ((0, 0), (0, 0), (0, HP - per_head))).reshape(r, HW)


def _unpad_heads_cols(wm, per_head):
    r = wm.shape[0]
    return wm.reshape(r, N_HEADS, HP)[:, :, :per_head].reshape(r, N_HEADS * per_head)


def align_weights(p):
    out = dict(p)
    if "w_in" in p:
        w_in = p["w_in"]
        r = w_in.shape[0]
        zeros = lambda n: jnp.zeros((r, n), w_in.dtype)
        conv = w_in[:, O_CONV:O_POOL].reshape(r, 2, CONV_C // LANE, LANE).transpose(0, 2, 1, 3).reshape(r, 2 * CONV_C)
        out["w_in"] = jnp.concatenate([
            w_in[:, O_GATE:D_IN], conv, w_in[:, O_POOL:O_GATE], w_in[:, O_Q:O_KR],
            zeros(KR_LANE), w_in[:, O_KR:O_CONV], zeros(HP - KR_LANE - ROPE)], axis=1)
    if "w_uq" in p:
        out["w_uq"] = _pad_heads_cols(p["w_uq"], NOPE + ROPE)
        out["w_ukv"] = jnp.concatenate([_pad_heads_cols(p["w_uk"], NOPE), _pad_heads_cols(p["w_uv"], VDIM)], axis=1)
        wo = p["w_attn_o"]
        out["w_attn_o"] = jnp.pad(wo.reshape(N_HEADS, VDIM, D_MODEL), ((0, 0), (0, HP - VDIM), (0, 0))).reshape(HW, D_MODEL)
        del out["w_uk"], out["w_uv"]
    if "w_gate" in p:
        out["w_gu"] = jnp.concatenate([p["w_gate"][:, :FF_HALF], p["w_up"][:, :FF_HALF], p["w_gate"][:, FF_HALF:], p["w_up"][:, FF_HALF:]], axis=1)
        del out["w_gate"], out["w_up"]
    return out


def unalign_grads(g):
    out = dict(g)
    if "w_in" in g:
        gi = g["w_in"]
        kr0 = ZA + Q_RANK + KV_RANK + KR_LANE
        r = gi.shape[0]
        conv = gi[:, ZC:ZP].reshape(r, CONV_C // LANE, 2, LANE).transpose(0, 2, 1, 3).reshape(r, 2 * CONV_C)
        out["w_in"] = jnp.concatenate([gi[:, ZA:ZA + Q_RANK + KV_RANK], gi[:, kr0:kr0 + ROPE], conv, gi[:, ZP:ZA], gi[:, ZG:ZC]], axis=1)
    if "w_uq" in g:
        out["w_uq"] = _unpad_heads_cols(g["w_uq"], NOPE + ROPE)
        out["w_uk"] = _unpad_heads_cols(g["w_ukv"][:, :HW], NOPE)
        out["w_uv"] = _unpad_heads_cols(g["w_ukv"][:, HW:], VDIM)
        out["w_attn_o"] = g["w_attn_o"].reshape(N_HEADS, HP, D_MODEL)[:, :VDIM].reshape(N_HEADS * VDIM, D_MODEL)
        del out["w_ukv"]
    if "w_gu" in g:
        gu = g["w_gu"]
        out["w_gate"] = jnp.concatenate([gu[:, 0:FF_HALF], gu[:, 2 * FF_HALF:3 * FF_HALF]], axis=1)
        out["w_up"] = jnp.concatenate([gu[:, FF_HALF:2 * FF_HALF], gu[:, 3 * FF_HALF:]], axis=1)
        del out["w_gu"]
    return out


MESH = pl.DeviceIdType.MESH
ANY = pl.BlockSpec(memory_space=pl.ANY)


def _place():
    return lax.axis_index("x"), lax.axis_index("y"), lax.axis_index("c")


def _other_chips(x, y):
    return [(1 - x, y), (x, 1 - y), (1 - x, 1 - y)]


def _half_rows(rows, c):
    assert rows % (2 * HALF_ALIGN) == 0, rows
    return pl.ds(pl.multiple_of(c * (rows // 2), HALF_ALIGN), rows // 2)


class GatherShards:
    def __init__(self, local):
        self.ins = list(local)
        self.outs = [jax.ShapeDtypeStruct((N_CHIPS, *a.shape), a.dtype) for a in local]
        self.n_sems = 6 * len(local)
        self.base = 0

    def _first(self, in_refs, out_refs, send_sems, recv_sems):
        x, y, c = _place()
        me = 2 * x + y
        chips = _other_chips(x, y)

        def copy(i, k, slot, core, to, src=None):
            dst = out_refs[i].at[slot, _half_rows(out_refs[i].shape[1], core)]
            return pltpu.make_async_remote_copy(src_ref=dst if src is None else src, dst_ref=dst, send_sem=send_sems.at[self.base + 6 * i + k],
                                                recv_sem=recv_sems.at[self.base + 6 * i + k], device_id=to, device_id_type=MESH)

        first = [copy(i, j, me, c, (*chip, c), src=in_refs[i].at[_half_rows(in_refs[i].shape[0], c)])
                 for i in range(len(in_refs)) for j, chip in enumerate(chips)]
        return first, copy

    def start(self, in_refs, out_refs, send_sems, recv_sems):
        first, _ = self._first(in_refs, out_refs, send_sems, recv_sems)
        for cp in first:
            cp.start()

    def finish(self, in_refs, out_refs, send_sems, recv_sems):
        first, copy = self._first(in_refs, out_refs, send_sems, recv_sems)
        x, y, c = _place()
        slots = [2 * cx + cy for cx, cy in _other_chips(x, y)]
        sibling = (x, y, 1 - c)
        passed = []
        for i in range(len(in_refs)):
            for j in range(3):
                copy(i, j, slots[j], c, sibling).wait_recv()
                fwd = copy(i, 3 + j, slots[j], c, sibling)
                fwd.start()
                passed.append(fwd)
        for i in range(len(in_refs)):
            for j in range(3):
                copy(i, 3 + j, slots[j], 1 - c, sibling).wait_recv()
        for cp in first + passed:
            cp.wait_send()


class ChipExchange:
    def __init__(self, parts):
        self.ins = list(parts)
        self.outs = [jax.ShapeDtypeStruct((3, *a.shape[1:]), a.dtype) for a in parts]
        self.n_sems = 3 * len(parts)
        self.base = 0

    def _copies(self, in_refs, out_refs, send_sems, recv_sems):
        x, y, c = _place()
        return [pltpu.make_async_remote_copy(src_ref=in_refs[i].at[2 * chip[0] + chip[1]], dst_ref=out_refs[i].at[j],
                                             send_sem=send_sems.at[self.base + 3 * i + j], recv_sem=recv_sems.at[self.base + 3 * i + j],
                                             device_id=(*chip, c), device_id_type=MESH)
                for i in range(len(in_refs)) for j, chip in enumerate(_other_chips(x, y))]

    def start(self, in_refs, out_refs, send_sems, recv_sems):
        for cp in self._copies(in_refs, out_refs, send_sems, recv_sems):
            cp.start()

    def finish(self, in_refs, out_refs, send_sems, recv_sems):
        copies = self._copies(in_refs, out_refs, send_sems, recv_sems)
        for cp in copies:
            cp.wait_recv()
        for cp in copies:
            cp.wait_send()


def run_exchange(ex, name):
    n_in, n_out = len(ex.ins), len(ex.outs)

    def body(*refs):
        ins, outs, sems = refs[:n_in], refs[n_in:n_in + n_out], refs[n_in + n_out:]
        ex.start(ins, outs, *sems)
        ex.finish(ins, outs, *sems)

    return pl.pallas_call(
        body,
        in_specs=[ANY] * n_in,
        out_specs=[ANY] * n_out,
        out_shape=list(ex.outs),
        scratch_shapes=[pltpu.SemaphoreType.DMA((ex.n_sems,)), pltpu.SemaphoreType.DMA((ex.n_sems,))],
        name=name,
    )(*ex.ins)


class SiblingSwap:
    def __init__(self, gs):
        self.ins = list(gs)
        self.outs = [jax.ShapeDtypeStruct((N_CHIPS, a.shape[1] // 2, a.shape[2]), a.dtype) for a in gs]
        self.n_sems = 4 * len(gs)

    def _copies(self, in_refs, out_refs, send_sems, recv_sems):
        x, y, c = _place()
        return [pltpu.make_async_remote_copy(src_ref=in_refs[i].at[j, _half_rows(in_refs[i].shape[1], 1 - c)], dst_ref=out_refs[i].at[j],
                                             send_sem=send_sems.at[4 * i + j], recv_sem=recv_sems.at[4 * i + j],
                                             device_id=(x, y, 1 - c), device_id_type=MESH)
                for i in range(len(in_refs)) for j in range(N_CHIPS)]

    def start(self, in_refs, out_refs, send_sems, recv_sems):
        for cp in self._copies(in_refs, out_refs, send_sems, recv_sems):
            cp.start()

    def finish(self, in_refs, out_refs, send_sems, recv_sems):
        copies = self._copies(in_refs, out_refs, send_sems, recv_sems)
        for cp in copies:
            cp.wait_recv()
        for cp in copies:
            cp.wait_send()


def sibling_gather(fs):
    n = len(fs)
    layers = fs[0].shape[0]

    def body(*refs):
        out_refs, (send_sems, recv_sems) = refs[n:2 * n], refs[2 * n:]
        x, y, c = _place()

        def copy(i, l, core):
            part = out_refs[i].at[l, _half_rows(out_refs[i].shape[1], core)]
            return pltpu.make_async_remote_copy(src_ref=part, dst_ref=part, send_sem=send_sems.at[layers * i + l],
                                                recv_sem=recv_sems.at[layers * i + l], device_id=(x, y, 1 - c), device_id_type=MESH)

        sends = [copy(i, l, c) for i in range(n) for l in range(layers)]
        for cp in sends:
            cp.start()
        for i in range(n):
            for l in range(layers):
                copy(i, l, 1 - c).wait_recv()
        for cp in sends:
            cp.wait_send()

    return pl.pallas_call(
        body,
        in_specs=[ANY] * n,
        out_specs=[ANY] * n,
        out_shape=[jax.ShapeDtypeStruct(a.shape, a.dtype) for a in fs],
        scratch_shapes=[pltpu.SemaphoreType.DMA((layers * n,)), pltpu.SemaphoreType.DMA((layers * n,))],
        input_output_aliases={i: i for i in range(n)},
        name="sibling_gather",
    )(*fs)


class GatherAll:
    def __init__(self, vs):
        self.ins = list(vs)
        self.outs = [jax.ShapeDtypeStruct((8, *a.shape), a.dtype) for a in vs]
        self.n_sems = 7 * len(vs)
        self.base = 0

    def _first(self, in_refs, out_refs, send_sems, recv_sems):
        x, y, c = _place()
        me, sibling = (x, y, c), (x, y, 1 - c)

        def copy(i, k, block, to, src=None):
            px, py, pc = block
            dst = out_refs[i].at[4 * px + 2 * py + pc]
            return pltpu.make_async_remote_copy(src_ref=dst if src is None else src, dst_ref=dst, send_sem=send_sems.at[self.base + 7 * i + k],
                                                recv_sem=recv_sems.at[self.base + 7 * i + k], device_id=to, device_id_type=MESH)

        first = []
        for i in range(len(in_refs)):
            first.append(copy(i, 0, me, sibling, src=in_refs[i]))
            first += [copy(i, 1 + j, me, (*chip, c), src=in_refs[i]) for j, chip in enumerate(_other_chips(x, y))]
        return first, copy

    def start(self, in_refs, out_refs, send_sems, recv_sems):
        first, _ = self._first(in_refs, out_refs, send_sems, recv_sems)
        for cp in first:
            cp.start()

    def finish(self, in_refs, out_refs, send_sems, recv_sems):
        first, copy = self._first(in_refs, out_refs, send_sems, recv_sems)
        x, y, c = _place()
        me, sibling = (x, y, c), (x, y, 1 - c)
        chips = _other_chips(x, y)
        passed = []
        for i in range(len(in_refs)):
            for j, chip in enumerate(chips):
                copy(i, 1 + j, (*chip, c), me).wait_recv()
                fwd = copy(i, 4 + j, (*chip, c), sibling)
                fwd.start()
                passed.append(fwd)
        for i in range(len(in_refs)):
            copy(i, 0, sibling, me).wait_recv()
            for j, chip in enumerate(chips):
                copy(i, 4 + j, (*chip, 1 - c), me).wait_recv()
        for cp in first + passed:
            cp.wait_send()


def _row_tile(rows, row_bytes):
    best = None
    for t in range(16, rows + 1, 16):
        if rows % t == 0 and t * row_bytes <= SUM_TILE_BYTES:
            best = t
    return best or rows


def sibling_sum(g, theirs, place, name):
    _, half, cols = theirs.shape
    tile = _row_tile(half, cols * 4)
    nt = half // tile

    def body(place_ref, g_ref, t_ref, o_ref):
        o_ref[...] = (g_ref[...].astype(F32) + t_ref[...].astype(F32)).astype(o_ref.dtype)

    spec = pl.BlockSpec((1, tile, cols), lambda j, i, place_ref: (j, i, 0))
    return pl.pallas_call(
        body,
        grid_spec=pltpu.PrefetchScalarGridSpec(
            num_scalar_prefetch=1, grid=(N_CHIPS, nt),
            in_specs=[pl.BlockSpec((1, tile, cols), lambda j, i, place_ref: (j, place_ref[1] * nt + i, 0)), spec], out_specs=spec),
        out_shape=jax.ShapeDtypeStruct(theirs.shape, BF16),
        compiler_params=_cparams(("parallel", "parallel")),
        name=name,
    )(place, g, theirs)


def chip_sum(p, others, place, layer, into, name):
    _, half, cols = p.shape
    tile = _row_tile(half, cols * 4)
    nt = half // tile

    def body(place_ref, p_ref, o3_ref, *rest):
        o_ref = rest[-1]
        acc = p_ref[0].astype(F32)
        for k in range(3):
            acc = acc + o3_ref[k].astype(F32)
        o_ref[0] = acc

    return pl.pallas_call(
        body,
        grid_spec=pltpu.PrefetchScalarGridSpec(
            num_scalar_prefetch=1, grid=(nt,),
            in_specs=[pl.BlockSpec((1, tile, cols), lambda i, place_ref: (place_ref[0], i, 0)),
                      pl.BlockSpec((3, tile, cols), lambda i, place_ref: (0, i, 0))] + ([] if into is None else [ANY]),
            out_specs=pl.BlockSpec((1, tile, cols), lambda i, place_ref: (layer, place_ref[1] * nt + i, 0))),
        out_shape=jax.ShapeDtypeStruct((N_LAYERS, 2 * half, cols), F32),
        input_output_aliases={} if into is None else {3: 0},
        compiler_params=_cparams(("parallel",)),
        name=name,
    )(place, p, others, *([] if into is None else [into]))


def sum_devices(a, name):
    n, rows, cols = a.shape
    tile = _row_tile(rows, cols * 4 * n)

    def body(a_ref, o_ref):
        acc = a_ref[0]
        for s in range(1, n):
            acc = acc + a_ref[s]
        o_ref[...] = acc

    return pl.pallas_call(body, grid=(rows // tile,), in_specs=[pl.BlockSpec((n, tile, cols), lambda i: (0, i, 0))],
                          out_specs=pl.BlockSpec((tile, cols), lambda i: (i, 0)), out_shape=jax.ShapeDtypeStruct((rows, cols), F32),
                          compiler_params=_cparams(("parallel",)), name=name)(a)


def adamw(w, g, m, v, name, carry=None):
    layers, rows, cols = w.shape
    tile = rows
    for t in range(8, rows, 8):
        if rows % t == 0 and t * cols * 4 <= ADAM_TILE_BYTES:
            tile = t
    nt = rows // tile

    def body(*refs):
        (w_ref, g_ref, m_ref, v_ref, d_ref, mo_ref, vo_ref), start, finish = _carried(carry, refs, 4, 3, 0)
        if start is not None:
            pl.when((pl.program_id(0) == 0) & (pl.program_id(1) == 0))(start)
        gg = g_ref[...]
        m_new = ADAM_B1 * m_ref[...] + (1.0 - ADAM_B1) * gg
        v_new = ADAM_B2 * v_ref[...] + (1.0 - ADAM_B2) * (gg * gg)
        m_hat = m_new / (1.0 - ADAM_B1 ** ADAM_STEP)
        v_hat = v_new / (1.0 - ADAM_B2 ** ADAM_STEP)
        d_ref[...] = -ADAM_LR * (m_hat / (jnp.sqrt(v_hat) + ADAM_EPS) + ADAM_WD * w_ref[...])
        mo_ref[...] = m_new
        vo_ref[...] = v_new
        if finish is not None:
            pl.when((pl.program_id(0) == layers - 1) & (pl.program_id(1) == nt - 1))(finish)

    ex_in_specs, ex_out_specs, ex_out_shape, ex_scratch, ex_inputs = _carry_specs(carry)
    spec = pl.BlockSpec((1, tile, cols), lambda l, i: (l, i, 0))
    shape = jax.ShapeDtypeStruct((layers, rows, cols), F32)
    d, m_new, v_new, *carried = pl.pallas_call(
        body, grid=(layers, nt), in_specs=[spec] * 4 + ex_in_specs, out_specs=[spec] * 3 + ex_out_specs,
        out_shape=[shape] * 3 + ex_out_shape, scratch_shapes=ex_scratch,
        compiler_params=_cparams(("arbitrary", "arbitrary") if carry is not None else ("parallel", "parallel")), name=name,
    )(w, g, m, v, *ex_inputs)
    return (d, m_new, v_new) if carry is None else (d, m_new, v_new, carried)


WEIGHTS = ["mix_norm_pre", "w_in", "q_norm", "w_uq", "kv_norm", "w_uk", "w_uv", "w_attn_o", "conv_w", "conv_b", "conv_ln_g", "conv_ln_b",
           "w_conv_o", "pool_w", "pool_scale", "w_pool_o", "w_mix_o", "mix_norm_post", "ffn_norm_pre", "w_gate", "w_up", "w_down",
           "ffn_norm_post"]
SHARDED = {"w_in": 2, "w_uq": 2, "w_uk": 2, "w_uv": 2, "w_attn_o": 2, "conv_w": 2, "w_conv_o": 2, "w_pool_o": 2, "w_mix_o": 1,
           "w_gate": 2, "w_up": 2, "w_down": 1}
REPLICATED = [n for n in WEIGHTS if n not in SHARDED]
ROW_PARAMS = [n for n in REPLICATED if n != "pool_w"]
ROWS_MINOR = ("w_in", "w_uq", "w_gate", "w_up", "conv_w")
N_CHIPS = 4
N_MIX_GROUPS = 6
MIX_MATRICES = ("w_in", "w_uq", "w_ukv", "w_attn_o", "w_conv_o", "w_pool_o", "conv_w", "w_mix_o")
CONV_WIRE_ROWS = 32
GROUPS = [(("w_in",), 1), (("w_uq",), 1), (("w_uk", "w_uv"), 1), (("w_attn_o", "w_conv_o", "w_pool_o"), 1), (("conv_w",), 1),
          (("w_mix_o",), 1), (("w_gate", "w_up"), 2), (("w_down",), 1)]


def _join(parts, axis):
    return parts[0] if len(parts) == 1 else jnp.concatenate(parts, axis=axis)


def _split_group(arr, names, axis, shapes):
    out, off = {}, 0
    ax = arr.ndim - 3 + axis
    for n in names:
        size = shapes[n][axis]
        out[n] = lax.slice_in_dim(arr, off, off + size, axis=ax)
        off += size
    return out


def _pack_rows(vectors):
    blocks = []
    for v in vectors:
        for li in range(v.shape[0]):
            blocks.append(jnp.pad(v[li][None, :], ((0, PACK_ROWS - 1), (0, PACK_W - v.shape[1]))))
    return jnp.concatenate(blocks, axis=0)


def _unpack_rows(packed, shapes):
    out, r = [], 0
    for layers, width in shapes:
        out.append(jnp.stack([packed[r + PACK_ROWS * li, :width] for li in range(layers)]))
        r += PACK_ROWS * layers
    return out


def kernel(x, positions, mix_norm_pre, w_in, q_norm, w_uq, kv_norm, w_uk, w_uv, w_attn_o, conv_w, conv_b, conv_ln_g, conv_ln_b, w_conv_o, pool_w, pool_scale, w_pool_o, w_mix_o, mix_norm_post, ffn_norm_pre, w_gate, w_up, w_down, ffn_norm_post, loss_target, m_mix_norm_pre, m_w_in, m_q_norm, m_w_uq, m_kv_norm, m_w_uk, m_w_uv, m_w_attn_o, m_conv_w, m_conv_b, m_conv_ln_g, m_conv_ln_b, m_w_conv_o, m_pool_w, m_pool_scale, m_w_pool_o, m_w_mix_o, m_mix_norm_post, m_ffn_norm_pre, m_w_gate, m_w_up, m_w_down, m_ffn_norm_post, v_mix_norm_pre, v_w_in, v_q_norm, v_w_uq, v_kv_norm, v_w_uk, v_w_uv, v_w_attn_o, v_conv_w, v_conv_b, v_conv_ln_g, v_conv_ln_b, v_w_conv_o, v_pool_w, v_pool_scale, v_w_pool_o, v_w_mix_o, v_mix_norm_post, v_ffn_norm_pre, v_w_gate, v_w_up, v_w_down, v_ffn_norm_post):
    given = dict(mix_norm_pre=mix_norm_pre, w_in=w_in, q_norm=q_norm, w_uq=w_uq, kv_norm=kv_norm, w_uk=w_uk, w_uv=w_uv, w_attn_o=w_attn_o,
                 conv_w=conv_w, conv_b=conv_b, conv_ln_g=conv_ln_g, conv_ln_b=conv_ln_b, w_conv_o=w_conv_o, pool_w=pool_w,
                 pool_scale=pool_scale, w_pool_o=w_pool_o, w_mix_o=w_mix_o, mix_norm_post=mix_norm_post, ffn_norm_pre=ffn_norm_pre,
                 w_gate=w_gate, w_up=w_up, w_down=w_down, ffn_norm_post=ffn_norm_post)
    mom = dict(mix_norm_pre=m_mix_norm_pre, w_in=m_w_in, q_norm=m_q_norm, w_uq=m_w_uq, kv_norm=m_kv_norm, w_uk=m_w_uk, w_uv=m_w_uv,
               w_attn_o=m_w_attn_o, conv_w=m_conv_w, conv_b=m_conv_b, conv_ln_g=m_conv_ln_g, conv_ln_b=m_conv_ln_b, w_conv_o=m_w_conv_o,
               pool_w=m_pool_w, pool_scale=m_pool_scale, w_pool_o=m_w_pool_o, w_mix_o=m_w_mix_o, mix_norm_post=m_mix_norm_post,
               ffn_norm_pre=m_ffn_norm_pre, w_gate=m_w_gate, w_up=m_w_up, w_down=m_w_down, ffn_norm_post=m_ffn_norm_post)
    var = dict(mix_norm_pre=v_mix_norm_pre, w_in=v_w_in, q_norm=v_q_norm, w_uq=v_w_uq, kv_norm=v_kv_norm, w_uk=v_w_uk, w_uv=v_w_uv,
               w_attn_o=v_w_attn_o, conv_w=v_conv_w, conv_b=v_conv_b, conv_ln_g=v_conv_ln_g, conv_ln_b=v_conv_ln_b, w_conv_o=v_w_conv_o,
               pool_w=v_pool_w, pool_scale=v_pool_scale, w_pool_o=v_w_pool_o, w_mix_o=v_w_mix_o, mix_norm_post=v_mix_norm_post,
               ffn_norm_pre=v_ffn_norm_pre, w_gate=v_w_gate, w_up=v_w_up, w_down=v_w_down, ffn_norm_post=v_ffn_norm_post)
    s_len = x.shape[1]
    sharded_names = [n for n in WEIGHTS if n in SHARDED]
    chip = 2 * lax.axis_index("x") + lax.axis_index("y")
    place = jnp.stack([chip, lax.axis_index("c")]).astype(jnp.int32)
    shard_shape = {n: given[n].shape for n in sharded_names}

    mix_groups, ffn_groups = GROUPS[:N_MIX_GROUPS], GROUPS[N_MIX_GROUPS:]
    weight_wire_shape = {n: (N_LAYERS, 2 * CONV_WIRE_ROWS, shard_shape[n][2]) if n == "conv_w" else shard_shape[n] for n in sharded_names}
    grad_wire_shape = {n: (N_LAYERS, CONV_WIRE_ROWS, shard_shape[n][2]) if n == "conv_w" else shard_shape[n] for n in sharded_names}
    pad_rows = lambda a: jnp.pad(a, ((0, CONV_WIRE_ROWS - CONV_W), (0, 0)))

    def weight_wires(groups, li):
        def wire(name):
            a = given[name][li]
            if name == "conv_w":
                hi = a.astype(BF16)
                return jnp.concatenate([pad_rows(hi), pad_rows((a - hi.astype(F32)).astype(BF16))], axis=0)
            return a.astype(BF16)

        return [_join([wire(n) for n in names], axis - 1) for names, axis in groups]

    def full_weights(groups, li, local, gathered):
        p = {n: given[n][li] for n in REPLICATED}
        p["pool_w"] = p["pool_w"].astype(BF16)
        for (names, axis), loc, got in zip(groups, local, gathered):
            got = lax.dynamic_update_slice(got, loc[None], (chip, 0, 0))
            per_chip = [_split_group(got[j], names, axis, weight_wire_shape) for j in range(N_CHIPS)]
            for n in names:
                parts = [pc[n] for pc in per_chip]
                if n == "conv_w":
                    parts = [q[:CONV_W].astype(F32) + q[CONV_WIRE_ROWS:CONV_WIRE_ROWS + CONV_W].astype(F32) for q in parts]
                p[n] = jnp.concatenate(parts, axis=SHARDED[n] - 1)
        return align_weights(p)

    def grad_wires(groups, g):
        wires = []
        for names, axis in groups:
            split = {n: jnp.split(pad_rows(g[n]) if n == "conv_w" else g[n], N_CHIPS, axis=SHARDED[n] - 1) for n in names}
            wires.append(jnp.stack([_join([split[n][j].astype(BF16) for n in names], axis - 1) for j in range(N_CHIPS)]))
        return wires

    def partial_sums(wires, theirs, tag):
        return [sibling_sum(w, t, place, f"sibling_sum_{tag}_{i}") for i, (w, t) in enumerate(zip(wires, theirs))]

    def chip_partials(groups, g, tag):
        wires = grad_wires(groups, g)
        return partial_sums(wires, run_exchange(SiblingSwap(wires), "sibling_swap_" + tag), tag)

    tabs = rope_tables(positions.reshape(s_len, 1), s_len)
    in_groups, rest_groups = mix_groups[:1], mix_groups[1:]
    heads = []
    for li in range(N_LAYERS):
        loc_rest = weight_wires(rest_groups, li)
        heads.append((GatherShards(loc_rest), functools.partial(full_weights, rest_groups, li, loc_rest)))
    loc_in0 = weight_wires(in_groups, 0)
    w_in0 = full_weights(in_groups, 0, loc_in0, run_exchange(GatherShards(loc_in0), "gather_w_in_l0"))
    loc_f0, loc_in1 = weight_wires(ffn_groups, 0), weight_wires(in_groups, 1)
    h, sv_m0, got = mixer_fwd(x[0], tabs, w_in0, "_l0", GatherShards(loc_f0 + loc_in1), heads[0])
    w_m0 = sv_m0["w"]
    w_f0 = full_weights(ffn_groups, 0, loc_f0, got[:len(loc_f0)])
    w_in1 = full_weights(in_groups, 1, loc_in1, got[len(loc_f0):])
    h, sv_f0, h_normed = ffn_fwd(h, sv_m0["hn"], w_f0, "_l0", w_in1["mix_norm_pre"])
    loc_f1 = weight_wires(ffn_groups, 1)
    h, sv_m1, got = mixer_fwd(h, tabs, w_in1, "_l1", GatherShards(loc_f1), heads[1], h=h_normed)
    w_m1 = sv_m1["w"]
    w_f1 = full_weights(ffn_groups, 1, loc_f1, got)
    h, sv_f1, _ = ffn_fwd(h, sv_m1["hn"], w_f1, "_l1")
    dh, loss_local = loss_head(h, loss_target[0])
    loss = lax.psum(loss_local[0, 0], MESH_AXES)

    last = {}

    def then_exchange(key, wires, tag, more=()):
        def make(theirs):
            last[key] = partial_sums(wires, theirs, tag)
            return ChipExchange(list(more) + last[key])
        return make

    dh, g_f1, post_m1, _ = ffn_bwd(dh, sv_f1, w_f1, "_l1", sv_m1["mo"])
    g_f1 = unalign_grads(g_f1)
    wires_f1 = grad_wires(ffn_groups, g_f1)
    dh, g_m1, o_f1, post_f0 = mixer_bwd(dh, sv_m1, tabs, w_m1, "_l1", post_m1, then_exchange("f1", wires_f1, "ffn_l1"),
                                        prev=(sv_f0["y"], w_f0["ffn_norm_post"]), first=SiblingSwap(wires_f1))
    p_f1 = last["f1"]
    g_m1 = unalign_grads(g_m1)
    wires_m1 = grad_wires(mix_groups, g_m1)
    dh, g_f0, post_m0, theirs_m1 = ffn_bwd(dh, sv_f0, w_f0, "_l0", sv_m0["mo"], post_f0, first=SiblingSwap(wires_m1))
    p_m1 = partial_sums(wires_m1, theirs_m1, "mix_l1")
    g_f0 = unalign_grads(g_f0)
    wires_f0 = grad_wires(ffn_groups, g_f0)

    def tail_rest(g):
        last["rest"] = chip_partials(rest_groups, unalign_grads({n: g[n] for n in MIX_MATRICES[1:]}), "rest_l0")
        return ChipExchange(last["rest"])

    def tail_in(g):
        last["in"] = chip_partials(in_groups, unalign_grads({"w_in": g["w_in"]}), "w_in_l0")
        return ChipExchange(last["in"])

    grad_x, g_m0, (got, o_m0), _ = mixer_bwd(dh, sv_m0, tabs, w_m0, "_l0", post_m0, then_exchange("f0", wires_f0, "ffn_l0", p_m1),
                                             (tail_rest, tail_in), first=SiblingSwap(wires_f0))
    p_f0 = last["f0"]
    o_m1, o_f0 = got[:len(p_m1)], got[len(p_m1):]
    p_m0 = last["in"] + last["rest"]
    grads = [{**g_m0, **g_f0}, {**g_m1, **g_f1}]
    grad_full = {n: jnp.stack([g[n].reshape(given[n].shape[1:]) for g in grads]) for n in REPLICATED}
    sums = {}
    for groups, base, per_layer in ((ffn_groups, N_MIX_GROUPS, ((1, p_f1, o_f1), (0, p_f0, o_f0))), (mix_groups, 0, ((1, p_m1, o_m1), (0, p_m0, o_m0)))):
        for li, parts, others in per_layer:
            for i, (p, o) in enumerate(zip(parts, others)):
                sums[base + i] = chip_sum(p, o, place, li, sums.get(base + i), f"chip_sum_{base + i}_l{li}")
    g_shard = {}
    for (names, axis), s in zip(GROUPS, sibling_gather([sums[i] for i in range(len(GROUPS))])):
        g_shard.update(_split_group(s, names, axis, grad_wire_shape))
    g_shard["conv_w"] = g_shard["conv_w"][:, :CONV_W]

    replicated = [_pack_rows([grad_full[n] for n in ROW_PARAMS]), grad_full["pool_w"].reshape(-1, POOL_GD)]
    row_shapes = [given[n].shape for n in ROW_PARAMS]
    g_out, d_out, m_out, v_out, g_rep = {}, {}, {}, {}, {}
    for k, n in enumerate(sharded_names + ["pool_w"]):
        shp = given[n].shape
        three_d = (shp[0], int(np.prod(shp[1:-1])), shp[-1])
        g_n = g_shard[n] if n in SHARDED else g_rep[n]
        view = (lambda a: jnp.swapaxes(a.reshape(three_d), 1, 2)) if n in ROWS_MINOR else (lambda a: a.reshape(three_d))
        back = (lambda a: jnp.swapaxes(a, 1, 2).reshape(shp)) if n in ROWS_MINOR else (lambda a: a.reshape(shp))
        if k == 0:
            d, mn, vn, got = adamw(view(given[n]), view(g_n), view(mom[n]), view(var[n]), "adamw_" + n, GatherAll(replicated))
            device = 2 * chip + lax.axis_index("c")
            rows_all, pool_w_all = [lax.dynamic_update_slice(a, mine[None], (device, 0, 0)) for a, mine in zip(got, replicated)]
            g_rows = sum_devices(rows_all, "row_params_sum")
            g_rep = dict(zip(ROW_PARAMS, _unpack_rows(g_rows, row_shapes)))
            g_rep["pool_w"] = sum_devices(pool_w_all, "pool_w_sum").reshape(given["pool_w"].shape)
        else:
            d, mn, vn = adamw(view(given[n]), view(g_n), view(mom[n]), view(var[n]), "adamw_" + n)
        g_out[n], d_out[n], m_out[n], v_out[n] = g_n, back(d), back(mn), back(vn)
    rd, rm, rv = adamw(_pack_rows([given[n] for n in ROW_PARAMS])[None], g_rows[None], _pack_rows([mom[n] for n in ROW_PARAMS])[None],
                       _pack_rows([var[n] for n in ROW_PARAMS])[None], "adamw_row_params")
    for n, d, mn, vn in zip(ROW_PARAMS, *[_unpack_rows(a[0], row_shapes) for a in (rd, rm, rv)]):
        g_out[n], d_out[n], m_out[n], v_out[n] = g_rep[n], d, mn, vn

    return (loss, grad_x[None], *[g_out[n] for n in WEIGHTS], *[d_out[n] for n in WEIGHTS], *[m_out[n] for n in WEIGHTS],
            *[v_out[n] for n in WEIGHTS])
```

```python
import functools
import math

import numpy as np
import jax
import jax.numpy as jnp
from jax import lax
from jax.experimental import pallas as pl
from jax.experimental.pallas import tpu as pltpu

F32, BF16 = jnp.float32, jnp.bfloat16

D_MODEL = 1024
N_HEADS = 8
NOPE, ROPE, VDIM = 64, 32, 64
HALF_ROPE = ROPE // 2
Q_RANK, KV_RANK = 384, 256
CONV_C, CONV_W = 512, 31
POOL_C, POOL_G, POOL_GD = 512, 4, 128
POOL_WINDOWS = (2, 4, 8, 16)
D_FF = 2816
FF_HALF = D_FF // 2
N_LAYERS = 2
EPS = 1e-6
ROPE_THETA = 10000.0
ATT_SCALE = 1.0 / math.sqrt(NOPE + ROPE)
O_Q, O_KV, O_KR, O_CONV, O_POOL, O_GATE, D_IN = 0, 384, 640, 672, 1696, 2208, 5280

LANE = 128
HP = 128
ZG, ZC, ZP, ZA, ZW = 0, 3072, 4096, 4608, 5376
ZA_W = Q_RANK + KV_RANK + HP
KR_LANE = NOPE
HW = N_HEADS * HP

ADAM_LR, ADAM_B1, ADAM_B2, ADAM_EPS, ADAM_WD, ADAM_STEP = 0.001, 0.9, 0.999, 1e-08, 0.01, 10

ROW_TILE = 512
WIDE_ROW_TILE = 256
ATT_TILE_FWD = 1024
ATT_TILE_BWD = 512
ATT_HEADS = 4
CONV_CHUNK = 256
MM_TM, MM_TN, MM_TK = 1024, 1408, 1024
MM_TILE_MAX = 2048
FFN_TM = 512
MM_VMEM_BUDGET = 40 * 1024 * 1024
HBM_BYTES_PER_US = 3.0e6
GRID_STEP_US = 0.35
VMEM_LIMIT = 56 * 1024 * 1024
SUM_TILE_BYTES = 3 * 1024 * 1024
ADAM_TILE_BYTES = 2 * 1024 * 1024

HALF_ALIGN = 16
MESH_AXES = ("x", "y", "c")
PACK_W = 1024
PACK_ROWS = 8


def _cparams(sem):
    return pltpu.CompilerParams(dimension_semantics=sem, vmem_limit_bytes=VMEM_LIMIT)


def _tile(n, target):
    if n <= target:
        return n
    best = None
    for t in range(LANE, target + 1, LANE):
        if n % t == 0:
            best = t
    assert best is not None, (n, target)
    return best


def _mm_tiles(m, n, k, a_bytes, b_bytes, out_bytes):
    divs = lambda d: sorted({t for t in range(LANE, min(d, MM_TILE_MAX) + 1, LANE) if d % t == 0} | ({d} if d <= MM_TILE_MAX else set()))
    best = None
    for tm in divs(m):
        for tn in divs(n):
            blocks = tm * k * a_bytes + k * tn * b_bytes + tm * tn * out_bytes
            if 2 * blocks + tm * tn * 4 > MM_VMEM_BUDGET:
                continue
            steps = (m // tm) * (n // tn)
            for rows_outer in (True, False):
                moved = (m * k * a_bytes + k * n * b_bytes * (m // tm)) if rows_outer else (k * n * b_bytes + m * k * a_bytes * (n // tn))
                cost = (moved + m * n * out_bytes + blocks) / HBM_BYTES_PER_US + steps * GRID_STEP_US
                if best is None or cost < best[0]:
                    best = (cost, tm, tn, rows_outer)
    if best is not None:
        return best[1], best[2], k, best[3]
    return _tile(m, MM_TM), _tile(n, MM_TN), _tile(k, MM_TK), True


def mm(a, b, *, ta=False, tb=False, out_dtype=F32, name, carry=None):
    m, k = (a.shape[1], a.shape[0]) if ta else a.shape
    n, k2 = b.shape if tb else (b.shape[1], b.shape[0])
    assert k == k2, (a.shape, b.shape, ta, tb)
    tm, tn, tk, rows_outer = _mm_tiles(m, n, k, a.dtype.itemsize, b.dtype.itemsize, jnp.dtype(out_dtype).itemsize)
    nk = k // tk
    dims = (((0 if ta else 1,), (1 if tb else 0,)), ((), ()))
    grid = (m // tm, n // tn, nk) if rows_outer else (n // tn, m // tm, nk)

    def body(*refs):
        (a_ref, b_ref, o_ref, *acc), start, finish = _carried(carry, refs, 2, 1, 0 if nk == 1 else 1)
        ids = [pl.program_id(d) for d in range(3)]
        if start is not None:
            pl.when((ids[0] == 0) & (ids[1] == 0) & (ids[2] == 0))(start)
        part = lax.dot_general(a_ref[...].astype(BF16), b_ref[...].astype(BF16), dims, preferred_element_type=F32)
        if nk == 1:
            o_ref[...] = part.astype(o_ref.dtype)
        else:
            (acc_ref,) = acc
            kk = ids[2]

            @pl.when(kk == 0)
            def _():
                acc_ref[...] = part

            @pl.when(kk > 0)
            def _():
                acc_ref[...] += part

            @pl.when(kk == nk - 1)
            def _():
                o_ref[...] = acc_ref[...].astype(o_ref.dtype)
        if finish is not None:
            pl.when((ids[0] == grid[0] - 1) & (ids[1] == grid[1] - 1) & (ids[2] == nk - 1))(finish)

    ij = (lambda g0, g1: (g0, g1)) if rows_outer else (lambda g0, g1: (g1, g0))

    def a_map(g0, g1, kk):
        i, _ = ij(g0, g1)
        return (kk, i) if ta else (i, kk)

    def b_map(g0, g1, kk):
        _, j = ij(g0, g1)
        return (j, kk) if tb else (kk, j)

    ex_in_specs, ex_out_specs, ex_out_shape, ex_scratch, ex_inputs = _carry_specs(carry)
    a_spec = pl.BlockSpec((tk, tm) if ta else (tm, tk), a_map)
    b_spec = pl.BlockSpec((tn, tk) if tb else (tk, tn), b_map)
    out, *carried = pl.pallas_call(
        body,
        grid=grid,
        in_specs=[a_spec, b_spec] + ex_in_specs,
        out_specs=[pl.BlockSpec((tm, tn), lambda g0, g1, kk: ij(g0, g1))] + ex_out_specs,
        out_shape=[jax.ShapeDtypeStruct((m, n), out_dtype)] + ex_out_shape,
        scratch_shapes=([] if nk == 1 else [pltpu.VMEM((tm, tn), F32)]) + ex_scratch,
        compiler_params=_cparams(("arbitrary", "arbitrary", "arbitrary") if carry is not None else ("parallel", "parallel", "arbitrary")),
        name=name,
    )(a, b, *ex_inputs)
    return out if carry is None else (out, carried)


def ffn_in(hn, w_gu, name):
    s_len, k = hn.shape
    tm = min(FFN_TM, s_len)

    def body(a_ref, b_ref, gu_ref, act_ref):
        r = jnp.dot(a_ref[...], b_ref[...], preferred_element_type=F32)
        gu_ref[...] = r.astype(gu_ref.dtype)
        act_ref[...] = (_silu(r[:, :FF_HALF]) * r[:, FF_HALF:]).astype(act_ref.dtype)

    return pl.pallas_call(
        body,
        grid=(2, s_len // tm),
        in_specs=[pl.BlockSpec((tm, k), lambda j, i: (i, 0)), pl.BlockSpec((k, 2 * FF_HALF), lambda j, i: (0, j))],
        out_specs=[pl.BlockSpec((tm, 2 * FF_HALF), lambda j, i: (i, j)), pl.BlockSpec((tm, FF_HALF), lambda j, i: (i, j))],
        out_shape=[jax.ShapeDtypeStruct((s_len, 2 * D_FF), BF16), jax.ShapeDtypeStruct((s_len, D_FF), BF16)],
        compiler_params=_cparams(("arbitrary", "parallel")),
        name=name,
    )(hn, w_gu)


def ffn_out_dx(d_y, w_down, gu, name):
    s_len, k = d_y.shape
    tm = min(FFN_TM, s_len)

    def body(a_ref, b_ref, gu_ref, dgu_ref):
        da = lax.dot_general(a_ref[...], b_ref[...], NT_DIMS, preferred_element_type=F32)
        gt = gu_ref[:, :FF_HALF].astype(F32)
        up = gu_ref[:, FF_HALF:].astype(F32)
        sg = jax.nn.sigmoid(gt)
        dgu_ref[:, :FF_HALF] = (da * up * sg * (1.0 + gt * (1.0 - sg))).astype(dgu_ref.dtype)
        dgu_ref[:, FF_HALF:] = (da * gt * sg).astype(dgu_ref.dtype)

    pair = pl.BlockSpec((tm, 2 * FF_HALF), lambda j, i: (i, j))
    return pl.pallas_call(
        body,
        grid=(2, s_len // tm),
        in_specs=[pl.BlockSpec((tm, k), lambda j, i: (i, 0)), pl.BlockSpec((FF_HALF, k), lambda j, i: (j, 0)), pair],
        out_specs=pair,
        out_shape=jax.ShapeDtypeStruct((s_len, 2 * D_FF), BF16),
        compiler_params=_cparams(("parallel", "arbitrary")),
        name=name,
    )(d_y, w_down, gu)


def rowwise(name, body, rows, row_ins, full_ins, row_outs, acc_outs=(), into=None, tile=None):
    tile = min(tile or ROW_TILE, rows)
    into = into or {}
    in_specs = [pl.BlockSpec((tile, w), lambda i, cb=cb: (i, cb)) for _, w, cb in row_ins]
    in_specs += [pl.BlockSpec(a.shape, lambda i, nd=a.ndim: (0,) * nd) for a in full_ins]
    in_specs += [ANY for _ in into]
    n_in = len(row_ins) + len(full_ins)
    aliases = {n_in + k: oi for k, oi in enumerate(into)}
    out_specs, out_shape = [], []
    for ro in row_outs:
        w, dt, full_w, cb = ro if len(ro) == 4 else (*ro, ro[0], 0)
        out_specs.append(pl.BlockSpec((tile, w), lambda i, cb=cb: (i, cb)))
        out_shape.append(jax.ShapeDtypeStruct((rows, full_w), dt))
    out_specs += [pl.BlockSpec(s, lambda i, nd=len(s): (0,) * nd) for s, _ in acc_outs]
    out_shape += [jax.ShapeDtypeStruct(s, dt) for s, dt in acc_outs]
    n_refs = n_in

    def call_body(*refs):
        body(*refs[:n_refs], *refs[n_refs + len(into):])

    outs = pl.pallas_call(
        call_body,
        grid=(rows // tile,),
        in_specs=in_specs,
        out_specs=out_specs,
        out_shape=out_shape,
        input_output_aliases=aliases,
        compiler_params=_cparams(("arbitrary",)),
        name=name,
    )(*[a for a, _, _ in row_ins], *full_ins, *into.values())
    return outs


def _whole(a):
    return (a, a.shape[1], 0)


def _acc(ref, val):
    @pl.when(pl.program_id(0) == 0)
    def _():
        ref[...] = val

    @pl.when(pl.program_id(0) > 0)
    def _():
        ref[...] += val


def _rms(x, g):
    return x * lax.rsqrt(jnp.mean(x * x, axis=-1, keepdims=True) + EPS) * g


def _layer_norm(x, g, b):
    mu = jnp.mean(x, axis=-1, keepdims=True)
    xc = x - mu
    return xc * lax.rsqrt(jnp.mean(xc * xc, axis=-1, keepdims=True) + EPS) * g + b


def _silu(x):
    return x * jax.nn.sigmoid(x)


def _rope(x, cc, sa, sb):
    return x * cc + pltpu.roll(x, HALF_ROPE, 1) * sa + pltpu.roll(x, HP - HALF_ROPE, 1) * sb


def _rope_t(dy, cc, sa, sb):
    return dy * cc + pltpu.roll(dy * sa, HP - HALF_ROPE, 1) + pltpu.roll(dy * sb, HALF_ROPE, 1)


def rope_tables(pos_col, rows):
    lane = np.arange(HP)
    idx = np.where(lane < KR_LANE + HALF_ROPE, lane - KR_LANE, lane - KR_LANE - HALF_ROPE)
    in_rope = (lane >= KR_LANE) & (lane < KR_LANE + ROPE)
    inv_freq = (np.float32(ROPE_THETA) ** (-np.arange(0, ROPE, 2, dtype=np.float32) / np.float32(ROPE))).astype(np.float32)
    freq_row = np.where(in_rope, inv_freq[np.clip(idx, 0, HALF_ROPE - 1)], 0.0).astype(np.float32)[None, :]
    first = ((lane >= KR_LANE) & (lane < KR_LANE + HALF_ROPE)).astype(np.float32)[None, :]
    second = ((lane >= KR_LANE + HALF_ROPE) & (lane < KR_LANE + ROPE)).astype(np.float32)[None, :]

    def body(pos_ref, f_ref, a_ref, b_ref, cc_ref, sa_ref, sb_ref):
        ang = pos_ref[...].astype(F32) * f_ref[...]
        s = jnp.sin(ang)
        cc_ref[...] = jnp.cos(ang)
        sa_ref[...] = s * b_ref[...]
        sb_ref[...] = -s * a_ref[...]

    return rowwise("rope_tables", body, rows, [_whole(pos_col)], [jnp.asarray(freq_row), jnp.asarray(first), jnp.asarray(second)],
                   [(HP, F32)] * 3)


def _causal_mask(t):
    r = lax.broadcasted_iota(jnp.int32, (t, t), 0)
    c = lax.broadcasted_iota(jnp.int32, (t, t), 1)
    return r, c


NT_DIMS = (((1,), (1,)), ((), ()))


def _carried(carry, refs, n_in, n_out, n_scratch):
    if carry is None:
        return refs, None, None
    ni, no = len(carry.ins), len(carry.outs)
    own_in, ex_in = refs[:n_in], refs[n_in:n_in + ni]
    own_out, ex_out = refs[n_in + ni:n_in + ni + n_out], refs[n_in + ni + n_out:n_in + ni + n_out + no]
    scratch = refs[n_in + ni + n_out + no:]
    sems = scratch[n_scratch:]
    return (*own_in, *own_out, *scratch[:n_scratch]), (lambda: carry.start(ex_in, ex_out, *sems)), (lambda: carry.finish(ex_in, ex_out, *sems))


def _carry_specs(carry):
    if carry is None:
        return [], [], [], [], []
    sems = [pltpu.SemaphoreType.DMA((carry.n_sems,)), pltpu.SemaphoreType.DMA((carry.n_sems,))]
    return [ANY] * len(carry.ins), [ANY] * len(carry.outs), list(carry.outs), sems, list(carry.ins)


def attention_fwd(q, k, v, name, carry=None):
    s_len = q.shape[0]
    t = min(ATT_TILE_FWD, s_len)
    nb = s_len // t
    hb = ATT_HEADS
    w = hb * HP
    nh = N_HEADS // hb

    def body(*refs):
        (q_ref, k_ref, v_ref, o_ref, lse_ref, m_sc, acc_sc), start, finish = _carried(carry, refs, 3, 2, 2)
        qi = pl.program_id(1)
        if start is not None:
            pl.when((pl.program_id(0) == 0) & (qi == 0))(start)
        m_sc[...] = jnp.full_like(m_sc, -jnp.inf)
        acc_sc[...] = jnp.zeros_like(acc_sc)

        def block(j, masked):
            ks = pl.ds(pl.multiple_of(j * t, t), t)
            for hh in range(hb):
                ls = slice(hh * HP, (hh + 1) * HP)
                s = lax.dot_general(q_ref[:, ls], k_ref[ks, ls], NT_DIMS, preferred_element_type=F32)
                if masked:
                    r, c = _causal_mask(t)
                    s = jnp.where(c <= r, s, -jnp.inf)
                m_old = m_sc[hh]
                m_new = jnp.maximum(m_old, jnp.max(s, axis=-1, keepdims=True))
                p = jnp.exp(s - m_new)
                acc_sc[hh] = jnp.exp(m_old - m_new) * acc_sc[hh] + jnp.dot(p.astype(BF16), v_ref[ks, ls], preferred_element_type=F32)
                m_sc[hh] = m_new

        def loop_body(j, carry):
            block(j, False)
            return carry

        lax.fori_loop(0, qi, loop_body, 0)
        block(qi, True)
        lane = lax.broadcasted_iota(jnp.int32, (t, HP), 1)
        for hh in range(hb):
            acc = acc_sc[hh]
            l = jnp.sum(jnp.where(lane == VDIM, acc, 0.0), axis=-1, keepdims=True)
            o_ref[:, hh * HP:(hh + 1) * HP] = jnp.where(lane < VDIM, acc / l, 0.0).astype(o_ref.dtype)
            lse_ref[hh] = m_sc[hh] + jnp.log(l)
        if finish is not None:
            pl.when((pl.program_id(0) == nh - 1) & (qi == nb - 1))(finish)

    ex_in_specs, ex_out_specs, ex_out_shape, ex_scratch, ex_inputs = _carry_specs(carry)
    resident = pl.BlockSpec((s_len, w), lambda h, qi: (0, h))
    o, lse, *carried = pl.pallas_call(
        body,
        grid=(nh, nb),
        in_specs=[pl.BlockSpec((t, w), lambda h, qi: (qi, h)), resident, resident] + ex_in_specs,
        out_specs=[pl.BlockSpec((t, w), lambda h, qi: (qi, h)), pl.BlockSpec((hb, t, 1), lambda h, qi: (h, qi, 0))] + ex_out_specs,
        out_shape=[jax.ShapeDtypeStruct((s_len, HW), BF16), jax.ShapeDtypeStruct((N_HEADS, s_len, 1), F32)] + ex_out_shape,
        scratch_shapes=[pltpu.VMEM((hb, t, 1), F32), pltpu.VMEM((hb, t, HP), F32)] + ex_scratch,
        compiler_params=_cparams(("arbitrary", "arbitrary")),
        name=name,
    )(q, k, v, *ex_inputs)
    return o, lse, carried


def attention_delta(do, o):
    s_len = do.shape[0]
    t = min(ROW_TILE, s_len)

    def body(do_ref, o_ref, d_ref):
        prod = do_ref[...].astype(F32) * o_ref[...].astype(F32)
        for h in range(N_HEADS):
            d_ref[h] = jnp.sum(prod[:, h * HP:(h + 1) * HP], axis=-1, keepdims=True)

    return pl.pallas_call(
        body,
        grid=(s_len // t,),
        in_specs=[pl.BlockSpec((t, HW), lambda i: (i, 0))] * 2,
        out_specs=pl.BlockSpec((N_HEADS, t, 1), lambda i: (0, i, 0)),
        out_shape=jax.ShapeDtypeStruct((N_HEADS, s_len, 1), F32),
        compiler_params=_cparams(("arbitrary",)),
        name="attention_delta",
    )(do, o)


TN_DIMS = (((0,), (0,)), ((), ()))


def attention_bwd(q, k, v, do, lse_row, delta_row, name, carry=None):
    s_len = q.shape[0]
    t = min(ATT_TILE_BWD, s_len)
    nb = s_len // t
    hb = ATT_HEADS
    w = hb * HP
    nh = N_HEADS // hb

    def body(*refs):
        (q_ref, k_ref, v_ref, do_ref, lse_ref, dl_ref, dq_ref, dk_ref, dv_ref, dk_sc, dv_sc), start, finish = _carried(carry, refs, 6, 3, 2)
        ki = pl.program_id(1)
        if start is not None:
            pl.when((pl.program_id(0) == 0) & (ki == 0))(start)

        @pl.when(ki == 0)
        def _():
            dq_ref[...] = jnp.zeros_like(dq_ref)

        dk_sc[...] = jnp.zeros_like(dk_sc)
        dv_sc[...] = jnp.zeros_like(dv_sc)

        def block(j, masked):
            qs = pl.ds(pl.multiple_of(j * t, t), t)
            for hh in range(hb):
                ls = slice(hh * HP, (hh + 1) * HP)
                qb = q_ref[qs, ls]
                dob = do_ref[qs, ls]
                kb = k_ref[:, ls]
                st = lax.dot_general(kb, qb, NT_DIMS, preferred_element_type=F32)
                pt = jnp.exp(st - lse_ref[hh, j])
                if masked:
                    r, c = _causal_mask(t)
                    pt = jnp.where(r <= c, pt, 0.0)
                dv_sc[hh] += jnp.dot(pt.astype(BF16), dob, preferred_element_type=F32)
                dpt = lax.dot_general(v_ref[:, ls], dob, NT_DIMS, preferred_element_type=F32)
                dst = (pt * (dpt - dl_ref[hh, j])).astype(BF16)
                dk_sc[hh] += jnp.dot(dst, qb, preferred_element_type=F32)
                dq_ref[qs, ls] += lax.dot_general(dst, kb, TN_DIMS, preferred_element_type=F32)

        block(ki, True)

        def loop_body(j, carry):
            block(j, False)
            return carry

        lax.fori_loop(ki + 1, nb, loop_body, 0)
        for hh in range(hb):
            ls = slice(hh * HP, (hh + 1) * HP)
            dk_ref[:, ls] = dk_sc[hh].astype(dk_ref.dtype)
            dv_ref[:, ls] = dv_sc[hh].astype(dv_ref.dtype)
        if finish is not None:
            pl.when((pl.program_id(0) == nh - 1) & (ki == nb - 1))(finish)

    ex_in_specs, ex_out_specs, ex_out_shape, ex_scratch, ex_inputs = _carry_specs(carry)
    k_spec = pl.BlockSpec((t, w), lambda h, ki: (ki, h))
    resident = pl.BlockSpec((s_len, w), lambda h, ki: (0, h))
    row_spec = pl.BlockSpec((hb, nb, 1, t), lambda h, ki: (h, 0, 0, 0))
    dq, dk, dv, *carried = pl.pallas_call(
        body,
        grid=(nh, nb),
        in_specs=[resident, k_spec, k_spec, resident, row_spec, row_spec] + ex_in_specs,
        out_specs=[resident, k_spec, k_spec] + ex_out_specs,
        out_shape=[jax.ShapeDtypeStruct((s_len, HW), F32), jax.ShapeDtypeStruct((s_len, HW), BF16), jax.ShapeDtypeStruct((s_len, HW), BF16)]
        + ex_out_shape,
        scratch_shapes=[pltpu.VMEM((hb, t, HP), F32), pltpu.VMEM((hb, t, HP), F32)] + ex_scratch,
        compiler_params=_cparams(("arbitrary", "arbitrary")),
        name=name,
    )(q, k, v, do, lse_row, delta_row, *ex_inputs)
    return dq, dk, dv, carried


CONV_PAD = 32


def conv_fwd(z, conv_w, conv_b):
    s_len = z.shape[0]
    ch = min(CONV_CHUNK, s_len)

    def body(ag_ref, w_ref, b_ref, c_ref, pad_ref):
        pad_ref[0:CONV_PAD, :] = jnp.zeros((CONV_PAD, LANE), F32)
        pad_ref[CONV_PAD:CONV_PAD + s_len, :] = ag_ref[:, 0:LANE].astype(F32) * jax.nn.sigmoid(ag_ref[:, LANE:2 * LANE].astype(F32))

        def chunk(i, carry):
            base = pl.multiple_of(i * ch, ch)
            acc = jnp.zeros((ch, LANE), F32) + b_ref[...]
            for kk in range(CONV_W):
                acc = acc + pad_ref[pl.ds(base + CONV_PAD - (CONV_W - 1) + kk, ch), :] * w_ref[kk:kk + 1, :]
            c_ref[pl.ds(base, ch), :] = acc
            return carry

        lax.fori_loop(0, s_len // ch, chunk, 0)

    nblk = CONV_C // LANE
    return pl.pallas_call(
        body,
        grid=(nblk,),
        in_specs=[pl.BlockSpec((s_len, 2 * LANE), lambda j: (0, ZC // (2 * LANE) + j)),
                  pl.BlockSpec((CONV_W, LANE), lambda j: (0, j)), pl.BlockSpec((1, LANE), lambda j: (0, j))],
        out_specs=pl.BlockSpec((s_len, LANE), lambda j: (0, j)),
        out_shape=jax.ShapeDtypeStruct((s_len, CONV_C), F32),
        scratch_shapes=[pltpu.VMEM((s_len + CONV_PAD, LANE), F32)],
        compiler_params=_cparams(("arbitrary",)),
        name="conv_fwd",
    )(z, conv_w, conv_b)


def conv_bwd(z, dc, conv_w, dz):
    s_len = z.shape[0]
    ch = min(CONV_CHUNK, s_len)

    def body(ag_ref, dc_ref, w_ref, dz_in, dag_ref, dw_ref, db_ref, pad_ref, dpad_ref, wacc_ref):
        del dz_in
        pad_ref[0:CONV_PAD, :] = jnp.zeros((CONV_PAD, LANE), F32)
        pad_ref[CONV_PAD:CONV_PAD + s_len, :] = ag_ref[:, 0:LANE].astype(F32) * jax.nn.sigmoid(ag_ref[:, LANE:2 * LANE].astype(F32))
        dpad_ref[0:s_len, :] = dc_ref[...]
        dpad_ref[s_len:s_len + CONV_PAD, :] = jnp.zeros((CONV_PAD, LANE), F32)
        wacc_ref[...] = jnp.zeros_like(wacc_ref)
        db_ref[...] = jnp.sum(dc_ref[...], axis=0, keepdims=True)

        def chunk(i, carry):
            base = pl.multiple_of(i * ch, ch)
            dcc = dpad_ref[pl.ds(base, ch), :]
            dh = jnp.zeros((ch, LANE), F32)
            for kk in range(CONV_W):
                dh = dh + dpad_ref[pl.ds(base + (CONV_W - 1) - kk, ch), :] * w_ref[kk:kk + 1, :]
                prod = dcc * pad_ref[pl.ds(base + CONV_PAD - (CONV_W - 1) + kk, ch), :]
                wacc_ref[kk * 8:(kk + 1) * 8, :] += prod.reshape(ch // 8, 8, LANE).sum(axis=0)
            a = ag_ref[pl.ds(base, ch), 0:LANE].astype(F32)
            sgc = jax.nn.sigmoid(ag_ref[pl.ds(base, ch), LANE:2 * LANE].astype(F32))
            dag_ref[pl.ds(base, ch), 0:LANE] = (dh * sgc).astype(dag_ref.dtype)
            dag_ref[pl.ds(base, ch), LANE:2 * LANE] = (dh * a * sgc * (1.0 - sgc)).astype(dag_ref.dtype)
            return carry

        lax.fori_loop(0, s_len // ch, chunk, 0)
        for kk in range(CONV_W):
            dw_ref[kk:kk + 1, :] = jnp.sum(wacc_ref[kk * 8:(kk + 1) * 8, :], axis=0, keepdims=True)

    nblk = CONV_C // LANE
    pair = pl.BlockSpec((s_len, 2 * LANE), lambda j: (0, ZC // (2 * LANE) + j))
    return pl.pallas_call(
        body,
        grid=(nblk,),
        in_specs=[pair, pl.BlockSpec((s_len, LANE), lambda j: (0, j)), pl.BlockSpec((CONV_W, LANE), lambda j: (0, j)), ANY],
        out_specs=[pair, pl.BlockSpec((CONV_W, LANE), lambda j: (0, j)), pl.BlockSpec((1, LANE), lambda j: (0, j))],
        out_shape=[jax.ShapeDtypeStruct(dz.shape, dz.dtype), jax.ShapeDtypeStruct((CONV_W, CONV_C), F32), jax.ShapeDtypeStruct((1, CONV_C), F32)],
        scratch_shapes=[pltpu.VMEM((s_len + CONV_PAD, LANE), F32), pltpu.VMEM((s_len + CONV_PAD, LANE), F32),
                        pltpu.VMEM((CONV_W * 8, LANE), F32)],
        input_output_aliases={3: 0},
        compiler_params=_cparams(("arbitrary",)),
        name="conv_bwd",
    )(z, dc, conv_w, dz)


POOL_PAD = 16


def _pool_count(base, ch, w):
    t = base + lax.broadcasted_iota(jnp.int32, (ch, 1), 0)
    return jnp.minimum(t + 1, w).astype(F32)


def pool_fwd(z, pool_w, pool_scale):
    s_len = z.shape[0]
    ch = min(CONV_CHUNK, s_len)

    def body(u_ref, pw_ref, sc_ref, m_ref, pad_ref):
        gi = pl.program_id(0)
        pad_ref[0:POOL_PAD, :] = jnp.zeros((POOL_PAD, LANE), F32)
        pad_ref[POOL_PAD:POOL_PAD + s_len, :] = u_ref[...].astype(F32)

        def run(w):
            def chunk(i, carry):
                base = pl.multiple_of(i * ch, ch)
                acc = jnp.zeros((ch, LANE), F32)
                for j in range(w):
                    acc = acc + pad_ref[pl.ds(base + POOL_PAD - j, ch), :]
                d = acc / _pool_count(base, ch, w) - pad_ref[pl.ds(base + POOL_PAD, ch), :]
                md = jnp.dot(d.astype(BF16), pw_ref[0], preferred_element_type=F32)
                m_ref[pl.ds(base, ch), :] = (md * sc_ref[...]).astype(m_ref.dtype)
                return carry

            lax.fori_loop(0, s_len // ch, chunk, 0)

        for g, w in enumerate(POOL_WINDOWS):
            pl.when(gi == g)(functools.partial(run, w))

    return pl.pallas_call(
        body,
        grid=(POOL_G,),
        in_specs=[pl.BlockSpec((s_len, LANE), lambda g: (0, ZP // LANE + g)), pl.BlockSpec((1, POOL_GD, POOL_GD), lambda g: (g, 0, 0)),
                  pl.BlockSpec((1, LANE), lambda g: (0, g))],
        out_specs=pl.BlockSpec((s_len, LANE), lambda g: (0, g)),
        out_shape=jax.ShapeDtypeStruct((s_len, POOL_C), BF16),
        scratch_shapes=[pltpu.VMEM((s_len + POOL_PAD, LANE), F32)],
        compiler_params=_cparams(("arbitrary",)),
        name="pool_fwd",
    )(z, pool_w, pool_scale)


def pool_bwd(z, dm, pool_w, pool_scale, dz):
    s_len = z.shape[0]
    ch = min(CONV_CHUNK, s_len)

    def body(u_ref, dm_ref, pw_ref, sc_ref, dz_in, du_ref, dpw_ref, dsc_ref, pad_ref, epad_ref, dd_ref, sacc_ref):
        del dz_in
        gi = pl.program_id(0)
        pad_ref[0:POOL_PAD, :] = jnp.zeros((POOL_PAD, LANE), F32)
        pad_ref[POOL_PAD:POOL_PAD + s_len, :] = u_ref[...].astype(F32)
        epad_ref[s_len:s_len + POOL_PAD, :] = jnp.zeros((POOL_PAD, LANE), F32)
        dpw_ref[...] = jnp.zeros_like(dpw_ref)
        sacc_ref[...] = jnp.zeros_like(sacc_ref)

        def run(w):
            def first(i, carry):
                base = pl.multiple_of(i * ch, ch)
                acc = jnp.zeros((ch, LANE), F32)
                for j in range(w):
                    acc = acc + pad_ref[pl.ds(base + POOL_PAD - j, ch), :]
                cnt = _pool_count(base, ch, w)
                d = (acc / cnt - pad_ref[pl.ds(base + POOL_PAD, ch), :]).astype(BF16)
                md = jnp.dot(d, pw_ref[0], preferred_element_type=F32)
                dmc = dm_ref[pl.ds(base, ch), :]
                sacc_ref[...] += (dmc * md).reshape(ch // 8, 8, LANE).sum(axis=0)
                dmd = (dmc * sc_ref[...]).astype(BF16)
                dpw_ref[0] += lax.dot_general(d, dmd, (((0,), (0,)), ((), ())), preferred_element_type=F32)
                dd = lax.dot_general(dmd, pw_ref[0], (((1,), (1,)), ((), ())), preferred_element_type=F32)
                dd_ref[pl.ds(base, ch), :] = dd
                epad_ref[pl.ds(base, ch), :] = dd / cnt
                return carry

            lax.fori_loop(0, s_len // ch, first, 0)

            def second(i, carry):
                base = pl.multiple_of(i * ch, ch)
                acc = jnp.zeros((ch, LANE), F32)
                for j in range(w):
                    acc = acc + epad_ref[pl.ds(base + j, ch), :]
                du_ref[pl.ds(base, ch), :] = (acc - dd_ref[pl.ds(base, ch), :]).astype(du_ref.dtype)
                return carry

            lax.fori_loop(0, s_len // ch, second, 0)

        for g, w in enumerate(POOL_WINDOWS):
            pl.when(gi == g)(functools.partial(run, w))
        dsc_ref[...] = jnp.sum(sacc_ref[...], axis=0, keepdims=True)

    return pl.pallas_call(
        body,
        grid=(POOL_G,),
        in_specs=[pl.BlockSpec((s_len, LANE), lambda g: (0, ZP // LANE + g)), pl.BlockSpec((s_len, LANE), lambda g: (0, g)),
                  pl.BlockSpec((1, POOL_GD, POOL_GD), lambda g: (g, 0, 0)), pl.BlockSpec((1, LANE), lambda g: (0, g)), ANY],
        out_specs=[pl.BlockSpec((s_len, LANE), lambda g: (0, ZP // LANE + g)), pl.BlockSpec((1, POOL_GD, POOL_GD), lambda g: (g, 0, 0)),
                   pl.BlockSpec((1, LANE), lambda g: (0, g))],
        out_shape=[jax.ShapeDtypeStruct(dz.shape, dz.dtype), jax.ShapeDtypeStruct((POOL_G, POOL_GD, POOL_GD), F32),
                   jax.ShapeDtypeStruct((1, POOL_C), F32)],
        scratch_shapes=[pltpu.VMEM((s_len + POOL_PAD, LANE), F32), pltpu.VMEM((s_len + POOL_PAD, LANE), F32),
                        pltpu.VMEM((s_len, LANE), F32), pltpu.VMEM((8, LANE), F32)],
        input_output_aliases={4: 0},
        compiler_params=_cparams(("arbitrary",)),
        name="pool_bwd",
    )(z, dm, pool_w, pool_scale, dz)


def _row(v):
    return v.reshape(1, -1)


def _rms_body(x_ref, g_ref, o_ref):
    o_ref[...] = _rms(x_ref[...], g_ref[...]).astype(o_ref.dtype)


def _post_body(y_ref, x_ref, g_ref, o_ref):
    o_ref[...] = x_ref[...] + _rms(y_ref[...], g_ref[...])


def _post_bwd_body(y_ref, dh_ref, g_ref, dy_ref, dg_ref):
    _, vjp = jax.vjp(_rms, y_ref[...], g_ref[...])
    dy, dg = vjp(dh_ref[...])
    dy_ref[...] = dy.astype(dy_ref.dtype)
    _acc(dg_ref, dg)


def _pre_bwd_body(x_ref, dhn_ref, dres_ref, g_ref, dx_ref, dg_ref):
    _, vjp = jax.vjp(_rms, x_ref[...], g_ref[...])
    dx, dg = vjp(dhn_ref[...])
    dx_ref[...] = dres_ref[...] + dx
    _acc(dg_ref, dg)


def _post_next_body(y_ref, x_ref, g_ref, gn_ref, o_ref, n_ref):
    o = x_ref[...] + _rms(y_ref[...], g_ref[...])
    o_ref[...] = o
    n_ref[...] = _rms(o, gn_ref[...]).astype(n_ref.dtype)


def _pre_post_bwd_body(x_ref, dhn_ref, dres_ref, y_ref, g_ref, gy_ref, dx_ref, dy_ref, dg_ref, dgy_ref):
    _, vjp = jax.vjp(_rms, x_ref[...], g_ref[...])
    dx, dg = vjp(dhn_ref[...])
    dx = dres_ref[...] + dx
    dx_ref[...] = dx
    _, vjp_y = jax.vjp(_rms, y_ref[...], gy_ref[...])
    dy, dgy = vjp_y(dx)
    dy_ref[...] = dy.astype(dy_ref.dtype)
    _acc(dg_ref, dg)
    _acc(dgy_ref, dgy)


def mixer_fwd(x, tabs, w, tag, carry=None, head=None, h=None):
    s_len = x.shape[0]
    cc, sa, sb = tabs
    sv = {"x": x}

    if h is None:
        (h,) = rowwise("mix_norm_pre" + tag, _rms_body, s_len, [_whole(x)], [_row(w["mix_norm_pre"])], [(D_MODEL, BF16)])
    if head is None:
        z = mm(h, w["w_in"], out_dtype=BF16, name="in_proj" + tag)
    else:
        z, arrived = mm(h, w["w_in"], out_dtype=BF16, name="in_proj" + tag, carry=head[0])
        w = {**w, **head[1](arrived)}

    def prep_body(z_ref, cc_ref, sa_ref, sb_ref, qg_ref, kg_ref, qn_ref, ckv_ref, kr_ref):
        qn_ref[...] = _rms(z_ref[:, 0:Q_RANK].astype(F32), qg_ref[...]).astype(qn_ref.dtype)
        ckv_ref[...] = _rms(z_ref[:, Q_RANK:Q_RANK + KV_RANK].astype(F32), kg_ref[...]).astype(ckv_ref.dtype)
        kr_ref[...] = _rope(z_ref[:, Q_RANK + KV_RANK:ZA_W].astype(F32), cc_ref[...], sa_ref[...], sb_ref[...])

    qn, ckvn, kr = rowwise("attn_prep" + tag, prep_body, s_len, [(z, ZA_W, ZA // ZA_W), _whole(cc), _whole(sa), _whole(sb)],
                           [_row(w["q_norm"]), _row(w["kv_norm"])], [(Q_RANK, BF16), (KV_RANK, BF16), (HP, F32)])
    q_raw = mm(qn, w["w_uq"], out_dtype=BF16, name="q_proj" + tag)
    kv_raw = mm(ckvn, w["w_ukv"], out_dtype=BF16, name="kv_proj" + tag)

    def qkv_body(q_ref, kv_ref, kr_ref, cc_ref, sa_ref, sb_ref, qo_ref, ko_ref, vo_ref):
        c_, a_, b_, kro = cc_ref[...], sa_ref[...], sb_ref[...], kr_ref[...]
        for hh in range(N_HEADS):
            sl = slice(hh * HP, (hh + 1) * HP)
            qo_ref[:, sl] = (_rope(q_ref[:, sl].astype(F32), c_, a_, b_) * ATT_SCALE).astype(qo_ref.dtype)
            ko_ref[:, sl] = (kv_ref[:, sl].astype(F32) + kro).astype(ko_ref.dtype)
        lane = lax.broadcasted_iota(jnp.int32, (q_ref.shape[0], HW), 1)
        vo_ref[...] = jnp.where((lane & (HP - 1)) == VDIM, 1.0, kv_ref[:, HW:2 * HW].astype(F32)).astype(vo_ref.dtype)

    q, k, v = rowwise("qkv_rope" + tag, qkv_body, s_len, [_whole(q_raw), _whole(kv_raw), _whole(kr), _whole(cc), _whole(sa), _whole(sb)], [],
                      [(HW, BF16)] * 3)
    o, lse, carried = attention_fwd(q, k, v, "attention_fwd" + tag, carry)
    y_attn = mm(o, w["w_attn_o"], out_dtype=BF16, name="attn_out" + tag)

    c = conv_fwd(z, w["conv_w"], _row(w["conv_b"]))

    def ln_body(c_ref, g_ref, b_ref, o_ref):
        o_ref[...] = _silu(_layer_norm(c_ref[...], g_ref[...], b_ref[...])).astype(o_ref.dtype)

    (cs,) = rowwise("conv_ln_silu" + tag, ln_body, s_len, [_whole(c)], [_row(w["conv_ln_g"]), _row(w["conv_ln_b"])], [(CONV_C, BF16)])
    y_conv = mm(cs, w["w_conv_o"], out_dtype=BF16, name="conv_out" + tag)

    m = pool_fwd(z, w["pool_w"], _row(w["pool_scale"]))
    y_pool = mm(m, w["w_pool_o"], out_dtype=BF16, name="pool_out" + tag)

    def merge_body(ya_ref, yc_ref, yp_ref, gl_ref, o_ref):
        gl = gl_ref[...].astype(F32)
        o_ref[...] = (jax.nn.sigmoid(gl[:, 0:D_MODEL]) * ya_ref[...].astype(F32) + jax.nn.sigmoid(gl[:, D_MODEL:2 * D_MODEL]) * yc_ref[...].astype(F32)
                      + jax.nn.sigmoid(gl[:, 2 * D_MODEL:3 * D_MODEL]) * yp_ref[...].astype(F32)).astype(o_ref.dtype)

    (merged,) = rowwise("gate_merge" + tag, merge_body, s_len, [_whole(y_attn), _whole(y_conv), _whole(y_pool), (z, 3 * D_MODEL, 0)], [],
                        [(D_MODEL, BF16)])
    mo = mm(merged, w["w_mix_o"], name="mix_out" + tag)

    h1, hn = rowwise("mix_norm_post" + tag, _post_next_body, s_len, [_whole(mo), _whole(x)],
                     [_row(w["mix_norm_post"]), _row(w["ffn_norm_pre"])], [(D_MODEL, F32), (D_MODEL, BF16)])
    sv.update(h=h, z=z, qn=qn, ckvn=ckvn, q=q, k=k, v=v, o=o, lse=lse, c=c, cs=cs, m=m, y_attn=y_attn, y_conv=y_conv, y_pool=y_pool,
              merged=merged, mo=mo, w=w, hn=hn)
    return h1, sv, carried


def ffn_fwd(h1, hn, w, tag, next_gain=None):
    s_len = h1.shape[0]
    gu, act = ffn_in(hn, w["w_gu"], "ffn_in" + tag)
    y = mm(act, w["w_down"], name="ffn_out" + tag)
    if next_gain is None:
        (h2,) = rowwise("ffn_norm_post" + tag, _post_body, s_len, [_whole(y), _whole(h1)], [_row(w["ffn_norm_post"])], [(D_MODEL, F32)])
        h_next = None
    else:
        h2, h_next = rowwise("ffn_norm_post" + tag, _post_next_body, s_len, [_whole(y), _whole(h1)],
                             [_row(w["ffn_norm_post"]), _row(next_gain)], [(D_MODEL, F32), (D_MODEL, BF16)])
    return h2, dict(h1=h1, hn=hn, gu=gu, act=act, y=y), h_next


def ffn_bwd(dh2, sv, w, tag, mo, post=None, first=None):
    s_len = dh2.shape[0]
    g = {}
    if post is None:
        d_y, g["ffn_norm_post"] = rowwise("ffn_norm_post_bwd" + tag, _post_bwd_body, s_len, [_whole(sv["y"]), _whole(dh2)],
                                          [_row(w["ffn_norm_post"])], [(D_MODEL, BF16)], [((1, D_MODEL), F32)])
    else:
        d_y, g["ffn_norm_post"] = post
    swapped = None
    if first is None:
        g["w_down"] = mm(sv["act"], d_y, ta=True, name="ffn_out_dw" + tag)
    else:
        g["w_down"], swapped = mm(sv["act"], d_y, ta=True, name="ffn_out_dw" + tag, carry=first)
    d_gu = ffn_out_dx(d_y, w["w_down"], sv["gu"], "ffn_out_dx" + tag)
    g["w_gu"] = mm(sv["hn"], d_gu, ta=True, name="ffn_in_dw" + tag)
    d_hn = mm(d_gu, w["w_gu"], tb=True, name="ffn_in_dx" + tag)
    dh1, d_mo, g["ffn_norm_pre"], d_mix_post = rowwise(
        "ffn_norm_pre_bwd" + tag, _pre_post_bwd_body, s_len, [_whole(sv["h1"]), _whole(d_hn), _whole(dh2), _whole(mo)],
        [_row(w["ffn_norm_pre"]), _row(w["mix_norm_post"])], [(D_MODEL, F32), (D_MODEL, BF16)], [((1, D_MODEL), F32), ((1, D_MODEL), F32)])
    return dh1, g, (d_mo, d_mix_post), swapped


def mixer_bwd(dh1, sv, tabs, w, tag, post, carry=None, tail=None, prev=None, first=None):
    s_len = dh1.shape[0]
    cc, sa, sb = tabs
    g = {}

    d_mo, g["mix_norm_post"] = post
    if first is None:
        g["w_mix_o"] = mm(sv["merged"], d_mo, ta=True, name="mix_out_dw" + tag)
    else:
        g["w_mix_o"], swapped = mm(sv["merged"], d_mo, ta=True, name="mix_out_dw" + tag, carry=first)
        carry = carry(swapped)
    d_merged = mm(d_mo, w["w_mix_o"], tb=True, name="mix_out_dx" + tag)

    def merge_bwd_body(dm_ref, ya_ref, yc_ref, yp_ref, gl_ref, dya_ref, dyc_ref, dyp_ref, dgl_ref):
        dmg = dm_ref[...]
        for i, (y_ref, dy_ref) in enumerate(((ya_ref, dya_ref), (yc_ref, dyc_ref), (yp_ref, dyp_ref))):
            sg = jax.nn.sigmoid(gl_ref[:, i * D_MODEL:(i + 1) * D_MODEL].astype(F32))
            dy_ref[...] = (dmg * sg).astype(dy_ref.dtype)
            dgl_ref[:, i * D_MODEL:(i + 1) * D_MODEL] = (dmg * y_ref[...].astype(F32) * sg * (1.0 - sg)).astype(dgl_ref.dtype)

    d_ya, d_yc, d_yp, dz = rowwise(
        "gate_merge_bwd" + tag, merge_bwd_body, s_len,
        [_whole(d_merged), _whole(sv["y_attn"]), _whole(sv["y_conv"]), _whole(sv["y_pool"]), (sv["z"], 3 * D_MODEL, 0)], [],
        [(D_MODEL, BF16)] * 3 + [(3 * D_MODEL, BF16, ZW, 0)], tile=WIDE_ROW_TILE)

    g["w_pool_o"] = mm(sv["m"], d_yp, ta=True, name="pool_out_dw" + tag)
    d_m = mm(d_yp, w["w_pool_o"], tb=True, name="pool_out_dx" + tag)
    dz, g["pool_w"], g["pool_scale"] = pool_bwd(sv["z"], d_m, w["pool_w"], _row(w["pool_scale"]), dz)

    g["w_conv_o"] = mm(sv["cs"], d_yc, ta=True, name="conv_out_dw" + tag)
    d_cs = mm(d_yc, w["w_conv_o"], tb=True, name="conv_out_dx" + tag)

    def ln_bwd_body(c_ref, dcs_ref, g_ref, b_ref, dc_ref, dg_ref, db_ref):
        f = lambda c_, g_, b_: _silu(_layer_norm(c_, g_, b_))
        _, vjp = jax.vjp(f, c_ref[...], g_ref[...], b_ref[...])
        dc, dg, db = vjp(dcs_ref[...])
        dc_ref[...] = dc
        _acc(dg_ref, dg)
        _acc(db_ref, db)

    d_c, g["conv_ln_g"], g["conv_ln_b"] = rowwise("conv_ln_silu_bwd" + tag, ln_bwd_body, s_len, [_whole(sv["c"]), _whole(d_cs)],
                                                  [_row(w["conv_ln_g"]), _row(w["conv_ln_b"])], [(CONV_C, F32)],
                                                  [((1, CONV_C), F32), ((1, CONV_C), F32)])
    dz, g["conv_w"], g["conv_b"] = conv_bwd(sv["z"], d_c, w["conv_w"], dz)

    g["w_attn_o"] = mm(sv["o"], d_ya, ta=True, name="attn_out_dw" + tag)
    d_o = mm(d_ya, w["w_attn_o"], tb=True, out_dtype=BF16, name="attn_out_dx" + tag)
    delta = attention_delta(d_o, sv["o"])
    t_bwd = min(ATT_TILE_BWD, s_len)
    rows_of = lambda a: a.reshape(N_HEADS, s_len // t_bwd, 1, t_bwd)
    dq, dk, dv, carried = attention_bwd(sv["q"], sv["k"], sv["v"], d_o, rows_of(sv["lse"]), rows_of(delta), "attention_bwd" + tag, carry)

    def qkv_bwd_body(dq_ref, dk_ref, dv_ref, cc_ref, sa_ref, sb_ref, dqp_ref, dkv_ref, dkr_ref):
        c_, a_, b_ = cc_ref[...], sa_ref[...], sb_ref[...]
        dk_sum = jnp.zeros((dq_ref.shape[0], HP), F32)
        for hh in range(N_HEADS):
            sl = slice(hh * HP, (hh + 1) * HP)
            dqp_ref[:, sl] = (_rope_t(dq_ref[:, sl], c_, a_, b_) * ATT_SCALE).astype(dqp_ref.dtype)
            dkh = dk_ref[:, sl].astype(F32)
            dkv_ref[:, sl] = dk_ref[:, sl]
            dk_sum = dk_sum + dkh
        dkv_ref[:, HW:2 * HW] = dv_ref[...]
        dkr_ref[...] = _rope_t(dk_sum, c_, a_, b_)

    dq_pre, dkv_pre, d_kr = rowwise("qkv_rope_bwd" + tag, qkv_bwd_body, s_len,
                                    [_whole(dq), _whole(dk), _whole(dv), _whole(cc), _whole(sa), _whole(sb)], [],
                                    [(HW, BF16), (2 * HW, BF16), (HP, F32)])
    g["w_uq"] = mm(sv["qn"], dq_pre, ta=True, name="q_proj_dw" + tag)
    d_qn = mm(dq_pre, w["w_uq"], tb=True, name="q_proj_dx" + tag)
    g["w_ukv"] = mm(sv["ckvn"], dkv_pre, ta=True, name="kv_proj_dw" + tag)
    d_ckvn = mm(dkv_pre, w["w_ukv"], tb=True, name="kv_proj_dx" + tag)

    def prep_bwd_body(z_ref, dqn_ref, dckv_ref, dkr_ref, qg_ref, kg_ref, dz_ref, dqg_ref, dkg_ref):
        _, vq = jax.vjp(_rms, z_ref[:, 0:Q_RANK].astype(F32), qg_ref[...])
        dcq, dqg = vq(dqn_ref[...])
        _, vk = jax.vjp(_rms, z_ref[:, Q_RANK:Q_RANK + KV_RANK].astype(F32), kg_ref[...])
        dckv, dkg = vk(dckv_ref[...])
        dz_ref[:, 0:Q_RANK] = dcq.astype(dz_ref.dtype)
        dz_ref[:, Q_RANK:Q_RANK + KV_RANK] = dckv.astype(dz_ref.dtype)
        dz_ref[:, Q_RANK + KV_RANK:ZA_W] = dkr_ref[...].astype(dz_ref.dtype)
        _acc(dqg_ref, dqg)
        _acc(dkg_ref, dkg)

    dz, g["q_norm"], g["kv_norm"] = rowwise("attn_prep_bwd" + tag, prep_bwd_body, s_len,
                                            [(sv["z"], ZA_W, ZA // ZA_W), _whole(d_qn), _whole(d_ckvn), _whole(d_kr)],
                                            [_row(w["q_norm"]), _row(w["kv_norm"])], [(ZA_W, BF16, ZW, ZA // ZA_W)],
                                            [((1, Q_RANK), F32), ((1, KV_RANK), F32)], into={0: dz})

    if tail is None:
        g["w_in"] = mm(sv["h"], dz, ta=True, name="in_proj_dw" + tag)
        d_h = mm(dz, w["w_in"], tb=True, name="in_proj_dx" + tag)
    else:
        g["w_in"], tailed_rest = mm(sv["h"], dz, ta=True, name="in_proj_dw" + tag, carry=tail[0](g))
        d_h, tailed_in = mm(dz, w["w_in"], tb=True, name="in_proj_dx" + tag, carry=tail[1](g))
        carried = [carried, tailed_in + tailed_rest]
    if prev is None:
        dx, g["mix_norm_pre"] = rowwise(
            "mix_norm_pre_bwd" + tag, _pre_bwd_body, s_len, [_whole(sv["x"]), _whole(d_h), _whole(dh1)], [_row(w["mix_norm_pre"])], [(D_MODEL, F32)], [((1, D_MODEL), F32)])
        return dx, g, carried, None
    y_prev, gain_prev = prev
    dx, d_y_prev, g["mix_norm_pre"], d_gain_prev = rowwise(
        "mix_norm_pre_bwd" + tag, _pre_post_bwd_body, s_len, [_whole(sv["x"]), _whole(d_h), _whole(dh1), _whole(y_prev)],
        [_row(w["mix_norm_pre"]), _row(gain_prev)], [(D_MODEL, F32), (D_MODEL, BF16)], [((1, D_MODEL), F32), ((1, D_MODEL), F32)])
    return dx, g, carried, (d_y_prev, d_gain_prev)


def loss_head(h, target):
    s_len = h.shape[0]

    def body(h_ref, t_ref, dy_ref, loss_ref):
        err = h_ref[...] - t_ref[...]
        dy_ref[...] = err * (1.0 / D_MODEL)
        part = 0.5 * jnp.sum(jnp.mean(err * err, axis=-1, keepdims=True), axis=0, keepdims=True)
        _acc(loss_ref, jnp.broadcast_to(part, (1, LANE)))

    return rowwise("loss_head", body, s_len, [_whole(h), _whole(target)], [], [(D_MODEL, F32)], [((1, LANE), F32)])


def local_step(x, pos_col, target, layers):
    s_len = x.shape[0]
    tabs = rope_tables(pos_col, s_len)
    h, h_normed, saved = x, None, []
    for li, w in enumerate(layers):
        h, sv_mix, _ = mixer_fwd(h, tabs, w, f"_l{li}", h=h_normed)
        next_gain = layers[li + 1]["mix_norm_pre"] if li + 1 < len(layers) else None
        h, sv_ffn, h_normed = ffn_fwd(h, sv_mix["hn"], w, f"_l{li}", next_gain)
        saved.append((sv_mix, sv_ffn))
    dh, loss = loss_head(h, target)
    grads, post = [None] * len(layers), None
    for li in reversed(range(len(layers))):
        dh, g_ffn, mix_post, _ = ffn_bwd(dh, saved[li][1], layers[li], f"_l{li}", saved[li][0]["mo"], post)
        prev = (saved[li - 1][1]["y"], layers[li - 1]["ffn_norm_post"]) if li > 0 else None
        dh, g_mix, _, post = mixer_bwd(dh, saved[li][0], tabs, layers[li], f"_l{li}", mix_post, prev=prev)
        grads[li] = {**g_mix, **g_ffn}
    return loss[0, 0], dh, grads


def _pad_heads_cols(wm, per_head):
    r = wm.shape[0]
    return jnp.pad(wm.reshape(r, N_HEADS, per_head), ((0, 0), (0, 0), (0, HP - per_head))).reshape(r, HW)


def _unpad_heads_cols(wm, per_head):
    r = wm.shape[0]
    return wm.reshape(r, N_HEADS, HP)[:, :, :per_head].reshape(r, N_HEADS * per_head)


def align_weights(p):
    out = dict(p)
    if "w_in" in p:
        w_in = p["w_in"]
        r = w_in.shape[0]
        zeros = lambda n: jnp.zeros((r, n), w_in.dtype)
        conv = w_in[:, O_CONV:O_POOL].reshape(r, 2, CONV_C // LANE, LANE).transpose(0, 2, 1, 3).reshape(r, 2 * CONV_C)
        out["w_in"] = jnp.concatenate([
            w_in[:, O_GATE:D_IN], conv, w_in[:, O_POOL:O_GATE], w_in[:, O_Q:O_KR],
            zeros(KR_LANE), w_in[:, O_KR:O_CONV], zeros(HP - KR_LANE - ROPE)], axis=1)
    if "w_uq" in p:
        out["w_uq"] = _pad_heads_cols(p["w_uq"], NOPE + ROPE)
        out["w_ukv"] = jnp.concatenate([_pad_heads_cols(p["w_uk"], NOPE), _pad_heads_cols(p["w_uv"], VDIM)], axis=1)
        wo = p["w_attn_o"]
        out["w_attn_o"] = jnp.pad(wo.reshape(N_HEADS, VDIM, D_MODEL), ((0, 0), (0, HP - VDIM), (0, 0))).reshape(HW, D_MODEL)
        del out["w_uk"], out["w_uv"]
    if "w_gate" in p:
        out["w_gu"] = jnp.concatenate([p["w_gate"][:, :FF_HALF], p["w_up"][:, :FF_HALF], p["w_gate"][:, FF_HALF:], p["w_up"][:, FF_HALF:]], axis=1)
        del out["w_gate"], out["w_up"]
    return out


def unalign_grads(g):
    out = dict(g)
    if "w_in" in g:
        gi = g["w_in"]
        kr0 = ZA + Q_RANK + KV_RANK + KR_LANE
        r = gi.shape[0]
        conv = gi[:, ZC:ZP].reshape(r, CONV_C // LANE, 2, LANE).transpose(0, 2, 1, 3).reshape(r, 2 * CONV_C)
        out["w_in"] = jnp.concatenate([gi[:, ZA:ZA + Q_RANK + KV_RANK], gi[:, kr0:kr0 + ROPE], conv, gi[:, ZP:ZA], gi[:, ZG:ZC]], axis=1)
    if "w_uq" in g:
        out["w_uq"] = _unpad_heads_cols(g["w_uq"], NOPE + ROPE)
        out["w_uk"] = _unpad_heads_cols(g["w_ukv"][:, :HW], NOPE)
        out["w_uv"] = _unpad_heads_cols(g["w_ukv"][:, HW:], VDIM)
        out["w_attn_o"] = g["w_attn_o"].reshape(N_HEADS, HP, D_MODEL)[:, :VDIM].reshape(N_HEADS * VDIM, D_MODEL)
        del out["w_ukv"]
    if "w_gu" in g:
        gu = g["w_gu"]
        out["w_gate"] = jnp.concatenate([gu[:, 0:FF_HALF], gu[:, 2 * FF_HALF:3 * FF_HALF]], axis=1)
        out["w_up"] = jnp.concatenate([gu[:, FF_HALF:2 * FF_HALF], gu[:, 3 * FF_HALF:]], axis=1)
        del out["w_gu"]
    return out


MESH = pl.DeviceIdType.MESH
ANY = pl.BlockSpec(memory_space=pl.ANY)


def _place():
    return lax.axis_index("x"), lax.axis_index("y"), lax.axis_index("c")


def _other_chips(x, y):
    return [(1 - x, y), (x, 1 - y), (1 - x, 1 - y)]


def _half_rows(rows, c):
    assert rows % (2 * HALF_ALIGN) == 0, rows
    return pl.ds(pl.multiple_of(c * (rows // 2), HALF_ALIGN), rows // 2)


class GatherShards:
    def __init__(self, local):
        self.ins = list(local)
        self.outs = [jax.ShapeDtypeStruct((N_CHIPS, *a.shape), a.dtype) for a in local]
        self.n_sems = 6 * len(local)
        self.base = 0

    def _first(self, in_refs, out_refs, send_sems, recv_sems):
        x, y, c = _place()
        me = 2 * x + y
        chips = _other_chips(x, y)

        def copy(i, k, slot, core, to, src=None):
            dst = out_refs[i].at[slot, _half_rows(out_refs[i].shape[1], core)]
            return pltpu.make_async_remote_copy(src_ref=dst if src is None else src, dst_ref=dst, send_sem=send_sems.at[self.base + 6 * i + k],
                                                recv_sem=recv_sems.at[self.base + 6 * i + k], device_id=to, device_id_type=MESH)

        first = [copy(i, j, me, c, (*chip, c), src=in_refs[i].at[_half_rows(in_refs[i].shape[0], c)])
                 for i in range(len(in_refs)) for j, chip in enumerate(chips)]
        return first, copy

    def start(self, in_refs, out_refs, send_sems, recv_sems):
        first, _ = self._first(in_refs, out_refs, send_sems, recv_sems)
        for cp in first:
            cp.start()

    def finish(self, in_refs, out_refs, send_sems, recv_sems):
        first, copy = self._first(in_refs, out_refs, send_sems, recv_sems)
        x, y, c = _place()
        slots = [2 * cx + cy for cx, cy in _other_chips(x, y)]
        sibling = (x, y, 1 - c)
        passed = []
        for i in range(len(in_refs)):
            for j in range(3):
                copy(i, j, slots[j], c, sibling).wait_recv()
                fwd = copy(i, 3 + j, slots[j], c, sibling)
                fwd.start()
                passed.append(fwd)
        for i in range(len(in_refs)):
            for j in range(3):
                copy(i, 3 + j, slots[j], 1 - c, sibling).wait_recv()
        for cp in first + passed:
            cp.wait_send()


class ChipExchange:
    def __init__(self, parts):
        self.ins = list(parts)
        self.outs = [jax.ShapeDtypeStruct((3, *a.shape[1:]), a.dtype) for a in parts]
        self.n_sems = 3 * len(parts)
        self.base = 0

    def _copies(self, in_refs, out_refs, send_sems, recv_sems):
        x, y, c = _place()
        return [pltpu.make_async_remote_copy(src_ref=in_refs[i].at[2 * chip[0] + chip[1]], dst_ref=out_refs[i].at[j],
                                             send_sem=send_sems.at[self.base + 3 * i + j], recv_sem=recv_sems.at[self.base + 3 * i + j],
                                             device_id=(*chip, c), device_id_type=MESH)
                for i in range(len(in_refs)) for j, chip in enumerate(_other_chips(x, y))]

    def start(self, in_refs, out_refs, send_sems, recv_sems):
        for cp in self._copies(in_refs, out_refs, send_sems, recv_sems):
            cp.start()

    def finish(self, in_refs, out_refs, send_sems, recv_sems):
        copies = self._copies(in_refs, out_refs, send_sems, recv_sems)
        for cp in copies:
            cp.wait_recv()
        for cp in copies:
            cp.wait_send()


def run_exchange(ex, name):
    n_in, n_out = len(ex.ins), len(ex.outs)

    def body(*refs):
        ins, outs, sems = refs[:n_in], refs[n_in:n_in + n_out], refs[n_in + n_out:]
        ex.start(ins, outs, *sems)
        ex.finish(ins, outs, *sems)

    return pl.pallas_call(
        body,
        in_specs=[ANY] * n_in,
        out_specs=[ANY] * n_out,
        out_shape=list(ex.outs),
        scratch_shapes=[pltpu.SemaphoreType.DMA((ex.n_sems,)), pltpu.SemaphoreType.DMA((ex.n_sems,))],
        name=name,
    )(*ex.ins)


class SiblingSwap:
    def __init__(self, gs):
        self.ins = list(gs)
        self.outs = [jax.ShapeDtypeStruct((N_CHIPS, a.shape[1] // 2, a.shape[2]), a.dtype) for a in gs]
        self.n_sems = 4 * len(gs)

    def _copies(self, in_refs, out_refs, send_sems, recv_sems):
        x, y, c = _place()
        return [pltpu.make_async_remote_copy(src_ref=in_refs[i].at[j, _half_rows(in_refs[i].shape[1], 1 - c)], dst_ref=out_refs[i].at[j],
                                             send_sem=send_sems.at[4 * i + j], recv_sem=recv_sems.at[4 * i + j],
                                             device_id=(x, y, 1 - c), device_id_type=MESH)
                for i in range(len(in_refs)) for j in range(N_CHIPS)]

    def start(self, in_refs, out_refs, send_sems, recv_sems):
        for cp in self._copies(in_refs, out_refs, send_sems, recv_sems):
            cp.start()

    def finish(self, in_refs, out_refs, send_sems, recv_sems):
        copies = self._copies(in_refs, out_refs, send_sems, recv_sems)
        for cp in copies:
            cp.wait_recv()
        for cp in copies:
            cp.wait_send()


def sibling_gather(fs):
    n = len(fs)
    layers = fs[0].shape[0]

    def body(*refs):
        out_refs, (send_sems, recv_sems) = refs[n:2 * n], refs[2 * n:]
        x, y, c = _place()

        def copy(i, l, core):
            part = out_refs[i].at[l, _half_rows(out_refs[i].shape[1], core)]
            return pltpu.make_async_remote_copy(src_ref=part, dst_ref=part, send_sem=send_sems.at[layers * i + l],
                                                recv_sem=recv_sems.at[layers * i + l], device_id=(x, y, 1 - c), device_id_type=MESH)

        sends = [copy(i, l, c) for i in range(n) for l in range(layers)]
        for cp in sends:
            cp.start()
        for i in range(n):
            for l in range(layers):
                copy(i, l, 1 - c).wait_recv()
        for cp in sends:
            cp.wait_send()

    return pl.pallas_call(
        body,
        in_specs=[ANY] * n,
        out_specs=[ANY] * n,
        out_shape=[jax.ShapeDtypeStruct(a.shape, a.dtype) for a in fs],
        scratch_shapes=[pltpu.SemaphoreType.DMA((layers * n,)), pltpu.SemaphoreType.DMA((layers * n,))],
        input_output_aliases={i: i for i in range(n)},
        name="sibling_gather",
    )(*fs)


class GatherAll:
    def __init__(self, vs):
        self.ins = list(vs)
        self.outs = [jax.ShapeDtypeStruct((8, *a.shape), a.dtype) for a in vs]
        self.n_sems = 7 * len(vs)
        self.base = 0

    def _first(self, in_refs, out_refs, send_sems, recv_sems):
        x, y, c = _place()
        me, sibling = (x, y, c), (x, y, 1 - c)

        def copy(i, k, block, to, src=None):
            px, py, pc = block
            dst = out_refs[i].at[4 * px + 2 * py + pc]
            return pltpu.make_async_remote_copy(src_ref=dst if src is None else src, dst_ref=dst, send_sem=send_sems.at[self.base + 7 * i + k],
                                                recv_sem=recv_sems.at[self.base + 7 * i + k], device_id=to, device_id_type=MESH)

        first = []
        for i in range(len(in_refs)):
            first.append(copy(i, 0, me, sibling, src=in_refs[i]))
            first += [copy(i, 1 + j, me, (*chip, c), src=in_refs[i]) for j, chip in enumerate(_other_chips(x, y))]
        return first, copy

    def start(self, in_refs, out_refs, send_sems, recv_sems):
        first, _ = self._first(in_refs, out_refs, send_sems, recv_sems)
        for cp in first:
            cp.start()

    def finish(self, in_refs, out_refs, send_sems, recv_sems):
        first, copy = self._first(in_refs, out_refs, send_sems, recv_sems)
        x, y, c = _place()
        me, sibling = (x, y, c), (x, y, 1 - c)
        chips = _other_chips(x, y)
        passed = []
        for i in range(len(in_refs)):
            for j, chip in enumerate(chips):
                copy(i, 1 + j, (*chip, c), me).wait_recv()
                fwd = copy(i, 4 + j, (*chip, c), sibling)
                fwd.start()
                passed.append(fwd)
        for i in range(len(in_refs)):
            copy(i, 0, sibling, me).wait_recv()
            for j, chip in enumerate(chips):
                copy(i, 4 + j, (*chip, 1 - c), me).wait_recv()
        for cp in first + passed:
            cp.wait_send()


def _row_tile(rows, row_bytes):
    best = None
    for t in range(16, rows + 1, 16):
        if rows % t == 0 and t * row_bytes <= SUM_TILE_BYTES:
            best = t
    return best or rows


def sibling_sum(g, theirs, place, name):
    _, half, cols = theirs.shape
    tile = _row_tile(half, cols * 4)
    nt = half // tile

    def body(place_ref, g_ref, t_ref, o_ref):
        o_ref[...] = (g_ref[...].astype(F32) + t_ref[...].astype(F32)).astype(o_ref.dtype)

    spec = pl.BlockSpec((1, tile, cols), lambda j, i, place_ref: (j, i, 0))
    return pl.pallas_call(
        body,
        grid_spec=pltpu.PrefetchScalarGridSpec(
            num_scalar_prefetch=1, grid=(N_CHIPS, nt),
            in_specs=[pl.BlockSpec((1, tile, cols), lambda j, i, place_ref: (j, place_ref[1] * nt + i, 0)), spec], out_specs=spec),
        out_shape=jax.ShapeDtypeStruct(theirs.shape, BF16),
        compiler_params=_cparams(("parallel", "parallel")),
        name=name,
    )(place, g, theirs)


def chip_sum(p, others, place, layer, into, name):
    _, half, cols = p.shape
    tile = _row_tile(half, cols * 4)
    nt = half // tile

    def body(place_ref, p_ref, o3_ref, *rest):
        o_ref = rest[-1]
        acc = p_ref[0].astype(F32)
        for k in range(3):
            acc = acc + o3_ref[k].astype(F32)
        o_ref[0] = acc

    return pl.pallas_call(
        body,
        grid_spec=pltpu.PrefetchScalarGridSpec(
            num_scalar_prefetch=1, grid=(nt,),
            in_specs=[pl.BlockSpec((1, tile, cols), lambda i, place_ref: (place_ref[0], i, 0)),
                      pl.BlockSpec((3, tile, cols), lambda i, place_ref: (0, i, 0))] + ([] if into is None else [ANY]),
            out_specs=pl.BlockSpec((1, tile, cols), lambda i, place_ref: (layer, place_ref[1] * nt + i, 0))),
        out_shape=jax.ShapeDtypeStruct((N_LAYERS, 2 * half, cols), F32),
        input_output_aliases={} if into is None else {3: 0},
        compiler_params=_cparams(("parallel",)),
        name=name,
    )(place, p, others, *([] if into is None else [into]))


def sum_devices(a, name):
    n, rows, cols = a.shape
    tile = _row_tile(rows, cols * 4 * n)

    def body(a_ref, o_ref):
        acc = a_ref[0]
        for s in range(1, n):
            acc = acc + a_ref[s]
        o_ref[...] = acc

    return pl.pallas_call(body, grid=(rows // tile,), in_specs=[pl.BlockSpec((n, tile, cols), lambda i: (0, i, 0))],
                          out_specs=pl.BlockSpec((tile, cols), lambda i: (i, 0)), out_shape=jax.ShapeDtypeStruct((rows, cols), F32),
                          compiler_params=_cparams(("parallel",)), name=name)(a)


def adamw(w, g, m, v, name):
    layers, rows, cols = w.shape
    tile = rows
    for t in range(8, rows, 8):
        if rows % t == 0 and t * cols * 4 <= ADAM_TILE_BYTES:
            tile = t

    def body(w_ref, g_ref, m_ref, v_ref, d_ref, mo_ref, vo_ref):
        gg = g_ref[...]
        m_new = ADAM_B1 * m_ref[...] + (1.0 - ADAM_B1) * gg
        v_new = ADAM_B2 * v_ref[...] + (1.0 - ADAM_B2) * (gg * gg)
        m_hat = m_new / (1.0 - ADAM_B1 ** ADAM_STEP)
        v_hat = v_new / (1.0 - ADAM_B2 ** ADAM_STEP)
        d_ref[...] = -ADAM_LR * (m_hat / (jnp.sqrt(v_hat) + ADAM_EPS) + ADAM_WD * w_ref[...])
        mo_ref[...] = m_new
        vo_ref[...] = v_new

    spec = pl.BlockSpec((1, tile, cols), lambda l, i: (l, i, 0))
    shape = jax.ShapeDtypeStruct((layers, rows, cols), F32)
    return pl.pallas_call(body, grid=(layers, rows // tile), in_specs=[spec] * 4, out_specs=[spec] * 3, out_shape=[shape] * 3,
                          compiler_params=_cparams(("parallel", "parallel")), name=name)(w, g, m, v)


WEIGHTS = ["mix_norm_pre", "w_in", "q_norm", "w_uq", "kv_norm", "w_uk", "w_uv", "w_attn_o", "conv_w", "conv_b", "conv_ln_g", "conv_ln_b",
           "w_conv_o", "pool_w", "pool_scale", "w_pool_o", "w_mix_o", "mix_norm_post", "ffn_norm_pre", "w_gate", "w_up", "w_down",
           "ffn_norm_post"]
SHARDED = {"w_in": 2, "w_uq": 2, "w_uk": 2, "w_uv": 2, "w_attn_o": 2, "conv_w": 2, "w_conv_o": 2, "w_pool_o": 2, "w_mix_o": 1,
           "w_gate": 2, "w_up": 2, "w_down": 1}
REPLICATED = [n for n in WEIGHTS if n not in SHARDED]
ROW_PARAMS = [n for n in REPLICATED if n != "pool_w"]
ROWS_MINOR = ("w_in", "w_uq", "w_gate", "w_up", "conv_w")
N_CHIPS = 4
N_MIX_GROUPS = 6
MIX_MATRICES = ("w_in", "w_uq", "w_ukv", "w_attn_o", "w_conv_o", "w_pool_o", "conv_w", "w_mix_o")
CONV_WIRE_ROWS = 32
GROUPS = [(("w_in",), 1), (("w_uq",), 1), (("w_uk", "w_uv"), 1), (("w_attn_o", "w_conv_o", "w_pool_o"), 1), (("conv_w",), 1),
          (("w_mix_o",), 1), (("w_gate", "w_up"), 2), (("w_down",), 1)]


def _join(parts, axis):
    return parts[0] if len(parts) == 1 else jnp.concatenate(parts, axis=axis)


def _split_group(arr, names, axis, shapes):
    out, off = {}, 0
    ax = arr.ndim - 3 + axis
    for n in names:
        size = shapes[n][axis]
        out[n] = lax.slice_in_dim(arr, off, off + size, axis=ax)
        off += size
    return out


def _pack_rows(vectors):
    blocks = []
    for v in vectors:
        for li in range(v.shape[0]):
            blocks.append(jnp.pad(v[li][None, :], ((0, PACK_ROWS - 1), (0, PACK_W - v.shape[1]))))
    return jnp.concatenate(blocks, axis=0)


def _unpack_rows(packed, shapes):
    out, r = [], 0
    for layers, width in shapes:
        out.append(jnp.stack([packed[r + PACK_ROWS * li, :width] for li in range(layers)]))
        r += PACK_ROWS * layers
    return out


def kernel(x, positions, mix_norm_pre, w_in, q_norm, w_uq, kv_norm, w_uk, w_uv, w_attn_o, conv_w, conv_b, conv_ln_g, conv_ln_b, w_conv_o, pool_w, pool_scale, w_pool_o, w_mix_o, mix_norm_post, ffn_norm_pre, w_gate, w_up, w_down, ffn_norm_post, loss_target, m_mix_norm_pre, m_w_in, m_q_norm, m_w_uq, m_kv_norm, m_w_uk, m_w_uv, m_w_attn_o, m_conv_w, m_conv_b, m_conv_ln_g, m_conv_ln_b, m_w_conv_o, m_pool_w, m_pool_scale, m_w_pool_o, m_w_mix_o, m_mix_norm_post, m_ffn_norm_pre, m_w_gate, m_w_up, m_w_down, m_ffn_norm_post, v_mix_norm_pre, v_w_in, v_q_norm, v_w_uq, v_kv_norm, v_w_uk, v_w_uv, v_w_attn_o, v_conv_w, v_conv_b, v_conv_ln_g, v_conv_ln_b, v_w_conv_o, v_pool_w, v_pool_scale, v_w_pool_o, v_w_mix_o, v_mix_norm_post, v_ffn_norm_pre, v_w_gate, v_w_up, v_w_down, v_ffn_norm_post):
    given = dict(mix_norm_pre=mix_norm_pre, w_in=w_in, q_norm=q_norm, w_uq=w_uq, kv_norm=kv_norm, w_uk=w_uk, w_uv=w_uv, w_attn_o=w_attn_o,
                 conv_w=conv_w, conv_b=conv_b, conv_ln_g=conv_ln_g, conv_ln_b=conv_ln_b, w_conv_o=w_conv_o, pool_w=pool_w,
                 pool_scale=pool_scale, w_pool_o=w_pool_o, w_mix_o=w_mix_o, mix_norm_post=mix_norm_post, ffn_norm_pre=ffn_norm_pre,
                 w_gate=w_gate, w_up=w_up, w_down=w_down, ffn_norm_post=ffn_norm_post)
    mom = dict(mix_norm_pre=m_mix_norm_pre, w_in=m_w_in, q_norm=m_q_norm, w_uq=m_w_uq, kv_norm=m_kv_norm, w_uk=m_w_uk, w_uv=m_w_uv,
               w_attn_o=m_w_attn_o, conv_w=m_conv_w, conv_b=m_conv_b, conv_ln_g=m_conv_ln_g, conv_ln_b=m_conv_ln_b, w_conv_o=m_w_conv_o,
               pool_w=m_pool_w, pool_scale=m_pool_scale, w_pool_o=m_w_pool_o, w_mix_o=m_w_mix_o, mix_norm_post=m_mix_norm_post,
               ffn_norm_pre=m_ffn_norm_pre, w_gate=m_w_gate, w_up=m_w_up, w_down=m_w_down, ffn_norm_post=m_ffn_norm_post)
    var = dict(mix_norm_pre=v_mix_norm_pre, w_in=v_w_in, q_norm=v_q_norm, w_uq=v_w_uq, kv_norm=v_kv_norm, w_uk=v_w_uk, w_uv=v_w_uv,
               w_attn_o=v_w_attn_o, conv_w=v_conv_w, conv_b=v_conv_b, conv_ln_g=v_conv_ln_g, conv_ln_b=v_conv_ln_b, w_conv_o=v_w_conv_o,
               pool_w=v_pool_w, pool_scale=v_pool_scale, w_pool_o=v_w_pool_o, w_mix_o=v_w_mix_o, mix_norm_post=v_mix_norm_post,
               ffn_norm_pre=v_ffn_norm_pre, w_gate=v_w_gate, w_up=v_w_up, w_down=v_w_down, ffn_norm_post=v_ffn_norm_post)
    s_len = x.shape[1]
    sharded_names = [n for n in WEIGHTS if n in SHARDED]
    chip = 2 * lax.axis_index("x") + lax.axis_index("y")
    place = jnp.stack([chip, lax.axis_index("c")]).astype(jnp.int32)
    shard_shape = {n: given[n].shape for n in sharded_names}

    mix_groups, ffn_groups = GROUPS[:N_MIX_GROUPS], GROUPS[N_MIX_GROUPS:]
    weight_wire_shape = {n: (N_LAYERS, 2 * CONV_WIRE_ROWS, shard_shape[n][2]) if n == "conv_w" else shard_shape[n] for n in sharded_names}
    grad_wire_shape = {n: (N_LAYERS, CONV_WIRE_ROWS, shard_shape[n][2]) if n == "conv_w" else shard_shape[n] for n in sharded_names}
    pad_rows = lambda a: jnp.pad(a, ((0, CONV_WIRE_ROWS - CONV_W), (0, 0)))

    def weight_wires(groups, li):
        def wire(name):
            a = given[name][li]
            if name == "conv_w":
                hi = a.astype(BF16)
                return jnp.concatenate([pad_rows(hi), pad_rows((a - hi.astype(F32)).astype(BF16))], axis=0)
            return a.astype(BF16)

        return [_join([wire(n) for n in names], axis - 1) for names, axis in groups]

    def full_weights(groups, li, local, gathered):
        p = {n: given[n][li] for n in REPLICATED}
        p["pool_w"] = p["pool_w"].astype(BF16)
        for (names, axis), loc, got in zip(groups, local, gathered):
            got = lax.dynamic_update_slice(got, loc[None], (chip, 0, 0))
            per_chip = [_split_group(got[j], names, axis, weight_wire_shape) for j in range(N_CHIPS)]
            for n in names:
                parts = [pc[n] for pc in per_chip]
                if n == "conv_w":
                    parts = [q[:CONV_W].astype(F32) + q[CONV_WIRE_ROWS:CONV_WIRE_ROWS + CONV_W].astype(F32) for q in parts]
                p[n] = jnp.concatenate(parts, axis=SHARDED[n] - 1)
        return align_weights(p)

    def grad_wires(groups, g):
        wires = []
        for names, axis in groups:
            split = {n: jnp.split(pad_rows(g[n]) if n == "conv_w" else g[n], N_CHIPS, axis=SHARDED[n] - 1) for n in names}
            wires.append(jnp.stack([_join([split[n][j].astype(BF16) for n in names], axis - 1) for j in range(N_CHIPS)]))
        return wires

    def partial_sums(wires, theirs, tag):
        return [sibling_sum(w, t, place, f"sibling_sum_{tag}_{i}") for i, (w, t) in enumerate(zip(wires, theirs))]

    def chip_partials(groups, g, tag):
        wires = grad_wires(groups, g)
        return partial_sums(wires, run_exchange(SiblingSwap(wires), "sibling_swap_" + tag), tag)

    tabs = rope_tables(positions.reshape(s_len, 1), s_len)
    in_groups, rest_groups = mix_groups[:1], mix_groups[1:]
    heads = []
    for li in range(N_LAYERS):
        loc_rest = weight_wires(rest_groups, li)
        heads.append((GatherShards(loc_rest), functools.partial(full_weights, rest_groups, li, loc_rest)))
    loc_in0 = weight_wires(in_groups, 0)
    w_in0 = full_weights(in_groups, 0, loc_in0, run_exchange(GatherShards(loc_in0), "gather_w_in_l0"))
    loc_f0, loc_in1 = weight_wires(ffn_groups, 0), weight_wires(in_groups, 1)
    h, sv_m0, got = mixer_fwd(x[0], tabs, w_in0, "_l0", GatherShards(loc_f0 + loc_in1), heads[0])
    w_m0 = sv_m0["w"]
    w_f0 = full_weights(ffn_groups, 0, loc_f0, got[:len(loc_f0)])
    w_in1 = full_weights(in_groups, 1, loc_in1, got[len(loc_f0):])
    h, sv_f0, h_normed = ffn_fwd(h, sv_m0["hn"], w_f0, "_l0", w_in1["mix_norm_pre"])
    loc_f1 = weight_wires(ffn_groups, 1)
    h, sv_m1, got = mixer_fwd(h, tabs, w_in1, "_l1", GatherShards(loc_f1), heads[1], h=h_normed)
    w_m1 = sv_m1["w"]
    w_f1 = full_weights(ffn_groups, 1, loc_f1, got)
    h, sv_f1, _ = ffn_fwd(h, sv_m1["hn"], w_f1, "_l1")
    dh, loss_local = loss_head(h, loss_target[0])
    loss = lax.psum(loss_local[0, 0], MESH_AXES)

    last = {}

    def then_exchange(key, wires, tag, more=()):
        def make(theirs):
            last[key] = partial_sums(wires, theirs, tag)
            return ChipExchange(list(more) + last[key])
        return make

    dh, g_f1, post_m1, _ = ffn_bwd(dh, sv_f1, w_f1, "_l1", sv_m1["mo"])
    g_f1 = unalign_grads(g_f1)
    wires_f1 = grad_wires(ffn_groups, g_f1)
    dh, g_m1, o_f1, post_f0 = mixer_bwd(dh, sv_m1, tabs, w_m1, "_l1", post_m1, then_exchange("f1", wires_f1, "ffn_l1"),
                                        prev=(sv_f0["y"], w_f0["ffn_norm_post"]), first=SiblingSwap(wires_f1))
    p_f1 = last["f1"]
    g_m1 = unalign_grads(g_m1)
    wires_m1 = grad_wires(mix_groups, g_m1)
    dh, g_f0, post_m0, theirs_m1 = ffn_bwd(dh, sv_f0, w_f0, "_l0", sv_m0["mo"], post_f0, first=SiblingSwap(wires_m1))
    p_m1 = partial_sums(wires_m1, theirs_m1, "mix_l1")
    g_f0 = unalign_grads(g_f0)
    wires_f0 = grad_wires(ffn_groups, g_f0)

    def tail_rest(g):
        last["rest"] = chip_partials(rest_groups, unalign_grads({n: g[n] for n in MIX_MATRICES[1:]}), "rest_l0")
        return ChipExchange(last["rest"])

    def tail_in(g):
        last["in"] = chip_partials(in_groups, unalign_grads({"w_in": g["w_in"]}), "w_in_l0")
        return ChipExchange(last["in"])

    grad_x, g_m0, (got, o_m0), _ = mixer_bwd(dh, sv_m0, tabs, w_m0, "_l0", post_m0, then_exchange("f0", wires_f0, "ffn_l0", p_m1),
                                             (tail_rest, tail_in), first=SiblingSwap(wires_f0))
    p_f0 = last["f0"]
    o_m1, o_f0 = got[:len(p_m1)], got[len(p_m1):]
    p_m0 = last["in"] + last["rest"]
    grads = [{**g_m0, **g_f0}, {**g_m1, **g_f1}]
    grad_full = {n: jnp.stack([g[n].reshape(given[n].shape[1:]) for g in grads]) for n in REPLICATED}
    replicated = [_pack_rows([grad_full[n] for n in ROW_PARAMS]), grad_full["pool_w"].reshape(-1, POOL_GD)]
    device = 2 * chip + lax.axis_index("c")
    rows_all, pool_w_all = [lax.dynamic_update_slice(a, mine[None], (device, 0, 0))
                            for a, mine in zip(run_exchange(GatherAll(replicated), "gather_replicated"), replicated)]

    sums = {}
    for groups, base, per_layer in ((ffn_groups, N_MIX_GROUPS, ((1, p_f1, o_f1), (0, p_f0, o_f0))), (mix_groups, 0, ((1, p_m1, o_m1), (0, p_m0, o_m0)))):
        for li, parts, others in per_layer:
            for i, (p, o) in enumerate(zip(parts, others)):
                sums[base + i] = chip_sum(p, o, place, li, sums.get(base + i), f"chip_sum_{base + i}_l{li}")
    g_shard = {}
    for (names, axis), s in zip(GROUPS, sibling_gather([sums[i] for i in range(len(GROUPS))])):
        g_shard.update(_split_group(s, names, axis, grad_wire_shape))
    g_shard["conv_w"] = g_shard["conv_w"][:, :CONV_W]

    row_shapes = [given[n].shape for n in ROW_PARAMS]
    g_rows = sum_devices(rows_all, "row_params_sum")
    g_pool_w = sum_devices(pool_w_all, "pool_w_sum")
    g_rep = dict(zip(ROW_PARAMS, _unpack_rows(g_rows, row_shapes)))
    g_rep["pool_w"] = g_pool_w.reshape(given["pool_w"].shape)

    g_out, d_out, m_out, v_out = {}, {}, {}, {}
    for n in sharded_names + ["pool_w"]:
        shp = given[n].shape
        three_d = (shp[0], int(np.prod(shp[1:-1])), shp[-1])
        g_n = g_shard[n] if n in SHARDED else g_rep[n]
        view = (lambda a: jnp.swapaxes(a.reshape(three_d), 1, 2)) if n in ROWS_MINOR else (lambda a: a.reshape(three_d))
        back = (lambda a: jnp.swapaxes(a, 1, 2).reshape(shp)) if n in ROWS_MINOR else (lambda a: a.reshape(shp))
        d, mn, vn = adamw(view(given[n]), view(g_n), view(mom[n]), view(var[n]), "adamw_" + n)
        g_out[n], d_out[n], m_out[n], v_out[n] = g_n, back(d), back(mn), back(vn)
    rd, rm, rv = adamw(_pack_rows([given[n] for n in ROW_PARAMS])[None], g_rows[None], _pack_rows([mom[n] for n in ROW_PARAMS])[None],
                       _pack_rows([var[n] for n in ROW_PARAMS])[None], "adamw_row_params")
    for n, d, mn, vn in zip(ROW_PARAMS, *[_unpack_rows(a[0], row_shapes) for a in (rd, rm, rv)]):
        g_out[n], d_out[n], m_out[n], v_out[n] = g_rep[n], d, mn, vn

    return (loss, grad_x[None], *[g_out[n] for n in WEIGHTS], *[d_out[n] for n in WEIGHTS], *[m_out[n] for n in WEIGHTS],
            *[v_out[n] for n in WEIGHTS])
```
